```python
import math
import jax, jax.numpy as jnp
from jax import lax
import numpy as np

D_MODEL = 1024
BATCH = 8
SEQ = 8192
DEPTH = 2

SSM_WIDTH = 512
SSM_GROUP = 16
SSM_GROUPS = SSM_WIDTH // SSM_GROUP
SSM_STATE = 64
LOG_DT_MIN = math.log(1e-3)
LOG_DT_MAX = math.log(1e-1)
ATTN_HEADS = 8
HEAD_DIM = 64
ATTN_WIDTH = ATTN_HEADS * HEAD_DIM
Q_BLOCK = 128
D_FF = 2816
FFN_RES = 0.5
N_SUB = 3
RMS_EPS = 1e-6
IN_WIDTH = SSM_WIDTH + 3 * ATTN_WIDTH + ATTN_HEADS + 2 * D_MODEL

kernel_name = "hybrid_s5_fox_macaron_adaln"


def rms_norm(x, g):
    xf = x.astype(jnp.float32)
    xf = xf * lax.rsqrt(jnp.mean(xf * xf, axis=-1, keepdims=True) + RMS_EPS)
    return (xf * g.astype(jnp.float32)).astype(x.dtype)


def swiglu(h, w_in, w_out):
    gate, up = jnp.split(h @ w_in, 2, axis=-1)
    return (jax.nn.silu(gate) * up) @ w_out


def s5_ssm(u, a_re, a_im, log_dt, b_re, b_im, c_re, c_im, d_skip):
    f32 = jnp.float32
    bsz, L, _ = u.shape
    uf = u.astype(f32).reshape(bsz, L, SSM_GROUPS, SSM_GROUP)
    lam = lax.complex(jnp.minimum(a_re.astype(f32), -1e-4), a_im.astype(f32))
    dt = jnp.exp(log_dt.astype(f32))[:, None]
    lam_bar = jnp.exp(lam * dt)
    b = lax.complex(b_re.astype(f32), b_im.astype(f32))
    b_bar = ((lam_bar - 1.0) / lam)[..., None] * b
    bu = jnp.einsum('blgn,gpn->blgp', uf.astype(jnp.complex64), b_bar)
    a_all = jnp.broadcast_to(lam_bar, bu.shape)

    def combine(e1, e2):
        a1, s1 = e1
        a2, s2 = e2
        return a2 * a1, a2 * s1 + s2

    _, states = lax.associative_scan(combine, (a_all, bu), axis=1)
    c = lax.complex(c_re.astype(f32), c_im.astype(f32))
    y = jnp.real(jnp.einsum('gnp,blgp->blgn', c, states))
    y = y + d_skip.astype(f32).reshape(SSM_GROUPS, SSM_GROUP) * uf
    return y.reshape(bsz, L, SSM_WIDTH).astype(u.dtype)


def forgetting_attention(q, k, v, f_logit):
    f32 = jnp.float32
    bsz, L, H, Dh = q.shape
    nb = L // Q_BLOCK
    log_f = jax.nn.log_sigmoid(f_logit.astype(f32))
    cum = jnp.cumsum(log_f, axis=1).transpose(0, 2, 1)
    kt = k.transpose(0, 2, 1, 3)
    vt = v.transpose(0, 2, 1, 3)
    qb = q.transpose(0, 2, 1, 3).reshape(bsz, H, nb, Q_BLOCK, Dh).transpose(2, 0, 1, 3, 4)
    cqb = cum.reshape(bsz, H, nb, Q_BLOCK).transpose(2, 0, 1, 3)
    starts = jnp.arange(nb, dtype=jnp.int32) * Q_BLOCK
    k_pos = jnp.arange(L, dtype=jnp.int32)
    scale = Dh ** -0.5

    def one_block(args):
        q_blk, cq_blk, start = args
        s = jnp.einsum('bhqd,bhkd->bhqk', q_blk, kt).astype(f32) * scale
        s = s + (cq_blk[..., None] - cum[:, :, None, :])
        q_pos = start + jnp.arange(Q_BLOCK, dtype=jnp.int32)
        s = jnp.where(k_pos[None, :] <= q_pos[:, None], s, -jnp.inf)
        p = jax.nn.softmax(s, axis=-1).astype(vt.dtype)
        return jnp.einsum('bhqk,bhkd->bhqd', p, vt)

    out = lax.map(one_block, (qb, cqb, starts))
    return out.transpose(1, 0, 3, 2, 4).reshape(bsz, L, H * Dh)


def token_mixer(h, w_in, forget_b, a_re, a_im, log_dt, b_re, b_im, c_re, c_im,
                d_skip, glu_w, attn_w_out, w_out):
    bsz, L, _ = h.shape
    proj = h @ w_in
    cuts = np.cumsum([SSM_WIDTH, ATTN_WIDTH, ATTN_WIDTH, ATTN_WIDTH, ATTN_HEADS, D_MODEL]).tolist()
    u, q, k, v, f, g_a, g_b = jnp.split(proj, cuts, axis=-1)
    y_ssm = s5_ssm(u, a_re, a_im, log_dt, b_re, b_im, c_re, c_im, d_skip)
    z_val, z_gate = jnp.split(jax.nn.gelu(y_ssm) @ glu_w, 2, axis=-1)
    y_a = z_val * jax.nn.sigmoid(z_gate)
    shp = (bsz, L, ATTN_HEADS, HEAD_DIM)
    attn = forgetting_attention(q.reshape(shp), k.reshape(shp), v.reshape(shp), f + forget_b)
    y_b = attn @ attn_w_out
    merged = jax.nn.sigmoid(g_a) * y_a + jax.nn.sigmoid(g_b) * y_b
    return merged @ w_out


def _fwd_setup_inputs(seed: int = 0) -> dict:
    key = jax.random.key(seed)
    ks = jax.random.split(key, 24)
    f32 = jnp.float32
    nrm = lambda k, shape, s: jax.random.normal(k, shape, f32) * s
    D, G, P, N = D_MODEL, SSM_GROUPS, SSM_STATE, SSM_GROUP
    a_im_init = jnp.pi * jnp.arange(P, dtype=f32)
    return {
        "x": nrm(ks[0], (BATCH, SEQ, D), 1.0),
        "c": nrm(ks[1], (BATCH, D), 1.0),
        "mod_w": nrm(ks[2], (DEPTH, D, N_SUB * 3 * D), 0.5 * D ** -0.5),
        "mod_b": nrm(ks[3], (DEPTH, N_SUB * 3 * D), 0.01),
        "norm_pre": 1.0 + nrm(ks[4], (DEPTH, N_SUB, D), 0.02),
        "norm_post": 1.0 + nrm(ks[5], (DEPTH, N_SUB, D), 0.02),
        "ffn_w_in": nrm(ks[6], (DEPTH, 2, D, 2 * D_FF), D ** -0.5),
        "ffn_w_out": nrm(ks[7], (DEPTH, 2, D_FF, D), D_FF ** -0.5),
        "mix_w_in": nrm(ks[8], (DEPTH, D, IN_WIDTH), D ** -0.5),
        "forget_b": 3.0 + nrm(ks[9], (DEPTH, ATTN_HEADS), 0.5),
        "ssm_a_re": -0.5 + nrm(ks[10], (DEPTH, G, P), 0.01),
        "ssm_a_im": a_im_init + nrm(ks[11], (DEPTH, G, P), 0.01),
        "ssm_log_dt": jax.random.uniform(ks[12], (DEPTH, G), f32, LOG_DT_MIN, LOG_DT_MAX),
        "ssm_b_re": nrm(ks[13], (DEPTH, G, P, N), (2 * N) ** -0.5),
        "ssm_b_im": nrm(ks[14], (DEPTH, G, P, N), (2 * N) ** -0.5),
        "ssm_c_re": nrm(ks[15], (DEPTH, G, N, P), (2 * P) ** -0.5),
        "ssm_c_im": nrm(ks[16], (DEPTH, G, N, P), (2 * P) ** -0.5),
        "ssm_d": nrm(ks[17], (DEPTH, SSM_WIDTH), 1.0),
        "glu_w": nrm(ks[18], (DEPTH, SSM_WIDTH, 2 * D), SSM_WIDTH ** -0.5),
        "attn_w_out": nrm(ks[19], (DEPTH, ATTN_WIDTH, D), ATTN_WIDTH ** -0.5),
        "mix_w_out": nrm(ks[20], (DEPTH, D, D), D ** -0.5),
    }


def _fwd_reference(x, c, mod_w, mod_b, norm_pre, norm_post, ffn_w_in, ffn_w_out,
              mix_w_in, forget_b, ssm_a_re, ssm_a_im, ssm_log_dt, ssm_b_re,
              ssm_b_im, ssm_c_re, ssm_c_im, ssm_d, glu_w, attn_w_out, mix_w_out):
    bsz = x.shape[0]
    for l in range(DEPTH):
        mod = (jax.nn.silu(c) @ mod_w[l] + mod_b[l]).reshape(bsz, N_SUB, 3, D_MODEL)
        mod = mod[:, :, :, None, :]

        def pre(x_in, i):
            return rms_norm(x_in, norm_pre[l, i]) * (1.0 + mod[:, i, 1]) + mod[:, i, 0]

        def post_add(x_in, y, i, res_w):
            return x_in + res_w * mod[:, i, 2] * rms_norm(y, norm_post[l, i])

        x = post_add(x, swiglu(pre(x, 0), ffn_w_in[l, 0], ffn_w_out[l, 0]), 0, FFN_RES)
        y = token_mixer(pre(x, 1), mix_w_in[l], forget_b[l], ssm_a_re[l], ssm_a_im[l],
                        ssm_log_dt[l], ssm_b_re[l], ssm_b_im[l], ssm_c_re[l], ssm_c_im[l],
                        ssm_d[l], glu_w[l], attn_w_out[l], mix_w_out[l])
        x = post_add(x, y, 1, 1.0)
        x = post_add(x, swiglu(pre(x, 2), ffn_w_in[l, 1], ffn_w_out[l, 1]), 2, FFN_RES)
    return x


import jax as _jax
import jax.numpy as _jnp

TWIN_FORMAT = 'train_step'
FWD_PARAMS = ['x', 'c', 'mod_w', 'mod_b', 'norm_pre', 'norm_post', 'ffn_w_in', 'ffn_w_out', 'mix_w_in', 'forget_b', 'ssm_a_re', 'ssm_a_im', 'ssm_log_dt', 'ssm_b_re', 'ssm_b_im', 'ssm_c_re', 'ssm_c_im', 'ssm_d', 'glu_w', 'attn_w_out', 'mix_w_out']
TWIN_WEIGHTS = ['mod_w', 'mod_b', 'norm_pre', 'norm_post', 'ffn_w_in', 'ffn_w_out', 'mix_w_in', 'forget_b', 'ssm_a_re', 'ssm_a_im', 'ssm_log_dt', 'ssm_b_re', 'ssm_b_im', 'ssm_c_re', 'ssm_c_im', 'ssm_d', 'glu_w', 'attn_w_out', 'mix_w_out']
TWIN_DIFF_INPUT = 'x'
TWIN_INPUTS = ['x', 'c', 'mod_w', 'mod_b', 'norm_pre', 'norm_post', 'ffn_w_in', 'ffn_w_out', 'mix_w_in', 'forget_b', 'ssm_a_re', 'ssm_a_im', 'ssm_log_dt', 'ssm_b_re', 'ssm_b_im', 'ssm_c_re', 'ssm_c_im', 'ssm_d', 'glu_w', 'attn_w_out', 'mix_w_out', 'loss_target', 'm_mod_w', 'm_mod_b', 'm_norm_pre', 'm_norm_post', 'm_ffn_w_in', 'm_ffn_w_out', 'm_mix_w_in', 'm_forget_b', 'm_ssm_a_re', 'm_ssm_a_im', 'm_ssm_log_dt', 'm_ssm_b_re', 'm_ssm_b_im', 'm_ssm_c_re', 'm_ssm_c_im', 'm_ssm_d', 'm_glu_w', 'm_attn_w_out', 'm_mix_w_out', 'v_mod_w', 'v_mod_b', 'v_norm_pre', 'v_norm_post', 'v_ffn_w_in', 'v_ffn_w_out', 'v_mix_w_in', 'v_forget_b', 'v_ssm_a_re', 'v_ssm_a_im', 'v_ssm_log_dt', 'v_ssm_b_re', 'v_ssm_b_im', 'v_ssm_c_re', 'v_ssm_c_im', 'v_ssm_d', 'v_glu_w', 'v_attn_w_out', 'v_mix_w_out']
TWIN_OUTPUTS = ['loss', 'grad_x', 'grad_mod_w', 'grad_mod_b', 'grad_norm_pre', 'grad_norm_post', 'grad_ffn_w_in', 'grad_ffn_w_out', 'grad_mix_w_in', 'grad_forget_b', 'grad_ssm_a_re', 'grad_ssm_a_im', 'grad_ssm_log_dt', 'grad_ssm_b_re', 'grad_ssm_b_im', 'grad_ssm_c_re', 'grad_ssm_c_im', 'grad_ssm_d', 'grad_glu_w', 'grad_attn_w_out', 'grad_mix_w_out', 'delta_mod_w', 'delta_mod_b', 'delta_norm_pre', 'delta_norm_post', 'delta_ffn_w_in', 'delta_ffn_w_out', 'delta_mix_w_in', 'delta_forget_b', 'delta_ssm_a_re', 'delta_ssm_a_im', 'delta_ssm_log_dt', 'delta_ssm_b_re', 'delta_ssm_b_im', 'delta_ssm_c_re', 'delta_ssm_c_im', 'delta_ssm_d', 'delta_glu_w', 'delta_attn_w_out', 'delta_mix_w_out', 'new_m_mod_w', 'new_m_mod_b', 'new_m_norm_pre', 'new_m_norm_post', 'new_m_ffn_w_in', 'new_m_ffn_w_out', 'new_m_mix_w_in', 'new_m_forget_b', 'new_m_ssm_a_re', 'new_m_ssm_a_im', 'new_m_ssm_log_dt', 'new_m_ssm_b_re', 'new_m_ssm_b_im', 'new_m_ssm_c_re', 'new_m_ssm_c_im', 'new_m_ssm_d', 'new_m_glu_w', 'new_m_attn_w_out', 'new_m_mix_w_out', 'new_v_mod_w', 'new_v_mod_b', 'new_v_norm_pre', 'new_v_norm_post', 'new_v_ffn_w_in', 'new_v_ffn_w_out', 'new_v_mix_w_in', 'new_v_forget_b', 'new_v_ssm_a_re', 'new_v_ssm_a_im', 'new_v_ssm_log_dt', 'new_v_ssm_b_re', 'new_v_ssm_b_im', 'new_v_ssm_c_re', 'new_v_ssm_c_im', 'new_v_ssm_d', 'new_v_glu_w', 'new_v_attn_w_out', 'new_v_mix_w_out']
TWIN_LEAF_KINDS = {'loss': 'loss', 'grad_x': 'grad_x', 'grad_mod_w': 'grad_w', 'grad_mod_b': 'grad_w', 'grad_norm_pre': 'grad_w', 'grad_norm_post': 'grad_w', 'grad_ffn_w_in': 'grad_w', 'grad_ffn_w_out': 'grad_w', 'grad_mix_w_in': 'grad_w', 'grad_forget_b': 'grad_w', 'grad_ssm_a_re': 'grad_w', 'grad_ssm_a_im': 'grad_w', 'grad_ssm_log_dt': 'grad_w', 'grad_ssm_b_re': 'grad_w', 'grad_ssm_b_im': 'grad_w', 'grad_ssm_c_re': 'grad_w', 'grad_ssm_c_im': 'grad_w', 'grad_ssm_d': 'grad_w', 'grad_glu_w': 'grad_w', 'grad_attn_w_out': 'grad_w', 'grad_mix_w_out': 'grad_w', 'delta_mod_w': 'delta_w', 'delta_mod_b': 'delta_w', 'delta_norm_pre': 'delta_w', 'delta_norm_post': 'delta_w', 'delta_ffn_w_in': 'delta_w', 'delta_ffn_w_out': 'delta_w', 'delta_mix_w_in': 'delta_w', 'delta_forget_b': 'delta_w', 'delta_ssm_a_re': 'delta_w', 'delta_ssm_a_im': 'delta_w', 'delta_ssm_log_dt': 'delta_w', 'delta_ssm_b_re': 'delta_w', 'delta_ssm_b_im': 'delta_w', 'delta_ssm_c_re': 'delta_w', 'delta_ssm_c_im': 'delta_w', 'delta_ssm_d': 'delta_w', 'delta_glu_w': 'delta_w', 'delta_attn_w_out': 'delta_w', 'delta_mix_w_out': 'delta_w', 'new_m_mod_w': 'new_m', 'new_m_mod_b': 'new_m', 'new_m_norm_pre': 'new_m', 'new_m_norm_post': 'new_m', 'new_m_ffn_w_in': 'new_m', 'new_m_ffn_w_out': 'new_m', 'new_m_mix_w_in': 'new_m', 'new_m_forget_b': 'new_m', 'new_m_ssm_a_re': 'new_m', 'new_m_ssm_a_im': 'new_m', 'new_m_ssm_log_dt': 'new_m', 'new_m_ssm_b_re': 'new_m', 'new_m_ssm_b_im': 'new_m', 'new_m_ssm_c_re': 'new_m', 'new_m_ssm_c_im': 'new_m', 'new_m_ssm_d': 'new_m', 'new_m_glu_w': 'new_m', 'new_m_attn_w_out': 'new_m', 'new_m_mix_w_out': 'new_m', 'new_v_mod_w': 'new_v', 'new_v_mod_b': 'new_v', 'new_v_norm_pre': 'new_v', 'new_v_norm_post': 'new_v', 'new_v_ffn_w_in': 'new_v', 'new_v_ffn_w_out': 'new_v', 'new_v_mix_w_in': 'new_v', 'new_v_forget_b': 'new_v', 'new_v_ssm_a_re': 'new_v', 'new_v_ssm_a_im': 'new_v', 'new_v_ssm_log_dt': 'new_v', 'new_v_ssm_b_re': 'new_v', 'new_v_ssm_b_im': 'new_v', 'new_v_ssm_c_re': 'new_v', 'new_v_ssm_c_im': 'new_v', 'new_v_ssm_d': 'new_v', 'new_v_glu_w': 'new_v', 'new_v_attn_w_out': 'new_v', 'new_v_mix_w_out': 'new_v'}


def _forward(args):
    return _fwd_reference(*[args[k] for k in FWD_PARAMS])


def _output_shape():
    def fwd():
        inp = _fwd_setup_inputs(0)
        return _fwd_reference(*[inp[k] for k in FWD_PARAMS])
    out = _jax.eval_shape(fwd)
    return out.shape, out.dtype

N_MICROBATCH = 1
ADAM_LR = 0.001
ADAM_B1 = 0.9
ADAM_B2 = 0.999
ADAM_EPS = 1e-08
ADAM_WD = 0.01
ADAM_STEP = 10
PER_EXAMPLE_BATCH_AXIS = {'x': 0, 'c': 0, 'loss_target': 0}
SHARED_INPUTS = []
_WEIGHT_DTYPES = {'mod_w': _jnp.float32, 'mod_b': _jnp.float32, 'norm_pre': _jnp.float32, 'norm_post': _jnp.float32, 'ffn_w_in': _jnp.float32, 'ffn_w_out': _jnp.float32, 'mix_w_in': _jnp.float32, 'forget_b': _jnp.float32, 'ssm_a_re': _jnp.float32, 'ssm_a_im': _jnp.float32, 'ssm_log_dt': _jnp.float32, 'ssm_b_re': _jnp.float32, 'ssm_b_im': _jnp.float32, 'ssm_c_re': _jnp.float32, 'ssm_c_im': _jnp.float32, 'ssm_d': _jnp.float32, 'glu_w': _jnp.float32, 'attn_w_out': _jnp.float32, 'mix_w_out': _jnp.float32}
MOMENT_SCALE = {'mod_w': 1.884464e+00, 'mod_b': 3.718096e+00, 'norm_pre': 2.133605e-01, 'norm_post': 4.253548e+00, 'ffn_w_in': 6.279926e-02, 'ffn_w_out': 1.218948e-01, 'mix_w_in': 5.126410e-01, 'forget_b': 1.287084e+00, 'ssm_a_re': 3.674953e-02, 'ssm_a_im': 3.384959e-02, 'ssm_log_dt': 5.545699e+00, 'ssm_b_re': 3.122312e-02, 'ssm_b_im': 3.015803e-02, 'ssm_c_re': 6.120356e-02, 'ssm_c_im': 6.256095e-02, 'ssm_d': 9.976239e-01, 'glu_w': 4.818666e-01, 'attn_w_out': 1.117442e+00, 'mix_w_out': 1.300036e+00}


def _to_microbatches(a, axis):
    t = _jnp.moveaxis(a, axis, 0)
    t = t.reshape((N_MICROBATCH, t.shape[0] // N_MICROBATCH) + t.shape[1:])
    return _jnp.moveaxis(t, 1, axis + 1)


def setup_inputs(seed: int = 0) -> dict:
    inp = _fwd_setup_inputs(seed)
    key = _jax.random.fold_in(_jax.random.key(seed), 7919)
    shape, _ = _output_shape()
    out = dict(inp)
    out["loss_target"] = _jax.random.normal(_jax.random.fold_in(key, 0), shape, _jnp.float32)
    for i, name in enumerate(TWIN_WEIGHTS):
        w = inp[name].astype(_jnp.float32)
        if MOMENT_SCALE is None:
            s = _jnp.sqrt(_jnp.mean(_jnp.square(w)) + 1e-30)
        else:
            s = MOMENT_SCALE[name]
        km, kv = _jax.random.split(_jax.random.fold_in(key, i + 1))
        out[name] = w
        out["m_" + name] = s * _jax.random.normal(km, w.shape, _jnp.float32)
        out["v_" + name] = (s * s) * _jax.random.uniform(kv, w.shape, _jnp.float32, 0.5, 1.5)
    if N_MICROBATCH > 1:
        for name, axis in PER_EXAMPLE_BATCH_AXIS.items():
            out[name] = _to_microbatches(out[name], axis)
    return {'x': out['x'], 'c': out['c'], 'mod_w': out['mod_w'], 'mod_b': out['mod_b'], 'norm_pre': out['norm_pre'], 'norm_post': out['norm_post'], 'ffn_w_in': out['ffn_w_in'], 'ffn_w_out': out['ffn_w_out'], 'mix_w_in': out['mix_w_in'], 'forget_b': out['forget_b'], 'ssm_a_re': out['ssm_a_re'], 'ssm_a_im': out['ssm_a_im'], 'ssm_log_dt': out['ssm_log_dt'], 'ssm_b_re': out['ssm_b_re'], 'ssm_b_im': out['ssm_b_im'], 'ssm_c_re': out['ssm_c_re'], 'ssm_c_im': out['ssm_c_im'], 'ssm_d': out['ssm_d'], 'glu_w': out['glu_w'], 'attn_w_out': out['attn_w_out'], 'mix_w_out': out['mix_w_out'], 'loss_target': out['loss_target'], 'm_mod_w': out['m_mod_w'], 'm_mod_b': out['m_mod_b'], 'm_norm_pre': out['m_norm_pre'], 'm_norm_post': out['m_norm_post'], 'm_ffn_w_in': out['m_ffn_w_in'], 'm_ffn_w_out': out['m_ffn_w_out'], 'm_mix_w_in': out['m_mix_w_in'], 'm_forget_b': out['m_forget_b'], 'm_ssm_a_re': out['m_ssm_a_re'], 'm_ssm_a_im': out['m_ssm_a_im'], 'm_ssm_log_dt': out['m_ssm_log_dt'], 'm_ssm_b_re': out['m_ssm_b_re'], 'm_ssm_b_im': out['m_ssm_b_im'], 'm_ssm_c_re': out['m_ssm_c_re'], 'm_ssm_c_im': out['m_ssm_c_im'], 'm_ssm_d': out['m_ssm_d'], 'm_glu_w': out['m_glu_w'], 'm_attn_w_out': out['m_attn_w_out'], 'm_mix_w_out': out['m_mix_w_out'], 'v_mod_w': out['v_mod_w'], 'v_mod_b': out['v_mod_b'], 'v_norm_pre': out['v_norm_pre'], 'v_norm_post': out['v_norm_post'], 'v_ffn_w_in': out['v_ffn_w_in'], 'v_ffn_w_out': out['v_ffn_w_out'], 'v_mix_w_in': out['v_mix_w_in'], 'v_forget_b': out['v_forget_b'], 'v_ssm_a_re': out['v_ssm_a_re'], 'v_ssm_a_im': out['v_ssm_a_im'], 'v_ssm_log_dt': out['v_ssm_log_dt'], 'v_ssm_b_re': out['v_ssm_b_re'], 'v_ssm_b_im': out['v_ssm_b_im'], 'v_ssm_c_re': out['v_ssm_c_re'], 'v_ssm_c_im': out['v_ssm_c_im'], 'v_ssm_d': out['v_ssm_d'], 'v_glu_w': out['v_glu_w'], 'v_attn_w_out': out['v_attn_w_out'], 'v_mix_w_out': out['v_mix_w_out']}


def _loss(weights, diff, rest, loss_target):
    with _jax.named_scope("forward"):
        args = {**rest, TWIN_DIFF_INPUT: diff, **{k: w.astype(_WEIGHT_DTYPES[k]) for k, w in weights.items()}}
        y = _forward(args)
    with _jax.named_scope("loss_head"):
        err = _jnp.square(y.astype(_jnp.float32) - loss_target)
        return 0.5 * _jnp.sum(_jnp.mean(err, axis=-1)) if err.ndim else 0.5 * err


def _adamw(w, g, m, v):
    m = ADAM_B1 * m + (1.0 - ADAM_B1) * g
    v = ADAM_B2 * v + (1.0 - ADAM_B2) * _jnp.square(g)
    m_hat = m / (1.0 - ADAM_B1 ** ADAM_STEP)
    v_hat = v / (1.0 - ADAM_B2 ** ADAM_STEP)
    delta = -ADAM_LR * (m_hat / (_jnp.sqrt(v_hat) + ADAM_EPS) + ADAM_WD * w)
    return delta, m, v


def reference(x, c, mod_w, mod_b, norm_pre, norm_post, ffn_w_in, ffn_w_out, mix_w_in, forget_b, ssm_a_re, ssm_a_im, ssm_log_dt, ssm_b_re, ssm_b_im, ssm_c_re, ssm_c_im, ssm_d, glu_w, attn_w_out, mix_w_out, loss_target, m_mod_w, m_mod_b, m_norm_pre, m_norm_post, m_ffn_w_in, m_ffn_w_out, m_mix_w_in, m_forget_b, m_ssm_a_re, m_ssm_a_im, m_ssm_log_dt, m_ssm_b_re, m_ssm_b_im, m_ssm_c_re, m_ssm_c_im, m_ssm_d, m_glu_w, m_attn_w_out, m_mix_w_out, v_mod_w, v_mod_b, v_norm_pre, v_norm_post, v_ffn_w_in, v_ffn_w_out, v_mix_w_in, v_forget_b, v_ssm_a_re, v_ssm_a_im, v_ssm_log_dt, v_ssm_b_re, v_ssm_b_im, v_ssm_c_re, v_ssm_c_im, v_ssm_d, v_glu_w, v_attn_w_out, v_mix_w_out):
    given = dict(x=x, c=c, mod_w=mod_w, mod_b=mod_b, norm_pre=norm_pre, norm_post=norm_post, ffn_w_in=ffn_w_in, ffn_w_out=ffn_w_out, mix_w_in=mix_w_in, forget_b=forget_b, ssm_a_re=ssm_a_re, ssm_a_im=ssm_a_im, ssm_log_dt=ssm_log_dt, ssm_b_re=ssm_b_re, ssm_b_im=ssm_b_im, ssm_c_re=ssm_c_re, ssm_c_im=ssm_c_im, ssm_d=ssm_d, glu_w=glu_w, attn_w_out=attn_w_out, mix_w_out=mix_w_out, loss_target=loss_target, m_mod_w=m_mod_w, m_mod_b=m_mod_b, m_norm_pre=m_norm_pre, m_norm_post=m_norm_post, m_ffn_w_in=m_ffn_w_in, m_ffn_w_out=m_ffn_w_out, m_mix_w_in=m_mix_w_in, m_forget_b=m_forget_b, m_ssm_a_re=m_ssm_a_re, m_ssm_a_im=m_ssm_a_im, m_ssm_log_dt=m_ssm_log_dt, m_ssm_b_re=m_ssm_b_re, m_ssm_b_im=m_ssm_b_im, m_ssm_c_re=m_ssm_c_re, m_ssm_c_im=m_ssm_c_im, m_ssm_d=m_ssm_d, m_glu_w=m_glu_w, m_attn_w_out=m_attn_w_out, m_mix_w_out=m_mix_w_out, v_mod_w=v_mod_w, v_mod_b=v_mod_b, v_norm_pre=v_norm_pre, v_norm_post=v_norm_post, v_ffn_w_in=v_ffn_w_in, v_ffn_w_out=v_ffn_w_out, v_mix_w_in=v_mix_w_in, v_forget_b=v_forget_b, v_ssm_a_re=v_ssm_a_re, v_ssm_a_im=v_ssm_a_im, v_ssm_log_dt=v_ssm_log_dt, v_ssm_b_re=v_ssm_b_re, v_ssm_b_im=v_ssm_b_im, v_ssm_c_re=v_ssm_c_re, v_ssm_c_im=v_ssm_c_im, v_ssm_d=v_ssm_d, v_glu_w=v_glu_w, v_attn_w_out=v_attn_w_out, v_mix_w_out=v_mix_w_out)
    weights = {n: given[n] for n in TWIN_WEIGHTS}
    shared = {n: given[n] for n in SHARED_INPUTS}
    per_example = {n: given[n] for n in ['x', 'c']}
    grad_fn = _jax.value_and_grad(_loss, argnums=(0, 1))

    def one_microbatch(ex, loss_target):
        ex = dict(ex)
        diff = ex.pop(TWIN_DIFF_INPUT)
        return grad_fn(weights, diff, {**shared, **ex}, loss_target)

    if N_MICROBATCH == 1:
        loss, (grad_w, grad_x) = one_microbatch(per_example, given["loss_target"])
    else:
        def body(carry, xs):
            loss_sum, grad_sum = carry
            l_k, (gw_k, gx_k) = one_microbatch(xs[0], xs[1])
            with _jax.named_scope("update"):
                return (loss_sum + l_k, _jax.tree.map(_jnp.add, grad_sum, gw_k)), gx_k

        init = (_jnp.zeros((), _jnp.float32), _jax.tree.map(_jnp.zeros_like, weights))
        (loss, grad_w), grad_x = _jax.lax.scan(body, init, (per_example, given["loss_target"]))
    with _jax.named_scope("update"):
        delta_w, new_m, new_v = {}, {}, {}
        for n in TWIN_WEIGHTS:
            delta_w[n], new_m[n], new_v[n] = _adamw(weights[n], grad_w[n], given["m_" + n], given["v_" + n])
    return (loss, grad_x, *[grad_w[n] for n in TWIN_WEIGHTS], *[delta_w[n] for n in TWIN_WEIGHTS],
            *[new_m[n] for n in TWIN_WEIGHTS], *[new_v[n] for n in TWIN_WEIGHTS])
```

```python
import functools
import math

import jax
import jax.numpy as jnp
from jax import lax
from jax.experimental import pallas as pl
from jax.experimental.pallas import tpu as pltpu

F32 = jnp.float32
BF16 = jnp.bfloat16
MESH = pl.DeviceIdType.MESH
N_DEV = 8
LANES = 128
SUBLANES = 8
VMEM_LIMIT = 48 * 1024 * 1024

RMS_EPS = 1e-6
FFN_RES = 0.5
ADAM_LR = 0.001
ADAM_B1 = 0.9
ADAM_B2 = 0.999
ADAM_EPS = 1e-08
ADAM_WD = 0.01
ADAM_STEP = 10
GELU_C = math.sqrt(2.0 / math.pi)
GELU_A = 0.044715


def _pick(dim, target, mult=LANES):
    t = (min(dim, target) // mult) * mult
    while t >= mult:
        if dim % t == 0:
            return t
        t -= mult
    return dim


def _params(sem):
    return pltpu.CompilerParams(dimension_semantics=sem, vmem_limit_bytes=VMEM_LIMIT)


def _sigmoid(x):
    return 1.0 / (1.0 + jnp.exp(-x))


def mm(a, b, *, name, trans_a=False, trans_b=False, out_dtype=F32, tm=1024, tn=512, tk=1024):
    if trans_a:
        kdim, m = a.shape
    else:
        m, kdim = a.shape
    if trans_b:
        n, kb = b.shape
    else:
        kb, n = b.shape
    assert kdim == kb, (a.shape, b.shape)
    tm, tn, tk = _pick(m, tm), _pick(n, tn), _pick(kdim, tk)
    nk = kdim // tk
    dims = (((0 if trans_a else 1,), (1 if trans_b else 0,)), ((), ()))

    def body(a_ref, b_ref, o_ref, acc_ref):
        k = pl.program_id(2)

        @pl.when(k == 0)
        def _():
            acc_ref[...] = jnp.zeros_like(acc_ref)

        acc_ref[...] += lax.dot_general(a_ref[...].astype(BF16), b_ref[...].astype(BF16), dims,
                                        preferred_element_type=F32)

        @pl.when(k == nk - 1)
        def _():
            o_ref[...] = acc_ref[...].astype(out_dtype)

    a_spec = (pl.BlockSpec((tk, tm), lambda i, j, k: (k, i)) if trans_a
              else pl.BlockSpec((tm, tk), lambda i, j, k: (i, k)))
    b_spec = (pl.BlockSpec((tn, tk), lambda i, j, k: (j, k)) if trans_b
              else pl.BlockSpec((tk, tn), lambda i, j, k: (k, j)))
    return pl.pallas_call(
        body, name=name, grid=(m // tm, n // tn, nk),
        in_specs=[a_spec, b_spec],
        out_specs=pl.BlockSpec((tm, tn), lambda i, j, k: (i, j)),
        out_shape=jax.ShapeDtypeStruct((m, n), out_dtype),
        scratch_shapes=[pltpu.VMEM((tm, tn), F32)],
        compiler_params=_params(("parallel", "parallel", "arbitrary")),
    )(a, b)


def mm_blockdiag(a, b, *, name, a_cb0=0, out_dtype=F32, tm=512):
    m = a.shape[0]
    g_n, ka, nb = b.shape
    tm = _pick(m, tm)

    def body(a_ref, b_ref, o_ref):
        o_ref[...] = jnp.dot(a_ref[...].astype(BF16), b_ref[...].astype(BF16),
                             preferred_element_type=F32).astype(out_dtype)

    return pl.pallas_call(
        body, name=name, grid=(m // tm, g_n),
        in_specs=[pl.BlockSpec((tm, ka), lambda i, g: (i, a_cb0 + g)),
                  pl.BlockSpec((None, ka, nb), lambda i, g: (g, 0, 0))],
        out_specs=pl.BlockSpec((tm, nb), lambda i, g: (i, g)),
        out_shape=jax.ShapeDtypeStruct((m, g_n * nb), out_dtype),
        compiler_params=_params(("parallel", "parallel")),
    )(a, b)


def mm_blockdiag_tn(a, b, *, name, g_n, ka, kb, a_cb0=0, b_cb0=0, tk=512):
    rows = a.shape[0]
    tk = _pick(rows, tk)
    nk = rows // tk

    def body(a_ref, b_ref, o_ref):
        k = pl.program_id(1)

        @pl.when(k == 0)
        def _():
            o_ref[...] = jnp.zeros_like(o_ref)

        o_ref[...] += lax.dot_general(a_ref[...].astype(BF16), b_ref[...].astype(BF16),
                                      (((0,), (0,)), ((), ())), preferred_element_type=F32)

    return pl.pallas_call(
        body, name=name, grid=(g_n, nk),
        in_specs=[pl.BlockSpec((tk, ka), lambda g, k: (k, a_cb0 + g)),
                  pl.BlockSpec((tk, kb), lambda g, k: (k, b_cb0 + g))],
        out_specs=pl.BlockSpec((None, ka, kb), lambda g, k: (g, 0, 0)),
        out_shape=jax.ShapeDtypeStruct((g_n, ka, kb), F32),
        compiler_params=_params(("parallel", "arbitrary")),
    )(a, b)


def rowwise(fn, rows, vecs, outs, reds=(), *, name, tm=256):
    metas = []
    for r in rows:
        if isinstance(r, tuple) and len(r) == 3:
            metas.append(("col", r[0], r[1], r[2]))
        elif isinstance(r, tuple):
            metas.append(("lead", r[0], r[0].shape[2], r[1]))
        else:
            metas.append(("full", r, r.shape[1], 0))
    n_rows = metas[0][1].shape[1] if metas[0][0] == "lead" else metas[0][1].shape[0]
    rc = 16 if n_rows % 16 == 0 else (SUBLANES if n_rows % SUBLANES == 0 else n_rows)
    tm = _pick(n_rows, tm, rc)
    n_inner = tm // rc
    nr, nv, no = len(metas), len(vecs), len(outs)

    def body(*refs):
        row_refs, vec_refs = refs[:nr], refs[nr:nr + nv]
        out_refs, red_refs = refs[nr + nv:nr + nv + no], refs[nr + nv + no:]
        if reds:
            @pl.when(pl.program_id(0) == 0)
            def _():
                for rr in red_refs:
                    rr[...] = jnp.zeros_like(rr)
        vec_vals = [v[...] for v in vec_refs]

        def step(s, carry):
            r0 = pl.multiple_of(s * rc, rc)
            vals = [ref[pl.ds(r0, rc), :] for ref in row_refs]
            res = fn(*vals, *vec_vals)
            if not isinstance(res, (tuple, list)):
                res = (res,)
            for o_ref, (widths, dt), val in zip(out_refs, outs, res[:no]):
                pieces = val if isinstance(val, (tuple, list)) else (val,)
                off = 0
                for w_, piece in zip(widths, pieces):
                    o_ref[pl.ds(r0, rc), off:off + w_] = piece.astype(dt)
                    off += w_
            for rr, val in zip(red_refs, res[no:]):
                rr[...] += val
            return carry

        lax.fori_loop(0, n_inner, step, 0)

    in_specs = []
    for kind, arr, w_, idx in metas:
        if kind == "col":
            in_specs.append(pl.BlockSpec((tm, w_), functools.partial(lambda i, cb: (i, cb), cb=idx)))
        elif kind == "lead":
            in_specs.append(pl.BlockSpec((None, tm, w_), functools.partial(lambda i, p: (p, i, 0), p=idx)))
        else:
            in_specs.append(pl.BlockSpec((tm, w_), lambda i: (i, 0)))
    for v in vecs:
        in_specs.append(pl.BlockSpec(v.shape, lambda i: (0, 0)))
    out_specs = [pl.BlockSpec((tm, sum(ws)), lambda i: (i, 0)) for ws, _ in outs]
    out_specs += [pl.BlockSpec((1, w_), lambda i: (0, 0)) for w_ in reds]
    out_shape = [jax.ShapeDtypeStruct((n_rows, sum(ws)), dt) for ws, dt in outs]
    out_shape += [jax.ShapeDtypeStruct((1, w_), F32) for w_ in reds]
    res = pl.pallas_call(
        body, name=name, grid=(n_rows // tm,),
        in_specs=in_specs, out_specs=out_specs, out_shape=out_shape,
        compiler_params=_params(("arbitrary",)),
    )(*[m[1] for m in metas], *vecs)
    return res


def _rms(x):
    return lax.rsqrt(jnp.mean(x * x, axis=-1, keepdims=True) + RMS_EPS)


def _colsum(x):
    return jnp.sum(x, axis=0, keepdims=True)


def _f_silu(c):
    return c * _sigmoid(c)


def _f_pre(x, a, sh):
    return (x * _rms(x)) * a + sh


def _f_post_add(x, y, bv):
    return x + (y * _rms(y)) * bv


def _f_post_bwd(dxo, y, bv):
    ry = _rms(y)
    yn = y * ry
    dyn = dxo * bv
    dy = ry * (dyn - yn * jnp.mean(dyn * yn, axis=-1, keepdims=True))
    return dy, _colsum(dxo * yn)


def _f_pre_bwd(dxo, dh, x, a):
    r = _rms(x)
    xn = x * r
    dxn = dh * a
    dx = dxo + r * (dxn - xn * jnp.mean(dxn * xn, axis=-1, keepdims=True))
    return dx, _colsum(dh * xn), _colsum(dh)


def _f_swiglu(g, u):
    g = g.astype(F32)
    return (g * _sigmoid(g)) * u.astype(F32)


def _f_swiglu_bwd(g, u, dm):
    g, u, dm = g.astype(F32), u.astype(F32), dm.astype(F32)
    sg = _sigmoid(g)
    dg = dm * u * (sg * (1.0 + g * (1.0 - sg)))
    du = dm * (g * sg)
    return ((dg, du),)


def _gelu_t(x):
    return jnp.tanh(GELU_C * (x + GELU_A * x * x * x))


def _f_gelu_in(y0, u, dvec):
    y = y0 + dvec * u
    return 0.5 * y * (1.0 + _gelu_t(y)), y


def _f_gelu_bwd(dge, y, u):
    t = _gelu_t(y)
    dy = dge * (0.5 * (1.0 + t) + 0.5 * y * (1.0 - t * t) * GELU_C * (1.0 + 3.0 * GELU_A * y * y))
    return dy, _colsum(dy * u)


def _f_du_fin(du0, dys, dvec):
    return du0 + dvec * dys.astype(F32)


def _f_merge(zv, zg, yb, ga, gb):
    return _sigmoid(ga) * (zv * _sigmoid(zg)) + _sigmoid(gb) * yb


def _f_merge_bwd(dmg, zv, zg, yb, ga, gb):
    sa, sb, sz = _sigmoid(ga), _sigmoid(gb), _sigmoid(zg)
    ya = zv * sz
    dya = dmg * sa
    dga = dmg * ya * sa * (1.0 - sa)
    dyb = dmg * sb
    dgb = dmg * yb * sb * (1.0 - sb)
    dzv = dya * sz
    dzg = dya * zv * sz * (1.0 - sz)
    return (dzv, dzg), dyb, (dga, dgb)


def _f_sum_parts(*parts):
    acc = parts[0].astype(F32)
    for p in parts[1:]:
        acc = acc + p.astype(F32)
    return acc


def _f_adamw(*args):
    parts, (w, m, v) = args[:-3], args[-3:]
    g = _f_sum_parts(*parts)
    m = ADAM_B1 * m + (1.0 - ADAM_B1) * g
    v = ADAM_B2 * v + (1.0 - ADAM_B2) * (g * g)
    m_hat = m / (1.0 - ADAM_B1 ** ADAM_STEP)
    v_hat = v / (1.0 - ADAM_B2 ** ADAM_STEP)
    delta = -ADAM_LR * (m_hat / (jnp.sqrt(v_hat) + ADAM_EPS) + ADAM_WD * w)
    return g, delta, m, v


def adamw(parts3, w, m, v, *, name):
    c = w.shape[1]
    rows = [(parts3, p) for p in range(parts3.shape[0])] + [w, m, v]
    return rowwise(_f_adamw, rows, [], [((c,), F32)] * 4, name=name)


def _tri_dot(tri, x):
    x1 = x.astype(BF16)
    r1 = x - x1.astype(F32)
    x2 = r1.astype(BF16)
    x3 = (r1 - x2.astype(F32)).astype(BF16)
    dot = functools.partial(jnp.dot, preferred_element_type=F32)
    return dot(tri, x1) + dot(tri, x2) + dot(tri, x3)


def cum_fwd(proj, fb, *, col_block, name, t=256):
    n = proj.shape[0]
    t = _pick(n, t, SUBLANES)

    def body(f_ref, fb_ref, cum_ref, car_ref):
        @pl.when(pl.program_id(0) == 0)
        def _():
            car_ref[...] = jnp.zeros_like(car_ref)

        x = f_ref[...] + fb_ref[...]
        lf = jnp.minimum(x, 0.0) - jnp.log(1.0 + jnp.exp(-jnp.abs(x)))
        r = lax.broadcasted_iota(jnp.int32, (t, t), 0)
        c = lax.broadcasted_iota(jnp.int32, (t, t), 1)
        cs = _tri_dot((c <= r).astype(BF16), lf) + car_ref[0:1, :]
        cum_ref[...] = cs
        car_ref[0:1, :] = cs[t - 1:t, :]

    return pl.pallas_call(
        body, name=name, grid=(n // t,),
        in_specs=[pl.BlockSpec((t, LANES), lambda i: (i, col_block)),
                  pl.BlockSpec((1, LANES), lambda i: (0, 0))],
        out_specs=pl.BlockSpec((t, LANES), lambda i: (i, 0)),
        out_shape=jax.ShapeDtypeStruct((n, LANES), F32),
        scratch_shapes=[pltpu.VMEM((SUBLANES, LANES), F32)],
        compiler_params=_params(("arbitrary",)),
    )(proj, fb)


def cum_bwd(dcum, proj, fb, *, col_block, name, t=256):
    n = proj.shape[0]
    t = _pick(n, t, SUBLANES)
    nb = n // t

    def body(dc_ref, f_ref, fb_ref, df_ref, dfb_ref, car_ref):
        @pl.when(pl.program_id(0) == 0)
        def _():
            car_ref[...] = jnp.zeros_like(car_ref)
            dfb_ref[...] = jnp.zeros_like(dfb_ref)

        r = lax.broadcasted_iota(jnp.int32, (t, t), 0)
        c = lax.broadcasted_iota(jnp.int32, (t, t), 1)
        dl = _tri_dot((c >= r).astype(BF16), dc_ref[...]) + car_ref[0:1, :]
        car_ref[0:1, :] = dl[0:1, :]
        x = f_ref[...] + fb_ref[...]
        df = dl * (1.0 / (1.0 + jnp.exp(x)))
        df_ref[...] = df
        dfb_ref[...] += _colsum(df)

    return pl.pallas_call(
        body, name=name, grid=(nb,),
        in_specs=[pl.BlockSpec((t, LANES), lambda i: (nb - 1 - i, 0)),
                  pl.BlockSpec((t, LANES), lambda i: (nb - 1 - i, col_block)),
                  pl.BlockSpec((1, LANES), lambda i: (0, 0))],
        out_specs=[pl.BlockSpec((t, LANES), lambda i: (nb - 1 - i, 0)),
                   pl.BlockSpec((1, LANES), lambda i: (0, 0))],
        out_shape=[jax.ShapeDtypeStruct((n, LANES), F32), jax.ShapeDtypeStruct((1, LANES), F32)],
        scratch_shapes=[pltpu.VMEM((SUBLANES, LANES), F32)],
        compiler_params=_params(("arbitrary",)),
    )(dcum, proj, fb)


def _attn_logits(q, k, cc, cr, i, j, t, scale):
    s = lax.dot_general(q, k, (((1,), (1,)), ((), ())), preferred_element_type=F32) * scale
    s = s + (cc - cr)
    rows = i * t + lax.broadcasted_iota(jnp.int32, (t, t), 0)
    cols = j * t + lax.broadcasted_iota(jnp.int32, (t, t), 1)
    return jnp.where(cols <= rows, s, -jnp.inf)


def flash_fwd(q, k, v, ccol, crow, *, scale, name, t=512):
    h_n, n, dh = q.shape
    t = _pick(n, t)
    nb = n // t

    def body(q_ref, k_ref, v_ref, cc_ref, cr_ref, o_ref, lse_ref, m_sc, l_sc, acc_sc):
        i, j = pl.program_id(1), pl.program_id(2)

        @pl.when(j == 0)
        def _():
            m_sc[...] = jnp.full_like(m_sc, -jnp.inf)
            l_sc[...] = jnp.zeros_like(l_sc)
            acc_sc[...] = jnp.zeros_like(acc_sc)

        @pl.when(j <= i)
        def _():
            s = _attn_logits(q_ref[...], k_ref[...], cc_ref[...], cr_ref[...], i, j, t, scale)
            m_prev = m_sc[...]
            m_new = jnp.maximum(m_prev, jnp.max(s, axis=-1, keepdims=True))
            p = jnp.exp(s - m_new)
            alpha = jnp.exp(m_prev - m_new)
            l_sc[...] = alpha * l_sc[...] + jnp.sum(p, axis=-1, keepdims=True)
            acc_sc[...] = alpha * acc_sc[...] + jnp.dot(p.astype(BF16), v_ref[...],
                                                        preferred_element_type=F32)
            m_sc[...] = m_new

        @pl.when(j == nb - 1)
        def _():
            o_ref[...] = acc_sc[...] / l_sc[...]
            lse_ref[...] = m_sc[...] + jnp.log(l_sc[...])

    qspec = pl.BlockSpec((None, t, dh), lambda h, i, j: (h, i, 0))
    kspec = pl.BlockSpec((None, t, dh), lambda h, i, j: (h, jnp.minimum(j, i), 0))
    colspec = pl.BlockSpec((None, t, 1), lambda h, i, j: (h, i, 0))
    return pl.pallas_call(
        body, name=name, grid=(h_n, nb, nb),
        in_specs=[qspec, kspec, kspec, colspec,
                  pl.BlockSpec((None, 1, t), lambda h, i, j: (h, 0, jnp.minimum(j, i)))],
        out_specs=[qspec, colspec],
        out_shape=[jax.ShapeDtypeStruct((h_n, n, dh), F32), jax.ShapeDtypeStruct((h_n, n, 1), F32)],
        scratch_shapes=[pltpu.VMEM((t, 1), F32), pltpu.VMEM((t, 1), F32), pltpu.VMEM((t, dh), F32)],
        compiler_params=_params(("parallel", "parallel", "arbitrary")),
    )(q, k, v, ccol, crow)


def flash_bwd(q, k, v, do, lse, delta, ccol, crow, *, scale, name, t=512):
    h_n, n, dh = q.shape
    t = _pick(n, t)
    nb = n // t
    tn_dims = (((0,), (0,)), ((), ()))

    def body(q_ref, k_ref, v_ref, do_ref, lse_ref, dl_ref, cc_ref, cr_ref,
             dq_ref, dk_ref, dv_ref, dcr_ref, dk_acc, dv_acc, dcr_acc):
        j, i = pl.program_id(1), pl.program_id(2)

        @pl.when((j == 0) & (i == 0))
        def _():
            dq_ref[...] = jnp.zeros_like(dq_ref)

        @pl.when(i == 0)
        def _():
            dk_acc[...] = jnp.zeros_like(dk_acc)
            dv_acc[...] = jnp.zeros_like(dv_acc)
            dcr_acc[...] = jnp.zeros_like(dcr_acc)

        @pl.when(i >= j)
        def _():
            qv, kv, vv, dov = q_ref[...], k_ref[...], v_ref[...], do_ref[...]
            s = _attn_logits(qv, kv, cc_ref[...], cr_ref[...], i, j, t, scale)
            p = jnp.exp(s - lse_ref[...])
            dv_acc[...] += lax.dot_general(p.astype(BF16), dov, tn_dims, preferred_element_type=F32)
            dp = lax.dot_general(dov, vv, (((1,), (1,)), ((), ())), preferred_element_type=F32)
            ds = p * (dp - dl_ref[...])
            dcr_acc[...] -= _colsum(ds)
            dsb = ds.astype(BF16)
            dk_acc[...] += lax.dot_general(dsb, qv, tn_dims, preferred_element_type=F32) * scale
            r0 = pl.multiple_of(i * t, t)
            dq_ref[pl.ds(r0, t), :] += jnp.dot(dsb, kv, preferred_element_type=F32) * scale

        @pl.when(i == nb - 1)
        def _():
            dk_ref[...] = dk_acc[...]
            dv_ref[...] = dv_acc[...]
            dcr_ref[...] = dcr_acc[...]

    qspec = pl.BlockSpec((None, t, dh), lambda h, j, i: (h, jnp.maximum(i, j), 0))
    colspec = pl.BlockSpec((None, t, 1), lambda h, j, i: (h, jnp.maximum(i, j), 0))
    kspec = pl.BlockSpec((None, t, dh), lambda h, j, i: (h, j, 0))
    rowspec = pl.BlockSpec((None, 1, t), lambda h, j, i: (h, 0, j))
    return pl.pallas_call(
        body, name=name, grid=(h_n, nb, nb),
        in_specs=[qspec, kspec, kspec, qspec, colspec, colspec, colspec, rowspec],
        out_specs=[pl.BlockSpec((None, n, dh), lambda h, j, i: (h, 0, 0)), kspec, kspec, rowspec],
        out_shape=[jax.ShapeDtypeStruct((h_n, n, dh), F32)] * 3 + [jax.ShapeDtypeStruct((h_n, 1, n), F32)],
        scratch_shapes=[pltpu.VMEM((t, dh), F32), pltpu.VMEM((t, dh), F32), pltpu.VMEM((1, t), F32)],
        compiler_params=_params(("parallel", "arbitrary", "arbitrary")),
    )(q, k, v, do, lse, delta, ccol, crow)


def flash_delta(q, k, v, do, lse, ccol, crow, *, scale, name, t=512):
    h_n, n, dh = q.shape
    t = _pick(n, t)
    nb = n // t

    def body(q_ref, k_ref, v_ref, do_ref, lse_ref, cc_ref, cr_ref, dl_ref, acc_sc):
        i, j = pl.program_id(1), pl.program_id(2)

        @pl.when(j == 0)
        def _():
            acc_sc[...] = jnp.zeros_like(acc_sc)

        @pl.when(j <= i)
        def _():
            s = _attn_logits(q_ref[...], k_ref[...], cc_ref[...], cr_ref[...], i, j, t, scale)
            p = jnp.exp(s - lse_ref[...])
            dp = lax.dot_general(do_ref[...], v_ref[...], (((1,), (1,)), ((), ())), preferred_element_type=F32)
            acc_sc[...] += jnp.sum(p * dp, axis=-1, keepdims=True)

        @pl.when(j == nb - 1)
        def _():
            dl_ref[...] = acc_sc[...]

    qspec = pl.BlockSpec((None, t, dh), lambda h, i, j: (h, i, 0))
    kspec = pl.BlockSpec((None, t, dh), lambda h, i, j: (h, jnp.minimum(j, i), 0))
    colspec = pl.BlockSpec((None, t, 1), lambda h, i, j: (h, i, 0))
    return pl.pallas_call(
        body, name=name, grid=(h_n, nb, nb),
        in_specs=[qspec, kspec, kspec, qspec, colspec, colspec,
                  pl.BlockSpec((None, 1, t), lambda h, i, j: (h, 0, jnp.minimum(j, i)))],
        out_specs=colspec,
        out_shape=jax.ShapeDtypeStruct((h_n, n, 1), F32),
        scratch_shapes=[pltpu.VMEM((t, 1), F32)],
        compiler_params=_params(("parallel", "parallel", "arbitrary")),
    )(q, k, v, do, lse, ccol, crow)


def attn_backward(q, k, v, do, lse, ccol, crow, *, scale, tag):
    dob = do.astype(BF16)
    delta = flash_delta(q, k, v, dob, lse, ccol, crow, scale=scale, name=f"attn_delta_{tag}")
    return flash_bwd(q, k, v, dob, lse, delta, ccol, crow, scale=scale, name=f"attn_bwd_{tag}")


SCAN_STEPS = (1, 2, 4)


def ssm_scan(x, tab, *, reverse, name, s_prev=None, tt=512):
    n, width = x.shape
    nc, _, hw = tab.shape
    cw = 2 * hw
    assert width == nc * cw
    tt = _pick(n, tt, SUBLANES)
    nt = n // tt
    ng = tt // SUBLANES
    with_grad = s_prev is not None

    def body(*refs):
        if with_grad:
            x_ref, s_ref, tab_ref, o_ref, g_ref, car_ref = refs
        else:
            x_ref, tab_ref, o_ref, car_ref = refs

        @pl.when(pl.program_id(1) == 0)
        def _():
            car_ref[...] = jnp.zeros_like(car_ref)
            if with_grad:
                g_ref[...] = jnp.zeros_like(g_ref)

        q_re, q_im = tab_ref[0:8, :], tab_ref[8:16, :]
        p_re = [tab_ref[16 + i:17 + i, :] for i in range(3)]
        p_im = [tab_ref[24 + i:25 + i, :] for i in range(3)]
        row = lax.broadcasted_iota(jnp.int32, (SUBLANES, hw), 0)

        def group(gi, carry):
            c_re, c_im = carry
            g = (ng - 1 - gi) if reverse else gi
            r0 = pl.multiple_of(g * SUBLANES, SUBLANES)
            xr = x_ref[pl.ds(r0, SUBLANES), 0:hw]
            xi = x_ref[pl.ds(r0, SUBLANES), hw:cw]
            for i, d in enumerate(SCAN_STEPS):
                if reverse:
                    shift, keep = SUBLANES - d, row < SUBLANES - d
                else:
                    shift, keep = d, row >= d
                sr = jnp.where(keep, pltpu.roll(xr, shift, 0), 0.0)
                si = jnp.where(keep, pltpu.roll(xi, shift, 0), 0.0)
                xr, xi = (xr + p_re[i] * sr - p_im[i] * si,
                          xi + p_re[i] * si + p_im[i] * sr)
            xr, xi = (xr + q_re * c_re - q_im * c_im,
                      xi + q_re * c_im + q_im * c_re)
            o_ref[pl.ds(r0, SUBLANES), 0:hw] = xr
            o_ref[pl.ds(r0, SUBLANES), hw:cw] = xi
            if with_grad:
                nr = jnp.where(row < SUBLANES - 1, pltpu.roll(xr, SUBLANES - 1, 0), c_re)
                ni = jnp.where(row < SUBLANES - 1, pltpu.roll(xi, SUBLANES - 1, 0), c_im)
                sr = s_ref[pl.ds(r0, SUBLANES), 0:hw]
                si = s_ref[pl.ds(r0, SUBLANES), hw:cw]
                g_ref[:, 0:hw] += nr * sr + ni * si
                g_ref[:, hw:cw] += ni * sr - nr * si
            if reverse:
                return xr[0:1, :], xi[0:1, :]
            return xr[SUBLANES - 1:SUBLANES, :], xi[SUBLANES - 1:SUBLANES, :]

        c_re, c_im = lax.fori_loop(0, ng, group, (car_ref[0:1, 0:hw], car_ref[0:1, hw:cw]))
        car_ref[0:1, 0:hw] = c_re
        car_ref[0:1, hw:cw] = c_im

    if reverse:
        xspec = pl.BlockSpec((tt, cw), lambda c, t: (nt - 1 - t, c))
    else:
        xspec = pl.BlockSpec((tt, cw), lambda c, t: (t, c))
    tspec = pl.BlockSpec((None, 32, hw), lambda c, t: (c, 0, 0))
    in_specs = [xspec, xspec, tspec] if with_grad else [xspec, tspec]
    out_specs = [xspec]
    out_shape = [jax.ShapeDtypeStruct((n, width), F32)]
    if with_grad:
        out_specs.append(pl.BlockSpec((None, SUBLANES, cw), lambda c, t: (c, 0, 0)))
        out_shape.append(jax.ShapeDtypeStruct((nc, SUBLANES, cw), F32))
    operands = (x, s_prev, tab) if with_grad else (x, tab)
    return pl.pallas_call(
        body, name=name, grid=(nc, nt),
        in_specs=in_specs, out_specs=out_specs, out_shape=out_shape,
        scratch_shapes=[pltpu.VMEM((SUBLANES, cw), F32)],
        compiler_params=_params(("parallel", "arbitrary")),
    )(*operands)


def _slot(pos):
    return 4 * pos[0] + 2 * pos[1] + pos[2]


def all_gather(arrs, *, name):
    n = len(arrs)

    def body(*refs):
        ins, outs = refs[:n], refs[n:2 * n]
        send_sems, recv_sems, local_sems = refs[2 * n:]
        x, y, c = lax.axis_index("x"), lax.axis_index("y"), lax.axis_index("c")
        me, sibling = (x, y, c), (x, y, 1 - c)
        chips = [(1 - x, y), (x, 1 - y), (1 - x, 1 - y)]

        def copy(t, k, block, to, src=None):
            dst = outs[t].at[_slot(block)]
            return pltpu.make_async_remote_copy(
                src_ref=dst if src is None else src, dst_ref=dst,
                send_sem=send_sems.at[7 * t + k], recv_sem=recv_sems.at[7 * t + k],
                device_id=to, device_id_type=MESH)

        mine = [pltpu.make_async_copy(ins[t], outs[t].at[_slot(me)], local_sems.at[t]) for t in range(n)]
        for cp in mine:
            cp.start()
        first = []
        for t in range(n):
            first.append(copy(t, 0, me, sibling, src=ins[t]))
            first += [copy(t, 1 + j, me, (*chip, c), src=ins[t]) for j, chip in enumerate(chips)]
        for cp in first:
            cp.start()
        passed = []
        for t in range(n):
            for j, chip in enumerate(chips):
                copy(t, 1 + j, (*chip, c), me).wait_recv()
                cp = copy(t, 4 + j, (*chip, c), sibling)
                cp.start()
                passed.append(cp)
        for t in range(n):
            copy(t, 0, sibling, me).wait_recv()
            for j, chip in enumerate(chips):
                copy(t, 4 + j, (*chip, 1 - c), me).wait_recv()
        for cp in first + passed:
            cp.wait_send()
        for cp in mine:
            cp.wait()

    any_spec = pl.BlockSpec(memory_space=pl.ANY)
    return pl.pallas_call(
        body, name=name,
        in_specs=[any_spec] * n, out_specs=[any_spec] * n,
        out_shape=[jax.ShapeDtypeStruct((N_DEV,) + a.shape, a.dtype) for a in arrs],
        scratch_shapes=[pltpu.SemaphoreType.DMA((7 * n,)), pltpu.SemaphoreType.DMA((7 * n,)),
                        pltpu.SemaphoreType.DMA((n,))],
    )(*arrs)


def all_to_all(arrs, *, name):
    n = len(arrs)

    def body(*refs):
        ins, outs = refs[:n], refs[n:2 * n]
        send_sems, recv_sems, local_sems = refs[2 * n:]
        x, y, c = lax.axis_index("x"), lax.axis_index("y"), lax.axis_index("c")
        me = (x, y, c)
        peers = []
        for k in range(1, N_DEV):
            flip = ((k >> 2) & 1, (k >> 1) & 1, k & 1)
            peers.append(tuple(1 - p if f else p for p, f in zip(me, flip)))

        mine = [pltpu.make_async_copy(ins[t].at[_slot(me)], outs[t].at[_slot(me)], local_sems.at[t])
                for t in range(n)]
        for cp in mine:
            cp.start()
        copies = []
        for t in range(n):
            for k, peer in enumerate(peers):
                copies.append(pltpu.make_async_remote_copy(
                    src_ref=ins[t].at[_slot(peer)], dst_ref=outs[t].at[_slot(me)],
                    send_sem=send_sems.at[7 * t + k], recv_sem=recv_sems.at[7 * t + k],
                    device_id=peer, device_id_type=MESH))
        for cp in copies:
            cp.start()
        for t in range(n):
            for k, peer in enumerate(peers):
                pltpu.make_async_remote_copy(
                    src_ref=ins[t].at[_slot(peer)], dst_ref=outs[t].at[_slot(peer)],
                    send_sem=send_sems.at[7 * t + k], recv_sem=recv_sems.at[7 * t + k],
                    device_id=peer, device_id_type=MESH).wait()
        for cp in mine:
            cp.wait()

    any_spec = pl.BlockSpec(memory_space=pl.ANY)
    return pl.pallas_call(
        body, name=name,
        in_specs=[any_spec] * n, out_specs=[any_spec] * n,
        out_shape=[jax.ShapeDtypeStruct(a.shape, a.dtype) for a in arrs],
        scratch_shapes=[pltpu.SemaphoreType.DMA((7 * n,)), pltpu.SemaphoreType.DMA((7 * n,)),
                        pltpu.SemaphoreType.DMA((n,))],
    )(*arrs)


def _discretise(a_re, a_im, log_dt, b_re, b_im):
    ar = jnp.minimum(a_re, -1e-4)
    dt = jnp.exp(log_dt)[:, None]
    e, ph = ar * dt, a_im * dt
    mag = jnp.exp(e)
    lr, li = mag * jnp.cos(ph), mag * jnp.sin(ph)
    den = ar * ar + a_im * a_im
    nr, ni = lr - 1.0, li
    cr = (nr * ar + ni * a_im) / den
    ci = (ni * ar - nr * a_im) / den
    bb_re = cr[..., None] * b_re - ci[..., None] * b_im
    bb_im = cr[..., None] * b_im + ci[..., None] * b_re
    return e, ph, bb_re, bb_im


def _lam_pow(e, ph, k, conj):
    mag = jnp.exp(k * e)
    return mag * jnp.cos(k * ph), (-1.0 if conj else 1.0) * mag * jnp.sin(k * ph)


def _scan_table(e, ph, nc, reverse):
    hw = e.size // nc
    e, ph = e.reshape(nc, 1, hw), ph.reshape(nc, 1, hw)
    j = jnp.arange(SUBLANES, dtype=F32).reshape(1, SUBLANES, 1)
    kq = (SUBLANES - j) if reverse else (j + 1.0)
    q_re, q_im = _lam_pow(e, ph, kq, reverse)
    kp = jnp.array(SCAN_STEPS + (0,) * 5, F32).reshape(1, SUBLANES, 1)
    p_re, p_im = _lam_pow(e, ph, kp, reverse)
    return jnp.concatenate([q_re, q_im, p_re, p_im], axis=1)


def _blockdiag(m, nc):
    g, a, b = m.shape
    gc = g // nc
    m = m.reshape(nc, gc, a, b)
    eye = jnp.eye(gc, dtype=m.dtype)
    return jnp.einsum("cgab,gh->cgahb", m, eye).reshape(nc, gc * a, gc * b)


def _blockdiag_take(m, g):
    nc = m.shape[0]
    gc = g // nc
    a, b = m.shape[1] // gc, m.shape[2] // gc
    m = m.reshape(nc, gc, a, gc, b)
    eye = jnp.eye(gc, dtype=m.dtype)
    return jnp.einsum("cgahb,gh->cgab", m, eye).reshape(g, a, b)


def kernel(x, c, mod_w, mod_b, norm_pre, norm_post, ffn_w_in, ffn_w_out, mix_w_in, forget_b, ssm_a_re, ssm_a_im, ssm_log_dt, ssm_b_re, ssm_b_im, ssm_c_re, ssm_c_im, ssm_d, glu_w, attn_w_out, mix_w_out, loss_target, m_mod_w, m_mod_b, m_norm_pre, m_norm_post, m_ffn_w_in, m_ffn_w_out, m_mix_w_in, m_forget_b, m_ssm_a_re, m_ssm_a_im, m_ssm_log_dt, m_ssm_b_re, m_ssm_b_im, m_ssm_c_re, m_ssm_c_im, m_ssm_d, m_glu_w, m_attn_w_out, m_mix_w_out, v_mod_w, v_mod_b, v_norm_pre, v_norm_post, v_ffn_w_in, v_ffn_w_out, v_mix_w_in, v_forget_b, v_ssm_a_re, v_ssm_a_im, v_ssm_log_dt, v_ssm_b_re, v_ssm_b_im, v_ssm_c_re, v_ssm_c_im, v_ssm_d, v_glu_w, v_attn_w_out, v_mix_w_out):
    names = ["mod_w", "mod_b", "norm_pre", "norm_post", "ffn_w_in", "ffn_w_out", "mix_w_in", "forget_b",
             "ssm_a_re", "ssm_a_im", "ssm_log_dt", "ssm_b_re", "ssm_b_im", "ssm_c_re", "ssm_c_im", "ssm_d",
             "glu_w", "attn_w_out", "mix_w_out"]
    w_in = dict(zip(names, [mod_w, mod_b, norm_pre, norm_post, ffn_w_in, ffn_w_out, mix_w_in, forget_b,
                            ssm_a_re, ssm_a_im, ssm_log_dt, ssm_b_re, ssm_b_im, ssm_c_re, ssm_c_im, ssm_d,
                            glu_w, attn_w_out, mix_w_out]))
    m_in = dict(zip(names, [m_mod_w, m_mod_b, m_norm_pre, m_norm_post, m_ffn_w_in, m_ffn_w_out, m_mix_w_in,
                            m_forget_b, m_ssm_a_re, m_ssm_a_im, m_ssm_log_dt, m_ssm_b_re, m_ssm_b_im,
                            m_ssm_c_re, m_ssm_c_im, m_ssm_d, m_glu_w, m_attn_w_out, m_mix_w_out]))
    v_in = dict(zip(names, [v_mod_w, v_mod_b, v_norm_pre, v_norm_post, v_ffn_w_in, v_ffn_w_out, v_mix_w_in,
                            v_forget_b, v_ssm_a_re, v_ssm_a_im, v_ssm_log_dt, v_ssm_b_re, v_ssm_b_im,
                            v_ssm_c_re, v_ssm_c_im, v_ssm_d, v_glu_w, v_attn_w_out, v_mix_w_out]))

    depth = mod_w.shape[0]
    n_tok, d = x.shape[1], x.shape[2]
    ff = ffn_w_out.shape[2] * N_DEV
    heads = forget_b.shape[1]
    sw = ssm_d.shape[1]
    g_n, p_n, n_n = ssm_b_re.shape[1:]
    aw = attn_w_out.shape[1]
    dh = aw // heads
    iw = mix_w_in.shape[2] * N_DEV
    nc = sw // LANES
    hw = g_n * p_n // nc
    mod_cols = mod_w.shape[2]
    scale = dh ** -0.5
    assert iw == sw + 3 * aw + heads + 2 * d and heads <= LANES
    off_u, off_q, off_f = 2 * d, 2 * d + sw, 2 * d + sw + 3 * aw
    iwp = off_f + LANES
    assert off_u % sw == 0 and off_q % aw == 0 and off_f % LANES == 0

    me = 4 * lax.axis_index("x") + 2 * lax.axis_index("y") + lax.axis_index("c")
    x2 = x.reshape(n_tok, d)
    tgt = loss_target.reshape(n_tok, d)

    silu_c = rowwise(_f_silu, [c], [], [((d,), F32)], name="silu_c")[0]
    big = ["ffn_w_in", "ffn_w_out", "mix_w_in", "glu_w", "attn_w_out", "mix_w_out"]
    gathered = all_gather(
        [silu_c, norm_pre.reshape(-1, norm_pre.shape[-1]), norm_post.reshape(-1, norm_post.shape[-1])]
        + [w_in[k].astype(BF16) for k in big], name="gather_weights")
    sc_all = gathered[0].reshape(N_DEV, d)
    gpre = jnp.moveaxis(gathered[1].reshape(N_DEV, depth, 3, -1), 0, 2).reshape(depth, 3, d)
    gpost = jnp.moveaxis(gathered[2].reshape(N_DEV, depth, 3, -1), 0, 2).reshape(depth, 3, d)
    gw = dict(zip(big, gathered[3:]))
    w_ffn_in = jnp.moveaxis(gw["ffn_w_in"], 0, 3).reshape(depth, 2, d, 2 * ff)
    w_ffn_out = jnp.moveaxis(gw["ffn_w_out"], 0, 2).reshape(depth, 2, ff, d)
    w_mix_in = jnp.moveaxis(gw["mix_w_in"], 0, 2).reshape(depth, d, iw)
    w_glu = jnp.moveaxis(gw["glu_w"], 0, 2).reshape(depth, sw, 2 * d)
    w_ao = jnp.moveaxis(gw["attn_w_out"], 0, 2).reshape(depth, aw, d)
    w_mo = gw["mix_w_out"].transpose(1, 0, 2, 3).reshape(depth, d, d)
    cut = [0, sw, sw + aw, sw + 2 * aw, sw + 3 * aw, sw + 3 * aw + heads, sw + 3 * aw + heads + d, iw]
    seg = lambda a, i: a[..., cut[i]:cut[i + 1]]
    w_mix_p = jnp.concatenate(
        [seg(w_mix_in, 5), seg(w_mix_in, 6), seg(w_mix_in, 0), seg(w_mix_in, 1), seg(w_mix_in, 2),
         seg(w_mix_in, 3), jnp.pad(seg(w_mix_in, 4), ((0, 0), (0, 0), (0, LANES - heads)))], axis=-1)

    sc_pad = jnp.pad(sc_all, ((0, LANES - N_DEV), (0, 0)))
    mod_part = jnp.stack([mm(sc_pad, mod_w[l], name=f"mod_fwd{l}")[:N_DEV] for l in range(depth)], axis=1)
    mod_part = mod_part + lax.dynamic_slice_in_dim(mod_b, me * mod_cols, mod_cols, axis=1)[None]
    mod_all = all_gather([mod_part], name="gather_mod")[0]
    mod_own = lax.dynamic_index_in_dim(mod_all, me, axis=1, keepdims=False)
    mod_own = mod_own.transpose(1, 0, 2).reshape(depth, 3, 3, d)
    res_w = (FFN_RES, 1.0, FFN_RES)

    def vec_a(l, i):
        return (gpre[l, i] * (1.0 + mod_own[l, i, 1])).reshape(1, d)

    def vec_sh(l, i):
        return mod_own[l, i, 0].reshape(1, d)

    def vec_b(l, i):
        return (res_w[i] * mod_own[l, i, 2] * gpost[l, i]).reshape(1, d)

    ssm = []
    for l in range(depth):
        (e, ph, bb_re, bb_im), disc_vjp = jax.vjp(_discretise,ssm_a_re[l], ssm_a_im[l], ssm_log_dt[l],
                                                  ssm_b_re[l], ssm_b_im[l])
        b_mat = jnp.concatenate([_blockdiag(bb_re.transpose(0, 2, 1), nc),
                                 _blockdiag(bb_im.transpose(0, 2, 1), nc)], axis=2)
        c_mat = jnp.concatenate([_blockdiag(ssm_c_re[l].transpose(0, 2, 1), nc),
                                 _blockdiag(-ssm_c_im[l].transpose(0, 2, 1), nc)], axis=1)
        ssm.append(dict(e=e, ph=ph, vjp=disc_vjp, b=b_mat.astype(BF16), c=c_mat.astype(BF16),
                        bt=b_mat.transpose(0, 2, 1).astype(BF16), ct=c_mat.transpose(0, 2, 1).astype(BF16),
                        tab_f=_scan_table(e, ph, nc, False), tab_r=_scan_table(e, ph, nc, True),
                        dvec=ssm_d[l].reshape(1, sw)))

    fb_pad = jnp.pad(forget_b, ((0, 0), (0, LANES - heads)))

    def heads_first(a):
        return a.reshape(n_tok, heads, dh).transpose(1, 0, 2)

    def heads_last(a):
        return a.transpose(1, 0, 2).reshape(n_tok, heads * dh)

    def ffn_fwd(xin, l, i, j, tag):
        h = rowwise(_f_pre, [xin], [vec_a(l, i), vec_sh(l, i)], [((d,), BF16)], name=f"pre_{tag}")[0]
        a = mm(h, w_ffn_in[l, j], name=f"ffn_in_{tag}", out_dtype=BF16)
        m = rowwise(_f_swiglu, [(a, ff, 0), (a, ff, 1)], [], [((ff,), BF16)], name=f"swiglu_{tag}")[0]
        y = mm(m, w_ffn_out[l, j], name=f"ffn_out_{tag}", tk=1408)
        xout = rowwise(_f_post_add, [xin, y], [vec_b(l, i)], [((d,), F32)], name=f"post_{tag}")[0]
        return xout, dict(x=xin, h=h, a=a, m=m, y=y)

    def mixer_fwd(xin, l, tag):
        s5 = ssm[l]
        h = rowwise(_f_pre, [xin], [vec_a(l, 1), vec_sh(l, 1)], [((d,), BF16)], name=f"pre_{tag}")[0]
        proj = mm(h, w_mix_p[l], name=f"mix_in_{tag}", tn=384)
        bu = mm_blockdiag(proj, s5["b"], a_cb0=off_u // LANES, name=f"ssm_bu_{tag}")
        st = ssm_scan(bu, s5["tab_f"], reverse=False, name=f"ssm_scan_{tag}")[0]
        y0 = mm_blockdiag(st, s5["c"], name=f"ssm_y_{tag}")
        ge, ys = rowwise(_f_gelu_in, [y0, (proj, sw, off_u // sw)], [s5["dvec"]],
                         [((sw,), BF16), ((sw,), F32)], name=f"gelu_{tag}")
        z = mm(ge, w_glu[l], name=f"glu_{tag}")
        cum = cum_fwd(proj, fb_pad[l:l + 1], col_block=off_f // LANES, name=f"cum_{tag}")
        cum_h = cum[:, :heads].T
        ccol, crow = cum_h[:, :, None], cum_h[:, None, :]
        q, k, v = [heads_first(proj[:, off_q + i * aw:off_q + (i + 1) * aw]).astype(BF16) for i in range(3)]
        o, lse = flash_fwd(q, k, v, ccol, crow, scale=scale, name=f"attn_{tag}")
        attn = heads_last(o).astype(BF16)
        yb = mm(attn, w_ao[l], name=f"attn_out_{tag}")
        mg = rowwise(_f_merge, [(z, d, 0), (z, d, 1), yb, (proj, d, 0), (proj, d, 1)], [],
                     [((d,), BF16)], name=f"merge_{tag}")[0]
        y = mm(mg, w_mo[l], name=f"mix_out_{tag}")
        xout = rowwise(_f_post_add, [xin, y], [vec_b(l, 1)], [((d,), F32)], name=f"post_{tag}")[0]
        saved = dict(x=xin, h=h, proj=proj, st=st, ys=ys, ge=ge, z=z, q=q, k=k, v=v, o=o, lse=lse,
                     ccol=ccol, crow=crow, attn=attn, yb=yb, mg=mg, y=y)
        return xout, saved

    saved = []
    xc = x2
    for l in range(depth):
        xc, s0 = ffn_fwd(xc, l, 0, 0, f"l{l}a")
        xc, s1 = mixer_fwd(xc, l, f"l{l}m")
        xc, s2 = ffn_fwd(xc, l, 2, 1, f"l{l}b")
        saved.append((s0, s1, s2))

    def f_loss(xf, t):
        e_ = xf - t
        return e_ * (1.0 / d), _colsum(e_ * e_)

    dx, sq = rowwise(f_loss, [xc, tgt], [], [((d,), F32)], [d], name="loss_head")
    loss_part = 0.5 * jnp.sum(sq) / d

    grads = {k: [None] * depth for k in big}
    small_g = [dict() for _ in range(depth)]
    dmod = [[None] * 3 for _ in range(depth)]
    dgpre = [[None] * 3 for _ in range(depth)]
    dgpost = [[None] * 3 for _ in range(depth)]

    def norm_grads(l, i, d_a, d_sh, d_bv):
        d_a, d_sh, d_bv = d_a.reshape(d), d_sh.reshape(d), d_bv.reshape(d)
        dmod[l][i] = jnp.stack([d_sh, d_a * gpre[l, i], res_w[i] * gpost[l, i] * d_bv])
        dgpre[l][i] = d_a * (1.0 + mod_own[l, i, 1])
        dgpost[l][i] = res_w[i] * mod_own[l, i, 2] * d_bv

    def ffn_bwd(dxo, sv, l, i, j, tag):
        dy, d_bv = rowwise(_f_post_bwd, [dxo, sv["y"]], [vec_b(l, i)], [((d,), BF16)], [d],
                           name=f"post_bwd_{tag}")
        dm = mm(dy, w_ffn_out[l, j], trans_b=True, name=f"ffn_out_dx_{tag}", out_dtype=BF16, tn=1408)
        g_out = mm(sv["m"], dy, trans_a=True, name=f"ffn_out_dw_{tag}", out_dtype=BF16, tm=1408, tn=1024,
                   tk=512)
        da = rowwise(_f_swiglu_bwd, [(sv["a"], ff, 0), (sv["a"], ff, 1), dm], [], [((ff, ff), BF16)],
                     name=f"swiglu_bwd_{tag}")[0]
        g_in = mm(sv["h"], da, trans_a=True, name=f"ffn_in_dw_{tag}", out_dtype=BF16, tk=512)
        dh_ = mm(da, w_ffn_in[l, j], trans_b=True, name=f"ffn_in_dx_{tag}", tk=1408)
        dxn, d_a, d_sh = rowwise(_f_pre_bwd, [dxo, dh_, sv["x"]], [vec_a(l, i)], [((d,), F32)], [d, d],
                                 name=f"pre_bwd_{tag}")
        norm_grads(l, i, d_a, d_sh, d_bv)
        return dxn, g_in, g_out

    def mixer_bwd(dxo, sv, l, tag):
        s5 = ssm[l]
        proj = sv["proj"]
        dy, d_bv = rowwise(_f_post_bwd, [dxo, sv["y"]], [vec_b(l, 1)], [((d,), BF16)], [d],
                           name=f"post_bwd_{tag}")
        dmg = mm(dy, w_mo[l], trans_b=True, name=f"mix_out_dx_{tag}")
        g_mo = mm(sv["mg"], dy, trans_a=True, name=f"mix_out_dw_{tag}", out_dtype=BF16, tk=512)
        dz, dyb, dgab = rowwise(
            _f_merge_bwd, [dmg, (sv["z"], d, 0), (sv["z"], d, 1), sv["yb"], (proj, d, 0), (proj, d, 1)], [],
            [((d, d), BF16), ((d,), BF16), ((d, d), BF16)], name=f"merge_bwd_{tag}")
        dge = mm(dz, w_glu[l], trans_b=True, name=f"glu_dx_{tag}")
        g_glu = mm(sv["ge"], dz, trans_a=True, name=f"glu_dw_{tag}", out_dtype=BF16, tk=512)
        dys, d_dvec = rowwise(_f_gelu_bwd, [dge, sv["ys"], (proj, sw, off_u // sw)], [], [((sw,), BF16)],
                              [sw], name=f"gelu_bwd_{tag}")
        gadj = mm_blockdiag(dys, s5["ct"], name=f"ssm_dy_{tag}")
        adj, dlam8 = ssm_scan(gadj, s5["tab_r"], reverse=True, s_prev=sv["st"], name=f"ssm_scan_bwd_{tag}")
        du0 = mm_blockdiag(adj, s5["bt"], name=f"ssm_du_{tag}")
        d_bmat = mm_blockdiag_tn(proj, adj, g_n=nc, ka=LANES, kb=2 * hw, a_cb0=off_u // LANES,
                                 name=f"ssm_db_{tag}")
        d_cmat = mm_blockdiag_tn(sv["st"], dys, g_n=nc, ka=2 * hw, kb=LANES, name=f"ssm_dc_{tag}")
        du = rowwise(_f_du_fin, [du0, dys], [s5["dvec"]], [((sw,), BF16)], name=f"ssm_du_fin_{tag}")[0]
        dlam = jnp.sum(dlam8, axis=1)
        dlam_re, dlam_im = dlam[:, :hw].reshape(g_n, p_n), dlam[:, hw:].reshape(g_n, p_n)
        dbb_re = _blockdiag_take(d_bmat[:, :, :hw], g_n).transpose(0, 2, 1)
        dbb_im = _blockdiag_take(d_bmat[:, :, hw:], g_n).transpose(0, 2, 1)
        mag = jnp.exp(s5["e"])
        lr, li = mag * jnp.cos(s5["ph"]), mag * jnp.sin(s5["ph"])
        d_e = dlam_re * lr + dlam_im * li
        d_ph = -dlam_re * li + dlam_im * lr
        da_re, da_im, dlog_dt, db_re, db_im = s5["vjp"]((d_e, d_ph, dbb_re, dbb_im))
        small_g[l].update(
            ssm_a_re=da_re, ssm_a_im=da_im, ssm_log_dt=dlog_dt, ssm_b_re=db_re, ssm_b_im=db_im,
            ssm_c_re=_blockdiag_take(d_cmat[:, :hw, :], g_n).transpose(0, 2, 1),
            ssm_c_im=-_blockdiag_take(d_cmat[:, hw:, :], g_n).transpose(0, 2, 1),
            ssm_d=d_dvec.reshape(sw))
        dattn = mm(dyb, w_ao[l], trans_b=True, name=f"attn_out_dx_{tag}")
        g_ao = mm(sv["attn"], dyb, trans_a=True, name=f"attn_out_dw_{tag}", out_dtype=BF16, tk=512)
        dq, dk, dv, dcr = attn_backward(sv["q"], sv["k"], sv["v"], heads_first(dattn), sv["lse"],
                                        sv["ccol"], sv["crow"], scale=scale, tag=tag)
        dcum = jnp.pad(dcr[:, 0, :].T, ((0, 0), (0, LANES - heads)))
        df, dfb = cum_bwd(dcum, proj, fb_pad[l:l + 1], col_block=off_f // LANES, name=f"cum_bwd_{tag}")
        small_g[l]["forget_b"] = dfb[0, :heads]
        dproj = jnp.concatenate([dgab, du, heads_last(dq).astype(BF16), heads_last(dk).astype(BF16),
                                 heads_last(dv).astype(BF16), df.astype(BF16)], axis=1)
        g_mi = mm(sv["h"], dproj, trans_a=True, name=f"mix_in_dw_{tag}", out_dtype=BF16, tn=384, tk=512)
        dh_ = mm(dproj, w_mix_p[l], trans_b=True, name=f"mix_in_dx_{tag}", tk=1408)
        dxn, d_a, d_sh = rowwise(_f_pre_bwd, [dxo, dh_, sv["x"]], [vec_a(l, 1)], [((d,), F32)], [d, d],
                                 name=f"pre_bwd_{tag}")
        norm_grads(l, 1, d_a, d_sh, d_bv)
        g_mi = jnp.concatenate([g_mi[:, off_u:off_f + heads], g_mi[:, :off_u]], axis=1)
        return dxn, g_mi, g_glu, g_ao, g_mo

    g_ffn_in = [[None, None] for _ in range(depth)]
    g_ffn_out = [[None, None] for _ in range(depth)]
    for l in reversed(range(depth)):
        s0, s1, s2 = saved[l]
        dx, g_ffn_in[l][1], g_ffn_out[l][1] = ffn_bwd(dx, s2, l, 2, 1, f"l{l}b")
        dx, grads["mix_w_in"][l], grads["glu_w"][l], grads["attn_w_out"][l], grads["mix_w_out"][l] = \
            mixer_bwd(dx, s1, l, f"l{l}m")
        dx, g_ffn_in[l][0], g_ffn_out[l][0] = ffn_bwd(dx, s0, l, 0, 0, f"l{l}a")
    grad_x = dx.reshape(x.shape)

    small_names = ["forget_b", "ssm_a_re", "ssm_a_im", "ssm_log_dt", "ssm_b_re", "ssm_b_im", "ssm_c_re",
                   "ssm_c_im", "ssm_d"]
    pieces = [loss_part.reshape(1), jnp.stack([jnp.stack(dmod[l]) for l in range(depth)]).reshape(-1),
              jnp.stack([jnp.stack(dgpre[l]) for l in range(depth)]).reshape(-1),
              jnp.stack([jnp.stack(dgpost[l]) for l in range(depth)]).reshape(-1)]
    pieces += [jnp.stack([small_g[l][k] for l in range(depth)]).reshape(-1) for k in small_names]
    sizes = [p.size for p in pieces]
    chunk = SUBLANES * 1024
    total = -(-sum(sizes) // chunk) * chunk
    pack = jnp.pad(jnp.concatenate(pieces), (0, total - sum(sizes))).reshape(total // 1024, 1024)
    pack_all = all_gather([pack], name="gather_small_grads")[0]
    pack_sum = rowwise(_f_sum_parts, [(pack_all, p) for p in range(N_DEV)], [], [((1024,), F32)],
                       name="sum_small_grads")[0].reshape(-1)
    offs = [0]
    for s_ in sizes:
        offs.append(offs[-1] + s_)
    take = lambda i: pack_sum[offs[i]:offs[i + 1]]
    loss = take(0).reshape(())
    g_small = {"mod_b": take(1).reshape(mod_b.shape)}
    g_pre_full, g_post_full = take(2).reshape(depth, 3, d), take(3).reshape(depth, 3, d)
    shard = norm_pre.shape[-1]
    g_small["norm_pre"] = lax.dynamic_slice_in_dim(g_pre_full, me * shard, shard, axis=2)
    g_small["norm_post"] = lax.dynamic_slice_in_dim(g_post_full, me * shard, shard, axis=2)
    for i, k in enumerate(small_names):
        g_small[k] = take(4 + i).reshape(w_in[k].shape)

    dmod_all = pack_all.reshape(N_DEV, -1)[:, offs[1]:offs[2]].reshape(N_DEV, depth, 9 * d)
    dmod_mine = lax.dynamic_slice_in_dim(dmod_all, me * mod_cols, mod_cols, axis=2)
    sct_pad = jnp.pad(sc_all.T, ((0, 0), (0, LANES - N_DEV)))
    g_mod_w = jnp.stack([
        mm(sct_pad, jnp.pad(dmod_mine[:, l], ((0, LANES - N_DEV), (0, 0))), name=f"mod_dw{l}")
        for l in range(depth)])

    g_full = {
        "ffn_w_in": jnp.stack([jnp.stack(g_ffn_in[l]) for l in range(depth)]),
        "ffn_w_out": jnp.stack([jnp.stack(g_ffn_out[l]) for l in range(depth)]),
        "mix_w_in": jnp.stack(grads["mix_w_in"]), "glu_w": jnp.stack(grads["glu_w"]),
        "attn_w_out": jnp.stack(grads["attn_w_out"]), "mix_w_out": jnp.stack(grads["mix_w_out"])}

    def split_last(a):
        return jnp.moveaxis(a.reshape(a.shape[:-1] + (N_DEV, a.shape[-1] // N_DEV)), -2, 0)

    def split_rows(a):
        return jnp.moveaxis(a.reshape(a.shape[:-2] + (N_DEV, a.shape[-2] // N_DEV, a.shape[-1])), -3, 0)

    to_owner = [split_last(g_full["ffn_w_in"]), split_rows(g_full["ffn_w_out"]), split_last(g_full["mix_w_in"]),
                split_last(g_full["glu_w"]), split_last(g_full["attn_w_out"]), split_rows(g_full["mix_w_out"])]
    to_owner = [a.reshape(N_DEV, -1, a.shape[-1]) for a in to_owner]
    parts = dict(zip(big, all_to_all(to_owner, name="exchange_grads")))
    parts["mod_w"] = g_mod_w.reshape(1, -1, mod_cols)

    out_g, out_d, out_m, out_v = {}, {}, {}, {}
    for k in ["mod_w"] + big:
        shape = w_in[k].shape
        flat = lambda a: a.reshape(-1, shape[-1])
        res = adamw(parts[k], flat(w_in[k]), flat(m_in[k]), flat(v_in[k]), name=f"adamw_{k}")
        out_g[k], out_d[k], out_m[k], out_v[k] = [r.reshape(shape) for r in res]
    small_all = ["mod_b", "norm_pre", "norm_post"] + small_names

    def pack_small(dct):
        flat = jnp.concatenate([dct[k].reshape(-1) for k in small_all])
        tot = -(-flat.size // chunk) * chunk
        return jnp.pad(flat, (0, tot - flat.size)).reshape(tot // 1024, 1024)

    res = adamw(pack_small(g_small)[None], pack_small(w_in), pack_small(m_in), pack_small(v_in),
                name="adamw_small")
    pos = 0
    for k in small_all:
        size = w_in[k].size
        for dct, r in zip((out_g, out_d, out_m, out_v), res):
            dct[k] = r.reshape(-1)[pos:pos + size].reshape(w_in[k].shape)
        pos += size

    return (loss, grad_x, *[out_g[k] for k in names], *[out_d[k] for k in names],
            *[out_m[k] for k in names], *[out_v[k] for k in names])
```

```python
import functools
import math

import jax
import jax.numpy as jnp
from jax import lax
from jax.experimental import pallas as pl
from jax.experimental.pallas import tpu as pltpu

F32 = jnp.float32
BF16 = jnp.bfloat16
MESH = pl.DeviceIdType.MESH
N_DEV = 8
LANES = 128
SUBLANES = 8
VMEM_LIMIT = 48 * 1024 * 1024

RMS_EPS = 1e-6
FFN_RES = 0.5
ADAM_LR = 0.001
ADAM_B1 = 0.9
ADAM_B2 = 0.999
ADAM_EPS = 1e-08
ADAM_WD = 0.01
ADAM_STEP = 10
GELU_C = math.sqrt(2.0 / math.pi)
GELU_A = 0.044715


def _pick(dim, target, mult=LANES):
    t = (min(dim, target) // mult) * mult
    while t >= mult:
        if dim % t == 0:
            return t
        t -= mult
    return dim


def _params(sem):
    return pltpu.CompilerParams(dimension_semantics=sem, vmem_limit_bytes=VMEM_LIMIT)


def _sigmoid(x):
    return 1.0 / (1.0 + jnp.exp(-x))


def mm(a, b, *, name, trans_a=False, trans_b=False, out_dtype=F32, tm=1024, tn=512, tk=1024):
    if trans_a:
        kdim, m = a.shape
    else:
        m, kdim = a.shape
    if trans_b:
        n, kb = b.shape
    else:
        kb, n = b.shape
    assert kdim == kb, (a.shape, b.shape)
    tm, tn, tk = _pick(m, tm), _pick(n, tn), _pick(kdim, tk)
    nk = kdim // tk
    dims = (((0 if trans_a else 1,), (1 if trans_b else 0,)), ((), ()))

    def body(a_ref, b_ref, o_ref, acc_ref):
        k = pl.program_id(2)

        @pl.when(k == 0)
        def _():
            acc_ref[...] = jnp.zeros_like(acc_ref)

        acc_ref[...] += lax.dot_general(a_ref[...].astype(BF16), b_ref[...].astype(BF16), dims,
                                        preferred_element_type=F32)

        @pl.when(k == nk - 1)
        def _():
            o_ref[...] = acc_ref[...].astype(out_dtype)

    a_spec = (pl.BlockSpec((tk, tm), lambda i, j, k: (k, i)) if trans_a
              else pl.BlockSpec((tm, tk), lambda i, j, k: (i, k)))
    b_spec = (pl.BlockSpec((tn, tk), lambda i, j, k: (j, k)) if trans_b
              else pl.BlockSpec((tk, tn), lambda i, j, k: (k, j)))
    return pl.pallas_call(
        body, name=name, grid=(m // tm, n // tn, nk),
        in_specs=[a_spec, b_spec],
        out_specs=pl.BlockSpec((tm, tn), lambda i, j, k: (i, j)),
        out_shape=jax.ShapeDtypeStruct((m, n), out_dtype),
        scratch_shapes=[pltpu.VMEM((tm, tn), F32)],
        compiler_params=_params(("parallel", "parallel", "arbitrary")),
    )(a, b)


def mm_blockdiag(a, b, *, name, a_cb0=0, out_dtype=F32, tm=512):
    m = a.shape[0]
    g_n, ka, nb = b.shape
    tm = _pick(m, tm)

    def body(a_ref, b_ref, o_ref):
        o_ref[...] = jnp.dot(a_ref[...].astype(BF16), b_ref[...].astype(BF16),
                             preferred_element_type=F32).astype(out_dtype)

    return pl.pallas_call(
        body, name=name, grid=(m // tm, g_n),
        in_specs=[pl.BlockSpec((tm, ka), lambda i, g: (i, a_cb0 + g)),
                  pl.BlockSpec((None, ka, nb), lambda i, g: (g, 0, 0))],
        out_specs=pl.BlockSpec((tm, nb), lambda i, g: (i, g)),
        out_shape=jax.ShapeDtypeStruct((m, g_n * nb), out_dtype),
        compiler_params=_params(("parallel", "parallel")),
    )(a, b)


def mm_blockdiag_tn(a, b, *, name, g_n, ka, kb, a_cb0=0, b_cb0=0, tk=512):
    rows = a.shape[0]
    tk = _pick(rows, tk)
    nk = rows // tk

    def body(a_ref, b_ref, o_ref):
        k = pl.program_id(1)

        @pl.when(k == 0)
        def _():
            o_ref[...] = jnp.zeros_like(o_ref)

        o_ref[...] += lax.dot_general(a_ref[...].astype(BF16), b_ref[...].astype(BF16),
                                      (((0,), (0,)), ((), ())), preferred_element_type=F32)

    return pl.pallas_call(
        body, name=name, grid=(g_n, nk),
        in_specs=[pl.BlockSpec((tk, ka), lambda g, k: (k, a_cb0 + g)),
                  pl.BlockSpec((tk, kb), lambda g, k: (k, b_cb0 + g))],
        out_specs=pl.BlockSpec((None, ka, kb), lambda g, k: (g, 0, 0)),
        out_shape=jax.ShapeDtypeStruct((g_n, ka, kb), F32),
        compiler_params=_params(("parallel", "arbitrary")),
    )(a, b)


def rowwise(fn, rows, vecs, outs, reds=(), *, name, tm=256):
    metas = []
    for r in rows:
        if isinstance(r, tuple) and len(r) == 3:
            metas.append(("col", r[0], r[1], r[2]))
        elif isinstance(r, tuple):
            metas.append(("lead", r[0], r[0].shape[2], r[1]))
        else:
            metas.append(("full", r, r.shape[1], 0))
    n_rows = metas[0][1].shape[1] if metas[0][0] == "lead" else metas[0][1].shape[0]
    rc = 16 if n_rows % 16 == 0 else (SUBLANES if n_rows % SUBLANES == 0 else n_rows)
    tm = _pick(n_rows, tm, rc)
    n_inner = tm // rc
    nr, nv, no = len(metas), len(vecs), len(outs)

    def body(*refs):
        row_refs, vec_refs = refs[:nr], refs[nr:nr + nv]
        out_refs, red_refs = refs[nr + nv:nr + nv + no], refs[nr + nv + no:]
        if reds:
            @pl.when(pl.program_id(0) == 0)
            def _():
                for rr in red_refs:
                    rr[...] = jnp.zeros_like(rr)
        vec_vals = [v[...] for v in vec_refs]

        def step(s, carry):
            r0 = pl.multiple_of(s * rc, rc)
            vals = [ref[pl.ds(r0, rc), :] for ref in row_refs]
            res = fn(*vals, *vec_vals)
            if not isinstance(res, (tuple, list)):
                res = (res,)
            for o_ref, (widths, dt), val in zip(out_refs, outs, res[:no]):
                pieces = val if isinstance(val, (tuple, list)) else (val,)
                off = 0
                for w_, piece in zip(widths, pieces):
                    o_ref[pl.ds(r0, rc), off:off + w_] = piece.astype(dt)
                    off += w_
            for rr, val in zip(red_refs, res[no:]):
                rr[...] += val
            return carry

        lax.fori_loop(0, n_inner, step, 0, unroll=min(n_inner, 4))

    in_specs = []
    for kind, arr, w_, idx in metas:
        if kind == "col":
            in_specs.append(pl.BlockSpec((tm, w_), functools.partial(lambda i, cb: (i, cb), cb=idx)))
        elif kind == "lead":
            in_specs.append(pl.BlockSpec((None, tm, w_), functools.partial(lambda i, p: (p, i, 0), p=idx)))
        else:
            in_specs.append(pl.BlockSpec((tm, w_), lambda i: (i, 0)))
    for v in vecs:
        in_specs.append(pl.BlockSpec(v.shape, lambda i: (0, 0)))
    out_specs = [pl.BlockSpec((tm, sum(ws)), lambda i: (i, 0)) for ws, _ in outs]
    out_specs += [pl.BlockSpec((1, w_), lambda i: (0, 0)) for w_ in reds]
    out_shape = [jax.ShapeDtypeStruct((n_rows, sum(ws)), dt) for ws, dt in outs]
    out_shape += [jax.ShapeDtypeStruct((1, w_), F32) for w_ in reds]
    res = pl.pallas_call(
        body, name=name, grid=(n_rows // tm,),
        in_specs=in_specs, out_specs=out_specs, out_shape=out_shape,
        compiler_params=_params(("arbitrary",)),
    )(*[m[1] for m in metas], *vecs)
    return res


def _rms(x):
    return lax.rsqrt(jnp.mean(x * x, axis=-1, keepdims=True) + RMS_EPS)


def _colsum(x):
    return jnp.sum(x, axis=0, keepdims=True)


def _f_silu(c):
    return c * _sigmoid(c)


def _f_pre(x, a, sh):
    return (x * _rms(x)) * a + sh


def _f_post_add(x, y, bv):
    return x + (y * _rms(y)) * bv


def _f_post_bwd(dxo, y, bv):
    ry = _rms(y)
    yn = y * ry
    dyn = dxo * bv
    dy = ry * (dyn - yn * jnp.mean(dyn * yn, axis=-1, keepdims=True))
    return dy, _colsum(dxo * yn)


def _f_pre_bwd(dxo, dh, x, a):
    r = _rms(x)
    xn = x * r
    dxn = dh * a
    dx = dxo + r * (dxn - xn * jnp.mean(dxn * xn, axis=-1, keepdims=True))
    return dx, _colsum(dh * xn), _colsum(dh)


def _f_swiglu(g, u):
    g = g.astype(F32)
    return (g * _sigmoid(g)) * u.astype(F32)


def _f_swiglu_bwd(g, u, dm):
    g, u, dm = g.astype(F32), u.astype(F32), dm.astype(F32)
    sg = _sigmoid(g)
    dg = dm * u * (sg * (1.0 + g * (1.0 - sg)))
    du = dm * (g * sg)
    return ((dg, du),)


def _gelu_t(x):
    return jnp.tanh(GELU_C * (x + GELU_A * x * x * x))


def _f_gelu_in(y0, u, dvec):
    y = y0 + dvec * u
    return 0.5 * y * (1.0 + _gelu_t(y)), y


def _f_gelu_bwd(dge, y, u):
    t = _gelu_t(y)
    dy = dge * (0.5 * (1.0 + t) + 0.5 * y * (1.0 - t * t) * GELU_C * (1.0 + 3.0 * GELU_A * y * y))
    return dy, _colsum(dy * u)


def _f_du_fin(du0, dys, dvec):
    return du0 + dvec * dys.astype(F32)


def _f_merge(zv, zg, yb, ga, gb):
    return _sigmoid(ga) * (zv * _sigmoid(zg)) + _sigmoid(gb) * yb


def _f_merge_bwd(dmg, zv, zg, yb, ga, gb):
    sa, sb, sz = _sigmoid(ga), _sigmoid(gb), _sigmoid(zg)
    ya = zv * sz
    dya = dmg * sa
    dga = dmg * ya * sa * (1.0 - sa)
    dyb = dmg * sb
    dgb = dmg * yb * sb * (1.0 - sb)
    dzv = dya * sz
    dzg = dya * zv * sz * (1.0 - sz)
    return (dzv, dzg), dyb, (dga, dgb)


def _f_sum_parts(*parts):
    acc = parts[0].astype(F32)
    for p in parts[1:]:
        acc = acc + p.astype(F32)
    return acc


def _f_adamw(*args):
    parts, (w, m, v) = args[:-3], args[-3:]
    g = _f_sum_parts(*parts)
    m = ADAM_B1 * m + (1.0 - ADAM_B1) * g
    v = ADAM_B2 * v + (1.0 - ADAM_B2) * (g * g)
    m_hat = m / (1.0 - ADAM_B1 ** ADAM_STEP)
    v_hat = v / (1.0 - ADAM_B2 ** ADAM_STEP)
    delta = -ADAM_LR * (m_hat / (jnp.sqrt(v_hat) + ADAM_EPS) + ADAM_WD * w)
    return g, delta, m, v


def adamw(parts3, w, m, v, *, name):
    c = w.shape[1]
    rows = [(parts3, p) for p in range(parts3.shape[0])] + [w, m, v]
    return rowwise(_f_adamw, rows, [], [((c,), F32)] * 4, name=name)


def _tri_dot(tri, x):
    x1 = x.astype(BF16)
    r1 = x - x1.astype(F32)
    x2 = r1.astype(BF16)
    x3 = (r1 - x2.astype(F32)).astype(BF16)
    dot = functools.partial(jnp.dot, preferred_element_type=F32)
    return dot(tri, x1) + dot(tri, x2) + dot(tri, x3)


def cum_fwd(proj, fb, *, col_block, name, t=256):
    n = proj.shape[0]
    t = _pick(n, t, SUBLANES)

    def body(f_ref, fb_ref, cum_ref, car_ref):
        @pl.when(pl.program_id(0) == 0)
        def _():
            car_ref[...] = jnp.zeros_like(car_ref)

        x = f_ref[...] + fb_ref[...]
        lf = jnp.minimum(x, 0.0) - jnp.log(1.0 + jnp.exp(-jnp.abs(x)))
        r = lax.broadcasted_iota(jnp.int32, (t, t), 0)
        c = lax.broadcasted_iota(jnp.int32, (t, t), 1)
        cs = _tri_dot((c <= r).astype(BF16), lf) + car_ref[0:1, :]
        cum_ref[...] = cs
        car_ref[0:1, :] = cs[t - 1:t, :]

    return pl.pallas_call(
        body, name=name, grid=(n // t,),
        in_specs=[pl.BlockSpec((t, LANES), lambda i: (i, col_block)),
                  pl.BlockSpec((1, LANES), lambda i: (0, 0))],
        out_specs=pl.BlockSpec((t, LANES), lambda i: (i, 0)),
        out_shape=jax.ShapeDtypeStruct((n, LANES), F32),
        scratch_shapes=[pltpu.VMEM((SUBLANES, LANES), F32)],
        compiler_params=_params(("arbitrary",)),
    )(proj, fb)


def cum_bwd(dcum, proj, fb, *, col_block, name, t=256):
    n = proj.shape[0]
    t = _pick(n, t, SUBLANES)
    nb = n // t

    def body(dc_ref, f_ref, fb_ref, df_ref, dfb_ref, car_ref):
        @pl.when(pl.program_id(0) == 0)
        def _():
            car_ref[...] = jnp.zeros_like(car_ref)
            dfb_ref[...] = jnp.zeros_like(dfb_ref)

        r = lax.broadcasted_iota(jnp.int32, (t, t), 0)
        c = lax.broadcasted_iota(jnp.int32, (t, t), 1)
        dl = _tri_dot((c >= r).astype(BF16), dc_ref[...]) + car_ref[0:1, :]
        car_ref[0:1, :] = dl[0:1, :]
        x = f_ref[...] + fb_ref[...]
        df = dl * (1.0 / (1.0 + jnp.exp(x)))
        df_ref[...] = df
        dfb_ref[...] += _colsum(df)

    return pl.pallas_call(
        body, name=name, grid=(nb,),
        in_specs=[pl.BlockSpec((t, LANES), lambda i: (nb - 1 - i, 0)),
                  pl.BlockSpec((t, LANES), lambda i: (nb - 1 - i, col_block)),
                  pl.BlockSpec((1, LANES), lambda i: (0, 0))],
        out_specs=[pl.BlockSpec((t, LANES), lambda i: (nb - 1 - i, 0)),
                   pl.BlockSpec((1, LANES), lambda i: (0, 0))],
        out_shape=[jax.ShapeDtypeStruct((n, LANES), F32), jax.ShapeDtypeStruct((1, LANES), F32)],
        scratch_shapes=[pltpu.VMEM((SUBLANES, LANES), F32)],
        compiler_params=_params(("arbitrary",)),
    )(dcum, proj, fb)


_NT =(((1,), (1,)), ((), ()))
_TN = (((0,), (0,)), ((), ()))


def _causal_keep(t):
    return lax.broadcasted_iota(jnp.int32, (t, t), 1) <= lax.broadcasted_iota(jnp.int32, (t, t), 0)


def attn_fwd(q, k, v, crow, *, name, t=512, hb=8):
    h_n, n, dh = q.shape
    hb = min(hb, h_n)
    t = _pick(n, t)
    nb = n // t

    def body(q_ref, k_ref, v_ref, cr_ref, o_ref, lse_ref, m_sc, l_sc, acc_sc):
        i, j = pl.program_id(1), pl.program_id(2)

        @pl.when(j == 0)
        def _():
            m_sc[...] = jnp.full_like(m_sc, -jnp.inf)
            l_sc[...] = jnp.zeros_like(l_sc)
            acc_sc[...] = jnp.zeros_like(acc_sc)

        def update(diagonal):
            keep = _causal_keep(t) if diagonal else None
            for h in range(hb):
                s = lax.dot_general(q_ref[h], k_ref[h], _NT, preferred_element_type=F32) - cr_ref[h]
                if diagonal:
                    s = jnp.where(keep, s, -jnp.inf)
                m_prev = m_sc[h]
                m_new = jnp.maximum(m_prev, jnp.max(s, axis=-1, keepdims=True))
                p = jnp.exp(s - m_new)
                alpha = jnp.exp(m_prev - m_new)
                l_sc[h] = alpha * l_sc[h] + jnp.sum(p, axis=-1, keepdims=True)
                p_hi = p.astype(BF16)
                p_lo = (p - p_hi.astype(F32)).astype(BF16)
                vv = v_ref[h]
                acc_sc[h] = (alpha * acc_sc[h] + jnp.dot(p_hi, vv, preferred_element_type=F32)
                             + jnp.dot(p_lo, vv, preferred_element_type=F32))
                m_sc[h] = m_new

        @pl.when(j < i)
        def _():
            update(False)

        @pl.when(j == i)
        def _():
            update(True)

        @pl.when(j == nb - 1)
        def _():
            o_ref[...] = acc_sc[...] / l_sc[...]
            lse_ref[...] = m_sc[...] + jnp.log(l_sc[...])

    qspec = pl.BlockSpec((hb, t, dh), lambda g, i, j: (g, i, 0))
    kspec = pl.BlockSpec((hb, t, dh), lambda g, i, j: (g, jnp.minimum(j, i), 0))
    colspec = pl.BlockSpec((hb, t, 1), lambda g, i, j: (g, i, 0))
    return pl.pallas_call(
        body, name=name, grid=(h_n // hb, nb, nb),
        in_specs=[qspec, kspec, kspec, pl.BlockSpec((hb, 1, t), lambda g, i, j: (g, 0, jnp.minimum(j, i)))],
        out_specs=[qspec, colspec],
        out_shape=[jax.ShapeDtypeStruct((h_n, n, dh), F32), jax.ShapeDtypeStruct((h_n, n, 1), F32)],
        scratch_shapes=[pltpu.VMEM((hb, t, 1), F32), pltpu.VMEM((hb, t, 1), F32), pltpu.VMEM((hb, t, dh), F32)],
        compiler_params=_params(("parallel", "parallel", "arbitrary")),
    )(q, k, v, crow)


def _f_rowdot(a, b):
    return jnp.sum(a.astype(F32) * b, axis=-1, keepdims=True)


def attn_bwd(q, k, v, do, o, lse, crow, *, scale, tag, t=512, hb=2):
    h_n, n, dh = q.shape
    hb = min(hb, h_n)
    t = _pick(n, t)
    nb = n // t
    dob = do.astype(BF16)
    delta = rowwise(_f_rowdot, [dob.reshape(h_n * n, dh), o.reshape(h_n * n, dh)], [], [((1,), F32)],
                    name=f"attn_delta_{tag}")[0].reshape(h_n, n, 1)

    def body(q_ref, k_ref, v_ref, do_ref, lse_ref, dl_ref, cr_ref,
             dq_ref, dk_ref, dv_ref, dcr_ref, dk_acc, dv_acc, dcr_acc):
        j, i = pl.program_id(1), pl.program_id(2)

        @pl.when((j == 0) & (i == 0))
        def _():
            dq_ref[...] = jnp.zeros_like(dq_ref)

        @pl.when(i == 0)
        def _():
            dk_acc[...] = jnp.zeros_like(dk_acc)
            dv_acc[...] = jnp.zeros_like(dv_acc)
            dcr_acc[...] = jnp.zeros_like(dcr_acc)

        def update(diagonal):
            keep = _causal_keep(t) if diagonal else None
            r0 = pl.multiple_of(i * t, t)
            for h in range(hb):
                qv, kv, vv, dov = q_ref[h], k_ref[h], v_ref[h], do_ref[h]
                s = lax.dot_general(qv, kv, _NT, preferred_element_type=F32) - cr_ref[h]
                if diagonal:
                    s = jnp.where(keep, s, -jnp.inf)
                p = jnp.exp(s - lse_ref[h])
                dv_acc[h] += lax.dot_general(p.astype(BF16), dov, _TN, preferred_element_type=F32)
                dp = lax.dot_general(dov, vv, _NT, preferred_element_type=F32)
                ds = p * (dp - dl_ref[h])
                dcr_acc[h] -= _colsum(ds)
                dsb = ds.astype(BF16)
                dk_acc[h] += lax.dot_general(dsb, qv, _TN, preferred_element_type=F32)
                dq_ref[h, pl.ds(r0, t), :] += jnp.dot(dsb, kv, preferred_element_type=F32) * scale

        @pl.when(i > j)
        def _():
            update(False)

        @pl.when(i == j)
        def _():
            update(True)

        @pl.when(i == nb - 1)
        def _():
            dk_ref[...] = dk_acc[...]
            dv_ref[...] = dv_acc[...]
            dcr_ref[...] = dcr_acc[...]

    qspec = pl.BlockSpec((hb, t, dh), lambda g, j, i: (g, jnp.maximum(i, j), 0))
    colspec = pl.BlockSpec((hb, t, 1), lambda g, j, i: (g, jnp.maximum(i, j), 0))
    kspec = pl.BlockSpec((hb, t, dh), lambda g, j, i: (g, j, 0))
    rowspec = pl.BlockSpec((hb, 1, t), lambda g, j, i: (g, 0, j))
    return pl.pallas_call(
        body, name=f"attn_bwd_{tag}", grid=(h_n // hb, nb, nb),
        in_specs=[qspec, kspec, kspec, qspec, colspec, colspec, rowspec],
        out_specs=[pl.BlockSpec((hb, n, dh), lambda g, j, i: (g, 0, 0)), kspec, kspec, rowspec],
        out_shape=[jax.ShapeDtypeStruct((h_n, n, dh), F32)] * 3 + [jax.ShapeDtypeStruct((h_n, 1, n), F32)],
        scratch_shapes=[pltpu.VMEM((hb, t, dh), F32), pltpu.VMEM((hb, t, dh), F32), pltpu.VMEM((hb, 1, t), F32)],
        compiler_params=_params(("parallel", "arbitrary", "arbitrary")),
    )(q, k, v, dob, lse, delta, crow)


SCAN_STEPS = (1, 2, 4)


def ssm_scan(x, tab, *, reverse, name, s_prev=None, tt=512):
    n, width = x.shape
    nc, _, hw = tab.shape
    cw = 2 * hw
    assert width == nc * cw
    tt = _pick(n, tt, SUBLANES)
    nt = n // tt
    ng = tt // SUBLANES
    with_grad = s_prev is not None

    def body(*refs):
        if with_grad:
            x_ref, s_ref, tab_ref, o_ref, g_ref, car_ref = refs
        else:
            x_ref, tab_ref, o_ref, car_ref = refs

        @pl.when(pl.program_id(1) == 0)
        def _():
            car_ref[...] = jnp.zeros_like(car_ref)
            if with_grad:
                g_ref[...] = jnp.zeros_like(g_ref)

        q_re, q_im = tab_ref[0:8, :], tab_ref[8:16, :]
        p_re = [tab_ref[16 + i:17 + i, :] for i in range(3)]
        p_im = [tab_ref[24 + i:25 + i, :] for i in range(3)]
        row = lax.broadcasted_iota(jnp.int32, (SUBLANES, hw), 0)

        def group(gi, carry):
            c_re, c_im = carry
            g = (ng - 1 - gi) if reverse else gi
            r0 = pl.multiple_of(g * SUBLANES, SUBLANES)
            xr = x_ref[pl.ds(r0, SUBLANES), 0:hw]
            xi = x_ref[pl.ds(r0, SUBLANES), hw:cw]
            for i, d in enumerate(SCAN_STEPS):
                if reverse:
                    shift, keep = SUBLANES - d, row < SUBLANES - d
                else:
                    shift, keep = d, row >= d
                sr = jnp.where(keep, pltpu.roll(xr, shift, 0), 0.0)
                si = jnp.where(keep, pltpu.roll(xi, shift, 0), 0.0)
                xr, xi = (xr + p_re[i] * sr - p_im[i] * si,
                          xi + p_re[i] * si + p_im[i] * sr)
            xr, xi = (xr + q_re * c_re - q_im * c_im,
                      xi + q_re * c_im + q_im * c_re)
            o_ref[pl.ds(r0, SUBLANES), 0:hw] = xr
            o_ref[pl.ds(r0, SUBLANES), hw:cw] = xi
            if with_grad:
                nr = jnp.where(row < SUBLANES - 1, pltpu.roll(xr, SUBLANES - 1, 0), c_re)
                ni = jnp.where(row < SUBLANES - 1, pltpu.roll(xi, SUBLANES - 1, 0), c_im)
                sr = s_ref[pl.ds(r0, SUBLANES), 0:hw]
                si = s_ref[pl.ds(r0, SUBLANES), hw:cw]
                g_ref[:, 0:hw] += nr * sr + ni * si
                g_ref[:, hw:cw] += ni * sr - nr * si
            if reverse:
                return xr[0:1, :], xi[0:1, :]
            return xr[SUBLANES - 1:SUBLANES, :], xi[SUBLANES - 1:SUBLANES, :]

        c_re, c_im = lax.fori_loop(0, ng, group, (car_ref[0:1, 0:hw], car_ref[0:1, hw:cw]),
                                   unroll=min(ng, 4))
        car_ref[0:1, 0:hw] = c_re
        car_ref[0:1, hw:cw] = c_im

    if reverse:
        xspec = pl.BlockSpec((tt, cw), lambda c, t: (nt - 1 - t, c))
    else:
        xspec = pl.BlockSpec((tt, cw), lambda c, t: (t, c))
    tspec = pl.BlockSpec((None, 32, hw), lambda c, t: (c, 0, 0))
    in_specs = [xspec, xspec, tspec] if with_grad else [xspec, tspec]
    out_specs = [xspec]
    out_shape = [jax.ShapeDtypeStruct((n, width), F32)]
    if with_grad:
        out_specs.append(pl.BlockSpec((None, SUBLANES, cw), lambda c, t: (c, 0, 0)))
        out_shape.append(jax.ShapeDtypeStruct((nc, SUBLANES, cw), F32))
    operands = (x, s_prev, tab) if with_grad else (x, tab)
    return pl.pallas_call(
        body, name=name, grid=(nc, nt),
        in_specs=in_specs, out_specs=out_specs, out_shape=out_shape,
        scratch_shapes=[pltpu.VMEM((SUBLANES, cw), F32)],
        compiler_params=_params(("parallel", "arbitrary")),
    )(*operands)


def _slot(pos):
    return 4 * pos[0] + 2 * pos[1] + pos[2]


def all_gather(arrs, *, name):
    n = len(arrs)

    def body(*refs):
        ins, outs = refs[:n], refs[n:2 * n]
        send_sems, recv_sems, local_sems = refs[2 * n:]
        x, y, c = lax.axis_index("x"), lax.axis_index("y"), lax.axis_index("c")
        me, sibling = (x, y, c), (x, y, 1 - c)
        chips = [(1 - x, y), (x, 1 - y), (1 - x, 1 - y)]

        def copy(t, k, block, to, src=None):
            dst = outs[t].at[_slot(block)]
            return pltpu.make_async_remote_copy(
                src_ref=dst if src is None else src, dst_ref=dst,
                send_sem=send_sems.at[7 * t + k], recv_sem=recv_sems.at[7 * t + k],
                device_id=to, device_id_type=MESH)

        mine = [pltpu.make_async_copy(ins[t], outs[t].at[_slot(me)], local_sems.at[t]) for t in range(n)]
        for cp in mine:
            cp.start()
        first = []
        for t in range(n):
            first.append(copy(t, 0, me, sibling, src=ins[t]))
            first += [copy(t, 1 + j, me, (*chip, c), src=ins[t]) for j, chip in enumerate(chips)]
        for cp in first:
            cp.start()
        passed = []
        for t in range(n):
            for j, chip in enumerate(chips):
                copy(t, 1 + j, (*chip, c), me).wait_recv()
                cp = copy(t, 4 + j, (*chip, c), sibling)
                cp.start()
                passed.append(cp)
        for t in range(n):
            copy(t, 0, sibling, me).wait_recv()
            for j, chip in enumerate(chips):
                copy(t, 4 + j, (*chip, 1 - c), me).wait_recv()
        for cp in first + passed:
            cp.wait_send()
        for cp in mine:
            cp.wait()

    any_spec = pl.BlockSpec(memory_space=pl.ANY)
    return pl.pallas_call(
        body, name=name,
        in_specs=[any_spec] * n, out_specs=[any_spec] * n,
        out_shape=[jax.ShapeDtypeStruct((N_DEV,) + a.shape, a.dtype) for a in arrs],
        scratch_shapes=[pltpu.SemaphoreType.DMA((7 * n,)), pltpu.SemaphoreType.DMA((7 * n,)),
                        pltpu.SemaphoreType.DMA((n,))],
    )(*arrs)


def all_to_all(arrs, *, name):
    n = len(arrs)

    def body(*refs):
        ins, outs = refs[:n], refs[n:2 * n]
        send_sems, recv_sems, local_sems = refs[2 * n:]
        x, y, c = lax.axis_index("x"), lax.axis_index("y"), lax.axis_index("c")
        me = (x, y, c)
        peers = []
        for k in range(1, N_DEV):
            flip = ((k >> 2) & 1, (k >> 1) & 1, k & 1)
            peers.append(tuple(1 - p if f else p for p, f in zip(me, flip)))

        mine = [pltpu.make_async_copy(ins[t].at[_slot(me)], outs[t].at[_slot(me)], local_sems.at[t])
                for t in range(n)]
        for cp in mine:
            cp.start()
        copies = []
        for t in range(n):
            for k, peer in enumerate(peers):
                copies.append(pltpu.make_async_remote_copy(
                    src_ref=ins[t].at[_slot(peer)], dst_ref=outs[t].at[_slot(me)],
                    send_sem=send_sems.at[7 * t + k], recv_sem=recv_sems.at[7 * t + k],
                    device_id=peer, device_id_type=MESH))
        for cp in copies:
            cp.start()
        for t in range(n):
            for k, peer in enumerate(peers):
                pltpu.make_async_remote_copy(
                    src_ref=ins[t].at[_slot(peer)], dst_ref=outs[t].at[_slot(peer)],
                    send_sem=send_sems.at[7 * t + k], recv_sem=recv_sems.at[7 * t + k],
                    device_id=peer, device_id_type=MESH).wait()
        for cp in mine:
            cp.wait()

    any_spec = pl.BlockSpec(memory_space=pl.ANY)
    return pl.pallas_call(
        body, name=name,
        in_specs=[any_spec] * n, out_specs=[any_spec] * n,
        out_shape=[jax.ShapeDtypeStruct(a.shape, a.dtype) for a in arrs],
        scratch_shapes=[pltpu.SemaphoreType.DMA((7 * n,)), pltpu.SemaphoreType.DMA((7 * n,)),
                        pltpu.SemaphoreType.DMA((n,))],
    )(*arrs)


def _discretise(a_re, a_im, log_dt, b_re, b_im):
    ar = jnp.minimum(a_re, -1e-4)
    dt = jnp.exp(log_dt)[:, None]
    e, ph = ar * dt, a_im * dt
    mag = jnp.exp(e)
    lr, li = mag * jnp.cos(ph), mag * jnp.sin(ph)
    den = ar * ar + a_im * a_im
    nr, ni = lr - 1.0, li
    cr = (nr * ar + ni * a_im) / den
    ci = (ni * ar - nr * a_im) / den
    bb_re = cr[..., None] * b_re - ci[..., None] * b_im
    bb_im = cr[..., None] * b_im + ci[..., None] * b_re
    return e, ph, bb_re, bb_im


def _lam_pow(e, ph, k, conj):
    mag = jnp.exp(k * e)
    return mag * jnp.cos(k * ph), (-1.0 if conj else 1.0) * mag * jnp.sin(k * ph)


def _scan_table(e, ph, nc, reverse):
    hw = e.size // nc
    e, ph = e.reshape(nc, 1, hw), ph.reshape(nc, 1, hw)
    j = jnp.arange(SUBLANES, dtype=F32).reshape(1, SUBLANES, 1)
    kq = (SUBLANES - j) if reverse else (j + 1.0)
    q_re, q_im = _lam_pow(e, ph, kq, reverse)
    kp = jnp.array(SCAN_STEPS + (0,) * 5, F32).reshape(1, SUBLANES, 1)
    p_re, p_im = _lam_pow(e, ph, kp, reverse)
    return jnp.concatenate([q_re, q_im, p_re, p_im], axis=1)


def _blockdiag(m, nc):
    g, a, b = m.shape
    gc = g // nc
    m = m.reshape(nc, gc, a, b)
    eye = jnp.eye(gc, dtype=m.dtype)
    return jnp.einsum("cgab,gh->cgahb", m, eye).reshape(nc, gc * a, gc * b)


def _blockdiag_take(m, g):
    nc = m.shape[0]
    gc = g // nc
    a, b = m.shape[1] // gc, m.shape[2] // gc
    m = m.reshape(nc, gc, a, gc, b)
    eye = jnp.eye(gc, dtype=m.dtype)
    return jnp.einsum("cgahb,gh->cgab", m, eye).reshape(g, a, b)


def kernel(x, c, mod_w, mod_b, norm_pre, norm_post, ffn_w_in, ffn_w_out, mix_w_in, forget_b, ssm_a_re, ssm_a_im, ssm_log_dt, ssm_b_re, ssm_b_im, ssm_c_re, ssm_c_im, ssm_d, glu_w, attn_w_out, mix_w_out, loss_target, m_mod_w, m_mod_b, m_norm_pre, m_norm_post, m_ffn_w_in, m_ffn_w_out, m_mix_w_in, m_forget_b, m_ssm_a_re, m_ssm_a_im, m_ssm_log_dt, m_ssm_b_re, m_ssm_b_im, m_ssm_c_re, m_ssm_c_im, m_ssm_d, m_glu_w, m_attn_w_out, m_mix_w_out, v_mod_w, v_mod_b, v_norm_pre, v_norm_post, v_ffn_w_in, v_ffn_w_out, v_mix_w_in, v_forget_b, v_ssm_a_re, v_ssm_a_im, v_ssm_log_dt, v_ssm_b_re, v_ssm_b_im, v_ssm_c_re, v_ssm_c_im, v_ssm_d, v_glu_w, v_attn_w_out, v_mix_w_out):
    names = ["mod_w", "mod_b", "norm_pre", "norm_post", "ffn_w_in", "ffn_w_out", "mix_w_in", "forget_b",
             "ssm_a_re", "ssm_a_im", "ssm_log_dt", "ssm_b_re", "ssm_b_im", "ssm_c_re", "ssm_c_im", "ssm_d",
             "glu_w", "attn_w_out", "mix_w_out"]
    w_in = dict(zip(names, [mod_w, mod_b, norm_pre, norm_post, ffn_w_in, ffn_w_out, mix_w_in, forget_b,
                            ssm_a_re, ssm_a_im, ssm_log_dt, ssm_b_re, ssm_b_im, ssm_c_re, ssm_c_im, ssm_d,
                            glu_w, attn_w_out, mix_w_out]))
    m_in = dict(zip(names, [m_mod_w, m_mod_b, m_norm_pre, m_norm_post, m_ffn_w_in, m_ffn_w_out, m_mix_w_in,
                            m_forget_b, m_ssm_a_re, m_ssm_a_im, m_ssm_log_dt, m_ssm_b_re, m_ssm_b_im,
                            m_ssm_c_re, m_ssm_c_im, m_ssm_d, m_glu_w, m_attn_w_out, m_mix_w_out]))
    v_in = dict(zip(names, [v_mod_w, v_mod_b, v_norm_pre, v_norm_post, v_ffn_w_in, v_ffn_w_out, v_mix_w_in,
                            v_forget_b, v_ssm_a_re, v_ssm_a_im, v_ssm_log_dt, v_ssm_b_re, v_ssm_b_im,
                            v_ssm_c_re, v_ssm_c_im, v_ssm_d, v_glu_w, v_attn_w_out, v_mix_w_out]))

    depth = mod_w.shape[0]
    n_tok, d = x.shape[1], x.shape[2]
    ff = ffn_w_out.shape[2] * N_DEV
    heads = forget_b.shape[1]
    sw = ssm_d.shape[1]
    g_n, p_n, n_n = ssm_b_re.shape[1:]
    aw = attn_w_out.shape[1]
    dh = aw // heads
    iw = mix_w_in.shape[2] * N_DEV
    nc = sw // LANES
    hw = g_n * p_n // nc
    mod_cols = mod_w.shape[2]
    scale = dh ** -0.5
    assert iw == sw + 3 * aw + heads + 2 * d and heads <= LANES
    assert math.log2(scale).is_integer(), "q is pre-scaled in bf16: exact only for a power of two"
    off_u, off_q, off_f = 2 * d, 2 * d + sw, 2 * d + sw + 3 * aw
    iwp = off_f + LANES
    assert off_u % sw == 0 and off_q % aw == 0 and off_f % LANES == 0

    me = 4 * lax.axis_index("x") + 2 * lax.axis_index("y") + lax.axis_index("c")
    x2 = x.reshape(n_tok, d)
    tgt = loss_target.reshape(n_tok, d)

    silu_c = rowwise(_f_silu, [c], [], [((d,), F32)], name="silu_c")[0]
    big = ["ffn_w_in", "ffn_w_out", "mix_w_in", "glu_w", "attn_w_out", "mix_w_out"]
    gathered = all_gather(
        [silu_c, norm_pre.reshape(-1, norm_pre.shape[-1]), norm_post.reshape(-1, norm_post.shape[-1])]
        + [w_in[k].astype(BF16) for k in big], name="gather_weights")
    sc_all = gathered[0].reshape(N_DEV, d)
    gpre = jnp.moveaxis(gathered[1].reshape(N_DEV, depth, 3, -1), 0, 2).reshape(depth, 3, d)
    gpost = jnp.moveaxis(gathered[2].reshape(N_DEV, depth, 3, -1), 0, 2).reshape(depth, 3, d)
    gw = dict(zip(big, gathered[3:]))
    w_ffn_in = jnp.moveaxis(gw["ffn_w_in"], 0, 3).reshape(depth, 2, d, 2 * ff)
    w_ffn_out = jnp.moveaxis(gw["ffn_w_out"], 0, 2).reshape(depth, 2, ff, d)
    w_mix_in = jnp.moveaxis(gw["mix_w_in"], 0, 2).reshape(depth, d, iw)
    w_glu = jnp.moveaxis(gw["glu_w"], 0, 2).reshape(depth, sw, 2 * d)
    w_ao = jnp.moveaxis(gw["attn_w_out"], 0, 2).reshape(depth, aw, d)
    w_mo = gw["mix_w_out"].transpose(1, 0, 2, 3).reshape(depth, d, d)
    cut = [0, sw, sw + aw, sw + 2 * aw, sw + 3 * aw, sw + 3 * aw + heads, sw + 3 * aw + heads + d, iw]
    seg = lambda a, i: a[..., cut[i]:cut[i + 1]]
    w_mix_p = jnp.concatenate(
        [seg(w_mix_in, 5), seg(w_mix_in, 6), seg(w_mix_in, 0), seg(w_mix_in, 1), seg(w_mix_in, 2),
         seg(w_mix_in, 3), jnp.pad(seg(w_mix_in, 4), ((0, 0), (0, 0), (0, LANES - heads)))], axis=-1)

    sc_pad = jnp.pad(sc_all, ((0, LANES - N_DEV), (0, 0)))
    mod_part = jnp.stack([mm(sc_pad, mod_w[l], name=f"mod_fwd{l}")[:N_DEV] for l in range(depth)], axis=1)
    mod_part = mod_part + lax.dynamic_slice_in_dim(mod_b, me * mod_cols, mod_cols, axis=1)[None]
    mod_all = all_gather([mod_part], name="gather_mod")[0]
    mod_own = lax.dynamic_index_in_dim(mod_all, me, axis=1, keepdims=False)
    mod_own = mod_own.transpose(1, 0, 2).reshape(depth, 3, 3, d)
    res_w = (FFN_RES, 1.0, FFN_RES)

    def vec_a(l, i):
        return (gpre[l, i] * (1.0 + mod_own[l, i, 1])).reshape(1, d)

    def vec_sh(l, i):
        return mod_own[l, i, 0].reshape(1, d)

    def vec_b(l, i):
        return (res_w[i] * mod_own[l, i, 2] * gpost[l, i]).reshape(1, d)

    ssm = []
    for l in range(depth):
        (e, ph, bb_re, bb_im), disc_vjp = jax.vjp(_discretise,ssm_a_re[l], ssm_a_im[l], ssm_log_dt[l],
                                                  ssm_b_re[l], ssm_b_im[l])
        b_mat = jnp.concatenate([_blockdiag(bb_re.transpose(0, 2, 1), nc),
                                 _blockdiag(bb_im.transpose(0, 2, 1), nc)], axis=2)
        c_mat = jnp.concatenate([_blockdiag(ssm_c_re[l].transpose(0, 2, 1), nc),
                                 _blockdiag(-ssm_c_im[l].transpose(0, 2, 1), nc)], axis=1)
        ssm.append(dict(e=e, ph=ph, vjp=disc_vjp, b=b_mat.astype(BF16), c=c_mat.astype(BF16),
                        bt=b_mat.transpose(0, 2, 1).astype(BF16), ct=c_mat.transpose(0, 2, 1).astype(BF16),
                        tab_f=_scan_table(e, ph, nc, False), tab_r=_scan_table(e, ph, nc, True),
                        dvec=ssm_d[l].reshape(1, sw)))

    fb_pad = jnp.pad(forget_b, ((0, 0), (0, LANES - heads)))

    def heads_first(a):
        return a.reshape(n_tok, heads, dh).transpose(1, 0, 2)

    def heads_last(a):
        return a.transpose(1, 0, 2).reshape(n_tok, heads * dh)

    def ffn_fwd(xin, l, i, j, tag):
        h = rowwise(_f_pre, [xin], [vec_a(l, i), vec_sh(l, i)], [((d,), BF16)], name=f"pre_{tag}")[0]
        a = mm(h, w_ffn_in[l, j], name=f"ffn_in_{tag}", out_dtype=BF16)
        m = rowwise(_f_swiglu, [(a, ff, 0), (a, ff, 1)], [], [((ff,), BF16)], name=f"swiglu_{tag}")[0]
        y = mm(m, w_ffn_out[l, j], name=f"ffn_out_{tag}", tk=1408)
        xout = rowwise(_f_post_add, [xin, y], [vec_b(l, i)], [((d,), F32)], name=f"post_{tag}")[0]
        return xout, dict(x=xin, h=h, a=a, m=m, y=y)

    def mixer_fwd(xin, l, tag):
        s5 = ssm[l]
        h = rowwise(_f_pre, [xin], [vec_a(l, 1), vec_sh(l, 1)], [((d,), BF16)], name=f"pre_{tag}")[0]
        proj = mm(h, w_mix_p[l], name=f"mix_in_{tag}", tn=384)
        bu = mm_blockdiag(proj, s5["b"], a_cb0=off_u // LANES, name=f"ssm_bu_{tag}")
        st = ssm_scan(bu, s5["tab_f"], reverse=False, name=f"ssm_scan_{tag}")[0]
        y0 = mm_blockdiag(st, s5["c"], name=f"ssm_y_{tag}")
        ge, ys = rowwise(_f_gelu_in, [y0, (proj, sw, off_u // sw)], [s5["dvec"]],
                         [((sw,), BF16), ((sw,), F32)], name=f"gelu_{tag}")
        z = mm(ge, w_glu[l], name=f"glu_{tag}")
        cum = cum_fwd(proj, fb_pad[l:l + 1], col_block=off_f // LANES, name=f"cum_{tag}")
        crow = cum[:, :heads].T[:, None, :]
        q, k, v = [heads_first(proj[:, off_q + i * aw:off_q + (i + 1) * aw] * sc_).astype(BF16)
                   for i, sc_ in enumerate((scale, 1.0, 1.0))]
        o, lse = attn_fwd(q, k, v, crow, name=f"attn_{tag}")
        attn = heads_last(o).astype(BF16)
        yb = mm(attn, w_ao[l], name=f"attn_out_{tag}")
        mg = rowwise(_f_merge, [(z, d, 0), (z, d, 1), yb, (proj, d, 0), (proj, d, 1)], [],
                     [((d,), BF16)], name=f"merge_{tag}")[0]
        y = mm(mg, w_mo[l], name=f"mix_out_{tag}")
        xout = rowwise(_f_post_add, [xin, y], [vec_b(l, 1)], [((d,), F32)], name=f"post_{tag}")[0]
        saved = dict(x=xin, h=h, proj=proj, st=st, ys=ys, ge=ge, z=z, q=q, k=k, v=v, o=o, lse=lse,
                     crow=crow, attn=attn, yb=yb, mg=mg, y=y)
        return xout, saved

    saved = []
    xc = x2
    for l in range(depth):
        xc, s0 = ffn_fwd(xc, l, 0, 0, f"l{l}a")
        xc, s1 = mixer_fwd(xc, l, f"l{l}m")
        xc, s2 = ffn_fwd(xc, l, 2, 1, f"l{l}b")
        saved.append((s0, s1, s2))

    def f_loss(xf, t):
        e_ = xf - t
        return e_ * (1.0 / d), _colsum(e_ * e_)

    dx, sq = rowwise(f_loss, [xc, tgt], [], [((d,), F32)], [d], name="loss_head")
    loss_part = 0.5 * jnp.sum(sq) / d

    grads = {k: [None] * depth for k in big}
    small_g = [dict() for _ in range(depth)]
    dmod = [[None] * 3 for _ in range(depth)]
    dgpre = [[None] * 3 for _ in range(depth)]
    dgpost = [[None] * 3 for _ in range(depth)]

    def norm_grads(l, i, d_a, d_sh, d_bv):
        d_a, d_sh, d_bv = d_a.reshape(d), d_sh.reshape(d), d_bv.reshape(d)
        dmod[l][i] = jnp.stack([d_sh, d_a * gpre[l, i], res_w[i] * gpost[l, i] * d_bv])
        dgpre[l][i] = d_a * (1.0 + mod_own[l, i, 1])
        dgpost[l][i] = res_w[i] * mod_own[l, i, 2] * d_bv

    def ffn_bwd(dxo, sv, l, i, j, tag):
        dy, d_bv = rowwise(_f_post_bwd, [dxo, sv["y"]], [vec_b(l, i)], [((d,), BF16)], [d],
                           name=f"post_bwd_{tag}")
        dm = mm(dy, w_ffn_out[l, j], trans_b=True, name=f"ffn_out_dx_{tag}", out_dtype=BF16, tn=1408)
        g_out = mm(sv["m"], dy, trans_a=True, name=f"ffn_out_dw_{tag}", out_dtype=BF16, tm=1408, tn=1024,
                   tk=512)
        da = rowwise(_f_swiglu_bwd, [(sv["a"], ff, 0), (sv["a"], ff, 1), dm], [], [((ff, ff), BF16)],
                     name=f"swiglu_bwd_{tag}")[0]
        g_in = mm(sv["h"], da, trans_a=True, name=f"ffn_in_dw_{tag}", out_dtype=BF16, tk=512)
        dh_ = mm(da, w_ffn_in[l, j], trans_b=True, name=f"ffn_in_dx_{tag}", tk=1408)
        dxn, d_a, d_sh = rowwise(_f_pre_bwd, [dxo, dh_, sv["x"]], [vec_a(l, i)], [((d,), F32)], [d, d],
                                 name=f"pre_bwd_{tag}")
        norm_grads(l, i, d_a, d_sh, d_bv)
        return dxn, g_in, g_out

    def mixer_bwd(dxo, sv, l, tag):
        s5 = ssm[l]
        proj = sv["proj"]
        dy, d_bv = rowwise(_f_post_bwd, [dxo, sv["y"]], [vec_b(l, 1)], [((d,), BF16)], [d],
                           name=f"post_bwd_{tag}")
        dmg = mm(dy, w_mo[l], trans_b=True, name=f"mix_out_dx_{tag}")
        g_mo = mm(sv["mg"], dy, trans_a=True, name=f"mix_out_dw_{tag}", out_dtype=BF16, tk=512)
        dz, dyb, dgab = rowwise(
            _f_merge_bwd, [dmg, (sv["z"], d, 0), (sv["z"], d, 1), sv["yb"], (proj, d, 0), (proj, d, 1)], [],
            [((d, d), BF16), ((d,), BF16), ((d, d), BF16)], name=f"merge_bwd_{tag}")
        dge = mm(dz, w_glu[l], trans_b=True, name=f"glu_dx_{tag}")
        g_glu = mm(sv["ge"], dz, trans_a=True, name=f"glu_dw_{tag}", out_dtype=BF16, tk=512)
        dys, d_dvec = rowwise(_f_gelu_bwd, [dge, sv["ys"], (proj, sw, off_u // sw)], [], [((sw,), BF16)],
                              [sw], name=f"gelu_bwd_{tag}")
        gadj = mm_blockdiag(dys, s5["ct"], name=f"ssm_dy_{tag}")
        adj, dlam8 = ssm_scan(gadj, s5["tab_r"], reverse=True, s_prev=sv["st"], name=f"ssm_scan_bwd_{tag}")
        du0 = mm_blockdiag(adj, s5["bt"], name=f"ssm_du_{tag}")
        d_bmat = mm_blockdiag_tn(proj, adj, g_n=nc, ka=LANES, kb=2 * hw, a_cb0=off_u // LANES,
                                 name=f"ssm_db_{tag}")
        d_cmat = mm_blockdiag_tn(sv["st"], dys, g_n=nc, ka=2 * hw, kb=LANES, name=f"ssm_dc_{tag}")
        du = rowwise(_f_du_fin, [du0, dys], [s5["dvec"]], [((sw,), BF16)], name=f"ssm_du_fin_{tag}")[0]
        dlam = jnp.sum(dlam8, axis=1)
        dlam_re, dlam_im = dlam[:, :hw].reshape(g_n, p_n), dlam[:, hw:].reshape(g_n, p_n)
        dbb_re = _blockdiag_take(d_bmat[:, :, :hw], g_n).transpose(0, 2, 1)
        dbb_im = _blockdiag_take(d_bmat[:, :, hw:], g_n).transpose(0, 2, 1)
        mag = jnp.exp(s5["e"])
        lr, li = mag * jnp.cos(s5["ph"]), mag * jnp.sin(s5["ph"])
        d_e = dlam_re * lr + dlam_im * li
        d_ph = -dlam_re * li + dlam_im * lr
        da_re, da_im, dlog_dt, db_re, db_im = s5["vjp"]((d_e, d_ph, dbb_re, dbb_im))
        small_g[l].update(
            ssm_a_re=da_re, ssm_a_im=da_im, ssm_log_dt=dlog_dt, ssm_b_re=db_re, ssm_b_im=db_im,
            ssm_c_re=_blockdiag_take(d_cmat[:, :hw, :], g_n).transpose(0, 2, 1),
            ssm_c_im=-_blockdiag_take(d_cmat[:, hw:, :], g_n).transpose(0, 2, 1),
            ssm_d=d_dvec.reshape(sw))
        dattn = mm(dyb, w_ao[l], trans_b=True, name=f"attn_out_dx_{tag}")
        g_ao = mm(sv["attn"], dyb, trans_a=True, name=f"attn_out_dw_{tag}", out_dtype=BF16, tk=512)
        dq, dk, dv, dcr = attn_bwd(sv["q"], sv["k"], sv["v"], heads_first(dattn), sv["o"], sv["lse"],
                                   sv["crow"], scale=scale, tag=tag)
        dcum = jnp.pad(dcr[:, 0, :].T, ((0, 0), (0, LANES - heads)))
        df, dfb = cum_bwd(dcum, proj, fb_pad[l:l + 1], col_block=off_f // LANES, name=f"cum_bwd_{tag}")
        small_g[l]["forget_b"] = dfb[0, :heads]
        dproj = jnp.concatenate([dgab, du, heads_last(dq).astype(BF16), heads_last(dk).astype(BF16),
                                 heads_last(dv).astype(BF16), df.astype(BF16)], axis=1)
        g_mi = mm(sv["h"], dproj, trans_a=True, name=f"mix_in_dw_{tag}", out_dtype=BF16, tn=384, tk=512)
        dh_ = mm(dproj, w_mix_p[l], trans_b=True, name=f"mix_in_dx_{tag}", tk=1408)
        dxn, d_a, d_sh = rowwise(_f_pre_bwd, [dxo, dh_, sv["x"]], [vec_a(l, 1)], [((d,), F32)], [d, d],
                                 name=f"pre_bwd_{tag}")
        norm_grads(l, 1, d_a, d_sh, d_bv)
        g_mi = jnp.concatenate([g_mi[:, off_u:off_f + heads], g_mi[:, :off_u]], axis=1)
        return dxn, g_mi, g_glu, g_ao, g_mo

    g_ffn_in = [[None, None] for _ in range(depth)]
    g_ffn_out = [[None, None] for _ in range(depth)]
    for l in reversed(range(depth)):
        s0, s1, s2 = saved[l]
        dx, g_ffn_in[l][1], g_ffn_out[l][1] = ffn_bwd(dx, s2, l, 2, 1, f"l{l}b")
        dx, grads["mix_w_in"][l], grads["glu_w"][l], grads["attn_w_out"][l], grads["mix_w_out"][l] = \
            mixer_bwd(dx, s1, l, f"l{l}m")
        dx, g_ffn_in[l][0], g_ffn_out[l][0] = ffn_bwd(dx, s0, l, 0, 0, f"l{l}a")
    grad_x = dx.reshape(x.shape)

    small_names = ["forget_b", "ssm_a_re", "ssm_a_im", "ssm_log_dt", "ssm_b_re", "ssm_b_im", "ssm_c_re",
                   "ssm_c_im", "ssm_d"]
    pieces = [loss_part.reshape(1), jnp.stack([jnp.stack(dmod[l]) for l in range(depth)]).reshape(-1),
              jnp.stack([jnp.stack(dgpre[l]) for l in range(depth)]).reshape(-1),
              jnp.stack([jnp.stack(dgpost[l]) for l in range(depth)]).reshape(-1)]
    pieces += [jnp.stack([small_g[l][k] for l in range(depth)]).reshape(-1) for k in small_names]
    sizes = [p.size for p in pieces]
    chunk = SUBLANES * 1024
    total = -(-sum(sizes) // chunk) * chunk
    pack = jnp.pad(jnp.concatenate(pieces), (0, total - sum(sizes))).reshape(total // 1024, 1024)
    pack_all = all_gather([pack], name="gather_small_grads")[0]
    pack_sum = rowwise(_f_sum_parts, [(pack_all, p) for p in range(N_DEV)], [], [((1024,), F32)],
                       name="sum_small_grads")[0].reshape(-1)
    offs = [0]
    for s_ in sizes:
        offs.append(offs[-1] + s_)
    take = lambda i: pack_sum[offs[i]:offs[i + 1]]
    loss = take(0).reshape(())
    g_small = {"mod_b": take(1).reshape(mod_b.shape)}
    g_pre_full, g_post_full = take(2).reshape(depth, 3, d), take(3).reshape(depth, 3, d)
    shard = norm_pre.shape[-1]
    g_small["norm_pre"] = lax.dynamic_slice_in_dim(g_pre_full, me * shard, shard, axis=2)
    g_small["norm_post"] = lax.dynamic_slice_in_dim(g_post_full, me * shard, shard, axis=2)
    for i, k in enumerate(small_names):
        g_small[k] = take(4 + i).reshape(w_in[k].shape)

    dmod_all = pack_all.reshape(N_DEV, -1)[:, offs[1]:offs[2]].reshape(N_DEV, depth, 9 * d)
    dmod_mine = lax.dynamic_slice_in_dim(dmod_all, me * mod_cols, mod_cols, axis=2)
    sct_pad = jnp.pad(sc_all.T, ((0, 0), (0, LANES - N_DEV)))
    g_mod_w = jnp.stack([
        mm(sct_pad, jnp.pad(dmod_mine[:, l], ((0, LANES - N_DEV), (0, 0))), name=f"mod_dw{l}")
        for l in range(depth)])

    g_full = {
        "ffn_w_in": jnp.stack([jnp.stack(g_ffn_in[l]) for l in range(depth)]),
        "ffn_w_out": jnp.stack([jnp.stack(g_ffn_out[l]) for l in range(depth)]),
        "mix_w_in": jnp.stack(grads["mix_w_in"]), "glu_w": jnp.stack(grads["glu_w"]),
        "attn_w_out": jnp.stack(grads["attn_w_out"]), "mix_w_out": jnp.stack(grads["mix_w_out"])}

    def split_last(a):
        return jnp.moveaxis(a.reshape(a.shape[:-1] + (N_DEV, a.shape[-1] // N_DEV)), -2, 0)

    def split_rows(a):
        return jnp.moveaxis(a.reshape(a.shape[:-2] + (N_DEV, a.shape[-2] // N_DEV, a.shape[-1])), -3, 0)

    to_owner = [split_last(g_full["ffn_w_in"]), split_rows(g_full["ffn_w_out"]), split_last(g_full["mix_w_in"]),
                split_last(g_full["glu_w"]), split_last(g_full["attn_w_out"]), split_rows(g_full["mix_w_out"])]
    to_owner = [a.reshape(N_DEV, -1, a.shape[-1]) for a in to_owner]
    parts = dict(zip(big, all_to_all(to_owner, name="exchange_grads")))
    parts["mod_w"] = g_mod_w.reshape(1, -1, mod_cols)

    out_g, out_d, out_m, out_v = {}, {}, {}, {}
    for k in ["mod_w"] + big:
        shape = w_in[k].shape
        flat = lambda a: a.reshape(-1, shape[-1])
        res = adamw(parts[k], flat(w_in[k]), flat(m_in[k]), flat(v_in[k]), name=f"adamw_{k}")
        out_g[k], out_d[k], out_m[k], out_v[k] = [r.reshape(shape) for r in res]
    small_all = ["mod_b", "norm_pre", "norm_post"] + small_names

    def pack_small(dct):
        flat = jnp.concatenate([dct[k].reshape(-1) for k in small_all])
        tot = -(-flat.size // chunk) * chunk
        return jnp.pad(flat, (0, tot - flat.size)).reshape(tot // 1024, 1024)

    res = adamw(pack_small(g_small)[None], pack_small(w_in), pack_small(m_in), pack_small(v_in),
                name="adamw_small")
    pos = 0
    for k in small_all:
        size = w_in[k].size
        for dct, r in zip((out_g, out_d, out_m, out_v), res):
            dct[k] = r.reshape(-1)[pos:pos + size].reshape(w_in[k].shape)
        pos += size

    return (loss, grad_x, *[out_g[k] for k in names], *[out_d[k] for k in names],
            *[out_m[k] for k in names], *[out_v[k] for k in names])
```

```python
import functools
import math

import jax
import jax.numpy as jnp
from jax import lax
from jax.experimental import pallas as pl
from jax.experimental.pallas import tpu as pltpu

F32 = jnp.float32
BF16 = jnp.bfloat16
MESH = pl.DeviceIdType.MESH
N_DEV = 8
LANES = 128
SUBLANES = 8
VMEM_LIMIT = 48 * 1024 * 1024

RMS_EPS = 1e-6
FFN_RES = 0.5
ADAM_LR = 0.001
ADAM_B1 = 0.9
ADAM_B2 = 0.999
ADAM_EPS = 1e-08
ADAM_WD = 0.01
ADAM_STEP = 10
GELU_C = math.sqrt(2.0 / math.pi)
GELU_A = 0.044715


def _pick(dim, target, mult=LANES):
    t = (min(dim, target) // mult) * mult
    while t >= mult:
        if dim % t == 0:
            return t
        t -= mult
    return dim


def _params(sem):
    return pltpu.CompilerParams(dimension_semantics=sem, vmem_limit_bytes=VMEM_LIMIT)


def _sigmoid(x):
    return 1.0 / (1.0 + jnp.exp(-x))


def mm(a, b, *, name, trans_a=False, trans_b=False, out_dtype=F32, tm=1024, tn=512, tk=1024):
    if trans_a:
        kdim, m = a.shape
    else:
        m, kdim = a.shape
    if trans_b:
        n, kb = b.shape
    else:
        kb, n = b.shape
    assert kdim == kb, (a.shape, b.shape)
    tm, tn, tk = _pick(m, tm), _pick(n, tn), _pick(kdim, tk)
    nk = kdim // tk
    dims = (((0 if trans_a else 1,), (1 if trans_b else 0,)), ((), ()))

    def body(a_ref, b_ref, o_ref, acc_ref):
        k = pl.program_id(2)

        @pl.when(k == 0)
        def _():
            acc_ref[...] = jnp.zeros_like(acc_ref)

        acc_ref[...] += lax.dot_general(a_ref[...].astype(BF16), b_ref[...].astype(BF16), dims,
                                        preferred_element_type=F32)

        @pl.when(k == nk - 1)
        def _():
            o_ref[...] = acc_ref[...].astype(out_dtype)

    a_spec = (pl.BlockSpec((tk, tm), lambda i, j, k: (k, i)) if trans_a
              else pl.BlockSpec((tm, tk), lambda i, j, k: (i, k)))
    b_spec = (pl.BlockSpec((tn, tk), lambda i, j, k: (j, k)) if trans_b
              else pl.BlockSpec((tk, tn), lambda i, j, k: (k, j)))
    return pl.pallas_call(
        body, name=name, grid=(m // tm, n // tn, nk),
        in_specs=[a_spec, b_spec],
        out_specs=pl.BlockSpec((tm, tn), lambda i, j, k: (i, j)),
        out_shape=jax.ShapeDtypeStruct((m, n), out_dtype),
        scratch_shapes=[pltpu.VMEM((tm, tn), F32)],
        compiler_params=_params(("parallel", "parallel", "arbitrary")),
    )(a, b)


def mm_blockdiag(a, b, *, name, a_cb0=0, out_dtype=F32, tm=512):
    m = a.shape[0]
    g_n, ka, nb = b.shape
    tm = _pick(m, tm)

    def body(a_ref, b_ref, o_ref):
        o_ref[...] = jnp.dot(a_ref[...].astype(BF16), b_ref[...].astype(BF16),
                             preferred_element_type=F32).astype(out_dtype)

    return pl.pallas_call(
        body, name=name, grid=(m // tm, g_n),
        in_specs=[pl.BlockSpec((tm, ka), lambda i, g: (i, a_cb0 + g)),
                  pl.BlockSpec((None, ka, nb), lambda i, g: (g, 0, 0))],
        out_specs=pl.BlockSpec((tm, nb), lambda i, g: (i, g)),
        out_shape=jax.ShapeDtypeStruct((m, g_n * nb), out_dtype),
        compiler_params=_params(("parallel", "parallel")),
    )(a, b)


def mm_blockdiag_tn(a, b, *, name, g_n, ka, kb, a_cb0=0, b_cb0=0, tk=512):
    rows = a.shape[0]
    tk = _pick(rows, tk)
    nk = rows // tk

    def body(a_ref, b_ref, o_ref):
        k = pl.program_id(1)

        @pl.when(k == 0)
        def _():
            o_ref[...] = jnp.zeros_like(o_ref)

        o_ref[...] += lax.dot_general(a_ref[...].astype(BF16), b_ref[...].astype(BF16),
                                      (((0,), (0,)), ((), ())), preferred_element_type=F32)

    return pl.pallas_call(
        body, name=name, grid=(g_n, nk),
        in_specs=[pl.BlockSpec((tk, ka), lambda g, k: (k, a_cb0 + g)),
                  pl.BlockSpec((tk, kb), lambda g, k: (k, b_cb0 + g))],
        out_specs=pl.BlockSpec((None, ka, kb), lambda g, k: (g, 0, 0)),
        out_shape=jax.ShapeDtypeStruct((g_n, ka, kb), F32),
        compiler_params=_params(("parallel", "arbitrary")),
    )(a, b)


def rowwise(fn, rows, vecs, outs, reds=(), *, name, tm=256):
    metas = []
    for r in rows:
        if isinstance(r, tuple) and len(r) == 3:
            metas.append(("col", r[0], r[1], r[2]))
        elif isinstance(r, tuple):
            metas.append(("lead", r[0], r[0].shape[2], r[1]))
        else:
            metas.append(("full", r, r.shape[1], 0))
    n_rows = metas[0][1].shape[1] if metas[0][0] == "lead" else metas[0][1].shape[0]
    rc = 16 if n_rows % 16 == 0 else (SUBLANES if n_rows % SUBLANES == 0 else n_rows)
    tm = _pick(n_rows, tm, rc)
    n_inner = tm // rc
    nr, nv, no = len(metas), len(vecs), len(outs)

    def body(*refs):
        row_refs, vec_refs = refs[:nr], refs[nr:nr + nv]
        out_refs, red_refs = refs[nr + nv:nr + nv + no], refs[nr + nv + no:]
        if reds:
            @pl.when(pl.program_id(0) == 0)
            def _():
                for rr in red_refs:
                    rr[...] = jnp.zeros_like(rr)
        vec_vals = [v[...] for v in vec_refs]

        def step(s, carry):
            r0 = pl.multiple_of(s * rc, rc)
            vals = [ref[pl.ds(r0, rc), :] for ref in row_refs]
            res = fn(*vals, *vec_vals)
            if not isinstance(res, (tuple, list)):
                res = (res,)
            for o_ref, (widths, dt), val in zip(out_refs, outs, res[:no]):
                pieces = val if isinstance(val, (tuple, list)) else (val,)
                off = 0
                for w_, piece in zip(widths, pieces):
                    o_ref[pl.ds(r0, rc), off:off + w_] = piece.astype(dt)
                    off += w_
            for rr, val in zip(red_refs, res[no:]):
                rr[...] += val
            return carry

        lax.fori_loop(0, n_inner, step, 0, unroll=min(n_inner, 4))

    in_specs = []
    for kind, arr, w_, idx in metas:
        if kind == "col":
            in_specs.append(pl.BlockSpec((tm, w_), functools.partial(lambda i, cb: (i, cb), cb=idx)))
        elif kind == "lead":
            in_specs.append(pl.BlockSpec((None, tm, w_), functools.partial(lambda i, p: (p, i, 0), p=idx)))
        else:
            in_specs.append(pl.BlockSpec((tm, w_), lambda i: (i, 0)))
    for v in vecs:
        in_specs.append(pl.BlockSpec(v.shape, lambda i: (0, 0)))
    out_specs = [pl.BlockSpec((tm, sum(ws)), lambda i: (i, 0)) for ws, _ in outs]
    out_specs += [pl.BlockSpec((1, w_), lambda i: (0, 0)) for w_ in reds]
    out_shape = [jax.ShapeDtypeStruct((n_rows, sum(ws)), dt) for ws, dt in outs]
    out_shape += [jax.ShapeDtypeStruct((1, w_), F32) for w_ in reds]
    res = pl.pallas_call(
        body, name=name, grid=(n_rows // tm,),
        in_specs=in_specs, out_specs=out_specs, out_shape=out_shape,
        compiler_params=_params(("arbitrary",)),
    )(*[m[1] for m in metas], *vecs)
    return res


def _rms(x):
    return lax.rsqrt(jnp.mean(x * x, axis=-1, keepdims=True) + RMS_EPS)


def _colsum(x):
    return jnp.sum(x, axis=0, keepdims=True)


def _f_silu(c):
    return c * _sigmoid(c)


def _f_pre(x, a, sh):
    return (x * _rms(x)) * a + sh


def _f_post_add(x, y, bv):
    return x + (y * _rms(y)) * bv


def _f_post_bwd(dxo, y, bv):
    ry = _rms(y)
    yn = y * ry
    dyn = dxo * bv
    dy = ry * (dyn - yn * jnp.mean(dyn * yn, axis=-1, keepdims=True))
    return dy, _colsum(dxo * yn)


def _f_pre_bwd(dxo, dh, x, a):
    r = _rms(x)
    xn = x * r
    dxn = dh * a
    dx = dxo + r * (dxn - xn * jnp.mean(dxn * xn, axis=-1, keepdims=True))
    return dx, _colsum(dh * xn), _colsum(dh)


def _f_swiglu(g, u):
    g = g.astype(F32)
    return (g * _sigmoid(g)) * u.astype(F32)


def _f_swiglu_bwd(g, u, dm):
    g, u, dm = g.astype(F32), u.astype(F32), dm.astype(F32)
    sg = _sigmoid(g)
    dg = dm * u * (sg * (1.0 + g * (1.0 - sg)))
    du = dm * (g * sg)
    return ((dg, du),)


def _gelu_t(x):
    return jnp.tanh(GELU_C * (x + GELU_A * x * x * x))


def _f_gelu_in(y0, u, dvec):
    y = y0 + dvec * u
    return 0.5 * y * (1.0 + _gelu_t(y)), y


def _f_gelu_bwd(dge, y, u):
    t = _gelu_t(y)
    dy = dge * (0.5 * (1.0 + t) + 0.5 * y * (1.0 - t * t) * GELU_C * (1.0 + 3.0 * GELU_A * y * y))
    return dy, _colsum(dy * u)


def _f_du_fin(du0, dys, dvec):
    return du0 + dvec * dys.astype(F32)


def _f_merge(zv, zg, yb, ga, gb):
    return _sigmoid(ga) * (zv * _sigmoid(zg)) + _sigmoid(gb) * yb


def _f_merge_bwd(dmg, zv, zg, yb, ga, gb):
    sa, sb, sz = _sigmoid(ga), _sigmoid(gb), _sigmoid(zg)
    ya = zv * sz
    dya = dmg * sa
    dga = dmg * ya * sa * (1.0 - sa)
    dyb = dmg * sb
    dgb = dmg * yb * sb * (1.0 - sb)
    dzv = dya * sz
    dzg = dya * zv * sz * (1.0 - sz)
    return (dzv, dzg), dyb, (dga, dgb)


def _f_sum_parts(*parts):
    acc = parts[0].astype(F32)
    for p in parts[1:]:
        acc = acc + p.astype(F32)
    return acc


def _f_adamw(*args):
    parts, (w, m, v) = args[:-3], args[-3:]
    g = _f_sum_parts(*parts)
    m = ADAM_B1 * m + (1.0 - ADAM_B1) * g
    v = ADAM_B2 * v + (1.0 - ADAM_B2) * (g * g)
    m_hat = m / (1.0 - ADAM_B1 ** ADAM_STEP)
    v_hat = v / (1.0 - ADAM_B2 ** ADAM_STEP)
    delta = -ADAM_LR * (m_hat / (jnp.sqrt(v_hat) + ADAM_EPS) + ADAM_WD * w)
    return g, delta, m, v


def adamw(parts3, w, m, v, *, name):
    c = w.shape[1]
    rows = [(parts3, p) for p in range(parts3.shape[0])] + [w, m, v]
    return rowwise(_f_adamw, rows, [], [((c,), F32)] * 4, name=name)


def _tri_dot(tri, x):
    x1 = x.astype(BF16)
    r1 = x - x1.astype(F32)
    x2 = r1.astype(BF16)
    x3 = (r1 - x2.astype(F32)).astype(BF16)
    dot = functools.partial(jnp.dot, preferred_element_type=F32)
    return dot(tri, x1) + dot(tri, x2) + dot(tri, x3)


def cum_fwd(proj, fb, *, col_block, name, t=256):
    n = proj.shape[0]
    t = _pick(n, t, SUBLANES)

    def body(f_ref, fb_ref, cum_ref, car_ref):
        @pl.when(pl.program_id(0) == 0)
        def _():
            car_ref[...] = jnp.zeros_like(car_ref)

        x = f_ref[...] + fb_ref[...]
        lf = jnp.minimum(x, 0.0) - jnp.log(1.0 + jnp.exp(-jnp.abs(x)))
        r = lax.broadcasted_iota(jnp.int32, (t, t), 0)
        c = lax.broadcasted_iota(jnp.int32, (t, t), 1)
        cs = _tri_dot((c <= r).astype(BF16), lf) + car_ref[0:1, :]
        cum_ref[...] = cs
        car_ref[0:1, :] = cs[t - 1:t, :]

    return pl.pallas_call(
        body, name=name, grid=(n // t,),
        in_specs=[pl.BlockSpec((t, LANES), lambda i: (i, col_block)),
                  pl.BlockSpec((1, LANES), lambda i: (0, 0))],
        out_specs=pl.BlockSpec((t, LANES), lambda i: (i, 0)),
        out_shape=jax.ShapeDtypeStruct((n, LANES), F32),
        scratch_shapes=[pltpu.VMEM((SUBLANES, LANES), F32)],
        compiler_params=_params(("arbitrary",)),
    )(proj, fb)


def cum_bwd(dcum, proj, fb, *, col_block, name, t=256):
    n = proj.shape[0]
    t = _pick(n, t, SUBLANES)
    nb = n // t

    def body(dc_ref, f_ref, fb_ref, df_ref, dfb_ref, car_ref):
        @pl.when(pl.program_id(0) == 0)
        def _():
            car_ref[...] = jnp.zeros_like(car_ref)
            dfb_ref[...] = jnp.zeros_like(dfb_ref)

        r = lax.broadcasted_iota(jnp.int32, (t, t), 0)
        c = lax.broadcasted_iota(jnp.int32, (t, t), 1)
        dl = _tri_dot((c >= r).astype(BF16), dc_ref[...]) + car_ref[0:1, :]
        car_ref[0:1, :] = dl[0:1, :]
        x = f_ref[...] + fb_ref[...]
        df = dl * (1.0 / (1.0 + jnp.exp(x)))
        df_ref[...] = df
        dfb_ref[...] += _colsum(df)

    return pl.pallas_call(
        body, name=name, grid=(nb,),
        in_specs=[pl.BlockSpec((t, LANES), lambda i: (nb - 1 - i, 0)),
                  pl.BlockSpec((t, LANES), lambda i: (nb - 1 - i, col_block)),
                  pl.BlockSpec((1, LANES), lambda i: (0, 0))],
        out_specs=[pl.BlockSpec((t, LANES), lambda i: (nb - 1 - i, 0)),
                   pl.BlockSpec((1, LANES), lambda i: (0, 0))],
        out_shape=[jax.ShapeDtypeStruct((n, LANES), F32), jax.ShapeDtypeStruct((1, LANES), F32)],
        scratch_shapes=[pltpu.VMEM((SUBLANES, LANES), F32)],
        compiler_params=_params(("arbitrary",)),
    )(dcum, proj, fb)


_NT =(((1,), (1,)), ((), ()))
_TN = (((0,), (0,)), ((), ()))


def _causal_keep(t):
    return lax.broadcasted_iota(jnp.int32, (t, t), 1) <= lax.broadcasted_iota(jnp.int32, (t, t), 0)


def attn_fwd(q, k, v, crow, *, name, t=512, hb=8, comm=None):
    h_n, n, dh = q.shape
    hb = min(hb, h_n)
    t = _pick(n, t)
    nb = n // t
    ng = h_n // hb
    host = _Hosted(comm, 4, 2, 3)

    def body(*refs):
        (q_ref, k_ref, v_ref, cr_ref, o_ref, lse_ref, m_sc, l_sc, acc_sc), crefs = host.split(refs)
        g, i, j = pl.program_id(0), pl.program_id(1), pl.program_id(2)
        host.phase("start", (g == 0) & (i == 0) & (j == 0), crefs)
        host.phase("mid", (g == ng - 1) & (i == (3 * nb) // 4) & (j == 0), crefs)

        @pl.when(j == 0)
        def _():
            m_sc[...] = jnp.full_like(m_sc, -jnp.inf)
            l_sc[...] = jnp.zeros_like(l_sc)
            acc_sc[...] = jnp.zeros_like(acc_sc)

        def update(diagonal):
            keep = _causal_keep(t) if diagonal else None
            for h in range(hb):
                s = lax.dot_general(q_ref[h], k_ref[h], _NT, preferred_element_type=F32) - cr_ref[h]
                if diagonal:
                    s = jnp.where(keep, s, -jnp.inf)
                m_prev = m_sc[h]
                m_new = jnp.maximum(m_prev, jnp.max(s, axis=-1, keepdims=True))
                p = jnp.exp(s - m_new)
                alpha = jnp.exp(m_prev - m_new)
                l_sc[h] = alpha * l_sc[h] + jnp.sum(p, axis=-1, keepdims=True)
                p_hi = p.astype(BF16)
                p_lo = (p - p_hi.astype(F32)).astype(BF16)
                vv = v_ref[h]
                acc_sc[h] = (alpha * acc_sc[h] + jnp.dot(p_hi, vv, preferred_element_type=F32)
                             + jnp.dot(p_lo, vv, preferred_element_type=F32))
                m_sc[h] = m_new

        @pl.when(j < i)
        def _():
            update(False)

        @pl.when(j == i)
        def _():
            update(True)

        @pl.when(j == nb - 1)
        def _():
            o_ref[...] = acc_sc[...] / l_sc[...]
            lse_ref[...] = m_sc[...] + jnp.log(l_sc[...])

        host.phase("finish", (g == ng - 1) & (i == nb - 1) & (j == nb - 1), crefs)

    qspec = pl.BlockSpec((hb, t, dh), lambda g, i, j: (g, i, 0))
    kspec = pl.BlockSpec((hb, t, dh), lambda g, i, j: (g, jnp.minimum(j, i), 0))
    colspec = pl.BlockSpec((hb, t, 1), lambda g, i, j: (g, i, 0))
    in_specs, out_specs, out_shape, scratch, extra = host.specs(
        [qspec, kspec, kspec, pl.BlockSpec((hb, 1, t), lambda g, i, j: (g, 0, jnp.minimum(j, i)))],
        [qspec, colspec],
        [jax.ShapeDtypeStruct((h_n, n, dh), F32), jax.ShapeDtypeStruct((h_n, n, 1), F32)],
        [pltpu.VMEM((hb, t, 1), F32), pltpu.VMEM((hb, t, 1), F32), pltpu.VMEM((hb, t, dh), F32)])
    return pl.pallas_call(
        body, name=name, grid=(ng, nb, nb),
        in_specs=in_specs, out_specs=out_specs, out_shape=out_shape, scratch_shapes=scratch,
        compiler_params=_params(("arbitrary", "arbitrary", "arbitrary")),
    )(q, k, v, crow, *extra)


def _f_rowdot(a, b):
    return jnp.sum(a.astype(F32) * b, axis=-1, keepdims=True)


def attn_bwd(q, k, v, do, o, lse, crow, *, scale, tag, t=512, hb=2, comm=None):
    h_n, n, dh = q.shape
    hb = min(hb, h_n)
    t = _pick(n, t)
    nb = n // t
    dob = do.astype(BF16)
    delta = rowwise(_f_rowdot, [dob.reshape(h_n * n, dh), o.reshape(h_n * n, dh)], [], [((1,), F32)],
                    name=f"attn_delta_{tag}")[0].reshape(h_n, n, 1)

    ng = h_n // hb
    host = _Hosted(comm, 7, 4, 3)

    def body(*refs):
        (q_ref, k_ref, v_ref, do_ref, lse_ref, dl_ref, cr_ref,
         dq_ref, dk_ref, dv_ref, dcr_ref, dk_acc, dv_acc, dcr_acc), crefs = host.split(refs)
        g, j, i = pl.program_id(0), pl.program_id(1), pl.program_id(2)
        host.phase("start", (g == 0) & (j == 0) & (i == 0), crefs)

        @pl.when((j == 0) & (i == 0))
        def _():
            dq_ref[...] = jnp.zeros_like(dq_ref)

        @pl.when(i == 0)
        def _():
            dk_acc[...] = jnp.zeros_like(dk_acc)
            dv_acc[...] = jnp.zeros_like(dv_acc)
            dcr_acc[...] = jnp.zeros_like(dcr_acc)

        def update(diagonal):
            keep = _causal_keep(t) if diagonal else None
            r0 = pl.multiple_of(i * t, t)
            for h in range(hb):
                qv, kv, vv, dov = q_ref[h], k_ref[h], v_ref[h], do_ref[h]
                s = lax.dot_general(qv, kv, _NT, preferred_element_type=F32) - cr_ref[h]
                if diagonal:
                    s = jnp.where(keep, s, -jnp.inf)
                p = jnp.exp(s - lse_ref[h])
                dv_acc[h] += lax.dot_general(p.astype(BF16), dov, _TN, preferred_element_type=F32)
                dp = lax.dot_general(dov, vv, _NT, preferred_element_type=F32)
                ds = p * (dp - dl_ref[h])
                dcr_acc[h] -= _colsum(ds)
                dsb = ds.astype(BF16)
                dk_acc[h] += lax.dot_general(dsb, qv, _TN, preferred_element_type=F32)
                dq_ref[h, pl.ds(r0, t), :] += jnp.dot(dsb, kv, preferred_element_type=F32) * scale

        @pl.when(i > j)
        def _():
            update(False)

        @pl.when(i == j)
        def _():
            update(True)

        @pl.when(i == nb - 1)
        def _():
            dk_ref[...] = dk_acc[...]
            dv_ref[...] = dv_acc[...]
            dcr_ref[...] = dcr_acc[...]

        host.phase("finish", (g == ng - 1) & (j == nb - 1) & (i == nb - 1), crefs)

    qspec = pl.BlockSpec((hb, t, dh), lambda g, j, i: (g, jnp.maximum(i, j), 0))
    colspec = pl.BlockSpec((hb, t, 1), lambda g, j, i: (g, jnp.maximum(i, j), 0))
    kspec = pl.BlockSpec((hb, t, dh), lambda g, j, i: (g, j, 0))
    rowspec = pl.BlockSpec((hb, 1, t), lambda g, j, i: (g, 0, j))
    in_specs, out_specs, out_shape, scratch, extra = host.specs(
        [qspec, kspec, kspec, qspec, colspec, colspec, rowspec],
        [pl.BlockSpec((hb, n, dh), lambda g, j, i: (g, 0, 0)), kspec, kspec, rowspec],
        [jax.ShapeDtypeStruct((h_n, n, dh), F32)] * 3 + [jax.ShapeDtypeStruct((h_n, 1, n), F32)],
        [pltpu.VMEM((hb, t, dh), F32), pltpu.VMEM((hb, t, dh), F32), pltpu.VMEM((hb, 1, t), F32)])
    return pl.pallas_call(
        body, name=f"attn_bwd_{tag}", grid=(ng, nb, nb),
        in_specs=in_specs, out_specs=out_specs, out_shape=out_shape, scratch_shapes=scratch,
        compiler_params=_params(("arbitrary", "arbitrary", "arbitrary")),
    )(q, k, v, dob, lse, delta, crow, *extra)


SCAN_STEPS = (1, 2, 4)


def ssm_scan(x, tab, *, reverse, name, s_prev=None, tt=512):
    n, width = x.shape
    nc, _, hw = tab.shape
    cw = 2 * hw
    assert width == nc * cw
    tt = _pick(n, tt, SUBLANES)
    nt = n // tt
    ng = tt // SUBLANES
    with_grad = s_prev is not None

    def body(*refs):
        if with_grad:
            x_ref, s_ref, tab_ref, o_ref, g_ref, car_ref = refs
        else:
            x_ref, tab_ref, o_ref, car_ref = refs

        @pl.when(pl.program_id(1) == 0)
        def _():
            car_ref[...] = jnp.zeros_like(car_ref)
            if with_grad:
                g_ref[...] = jnp.zeros_like(g_ref)

        q_re, q_im = tab_ref[0:8, :], tab_ref[8:16, :]
        p_re = [tab_ref[16 + i:17 + i, :] for i in range(3)]
        p_im = [tab_ref[24 + i:25 + i, :] for i in range(3)]
        row = lax.broadcasted_iota(jnp.int32, (SUBLANES, hw), 0)

        def group(gi, carry):
            c_re, c_im = carry
            g = (ng - 1 - gi) if reverse else gi
            r0 = pl.multiple_of(g * SUBLANES, SUBLANES)
            xr = x_ref[pl.ds(r0, SUBLANES), 0:hw]
            xi = x_ref[pl.ds(r0, SUBLANES), hw:cw]
            for i, d in enumerate(SCAN_STEPS):
                if reverse:
                    shift, keep = SUBLANES - d, row < SUBLANES - d
                else:
                    shift, keep = d, row >= d
                sr = jnp.where(keep, pltpu.roll(xr, shift, 0), 0.0)
                si = jnp.where(keep, pltpu.roll(xi, shift, 0), 0.0)
                xr, xi = (xr + p_re[i] * sr - p_im[i] * si,
                          xi + p_re[i] * si + p_im[i] * sr)
            xr, xi = (xr + q_re * c_re - q_im * c_im,
                      xi + q_re * c_im + q_im * c_re)
            o_ref[pl.ds(r0, SUBLANES), 0:hw] = xr
            o_ref[pl.ds(r0, SUBLANES), hw:cw] = xi
            if with_grad:
                nr = jnp.where(row < SUBLANES - 1, pltpu.roll(xr, SUBLANES - 1, 0), c_re)
                ni = jnp.where(row < SUBLANES - 1, pltpu.roll(xi, SUBLANES - 1, 0), c_im)
                sr = s_ref[pl.ds(r0, SUBLANES), 0:hw]
                si = s_ref[pl.ds(r0, SUBLANES), hw:cw]
                g_ref[:, 0:hw] += nr * sr + ni * si
                g_ref[:, hw:cw] += ni * sr - nr * si
            if reverse:
                return xr[0:1, :], xi[0:1, :]
            return xr[SUBLANES - 1:SUBLANES, :], xi[SUBLANES - 1:SUBLANES, :]

        c_re, c_im = lax.fori_loop(0, ng, group, (car_ref[0:1, 0:hw], car_ref[0:1, hw:cw]),
                                   unroll=min(ng, 4))
        car_ref[0:1, 0:hw] = c_re
        car_ref[0:1, hw:cw] = c_im

    if reverse:
        xspec = pl.BlockSpec((tt, cw), lambda c, t: (nt - 1 - t, c))
    else:
        xspec = pl.BlockSpec((tt, cw), lambda c, t: (t, c))
    tspec = pl.BlockSpec((None, 32, hw), lambda c, t: (c, 0, 0))
    in_specs = [xspec, xspec, tspec] if with_grad else [xspec, tspec]
    out_specs = [xspec]
    out_shape = [jax.ShapeDtypeStruct((n, width), F32)]
    if with_grad:
        out_specs.append(pl.BlockSpec((None, SUBLANES, cw), lambda c, t: (c, 0, 0)))
        out_shape.append(jax.ShapeDtypeStruct((nc, SUBLANES, cw), F32))
    operands = (x, s_prev, tab) if with_grad else (x, tab)
    return pl.pallas_call(
        body, name=name, grid=(nc, nt),
        in_specs=in_specs, out_specs=out_specs, out_shape=out_shape,
        scratch_shapes=[pltpu.VMEM((SUBLANES, cw), F32)],
        compiler_params=_params(("parallel", "arbitrary")),
    )(*operands)


def _slot(pos):
    return 4 * pos[0] + 2 * pos[1] + pos[2]


def _comm_scratch(n):
    return [pltpu.SemaphoreType.DMA((7 * n,)), pltpu.SemaphoreType.DMA((7 * n,)), pltpu.SemaphoreType.DMA((n,))]


def _gather_copies(ins, outs, sems):
    send_sems, recv_sems, local_sems = sems
    n = len(ins)
    x, y, c = lax.axis_index("x"), lax.axis_index("y"), lax.axis_index("c")
    me, sibling = (x, y, c), (x, y, 1 - c)
    chips = [(1 - x, y), (x, 1 - y), (1 - x, 1 - y)]

    def copy(t, k, block, to, src=None):
        dst = outs[t].at[_slot(block)]
        return pltpu.make_async_remote_copy(
            src_ref=dst if src is None else src, dst_ref=dst,
            send_sem=send_sems.at[7 * t + k], recv_sem=recv_sems.at[7 * t + k],
            device_id=to, device_id_type=MESH)

    jc = list(enumerate(chips))
    return dict(
        mine=[pltpu.make_async_copy(ins[t], outs[t].at[_slot(me)], local_sems.at[t]) for t in range(n)],
        first=[cp for t in range(n) for cp in
               [copy(t, 0, me, sibling, src=ins[t])] + [copy(t, 1 + j, me, (*chip, c), src=ins[t]) for j, chip in jc]],
        arrive=[copy(t, 1 + j, (*chip, c), me) for t in range(n) for j, chip in jc],
        passed=[copy(t, 4 + j, (*chip, c), sibling) for t in range(n) for j, chip in jc],
        from_sibling=[cp for t in range(n) for cp in
                      [copy(t, 0, sibling, me)] + [copy(t, 4 + j, (*chip, 1 - c), me) for j, chip in jc]])


def _gather_start(ins, outs, sems):
    cps = _gather_copies(ins, outs, sems)
    for cp in cps["mine"] + cps["first"]:
        cp.start()


def _gather_forward(ins, outs, sems):
    cps = _gather_copies(ins, outs, sems)
    for arrived, onward in zip(cps["arrive"], cps["passed"]):
        arrived.wait_recv()
        onward.start()


def _gather_finish(ins, outs, sems):
    cps = _gather_copies(ins, outs, sems)
    for cp in cps["from_sibling"]:
        cp.wait_recv()
    for cp in cps["first"] + cps["passed"]:
        cp.wait_send()
    for cp in cps["mine"]:
        cp.wait()


def gather_comm(arrs):
    return dict(ins=list(arrs), out_shape=[jax.ShapeDtypeStruct((N_DEV,) + a.shape, a.dtype) for a in arrs],
                scratch=_comm_scratch(len(arrs)), start=_gather_start, mid=_gather_forward, finish=_gather_finish)


def _exchange_copies(ins, outs, sems):
    send_sems, recv_sems, local_sems = sems
    n = len(ins)
    me = (lax.axis_index("x"), lax.axis_index("y"), lax.axis_index("c"))
    peers = []
    for k in range(1, N_DEV):
        flip = ((k >> 2) & 1, (k >> 1) & 1, k & 1)
        peers.append(tuple(1 - p if f else p for p, f in zip(me, flip)))

    def copy(t, k, peer, dst_slot):
        return pltpu.make_async_remote_copy(
            src_ref=ins[t].at[_slot(peer)], dst_ref=outs[t].at[dst_slot],
            send_sem=send_sems.at[7 * t + k], recv_sem=recv_sems.at[7 * t + k],
            device_id=peer, device_id_type=MESH)

    return dict(
        mine=[pltpu.make_async_copy(ins[t].at[_slot(me)], outs[t].at[_slot(me)], local_sems.at[t])
              for t in range(n)],
        send=[copy(t, k, peer, _slot(me)) for t in range(n) for k, peer in enumerate(peers)],
        both=[copy(t, k, peer, _slot(peer)) for t in range(n) for k, peer in enumerate(peers)])


def _exchange_start(ins, outs, sems):
    cps = _exchange_copies(ins, outs, sems)
    for cp in cps["mine"] + cps["send"]:
        cp.start()


def _exchange_finish(ins, outs, sems):
    cps = _exchange_copies(ins, outs, sems)
    for cp in cps["both"]:
        cp.wait()
    for cp in cps["mine"]:
        cp.wait()


def exchange_comm(arrs):
    return dict(ins=list(arrs), out_shape=[jax.ShapeDtypeStruct(a.shape, a.dtype) for a in arrs],
                scratch=_comm_scratch(len(arrs)), start=_exchange_start, mid=None, finish=_exchange_finish)


def run_comm(comm, *, name):
    n_in, n_out = len(comm["ins"]), len(comm["out_shape"])

    def body(*refs):
        ins, outs, sems = refs[:n_in], refs[n_in:n_in + n_out], refs[n_in + n_out:]
        comm["start"](ins, outs, sems)
        if comm["mid"] is not None:
            comm["mid"](ins, outs, sems)
        comm["finish"](ins, outs, sems)

    any_spec = pl.BlockSpec(memory_space=pl.ANY)
    return pl.pallas_call(
        body, name=name, in_specs=[any_spec] * n_in, out_specs=[any_spec] * n_out,
        out_shape=comm["out_shape"], scratch_shapes=comm["scratch"],
    )(*comm["ins"])


class _Hosted:
    def __init__(self, comm, n_in, n_out, n_scratch):
        self.comm = comm
        self.n_ci = len(comm["ins"]) if comm else 0
        self.n_co = len(comm["out_shape"]) if comm else 0
        self.n_in, self.n_out, self.n_scratch = n_in, n_out, n_scratch

    def split(self, refs):
        a = self.n_in
        b = a + self.n_ci
        c = b + self.n_out
        e = c + self.n_co
        f = e + self.n_scratch
        return refs[:a] + refs[b:c] + refs[e:f], (refs[a:b], refs[c:e], refs[f:])

    def phase(self, which, when, crefs):
        fn = self.comm[which] if self.comm else None
        if fn is not None:
            pl.when(when)(lambda: fn(*crefs))

    def specs(self, in_specs, out_specs, out_shape, scratch):
        any_spec = pl.BlockSpec(memory_space=pl.ANY)
        if not self.comm:
            return in_specs, out_specs, out_shape, scratch, ()
        return (in_specs + [any_spec] * self.n_ci, out_specs + [any_spec] * self.n_co,
                out_shape + self.comm["out_shape"], scratch + self.comm["scratch"], tuple(self.comm["ins"]))


def all_gather(arrs, *, name):
    return run_comm(gather_comm(arrs), name=name)


def _discretise(a_re, a_im, log_dt, b_re, b_im):
    ar = jnp.minimum(a_re, -1e-4)
    dt = jnp.exp(log_dt)[:, None]
    e, ph = ar * dt, a_im * dt
    mag = jnp.exp(e)
    lr, li = mag * jnp.cos(ph), mag * jnp.sin(ph)
    den = ar * ar + a_im * a_im
    nr, ni = lr - 1.0, li
    cr = (nr * ar + ni * a_im) / den
    ci = (ni * ar - nr * a_im) / den
    bb_re = cr[..., None] * b_re - ci[..., None] * b_im
    bb_im = cr[..., None] * b_im + ci[..., None] * b_re
    return e, ph, bb_re, bb_im


def _lam_pow(e, ph, k, conj):
    mag = jnp.exp(k * e)
    return mag * jnp.cos(k * ph), (-1.0 if conj else 1.0) * mag * jnp.sin(k * ph)


def _scan_table(e, ph, nc, reverse):
    hw = e.size // nc
    e, ph = e.reshape(nc, 1, hw), ph.reshape(nc, 1, hw)
    j = jnp.arange(SUBLANES, dtype=F32).reshape(1, SUBLANES, 1)
    kq = (SUBLANES - j) if reverse else (j + 1.0)
    q_re, q_im = _lam_pow(e, ph, kq, reverse)
    kp = jnp.array(SCAN_STEPS + (0,) * 5, F32).reshape(1, SUBLANES, 1)
    p_re, p_im = _lam_pow(e, ph, kp, reverse)
    return jnp.concatenate([q_re, q_im, p_re, p_im], axis=1)


def _blockdiag(m, nc):
    g, a, b = m.shape
    gc = g // nc
    m = m.reshape(nc, gc, a, b)
    eye = jnp.eye(gc, dtype=m.dtype)
    return jnp.einsum("cgab,gh->cgahb", m, eye).reshape(nc, gc * a, gc * b)


def _blockdiag_take(m, g):
    nc = m.shape[0]
    gc = g // nc
    a, b = m.shape[1] // gc, m.shape[2] // gc
    m = m.reshape(nc, gc, a, gc, b)
    eye = jnp.eye(gc, dtype=m.dtype)
    return jnp.einsum("cgahb,gh->cgab", m, eye).reshape(g, a, b)


def kernel(x, c, mod_w, mod_b, norm_pre, norm_post, ffn_w_in, ffn_w_out, mix_w_in, forget_b, ssm_a_re, ssm_a_im, ssm_log_dt, ssm_b_re, ssm_b_im, ssm_c_re, ssm_c_im, ssm_d, glu_w, attn_w_out, mix_w_out, loss_target, m_mod_w, m_mod_b, m_norm_pre, m_norm_post, m_ffn_w_in, m_ffn_w_out, m_mix_w_in, m_forget_b, m_ssm_a_re, m_ssm_a_im, m_ssm_log_dt, m_ssm_b_re, m_ssm_b_im, m_ssm_c_re, m_ssm_c_im, m_ssm_d, m_glu_w, m_attn_w_out, m_mix_w_out, v_mod_w, v_mod_b, v_norm_pre, v_norm_post, v_ffn_w_in, v_ffn_w_out, v_mix_w_in, v_forget_b, v_ssm_a_re, v_ssm_a_im, v_ssm_log_dt, v_ssm_b_re, v_ssm_b_im, v_ssm_c_re, v_ssm_c_im, v_ssm_d, v_glu_w, v_attn_w_out, v_mix_w_out):
    names = ["mod_w", "mod_b", "norm_pre", "norm_post", "ffn_w_in", "ffn_w_out", "mix_w_in", "forget_b",
             "ssm_a_re", "ssm_a_im", "ssm_log_dt", "ssm_b_re", "ssm_b_im", "ssm_c_re", "ssm_c_im", "ssm_d",
             "glu_w", "attn_w_out", "mix_w_out"]
    w_in = dict(zip(names, [mod_w, mod_b, norm_pre, norm_post, ffn_w_in, ffn_w_out, mix_w_in, forget_b,
                            ssm_a_re, ssm_a_im, ssm_log_dt, ssm_b_re, ssm_b_im, ssm_c_re, ssm_c_im, ssm_d,
                            glu_w, attn_w_out, mix_w_out]))
    m_in = dict(zip(names, [m_mod_w, m_mod_b, m_norm_pre, m_norm_post, m_ffn_w_in, m_ffn_w_out, m_mix_w_in,
                            m_forget_b, m_ssm_a_re, m_ssm_a_im, m_ssm_log_dt, m_ssm_b_re, m_ssm_b_im,
                            m_ssm_c_re, m_ssm_c_im, m_ssm_d, m_glu_w, m_attn_w_out, m_mix_w_out]))
    v_in = dict(zip(names, [v_mod_w, v_mod_b, v_norm_pre, v_norm_post, v_ffn_w_in, v_ffn_w_out, v_mix_w_in,
                            v_forget_b, v_ssm_a_re, v_ssm_a_im, v_ssm_log_dt, v_ssm_b_re, v_ssm_b_im,
                            v_ssm_c_re, v_ssm_c_im, v_ssm_d, v_glu_w, v_attn_w_out, v_mix_w_out]))

    depth = mod_w.shape[0]
    n_tok, d = x.shape[1], x.shape[2]
    ff = ffn_w_out.shape[2] * N_DEV
    heads = forget_b.shape[1]
    sw = ssm_d.shape[1]
    g_n, p_n, n_n = ssm_b_re.shape[1:]
    aw = attn_w_out.shape[1]
    dh = aw // heads
    iw = mix_w_in.shape[2] * N_DEV
    nc = sw // LANES
    hw = g_n * p_n // nc
    mod_cols = mod_w.shape[2]
    scale = dh ** -0.5
    assert iw == sw + 3 * aw + heads + 2 * d and heads <= LANES
    assert math.log2(scale).is_integer(), "q is pre-scaled in bf16: exact only for a power of two"
    off_u, off_q, off_f = 2 * d, 2 * d + sw, 2 * d + sw + 3 * aw
    iwp = off_f + LANES
    assert off_u % sw == 0 and off_q % aw == 0 and off_f % LANES == 0

    me = 4 * lax.axis_index("x") + 2 * lax.axis_index("y") + lax.axis_index("c")
    x2 = x.reshape(n_tok, d)
    tgt = loss_target.reshape(n_tok, d)

    silu_c = rowwise(_f_silu, [c], [], [((d,), F32)], name="silu_c")[0]
    big = ["ffn_w_in", "ffn_w_out", "mix_w_in", "glu_w", "attn_w_out", "mix_w_out"]
    def layer_shards(l):
        return [w_in[k][l].astype(BF16) for k in big]

    cut = [0, sw, sw + aw, sw + 2 * aw, sw + 3 * aw, sw + 3 * aw + heads, sw + 3 * aw + heads + d, iw]

    def assemble(gw):
        gw = dict(zip(big, gw))
        w_mix_in = jnp.moveaxis(gw["mix_w_in"], 0, 1).reshape(d, iw)
        seg = lambda i: w_mix_in[:, cut[i]:cut[i + 1]]
        return dict(
            win=jnp.moveaxis(gw["ffn_w_in"], 0, 2).reshape(2, d, 2 * ff),
            wout=jnp.moveaxis(gw["ffn_w_out"], 0, 1).reshape(2, ff, d),
            wmi=jnp.concatenate([seg(5), seg(6), seg(0), seg(1), seg(2), seg(3),
                                 jnp.pad(seg(4), ((0, 0), (0, LANES - heads)))], axis=-1),
            glu=jnp.moveaxis(gw["glu_w"], 0, 1).reshape(sw, 2 * d),
            ao=jnp.moveaxis(gw["attn_w_out"], 0, 1).reshape(aw, d),
            mo=gw["mix_w_out"].reshape(d, d))

    gathered = all_gather(
        [silu_c, norm_pre.reshape(-1, norm_pre.shape[-1]), norm_post.reshape(-1, norm_post.shape[-1])]
        + layer_shards(0), name="gather_layer0")
    sc_all = gathered[0].reshape(N_DEV, d)
    gpre = jnp.moveaxis(gathered[1].reshape(N_DEV, depth, 3, -1), 0, 2).reshape(depth, 3, d)
    gpost = jnp.moveaxis(gathered[2].reshape(N_DEV, depth, 3, -1), 0, 2).reshape(depth, 3, d)
    lw = [assemble(gathered[3:])] + [None] * (depth - 1)

    sc_pad = jnp.pad(sc_all, ((0, LANES - N_DEV), (0, 0)))
    mod_part = jnp.stack([mm(sc_pad, mod_w[l], name=f"mod_fwd{l}")[:N_DEV] for l in range(depth)], axis=1)
    mod_part = mod_part + lax.dynamic_slice_in_dim(mod_b, me * mod_cols, mod_cols, axis=1)[None]
    mod_all = all_gather([mod_part], name="gather_mod")[0]
    mod_own = lax.dynamic_index_in_dim(mod_all, me, axis=1, keepdims=False)
    mod_own = mod_own.transpose(1, 0, 2).reshape(depth, 3, 3, d)
    res_w = (FFN_RES, 1.0, FFN_RES)

    def vec_a(l, i):
        return (gpre[l, i] * (1.0 + mod_own[l, i, 1])).reshape(1, d)

    def vec_sh(l, i):
        return mod_own[l, i, 0].reshape(1, d)

    def vec_b(l, i):
        return (res_w[i] * mod_own[l, i, 2] * gpost[l, i]).reshape(1, d)

    ssm = []
    for l in range(depth):
        (e, ph, bb_re, bb_im), disc_vjp = jax.vjp(_discretise,ssm_a_re[l], ssm_a_im[l], ssm_log_dt[l],
                                                  ssm_b_re[l], ssm_b_im[l])
        b_mat = jnp.concatenate([_blockdiag(bb_re.transpose(0, 2, 1), nc),
                                 _blockdiag(bb_im.transpose(0, 2, 1), nc)], axis=2)
        c_mat = jnp.concatenate([_blockdiag(ssm_c_re[l].transpose(0, 2, 1), nc),
                                 _blockdiag(-ssm_c_im[l].transpose(0, 2, 1), nc)], axis=1)
        ssm.append(dict(e=e, ph=ph, vjp=disc_vjp, b=b_mat.astype(BF16), c=c_mat.astype(BF16),
                        bt=b_mat.transpose(0, 2, 1).astype(BF16), ct=c_mat.transpose(0, 2, 1).astype(BF16),
                        tab_f=_scan_table(e, ph, nc, False), tab_r=_scan_table(e, ph, nc, True),
                        dvec=ssm_d[l].reshape(1, sw)))

    fb_pad = jnp.pad(forget_b, ((0, 0), (0, LANES - heads)))

    def heads_first(a):
        return a.reshape(n_tok, heads, dh).transpose(1, 0, 2)

    def heads_last(a):
        return a.transpose(1, 0, 2).reshape(n_tok, heads * dh)

    def ffn_fwd(xin, l, i, j, tag):
        h = rowwise(_f_pre, [xin], [vec_a(l, i), vec_sh(l, i)], [((d,), BF16)], name=f"pre_{tag}")[0]
        a = mm(h, lw[l]["win"][j], name=f"ffn_in_{tag}", out_dtype=BF16)
        m = rowwise(_f_swiglu, [(a, ff, 0), (a, ff, 1)], [], [((ff,), BF16)], name=f"swiglu_{tag}")[0]
        y = mm(m, lw[l]["wout"][j], name=f"ffn_out_{tag}", tk=1408)
        xout = rowwise(_f_post_add, [xin, y], [vec_b(l, i)], [((d,), F32)], name=f"post_{tag}")[0]
        return xout, dict(x=xin, h=h, a=a, m=m, y=y)

    def mixer_fwd(xin, l, tag):
        s5 = ssm[l]
        h = rowwise(_f_pre, [xin], [vec_a(l, 1), vec_sh(l, 1)], [((d,), BF16)], name=f"pre_{tag}")[0]
        proj = mm(h, lw[l]["wmi"], name=f"mix_in_{tag}", tn=384)
        bu = mm_blockdiag(proj, s5["b"], a_cb0=off_u // LANES, name=f"ssm_bu_{tag}")
        st = ssm_scan(bu, s5["tab_f"], reverse=False, name=f"ssm_scan_{tag}")[0]
        y0 = mm_blockdiag(st, s5["c"], name=f"ssm_y_{tag}")
        ge, ys = rowwise(_f_gelu_in, [y0, (proj, sw, off_u // sw)], [s5["dvec"]],
                         [((sw,), BF16), ((sw,), F32)], name=f"gelu_{tag}")
        z = mm(ge, lw[l]["glu"], name=f"glu_{tag}")
        cum = cum_fwd(proj, fb_pad[l:l + 1], col_block=off_f // LANES, name=f"cum_{tag}")
        crow = cum[:, :heads].T[:, None, :]
        q, k, v = [heads_first(proj[:, off_q + i * aw:off_q + (i + 1) * aw] * sc_).astype(BF16)
                   for i, sc_ in enumerate((scale, 1.0, 1.0))]
        nxt = gather_comm(layer_shards(l + 1)) if l + 1 < depth else None
        o, lse, *arrived = attn_fwd(q, k, v, crow, name=f"attn_{tag}", comm=nxt)
        if nxt is not None:
            lw[l + 1] = assemble(arrived)
        attn = heads_last(o).astype(BF16)
        yb = mm(attn, lw[l]["ao"], name=f"attn_out_{tag}")
        mg = rowwise(_f_merge, [(z, d, 0), (z, d, 1), yb, (proj, d, 0), (proj, d, 1)], [],
                     [((d,), BF16)], name=f"merge_{tag}")[0]
        y = mm(mg, lw[l]["mo"], name=f"mix_out_{tag}")
        xout = rowwise(_f_post_add, [xin, y], [vec_b(l, 1)], [((d,), F32)], name=f"post_{tag}")[0]
        saved = dict(x=xin, h=h, proj=proj, st=st, ys=ys, ge=ge, z=z, q=q, k=k, v=v, o=o, lse=lse,
                     crow=crow, attn=attn, yb=yb, mg=mg, y=y)
        return xout, saved

    saved = []
    xc = x2
    for l in range(depth):
        xc, s0 = ffn_fwd(xc, l, 0, 0, f"l{l}a")
        xc, s1 = mixer_fwd(xc, l, f"l{l}m")
        xc, s2 = ffn_fwd(xc, l, 2, 1, f"l{l}b")
        saved.append((s0, s1, s2))

    def f_loss(xf, t):
        e_ = xf - t
        return e_ * (1.0 / d), _colsum(e_ * e_)

    dx, sq = rowwise(f_loss, [xc, tgt], [], [((d,), F32)], [d], name="loss_head")
    loss_part = 0.5 * jnp.sum(sq) / d

    grads = {k: [None] * depth for k in big}
    small_g = [dict() for _ in range(depth)]
    dmod = [[None] * 3 for _ in range(depth)]
    dgpre = [[None] * 3 for _ in range(depth)]
    dgpost = [[None] * 3 for _ in range(depth)]

    def norm_grads(l, i, d_a, d_sh, d_bv):
        d_a, d_sh, d_bv = d_a.reshape(d), d_sh.reshape(d), d_bv.reshape(d)
        dmod[l][i] = jnp.stack([d_sh, d_a * gpre[l, i], res_w[i] * gpost[l, i] * d_bv])
        dgpre[l][i] = d_a * (1.0 + mod_own[l, i, 1])
        dgpost[l][i] = res_w[i] * mod_own[l, i, 2] * d_bv

    def ffn_bwd(dxo, sv, l, i, j, tag):
        dy, d_bv = rowwise(_f_post_bwd, [dxo, sv["y"]], [vec_b(l, i)], [((d,), BF16)], [d],
                           name=f"post_bwd_{tag}")
        dm = mm(dy, lw[l]["wout"][j], trans_b=True, name=f"ffn_out_dx_{tag}", out_dtype=BF16, tn=1408)
        g_out = mm(sv["m"], dy, trans_a=True, name=f"ffn_out_dw_{tag}", out_dtype=BF16, tm=1408, tn=1024,
                   tk=512)
        da = rowwise(_f_swiglu_bwd, [(sv["a"], ff, 0), (sv["a"], ff, 1), dm], [], [((ff, ff), BF16)],
                     name=f"swiglu_bwd_{tag}")[0]
        g_in = mm(sv["h"], da, trans_a=True, name=f"ffn_in_dw_{tag}", out_dtype=BF16, tk=512)
        dh_ = mm(da, lw[l]["win"][j], trans_b=True, name=f"ffn_in_dx_{tag}", tk=1408)
        dxn, d_a, d_sh = rowwise(_f_pre_bwd, [dxo, dh_, sv["x"]], [vec_a(l, i)], [((d,), F32)], [d, d],
                                 name=f"pre_bwd_{tag}")
        norm_grads(l, i, d_a, d_sh, d_bv)
        return dxn, g_in, g_out

    def mixer_bwd(dxo, sv, l, tag, comm):
        s5 = ssm[l]
        proj = sv["proj"]
        dy, d_bv = rowwise(_f_post_bwd, [dxo, sv["y"]], [vec_b(l, 1)], [((d,), BF16)], [d],
                           name=f"post_bwd_{tag}")
        dmg = mm(dy, lw[l]["mo"], trans_b=True, name=f"mix_out_dx_{tag}")
        g_mo = mm(sv["mg"], dy, trans_a=True, name=f"mix_out_dw_{tag}", out_dtype=BF16, tk=512)
        dz, dyb, dgab = rowwise(
            _f_merge_bwd, [dmg, (sv["z"], d, 0), (sv["z"], d, 1), sv["yb"], (proj, d, 0), (proj, d, 1)], [],
            [((d, d), BF16), ((d,), BF16), ((d, d), BF16)], name=f"merge_bwd_{tag}")
        dge = mm(dz, lw[l]["glu"], trans_b=True, name=f"glu_dx_{tag}")
        g_glu = mm(sv["ge"], dz, trans_a=True, name=f"glu_dw_{tag}", out_dtype=BF16, tk=512)
        dys, d_dvec = rowwise(_f_gelu_bwd, [dge, sv["ys"], (proj, sw, off_u // sw)], [], [((sw,), BF16)],
                              [sw], name=f"gelu_bwd_{tag}")
        gadj = mm_blockdiag(dys, s5["ct"], name=f"ssm_dy_{tag}")
        adj, dlam8 = ssm_scan(gadj, s5["tab_r"], reverse=True, s_prev=sv["st"], name=f"ssm_scan_bwd_{tag}")
        du0 = mm_blockdiag(adj, s5["bt"], name=f"ssm_du_{tag}")
        d_bmat = mm_blockdiag_tn(proj, adj, g_n=nc, ka=LANES, kb=2 * hw, a_cb0=off_u // LANES,
                                 name=f"ssm_db_{tag}")
        d_cmat = mm_blockdiag_tn(sv["st"], dys, g_n=nc, ka=2 * hw, kb=LANES, name=f"ssm_dc_{tag}")
        du = rowwise(_f_du_fin, [du0, dys], [s5["dvec"]], [((sw,), BF16)], name=f"ssm_du_fin_{tag}")[0]
        dlam = jnp.sum(dlam8, axis=1)
        dlam_re, dlam_im = dlam[:, :hw].reshape(g_n, p_n), dlam[:, hw:].reshape(g_n, p_n)
        dbb_re = _blockdiag_take(d_bmat[:, :, :hw], g_n).transpose(0, 2, 1)
        dbb_im = _blockdiag_take(d_bmat[:, :, hw:], g_n).transpose(0, 2, 1)
        mag = jnp.exp(s5["e"])
        lr, li = mag * jnp.cos(s5["ph"]), mag * jnp.sin(s5["ph"])
        d_e = dlam_re * lr + dlam_im * li
        d_ph = -dlam_re * li + dlam_im * lr
        da_re, da_im, dlog_dt, db_re, db_im = s5["vjp"]((d_e, d_ph, dbb_re, dbb_im))
        small_g[l].update(
            ssm_a_re=da_re, ssm_a_im=da_im, ssm_log_dt=dlog_dt, ssm_b_re=db_re, ssm_b_im=db_im,
            ssm_c_re=_blockdiag_take(d_cmat[:, :hw, :], g_n).transpose(0, 2, 1),
            ssm_c_im=-_blockdiag_take(d_cmat[:, hw:, :], g_n).transpose(0, 2, 1),
            ssm_d=d_dvec.reshape(sw))
        dattn = mm(dyb, lw[l]["ao"], trans_b=True, name=f"attn_out_dx_{tag}")
        g_ao = mm(sv["attn"], dyb, trans_a=True, name=f"attn_out_dw_{tag}", out_dtype=BF16, tk=512)
        dq, dk, dv, dcr, *arrived = attn_bwd(sv["q"], sv["k"], sv["v"], heads_first(dattn), sv["o"], sv["lse"],
                                             sv["crow"], scale=scale, tag=tag, comm=comm)
        dcum = jnp.pad(dcr[:, 0, :].T, ((0, 0), (0, LANES - heads)))
        df, dfb = cum_bwd(dcum, proj, fb_pad[l:l + 1], col_block=off_f // LANES, name=f"cum_bwd_{tag}")
        small_g[l]["forget_b"] = dfb[0, :heads]
        dproj = jnp.concatenate([dgab, du, heads_last(dq).astype(BF16), heads_last(dk).astype(BF16),
                                 heads_last(dv).astype(BF16), df.astype(BF16)], axis=1)
        g_mi = mm(sv["h"], dproj, trans_a=True, name=f"mix_in_dw_{tag}", out_dtype=BF16, tn=384, tk=512)
        dh_ = mm(dproj, lw[l]["wmi"], trans_b=True, name=f"mix_in_dx_{tag}", tk=1408)
        dxn, d_a, d_sh = rowwise(_f_pre_bwd, [dxo, dh_, sv["x"]], [vec_a(l, 1)], [((d,), F32)], [d, d],
                                 name=f"pre_bwd_{tag}")
        norm_grads(l, 1, d_a, d_sh, d_bv)
        g_mi = jnp.concatenate([g_mi[:, off_u:off_f + heads], g_mi[:, :off_u]], axis=1)
        return dxn, g_mi, g_glu, g_ao, g_mo, arrived

    def split_last(a):
        return jnp.moveaxis(a.reshape(a.shape[:-1] + (N_DEV, a.shape[-1] // N_DEV)), -2, 0)

    def split_rows(a):
        return jnp.moveaxis(a.reshape(a.shape[:-2] + (N_DEV, a.shape[-2] // N_DEV, a.shape[-1])), -3, 0)

    def owner_blocks(l):
        blocks = [split_last(jnp.stack(g_ffn_in[l])), split_rows(jnp.stack(g_ffn_out[l])),
                  split_last(grads["mix_w_in"][l]), split_last(grads["glu_w"][l]),
                  split_last(grads["attn_w_out"][l]), split_rows(grads["mix_w_out"][l])]
        return [a.reshape(N_DEV, -1, a.shape[-1]) for a in blocks]

    g_ffn_in = [[None, None] for _ in range(depth)]
    g_ffn_out = [[None, None] for _ in range(depth)]
    layer_parts = [None] * depth
    for l in reversed(range(depth)):
        s0, s1, s2 = saved[l]
        dx, g_ffn_in[l][1], g_ffn_out[l][1] = ffn_bwd(dx, s2, l, 2, 1, f"l{l}b")
        pending = exchange_comm(owner_blocks(l + 1)) if l + 1 < depth else None
        (dx, grads["mix_w_in"][l], grads["glu_w"][l], grads["attn_w_out"][l], grads["mix_w_out"][l],
         arrived) = mixer_bwd(dx, s1, l, f"l{l}m", pending)
        if pending is not None:
            layer_parts[l + 1] = arrived
        dx, g_ffn_in[l][0], g_ffn_out[l][0] = ffn_bwd(dx, s0, l, 0, 0, f"l{l}a")
    layer_parts[0] = run_comm(exchange_comm(owner_blocks(0)), name="exchange_layer0")
    grad_x = dx.reshape(x.shape)

    small_names = ["forget_b", "ssm_a_re", "ssm_a_im", "ssm_log_dt", "ssm_b_re", "ssm_b_im", "ssm_c_re",
                   "ssm_c_im", "ssm_d"]
    pieces = [loss_part.reshape(1), jnp.stack([jnp.stack(dmod[l]) for l in range(depth)]).reshape(-1),
              jnp.stack([jnp.stack(dgpre[l]) for l in range(depth)]).reshape(-1),
              jnp.stack([jnp.stack(dgpost[l]) for l in range(depth)]).reshape(-1)]
    pieces += [jnp.stack([small_g[l][k] for l in range(depth)]).reshape(-1) for k in small_names]
    sizes = [p.size for p in pieces]
    chunk = SUBLANES * 1024
    total = -(-sum(sizes) // chunk) * chunk
    pack = jnp.pad(jnp.concatenate(pieces), (0, total - sum(sizes))).reshape(total // 1024, 1024)
    pack_all = all_gather([pack], name="gather_small_grads")[0]
    pack_sum = rowwise(_f_sum_parts, [(pack_all, p) for p in range(N_DEV)], [], [((1024,), F32)],
                       name="sum_small_grads")[0].reshape(-1)
    offs = [0]
    for s_ in sizes:
        offs.append(offs[-1] + s_)
    take = lambda i: pack_sum[offs[i]:offs[i + 1]]
    loss = take(0).reshape(())
    g_small = {"mod_b": take(1).reshape(mod_b.shape)}
    g_pre_full, g_post_full = take(2).reshape(depth, 3, d), take(3).reshape(depth, 3, d)
    shard = norm_pre.shape[-1]
    g_small["norm_pre"] = lax.dynamic_slice_in_dim(g_pre_full, me * shard, shard, axis=2)
    g_small["norm_post"] = lax.dynamic_slice_in_dim(g_post_full, me * shard, shard, axis=2)
    for i, k in enumerate(small_names):
        g_small[k] = take(4 + i).reshape(w_in[k].shape)

    dmod_all = pack_all.reshape(N_DEV, -1)[:, offs[1]:offs[2]].reshape(N_DEV, depth, 9 * d)
    dmod_mine = lax.dynamic_slice_in_dim(dmod_all, me * mod_cols, mod_cols, axis=2)
    sct_pad = jnp.pad(sc_all.T, ((0, 0), (0, LANES - N_DEV)))
    g_mod_w = jnp.stack([
        mm(sct_pad, jnp.pad(dmod_mine[:, l], ((0, LANES - N_DEV), (0, 0))), name=f"mod_dw{l}")
        for l in range(depth)])

    out_g, out_d, out_m, out_v = {}, {}, {}, {}
    flat = lambda a: a.reshape(-1, a.shape[-1])
    res = adamw(g_mod_w.reshape(1, -1, mod_cols), flat(mod_w), flat(m_mod_w), flat(v_mod_w), name="adamw_mod_w")
    out_g["mod_w"], out_d["mod_w"], out_m["mod_w"], out_v["mod_w"] = [r.reshape(mod_w.shape) for r in res]
    for i, k in enumerate(big):
        per_layer = [adamw(layer_parts[l][i], flat(w_in[k][l]), flat(m_in[k][l]), flat(v_in[k][l]),
                           name=f"adamw_{k}_l{l}") for l in range(depth)]
        for dct, rs in zip((out_g, out_d, out_m, out_v), zip(*per_layer)):
            dct[k] = jnp.stack([r.reshape(w_in[k].shape[1:]) for r in rs])
    small_all = ["mod_b", "norm_pre", "norm_post"] + small_names

    def pack_small(dct):
        flat = jnp.concatenate([dct[k].reshape(-1) for k in small_all])
        tot = -(-flat.size // chunk) * chunk
        return jnp.pad(flat, (0, tot - flat.size)).reshape(tot // 1024, 1024)

    res = adamw(pack_small(g_small)[None], pack_small(w_in), pack_small(m_in), pack_small(v_in),
                name="adamw_small")
    pos = 0
    for k in small_all:
        size = w_in[k].size
        for dct, r in zip((out_g, out_d, out_m, out_v), res):
            dct[k] = r.reshape(-1)[pos:pos + size].reshape(w_in[k].shape)
        pos += size

    return (loss, grad_x, *[out_g[k] for k in names], *[out_d[k] for k in names],
            *[out_m[k] for k in names], *[out_v[k] for k in names])
```

```python
import functools
import math

import jax
import jax.numpy as jnp
from jax import lax
from jax.experimental import pallas as pl
from jax.experimental.pallas import tpu as pltpu

F32 = jnp.float32
BF16 = jnp.bfloat16
MESH = pl.DeviceIdType.MESH
N_DEV = 8
LANES = 128
SUBLANES = 8
VMEM_LIMIT = 48 * 1024 * 1024

RMS_EPS = 1e-6
FFN_RES = 0.5
ADAM_LR = 0.001
ADAM_B1 = 0.9
ADAM_B2 = 0.999
ADAM_EPS = 1e-08
ADAM_WD = 0.01
ADAM_STEP = 10
GELU_C = math.sqrt(2.0 / math.pi)
GELU_A = 0.044715


def _pick(dim, target, mult=LANES):
    t = (min(dim, target) // mult) * mult
    while t >= mult:
        if dim % t == 0:
            return t
        t -= mult
    return dim


def _params(sem):
    return pltpu.CompilerParams(dimension_semantics=sem, vmem_limit_bytes=VMEM_LIMIT)


def _sigmoid(x):
    return 1.0 / (1.0 + jnp.exp(-x))


def mm(a, b, *, name, trans_a=False, trans_b=False, out_dtype=F32, tm=1024, tn=1408, tk=2816, comm=None):
    if trans_a:
        kdim, m = a.shape
    else:
        m, kdim = a.shape
    if trans_b:
        n, kb = b.shape
    else:
        kb, n = b.shape
    assert kdim == kb, (a.shape, b.shape)
    tm, tn, tk = _pick(m, tm), _pick(n, tn), _pick(kdim, tk)
    gm, gn, nk = m // tm, n // tn, kdim // tk
    dims = (((0 if trans_a else 1,), (1 if trans_b else 0,)), ((), ()))
    host = _Hosted(comm, 2, 1, 1 if nk > 1 else 0)

    def body(*refs):
        (a_ref, b_ref, o_ref, *acc), crefs = host.split(refs)
        i, j, k = pl.program_id(0), pl.program_id(1), pl.program_id(2)
        host.phase("start", (i == 0) & (j == 0) & (k == 0), crefs)
        prod = lax.dot_general(a_ref[...].astype(BF16), b_ref[...].astype(BF16), dims,
                               preferred_element_type=F32)
        if nk == 1:
            o_ref[...] = prod.astype(out_dtype)
        else:
            acc_ref, = acc

            @pl.when(k == 0)
            def _():
                acc_ref[...] = prod

            @pl.when((k > 0) & (k < nk - 1))
            def _():
                acc_ref[...] += prod

            @pl.when(k == nk - 1)
            def _():
                o_ref[...] = (acc_ref[...] + prod).astype(out_dtype)

        last = (i == gm - 1) & (j == gn - 1) & (k == nk - 1)
        host.phase("mid", last, crefs)
        host.phase("finish", last, crefs)

    a_spec = (pl.BlockSpec((tk, tm), lambda i, j, k: (k, i)) if trans_a
              else pl.BlockSpec((tm, tk), lambda i, j, k: (i, k)))
    b_spec = (pl.BlockSpec((tn, tk), lambda i, j, k: (j, k)) if trans_b
              else pl.BlockSpec((tk, tn), lambda i, j, k: (k, j)))
    in_specs, out_specs, out_shape, scratch, extra = host.specs(
        [a_spec, b_spec], [pl.BlockSpec((tm, tn), lambda i, j, k: (i, j))],
        [jax.ShapeDtypeStruct((m, n), out_dtype)], [pltpu.VMEM((tm, tn), F32)] if nk > 1 else [])
    res = pl.pallas_call(
        body, name=name, grid=(gm, gn, nk),
        in_specs=in_specs, out_specs=out_specs, out_shape=out_shape, scratch_shapes=scratch,
        compiler_params=_params(("arbitrary", "arbitrary", "arbitrary")),
    )(a, b, *extra)
    return res if comm else res[0]


def mm_blockdiag(a, b, *, name, a_cb0=0, out_dtype=F32, tm=512):
    m = a.shape[0]
    g_n, ka, nb = b.shape
    tm = _pick(m, tm)

    def body(a_ref, b_ref, o_ref):
        o_ref[...] = jnp.dot(a_ref[...].astype(BF16), b_ref[...].astype(BF16),
                             preferred_element_type=F32).astype(out_dtype)

    return pl.pallas_call(
        body, name=name, grid=(m // tm, g_n),
        in_specs=[pl.BlockSpec((tm, ka), lambda i, g: (i, a_cb0 + g)),
                  pl.BlockSpec((None, ka, nb), lambda i, g: (g, 0, 0))],
        out_specs=pl.BlockSpec((tm, nb), lambda i, g: (i, g)),
        out_shape=jax.ShapeDtypeStruct((m, g_n * nb), out_dtype),
        compiler_params=_params(("parallel", "parallel")),
    )(a, b)


def mm_blockdiag_tn(a, b, *, name, g_n, ka, kb, a_cb0=0, b_cb0=0, tk=512):
    rows = a.shape[0]
    tk = _pick(rows, tk)
    nk = rows // tk

    def body(a_ref, b_ref, o_ref):
        k = pl.program_id(1)

        @pl.when(k == 0)
        def _():
            o_ref[...] = jnp.zeros_like(o_ref)

        o_ref[...] += lax.dot_general(a_ref[...].astype(BF16), b_ref[...].astype(BF16),
                                      (((0,), (0,)), ((), ())), preferred_element_type=F32)

    return pl.pallas_call(
        body, name=name, grid=(g_n, nk),
        in_specs=[pl.BlockSpec((tk, ka), lambda g, k: (k, a_cb0 + g)),
                  pl.BlockSpec((tk, kb), lambda g, k: (k, b_cb0 + g))],
        out_specs=pl.BlockSpec((None, ka, kb), lambda g, k: (g, 0, 0)),
        out_shape=jax.ShapeDtypeStruct((g_n, ka, kb), F32),
        compiler_params=_params(("parallel", "arbitrary")),
    )(a, b)


def rowwise(fn, rows, vecs, outs, reds=(), *, name, tm=256):
    metas = []
    for r in rows:
        if isinstance(r, tuple) and len(r) == 3:
            metas.append(("col", r[0], r[1], r[2]))
        elif isinstance(r, tuple):
            metas.append(("lead", r[0], r[0].shape[2], r[1]))
        else:
            metas.append(("full", r, r.shape[1], 0))
    n_rows = metas[0][1].shape[1] if metas[0][0] == "lead" else metas[0][1].shape[0]
    rc = 16 if n_rows % 16 == 0 else (SUBLANES if n_rows % SUBLANES == 0 else n_rows)
    tm = _pick(n_rows, tm, rc)
    n_inner = tm // rc
    nr, nv, no = len(metas), len(vecs), len(outs)

    def body(*refs):
        row_refs, vec_refs = refs[:nr], refs[nr:nr + nv]
        out_refs, red_refs = refs[nr + nv:nr + nv + no], refs[nr + nv + no:]
        if reds:
            @pl.when(pl.program_id(0) == 0)
            def _():
                for rr in red_refs:
                    rr[...] = jnp.zeros_like(rr)
        vec_vals = [v[...] for v in vec_refs]

        def step(s, carry):
            r0 = pl.multiple_of(s * rc, rc)
            vals = [ref[pl.ds(r0, rc), :] for ref in row_refs]
            res = fn(*vals, *vec_vals)
            if not isinstance(res, (tuple, list)):
                res = (res,)
            for o_ref, (widths, dt), val in zip(out_refs, outs, res[:no]):
                pieces = val if isinstance(val, (tuple, list)) else (val,)
                off = 0
                for w_, piece in zip(widths, pieces):
                    o_ref[pl.ds(r0, rc), off:off + w_] = piece.astype(dt)
                    off += w_
            for rr, val in zip(red_refs, res[no:]):
                rr[...] += val
            return carry

        lax.fori_loop(0, n_inner, step, 0, unroll=min(n_inner, 4))

    in_specs = []
    for kind, arr, w_, idx in metas:
        if kind == "col":
            in_specs.append(pl.BlockSpec((tm, w_), functools.partial(lambda i, cb: (i, cb), cb=idx)))
        elif kind == "lead":
            in_specs.append(pl.BlockSpec((None, tm, w_), functools.partial(lambda i, p: (p, i, 0), p=idx)))
        else:
            in_specs.append(pl.BlockSpec((tm, w_), lambda i: (i, 0)))
    for v in vecs:
        in_specs.append(pl.BlockSpec(v.shape, lambda i: (0, 0)))
    out_specs = [pl.BlockSpec((tm, sum(ws)), lambda i: (i, 0)) for ws, _ in outs]
    out_specs += [pl.BlockSpec((1, w_), lambda i: (0, 0)) for w_ in reds]
    out_shape = [jax.ShapeDtypeStruct((n_rows, sum(ws)), dt) for ws, dt in outs]
    out_shape += [jax.ShapeDtypeStruct((1, w_), F32) for w_ in reds]
    res = pl.pallas_call(
        body, name=name, grid=(n_rows // tm,),
        in_specs=in_specs, out_specs=out_specs, out_shape=out_shape,
        compiler_params=_params(("arbitrary",)),
    )(*[m[1] for m in metas], *vecs)
    return res


def _rms(x):
    return lax.rsqrt(jnp.mean(x * x, axis=-1, keepdims=True) + RMS_EPS)


def _colsum(x):
    return jnp.sum(x, axis=0, keepdims=True)


def _f_silu(c):
    return c * _sigmoid(c)


def _f_pre(x, a, sh):
    return (x * _rms(x)) * a + sh


def _f_post_add(x, y, bv):
    return x + (y * _rms(y)) * bv


def _f_post_bwd(dxo, y, bv):
    ry = _rms(y)
    yn = y * ry
    dyn = dxo * bv
    dy = ry * (dyn - yn * jnp.mean(dyn * yn, axis=-1, keepdims=True))
    return dy, _colsum(dxo * yn)


def _f_pre_bwd(dxo, dh, x, a):
    r = _rms(x)
    xn = x * r
    dxn = dh * a
    dx = dxo + r * (dxn - xn * jnp.mean(dxn * xn, axis=-1, keepdims=True))
    return dx, _colsum(dh * xn), _colsum(dh)


def _f_swiglu(g, u):
    g = g.astype(F32)
    return (g * _sigmoid(g)) * u.astype(F32)


def _f_swiglu_bwd(g, u, dm):
    g, u, dm = g.astype(F32), u.astype(F32), dm.astype(F32)
    sg = _sigmoid(g)
    dg = dm * u * (sg * (1.0 + g * (1.0 - sg)))
    du = dm * (g * sg)
    return ((dg, du),)


def _gelu_t(x):
    return jnp.tanh(GELU_C * (x + GELU_A * x * x * x))


def _f_gelu_in(y0, u, dvec):
    y = y0 + dvec * u
    return 0.5 * y * (1.0 + _gelu_t(y)), y


def _f_gelu_bwd(dge, y, u):
    t = _gelu_t(y)
    dy = dge * (0.5 * (1.0 + t) + 0.5 * y * (1.0 - t * t) * GELU_C * (1.0 + 3.0 * GELU_A * y * y))
    return dy, _colsum(dy * u)


def _f_du_fin(du0, dys, dvec):
    return du0 + dvec * dys.astype(F32)


def _f_merge(zv, zg, yb, ga, gb):
    return _sigmoid(ga) * (zv * _sigmoid(zg)) + _sigmoid(gb) * yb


def _f_merge_bwd(dmg, zv, zg, yb, ga, gb):
    sa, sb, sz = _sigmoid(ga), _sigmoid(gb), _sigmoid(zg)
    ya = zv * sz
    dya = dmg * sa
    dga = dmg * ya * sa * (1.0 - sa)
    dyb = dmg * sb
    dgb = dmg * yb * sb * (1.0 - sb)
    dzv = dya * sz
    dzg = dya * zv * sz * (1.0 - sz)
    return (dzv, dzg), dyb, (dga, dgb)


def _f_sum_parts(*parts):
    acc = parts[0].astype(F32)
    for p in parts[1:]:
        acc = acc + p.astype(F32)
    return acc


def _f_adamw(*args):
    parts, (w, m, v) = args[:-3], args[-3:]
    g = _f_sum_parts(*parts)
    m = ADAM_B1 * m + (1.0 - ADAM_B1) * g
    v = ADAM_B2 * v + (1.0 - ADAM_B2) * (g * g)
    m_hat = m / (1.0 - ADAM_B1 ** ADAM_STEP)
    v_hat = v / (1.0 - ADAM_B2 ** ADAM_STEP)
    delta = -ADAM_LR * (m_hat / (jnp.sqrt(v_hat) + ADAM_EPS) + ADAM_WD * w)
    return g, delta, m, v


def adamw(parts3, w, m, v, *, name):
    c = w.shape[1]
    rows = [(parts3, p) for p in range(parts3.shape[0])] + [w, m, v]
    return rowwise(_f_adamw, rows, [], [((c,), F32)] * 4, name=name)


def _tri_dot(tri, x):
    x1 = x.astype(BF16)
    r1 = x - x1.astype(F32)
    x2 = r1.astype(BF16)
    x3 = (r1 - x2.astype(F32)).astype(BF16)
    dot = functools.partial(jnp.dot, preferred_element_type=F32)
    return dot(tri, x1) + dot(tri, x2) + dot(tri, x3)


def cum_fwd(proj, fb, *, col_block, name, t=256):
    n = proj.shape[0]
    t = _pick(n, t, SUBLANES)

    def body(f_ref, fb_ref, cum_ref, car_ref):
        @pl.when(pl.program_id(0) == 0)
        def _():
            car_ref[...] = jnp.zeros_like(car_ref)

        x = f_ref[...] + fb_ref[...]
        lf = jnp.minimum(x, 0.0) - jnp.log(1.0 + jnp.exp(-jnp.abs(x)))
        r = lax.broadcasted_iota(jnp.int32, (t, t), 0)
        c = lax.broadcasted_iota(jnp.int32, (t, t), 1)
        cs = _tri_dot((c <= r).astype(BF16), lf) + car_ref[0:1, :]
        cum_ref[...] = cs
        car_ref[0:1, :] = cs[t - 1:t, :]

    return pl.pallas_call(
        body, name=name, grid=(n // t,),
        in_specs=[pl.BlockSpec((t, LANES), lambda i: (i, col_block)),
                  pl.BlockSpec((1, LANES), lambda i: (0, 0))],
        out_specs=pl.BlockSpec((t, LANES), lambda i: (i, 0)),
        out_shape=jax.ShapeDtypeStruct((n, LANES), F32),
        scratch_shapes=[pltpu.VMEM((SUBLANES, LANES), F32)],
        compiler_params=_params(("arbitrary",)),
    )(proj, fb)


def cum_bwd(dcum, proj, fb, *, col_block, name, t=256):
    n = proj.shape[0]
    t = _pick(n, t, SUBLANES)
    nb = n // t

    def body(dc_ref, f_ref, fb_ref, df_ref, dfb_ref, car_ref):
        @pl.when(pl.program_id(0) == 0)
        def _():
            car_ref[...] = jnp.zeros_like(car_ref)
            dfb_ref[...] = jnp.zeros_like(dfb_ref)

        r = lax.broadcasted_iota(jnp.int32, (t, t), 0)
        c = lax.broadcasted_iota(jnp.int32, (t, t), 1)
        dl = _tri_dot((c >= r).astype(BF16), dc_ref[...]) + car_ref[0:1, :]
        car_ref[0:1, :] = dl[0:1, :]
        x = f_ref[...] + fb_ref[...]
        df = dl * (1.0 / (1.0 + jnp.exp(x)))
        df_ref[...] = df
        dfb_ref[...] += _colsum(df)

    return pl.pallas_call(
        body, name=name, grid=(nb,),
        in_specs=[pl.BlockSpec((t, LANES), lambda i: (nb - 1 - i, 0)),
                  pl.BlockSpec((t, LANES), lambda i: (nb - 1 - i, col_block)),
                  pl.BlockSpec((1, LANES), lambda i: (0, 0))],
        out_specs=[pl.BlockSpec((t, LANES), lambda i: (nb - 1 - i, 0)),
                   pl.BlockSpec((1, LANES), lambda i: (0, 0))],
        out_shape=[jax.ShapeDtypeStruct((n, LANES), F32), jax.ShapeDtypeStruct((1, LANES), F32)],
        scratch_shapes=[pltpu.VMEM((SUBLANES, LANES), F32)],
        compiler_params=_params(("arbitrary",)),
    )(dcum, proj, fb)


_NT =(((1,), (1,)), ((), ()))
_TN = (((0,), (0,)), ((), ()))


def _causal_keep(t):
    return lax.broadcasted_iota(jnp.int32, (t, t), 1) <= lax.broadcasted_iota(jnp.int32, (t, t), 0)


def attn_fwd(q, k, v, crow, *, name, t=512, hb=8, comm=None):
    h_n, n, dh = q.shape
    hb = min(hb, h_n)
    t = _pick(n, t)
    nb = n // t
    ng = h_n // hb
    host = _Hosted(comm, 4, 2, 3)

    def body(*refs):
        (q_ref, k_ref, v_ref, cr_ref, o_ref, lse_ref, m_sc, l_sc, acc_sc), crefs = host.split(refs)
        g, i, j = pl.program_id(0), pl.program_id(1), pl.program_id(2)
        host.phase("start", (g == 0) & (i == 0) & (j == 0), crefs)
        host.phase("mid", (g == ng - 1) & (i == (3 * nb) // 4) & (j == 0), crefs)

        @pl.when(j == 0)
        def _():
            m_sc[...] = jnp.full_like(m_sc, -jnp.inf)
            l_sc[...] = jnp.zeros_like(l_sc)
            acc_sc[...] = jnp.zeros_like(acc_sc)

        def update(diagonal):
            keep = _causal_keep(t) if diagonal else None
            for h in range(hb):
                s = lax.dot_general(q_ref[h], k_ref[h], _NT, preferred_element_type=F32) - cr_ref[h]
                if diagonal:
                    s = jnp.where(keep, s, -jnp.inf)
                m_prev = m_sc[h]
                m_new = jnp.maximum(m_prev, jnp.max(s, axis=-1, keepdims=True))
                p = jnp.exp(s - m_new)
                alpha = jnp.exp(m_prev - m_new)
                l_sc[h] = alpha * l_sc[h] + jnp.sum(p, axis=-1, keepdims=True)
                p_hi = p.astype(BF16)
                p_lo = (p - p_hi.astype(F32)).astype(BF16)
                vv = v_ref[h]
                acc_sc[h] = (alpha * acc_sc[h] + jnp.dot(p_hi, vv, preferred_element_type=F32)
                             + jnp.dot(p_lo, vv, preferred_element_type=F32))
                m_sc[h] = m_new

        @pl.when(j < i)
        def _():
            update(False)

        @pl.when(j == i)
        def _():
            update(True)

        @pl.when(j == nb - 1)
        def _():
            o_ref[...] = acc_sc[...] / l_sc[...]
            lse_ref[...] = m_sc[...] + jnp.log(l_sc[...])

        host.phase("finish", (g == ng - 1) & (i == nb - 1) & (j == nb - 1), crefs)

    qspec = pl.BlockSpec((hb, t, dh), lambda g, i, j: (g, i, 0))
    kspec = pl.BlockSpec((hb, t, dh), lambda g, i, j: (g, jnp.minimum(j, i), 0))
    colspec = pl.BlockSpec((hb, t, 1), lambda g, i, j: (g, i, 0))
    in_specs, out_specs, out_shape, scratch, extra = host.specs(
        [qspec, kspec, kspec, pl.BlockSpec((hb, 1, t), lambda g, i, j: (g, 0, jnp.minimum(j, i)))],
        [qspec, colspec],
        [jax.ShapeDtypeStruct((h_n, n, dh), F32), jax.ShapeDtypeStruct((h_n, n, 1), F32)],
        [pltpu.VMEM((hb, t, 1), F32), pltpu.VMEM((hb, t, 1), F32), pltpu.VMEM((hb, t, dh), F32)])
    return pl.pallas_call(
        body, name=name, grid=(ng, nb, nb),
        in_specs=in_specs, out_specs=out_specs, out_shape=out_shape, scratch_shapes=scratch,
        compiler_params=_params(("arbitrary", "arbitrary", "arbitrary")),
    )(q, k, v, crow, *extra)


def _f_rowdot(a, b):
    return jnp.sum(a.astype(F32) * b, axis=-1, keepdims=True)


def attn_bwd(q, k, v, do, o, lse, crow, *, scale, tag, t=512, hb=2, comm=None):
    h_n, n, dh = q.shape
    hb = min(hb, h_n)
    t = _pick(n, t)
    nb = n // t
    dob = do.astype(BF16)
    delta = rowwise(_f_rowdot, [dob.reshape(h_n * n, dh), o.reshape(h_n * n, dh)], [], [((1,), F32)],
                    name=f"attn_delta_{tag}")[0].reshape(h_n, n, 1)

    ng = h_n // hb
    host = _Hosted(comm, 7, 4, 3)

    def body(*refs):
        (q_ref, k_ref, v_ref, do_ref, lse_ref, dl_ref, cr_ref,
         dq_ref, dk_ref, dv_ref, dcr_ref, dk_acc, dv_acc, dcr_acc), crefs = host.split(refs)
        g, j, i = pl.program_id(0), pl.program_id(1), pl.program_id(2)
        host.phase("start", (g == 0) & (j == 0) & (i == 0), crefs)

        @pl.when((j == 0) & (i == 0))
        def _():
            dq_ref[...] = jnp.zeros_like(dq_ref)

        @pl.when(i == 0)
        def _():
            dk_acc[...] = jnp.zeros_like(dk_acc)
            dv_acc[...] = jnp.zeros_like(dv_acc)
            dcr_acc[...] = jnp.zeros_like(dcr_acc)

        def update(diagonal):
            keep = _causal_keep(t) if diagonal else None
            r0 = pl.multiple_of(i * t, t)
            for h in range(hb):
                qv, kv, vv, dov = q_ref[h], k_ref[h], v_ref[h], do_ref[h]
                s = lax.dot_general(qv, kv, _NT, preferred_element_type=F32) - cr_ref[h]
                if diagonal:
                    s = jnp.where(keep, s, -jnp.inf)
                p = jnp.exp(s - lse_ref[h])
                dv_acc[h] += lax.dot_general(p.astype(BF16), dov, _TN, preferred_element_type=F32)
                dp = lax.dot_general(dov, vv, _NT, preferred_element_type=F32)
                ds = p * (dp - dl_ref[h])
                dcr_acc[h] -= _colsum(ds)
                dsb = ds.astype(BF16)
                dk_acc[h] += lax.dot_general(dsb, qv, _TN, preferred_element_type=F32)
                dq_ref[h, pl.ds(r0, t), :] += jnp.dot(dsb, kv, preferred_element_type=F32) * scale

        @pl.when(i > j)
        def _():
            update(False)

        @pl.when(i == j)
        def _():
            update(True)

        @pl.when(i == nb - 1)
        def _():
            dk_ref[...] = dk_acc[...]
            dv_ref[...] = dv_acc[...]
            dcr_ref[...] = dcr_acc[...]

        host.phase("finish", (g == ng - 1) & (j == nb - 1) & (i == nb - 1), crefs)

    qspec = pl.BlockSpec((hb, t, dh), lambda g, j, i: (g, jnp.maximum(i, j), 0))
    colspec = pl.BlockSpec((hb, t, 1), lambda g, j, i: (g, jnp.maximum(i, j), 0))
    kspec = pl.BlockSpec((hb, t, dh), lambda g, j, i: (g, j, 0))
    rowspec = pl.BlockSpec((hb, 1, t), lambda g, j, i: (g, 0, j))
    in_specs, out_specs, out_shape, scratch, extra = host.specs(
        [qspec, kspec, kspec, qspec, colspec, colspec, rowspec],
        [pl.BlockSpec((hb, n, dh), lambda g, j, i: (g, 0, 0)), kspec, kspec, rowspec],
        [jax.ShapeDtypeStruct((h_n, n, dh), F32)] * 3 + [jax.ShapeDtypeStruct((h_n, 1, n), F32)],
        [pltpu.VMEM((hb, t, dh), F32), pltpu.VMEM((hb, t, dh), F32), pltpu.VMEM((hb, 1, t), F32)])
    return pl.pallas_call(
        body, name=f"attn_bwd_{tag}", grid=(ng, nb, nb),
        in_specs=in_specs, out_specs=out_specs, out_shape=out_shape, scratch_shapes=scratch,
        compiler_params=_params(("arbitrary", "arbitrary", "arbitrary")),
    )(q, k, v, dob, lse, delta, crow, *extra)


SCAN_STEPS = (1, 2, 4)


def ssm_scan(x, tab, *, reverse, name, s_prev=None, tt=512):
    n, width = x.shape
    nc, _, hw = tab.shape
    cw = 2 * hw
    assert width == nc * cw
    tt = _pick(n, tt, SUBLANES)
    nt = n // tt
    ng = tt // SUBLANES
    with_grad = s_prev is not None

    def body(*refs):
        if with_grad:
            x_ref, s_ref, tab_ref, o_ref, g_ref, car_ref = refs
        else:
            x_ref, tab_ref, o_ref, car_ref = refs

        @pl.when(pl.program_id(1) == 0)
        def _():
            car_ref[...] = jnp.zeros_like(car_ref)
            if with_grad:
                g_ref[...] = jnp.zeros_like(g_ref)

        q_re, q_im = tab_ref[0:8, :], tab_ref[8:16, :]
        p_re = [tab_ref[16 + i:17 + i, :] for i in range(3)]
        p_im = [tab_ref[24 + i:25 + i, :] for i in range(3)]
        row = lax.broadcasted_iota(jnp.int32, (SUBLANES, hw), 0)

        def group(gi, carry):
            c_re, c_im = carry
            g = (ng - 1 - gi) if reverse else gi
            r0 = pl.multiple_of(g * SUBLANES, SUBLANES)
            xr = x_ref[pl.ds(r0, SUBLANES), 0:hw]
            xi = x_ref[pl.ds(r0, SUBLANES), hw:cw]
            for i, d in enumerate(SCAN_STEPS):
                if reverse:
                    shift, keep = SUBLANES - d, row < SUBLANES - d
                else:
                    shift, keep = d, row >= d
                sr = jnp.where(keep, pltpu.roll(xr, shift, 0), 0.0)
                si = jnp.where(keep, pltpu.roll(xi, shift, 0), 0.0)
                xr, xi = (xr + p_re[i] * sr - p_im[i] * si,
                          xi + p_re[i] * si + p_im[i] * sr)
            xr, xi = (xr + q_re * c_re - q_im * c_im,
                      xi + q_re * c_im + q_im * c_re)
            o_ref[pl.ds(r0, SUBLANES), 0:hw] = xr
            o_ref[pl.ds(r0, SUBLANES), hw:cw] = xi
            if with_grad:
                nr = jnp.where(row < SUBLANES - 1, pltpu.roll(xr, SUBLANES - 1, 0), c_re)
                ni = jnp.where(row < SUBLANES - 1, pltpu.roll(xi, SUBLANES - 1, 0), c_im)
                sr = s_ref[pl.ds(r0, SUBLANES), 0:hw]
                si = s_ref[pl.ds(r0, SUBLANES), hw:cw]
                g_ref[:, 0:hw] += nr * sr + ni * si
                g_ref[:, hw:cw] += ni * sr - nr * si
            if reverse:
                return xr[0:1, :], xi[0:1, :]
            return xr[SUBLANES - 1:SUBLANES, :], xi[SUBLANES - 1:SUBLANES, :]

        c_re, c_im = lax.fori_loop(0, ng, group, (car_ref[0:1, 0:hw], car_ref[0:1, hw:cw]),
                                   unroll=min(ng, 4))
        car_ref[0:1, 0:hw] = c_re
        car_ref[0:1, hw:cw] = c_im

    if reverse:
        xspec = pl.BlockSpec((tt, cw), lambda c, t: (nt - 1 - t, c))
    else:
        xspec = pl.BlockSpec((tt, cw), lambda c, t: (t, c))
    tspec = pl.BlockSpec((None, 32, hw), lambda c, t: (c, 0, 0))
    in_specs = [xspec, xspec, tspec] if with_grad else [xspec, tspec]
    out_specs = [xspec]
    out_shape = [jax.ShapeDtypeStruct((n, width), F32)]
    if with_grad:
        out_specs.append(pl.BlockSpec((None, SUBLANES, cw), lambda c, t: (c, 0, 0)))
        out_shape.append(jax.ShapeDtypeStruct((nc, SUBLANES, cw), F32))
    operands = (x, s_prev, tab) if with_grad else (x, tab)
    return pl.pallas_call(
        body, name=name, grid=(nc, nt),
        in_specs=in_specs, out_specs=out_specs, out_shape=out_shape,
        scratch_shapes=[pltpu.VMEM((SUBLANES, cw), F32)],
        compiler_params=_params(("parallel", "arbitrary")),
    )(*operands)


def _slot(pos):
    return 4 * pos[0] + 2 * pos[1] + pos[2]


def _comm_scratch(n):
    return [pltpu.SemaphoreType.DMA((7 * n,)), pltpu.SemaphoreType.DMA((7 * n,)), pltpu.SemaphoreType.DMA((n,))]


def _gather_copies(ins, outs, sems):
    send_sems, recv_sems, local_sems = sems
    n = len(ins)
    x, y, c = lax.axis_index("x"), lax.axis_index("y"), lax.axis_index("c")
    me, sibling = (x, y, c), (x, y, 1 - c)
    chips = [(1 - x, y), (x, 1 - y), (1 - x, 1 - y)]

    def copy(t, k, block, to, src=None):
        dst = outs[t].at[_slot(block)]
        return pltpu.make_async_remote_copy(
            src_ref=dst if src is None else src, dst_ref=dst,
            send_sem=send_sems.at[7 * t + k], recv_sem=recv_sems.at[7 * t + k],
            device_id=to, device_id_type=MESH)

    jc = list(enumerate(chips))
    return dict(
        mine=[pltpu.make_async_copy(ins[t], outs[t].at[_slot(me)], local_sems.at[t]) for t in range(n)],
        first=[cp for t in range(n) for cp in
               [copy(t, 0, me, sibling, src=ins[t])] + [copy(t, 1 + j, me, (*chip, c), src=ins[t]) for j, chip in jc]],
        arrive=[copy(t, 1 + j, (*chip, c), me) for t in range(n) for j, chip in jc],
        passed=[copy(t, 4 + j, (*chip, c), sibling) for t in range(n) for j, chip in jc],
        from_sibling=[cp for t in range(n) for cp in
                      [copy(t, 0, sibling, me)] + [copy(t, 4 + j, (*chip, 1 - c), me) for j, chip in jc]])


def _gather_start(ins, outs, sems):
    cps = _gather_copies(ins, outs, sems)
    for cp in cps["mine"] + cps["first"]:
        cp.start()


def _gather_forward(ins, outs, sems):
    cps = _gather_copies(ins, outs, sems)
    for arrived, onward in zip(cps["arrive"], cps["passed"]):
        arrived.wait_recv()
        onward.start()


def _gather_finish(ins, outs, sems):
    cps = _gather_copies(ins, outs, sems)
    for cp in cps["from_sibling"]:
        cp.wait_recv()
    for cp in cps["first"] + cps["passed"]:
        cp.wait_send()
    for cp in cps["mine"]:
        cp.wait()


def gather_comm(arrs):
    return dict(ins=list(arrs), out_shape=[jax.ShapeDtypeStruct((N_DEV,) + a.shape, a.dtype) for a in arrs],
                scratch=_comm_scratch(len(arrs)), start=_gather_start, mid=_gather_forward, finish=_gather_finish)


def _exchange_copies(ins, outs, sems):
    send_sems, recv_sems, local_sems = sems
    n = len(ins)
    me = (lax.axis_index("x"), lax.axis_index("y"), lax.axis_index("c"))
    peers = []
    for k in range(1, N_DEV):
        flip = ((k >> 2) & 1, (k >> 1) & 1, k & 1)
        peers.append(tuple(1 - p if f else p for p, f in zip(me, flip)))

    def copy(t, k, peer, dst_slot):
        return pltpu.make_async_remote_copy(
            src_ref=ins[t].at[_slot(peer)], dst_ref=outs[t].at[dst_slot],
            send_sem=send_sems.at[7 * t + k], recv_sem=recv_sems.at[7 * t + k],
            device_id=peer, device_id_type=MESH)

    return dict(
        mine=[pltpu.make_async_copy(ins[t].at[_slot(me)], outs[t].at[_slot(me)], local_sems.at[t])
              for t in range(n)],
        send=[copy(t, k, peer, _slot(me)) for t in range(n) for k, peer in enumerate(peers)],
        both=[copy(t, k, peer, _slot(peer)) for t in range(n) for k, peer in enumerate(peers)])


def _exchange_start(ins, outs, sems):
    cps = _exchange_copies(ins, outs, sems)
    for cp in cps["mine"] + cps["send"]:
        cp.start()


def _exchange_finish(ins, outs, sems):
    cps = _exchange_copies(ins, outs, sems)
    for cp in cps["both"]:
        cp.wait()
    for cp in cps["mine"]:
        cp.wait()


def exchange_comm(arrs):
    return dict(ins=list(arrs), out_shape=[jax.ShapeDtypeStruct(a.shape, a.dtype) for a in arrs],
                scratch=_comm_scratch(len(arrs)), start=_exchange_start, mid=None, finish=_exchange_finish)


def run_comm(comm, *, name):
    n_in, n_out = len(comm["ins"]), len(comm["out_shape"])

    def body(*refs):
        ins, outs, sems = refs[:n_in], refs[n_in:n_in + n_out], refs[n_in + n_out:]
        comm["start"](ins, outs, sems)
        if comm["mid"] is not None:
            comm["mid"](ins, outs, sems)
        comm["finish"](ins, outs, sems)

    any_spec = pl.BlockSpec(memory_space=pl.ANY)
    return pl.pallas_call(
        body, name=name, in_specs=[any_spec] * n_in, out_specs=[any_spec] * n_out,
        out_shape=comm["out_shape"], scratch_shapes=comm["scratch"],
    )(*comm["ins"])


class _Hosted:
    def __init__(self, comm, n_in, n_out, n_scratch):
        self.comm = comm
        self.n_ci = len(comm["ins"]) if comm else 0
        self.n_co = len(comm["out_shape"]) if comm else 0
        self.n_in, self.n_out, self.n_scratch = n_in, n_out, n_scratch

    def split(self, refs):
        a = self.n_in
        b = a + self.n_ci
        c = b + self.n_out
        e = c + self.n_co
        f = e + self.n_scratch
        return refs[:a] + refs[b:c] + refs[e:f], (refs[a:b], refs[c:e], refs[f:])

    def phase(self, which, when, crefs):
        fn = self.comm[which] if self.comm else None
        if fn is not None:
            pl.when(when)(lambda: fn(*crefs))

    def specs(self, in_specs, out_specs, out_shape, scratch):
        any_spec = pl.BlockSpec(memory_space=pl.ANY)
        if not self.comm:
            return in_specs, out_specs, out_shape, scratch, ()
        return (in_specs + [any_spec] * self.n_ci, out_specs + [any_spec] * self.n_co,
                out_shape + self.comm["out_shape"], scratch + self.comm["scratch"], tuple(self.comm["ins"]))


def all_gather(arrs, *, name):
    return run_comm(gather_comm(arrs), name=name)


def _discretise(a_re, a_im, log_dt, b_re, b_im):
    ar = jnp.minimum(a_re, -1e-4)
    dt = jnp.exp(log_dt)[:, None]
    e, ph = ar * dt, a_im * dt
    mag = jnp.exp(e)
    lr, li = mag * jnp.cos(ph), mag * jnp.sin(ph)
    den = ar * ar + a_im * a_im
    nr, ni = lr - 1.0, li
    cr = (nr * ar + ni * a_im) / den
    ci = (ni * ar - nr * a_im) / den
    bb_re = cr[..., None] * b_re - ci[..., None] * b_im
    bb_im = cr[..., None] * b_im + ci[..., None] * b_re
    return e, ph, bb_re, bb_im


def _lam_pow(e, ph, k, conj):
    mag = jnp.exp(k * e)
    return mag * jnp.cos(k * ph), (-1.0 if conj else 1.0) * mag * jnp.sin(k * ph)


def _scan_table(e, ph, nc, reverse):
    hw = e.size // nc
    e, ph = e.reshape(nc, 1, hw), ph.reshape(nc, 1, hw)
    j = jnp.arange(SUBLANES, dtype=F32).reshape(1, SUBLANES, 1)
    kq = (SUBLANES - j) if reverse else (j + 1.0)
    q_re, q_im = _lam_pow(e, ph, kq, reverse)
    kp = jnp.array(SCAN_STEPS + (0,) * 5, F32).reshape(1, SUBLANES, 1)
    p_re, p_im = _lam_pow(e, ph, kp, reverse)
    return jnp.concatenate([q_re, q_im, p_re, p_im], axis=1)


def _blockdiag(m, nc):
    g, a, b = m.shape
    gc = g // nc
    m = m.reshape(nc, gc, a, b)
    eye = jnp.eye(gc, dtype=m.dtype)
    return jnp.einsum("cgab,gh->cgahb", m, eye).reshape(nc, gc * a, gc * b)


def _blockdiag_take(m, g):
    nc = m.shape[0]
    gc = g // nc
    a, b = m.shape[1] // gc, m.shape[2] // gc
    m = m.reshape(nc, gc, a, gc, b)
    eye = jnp.eye(gc, dtype=m.dtype)
    return jnp.einsum("cgahb,gh->cgab", m, eye).reshape(g, a, b)


def kernel(x, c, mod_w, mod_b, norm_pre, norm_post, ffn_w_in, ffn_w_out, mix_w_in, forget_b, ssm_a_re, ssm_a_im, ssm_log_dt, ssm_b_re, ssm_b_im, ssm_c_re, ssm_c_im, ssm_d, glu_w, attn_w_out, mix_w_out, loss_target, m_mod_w, m_mod_b, m_norm_pre, m_norm_post, m_ffn_w_in, m_ffn_w_out, m_mix_w_in, m_forget_b, m_ssm_a_re, m_ssm_a_im, m_ssm_log_dt, m_ssm_b_re, m_ssm_b_im, m_ssm_c_re, m_ssm_c_im, m_ssm_d, m_glu_w, m_attn_w_out, m_mix_w_out, v_mod_w, v_mod_b, v_norm_pre, v_norm_post, v_ffn_w_in, v_ffn_w_out, v_mix_w_in, v_forget_b, v_ssm_a_re, v_ssm_a_im, v_ssm_log_dt, v_ssm_b_re, v_ssm_b_im, v_ssm_c_re, v_ssm_c_im, v_ssm_d, v_glu_w, v_attn_w_out, v_mix_w_out):
    names = ["mod_w", "mod_b", "norm_pre", "norm_post", "ffn_w_in", "ffn_w_out", "mix_w_in", "forget_b",
             "ssm_a_re", "ssm_a_im", "ssm_log_dt", "ssm_b_re", "ssm_b_im", "ssm_c_re", "ssm_c_im", "ssm_d",
             "glu_w", "attn_w_out", "mix_w_out"]
    w_in = dict(zip(names, [mod_w, mod_b, norm_pre, norm_post, ffn_w_in, ffn_w_out, mix_w_in, forget_b,
                            ssm_a_re, ssm_a_im, ssm_log_dt, ssm_b_re, ssm_b_im, ssm_c_re, ssm_c_im, ssm_d,
                            glu_w, attn_w_out, mix_w_out]))
    m_in = dict(zip(names, [m_mod_w, m_mod_b, m_norm_pre, m_norm_post, m_ffn_w_in, m_ffn_w_out, m_mix_w_in,
                            m_forget_b, m_ssm_a_re, m_ssm_a_im, m_ssm_log_dt, m_ssm_b_re, m_ssm_b_im,
                            m_ssm_c_re, m_ssm_c_im, m_ssm_d, m_glu_w, m_attn_w_out, m_mix_w_out]))
    v_in = dict(zip(names, [v_mod_w, v_mod_b, v_norm_pre, v_norm_post, v_ffn_w_in, v_ffn_w_out, v_mix_w_in,
                            v_forget_b, v_ssm_a_re, v_ssm_a_im, v_ssm_log_dt, v_ssm_b_re, v_ssm_b_im,
                            v_ssm_c_re, v_ssm_c_im, v_ssm_d, v_glu_w, v_attn_w_out, v_mix_w_out]))

    depth = mod_w.shape[0]
    n_tok, d = x.shape[1], x.shape[2]
    ff = ffn_w_out.shape[2] * N_DEV
    heads = forget_b.shape[1]
    sw = ssm_d.shape[1]
    g_n, p_n, n_n = ssm_b_re.shape[1:]
    aw = attn_w_out.shape[1]
    dh = aw // heads
    iw = mix_w_in.shape[2] * N_DEV
    nc = sw // LANES
    hw = g_n * p_n // nc
    mod_cols = mod_w.shape[2]
    scale = dh ** -0.5
    assert iw == sw + 3 * aw + heads + 2 * d and heads <= LANES
    assert math.log2(scale).is_integer(), "q is pre-scaled in bf16: exact only for a power of two"
    off_u, off_q, off_f = 2 * d, 2 * d + sw, 2 * d + sw + 3 * aw
    iwp = off_f + LANES
    assert off_u % sw == 0 and off_q % aw == 0 and off_f % LANES == 0

    me = 4 * lax.axis_index("x") + 2 * lax.axis_index("y") + lax.axis_index("c")
    x2 = x.reshape(n_tok, d)
    tgt = loss_target.reshape(n_tok, d)

    silu_c = rowwise(_f_silu, [c], [], [((d,), F32)], name="silu_c")[0]
    big = ["ffn_w_in", "ffn_w_out", "mix_w_in", "glu_w", "attn_w_out", "mix_w_out"]
    ffn1 = [("ffn_w_in", 0), ("ffn_w_out", 0)]
    rest = [("ffn_w_in", 1), ("ffn_w_out", 1), ("mix_w_in", None), ("glu_w", None), ("attn_w_out", None),
            ("mix_w_out", None)]

    def piece_of(dct, piece, l):
        name, j = piece
        return dct[name][l] if j is None else dct[name][l][j]

    def shards(l, pieces):
        return [piece_of(w_in, p, l).astype(BF16) for p in pieces]

    cut = [0, sw, sw + aw, sw + 2 * aw, sw + 3 * aw, sw + 3 * aw + heads, sw + 3 * aw + heads + d, iw]
    lw = [dict(win=[None, None], wout=[None, None]) for _ in range(depth)]

    def install(l, pieces, gathered):
        for (name, j), g in zip(pieces, gathered):
            if name == "ffn_w_in":
                lw[l]["win"][j] = jnp.moveaxis(g, 0, 1).reshape(d, 2 * ff)
            elif name == "ffn_w_out":
                lw[l]["wout"][j] = g.reshape(ff, d)
            elif name == "mix_w_in":
                w_mix_in = jnp.moveaxis(g, 0, 1).reshape(d, iw)
                seg = lambda i: w_mix_in[:, cut[i]:cut[i + 1]]
                lw[l]["wmi"] = jnp.concatenate([seg(5), seg(6), seg(0), seg(1), seg(2), seg(3),
                                                jnp.pad(seg(4), ((0, 0), (0, LANES - heads)))], axis=-1)
            elif name == "mix_w_out":
                lw[l]["mo"] = g.reshape(d, d)
            else:
                lw[l]["glu" if name == "glu_w" else "ao"] = jnp.moveaxis(g, 0, 1).reshape(g.shape[1], -1)

    gathered = all_gather(
        [silu_c, norm_pre.reshape(-1, norm_pre.shape[-1]), norm_post.reshape(-1, norm_post.shape[-1])]
        + shards(0, ffn1), name="gather_first")
    sc_all = gathered[0].reshape(N_DEV, d)
    gpre = jnp.moveaxis(gathered[1].reshape(N_DEV, depth, 3, -1), 0, 2).reshape(depth, 3, d)
    gpost = jnp.moveaxis(gathered[2].reshape(N_DEV, depth, 3, -1), 0, 2).reshape(depth, 3, d)
    install(0, ffn1, gathered[3:])

    sc_pad = jnp.pad(sc_all, ((0, LANES - N_DEV), (0, 0)))
    mod_part = jnp.stack([mm(sc_pad, mod_w[l], name=f"mod_fwd{l}")[:N_DEV] for l in range(depth)], axis=1)
    mod_part = mod_part + lax.dynamic_slice_in_dim(mod_b, me * mod_cols, mod_cols, axis=1)[None]
    mod_all = all_gather([mod_part], name="gather_mod")[0]
    mod_own = lax.dynamic_index_in_dim(mod_all, me, axis=1, keepdims=False)
    mod_own = mod_own.transpose(1, 0, 2).reshape(depth, 3, 3, d)
    res_w = (FFN_RES, 1.0, FFN_RES)

    def vec_a(l, i):
        return (gpre[l, i] * (1.0 + mod_own[l, i, 1])).reshape(1, d)

    def vec_sh(l, i):
        return mod_own[l, i, 0].reshape(1, d)

    def vec_b(l, i):
        return (res_w[i] * mod_own[l, i, 2] * gpost[l, i]).reshape(1, d)

    ssm = []
    for l in range(depth):
        (e, ph, bb_re, bb_im), disc_vjp = jax.vjp(_discretise,ssm_a_re[l], ssm_a_im[l], ssm_log_dt[l],
                                                  ssm_b_re[l], ssm_b_im[l])
        b_mat = jnp.concatenate([_blockdiag(bb_re.transpose(0, 2, 1), nc),
                                 _blockdiag(bb_im.transpose(0, 2, 1), nc)], axis=2)
        c_mat = jnp.concatenate([_blockdiag(ssm_c_re[l].transpose(0, 2, 1), nc),
                                 _blockdiag(-ssm_c_im[l].transpose(0, 2, 1), nc)], axis=1)
        ssm.append(dict(e=e, ph=ph, vjp=disc_vjp, b=b_mat.astype(BF16), c=c_mat.astype(BF16),
                        bt=b_mat.transpose(0, 2, 1).astype(BF16), ct=c_mat.transpose(0, 2, 1).astype(BF16),
                        tab_f=_scan_table(e, ph, nc, False), tab_r=_scan_table(e, ph, nc, True),
                        dvec=ssm_d[l].reshape(1, sw)))

    fb_pad = jnp.pad(forget_b, ((0, 0), (0, LANES - heads)))

    def heads_first(a):
        return a.reshape(n_tok, heads, dh).transpose(1, 0, 2)

    def heads_last(a):
        return a.transpose(1, 0, 2).reshape(n_tok, heads * dh)

    def mm_hosting(a, b, comm, **kw):
        if comm is None:
            return mm(a, b, **kw), []
        out, *arrived = mm(a, b, comm=comm, **kw)
        return out, arrived

    def ffn_fwd(xin, l, i, j, tag, comm=None):
        h = rowwise(_f_pre, [xin], [vec_a(l, i), vec_sh(l, i)], [((d,), BF16)], name=f"pre_{tag}")[0]
        a, arrived = mm_hosting(h, lw[l]["win"][j], comm, name=f"ffn_in_{tag}", out_dtype=BF16)
        m = rowwise(_f_swiglu, [(a, ff, 0), (a, ff, 1)], [], [((ff,), BF16)], name=f"swiglu_{tag}")[0]
        y = mm(m, lw[l]["wout"][j], name=f"ffn_out_{tag}")
        xout = rowwise(_f_post_add, [xin, y], [vec_b(l, i)], [((d,), F32)], name=f"post_{tag}")[0]
        return xout, dict(x=xin, h=h, a=a, m=m, y=y), arrived

    def mixer_fwd(xin, l, tag):
        s5 = ssm[l]
        h = rowwise(_f_pre, [xin], [vec_a(l, 1), vec_sh(l, 1)], [((d,), BF16)], name=f"pre_{tag}")[0]
        proj = mm(h, lw[l]["wmi"], name=f"mix_in_{tag}")
        bu = mm_blockdiag(proj, s5["b"], a_cb0=off_u // LANES, name=f"ssm_bu_{tag}")
        st = ssm_scan(bu, s5["tab_f"], reverse=False, name=f"ssm_scan_{tag}")[0]
        y0 = mm_blockdiag(st, s5["c"], name=f"ssm_y_{tag}")
        ge, ys = rowwise(_f_gelu_in, [y0, (proj, sw, off_u // sw)], [s5["dvec"]],
                         [((sw,), BF16), ((sw,), F32)], name=f"gelu_{tag}")
        z = mm(ge, lw[l]["glu"], name=f"glu_{tag}")
        cum = cum_fwd(proj, fb_pad[l:l + 1], col_block=off_f // LANES, name=f"cum_{tag}")
        crow = cum[:, :heads].T[:, None, :]
        q, k, v = [heads_first(proj[:, off_q + i * aw:off_q + (i + 1) * aw] * sc_).astype(BF16)
                   for i, sc_ in enumerate((scale, 1.0, 1.0))]
        nxt = gather_comm(shards(l + 1, ffn1 + rest)) if l + 1 < depth else None
        o, lse, *arrived = attn_fwd(q, k, v, crow, name=f"attn_{tag}", comm=nxt)
        if nxt is not None:
            install(l + 1, ffn1 + rest, arrived)
        attn = heads_last(o).astype(BF16)
        yb = mm(attn, lw[l]["ao"], name=f"attn_out_{tag}")
        mg = rowwise(_f_merge, [(z, d, 0), (z, d, 1), yb, (proj, d, 0), (proj, d, 1)], [],
                     [((d,), BF16)], name=f"merge_{tag}")[0]
        y = mm(mg, lw[l]["mo"], name=f"mix_out_{tag}")
        xout = rowwise(_f_post_add, [xin, y], [vec_b(l, 1)], [((d,), F32)], name=f"post_{tag}")[0]
        saved = dict(x=xin, h=h, proj=proj, st=st, ys=ys, ge=ge, z=z, q=q, k=k, v=v, o=o, lse=lse,
                     crow=crow, attn=attn, yb=yb, mg=mg, y=y)
        return xout, saved

    saved = []
    xc = x2
    for l in range(depth):
        xc, s0, arrived = ffn_fwd(xc, l, 0, 0, f"l{l}a", gather_comm(shards(0, rest)) if l == 0 else None)
        if l == 0:
            install(0, rest, arrived)
        xc, s1 = mixer_fwd(xc, l, f"l{l}m")
        xc, s2, _ = ffn_fwd(xc, l, 2, 1, f"l{l}b")
        saved.append((s0, s1, s2))

    def f_loss(xf, t):
        e_ = xf - t
        return e_ * (1.0 / d), _colsum(e_ * e_)

    dx, sq = rowwise(f_loss, [xc, tgt], [], [((d,), F32)], [d], name="loss_head")
    loss_part = 0.5 * jnp.sum(sq) / d

    grads = {k: [None] * depth for k in big}
    small_g = [dict() for _ in range(depth)]
    dmod = [[None] * 3 for _ in range(depth)]
    dgpre = [[None] * 3 for _ in range(depth)]
    dgpost = [[None] * 3 for _ in range(depth)]

    def norm_grads(l, i, d_a, d_sh, d_bv):
        d_a, d_sh, d_bv = d_a.reshape(d), d_sh.reshape(d), d_bv.reshape(d)
        dmod[l][i] = jnp.stack([d_sh, d_a * gpre[l, i], res_w[i] * gpost[l, i] * d_bv])
        dgpre[l][i] = d_a * (1.0 + mod_own[l, i, 1])
        dgpost[l][i] = res_w[i] * mod_own[l, i, 2] * d_bv

    def ffn_bwd(dxo, sv, l, i, j, tag, comm_dw=None, comm_dx_of=None):
        dy, d_bv = rowwise(_f_post_bwd, [dxo, sv["y"]], [vec_b(l, i)], [((d,), BF16)], [d],
                           name=f"post_bwd_{tag}")
        dm = mm(dy, lw[l]["wout"][j], trans_b=True, name=f"ffn_out_dx_{tag}", out_dtype=BF16)
        g_out = mm(sv["m"], dy, trans_a=True, name=f"ffn_out_dw_{tag}", out_dtype=BF16, tm=1408, tn=1024)
        da = rowwise(_f_swiglu_bwd, [(sv["a"], ff, 0), (sv["a"], ff, 1), dm], [], [((ff, ff), BF16)],
                     name=f"swiglu_bwd_{tag}")[0]
        g_in, arrived_dw = mm_hosting(sv["h"], da, comm_dw, trans_a=True, name=f"ffn_in_dw_{tag}",
                                      out_dtype=BF16)
        comm_dx = comm_dx_of(g_in, g_out) if comm_dx_of is not None else None
        dh_, arrived_dx = mm_hosting(da, lw[l]["win"][j], comm_dx, trans_b=True, name=f"ffn_in_dx_{tag}")
        dxn, d_a, d_sh = rowwise(_f_pre_bwd, [dxo, dh_, sv["x"]], [vec_a(l, i)], [((d,), F32)], [d, d],
                                 name=f"pre_bwd_{tag}")
        norm_grads(l, i, d_a, d_sh, d_bv)
        return dxn, g_in, g_out, arrived_dw, arrived_dx

    def mixer_bwd(dxo, sv, l, tag, comm):
        s5 = ssm[l]
        proj = sv["proj"]
        dy, d_bv = rowwise(_f_post_bwd, [dxo, sv["y"]], [vec_b(l, 1)], [((d,), BF16)], [d],
                           name=f"post_bwd_{tag}")
        dmg = mm(dy, lw[l]["mo"], trans_b=True, name=f"mix_out_dx_{tag}")
        g_mo = mm(sv["mg"], dy, trans_a=True, name=f"mix_out_dw_{tag}", out_dtype=BF16)
        dz, dyb, dgab = rowwise(
            _f_merge_bwd, [dmg, (sv["z"], d, 0), (sv["z"], d, 1), sv["yb"], (proj, d, 0), (proj, d, 1)], [],
            [((d, d), BF16), ((d,), BF16), ((d, d), BF16)], name=f"merge_bwd_{tag}")
        dge = mm(dz, lw[l]["glu"], trans_b=True, name=f"glu_dx_{tag}")
        g_glu = mm(sv["ge"], dz, trans_a=True, name=f"glu_dw_{tag}", out_dtype=BF16)
        dys, d_dvec = rowwise(_f_gelu_bwd, [dge, sv["ys"], (proj, sw, off_u // sw)], [], [((sw,), BF16)],
                              [sw], name=f"gelu_bwd_{tag}")
        gadj = mm_blockdiag(dys, s5["ct"], name=f"ssm_dy_{tag}")
        adj, dlam8 = ssm_scan(gadj, s5["tab_r"], reverse=True, s_prev=sv["st"], name=f"ssm_scan_bwd_{tag}")
        du0 = mm_blockdiag(adj, s5["bt"], name=f"ssm_du_{tag}")
        d_bmat = mm_blockdiag_tn(proj, adj, g_n=nc, ka=LANES, kb=2 * hw, a_cb0=off_u // LANES,
                                 name=f"ssm_db_{tag}")
        d_cmat = mm_blockdiag_tn(sv["st"], dys, g_n=nc, ka=2 * hw, kb=LANES, name=f"ssm_dc_{tag}")
        du = rowwise(_f_du_fin, [du0, dys], [s5["dvec"]], [((sw,), BF16)], name=f"ssm_du_fin_{tag}")[0]
        dlam = jnp.sum(dlam8, axis=1)
        dlam_re, dlam_im = dlam[:, :hw].reshape(g_n, p_n), dlam[:, hw:].reshape(g_n, p_n)
        dbb_re = _blockdiag_take(d_bmat[:, :, :hw], g_n).transpose(0, 2, 1)
        dbb_im = _blockdiag_take(d_bmat[:, :, hw:], g_n).transpose(0, 2, 1)
        mag = jnp.exp(s5["e"])
        lr, li = mag * jnp.cos(s5["ph"]), mag * jnp.sin(s5["ph"])
        d_e = dlam_re * lr + dlam_im * li
        d_ph = -dlam_re * li + dlam_im * lr
        da_re, da_im, dlog_dt, db_re, db_im = s5["vjp"]((d_e, d_ph, dbb_re, dbb_im))
        small_g[l].update(
            ssm_a_re=da_re, ssm_a_im=da_im, ssm_log_dt=dlog_dt, ssm_b_re=db_re, ssm_b_im=db_im,
            ssm_c_re=_blockdiag_take(d_cmat[:, :hw, :], g_n).transpose(0, 2, 1),
            ssm_c_im=-_blockdiag_take(d_cmat[:, hw:, :], g_n).transpose(0, 2, 1),
            ssm_d=d_dvec.reshape(sw))
        dattn = mm(dyb, lw[l]["ao"], trans_b=True, name=f"attn_out_dx_{tag}")
        g_ao = mm(sv["attn"], dyb, trans_a=True, name=f"attn_out_dw_{tag}", out_dtype=BF16)
        dq, dk, dv, dcr, *arrived = attn_bwd(sv["q"], sv["k"], sv["v"], heads_first(dattn), sv["o"], sv["lse"],
                                             sv["crow"], scale=scale, tag=tag, comm=comm)
        dcum = jnp.pad(dcr[:, 0, :].T, ((0, 0), (0, LANES - heads)))
        df, dfb = cum_bwd(dcum, proj, fb_pad[l:l + 1], col_block=off_f // LANES, name=f"cum_bwd_{tag}")
        small_g[l]["forget_b"] = dfb[0, :heads]
        dproj = jnp.concatenate([dgab, du, heads_last(dq).astype(BF16), heads_last(dk).astype(BF16),
                                 heads_last(dv).astype(BF16), df.astype(BF16)], axis=1)
        g_mi = mm(sv["h"], dproj, trans_a=True, name=f"mix_in_dw_{tag}", out_dtype=BF16)
        dh_ = mm(dproj, lw[l]["wmi"], trans_b=True, name=f"mix_in_dx_{tag}")
        dxn, d_a, d_sh = rowwise(_f_pre_bwd, [dxo, dh_, sv["x"]], [vec_a(l, 1)], [((d,), F32)], [d, d],
                                 name=f"pre_bwd_{tag}")
        norm_grads(l, 1, d_a, d_sh, d_bv)
        g_mi = jnp.concatenate([g_mi[:, off_u:off_f + heads], g_mi[:, :off_u]], axis=1)
        return dxn, g_mi, g_glu, g_ao, g_mo, arrived

    def split_last(a):
        return jnp.moveaxis(a.reshape(a.shape[:-1] + (N_DEV, a.shape[-1] // N_DEV)), -2, 0)

    def split_rows(a):
        return jnp.moveaxis(a.reshape(a.shape[:-2] + (N_DEV, a.shape[-2] // N_DEV, a.shape[-1])), -3, 0)

    def owner_blocks(l, pieces):
        out = []
        for name, j in pieces:
            if name == "ffn_w_in":
                out.append(split_last(g_ffn_in[l][j]))
            elif name == "ffn_w_out":
                out.append(split_rows(g_ffn_out[l][j]))
            elif name == "mix_w_out":
                out.append(split_rows(grads[name][l]))
            else:
                out.append(split_last(grads[name][l]))
        return out

    g_ffn_in = [[None, None] for _ in range(depth)]
    g_ffn_out = [[None, None] for _ in range(depth)]
    parts = {}

    def record(l, pieces, arrived):
        for p, a in zip(pieces, arrived):
            parts[(p, l)] = a

    def last_ffn_blocks(g_in, g_out):
        return exchange_comm([split_last(g_in), split_rows(g_out)])

    for l in reversed(range(depth)):
        s0, s1, s2 = saved[l]
        dx, g_ffn_in[l][1], g_ffn_out[l][1], _, _ = ffn_bwd(dx, s2, l, 2, 1, f"l{l}b")
        pending = exchange_comm(owner_blocks(l + 1, ffn1 + rest)) if l + 1 < depth else None
        (dx, grads["mix_w_in"][l], grads["glu_w"][l], grads["attn_w_out"][l], grads["mix_w_out"][l],
         arrived) = mixer_bwd(dx, s1, l, f"l{l}m", pending)
        if pending is not None:
            record(l + 1, ffn1 + rest, arrived)
        if l == 0:
            dx, g_ffn_in[l][0], g_ffn_out[l][0], arrived_rest, arrived_ffn1 = ffn_bwd(
                dx, s0, l, 0, 0, f"l{l}a", exchange_comm(owner_blocks(0, rest)), last_ffn_blocks)
            record(0, rest, arrived_rest)
            record(0, ffn1, arrived_ffn1)
        else:
            dx, g_ffn_in[l][0], g_ffn_out[l][0], _, _ = ffn_bwd(dx, s0, l, 0, 0, f"l{l}a")
    grad_x = dx.reshape(x.shape)

    small_names = ["forget_b", "ssm_a_re", "ssm_a_im", "ssm_log_dt", "ssm_b_re", "ssm_b_im", "ssm_c_re",
                   "ssm_c_im", "ssm_d"]
    pieces = [loss_part.reshape(1), jnp.stack([jnp.stack(dmod[l]) for l in range(depth)]).reshape(-1),
              jnp.stack([jnp.stack(dgpre[l]) for l in range(depth)]).reshape(-1),
              jnp.stack([jnp.stack(dgpost[l]) for l in range(depth)]).reshape(-1)]
    pieces += [jnp.stack([small_g[l][k] for l in range(depth)]).reshape(-1) for k in small_names]
    sizes = [p.size for p in pieces]
    chunk = SUBLANES * 1024
    total = -(-sum(sizes) // chunk) * chunk
    pack = jnp.pad(jnp.concatenate(pieces), (0, total - sum(sizes))).reshape(total // 1024, 1024)
    pack_all = all_gather([pack], name="gather_small_grads")[0]
    pack_sum = rowwise(_f_sum_parts, [(pack_all, p) for p in range(N_DEV)], [], [((1024,), F32)],
                       name="sum_small_grads")[0].reshape(-1)
    offs = [0]
    for s_ in sizes:
        offs.append(offs[-1] + s_)
    take = lambda i: pack_sum[offs[i]:offs[i + 1]]
    loss = take(0).reshape(())
    g_small = {"mod_b": take(1).reshape(mod_b.shape)}
    g_pre_full, g_post_full = take(2).reshape(depth, 3, d), take(3).reshape(depth, 3, d)
    shard = norm_pre.shape[-1]
    g_small["norm_pre"] = lax.dynamic_slice_in_dim(g_pre_full, me * shard, shard, axis=2)
    g_small["norm_post"] = lax.dynamic_slice_in_dim(g_post_full, me * shard, shard, axis=2)
    for i, k in enumerate(small_names):
        g_small[k] = take(4 + i).reshape(w_in[k].shape)

    dmod_all = pack_all.reshape(N_DEV, -1)[:, offs[1]:offs[2]].reshape(N_DEV, depth, 9 * d)
    dmod_mine = lax.dynamic_slice_in_dim(dmod_all, me * mod_cols, mod_cols, axis=2)
    sct_pad = jnp.pad(sc_all.T, ((0, 0), (0, LANES - N_DEV)))
    g_mod_w = jnp.stack([
        mm(sct_pad, jnp.pad(dmod_mine[:, l], ((0, LANES - N_DEV), (0, 0))), name=f"mod_dw{l}")
        for l in range(depth)])

    out_g, out_d, out_m, out_v = {}, {}, {}, {}
    flat = lambda a: a.reshape(-1, a.shape[-1])
    res = adamw(g_mod_w.reshape(1, -1, mod_cols), flat(mod_w), flat(m_mod_w), flat(v_mod_w), name="adamw_mod_w")
    out_g["mod_w"], out_d["mod_w"], out_m["mod_w"], out_v["mod_w"] = [r.reshape(mod_w.shape) for r in res]
    updated = {}
    for l in range(depth):
        for p in ffn1 + rest:
            w_p = piece_of(w_in, p, l)
            res = adamw(parts[(p, l)], flat(w_p), flat(piece_of(m_in, p, l)), flat(piece_of(v_in, p, l)),
                        name=f"adamw_{p[0]}_l{l}" + ("" if p[1] is None else f"_{p[1]}"))
            updated[(p, l)] = [r.reshape(w_p.shape) for r in res]
    for k in big:
        for q_, dct in enumerate((out_g, out_d, out_m, out_v)):
            if k.startswith("ffn"):
                dct[k] = jnp.stack([jnp.stack([updated[((k, j), l)][q_] for j in range(2)])
                                    for l in range(depth)])
            else:
                dct[k] = jnp.stack([updated[((k, None), l)][q_] for l in range(depth)])
    small_all = ["mod_b", "norm_pre", "norm_post"] + small_names

    def pack_small(dct):
        flat = jnp.concatenate([dct[k].reshape(-1) for k in small_all])
        tot = -(-flat.size // chunk) * chunk
        return jnp.pad(flat, (0, tot - flat.size)).reshape(tot // 1024, 1024)

    res = adamw(pack_small(g_small)[None], pack_small(w_in), pack_small(m_in), pack_small(v_in),
                name="adamw_small")
    pos = 0
    for k in small_all:
        size = w_in[k].size
        for dct, r in zip((out_g, out_d, out_m, out_v), res):
            dct[k] = r.reshape(-1)[pos:pos + size].reshape(w_in[k].shape)
        pos += size

    return (loss, grad_x, *[out_g[k] for k in names], *[out_d[k] for k in names],
            *[out_m[k] for k in names], *[out_v[k] for k in names])
```

```python
import functools
import math

import jax
import jax.numpy as jnp
from jax import lax
from jax.experimental import pallas as pl
from jax.experimental.pallas import tpu as pltpu

F32 = jnp.float32
BF16 = jnp.bfloat16
MESH = pl.DeviceIdType.MESH
N_DEV = 8
LANES = 128
SUBLANES = 8
VMEM_LIMIT = 48 * 1024 * 1024

RMS_EPS = 1e-6
FFN_RES = 0.5
ADAM_LR = 0.001
ADAM_B1 = 0.9
ADAM_B2 = 0.999
ADAM_EPS = 1e-08
ADAM_WD = 0.01
ADAM_STEP = 10
GELU_C = math.sqrt(2.0 / math.pi)
GELU_A = 0.044715


def _pick(dim, target, mult=LANES):
    t = (min(dim, target) // mult) * mult
    while t >= mult:
        if dim % t == 0:
            return t
        t -= mult
    return dim


def _params(sem):
    return pltpu.CompilerParams(dimension_semantics=sem, vmem_limit_bytes=VMEM_LIMIT)


def _sigmoid(x):
    return 1.0 / (1.0 + jnp.exp(-x))


def mm(a, b, *, name, trans_a=False, trans_b=False, out_dtype=F32, tm=1024, tn=1408, tk=2816, comm=None):
    if trans_a:
        kdim, m = a.shape
    else:
        m, kdim = a.shape
    if trans_b:
        n, kb = b.shape
    else:
        kb, n = b.shape
    assert kdim == kb, (a.shape, b.shape)
    tm, tn, tk = _pick(m, tm), _pick(n, tn), _pick(kdim, tk)
    gm, gn, nk = m // tm, n // tn, kdim // tk
    dims = (((0 if trans_a else 1,), (1 if trans_b else 0,)), ((), ()))
    host = _Hosted(comm, 2, 1, 1 if nk > 1 else 0)

    def body(*refs):
        (a_ref, b_ref, o_ref, *acc), crefs = host.split(refs)
        i, j, k = pl.program_id(0), pl.program_id(1), pl.program_id(2)
        host.phase("start", (i == 0) & (j == 0) & (k == 0), crefs)
        prod = lax.dot_general(a_ref[...].astype(BF16), b_ref[...].astype(BF16), dims,
                               preferred_element_type=F32)
        if nk == 1:
            o_ref[...] = prod.astype(out_dtype)
        else:
            acc_ref, = acc

            @pl.when(k == 0)
            def _():
                acc_ref[...] = prod

            @pl.when((k > 0) & (k < nk - 1))
            def _():
                acc_ref[...] += prod

            @pl.when(k == nk - 1)
            def _():
                o_ref[...] = (acc_ref[...] + prod).astype(out_dtype)

        last = (i == gm - 1) & (j == gn - 1) & (k == nk - 1)
        host.phase("mid", last, crefs)
        host.phase("finish", last, crefs)

    a_spec = (pl.BlockSpec((tk, tm), lambda i, j, k: (k, i)) if trans_a
              else pl.BlockSpec((tm, tk), lambda i, j, k: (i, k)))
    b_spec = (pl.BlockSpec((tn, tk), lambda i, j, k: (j, k)) if trans_b
              else pl.BlockSpec((tk, tn), lambda i, j, k: (k, j)))
    in_specs, out_specs, out_shape, scratch, extra = host.specs(
        [a_spec, b_spec], [pl.BlockSpec((tm, tn), lambda i, j, k: (i, j))],
        [jax.ShapeDtypeStruct((m, n), out_dtype)], [pltpu.VMEM((tm, tn), F32)] if nk > 1 else [])
    res = pl.pallas_call(
        body, name=name, grid=(gm, gn, nk),
        in_specs=in_specs, out_specs=out_specs, out_shape=out_shape, scratch_shapes=scratch,
        compiler_params=_params(("arbitrary", "arbitrary", "arbitrary")),
    )(a, b, *extra)
    return res if comm else res[0]


def mm_blockdiag(a, b, *, name, a_cb0=0, out_dtype=F32, tm=512):
    m = a.shape[0]
    g_n, ka, nb = b.shape
    tm = _pick(m, tm)
    assert a_cb0 % g_n == 0

    def body(a_ref, b_ref, o_ref):
        for g in range(g_n):
            o_ref[:, g * nb:(g + 1) * nb] = jnp.dot(
                a_ref[:, g * ka:(g + 1) * ka].astype(BF16), b_ref[g].astype(BF16),
                preferred_element_type=F32).astype(out_dtype)

    return pl.pallas_call(
        body, name=name, grid=(m // tm,),
        in_specs=[pl.BlockSpec((tm, g_n * ka), lambda i: (i, a_cb0 // g_n)),
                  pl.BlockSpec((g_n, ka, nb), lambda i: (0, 0, 0))],
        out_specs=pl.BlockSpec((tm, g_n * nb), lambda i: (i, 0)),
        out_shape=jax.ShapeDtypeStruct((m, g_n * nb), out_dtype),
        compiler_params=_params(("parallel",)),
    )(a, b)


def mm_blockdiag_tn(a, b, *, name, g_n, ka, kb, a_cb0=0, b_cb0=0, tk=512):
    rows = a.shape[0]
    tk = _pick(rows, tk)
    nk = rows // tk
    assert a_cb0 % g_n == 0 and b_cb0 % g_n == 0

    def body(a_ref, b_ref, o_ref):
        @pl.when(pl.program_id(0) == 0)
        def _():
            o_ref[...] = jnp.zeros_like(o_ref)

        for g in range(g_n):
            o_ref[g] += lax.dot_general(a_ref[:, g * ka:(g + 1) * ka].astype(BF16),
                                        b_ref[:, g * kb:(g + 1) * kb].astype(BF16),
                                        (((0,), (0,)), ((), ())), preferred_element_type=F32)

    return pl.pallas_call(
        body, name=name, grid=(nk,),
        in_specs=[pl.BlockSpec((tk, g_n * ka), lambda k: (k, a_cb0 // g_n)),
                  pl.BlockSpec((tk, g_n * kb), lambda k: (k, b_cb0 // g_n))],
        out_specs=pl.BlockSpec((g_n, ka, kb), lambda k: (0, 0, 0)),
        out_shape=jax.ShapeDtypeStruct((g_n, ka, kb), F32),
        compiler_params=_params(("arbitrary",)),
    )(a, b)


def rowwise(fn, rows, vecs, outs, reds=(), *, name, tm=256):
    metas = []
    for r in rows:
        if isinstance(r, tuple) and len(r) == 3:
            metas.append(("col", r[0], r[1], r[2]))
        elif isinstance(r, tuple):
            metas.append(("lead", r[0], r[0].shape[2], r[1]))
        else:
            metas.append(("full", r, r.shape[1], 0))
    n_rows = metas[0][1].shape[1] if metas[0][0] == "lead" else metas[0][1].shape[0]
    rc = 16 if n_rows % 16 == 0 else (SUBLANES if n_rows % SUBLANES == 0 else n_rows)
    tm = _pick(n_rows, tm, rc)
    n_inner = tm // rc
    nr, nv, no = len(metas), len(vecs), len(outs)

    def body(*refs):
        row_refs, vec_refs = refs[:nr], refs[nr:nr + nv]
        out_refs, red_refs = refs[nr + nv:nr + nv + no], refs[nr + nv + no:]
        if reds:
            @pl.when(pl.program_id(0) == 0)
            def _():
                for rr in red_refs:
                    rr[...] = jnp.zeros_like(rr)
        vec_vals = [v[...] for v in vec_refs]

        def step(s, carry):
            r0 = pl.multiple_of(s * rc, rc)
            vals = [ref[pl.ds(r0, rc), :] for ref in row_refs]
            res = fn(*vals, *vec_vals)
            if not isinstance(res, (tuple, list)):
                res = (res,)
            for o_ref, (widths, dt), val in zip(out_refs, outs, res[:no]):
                pieces = val if isinstance(val, (tuple, list)) else (val,)
                off = 0
                for w_, piece in zip(widths, pieces):
                    o_ref[pl.ds(r0, rc), off:off + w_] = piece.astype(dt)
                    off += w_
            for rr, val in zip(red_refs, res[no:]):
                rr[...] += val
            return carry

        lax.fori_loop(0, n_inner, step, 0, unroll=min(n_inner, 4))

    in_specs = []
    for kind, arr, w_, idx in metas:
        if kind == "col":
            in_specs.append(pl.BlockSpec((tm, w_), functools.partial(lambda i, cb: (i, cb), cb=idx)))
        elif kind == "lead":
            in_specs.append(pl.BlockSpec((None, tm, w_), functools.partial(lambda i, p: (p, i, 0), p=idx)))
        else:
            in_specs.append(pl.BlockSpec((tm, w_), lambda i: (i, 0)))
    for v in vecs:
        in_specs.append(pl.BlockSpec(v.shape, lambda i: (0, 0)))
    out_specs = [pl.BlockSpec((tm, sum(ws)), lambda i: (i, 0)) for ws, _ in outs]
    out_specs += [pl.BlockSpec((1, w_), lambda i: (0, 0)) for w_ in reds]
    out_shape = [jax.ShapeDtypeStruct((n_rows, sum(ws)), dt) for ws, dt in outs]
    out_shape += [jax.ShapeDtypeStruct((1, w_), F32) for w_ in reds]
    res = pl.pallas_call(
        body, name=name, grid=(n_rows // tm,),
        in_specs=in_specs, out_specs=out_specs, out_shape=out_shape,
        compiler_params=_params(("arbitrary",)),
    )(*[m[1] for m in metas], *vecs)
    return res


def _rms(x):
    return lax.rsqrt(jnp.mean(x * x, axis=-1, keepdims=True) + RMS_EPS)


def _colsum(x):
    return jnp.sum(x, axis=0, keepdims=True)


def _f_silu(c):
    return c * _sigmoid(c)


def _f_pre(x, a, sh):
    return (x * _rms(x)) * a + sh


def _f_post_add(x, y, bv):
    return x + (y * _rms(y)) * bv


def _f_post_bwd(dxo, y, bv):
    ry = _rms(y)
    yn = y * ry
    dyn = dxo * bv
    dy = ry * (dyn - yn * jnp.mean(dyn * yn, axis=-1, keepdims=True))
    return dy, _colsum(dxo * yn)


def _f_pre_bwd(dxo, dh, x, a):
    r = _rms(x)
    xn = x * r
    dxn = dh * a
    dx = dxo + r * (dxn - xn * jnp.mean(dxn * xn, axis=-1, keepdims=True))
    return dx, _colsum(dh * xn), _colsum(dh)


def _f_swiglu(g, u):
    g = g.astype(F32)
    return (g * _sigmoid(g)) * u.astype(F32)


def _f_swiglu_bwd(g, u, dm):
    g, u, dm = g.astype(F32), u.astype(F32), dm.astype(F32)
    sg = _sigmoid(g)
    dg = dm * u * (sg * (1.0 + g * (1.0 - sg)))
    du = dm * (g * sg)
    return ((dg, du),)


def _gelu_t(x):
    return jnp.tanh(GELU_C * (x + GELU_A * x * x * x))


def _f_gelu_in(y0, u, dvec):
    y = y0 + dvec * u
    return 0.5 * y * (1.0 + _gelu_t(y)), y


def _f_gelu_bwd(dge, y, u):
    t = _gelu_t(y)
    dy = dge * (0.5 * (1.0 + t) + 0.5 * y * (1.0 - t * t) * GELU_C * (1.0 + 3.0 * GELU_A * y * y))
    return dy, _colsum(dy * u)


def _f_du_fin(du0, dys, dvec):
    return du0 + dvec * dys.astype(F32)


def _f_merge(zv, zg, yb, ga, gb):
    zv, zg, yb = zv.astype(F32), zg.astype(F32), yb.astype(F32)
    return _sigmoid(ga) * (zv * _sigmoid(zg)) + _sigmoid(gb) * yb


def _f_merge_bwd(dmg, zv, zg, yb, ga, gb):
    zv, zg, yb = zv.astype(F32), zg.astype(F32), yb.astype(F32)
    sa, sb, sz = _sigmoid(ga), _sigmoid(gb), _sigmoid(zg)
    ya = zv * sz
    dya = dmg * sa
    dga = dmg * ya * sa * (1.0 - sa)
    dyb = dmg * sb
    dgb = dmg * yb * sb * (1.0 - sb)
    dzv = dya * sz
    dzg = dya * zv * sz * (1.0 - sz)
    return (dzv, dzg), dyb, (dga, dgb)


def _f_sum_parts(*parts):
    acc = parts[0].astype(F32)
    for p in parts[1:]:
        acc = acc + p.astype(F32)
    return acc


def _f_adamw(*args):
    parts, (w, m, v) = args[:-3], args[-3:]
    g = _f_sum_parts(*parts)
    m = ADAM_B1 * m + (1.0 - ADAM_B1) * g
    v = ADAM_B2 * v + (1.0 - ADAM_B2) * (g * g)
    m_hat = m / (1.0 - ADAM_B1 ** ADAM_STEP)
    v_hat = v / (1.0 - ADAM_B2 ** ADAM_STEP)
    delta = -ADAM_LR * (m_hat / (jnp.sqrt(v_hat) + ADAM_EPS) + ADAM_WD * w)
    return g, delta, m, v


def adamw(parts3, w, m, v, *, name):
    c = w.shape[1]
    rows = [(parts3, p) for p in range(parts3.shape[0])] + [w, m, v]
    return rowwise(_f_adamw, rows, [], [((c,), F32)] * 4, name=name)


def _tri_dot(tri, x):
    x1 = x.astype(BF16)
    r1 = x - x1.astype(F32)
    x2 = r1.astype(BF16)
    x3 = (r1 - x2.astype(F32)).astype(BF16)
    dot = functools.partial(jnp.dot, preferred_element_type=F32)
    return dot(tri, x1) + dot(tri, x2) + dot(tri, x3)


def cum_fwd(proj, fb, *, col_block, name, t=256):
    n = proj.shape[0]
    t = _pick(n, t, SUBLANES)

    def body(f_ref, fb_ref, cum_ref, car_ref):
        @pl.when(pl.program_id(0) == 0)
        def _():
            car_ref[...] = jnp.zeros_like(car_ref)

        x = f_ref[...] + fb_ref[...]
        lf = jnp.minimum(x, 0.0) - jnp.log(1.0 + jnp.exp(-jnp.abs(x)))
        r = lax.broadcasted_iota(jnp.int32, (t, t), 0)
        c = lax.broadcasted_iota(jnp.int32, (t, t), 1)
        cs = _tri_dot((c <= r).astype(BF16), lf) + car_ref[0:1, :]
        cum_ref[...] = cs
        car_ref[0:1, :] = cs[t - 1:t, :]

    return pl.pallas_call(
        body, name=name, grid=(n // t,),
        in_specs=[pl.BlockSpec((t, LANES), lambda i: (i, col_block)),
                  pl.BlockSpec((1, LANES), lambda i: (0, 0))],
        out_specs=pl.BlockSpec((t, LANES), lambda i: (i, 0)),
        out_shape=jax.ShapeDtypeStruct((n, LANES), F32),
        scratch_shapes=[pltpu.VMEM((SUBLANES, LANES), F32)],
        compiler_params=_params(("arbitrary",)),
    )(proj, fb)


def cum_bwd(dcum, proj, fb, *, col_block, name, t=256):
    n = proj.shape[0]
    t = _pick(n, t, SUBLANES)
    nb = n // t

    def body(dc_ref, f_ref, fb_ref, df_ref, dfb_ref, car_ref):
        @pl.when(pl.program_id(0) == 0)
        def _():
            car_ref[...] = jnp.zeros_like(car_ref)
            dfb_ref[...] = jnp.zeros_like(dfb_ref)

        r = lax.broadcasted_iota(jnp.int32, (t, t), 0)
        c = lax.broadcasted_iota(jnp.int32, (t, t), 1)
        dl = _tri_dot((c >= r).astype(BF16), dc_ref[...]) + car_ref[0:1, :]
        car_ref[0:1, :] = dl[0:1, :]
        x = f_ref[...] + fb_ref[...]
        df = dl * (1.0 / (1.0 + jnp.exp(x)))
        df_ref[...] = df
        dfb_ref[...] += _colsum(df)

    return pl.pallas_call(
        body, name=name, grid=(nb,),
        in_specs=[pl.BlockSpec((t, LANES), lambda i: (nb - 1 - i, 0)),
                  pl.BlockSpec((t, LANES), lambda i: (nb - 1 - i, col_block)),
                  pl.BlockSpec((1, LANES), lambda i: (0, 0))],
        out_specs=[pl.BlockSpec((t, LANES), lambda i: (nb - 1 - i, 0)),
                   pl.BlockSpec((1, LANES), lambda i: (0, 0))],
        out_shape=[jax.ShapeDtypeStruct((n, LANES), F32), jax.ShapeDtypeStruct((1, LANES), F32)],
        scratch_shapes=[pltpu.VMEM((SUBLANES, LANES), F32)],
        compiler_params=_params(("arbitrary",)),
    )(dcum, proj, fb)


_NT =(((1,), (1,)), ((), ()))
_TN = (((0,), (0,)), ((), ()))


def _causal_keep(t):
    return lax.broadcasted_iota(jnp.int32, (t, t), 1) <= lax.broadcasted_iota(jnp.int32, (t, t), 0)


def attn_fwd(q, k, v, crow, *, name, t=512, hb=8, comm=None):
    h_n, n, dh = q.shape
    hb = min(hb, h_n)
    t = _pick(n, t)
    nb = n // t
    ng = h_n // hb
    host = _Hosted(comm, 4, 2, 3)

    def body(*refs):
        (q_ref, k_ref, v_ref, cr_ref, o_ref, lse_ref, m_sc, l_sc, acc_sc), crefs = host.split(refs)
        g, i, j = pl.program_id(0), pl.program_id(1), pl.program_id(2)
        host.phase("start", (g == 0) & (i == 0) & (j == 0), crefs)
        host.phase("mid", (g == ng - 1) & (i == (3 * nb) // 4) & (j == 0), crefs)

        @pl.when(j == 0)
        def _():
            m_sc[...] = jnp.full_like(m_sc, -jnp.inf)
            l_sc[...] = jnp.zeros_like(l_sc)
            acc_sc[...] = jnp.zeros_like(acc_sc)

        def update(diagonal):
            keep = _causal_keep(t) if diagonal else None
            heads = range(hb)
            ss = [lax.dot_general(q_ref[h], k_ref[h], _NT, preferred_element_type=F32) for h in heads]
            pairs, alphas = [], []
            for h in heads:
                s = ss[h] - cr_ref[h]
                if diagonal:
                    s = jnp.where(keep, s, -jnp.inf)
                m_prev = m_sc[h]
                m_new = jnp.maximum(m_prev, jnp.max(s, axis=-1, keepdims=True))
                p = jnp.exp(s - m_new)
                alpha = jnp.exp(m_prev - m_new)
                l_sc[h] = alpha * l_sc[h] + jnp.sum(p, axis=-1, keepdims=True)
                m_sc[h] = m_new
                p_hi = p.astype(BF16)
                pairs.append((p_hi, (p - p_hi.astype(F32)).astype(BF16)))
                alphas.append(alpha)
            for h in heads:
                vv = v_ref[h]
                acc_sc[h] = (alphas[h] * acc_sc[h] + jnp.dot(pairs[h][0], vv, preferred_element_type=F32)
                             + jnp.dot(pairs[h][1], vv, preferred_element_type=F32))

        @pl.when(j < i)
        def _():
            update(False)

        @pl.when(j == i)
        def _():
            update(True)

        @pl.when(j == nb - 1)
        def _():
            o_ref[...] = acc_sc[...] / l_sc[...]
            lse_ref[...] = m_sc[...] + jnp.log(l_sc[...])

        host.phase("finish", (g == ng - 1) & (i == nb - 1) & (j == nb - 1), crefs)

    qspec = pl.BlockSpec((hb, t, dh), lambda g, i, j: (g, i, 0))
    kspec = pl.BlockSpec((hb, t, dh), lambda g, i, j: (g, jnp.minimum(j, i), 0))
    colspec = pl.BlockSpec((hb, t, 1), lambda g, i, j: (g, i, 0))
    in_specs, out_specs, out_shape, scratch, extra = host.specs(
        [qspec, kspec, kspec, pl.BlockSpec((hb, 1, t), lambda g, i, j: (g, 0, jnp.minimum(j, i)))],
        [qspec, colspec],
        [jax.ShapeDtypeStruct((h_n, n, dh), F32), jax.ShapeDtypeStruct((h_n, n, 1), F32)],
        [pltpu.VMEM((hb, t, 1), F32), pltpu.VMEM((hb, t, 1), F32), pltpu.VMEM((hb, t, dh), F32)])
    return pl.pallas_call(
        body, name=name, grid=(ng, nb, nb),
        in_specs=in_specs, out_specs=out_specs, out_shape=out_shape, scratch_shapes=scratch,
        compiler_params=_params(("arbitrary", "arbitrary", "arbitrary")),
    )(q, k, v, crow, *extra)


def _f_rowdot(a, b):
    return jnp.sum(a.astype(F32) * b, axis=-1, keepdims=True)


def attn_bwd(q, k, v, do, o, lse, crow, *, scale, tag, t=512, hb=2, comm=None):
    h_n, n, dh = q.shape
    hb = min(hb, h_n)
    t = _pick(n, t)
    nb = n // t
    dob = do.astype(BF16)
    delta = rowwise(_f_rowdot, [dob.reshape(h_n * n, dh), o.reshape(h_n * n, dh)], [], [((1,), F32)],
                    name=f"attn_delta_{tag}")[0].reshape(h_n, 1, n)
    lse_row, ccol = lse.reshape(h_n, 1, n), crow.reshape(h_n, n, 1)

    ng = h_n // hb
    host = _Hosted(comm, 7, 4, 3)

    def body(*refs):
        (q_ref, k_ref, v_ref, do_ref, lse_ref, dl_ref, cc_ref,
         dq_ref, dk_ref, dv_ref, dcc_ref, dk_acc, dv_acc, dcc_acc), crefs = host.split(refs)
        g, j, i = pl.program_id(0), pl.program_id(1), pl.program_id(2)
        host.phase("start", (g == 0) & (j == 0) & (i == 0), crefs)

        @pl.when((j == 0) & (i == 0))
        def _():
            dq_ref[...] = jnp.zeros_like(dq_ref)

        @pl.when(i == 0)
        def _():
            dk_acc[...] = jnp.zeros_like(dk_acc)
            dv_acc[...] = jnp.zeros_like(dv_acc)
            dcc_acc[...] = jnp.zeros_like(dcc_acc)

        def update(diagonal):
            heads = range(hb)
            r0 = pl.multiple_of(i * t, t)
            if diagonal:
                keep = lax.broadcasted_iota(jnp.int32, (t, t), 0) <= lax.broadcasted_iota(jnp.int32, (t, t), 1)
            qv, kv = [q_ref[h] for h in heads], [k_ref[h] for h in heads]
            vv, dov = [v_ref[h] for h in heads], [do_ref[h] for h in heads]
            st = [lax.dot_general(kv[h], qv[h], _NT, preferred_element_type=F32) for h in heads]
            dpt = [lax.dot_general(vv[h], dov[h], _NT, preferred_element_type=F32) for h in heads]
            pt = []
            for h in heads:
                s = st[h] - cc_ref[h]
                if diagonal:
                    s = jnp.where(keep, s, -jnp.inf)
                pt.append(jnp.exp(s - lse_ref[h]))
            for h in heads:
                dv_acc[h] += jnp.dot(pt[h].astype(BF16), dov[h], preferred_element_type=F32)
            dsb = []
            for h in heads:
                ds = pt[h] * (dpt[h] - dl_ref[h])
                dcc_acc[h] -= jnp.sum(ds, axis=1, keepdims=True)
                dsb.append(ds.astype(BF16))
            for h in heads:
                dk_acc[h] += jnp.dot(dsb[h], qv[h], preferred_element_type=F32)
            for h in heads:
                dq_ref[h, pl.ds(r0, t), :] += lax.dot_general(dsb[h], kv[h], _TN,
                                                              preferred_element_type=F32) * scale

        @pl.when(i > j)
        def _():
            update(False)

        @pl.when(i == j)
        def _():
            update(True)

        @pl.when(i == nb - 1)
        def _():
            dk_ref[...] = dk_acc[...]
            dv_ref[...] = dv_acc[...]
            dcc_ref[...] = dcc_acc[...]

        host.phase("finish", (g == ng - 1) & (j == nb - 1) & (i == nb - 1), crefs)

    qspec = pl.BlockSpec((hb, t, dh), lambda g, j, i: (g, jnp.maximum(i, j), 0))
    qrow = pl.BlockSpec((hb, 1, t), lambda g, j, i: (g, 0, jnp.maximum(i, j)))
    kspec = pl.BlockSpec((hb, t, dh), lambda g, j, i: (g, j, 0))
    kcol = pl.BlockSpec((hb, t, 1), lambda g, j, i: (g, j, 0))
    in_specs, out_specs, out_shape, scratch, extra = host.specs(
        [qspec, kspec, kspec, qspec, qrow, qrow, kcol],
        [pl.BlockSpec((hb, n, dh), lambda g, j, i: (g, 0, 0)), kspec, kspec, kcol],
        [jax.ShapeDtypeStruct((h_n, n, dh), F32)] * 3 + [jax.ShapeDtypeStruct((h_n, n, 1), F32)],
        [pltpu.VMEM((hb, t, dh), F32), pltpu.VMEM((hb, t, dh), F32), pltpu.VMEM((hb, t, 1), F32)])
    dq, dk, dv, dcc, *arrived = pl.pallas_call(
        body, name=f"attn_bwd_{tag}", grid=(ng, nb, nb),
        in_specs=in_specs, out_specs=out_specs, out_shape=out_shape, scratch_shapes=scratch,
        compiler_params=_params(("arbitrary", "arbitrary", "arbitrary")),
    )(q, k, v, dob, lse_row, delta, ccol, *extra)
    return [dq, dk, dv, dcc.reshape(h_n, 1, n)] + arrived


SCAN_STEPS = (1, 2, 4)


def ssm_scan(x, tab, *, reverse, name, s_prev=None, tt=512):
    n, width = x.shape
    nc, _, hw = tab.shape
    cw = 2 * hw
    assert width == nc * cw
    tt = _pick(n, tt, SUBLANES)
    nt = n // tt
    ng = tt // SUBLANES
    with_grad = s_prev is not None

    def body(*refs):
        if with_grad:
            x_ref, s_ref, tab_ref, o_ref, g_ref, car_ref = refs
        else:
            x_ref, tab_ref, o_ref, car_ref = refs

        @pl.when(pl.program_id(1) == 0)
        def _():
            car_ref[...] = jnp.zeros_like(car_ref)
            if with_grad:
                g_ref[...] = jnp.zeros_like(g_ref)

        q_re, q_im = tab_ref[0:8, :], tab_ref[8:16, :]
        p_re = [tab_ref[16 + i:17 + i, :] for i in range(3)]
        p_im = [tab_ref[24 + i:25 + i, :] for i in range(3)]
        row = lax.broadcasted_iota(jnp.int32, (SUBLANES, hw), 0)

        def group(gi, carry):
            c_re, c_im = carry
            g = (ng - 1 - gi) if reverse else gi
            r0 = pl.multiple_of(g * SUBLANES, SUBLANES)
            xr = x_ref[pl.ds(r0, SUBLANES), 0:hw]
            xi = x_ref[pl.ds(r0, SUBLANES), hw:cw]
            for i, d in enumerate(SCAN_STEPS):
                if reverse:
                    shift, keep = SUBLANES - d, row < SUBLANES - d
                else:
                    shift, keep = d, row >= d
                sr = jnp.where(keep, pltpu.roll(xr, shift, 0), 0.0)
                si = jnp.where(keep, pltpu.roll(xi, shift, 0), 0.0)
                xr, xi = (xr + p_re[i] * sr - p_im[i] * si,
                          xi + p_re[i] * si + p_im[i] * sr)
            xr, xi = (xr + q_re * c_re - q_im * c_im,
                      xi + q_re * c_im + q_im * c_re)
            o_ref[pl.ds(r0, SUBLANES), 0:hw] = xr
            o_ref[pl.ds(r0, SUBLANES), hw:cw] = xi
            if with_grad:
                nr = jnp.where(row < SUBLANES - 1, pltpu.roll(xr, SUBLANES - 1, 0), c_re)
                ni = jnp.where(row < SUBLANES - 1, pltpu.roll(xi, SUBLANES - 1, 0), c_im)
                sr = s_ref[pl.ds(r0, SUBLANES), 0:hw]
                si = s_ref[pl.ds(r0, SUBLANES), hw:cw]
                g_ref[:, 0:hw] += nr * sr + ni * si
                g_ref[:, hw:cw] += ni * sr - nr * si
            if reverse:
                return xr[0:1, :], xi[0:1, :]
            return xr[SUBLANES - 1:SUBLANES, :], xi[SUBLANES - 1:SUBLANES, :]

        c_re, c_im = lax.fori_loop(0, ng, group, (car_ref[0:1, 0:hw], car_ref[0:1, hw:cw]),
                                   unroll=min(ng, 4))
        car_ref[0:1, 0:hw] = c_re
        car_ref[0:1, hw:cw] = c_im

    if reverse:
        xspec = pl.BlockSpec((tt, cw), lambda c, t: (nt - 1 - t, c))
    else:
        xspec = pl.BlockSpec((tt, cw), lambda c, t: (t, c))
    tspec = pl.BlockSpec((None, 32, hw), lambda c, t: (c, 0, 0))
    in_specs = [xspec, xspec, tspec] if with_grad else [xspec, tspec]
    out_specs = [xspec]
    out_shape = [jax.ShapeDtypeStruct((n, width), F32)]
    if with_grad:
        out_specs.append(pl.BlockSpec((None, SUBLANES, cw), lambda c, t: (c, 0, 0)))
        out_shape.append(jax.ShapeDtypeStruct((nc, SUBLANES, cw), F32))
    operands = (x, s_prev, tab) if with_grad else (x, tab)
    return pl.pallas_call(
        body, name=name, grid=(nc, nt),
        in_specs=in_specs, out_specs=out_specs, out_shape=out_shape,
        scratch_shapes=[pltpu.VMEM((SUBLANES, cw), F32)],
        compiler_params=_params(("parallel", "arbitrary")),
    )(*operands)


def _slot(pos):
    return 4 * pos[0] + 2 * pos[1] + pos[2]


def _comm_scratch(n):
    return [pltpu.SemaphoreType.DMA((7 * n,)), pltpu.SemaphoreType.DMA((7 * n,)), pltpu.SemaphoreType.DMA((n,))]


def _gather_copies(ins, outs, sems):
    send_sems, recv_sems, local_sems = sems
    n = len(ins)
    x, y, c = lax.axis_index("x"), lax.axis_index("y"), lax.axis_index("c")
    me, sibling = (x, y, c), (x, y, 1 - c)
    chips = [(1 - x, y), (x, 1 - y), (1 - x, 1 - y)]

    def copy(t, k, block, to, src=None):
        dst = outs[t].at[_slot(block)]
        return pltpu.make_async_remote_copy(
            src_ref=dst if src is None else src, dst_ref=dst,
            send_sem=send_sems.at[7 * t + k], recv_sem=recv_sems.at[7 * t + k],
            device_id=to, device_id_type=MESH)

    jc = list(enumerate(chips))
    return dict(
        mine=[pltpu.make_async_copy(ins[t], outs[t].at[_slot(me)], local_sems.at[t]) for t in range(n)],
        first=[cp for t in range(n) for cp in
               [copy(t, 0, me, sibling, src=ins[t])] + [copy(t, 1 + j, me, (*chip, c), src=ins[t]) for j, chip in jc]],
        arrive=[copy(t, 1 + j, (*chip, c), me) for t in range(n) for j, chip in jc],
        passed=[copy(t, 4 + j, (*chip, c), sibling) for t in range(n) for j, chip in jc],
        from_sibling=[cp for t in range(n) for cp in
                      [copy(t, 0, sibling, me)] + [copy(t, 4 + j, (*chip, 1 - c), me) for j, chip in jc]])


def _gather_start(ins, outs, sems):
    cps = _gather_copies(ins, outs, sems)
    for cp in cps["mine"] + cps["first"]:
        cp.start()


def _gather_forward(ins, outs, sems):
    cps = _gather_copies(ins, outs, sems)
    for arrived, onward in zip(cps["arrive"], cps["passed"]):
        arrived.wait_recv()
        onward.start()


def _gather_finish(ins, outs, sems):
    cps = _gather_copies(ins, outs, sems)
    for cp in cps["from_sibling"]:
        cp.wait_recv()
    for cp in cps["first"] + cps["passed"]:
        cp.wait_send()
    for cp in cps["mine"]:
        cp.wait()


def gather_comm(arrs):
    return dict(ins=list(arrs), out_shape=[jax.ShapeDtypeStruct((N_DEV,) + a.shape, a.dtype) for a in arrs],
                scratch=_comm_scratch(len(arrs)), start=_gather_start, mid=_gather_forward, finish=_gather_finish)


def _exchange_copies(ins, outs, sems):
    send_sems, recv_sems, local_sems = sems
    n = len(ins)
    me = (lax.axis_index("x"), lax.axis_index("y"), lax.axis_index("c"))
    peers = []
    for k in range(1, N_DEV):
        flip = ((k >> 2) & 1, (k >> 1) & 1, k & 1)
        peers.append(tuple(1 - p if f else p for p, f in zip(me, flip)))

    def copy(t, k, peer, dst_slot):
        return pltpu.make_async_remote_copy(
            src_ref=ins[t].at[_slot(peer)], dst_ref=outs[t].at[dst_slot],
            send_sem=send_sems.at[7 * t + k], recv_sem=recv_sems.at[7 * t + k],
            device_id=peer, device_id_type=MESH)

    return dict(
        mine=[pltpu.make_async_copy(ins[t].at[_slot(me)], outs[t].at[_slot(me)], local_sems.at[t])
              for t in range(n)],
        send=[copy(t, k, peer, _slot(me)) for t in range(n) for k, peer in enumerate(peers)],
        both=[copy(t, k, peer, _slot(peer)) for t in range(n) for k, peer in enumerate(peers)])


def _exchange_start(ins, outs, sems):
    cps = _exchange_copies(ins, outs, sems)
    for cp in cps["mine"] + cps["send"]:
        cp.start()


def _exchange_finish(ins, outs, sems):
    cps = _exchange_copies(ins, outs, sems)
    for cp in cps["both"]:
        cp.wait()
    for cp in cps["mine"]:
        cp.wait()


def exchange_comm(arrs):
    return dict(ins=list(arrs), out_shape=[jax.ShapeDtypeStruct(a.shape, a.dtype) for a in arrs],
                scratch=_comm_scratch(len(arrs)), start=_exchange_start, mid=None, finish=_exchange_finish)


def run_comm(comm, *, name):
    n_in, n_out = len(comm["ins"]), len(comm["out_shape"])

    def body(*refs):
        ins, outs, sems = refs[:n_in], refs[n_in:n_in + n_out], refs[n_in + n_out:]
        comm["start"](ins, outs, sems)
        if comm["mid"] is not None:
            comm["mid"](ins, outs, sems)
        comm["finish"](ins, outs, sems)

    any_spec = pl.BlockSpec(memory_space=pl.ANY)
    return pl.pallas_call(
        body, name=name, in_specs=[any_spec] * n_in, out_specs=[any_spec] * n_out,
        out_shape=comm["out_shape"], scratch_shapes=comm["scratch"],
    )(*comm["ins"])


class _Hosted:
    def __init__(self, comm, n_in, n_out, n_scratch):
        self.comm = comm
        self.n_ci = len(comm["ins"]) if comm else 0
        self.n_co = len(comm["out_shape"]) if comm else 0
        self.n_in, self.n_out, self.n_scratch = n_in, n_out, n_scratch

    def split(self, refs):
        a = self.n_in
        b = a + self.n_ci
        c = b + self.n_out
        e = c + self.n_co
        f = e + self.n_scratch
        return refs[:a] + refs[b:c] + refs[e:f], (refs[a:b], refs[c:e], refs[f:])

    def phase(self, which, when, crefs):
        fn = self.comm[which] if self.comm else None
        if fn is not None:
            pl.when(when)(lambda: fn(*crefs))

    def specs(self, in_specs, out_specs, out_shape, scratch):
        any_spec = pl.BlockSpec(memory_space=pl.ANY)
        if not self.comm:
            return in_specs, out_specs, out_shape, scratch, ()
        return (in_specs + [any_spec] * self.n_ci, out_specs + [any_spec] * self.n_co,
                out_shape + self.comm["out_shape"], scratch + self.comm["scratch"], tuple(self.comm["ins"]))


def all_gather(arrs, *, name):
    return run_comm(gather_comm(arrs), name=name)


def _discretise(a_re, a_im, log_dt, b_re, b_im):
    ar = jnp.minimum(a_re, -1e-4)
    dt = jnp.exp(log_dt)[:, None]
    e, ph = ar * dt, a_im * dt
    mag = jnp.exp(e)
    lr, li = mag * jnp.cos(ph), mag * jnp.sin(ph)
    den = ar * ar + a_im * a_im
    nr, ni = lr - 1.0, li
    cr = (nr * ar + ni * a_im) / den
    ci = (ni * ar - nr * a_im) / den
    bb_re = cr[..., None] * b_re - ci[..., None] * b_im
    bb_im = cr[..., None] * b_im + ci[..., None] * b_re
    return e, ph, bb_re, bb_im


def _lam_pow(e, ph, k, conj):
    mag = jnp.exp(k * e)
    return mag * jnp.cos(k * ph), (-1.0 if conj else 1.0) * mag * jnp.sin(k * ph)


def _scan_table(e, ph, nc, reverse):
    hw = e.size // nc
    e, ph = e.reshape(nc, 1, hw), ph.reshape(nc, 1, hw)
    j = jnp.arange(SUBLANES, dtype=F32).reshape(1, SUBLANES, 1)
    kq = (SUBLANES - j) if reverse else (j + 1.0)
    q_re, q_im = _lam_pow(e, ph, kq, reverse)
    kp = jnp.array(SCAN_STEPS + (0,) * 5, F32).reshape(1, SUBLANES, 1)
    p_re, p_im = _lam_pow(e, ph, kp, reverse)
    return jnp.concatenate([q_re, q_im, p_re, p_im], axis=1)


def _blockdiag(m, nc):
    g, a, b = m.shape
    gc = g // nc
    m = m.reshape(nc, gc, a, b)
    eye = jnp.eye(gc, dtype=m.dtype)
    return jnp.einsum("cgab,gh->cgahb", m, eye).reshape(nc, gc * a, gc * b)


def _blockdiag_take(m, g):
    nc = m.shape[0]
    gc = g // nc
    a, b = m.shape[1] // gc, m.shape[2] // gc
    m = m.reshape(nc, gc, a, gc, b)
    eye = jnp.eye(gc, dtype=m.dtype)
    return jnp.einsum("cgahb,gh->cgab", m, eye).reshape(g, a, b)


def kernel(x, c, mod_w, mod_b, norm_pre, norm_post, ffn_w_in, ffn_w_out, mix_w_in, forget_b, ssm_a_re, ssm_a_im, ssm_log_dt, ssm_b_re, ssm_b_im, ssm_c_re, ssm_c_im, ssm_d, glu_w, attn_w_out, mix_w_out, loss_target, m_mod_w, m_mod_b, m_norm_pre, m_norm_post, m_ffn_w_in, m_ffn_w_out, m_mix_w_in, m_forget_b, m_ssm_a_re, m_ssm_a_im, m_ssm_log_dt, m_ssm_b_re, m_ssm_b_im, m_ssm_c_re, m_ssm_c_im, m_ssm_d, m_glu_w, m_attn_w_out, m_mix_w_out, v_mod_w, v_mod_b, v_norm_pre, v_norm_post, v_ffn_w_in, v_ffn_w_out, v_mix_w_in, v_forget_b, v_ssm_a_re, v_ssm_a_im, v_ssm_log_dt, v_ssm_b_re, v_ssm_b_im, v_ssm_c_re, v_ssm_c_im, v_ssm_d, v_glu_w, v_attn_w_out, v_mix_w_out):
    names = ["mod_w", "mod_b", "norm_pre", "norm_post", "ffn_w_in", "ffn_w_out", "mix_w_in", "forget_b",
             "ssm_a_re", "ssm_a_im", "ssm_log_dt", "ssm_b_re", "ssm_b_im", "ssm_c_re", "ssm_c_im", "ssm_d",
             "glu_w", "attn_w_out", "mix_w_out"]
    w_in = dict(zip(names, [mod_w, mod_b, norm_pre, norm_post, ffn_w_in, ffn_w_out, mix_w_in, forget_b,
                            ssm_a_re, ssm_a_im, ssm_log_dt, ssm_b_re, ssm_b_im, ssm_c_re, ssm_c_im, ssm_d,
                            glu_w, attn_w_out, mix_w_out]))
    m_in = dict(zip(names, [m_mod_w, m_mod_b, m_norm_pre, m_norm_post, m_ffn_w_in, m_ffn_w_out, m_mix_w_in,
                            m_forget_b, m_ssm_a_re, m_ssm_a_im, m_ssm_log_dt, m_ssm_b_re, m_ssm_b_im,
                            m_ssm_c_re, m_ssm_c_im, m_ssm_d, m_glu_w, m_attn_w_out, m_mix_w_out]))
    v_in = dict(zip(names, [v_mod_w, v_mod_b, v_norm_pre, v_norm_post, v_ffn_w_in, v_ffn_w_out, v_mix_w_in,
                            v_forget_b, v_ssm_a_re, v_ssm_a_im, v_ssm_log_dt, v_ssm_b_re, v_ssm_b_im,
                            v_ssm_c_re, v_ssm_c_im, v_ssm_d, v_glu_w, v_attn_w_out, v_mix_w_out]))

    depth = mod_w.shape[0]
    n_tok, d = x.shape[1], x.shape[2]
    ff = ffn_w_out.shape[2] * N_DEV
    heads = forget_b.shape[1]
    sw = ssm_d.shape[1]
    g_n, p_n, n_n = ssm_b_re.shape[1:]
    aw = attn_w_out.shape[1]
    dh = aw // heads
    iw = mix_w_in.shape[2] * N_DEV
    nc = sw // LANES
    hw = g_n * p_n // nc
    mod_cols = mod_w.shape[2]
    scale = dh ** -0.5
    assert iw == sw + 3 * aw + heads + 2 * d and heads <= LANES
    assert math.log2(scale).is_integer(), "q is pre-scaled in bf16: exact only for a power of two"
    off_u, off_q, off_f = 2 * d, 2 * d + sw, 2 * d + sw + 3 * aw
    iwp = off_f + LANES
    assert off_u % sw == 0 and off_q % aw == 0 and off_f % LANES == 0

    me = 4 * lax.axis_index("x") + 2 * lax.axis_index("y") + lax.axis_index("c")
    x2 = x.reshape(n_tok, d)
    tgt = loss_target.reshape(n_tok, d)

    silu_c = rowwise(_f_silu, [c], [], [((d,), F32)], name="silu_c")[0]
    big = ["ffn_w_in", "ffn_w_out", "mix_w_in", "glu_w", "attn_w_out", "mix_w_out"]
    ffn1 = [("ffn_w_in", 0), ("ffn_w_out", 0)]
    rest = [("ffn_w_in", 1), ("ffn_w_out", 1), ("mix_w_in", None), ("glu_w", None), ("attn_w_out", None),
            ("mix_w_out", None)]

    def piece_of(dct, piece, l):
        name, j = piece
        return dct[name][l] if j is None else dct[name][l][j]

    def shards(l, pieces):
        return [piece_of(w_in, p, l).astype(BF16) for p in pieces]

    cut = [0, sw, sw + aw, sw + 2 * aw, sw + 3 * aw, sw + 3 * aw + heads, sw + 3 * aw + heads + d, iw]
    lw = [dict(win=[None, None], wout=[None, None]) for _ in range(depth)]

    def install(l, pieces, gathered):
        for (name, j), g in zip(pieces, gathered):
            if name == "ffn_w_in":
                lw[l]["win"][j] = jnp.moveaxis(g, 0, 1).reshape(d, 2 * ff)
            elif name == "ffn_w_out":
                lw[l]["wout"][j] = g.reshape(ff, d)
            elif name == "mix_w_in":
                w_mix_in = jnp.moveaxis(g, 0, 1).reshape(d, iw)
                seg = lambda i: w_mix_in[:, cut[i]:cut[i + 1]]
                lw[l]["wmi"] = jnp.concatenate([seg(5), seg(6), seg(0), seg(1), seg(2), seg(3),
                                                jnp.pad(seg(4), ((0, 0), (0, LANES - heads)))], axis=-1)
            elif name == "mix_w_out":
                lw[l]["mo"] = g.reshape(d, d)
            else:
                lw[l]["glu" if name == "glu_w" else "ao"] = jnp.moveaxis(g, 0, 1).reshape(g.shape[1], -1)

    gathered = all_gather(
        [silu_c, norm_pre.reshape(-1, norm_pre.shape[-1]), norm_post.reshape(-1, norm_post.shape[-1])]
        + shards(0, ffn1), name="gather_first")
    sc_all = gathered[0].reshape(N_DEV, d)
    gpre = jnp.moveaxis(gathered[1].reshape(N_DEV, depth, 3, -1), 0, 2).reshape(depth, 3, d)
    gpost = jnp.moveaxis(gathered[2].reshape(N_DEV, depth, 3, -1), 0, 2).reshape(depth, 3, d)
    install(0, ffn1, gathered[3:])

    sc_pad = jnp.pad(sc_all, ((0, LANES - N_DEV), (0, 0)))
    mod_part = jnp.stack([mm(sc_pad, mod_w[l], name=f"mod_fwd{l}")[:N_DEV] for l in range(depth)], axis=1)
    mod_part = mod_part + lax.dynamic_slice_in_dim(mod_b, me * mod_cols, mod_cols, axis=1)[None]
    mod_all = all_gather([mod_part], name="gather_mod")[0]
    mod_own = lax.dynamic_index_in_dim(mod_all, me, axis=1, keepdims=False)
    mod_own = mod_own.transpose(1, 0, 2).reshape(depth, 3, 3, d)
    res_w = (FFN_RES, 1.0, FFN_RES)

    def vec_a(l, i):
        return (gpre[l, i] * (1.0 + mod_own[l, i, 1])).reshape(1, d)

    def vec_sh(l, i):
        return mod_own[l, i, 0].reshape(1, d)

    def vec_b(l, i):
        return (res_w[i] * mod_own[l, i, 2] * gpost[l, i]).reshape(1, d)

    ssm = []
    for l in range(depth):
        (e, ph, bb_re, bb_im), disc_vjp = jax.vjp(_discretise,ssm_a_re[l], ssm_a_im[l], ssm_log_dt[l],
                                                  ssm_b_re[l], ssm_b_im[l])
        b_mat = jnp.concatenate([_blockdiag(bb_re.transpose(0, 2, 1), nc),
                                 _blockdiag(bb_im.transpose(0, 2, 1), nc)], axis=2)
        c_mat = jnp.concatenate([_blockdiag(ssm_c_re[l].transpose(0, 2, 1), nc),
                                 _blockdiag(-ssm_c_im[l].transpose(0, 2, 1), nc)], axis=1)
        ssm.append(dict(e=e, ph=ph, vjp=disc_vjp, b=b_mat.astype(BF16), c=c_mat.astype(BF16),
                        bt=b_mat.transpose(0, 2, 1).astype(BF16), ct=c_mat.transpose(0, 2, 1).astype(BF16),
                        tab_f=_scan_table(e, ph, nc, False), tab_r=_scan_table(e, ph, nc, True),
                        dvec=ssm_d[l].reshape(1, sw)))

    fb_pad = jnp.pad(forget_b, ((0, 0), (0, LANES - heads)))

    def heads_first(a):
        return a.reshape(n_tok, heads, dh).transpose(1, 0, 2)

    def heads_last(a):
        return a.transpose(1, 0, 2).reshape(n_tok, heads * dh)

    def mm_hosting(a, b, comm, **kw):
        if comm is None:
            return mm(a, b, **kw), []
        out, *arrived = mm(a, b, comm=comm, **kw)
        return out, arrived

    def ffn_fwd(xin, l, i, j, tag, comm=None):
        h = rowwise(_f_pre, [xin], [vec_a(l, i), vec_sh(l, i)], [((d,), BF16)], name=f"pre_{tag}")[0]
        a, arrived = mm_hosting(h, lw[l]["win"][j], comm, name=f"ffn_in_{tag}", out_dtype=BF16)
        m = rowwise(_f_swiglu, [(a, ff, 0), (a, ff, 1)], [], [((ff,), BF16)], name=f"swiglu_{tag}")[0]
        y = mm(m, lw[l]["wout"][j], name=f"ffn_out_{tag}")
        xout = rowwise(_f_post_add, [xin, y], [vec_b(l, i)], [((d,), F32)], name=f"post_{tag}")[0]
        return xout, dict(x=xin, h=h, a=a, m=m, y=y), arrived

    def mixer_fwd(xin, l, tag):
        s5 = ssm[l]
        h = rowwise(_f_pre, [xin], [vec_a(l, 1), vec_sh(l, 1)], [((d,), BF16)], name=f"pre_{tag}")[0]
        proj = mm(h, lw[l]["wmi"], name=f"mix_in_{tag}")
        bu = mm_blockdiag(proj, s5["b"], a_cb0=off_u // LANES, name=f"ssm_bu_{tag}")
        st = ssm_scan(bu, s5["tab_f"], reverse=False, name=f"ssm_scan_{tag}")[0]
        y0 = mm_blockdiag(st, s5["c"], name=f"ssm_y_{tag}")
        ge, ys = rowwise(_f_gelu_in, [y0, (proj, sw, off_u // sw)], [s5["dvec"]],
                         [((sw,), BF16), ((sw,), F32)], name=f"gelu_{tag}")
        z = mm(ge, lw[l]["glu"], name=f"glu_{tag}", out_dtype=BF16)
        cum = cum_fwd(proj, fb_pad[l:l + 1], col_block=off_f // LANES, name=f"cum_{tag}")
        crow = cum[:, :heads].T[:, None, :]
        q, k, v = [heads_first(proj[:, off_q + i * aw:off_q + (i + 1) * aw] * sc_).astype(BF16)
                   for i, sc_ in enumerate((scale, 1.0, 1.0))]
        nxt = gather_comm(shards(l + 1, ffn1 + rest)) if l + 1 < depth else None
        o, lse, *arrived = attn_fwd(q, k, v, crow, name=f"attn_{tag}", comm=nxt)
        if nxt is not None:
            install(l + 1, ffn1 + rest, arrived)
        attn = heads_last(o).astype(BF16)
        yb = mm(attn, lw[l]["ao"], name=f"attn_out_{tag}", out_dtype=BF16)
        mg = rowwise(_f_merge, [(z, d, 0), (z, d, 1), yb, (proj, d, 0), (proj, d, 1)], [],
                     [((d,), BF16)], name=f"merge_{tag}")[0]
        y = mm(mg, lw[l]["mo"], name=f"mix_out_{tag}")
        xout = rowwise(_f_post_add, [xin, y], [vec_b(l, 1)], [((d,), F32)], name=f"post_{tag}")[0]
        saved = dict(x=xin, h=h, proj=proj, st=st, ys=ys, ge=ge, z=z, q=q, k=k, v=v, o=o, lse=lse,
                     crow=crow, attn=attn, yb=yb, mg=mg, y=y)
        return xout, saved

    saved = []
    xc = x2
    for l in range(depth):
        xc, s0, arrived = ffn_fwd(xc, l, 0, 0, f"l{l}a", gather_comm(shards(0, rest)) if l == 0 else None)
        if l == 0:
            install(0, rest, arrived)
        xc, s1 = mixer_fwd(xc, l, f"l{l}m")
        xc, s2, _ = ffn_fwd(xc, l, 2, 1, f"l{l}b")
        saved.append((s0, s1, s2))

    def f_loss(xf, t):
        e_ = xf - t
        return e_ * (1.0 / d), _colsum(e_ * e_)

    dx, sq = rowwise(f_loss, [xc, tgt], [], [((d,), F32)], [d], name="loss_head")
    loss_part = 0.5 * jnp.sum(sq) / d

    grads = {k: [None] * depth for k in big}
    small_g = [dict() for _ in range(depth)]
    dmod = [[None] * 3 for _ in range(depth)]
    dgpre = [[None] * 3 for _ in range(depth)]
    dgpost = [[None] * 3 for _ in range(depth)]

    def norm_grads(l, i, d_a, d_sh, d_bv):
        d_a, d_sh, d_bv = d_a.reshape(d), d_sh.reshape(d), d_bv.reshape(d)
        dmod[l][i] = jnp.stack([d_sh, d_a * gpre[l, i], res_w[i] * gpost[l, i] * d_bv])
        dgpre[l][i] = d_a * (1.0 + mod_own[l, i, 1])
        dgpost[l][i] = res_w[i] * mod_own[l, i, 2] * d_bv

    def ffn_bwd(dxo, sv, l, i, j, tag, comm_dw=None, comm_dx_of=None):
        dy, d_bv = rowwise(_f_post_bwd, [dxo, sv["y"]], [vec_b(l, i)], [((d,), BF16)], [d],
                           name=f"post_bwd_{tag}")
        dm = mm(dy, lw[l]["wout"][j], trans_b=True, name=f"ffn_out_dx_{tag}", out_dtype=BF16)
        g_out = mm(sv["m"], dy, trans_a=True, name=f"ffn_out_dw_{tag}", out_dtype=BF16, tm=1408, tn=1024)
        da = rowwise(_f_swiglu_bwd, [(sv["a"], ff, 0), (sv["a"], ff, 1), dm], [], [((ff, ff), BF16)],
                     name=f"swiglu_bwd_{tag}")[0]
        g_in, arrived_dw = mm_hosting(sv["h"], da, comm_dw, trans_a=True, name=f"ffn_in_dw_{tag}",
                                      out_dtype=BF16)
        comm_dx = comm_dx_of(g_in, g_out) if comm_dx_of is not None else None
        dh_, arrived_dx = mm_hosting(da, lw[l]["win"][j], comm_dx, trans_b=True, name=f"ffn_in_dx_{tag}")
        dxn, d_a, d_sh = rowwise(_f_pre_bwd, [dxo, dh_, sv["x"]], [vec_a(l, i)], [((d,), F32)], [d, d],
                                 name=f"pre_bwd_{tag}")
        norm_grads(l, i, d_a, d_sh, d_bv)
        return dxn, g_in, g_out, arrived_dw, arrived_dx

    def mixer_bwd(dxo, sv, l, tag, comm):
        s5 = ssm[l]
        proj = sv["proj"]
        dy, d_bv = rowwise(_f_post_bwd, [dxo, sv["y"]], [vec_b(l, 1)], [((d,), BF16)], [d],
                           name=f"post_bwd_{tag}")
        dmg = mm(dy, lw[l]["mo"], trans_b=True, name=f"mix_out_dx_{tag}", out_dtype=BF16)
        g_mo = mm(sv["mg"], dy, trans_a=True, name=f"mix_out_dw_{tag}", out_dtype=BF16)
        dz, dyb, dgab = rowwise(
            _f_merge_bwd, [dmg, (sv["z"], d, 0), (sv["z"], d, 1), sv["yb"], (proj, d, 0), (proj, d, 1)], [],
            [((d, d), BF16), ((d,), BF16), ((d, d), BF16)], name=f"merge_bwd_{tag}")
        dge = mm(dz, lw[l]["glu"], trans_b=True, name=f"glu_dx_{tag}")
        g_glu = mm(sv["ge"], dz, trans_a=True, name=f"glu_dw_{tag}", out_dtype=BF16)
        dys, d_dvec = rowwise(_f_gelu_bwd, [dge, sv["ys"], (proj, sw, off_u // sw)], [], [((sw,), BF16)],
                              [sw], name=f"gelu_bwd_{tag}")
        gadj = mm_blockdiag(dys, s5["ct"], name=f"ssm_dy_{tag}")
        adj, dlam8 = ssm_scan(gadj, s5["tab_r"], reverse=True, s_prev=sv["st"], name=f"ssm_scan_bwd_{tag}")
        du0 = mm_blockdiag(adj, s5["bt"], name=f"ssm_du_{tag}")
        d_bmat = mm_blockdiag_tn(proj, adj, g_n=nc, ka=LANES, kb=2 * hw, a_cb0=off_u // LANES,
                                 name=f"ssm_db_{tag}")
        d_cmat = mm_blockdiag_tn(sv["st"], dys, g_n=nc, ka=2 * hw, kb=LANES, name=f"ssm_dc_{tag}")
        du = rowwise(_f_du_fin, [du0, dys], [s5["dvec"]], [((sw,), BF16)], name=f"ssm_du_fin_{tag}")[0]
        dlam = jnp.sum(dlam8, axis=1)
        dlam_re, dlam_im = dlam[:, :hw].reshape(g_n, p_n), dlam[:, hw:].reshape(g_n, p_n)
        dbb_re = _blockdiag_take(d_bmat[:, :, :hw], g_n).transpose(0, 2, 1)
        dbb_im = _blockdiag_take(d_bmat[:, :, hw:], g_n).transpose(0, 2, 1)
        mag = jnp.exp(s5["e"])
        lr, li = mag * jnp.cos(s5["ph"]), mag * jnp.sin(s5["ph"])
        d_e = dlam_re * lr + dlam_im * li
        d_ph = -dlam_re * li + dlam_im * lr
        da_re, da_im, dlog_dt, db_re, db_im = s5["vjp"]((d_e, d_ph, dbb_re, dbb_im))
        small_g[l].update(
            ssm_a_re=da_re, ssm_a_im=da_im, ssm_log_dt=dlog_dt, ssm_b_re=db_re, ssm_b_im=db_im,
            ssm_c_re=_blockdiag_take(d_cmat[:, :hw, :], g_n).transpose(0, 2, 1),
            ssm_c_im=-_blockdiag_take(d_cmat[:, hw:, :], g_n).transpose(0, 2, 1),
            ssm_d=d_dvec.reshape(sw))
        dattn = mm(dyb, lw[l]["ao"], trans_b=True, name=f"attn_out_dx_{tag}", out_dtype=BF16)
        g_ao = mm(sv["attn"], dyb, trans_a=True, name=f"attn_out_dw_{tag}", out_dtype=BF16)
        dq, dk, dv, dcr, *arrived = attn_bwd(sv["q"], sv["k"], sv["v"], heads_first(dattn), sv["o"], sv["lse"],
                                             sv["crow"], scale=scale, tag=tag, comm=comm)
        dcum = jnp.pad(dcr[:, 0, :].T, ((0, 0), (0, LANES - heads)))
        df, dfb = cum_bwd(dcum, proj, fb_pad[l:l + 1], col_block=off_f // LANES, name=f"cum_bwd_{tag}")
        small_g[l]["forget_b"] = dfb[0, :heads]
        dproj = jnp.concatenate([dgab, du, heads_last(dq).astype(BF16), heads_last(dk).astype(BF16),
                                 heads_last(dv).astype(BF16), df.astype(BF16)], axis=1)
        g_mi = mm(sv["h"], dproj, trans_a=True, name=f"mix_in_dw_{tag}", out_dtype=BF16)
        dh_ = mm(dproj, lw[l]["wmi"], trans_b=True, name=f"mix_in_dx_{tag}")
        dxn, d_a, d_sh = rowwise(_f_pre_bwd, [dxo, dh_, sv["x"]], [vec_a(l, 1)], [((d,), F32)], [d, d],
                                 name=f"pre_bwd_{tag}")
        norm_grads(l, 1, d_a, d_sh, d_bv)
        g_mi = jnp.concatenate([g_mi[:, off_u:off_f + heads], g_mi[:, :off_u]], axis=1)
        return dxn, g_mi, g_glu, g_ao, g_mo, arrived

    def split_last(a):
        return jnp.moveaxis(a.reshape(a.shape[:-1] + (N_DEV, a.shape[-1] // N_DEV)), -2, 0)

    def split_rows(a):
        return jnp.moveaxis(a.reshape(a.shape[:-2] + (N_DEV, a.shape[-2] // N_DEV, a.shape[-1])), -3, 0)

    def owner_blocks(l, pieces):
        out = []
        for name, j in pieces:
            if name == "ffn_w_in":
                out.append(split_last(g_ffn_in[l][j]))
            elif name == "ffn_w_out":
                out.append(split_rows(g_ffn_out[l][j]))
            elif name == "mix_w_out":
                out.append(split_rows(grads[name][l]))
            else:
                out.append(split_last(grads[name][l]))
        return out

    g_ffn_in = [[None, None] for _ in range(depth)]
    g_ffn_out = [[None, None] for _ in range(depth)]
    parts = {}

    def record(l, pieces, arrived):
        for p, a in zip(pieces, arrived):
            parts[(p, l)] = a

    def last_ffn_blocks(g_in, g_out):
        return exchange_comm([split_last(g_in), split_rows(g_out)])

    for l in reversed(range(depth)):
        s0, s1, s2 = saved[l]
        dx, g_ffn_in[l][1], g_ffn_out[l][1], _, _ = ffn_bwd(dx, s2, l, 2, 1, f"l{l}b")
        pending = exchange_comm(owner_blocks(l + 1, ffn1 + rest)) if l + 1 < depth else None
        (dx, grads["mix_w_in"][l], grads["glu_w"][l], grads["attn_w_out"][l], grads["mix_w_out"][l],
         arrived) = mixer_bwd(dx, s1, l, f"l{l}m", pending)
        if pending is not None:
            record(l + 1, ffn1 + rest, arrived)
        if l == 0:
            dx, g_ffn_in[l][0], g_ffn_out[l][0], arrived_rest, arrived_ffn1 = ffn_bwd(
                dx, s0, l, 0, 0, f"l{l}a", exchange_comm(owner_blocks(0, rest)), last_ffn_blocks)
            record(0, rest, arrived_rest)
            record(0, ffn1, arrived_ffn1)
        else:
            dx, g_ffn_in[l][0], g_ffn_out[l][0], _, _ = ffn_bwd(dx, s0, l, 0, 0, f"l{l}a")
    grad_x = dx.reshape(x.shape)

    small_names = ["forget_b", "ssm_a_re", "ssm_a_im", "ssm_log_dt", "ssm_b_re", "ssm_b_im", "ssm_c_re",
                   "ssm_c_im", "ssm_d"]
    pieces = [loss_part.reshape(1), jnp.stack([jnp.stack(dmod[l]) for l in range(depth)]).reshape(-1),
              jnp.stack([jnp.stack(dgpre[l]) for l in range(depth)]).reshape(-1),
              jnp.stack([jnp.stack(dgpost[l]) for l in range(depth)]).reshape(-1)]
    pieces += [jnp.stack([small_g[l][k] for l in range(depth)]).reshape(-1) for k in small_names]
    sizes = [p.size for p in pieces]
    chunk = SUBLANES * 1024
    total = -(-sum(sizes) // chunk) * chunk
    pack = jnp.pad(jnp.concatenate(pieces), (0, total - sum(sizes))).reshape(total // 1024, 1024)
    pack_all = all_gather([pack], name="gather_small_grads")[0]
    pack_sum = rowwise(_f_sum_parts, [(pack_all, p) for p in range(N_DEV)], [], [((1024,), F32)],
                       name="sum_small_grads")[0].reshape(-1)
    offs = [0]
    for s_ in sizes:
        offs.append(offs[-1] + s_)
    take = lambda i: pack_sum[offs[i]:offs[i + 1]]
    loss = take(0).reshape(())
    g_small = {"mod_b": take(1).reshape(mod_b.shape)}
    g_pre_full, g_post_full = take(2).reshape(depth, 3, d), take(3).reshape(depth, 3, d)
    shard = norm_pre.shape[-1]
    g_small["norm_pre"] = lax.dynamic_slice_in_dim(g_pre_full, me * shard, shard, axis=2)
    g_small["norm_post"] = lax.dynamic_slice_in_dim(g_post_full, me * shard, shard, axis=2)
    for i, k in enumerate(small_names):
        g_small[k] = take(4 + i).reshape(w_in[k].shape)

    dmod_all = pack_all.reshape(N_DEV, -1)[:, offs[1]:offs[2]].reshape(N_DEV, depth, 9 * d)
    dmod_mine = lax.dynamic_slice_in_dim(dmod_all, me * mod_cols, mod_cols, axis=2)
    sct_pad = jnp.pad(sc_all.T, ((0, 0), (0, LANES - N_DEV)))
    g_mod_w = jnp.stack([
        mm(sct_pad, jnp.pad(dmod_mine[:, l], ((0, LANES - N_DEV), (0, 0))), name=f"mod_dw{l}")
        for l in range(depth)])

    out_g, out_d, out_m, out_v = {}, {}, {}, {}
    flat = lambda a: a.reshape(-1, a.shape[-1])
    res = adamw(g_mod_w.reshape(1, -1, mod_cols), flat(mod_w), flat(m_mod_w), flat(v_mod_w), name="adamw_mod_w")
    out_g["mod_w"], out_d["mod_w"], out_m["mod_w"], out_v["mod_w"] = [r.reshape(mod_w.shape) for r in res]
    updated = {}
    for l in range(depth):
        for p in ffn1 + rest:
            w_p = piece_of(w_in, p, l)
            res = adamw(parts[(p, l)], flat(w_p), flat(piece_of(m_in, p, l)), flat(piece_of(v_in, p, l)),
                        name=f"adamw_{p[0]}_l{l}" + ("" if p[1] is None else f"_{p[1]}"))
            updated[(p, l)] = [r.reshape(w_p.shape) for r in res]
    for k in big:
        for q_, dct in enumerate((out_g, out_d, out_m, out_v)):
            if k.startswith("ffn"):
                dct[k] = jnp.stack([jnp.stack([updated[((k, j), l)][q_] for j in range(2)])
                                    for l in range(depth)])
            else:
                dct[k] = jnp.stack([updated[((k, None), l)][q_] for l in range(depth)])
    small_all = ["mod_b", "norm_pre", "norm_post"] + small_names

    def pack_small(dct):
        flat = jnp.concatenate([dct[k].reshape(-1) for k in small_all])
        tot = -(-flat.size // chunk) * chunk
        return jnp.pad(flat, (0, tot - flat.size)).reshape(tot // 1024, 1024)

    res = adamw(pack_small(g_small)[None], pack_small(w_in), pack_small(m_in), pack_small(v_in),
                name="adamw_small")
    pos = 0
    for k in small_all:
        size = w_in[k].size
        for dct, r in zip((out_g, out_d, out_m, out_v), res):
            dct[k] = r.reshape(-1)[pos:pos + size].reshape(w_in[k].shape)
        pos += size

    return (loss, grad_x, *[out_g[k] for k in names], *[out_d[k] for k in names],
            *[out_m[k] for k in names], *[out_v[k] for k in names])
```

```python
import functools
import math

import jax
import jax.numpy as jnp
from jax import lax
from jax.experimental import pallas as pl
from jax.experimental.pallas import tpu as pltpu

F32 = jnp.float32
BF16 = jnp.bfloat16
MESH = pl.DeviceIdType.MESH
N_DEV = 8
LANES = 128
SUBLANES = 8
VMEM_LIMIT = 48 * 1024 * 1024

RMS_EPS = 1e-6
FFN_RES = 0.5
ADAM_LR = 0.001
ADAM_B1 = 0.9
ADAM_B2 = 0.999
ADAM_EPS = 1e-08
ADAM_WD = 0.01
ADAM_STEP = 10
GELU_C = math.sqrt(2.0 / math.pi)
GELU_A = 0.044715


def _pick(dim, target, mult=LANES):
    t = (min(dim, target) // mult) * mult
    while t >= mult:
        if dim % t == 0:
            return t
        t -= mult
    return dim


def _params(sem):
    return pltpu.CompilerParams(dimension_semantics=sem, vmem_limit_bytes=VMEM_LIMIT)


def _sigmoid(x):
    return 1.0 / (1.0 + jnp.exp(-x))


def mm(a, b, *, name, trans_a=False, trans_b=False, out_dtype=F32, tm=1024, tn=1408, tk=2816, comm=None):
    if trans_a:
        kdim, m = a.shape
    else:
        m, kdim = a.shape
    if trans_b:
        n, kb = b.shape
    else:
        kb, n = b.shape
    assert kdim == kb, (a.shape, b.shape)
    tm, tn, tk = _pick(m, tm), _pick(n, tn), _pick(kdim, tk)
    gm, gn, nk = m // tm, n // tn, kdim // tk
    dims = (((0 if trans_a else 1,), (1 if trans_b else 0,)), ((), ()))
    host = _Hosted(comm, 2, 1, 1 if nk > 1 else 0)

    def body(*refs):
        (a_ref, b_ref, o_ref, *acc), crefs = host.split(refs)
        i, j, k = pl.program_id(0), pl.program_id(1), pl.program_id(2)
        host.phase("start", (i == 0) & (j == 0) & (k == 0), crefs)
        prod = lax.dot_general(a_ref[...].astype(BF16), b_ref[...].astype(BF16), dims,
                               preferred_element_type=F32)
        if nk == 1:
            o_ref[...] = prod.astype(out_dtype)
        else:
            acc_ref, = acc

            @pl.when(k == 0)
            def _():
                acc_ref[...] = prod

            @pl.when((k > 0) & (k < nk - 1))
            def _():
                acc_ref[...] += prod

            @pl.when(k == nk - 1)
            def _():
                o_ref[...] = (acc_ref[...] + prod).astype(out_dtype)

        last = (i == gm - 1) & (j == gn - 1) & (k == nk - 1)
        host.phase("mid", last, crefs)
        host.phase("finish", last, crefs)

    a_spec = (pl.BlockSpec((tk, tm), lambda i, j, k: (k, i)) if trans_a
              else pl.BlockSpec((tm, tk), lambda i, j, k: (i, k)))
    b_spec = (pl.BlockSpec((tn, tk), lambda i, j, k: (j, k)) if trans_b
              else pl.BlockSpec((tk, tn), lambda i, j, k: (k, j)))
    in_specs, out_specs, out_shape, scratch, extra = host.specs(
        [a_spec, b_spec], [pl.BlockSpec((tm, tn), lambda i, j, k: (i, j))],
        [jax.ShapeDtypeStruct((m, n), out_dtype)], [pltpu.VMEM((tm, tn), F32)] if nk > 1 else [])
    res = pl.pallas_call(
        body, name=name, grid=(gm, gn, nk),
        in_specs=in_specs, out_specs=out_specs, out_shape=out_shape, scratch_shapes=scratch,
        compiler_params=_params(("arbitrary", "arbitrary", "arbitrary")),
    )(a, b, *extra)
    return res if comm else res[0]


def mm_blockdiag(a, b, *, name, a_cb0=0, out_dtype=F32, tm=512):
    m = a.shape[0]
    g_n, ka, nb = b.shape
    tm = _pick(m, tm)
    assert a_cb0 % g_n == 0

    def body(a_ref, b_ref, o_ref):
        for g in range(g_n):
            o_ref[:, g * nb:(g + 1) * nb] = jnp.dot(
                a_ref[:, g * ka:(g + 1) * ka].astype(BF16), b_ref[g].astype(BF16),
                preferred_element_type=F32).astype(out_dtype)

    return pl.pallas_call(
        body, name=name, grid=(m // tm,),
        in_specs=[pl.BlockSpec((tm, g_n * ka), lambda i: (i, a_cb0 // g_n)),
                  pl.BlockSpec((g_n, ka, nb), lambda i: (0, 0, 0))],
        out_specs=pl.BlockSpec((tm, g_n * nb), lambda i: (i, 0)),
        out_shape=jax.ShapeDtypeStruct((m, g_n * nb), out_dtype),
        compiler_params=_params(("parallel",)),
    )(a, b)


def mm_blockdiag_tn(a, b, *, name, g_n, ka, kb, a_cb0=0, b_cb0=0, tk=512):
    rows = a.shape[0]
    tk = _pick(rows, tk)
    nk = rows // tk
    assert a_cb0 % g_n == 0 and b_cb0 % g_n == 0

    def body(a_ref, b_ref, o_ref):
        @pl.when(pl.program_id(0) == 0)
        def _():
            o_ref[...] = jnp.zeros_like(o_ref)

        for g in range(g_n):
            o_ref[g] += lax.dot_general(a_ref[:, g * ka:(g + 1) * ka].astype(BF16),
                                        b_ref[:, g * kb:(g + 1) * kb].astype(BF16),
                                        (((0,), (0,)), ((), ())), preferred_element_type=F32)

    return pl.pallas_call(
        body, name=name, grid=(nk,),
        in_specs=[pl.BlockSpec((tk, g_n * ka), lambda k: (k, a_cb0 // g_n)),
                  pl.BlockSpec((tk, g_n * kb), lambda k: (k, b_cb0 // g_n))],
        out_specs=pl.BlockSpec((g_n, ka, kb), lambda k: (0, 0, 0)),
        out_shape=jax.ShapeDtypeStruct((g_n, ka, kb), F32),
        compiler_params=_params(("arbitrary",)),
    )(a, b)


def rowwise(fn, rows, vecs, outs, reds=(), *, name, tm=256):
    metas = []
    for r in rows:
        if isinstance(r, tuple) and len(r) == 3:
            metas.append(("col", r[0], r[1], r[2]))
        elif isinstance(r, tuple):
            metas.append(("lead", r[0], r[0].shape[2], r[1]))
        else:
            metas.append(("full", r, r.shape[1], 0))
    n_rows = metas[0][1].shape[1] if metas[0][0] == "lead" else metas[0][1].shape[0]
    rc = 16 if n_rows % 16 == 0 else (SUBLANES if n_rows % SUBLANES == 0 else n_rows)
    tm = _pick(n_rows, tm, rc)
    n_inner = tm // rc
    windows = [(o[2] if len(o) > 2 else None) for o in outs]
    bases = [(k, o[3]) for k, o in enumerate(outs) if len(o) > 3 and o[3] is not None]
    outs = [(o[0], o[1]) for o in outs]
    nr, nv, no, nbase = len(metas), len(vecs), len(outs), len(bases)

    def body(*refs):
        row_refs, vec_refs = refs[:nr], refs[nr:nr + nv]
        refs = refs[nr + nv + nbase:]
        out_refs, red_refs = refs[:no], refs[no:]
        if reds:
            @pl.when(pl.program_id(0) == 0)
            def _():
                for rr in red_refs:
                    rr[...] = jnp.zeros_like(rr)
        vec_vals = [v[...] for v in vec_refs]

        def step(s, carry):
            r0 = pl.multiple_of(s * rc, rc)
            vals = [ref[pl.ds(r0, rc), :] for ref in row_refs]
            res = fn(*vals, *vec_vals)
            if not isinstance(res, (tuple, list)):
                res = (res,)
            for o_ref, (widths, dt), val in zip(out_refs, outs, res[:no]):
                pieces = val if isinstance(val, (tuple, list)) else (val,)
                off = 0
                for w_, piece in zip(widths, pieces):
                    o_ref[pl.ds(r0, rc), off:off + w_] = piece.astype(dt)
                    off += w_
            for rr, val in zip(red_refs, res[no:]):
                rr[...] += val
            return carry

        lax.fori_loop(0, n_inner, step, 0, unroll=min(n_inner, 4))

    in_specs = []
    for kind, arr, w_, idx in metas:
        if kind == "col":
            in_specs.append(pl.BlockSpec((tm, w_), functools.partial(lambda i, cb: (i, cb), cb=idx)))
        elif kind == "lead":
            in_specs.append(pl.BlockSpec((None, tm, w_), functools.partial(lambda i, p: (p, i, 0), p=idx)))
        else:
            in_specs.append(pl.BlockSpec((tm, w_), lambda i: (i, 0)))
    for v in vecs:
        in_specs.append(pl.BlockSpec(v.shape, lambda i: (0, 0)))
    in_specs += [pl.BlockSpec(memory_space=pl.ANY)] * nbase
    out_specs, out_shape = [], []
    for (ws, dt), win in zip(outs, windows):
        total, cb = win if win is not None else (sum(ws), 0)
        out_specs.append(pl.BlockSpec((tm, sum(ws)), functools.partial(lambda i, cb: (i, cb), cb=cb)))
        out_shape.append(jax.ShapeDtypeStruct((n_rows, total), dt))
    out_specs += [pl.BlockSpec((1, w_), lambda i: (0, 0)) for w_ in reds]
    out_shape += [jax.ShapeDtypeStruct((1, w_), F32) for w_ in reds]
    res = pl.pallas_call(
        body, name=name, grid=(n_rows // tm,),
        in_specs=in_specs, out_specs=out_specs, out_shape=out_shape,
        input_output_aliases={nr + nv + b: k for b, (k, _) in enumerate(bases)},
        compiler_params=_params(("arbitrary",)),
    )(*[m[1] for m in metas], *vecs, *[b for _, b in bases])
    return res


def _rms(x):
    return lax.rsqrt(jnp.mean(x * x, axis=-1, keepdims=True) + RMS_EPS)


def _colsum(x):
    return jnp.sum(x, axis=0, keepdims=True)


def _f_silu(c):
    return c * _sigmoid(c)


def _f_pre(x, a, sh):
    return (x * _rms(x)) * a + sh


def _f_post_add(x, y, bv):
    return x + (y * _rms(y)) * bv


def _f_post_bwd(dxo, y, bv):
    ry = _rms(y)
    yn = y * ry
    dyn = dxo * bv
    dy = ry * (dyn - yn * jnp.mean(dyn * yn, axis=-1, keepdims=True))
    return dy, _colsum(dxo * yn)


def _f_pre_bwd(dxo, dh, x, a):
    r = _rms(x)
    xn = x * r
    dxn = dh * a
    dx = dxo + r * (dxn - xn * jnp.mean(dxn * xn, axis=-1, keepdims=True))
    return dx, _colsum(dh * xn), _colsum(dh)


def _f_swiglu(g, u):
    g = g.astype(F32)
    return (g * _sigmoid(g)) * u.astype(F32)


def _f_swiglu_bwd(g, u, dm):
    g, u, dm = g.astype(F32), u.astype(F32), dm.astype(F32)
    sg = _sigmoid(g)
    dg = dm * u * (sg * (1.0 + g * (1.0 - sg)))
    du = dm * (g * sg)
    return ((dg, du),)


def _gelu_t(x):
    return jnp.tanh(GELU_C * (x + GELU_A * x * x * x))


def _f_gelu_in(y0, u, dvec):
    y = y0 + dvec * u
    return 0.5 * y * (1.0 + _gelu_t(y)), y


def _f_gelu_bwd(dge, y, u):
    t = _gelu_t(y)
    dy = dge * (0.5 * (1.0 + t) + 0.5 * y * (1.0 - t * t) * GELU_C * (1.0 + 3.0 * GELU_A * y * y))
    return dy, _colsum(dy * u)


def _f_du_fin(du0, dys, dvec):
    return du0 + dvec * dys.astype(F32)


def _f_merge(zv, zg, yb, ga, gb):
    zv, zg, yb = zv.astype(F32), zg.astype(F32), yb.astype(F32)
    return _sigmoid(ga) * (zv * _sigmoid(zg)) + _sigmoid(gb) * yb


def _f_merge_bwd(dmg, zv, zg, yb, ga, gb):
    zv, zg, yb = zv.astype(F32), zg.astype(F32), yb.astype(F32)
    sa, sb, sz = _sigmoid(ga), _sigmoid(gb), _sigmoid(zg)
    ya = zv * sz
    dya = dmg * sa
    dga = dmg * ya * sa * (1.0 - sa)
    dyb = dmg * sb
    dgb = dmg * yb * sb * (1.0 - sb)
    dzv = dya * sz
    dzg = dya * zv * sz * (1.0 - sz)
    return (dzv, dzg), dyb, (dga, dgb)


def _f_sum_parts(*parts):
    acc = parts[0].astype(F32)
    for p in parts[1:]:
        acc = acc + p.astype(F32)
    return acc


def _f_adamw(*args):
    parts, (w, m, v) = args[:-3], args[-3:]
    g = _f_sum_parts(*parts)
    m = ADAM_B1 * m + (1.0 - ADAM_B1) * g
    v = ADAM_B2 * v + (1.0 - ADAM_B2) * (g * g)
    m_hat = m / (1.0 - ADAM_B1 ** ADAM_STEP)
    v_hat = v / (1.0 - ADAM_B2 ** ADAM_STEP)
    delta = -ADAM_LR * (m_hat / (jnp.sqrt(v_hat) + ADAM_EPS) + ADAM_WD * w)
    return g, delta, m, v


def adamw(parts3, w, m, v, *, name):
    c = w.shape[1]
    rows = [(parts3, p) for p in range(parts3.shape[0])] + [w, m, v]
    return rowwise(_f_adamw, rows, [], [((c,), F32)] * 4, name=name)


def _tri_dot(tri, x, dims=(((1,), (0,)), ((), ()))):
    x1 = x.astype(BF16)
    r1 = x - x1.astype(F32)
    x2 = r1.astype(BF16)
    x3 = (r1 - x2.astype(F32)).astype(BF16)
    dot = functools.partial(lax.dot_general, dimension_numbers=dims, preferred_element_type=F32)
    return dot(tri, x1) + dot(tri, x2) + dot(tri, x3)


def cum_fwd(proj, fb, *, col_block, name, t=256):
    n = proj.shape[0]
    t = _pick(n, t, SUBLANES)

    def body(f_ref, fb_ref, cum_ref, car_ref):
        @pl.when(pl.program_id(0) == 0)
        def _():
            car_ref[...] = jnp.zeros_like(car_ref)

        x = f_ref[...] + fb_ref[...]
        lf = jnp.minimum(x, 0.0) - jnp.log(1.0 + jnp.exp(-jnp.abs(x)))
        r = lax.broadcasted_iota(jnp.int32, (t, t), 0)
        c = lax.broadcasted_iota(jnp.int32, (t, t), 1)
        cs = _tri_dot((c <= r).astype(BF16), lf) + car_ref[0:1, :]
        cum_ref[...] = cs
        car_ref[0:1, :] = cs[t - 1:t, :]

    return pl.pallas_call(
        body, name=name, grid=(n // t,),
        in_specs=[pl.BlockSpec((t, LANES), lambda i: (i, col_block)),
                  pl.BlockSpec((1, LANES), lambda i: (0, 0))],
        out_specs=pl.BlockSpec((t, LANES), lambda i: (i, 0)),
        out_shape=jax.ShapeDtypeStruct((n, LANES), F32),
        scratch_shapes=[pltpu.VMEM((SUBLANES, LANES), F32)],
        compiler_params=_params(("arbitrary",)),
    )(proj, fb)


def cum_bwd(dcum, proj, fb, *, col_block, name, t=256):
    n = proj.shape[0]
    slabs = dcum.shape[0]
    t = _pick(n, t, SUBLANES)
    nb = n // t

    def body(dc_ref, f_ref, fb_ref, df_ref, dfb_ref, car_ref):
        @pl.when(pl.program_id(0) == 0)
        def _():
            car_ref[...] = jnp.zeros_like(car_ref)
            dfb_ref[...] = jnp.zeros_like(dfb_ref)

        r = lax.broadcasted_iota(jnp.int32, (t, t), 0)
        c = lax.broadcasted_iota(jnp.int32, (t, t), 1)
        dl = _tri_dot((c >= r).astype(BF16), jnp.sum(dc_ref[...], axis=0)) + car_ref[0:1, :]
        car_ref[0:1, :] = dl[0:1, :]
        x = f_ref[...] + fb_ref[...]
        df = dl * (1.0 / (1.0 + jnp.exp(x)))
        df_ref[...] = df
        dfb_ref[...] += _colsum(df)

    return pl.pallas_call(
        body, name=name, grid=(nb,),
        in_specs=[pl.BlockSpec((slabs, t, LANES), lambda i: (0, nb - 1 - i, 0)),
                  pl.BlockSpec((t, LANES), lambda i: (nb - 1 - i, col_block)),
                  pl.BlockSpec((1, LANES), lambda i: (0, 0))],
        out_specs=[pl.BlockSpec((t, LANES), lambda i: (nb - 1 - i, 0)),
                   pl.BlockSpec((1, LANES), lambda i: (0, 0))],
        out_shape=[jax.ShapeDtypeStruct((n, LANES), F32), jax.ShapeDtypeStruct((1, LANES), F32)],
        scratch_shapes=[pltpu.VMEM((SUBLANES, LANES), F32)],
        compiler_params=_params(("arbitrary",)),
    )(dcum, proj, fb)


_NT =(((1,), (1,)), ((), ()))
_TN = (((0,), (0,)), ((), ()))


def _lane_sums_as_row(x):
    return _tri_dot(jnp.ones((SUBLANES, x.shape[1]), BF16), x, _NT)[0:1, :]


def _causal_keep(t):
    return lax.broadcasted_iota(jnp.int32, (t, t), 1) <= lax.broadcasted_iota(jnp.int32, (t, t), 0)


def attn_fwd(q, k, v, crow, *, name, t=512, hb=8, comm=None):
    h_n, n, dh = q.shape
    hb = min(hb, h_n)
    t = _pick(n, t)
    nb = n // t
    ng = h_n // hb
    host = _Hosted(comm, 4, 2, 3)

    def body(*refs):
        (q_ref, k_ref, v_ref, cr_ref, o_ref, lse_ref, m_sc, l_sc, acc_sc), crefs = host.split(refs)
        g, i, j = pl.program_id(0), pl.program_id(1), pl.program_id(2)
        host.phase("start", (g == 0) & (i == 0) & (j == 0), crefs)
        host.phase("mid", (g == ng - 1) & (i == (3 * nb) // 4) & (j == 0), crefs)

        @pl.when(j == 0)
        def _():
            m_sc[...] = jnp.full_like(m_sc, -jnp.inf)
            l_sc[...] = jnp.zeros_like(l_sc)
            acc_sc[...] = jnp.zeros_like(acc_sc)

        def update(diagonal):
            keep = _causal_keep(t) if diagonal else None
            heads = range(hb)
            ss = [lax.dot_general(q_ref[h], k_ref[h], _NT, preferred_element_type=F32) for h in heads]
            pairs, alphas = [], []
            for h in heads:
                s = ss[h] - cr_ref[h]
                if diagonal:
                    s = jnp.where(keep, s, -jnp.inf)
                m_prev = m_sc[h]
                m_new = jnp.maximum(m_prev, jnp.max(s, axis=-1, keepdims=True))
                p = jnp.exp(s - m_new)
                alpha = jnp.exp(m_prev - m_new)
                l_sc[h] = alpha * l_sc[h] + jnp.sum(p, axis=-1, keepdims=True)
                m_sc[h] = m_new
                p_hi = p.astype(BF16)
                pairs.append((p_hi, (p - p_hi.astype(F32)).astype(BF16)))
                alphas.append(alpha)
            for h in heads:
                vv = v_ref[h]
                acc_sc[h] = (alphas[h] * acc_sc[h] + jnp.dot(pairs[h][0], vv, preferred_element_type=F32)
                             + jnp.dot(pairs[h][1], vv, preferred_element_type=F32))

        @pl.when(j < i)
        def _():
            update(False)

        @pl.when(j == i)
        def _():
            update(True)

        @pl.when(j == nb - 1)
        def _():
            o_ref[...] = acc_sc[...] / l_sc[...]
            lane0 = lax.broadcasted_iota(jnp.int32, (t, LANES), 1) == 0
            for h in range(hb):
                lse_col = m_sc[h] + jnp.log(l_sc[h])
                lse_ref[h] = _lane_sums_as_row(jnp.where(lane0, lse_col, 0.0))

        host.phase("finish", (g == ng - 1) & (i == nb - 1) & (j == nb - 1), crefs)

    qspec = pl.BlockSpec((hb, t, dh), lambda g, i, j: (g, i, 0))
    kspec = pl.BlockSpec((hb, t, dh), lambda g, i, j: (g, jnp.minimum(j, i), 0))
    in_specs, out_specs, out_shape, scratch, extra = host.specs(
        [qspec, kspec, kspec, pl.BlockSpec((hb, 1, t), lambda g, i, j: (g, 0, jnp.minimum(j, i)))],
        [qspec, pl.BlockSpec((hb, 1, t), lambda g, i, j: (g, 0, i))],
        [jax.ShapeDtypeStruct((h_n, n, dh), F32), jax.ShapeDtypeStruct((h_n, 1, n), F32)],
        [pltpu.VMEM((hb, t, 1), F32), pltpu.VMEM((hb, t, 1), F32), pltpu.VMEM((hb, t, dh), F32)])
    return pl.pallas_call(
        body, name=name, grid=(ng, nb, nb),
        in_specs=in_specs, out_specs=out_specs, out_shape=out_shape, scratch_shapes=scratch,
        compiler_params=_params(("arbitrary", "arbitrary", "arbitrary")),
    )(q, k, v, crow, *extra)


def attn_delta(do, o, *, name, t=512, hb=8):
    h_n, n, dh = do.shape
    hb = min(hb, h_n)
    t = _pick(n, t)

    def body(do_ref, o_ref, dl_ref):
        for h in range(hb):
            dl_ref[h] = _lane_sums_as_row(do_ref[h].astype(F32) * o_ref[h])

    spec = pl.BlockSpec((hb, t, dh), lambda g, i: (g, i, 0))
    return pl.pallas_call(
        body, name=name, grid=(h_n // hb, n // t),
        in_specs=[spec, spec], out_specs=pl.BlockSpec((hb, 1, t), lambda g, i: (g, 0, i)),
        out_shape=jax.ShapeDtypeStruct((h_n, 1, n), F32),
        compiler_params=_params(("parallel", "parallel")),
    )(do, o)


def attn_bwd(q, k, v, do, o, lse, cum, *, scale, tag, t=512, hb=2, comm=None):
    h_n, n, dh = q.shape
    hb = min(hb, h_n)
    t = _pick(n, t)
    nb = n // t
    dob = do.astype(BF16)
    delta = attn_delta(dob, o, name=f"attn_delta_{tag}", t=t)

    ng = h_n // hb
    host = _Hosted(comm, 7, 4, 3)

    def body(*refs):
        (q_ref, k_ref, v_ref, do_ref, lse_ref, dl_ref, cum_ref,
         dq_ref, dk_ref, dv_ref, dcum_ref, dk_acc, dv_acc, dcc_acc), crefs = host.split(refs)
        g, j, i = pl.program_id(0), pl.program_id(1), pl.program_id(2)
        host.phase("start", (g == 0) & (j == 0) & (i == 0), crefs)
        lane = lax.broadcasted_iota(jnp.int32, (t, LANES), 1)

        @pl.when((j == 0) & (i == 0))
        def _():
            dq_ref[...] = jnp.zeros_like(dq_ref)

        @pl.when(i == 0)
        def _():
            dk_acc[...] = jnp.zeros_like(dk_acc)
            dv_acc[...] = jnp.zeros_like(dv_acc)
            dcc_acc[...] = jnp.zeros_like(dcc_acc)

        def update(diagonal):
            heads = range(hb)
            r0 = pl.multiple_of(i * t, t)
            if diagonal:
                keep = lax.broadcasted_iota(jnp.int32, (t, t), 0) <= lax.broadcasted_iota(jnp.int32, (t, t), 1)
            qv, kv = [q_ref[h] for h in heads], [k_ref[h] for h in heads]
            vv, dov = [v_ref[h] for h in heads], [do_ref[h] for h in heads]
            st = [lax.dot_general(kv[h], qv[h], _NT, preferred_element_type=F32) for h in heads]
            dpt = [lax.dot_general(vv[h], dov[h], _NT, preferred_element_type=F32) for h in heads]
            pt = []
            cum_tile = cum_ref[...]
            for h in heads:
                cc = jnp.sum(jnp.where(lane == g * hb + h, cum_tile, 0.0), axis=1, keepdims=True)
                s = st[h] - cc
                if diagonal:
                    s = jnp.where(keep, s, -jnp.inf)
                pt.append(jnp.exp(s - lse_ref[h]))
            for h in heads:
                dv_acc[h] += jnp.dot(pt[h].astype(BF16), dov[h], preferred_element_type=F32)
            dsb = []
            for h in heads:
                ds = pt[h] * (dpt[h] - dl_ref[h])
                dcc_acc[h] -= jnp.sum(ds, axis=1, keepdims=True)
                dsb.append(ds.astype(BF16))
            for h in heads:
                dk_acc[h] += jnp.dot(dsb[h], qv[h], preferred_element_type=F32)
            for h in heads:
                dq_ref[h, pl.ds(r0, t), :] += lax.dot_general(dsb[h], kv[h], _TN,
                                                              preferred_element_type=F32) * scale

        @pl.when(i > j)
        def _():
            update(False)

        @pl.when(i == j)
        def _():
            update(True)

        @pl.when(i == nb - 1)
        def _():
            dk_ref[...] = dk_acc[...]
            dv_ref[...] = dv_acc[...]
            tile = jnp.zeros((t, LANES), F32)
            for h in range(hb):
                tile = tile + jnp.where(lane == g * hb + h, dcc_acc[h], 0.0)
            dcum_ref[...] = tile

        host.phase("finish", (g == ng - 1) & (j == nb - 1) & (i == nb - 1), crefs)

    qspec = pl.BlockSpec((hb, t, dh), lambda g, j, i: (g, jnp.maximum(i, j), 0))
    qrow = pl.BlockSpec((hb, 1, t), lambda g, j, i: (g, 0, jnp.maximum(i, j)))
    kspec = pl.BlockSpec((hb, t, dh), lambda g, j, i: (g, j, 0))
    in_specs, out_specs, out_shape, scratch, extra = host.specs(
        [qspec, kspec, kspec, qspec, qrow, qrow, pl.BlockSpec((t, LANES), lambda g, j, i: (j, 0))],
        [pl.BlockSpec((hb, n, dh), lambda g, j, i: (g, 0, 0)), kspec, kspec,
         pl.BlockSpec((None, t, LANES), lambda g, j, i: (g, j, 0))],
        [jax.ShapeDtypeStruct((h_n, n, dh), F32)] * 3 + [jax.ShapeDtypeStruct((ng, n, LANES), F32)],
        [pltpu.VMEM((hb, t, dh), F32), pltpu.VMEM((hb, t, dh), F32), pltpu.VMEM((hb, t, 1), F32)])
    dq, dk, dv, dcum, *arrived = pl.pallas_call(
        body, name=f"attn_bwd_{tag}", grid=(ng, nb, nb),
        in_specs=in_specs, out_specs=out_specs, out_shape=out_shape, scratch_shapes=scratch,
        compiler_params=_params(("arbitrary", "arbitrary", "arbitrary")),
    )(q, k, v, dob, lse, delta, cum, *extra)
    return [dq, dk, dv, dcum] + arrived


SCAN_STEPS = (1, 2, 4)


def ssm_scan(x, tab, *, reverse, name, s_prev=None, tt=512):
    n, width = x.shape
    nc, _, hw = tab.shape
    cw = 2 * hw
    assert width == nc * cw
    tt = _pick(n, tt, SUBLANES)
    nt = n // tt
    ng = tt // SUBLANES
    with_grad = s_prev is not None

    def body(*refs):
        if with_grad:
            x_ref, s_ref, tab_ref, o_ref, g_ref, car_ref = refs
        else:
            x_ref, tab_ref, o_ref, car_ref = refs

        @pl.when(pl.program_id(1) == 0)
        def _():
            car_ref[...] = jnp.zeros_like(car_ref)
            if with_grad:
                g_ref[...] = jnp.zeros_like(g_ref)

        q_re, q_im = tab_ref[0:8, :], tab_ref[8:16, :]
        p_re = [tab_ref[16 + i:17 + i, :] for i in range(3)]
        p_im = [tab_ref[24 + i:25 + i, :] for i in range(3)]
        row = lax.broadcasted_iota(jnp.int32, (SUBLANES, hw), 0)

        def group(gi, carry):
            c_re, c_im = carry
            g = (ng - 1 - gi) if reverse else gi
            r0 = pl.multiple_of(g * SUBLANES, SUBLANES)
            xr = x_ref[pl.ds(r0, SUBLANES), 0:hw]
            xi = x_ref[pl.ds(r0, SUBLANES), hw:cw]
            for i, d in enumerate(SCAN_STEPS):
                if reverse:
                    shift, keep = SUBLANES - d, row < SUBLANES - d
                else:
                    shift, keep = d, row >= d
                sr = jnp.where(keep, pltpu.roll(xr, shift, 0), 0.0)
                si = jnp.where(keep, pltpu.roll(xi, shift, 0), 0.0)
                xr, xi = (xr + p_re[i] * sr - p_im[i] * si,
                          xi + p_re[i] * si + p_im[i] * sr)
            xr, xi = (xr + q_re * c_re - q_im * c_im,
                      xi + q_re * c_im + q_im * c_re)
            o_ref[pl.ds(r0, SUBLANES), 0:hw] = xr
            o_ref[pl.ds(r0, SUBLANES), hw:cw] = xi
            if with_grad:
                nr = jnp.where(row < SUBLANES - 1, pltpu.roll(xr, SUBLANES - 1, 0), c_re)
                ni = jnp.where(row < SUBLANES - 1, pltpu.roll(xi, SUBLANES - 1, 0), c_im)
                sr = s_ref[pl.ds(r0, SUBLANES), 0:hw]
                si = s_ref[pl.ds(r0, SUBLANES), hw:cw]
                g_ref[:, 0:hw] += nr * sr + ni * si
                g_ref[:, hw:cw] += ni * sr - nr * si
            if reverse:
                return xr[0:1, :], xi[0:1, :]
            return xr[SUBLANES - 1:SUBLANES, :], xi[SUBLANES - 1:SUBLANES, :]

        c_re, c_im = lax.fori_loop(0, ng, group, (car_ref[0:1, 0:hw], car_ref[0:1, hw:cw]),
                                   unroll=min(ng, 4))
        car_ref[0:1, 0:hw] = c_re
        car_ref[0:1, hw:cw] = c_im

    if reverse:
        xspec = pl.BlockSpec((tt, cw), lambda c, t: (nt - 1 - t, c))
    else:
        xspec = pl.BlockSpec((tt, cw), lambda c, t: (t, c))
    tspec = pl.BlockSpec((None, 32, hw), lambda c, t: (c, 0, 0))
    in_specs = [xspec, xspec, tspec] if with_grad else [xspec, tspec]
    out_specs = [xspec]
    out_shape = [jax.ShapeDtypeStruct((n, width), F32)]
    if with_grad:
        out_specs.append(pl.BlockSpec((None, SUBLANES, cw), lambda c, t: (c, 0, 0)))
        out_shape.append(jax.ShapeDtypeStruct((nc, SUBLANES, cw), F32))
    operands = (x, s_prev, tab) if with_grad else (x, tab)
    return pl.pallas_call(
        body, name=name, grid=(nc, nt),
        in_specs=in_specs, out_specs=out_specs, out_shape=out_shape,
        scratch_shapes=[pltpu.VMEM((SUBLANES, cw), F32)],
        compiler_params=_params(("parallel", "arbitrary")),
    )(*operands)


def _slot(pos):
    return 4 * pos[0] + 2 * pos[1] + pos[2]


def _comm_scratch(n):
    return [pltpu.SemaphoreType.DMA((7 * n,)), pltpu.SemaphoreType.DMA((7 * n,)), pltpu.SemaphoreType.DMA((n,))]


def _gather_copies(ins, outs, sems):
    send_sems, recv_sems, local_sems = sems
    n = len(ins)
    x, y, c = lax.axis_index("x"), lax.axis_index("y"), lax.axis_index("c")
    me, sibling = (x, y, c), (x, y, 1 - c)
    chips = [(1 - x, y), (x, 1 - y), (1 - x, 1 - y)]

    def copy(t, k, block, to, src=None):
        dst = outs[t].at[_slot(block)]
        return pltpu.make_async_remote_copy(
            src_ref=dst if src is None else src, dst_ref=dst,
            send_sem=send_sems.at[7 * t + k], recv_sem=recv_sems.at[7 * t + k],
            device_id=to, device_id_type=MESH)

    jc = list(enumerate(chips))
    return dict(
        mine=[pltpu.make_async_copy(ins[t], outs[t].at[_slot(me)], local_sems.at[t]) for t in range(n)],
        first=[cp for t in range(n) for cp in
               [copy(t, 0, me, sibling, src=ins[t])] + [copy(t, 1 + j, me, (*chip, c), src=ins[t]) for j, chip in jc]],
        arrive=[copy(t, 1 + j, (*chip, c), me) for t in range(n) for j, chip in jc],
        passed=[copy(t, 4 + j, (*chip, c), sibling) for t in range(n) for j, chip in jc],
        from_sibling=[cp for t in range(n) for cp in
                      [copy(t, 0, sibling, me)] + [copy(t, 4 + j, (*chip, 1 - c), me) for j, chip in jc]])


def _gather_start(ins, outs, sems):
    cps = _gather_copies(ins, outs, sems)
    for cp in cps["mine"] + cps["first"]:
        cp.start()


def _gather_forward(ins, outs, sems):
    cps = _gather_copies(ins, outs, sems)
    for arrived, onward in zip(cps["arrive"], cps["passed"]):
        arrived.wait_recv()
        onward.start()


def _gather_finish(ins, outs, sems):
    cps = _gather_copies(ins, outs, sems)
    for cp in cps["from_sibling"]:
        cp.wait_recv()
    for cp in cps["first"] + cps["passed"]:
        cp.wait_send()
    for cp in cps["mine"]:
        cp.wait()


def gather_comm(arrs):
    return dict(ins=list(arrs), out_shape=[jax.ShapeDtypeStruct((N_DEV,) + a.shape, a.dtype) for a in arrs],
                scratch=_comm_scratch(len(arrs)), start=_gather_start, mid=_gather_forward, finish=_gather_finish)


def _exchange_copies(ins, outs, sems):
    send_sems, recv_sems, local_sems = sems
    n = len(ins)
    me = (lax.axis_index("x"), lax.axis_index("y"), lax.axis_index("c"))
    peers = []
    for k in range(1, N_DEV):
        flip = ((k >> 2) & 1, (k >> 1) & 1, k & 1)
        peers.append(tuple(1 - p if f else p for p, f in zip(me, flip)))

    def copy(t, k, peer, dst_slot):
        return pltpu.make_async_remote_copy(
            src_ref=ins[t].at[_slot(peer)], dst_ref=outs[t].at[dst_slot],
            send_sem=send_sems.at[7 * t + k], recv_sem=recv_sems.at[7 * t + k],
            device_id=peer, device_id_type=MESH)

    return dict(
        mine=[pltpu.make_async_copy(ins[t].at[_slot(me)], outs[t].at[_slot(me)], local_sems.at[t])
              for t in range(n)],
        send=[copy(t, k, peer, _slot(me)) for t in range(n) for k, peer in enumerate(peers)],
        both=[copy(t, k, peer, _slot(peer)) for t in range(n) for k, peer in enumerate(peers)])


def _exchange_start(ins, outs, sems):
    cps = _exchange_copies(ins, outs, sems)
    for cp in cps["mine"] + cps["send"]:
        cp.start()


def _exchange_finish(ins, outs, sems):
    cps = _exchange_copies(ins, outs, sems)
    for cp in cps["both"]:
        cp.wait()
    for cp in cps["mine"]:
        cp.wait()


def exchange_comm(arrs):
    return dict(ins=list(arrs), out_shape=[jax.ShapeDtypeStruct(a.shape, a.dtype) for a in arrs],
                scratch=_comm_scratch(len(arrs)), start=_exchange_start, mid=None, finish=_exchange_finish)


def run_comm(comm, *, name):
    n_in, n_out = len(comm["ins"]), len(comm["out_shape"])

    def body(*refs):
        ins, outs, sems = refs[:n_in], refs[n_in:n_in + n_out], refs[n_in + n_out:]
        comm["start"](ins, outs, sems)
        if comm["mid"] is not None:
            comm["mid"](ins, outs, sems)
        comm["finish"](ins, outs, sems)

    any_spec = pl.BlockSpec(memory_space=pl.ANY)
    return pl.pallas_call(
        body, name=name, in_specs=[any_spec] * n_in, out_specs=[any_spec] * n_out,
        out_shape=comm["out_shape"], scratch_shapes=comm["scratch"],
    )(*comm["ins"])


class _Hosted:
    def __init__(self, comm, n_in, n_out, n_scratch):
        self.comm = comm
        self.n_ci = len(comm["ins"]) if comm else 0
        self.n_co = len(comm["out_shape"]) if comm else 0
        self.n_in, self.n_out, self.n_scratch = n_in, n_out, n_scratch

    def split(self, refs):
        a = self.n_in
        b = a + self.n_ci
        c = b + self.n_out
        e = c + self.n_co
        f = e + self.n_scratch
        return refs[:a] + refs[b:c] + refs[e:f], (refs[a:b], refs[c:e], refs[f:])

    def phase(self, which, when, crefs):
        fn = self.comm[which] if self.comm else None
        if fn is not None:
            pl.when(when)(lambda: fn(*crefs))

    def specs(self, in_specs, out_specs, out_shape, scratch):
        any_spec = pl.BlockSpec(memory_space=pl.ANY)
        if not self.comm:
            return in_specs, out_specs, out_shape, scratch, ()
        return (in_specs + [any_spec] * self.n_ci, out_specs + [any_spec] * self.n_co,
                out_shape + self.comm["out_shape"], scratch + self.comm["scratch"], tuple(self.comm["ins"]))


def all_gather(arrs, *, name):
    return run_comm(gather_comm(arrs), name=name)


def _discretise(a_re, a_im, log_dt, b_re, b_im):
    ar = jnp.minimum(a_re, -1e-4)
    dt = jnp.exp(log_dt)[:, None]
    e, ph = ar * dt, a_im * dt
    mag = jnp.exp(e)
    lr, li = mag * jnp.cos(ph), mag * jnp.sin(ph)
    den = ar * ar + a_im * a_im
    nr, ni = lr - 1.0, li
    cr = (nr * ar + ni * a_im) / den
    ci = (ni * ar - nr * a_im) / den
    bb_re = cr[..., None] * b_re - ci[..., None] * b_im
    bb_im = cr[..., None] * b_im + ci[..., None] * b_re
    return e, ph, bb_re, bb_im


def _lam_pow(e, ph, k, conj):
    mag = jnp.exp(k * e)
    return mag * jnp.cos(k * ph), (-1.0 if conj else 1.0) * mag * jnp.sin(k * ph)


def _scan_table(e, ph, nc, reverse):
    hw = e.size // nc
    e, ph = e.reshape(nc, 1, hw), ph.reshape(nc, 1, hw)
    j = jnp.arange(SUBLANES, dtype=F32).reshape(1, SUBLANES, 1)
    kq = (SUBLANES - j) if reverse else (j + 1.0)
    q_re, q_im = _lam_pow(e, ph, kq, reverse)
    kp = jnp.array(SCAN_STEPS + (0,) * 5, F32).reshape(1, SUBLANES, 1)
    p_re, p_im = _lam_pow(e, ph, kp, reverse)
    return jnp.concatenate([q_re, q_im, p_re, p_im], axis=1)


def _blockdiag(m, nc):
    g, a, b = m.shape
    gc = g // nc
    m = m.reshape(nc, gc, a, b)
    eye = jnp.eye(gc, dtype=m.dtype)
    return jnp.einsum("cgab,gh->cgahb", m, eye).reshape(nc, gc * a, gc * b)


def _blockdiag_take(m, g):
    nc = m.shape[0]
    gc = g // nc
    a, b = m.shape[1] // gc, m.shape[2] // gc
    m = m.reshape(nc, gc, a, gc, b)
    eye = jnp.eye(gc, dtype=m.dtype)
    return jnp.einsum("cgahb,gh->cgab", m, eye).reshape(g, a, b)


def kernel(x, c, mod_w, mod_b, norm_pre, norm_post, ffn_w_in, ffn_w_out, mix_w_in, forget_b, ssm_a_re, ssm_a_im, ssm_log_dt, ssm_b_re, ssm_b_im, ssm_c_re, ssm_c_im, ssm_d, glu_w, attn_w_out, mix_w_out, loss_target, m_mod_w, m_mod_b, m_norm_pre, m_norm_post, m_ffn_w_in, m_ffn_w_out, m_mix_w_in, m_forget_b, m_ssm_a_re, m_ssm_a_im, m_ssm_log_dt, m_ssm_b_re, m_ssm_b_im, m_ssm_c_re, m_ssm_c_im, m_ssm_d, m_glu_w, m_attn_w_out, m_mix_w_out, v_mod_w, v_mod_b, v_norm_pre, v_norm_post, v_ffn_w_in, v_ffn_w_out, v_mix_w_in, v_forget_b, v_ssm_a_re, v_ssm_a_im, v_ssm_log_dt, v_ssm_b_re, v_ssm_b_im, v_ssm_c_re, v_ssm_c_im, v_ssm_d, v_glu_w, v_attn_w_out, v_mix_w_out):
    names = ["mod_w", "mod_b", "norm_pre", "norm_post", "ffn_w_in", "ffn_w_out", "mix_w_in", "forget_b",
             "ssm_a_re", "ssm_a_im", "ssm_log_dt", "ssm_b_re", "ssm_b_im", "ssm_c_re", "ssm_c_im", "ssm_d",
             "glu_w", "attn_w_out", "mix_w_out"]
    w_in = dict(zip(names, [mod_w, mod_b, norm_pre, norm_post, ffn_w_in, ffn_w_out, mix_w_in, forget_b,
                            ssm_a_re, ssm_a_im, ssm_log_dt, ssm_b_re, ssm_b_im, ssm_c_re, ssm_c_im, ssm_d,
                            glu_w, attn_w_out, mix_w_out]))
    m_in = dict(zip(names, [m_mod_w, m_mod_b, m_norm_pre, m_norm_post, m_ffn_w_in, m_ffn_w_out, m_mix_w_in,
                            m_forget_b, m_ssm_a_re, m_ssm_a_im, m_ssm_log_dt, m_ssm_b_re, m_ssm_b_im,
                            m_ssm_c_re, m_ssm_c_im, m_ssm_d, m_glu_w, m_attn_w_out, m_mix_w_out]))
    v_in = dict(zip(names, [v_mod_w, v_mod_b, v_norm_pre, v_norm_post, v_ffn_w_in, v_ffn_w_out, v_mix_w_in,
                            v_forget_b, v_ssm_a_re, v_ssm_a_im, v_ssm_log_dt, v_ssm_b_re, v_ssm_b_im,
                            v_ssm_c_re, v_ssm_c_im, v_ssm_d, v_glu_w, v_attn_w_out, v_mix_w_out]))

    depth = mod_w.shape[0]
    n_tok, d = x.shape[1], x.shape[2]
    ff = ffn_w_out.shape[2] * N_DEV
    heads = forget_b.shape[1]
    sw = ssm_d.shape[1]
    g_n, p_n, n_n = ssm_b_re.shape[1:]
    aw = attn_w_out.shape[1]
    dh = aw // heads
    iw = mix_w_in.shape[2] * N_DEV
    nc = sw // LANES
    hw = g_n * p_n // nc
    mod_cols = mod_w.shape[2]
    scale = dh ** -0.5
    assert iw == sw + 3 * aw + heads + 2 * d and heads <= LANES
    assert math.log2(scale).is_integer(), "q is pre-scaled in bf16: exact only for a power of two"
    off_u, off_q, off_f = 2 * d, 2 * d + sw, 2 * d + sw + 3 * aw
    iwp = off_f + LANES
    assert off_u % sw == 0 and off_q % aw == 0 and off_f % LANES == 0

    me = 4 * lax.axis_index("x") + 2 * lax.axis_index("y") + lax.axis_index("c")
    x2 = x.reshape(n_tok, d)
    tgt = loss_target.reshape(n_tok, d)

    silu_c = rowwise(_f_silu, [c], [], [((d,), F32)], name="silu_c")[0]
    big = ["ffn_w_in", "ffn_w_out", "mix_w_in", "glu_w", "attn_w_out", "mix_w_out"]
    ffn1 = [("ffn_w_in", 0), ("ffn_w_out", 0)]
    rest = [("ffn_w_in", 1), ("ffn_w_out", 1), ("mix_w_in", None), ("glu_w", None), ("attn_w_out", None),
            ("mix_w_out", None)]

    def piece_of(dct, piece, l):
        name, j = piece
        return dct[name][l] if j is None else dct[name][l][j]

    def shards(l, pieces):
        return [piece_of(w_in, p, l).astype(BF16) for p in pieces]

    cut = [0, sw, sw + aw, sw + 2 * aw, sw + 3 * aw, sw + 3 * aw + heads, sw + 3 * aw + heads + d, iw]
    lw = [dict(win=[None, None], wout=[None, None]) for _ in range(depth)]

    def install(l, pieces, gathered):
        for (name, j), g in zip(pieces, gathered):
            if name == "ffn_w_in":
                lw[l]["win"][j] = jnp.moveaxis(g, 0, 1).reshape(d, 2 * ff)
            elif name == "ffn_w_out":
                lw[l]["wout"][j] = g.reshape(ff, d)
            elif name == "mix_w_in":
                w_mix_in = jnp.moveaxis(g, 0, 1).reshape(d, iw)
                seg = lambda i: w_mix_in[:, cut[i]:cut[i + 1]]
                lw[l]["wmi"] = jnp.concatenate([seg(5), seg(6), seg(0), seg(1), seg(2), seg(3),
                                                jnp.pad(seg(4), ((0, 0), (0, LANES - heads)))], axis=-1)
            elif name == "mix_w_out":
                lw[l]["mo"] = g.reshape(d, d)
            else:
                lw[l]["glu" if name == "glu_w" else "ao"] = jnp.moveaxis(g, 0, 1).reshape(g.shape[1], -1)

    gathered = all_gather(
        [silu_c, norm_pre.reshape(-1, norm_pre.shape[-1]), norm_post.reshape(-1, norm_post.shape[-1])]
        + shards(0, ffn1), name="gather_first")
    sc_all = gathered[0].reshape(N_DEV, d)
    gpre = jnp.moveaxis(gathered[1].reshape(N_DEV, depth, 3, -1), 0, 2).reshape(depth, 3, d)
    gpost = jnp.moveaxis(gathered[2].reshape(N_DEV, depth, 3, -1), 0, 2).reshape(depth, 3, d)
    install(0, ffn1, gathered[3:])

    sc_pad = jnp.pad(sc_all, ((0, LANES - N_DEV), (0, 0)))
    mod_part = jnp.stack([mm(sc_pad, mod_w[l], name=f"mod_fwd{l}")[:N_DEV] for l in range(depth)], axis=1)
    mod_part = mod_part + lax.dynamic_slice_in_dim(mod_b, me * mod_cols, mod_cols, axis=1)[None]
    mod_all = all_gather([mod_part], name="gather_mod")[0]
    mod_own = lax.dynamic_index_in_dim(mod_all, me, axis=1, keepdims=False)
    mod_own = mod_own.transpose(1, 0, 2).reshape(depth, 3, 3, d)
    res_w = (FFN_RES, 1.0, FFN_RES)

    def vec_a(l, i):
        return (gpre[l, i] * (1.0 + mod_own[l, i, 1])).reshape(1, d)

    def vec_sh(l, i):
        return mod_own[l, i, 0].reshape(1, d)

    def vec_b(l, i):
        return (res_w[i] * mod_own[l, i, 2] * gpost[l, i]).reshape(1, d)

    ssm = []
    for l in range(depth):
        (e, ph, bb_re, bb_im), disc_vjp = jax.vjp(_discretise,ssm_a_re[l], ssm_a_im[l], ssm_log_dt[l],
                                                  ssm_b_re[l], ssm_b_im[l])
        b_mat = jnp.concatenate([_blockdiag(bb_re.transpose(0, 2, 1), nc),
                                 _blockdiag(bb_im.transpose(0, 2, 1), nc)], axis=2)
        c_mat = jnp.concatenate([_blockdiag(ssm_c_re[l].transpose(0, 2, 1), nc),
                                 _blockdiag(-ssm_c_im[l].transpose(0, 2, 1), nc)], axis=1)
        ssm.append(dict(e=e, ph=ph, vjp=disc_vjp, b=b_mat.astype(BF16), c=c_mat.astype(BF16),
                        bt=b_mat.transpose(0, 2, 1).astype(BF16), ct=c_mat.transpose(0, 2, 1).astype(BF16),
                        tab_f=_scan_table(e, ph, nc, False), tab_r=_scan_table(e, ph, nc, True),
                        dvec=ssm_d[l].reshape(1, sw)))

    fb_pad = jnp.pad(forget_b, ((0, 0), (0, LANES - heads)))

    def heads_first(a):
        return a.reshape(n_tok, heads, dh).transpose(1, 0, 2)

    def heads_last(a):
        return a.transpose(1, 0, 2).reshape(n_tok, heads * dh)

    def mm_hosting(a, b, comm, **kw):
        if comm is None:
            return mm(a, b, **kw), []
        out, *arrived = mm(a, b, comm=comm, **kw)
        return out, arrived

    def ffn_fwd(xin, l, i, j, tag, comm=None):
        h = rowwise(_f_pre, [xin], [vec_a(l, i), vec_sh(l, i)], [((d,), BF16)], name=f"pre_{tag}")[0]
        a, arrived = mm_hosting(h, lw[l]["win"][j], comm, name=f"ffn_in_{tag}", out_dtype=BF16)
        m = rowwise(_f_swiglu, [(a, ff, 0), (a, ff, 1)], [], [((ff,), BF16)], name=f"swiglu_{tag}")[0]
        y = mm(m, lw[l]["wout"][j], name=f"ffn_out_{tag}")
        xout = rowwise(_f_post_add, [xin, y], [vec_b(l, i)], [((d,), F32)], name=f"post_{tag}")[0]
        return xout, dict(x=xin, h=h, a=a, m=m, y=y), arrived

    def mixer_fwd(xin, l, tag):
        s5 = ssm[l]
        h = rowwise(_f_pre, [xin], [vec_a(l, 1), vec_sh(l, 1)], [((d,), BF16)], name=f"pre_{tag}")[0]
        proj = mm(h, lw[l]["wmi"], name=f"mix_in_{tag}")
        bu = mm_blockdiag(proj, s5["b"], a_cb0=off_u // LANES, name=f"ssm_bu_{tag}")
        st = ssm_scan(bu, s5["tab_f"], reverse=False, name=f"ssm_scan_{tag}")[0]
        y0 = mm_blockdiag(st, s5["c"], name=f"ssm_y_{tag}")
        ge, ys = rowwise(_f_gelu_in, [y0, (proj, sw, off_u // sw)], [s5["dvec"]],
                         [((sw,), BF16), ((sw,), F32)], name=f"gelu_{tag}")
        z = mm(ge, lw[l]["glu"], name=f"glu_{tag}", out_dtype=BF16)
        cum = cum_fwd(proj, fb_pad[l:l + 1], col_block=off_f // LANES, name=f"cum_{tag}")
        crow = cum[:, :heads].T[:, None, :]
        q, k, v = [heads_first(proj[:, off_q + i * aw:off_q + (i + 1) * aw] * sc_).astype(BF16)
                   for i, sc_ in enumerate((scale, 1.0, 1.0))]
        nxt = gather_comm(shards(l + 1, ffn1 + rest)) if l + 1 < depth else None
        o, lse, *arrived = attn_fwd(q, k, v, crow, name=f"attn_{tag}", comm=nxt)
        if nxt is not None:
            install(l + 1, ffn1 + rest, arrived)
        attn = heads_last(o).astype(BF16)
        yb = mm(attn, lw[l]["ao"], name=f"attn_out_{tag}", out_dtype=BF16)
        mg = rowwise(_f_merge, [(z, d, 0), (z, d, 1), yb, (proj, d, 0), (proj, d, 1)], [],
                     [((d,), BF16)], name=f"merge_{tag}")[0]
        y = mm(mg, lw[l]["mo"], name=f"mix_out_{tag}")
        xout = rowwise(_f_post_add, [xin, y], [vec_b(l, 1)], [((d,), F32)], name=f"post_{tag}")[0]
        saved = dict(x=xin, h=h, proj=proj, st=st, ys=ys, ge=ge, z=z, q=q, k=k, v=v, o=o, lse=lse,
                     cum=cum, attn=attn, yb=yb, mg=mg, y=y)
        return xout, saved

    saved = []
    xc = x2
    for l in range(depth):
        xc, s0, arrived = ffn_fwd(xc, l, 0, 0, f"l{l}a", gather_comm(shards(0, rest)) if l == 0 else None)
        if l == 0:
            install(0, rest, arrived)
        xc, s1 = mixer_fwd(xc, l, f"l{l}m")
        xc, s2, _ = ffn_fwd(xc, l, 2, 1, f"l{l}b")
        saved.append((s0, s1, s2))

    def f_loss(xf, t):
        e_ = xf - t
        return e_ * (1.0 / d), _colsum(e_ * e_)

    dx, sq = rowwise(f_loss, [xc, tgt], [], [((d,), F32)], [d], name="loss_head")
    loss_part = 0.5 * jnp.sum(sq) / d

    grads = {k: [None] * depth for k in big}
    small_g = [dict() for _ in range(depth)]
    dmod = [[None] * 3 for _ in range(depth)]
    dgpre = [[None] * 3 for _ in range(depth)]
    dgpost = [[None] * 3 for _ in range(depth)]

    def norm_grads(l, i, d_a, d_sh, d_bv):
        d_a, d_sh, d_bv = d_a.reshape(d), d_sh.reshape(d), d_bv.reshape(d)
        dmod[l][i] = jnp.stack([d_sh, d_a * gpre[l, i], res_w[i] * gpost[l, i] * d_bv])
        dgpre[l][i] = d_a * (1.0 + mod_own[l, i, 1])
        dgpost[l][i] = res_w[i] * mod_own[l, i, 2] * d_bv

    def ffn_bwd(dxo, sv, l, i, j, tag, comm_dw=None, comm_dx_of=None):
        dy, d_bv = rowwise(_f_post_bwd, [dxo, sv["y"]], [vec_b(l, i)], [((d,), BF16)], [d],
                           name=f"post_bwd_{tag}")
        dm = mm(dy, lw[l]["wout"][j], trans_b=True, name=f"ffn_out_dx_{tag}", out_dtype=BF16)
        g_out = mm(sv["m"], dy, trans_a=True, name=f"ffn_out_dw_{tag}", out_dtype=BF16, tm=1408, tn=1024)
        da = rowwise(_f_swiglu_bwd, [(sv["a"], ff, 0), (sv["a"], ff, 1), dm], [], [((ff, ff), BF16)],
                     name=f"swiglu_bwd_{tag}")[0]
        g_in, arrived_dw = mm_hosting(sv["h"], da, comm_dw, trans_a=True, name=f"ffn_in_dw_{tag}",
                                      out_dtype=BF16)
        comm_dx = comm_dx_of(g_in, g_out) if comm_dx_of is not None else None
        dh_, arrived_dx = mm_hosting(da, lw[l]["win"][j], comm_dx, trans_b=True, name=f"ffn_in_dx_{tag}")
        dxn, d_a, d_sh = rowwise(_f_pre_bwd, [dxo, dh_, sv["x"]], [vec_a(l, i)], [((d,), F32)], [d, d],
                                 name=f"pre_bwd_{tag}")
        norm_grads(l, i, d_a, d_sh, d_bv)
        return dxn, g_in, g_out, arrived_dw, arrived_dx

    def mixer_bwd(dxo, sv, l, tag, comm):
        s5 = ssm[l]
        proj = sv["proj"]
        dy, d_bv = rowwise(_f_post_bwd, [dxo, sv["y"]], [vec_b(l, 1)], [((d,), BF16)], [d],
                           name=f"post_bwd_{tag}")
        dmg = mm(dy, lw[l]["mo"], trans_b=True, name=f"mix_out_dx_{tag}", out_dtype=BF16)
        g_mo = mm(sv["mg"], dy, trans_a=True, name=f"mix_out_dw_{tag}", out_dtype=BF16)
        dz, dyb, dproj = rowwise(
            _f_merge_bwd, [dmg, (sv["z"], d, 0), (sv["z"], d, 1), sv["yb"], (proj, d, 0), (proj, d, 1)], [],
            [((d, d), BF16), ((d,), BF16), ((d, d), BF16, (iwp, 0))], name=f"merge_bwd_{tag}")
        dge = mm(dz, lw[l]["glu"], trans_b=True, name=f"glu_dx_{tag}")
        g_glu = mm(sv["ge"], dz, trans_a=True, name=f"glu_dw_{tag}", out_dtype=BF16)
        dys, d_dvec = rowwise(_f_gelu_bwd, [dge, sv["ys"], (proj, sw, off_u // sw)], [], [((sw,), BF16)],
                              [sw], name=f"gelu_bwd_{tag}")
        gadj = mm_blockdiag(dys, s5["ct"], name=f"ssm_dy_{tag}")
        adj, dlam8 = ssm_scan(gadj, s5["tab_r"], reverse=True, s_prev=sv["st"], name=f"ssm_scan_bwd_{tag}")
        du0 = mm_blockdiag(adj, s5["bt"], name=f"ssm_du_{tag}")
        d_bmat = mm_blockdiag_tn(proj, adj, g_n=nc, ka=LANES, kb=2 * hw, a_cb0=off_u // LANES,
                                 name=f"ssm_db_{tag}")
        d_cmat = mm_blockdiag_tn(sv["st"], dys, g_n=nc, ka=2 * hw, kb=LANES, name=f"ssm_dc_{tag}")
        dproj = rowwise(_f_du_fin, [du0, dys], [s5["dvec"]], [((sw,), BF16, (iwp, off_u // sw), dproj)],
                        name=f"ssm_du_fin_{tag}")[0]
        dlam = jnp.sum(dlam8, axis=1)
        dlam_re, dlam_im = dlam[:, :hw].reshape(g_n, p_n), dlam[:, hw:].reshape(g_n, p_n)
        dbb_re = _blockdiag_take(d_bmat[:, :, :hw], g_n).transpose(0, 2, 1)
        dbb_im = _blockdiag_take(d_bmat[:, :, hw:], g_n).transpose(0, 2, 1)
        mag = jnp.exp(s5["e"])
        lr, li = mag * jnp.cos(s5["ph"]), mag * jnp.sin(s5["ph"])
        d_e = dlam_re * lr + dlam_im * li
        d_ph = -dlam_re * li + dlam_im * lr
        da_re, da_im, dlog_dt, db_re, db_im = s5["vjp"]((d_e, d_ph, dbb_re, dbb_im))
        small_g[l].update(
            ssm_a_re=da_re, ssm_a_im=da_im, ssm_log_dt=dlog_dt, ssm_b_re=db_re, ssm_b_im=db_im,
            ssm_c_re=_blockdiag_take(d_cmat[:, :hw, :], g_n).transpose(0, 2, 1),
            ssm_c_im=-_blockdiag_take(d_cmat[:, hw:, :], g_n).transpose(0, 2, 1),
            ssm_d=d_dvec.reshape(sw))
        dattn = mm(dyb, lw[l]["ao"], trans_b=True, name=f"attn_out_dx_{tag}", out_dtype=BF16)
        g_ao = mm(sv["attn"], dyb, trans_a=True, name=f"attn_out_dw_{tag}", out_dtype=BF16)
        dq, dk, dv, dcum, *arrived = attn_bwd(sv["q"], sv["k"], sv["v"], heads_first(dattn), sv["o"], sv["lse"],
                                              sv["cum"], scale=scale, tag=tag, comm=comm)
        df, dfb = cum_bwd(dcum, proj, fb_pad[l:l + 1], col_block=off_f // LANES, name=f"cum_bwd_{tag}")
        small_g[l]["forget_b"] = dfb[0, :heads]
        for piece, off in ((heads_last(dq), off_q), (heads_last(dk), off_q + aw), (heads_last(dv), off_q + 2 * aw),
                           (df, off_f)):
            dproj = lax.dynamic_update_slice(dproj, piece.astype(BF16), (0, off))
        g_mi = mm(sv["h"], dproj, trans_a=True, name=f"mix_in_dw_{tag}", out_dtype=BF16)
        dh_ = mm(dproj, lw[l]["wmi"], trans_b=True, name=f"mix_in_dx_{tag}")
        dxn, d_a, d_sh = rowwise(_f_pre_bwd, [dxo, dh_, sv["x"]], [vec_a(l, 1)], [((d,), F32)], [d, d],
                                 name=f"pre_bwd_{tag}")
        norm_grads(l, 1, d_a, d_sh, d_bv)
        g_mi = jnp.concatenate([g_mi[:, off_u:off_f + heads], g_mi[:, :off_u]], axis=1)
        return dxn, g_mi, g_glu, g_ao, g_mo, arrived

    def split_last(a):
        return jnp.moveaxis(a.reshape(a.shape[:-1] + (N_DEV, a.shape[-1] // N_DEV)), -2, 0)

    def split_rows(a):
        return jnp.moveaxis(a.reshape(a.shape[:-2] + (N_DEV, a.shape[-2] // N_DEV, a.shape[-1])), -3, 0)

    def owner_blocks(l, pieces):
        out = []
        for name, j in pieces:
            if name == "ffn_w_in":
                out.append(split_last(g_ffn_in[l][j]))
            elif name == "ffn_w_out":
                out.append(split_rows(g_ffn_out[l][j]))
            elif name == "mix_w_out":
                out.append(split_rows(grads[name][l]))
            else:
                out.append(split_last(grads[name][l]))
        return out

    g_ffn_in = [[None, None] for _ in range(depth)]
    g_ffn_out = [[None, None] for _ in range(depth)]
    parts = {}

    def record(l, pieces, arrived):
        for p, a in zip(pieces, arrived):
            parts[(p, l)] = a

    def last_ffn_blocks(g_in, g_out):
        return exchange_comm([split_last(g_in), split_rows(g_out)])

    for l in reversed(range(depth)):
        s0, s1, s2 = saved[l]
        dx, g_ffn_in[l][1], g_ffn_out[l][1], _, _ = ffn_bwd(dx, s2, l, 2, 1, f"l{l}b")
        pending = exchange_comm(owner_blocks(l + 1, ffn1 + rest)) if l + 1 < depth else None
        (dx, grads["mix_w_in"][l], grads["glu_w"][l], grads["attn_w_out"][l], grads["mix_w_out"][l],
         arrived) = mixer_bwd(dx, s1, l, f"l{l}m", pending)
        if pending is not None:
            record(l + 1, ffn1 + rest, arrived)
        if l == 0:
            dx, g_ffn_in[l][0], g_ffn_out[l][0], arrived_rest, arrived_ffn1 = ffn_bwd(
                dx, s0, l, 0, 0, f"l{l}a", exchange_comm(owner_blocks(0, rest)), last_ffn_blocks)
            record(0, rest, arrived_rest)
            record(0, ffn1, arrived_ffn1)
        else:
            dx, g_ffn_in[l][0], g_ffn_out[l][0], _, _ = ffn_bwd(dx, s0, l, 0, 0, f"l{l}a")
    grad_x = dx.reshape(x.shape)

    small_names = ["forget_b", "ssm_a_re", "ssm_a_im", "ssm_log_dt", "ssm_b_re", "ssm_b_im", "ssm_c_re",
                   "ssm_c_im", "ssm_d"]
    pieces = [loss_part.reshape(1), jnp.stack([jnp.stack(dmod[l]) for l in range(depth)]).reshape(-1),
              jnp.stack([jnp.stack(dgpre[l]) for l in range(depth)]).reshape(-1),
              jnp.stack([jnp.stack(dgpost[l]) for l in range(depth)]).reshape(-1)]
    pieces += [jnp.stack([small_g[l][k] for l in range(depth)]).reshape(-1) for k in small_names]
    sizes = [p.size for p in pieces]
    chunk = SUBLANES * 1024
    total = -(-sum(sizes) // chunk) * chunk
    pack = jnp.pad(jnp.concatenate(pieces), (0, total - sum(sizes))).reshape(total // 1024, 1024)
    pack_all = all_gather([pack], name="gather_small_grads")[0]
    pack_sum = rowwise(_f_sum_parts, [(pack_all, p) for p in range(N_DEV)], [], [((1024,), F32)],
                       name="sum_small_grads")[0].reshape(-1)
    offs = [0]
    for s_ in sizes:
        offs.append(offs[-1] + s_)
    take = lambda i: pack_sum[offs[i]:offs[i + 1]]
    loss = take(0).reshape(())
    g_small = {"mod_b": take(1).reshape(mod_b.shape)}
    g_pre_full, g_post_full = take(2).reshape(depth, 3, d), take(3).reshape(depth, 3, d)
    shard = norm_pre.shape[-1]
    g_small["norm_pre"] = lax.dynamic_slice_in_dim(g_pre_full, me * shard, shard, axis=2)
    g_small["norm_post"] = lax.dynamic_slice_in_dim(g_post_full, me * shard, shard, axis=2)
    for i, k in enumerate(small_names):
        g_small[k] = take(4 + i).reshape(w_in[k].shape)

    dmod_all = pack_all.reshape(N_DEV, -1)[:, offs[1]:offs[2]].reshape(N_DEV, depth, 9 * d)
    dmod_mine = lax.dynamic_slice_in_dim(dmod_all, me * mod_cols, mod_cols, axis=2)
    sct_pad = jnp.pad(sc_all.T, ((0, 0), (0, LANES - N_DEV)))
    g_mod_w = jnp.stack([
        mm(sct_pad, jnp.pad(dmod_mine[:, l], ((0, LANES - N_DEV), (0, 0))), name=f"mod_dw{l}")
        for l in range(depth)])

    out_g, out_d, out_m, out_v = {}, {}, {}, {}
    flat = lambda a: a.reshape(-1, a.shape[-1])
    res = adamw(g_mod_w.reshape(1, -1, mod_cols), flat(mod_w), flat(m_mod_w), flat(v_mod_w), name="adamw_mod_w")
    out_g["mod_w"], out_d["mod_w"], out_m["mod_w"], out_v["mod_w"] = [r.reshape(mod_w.shape) for r in res]
    updated = {}
    for l in range(depth):
        for p in ffn1 + rest:
            w_p = piece_of(w_in, p, l)
            res = adamw(parts[(p, l)], flat(w_p), flat(piece_of(m_in, p, l)), flat(piece_of(v_in, p, l)),
                        name=f"adamw_{p[0]}_l{l}" + ("" if p[1] is None else f"_{p[1]}"))
            updated[(p, l)] = [r.reshape(w_p.shape) for r in res]
    for k in big:
        for q_, dct in enumerate((out_g, out_d, out_m, out_v)):
            if k.startswith("ffn"):
                dct[k] = jnp.stack([jnp.stack([updated[((k, j), l)][q_] for j in range(2)])
                                    for l in range(depth)])
            else:
                dct[k] = jnp.stack([updated[((k, None), l)][q_] for l in range(depth)])
    small_all = ["mod_b", "norm_pre", "norm_post"] + small_names

    def pack_small(dct):
        flat = jnp.concatenate([dct[k].reshape(-1) for k in small_all])
        tot = -(-flat.size // chunk) * chunk
        return jnp.pad(flat, (0, tot - flat.size)).reshape(tot // 1024, 1024)

    res = adamw(pack_small(g_small)[None], pack_small(w_in), pack_small(m_in), pack_small(v_in),
                name="adamw_small")
    pos = 0
    for k in small_all:
        size = w_in[k].size
        for dct, r in zip((out_g, out_d, out_m, out_v), res):
            dct[k] = r.reshape(-1)[pos:pos + size].reshape(w_in[k].shape)
        pos += size

    return (loss, grad_x, *[out_g[k] for k in names], *[out_d[k] for k in names],
            *[out_m[k] for k in names], *[out_v[k] for k in names])
```

```python
import functools
import math

import jax
import jax.numpy as jnp
from jax import lax
from jax.experimental import pallas as pl
from jax.experimental.pallas import tpu as pltpu

F32 = jnp.float32
BF16 = jnp.bfloat16
MESH = pl.DeviceIdType.MESH
N_DEV = 8
LANES = 128
SUBLANES = 8
VMEM_LIMIT = 48 * 1024 * 1024

RMS_EPS = 1e-6
FFN_RES = 0.5
ADAM_LR = 0.001
ADAM_B1 = 0.9
ADAM_B2 = 0.999
ADAM_EPS = 1e-08
ADAM_WD = 0.01
ADAM_STEP = 10
GELU_C = math.sqrt(2.0 / math.pi)
GELU_A = 0.044715


def _pick(dim, target, mult=LANES):
    t = (min(dim, target) // mult) * mult
    while t >= mult:
        if dim % t == 0:
            return t
        t -= mult
    return dim


def _params(sem):
    return pltpu.CompilerParams(dimension_semantics=sem, vmem_limit_bytes=VMEM_LIMIT)


def _sigmoid(x):
    return 1.0 / (1.0 + jnp.exp(-x))


def mm(a, b, *, name, trans_a=False, trans_b=False, out_dtype=F32, tm=1024, tn=1408, tk=2816, comm=None):
    if trans_a:
        kdim, m = a.shape
    else:
        m, kdim = a.shape
    if trans_b:
        n, kb = b.shape
    else:
        kb, n = b.shape
    assert kdim == kb, (a.shape, b.shape)
    tm, tn, tk = _pick(m, tm), _pick(n, tn), _pick(kdim, tk)
    gm, gn, nk = m // tm, n // tn, kdim // tk
    dims = (((0 if trans_a else 1,), (1 if trans_b else 0,)), ((), ()))
    host = _Hosted(comm, 2, 1, 1 if nk > 1 else 0)

    def body(*refs):
        (a_ref, b_ref, o_ref, *acc), crefs = host.split(refs)
        i, j, k = pl.program_id(0), pl.program_id(1), pl.program_id(2)
        host.phase("start", (i == 0) & (j == 0) & (k == 0), crefs)
        prod = lax.dot_general(a_ref[...].astype(BF16), b_ref[...].astype(BF16), dims,
                               preferred_element_type=F32)
        if nk == 1:
            o_ref[...] = prod.astype(out_dtype)
        else:
            acc_ref, = acc

            @pl.when(k == 0)
            def _():
                acc_ref[...] = prod

            @pl.when((k > 0) & (k < nk - 1))
            def _():
                acc_ref[...] += prod

            @pl.when(k == nk - 1)
            def _():
                o_ref[...] = (acc_ref[...] + prod).astype(out_dtype)

        last = (i == gm - 1) & (j == gn - 1) & (k == nk - 1)
        host.phase("mid", last, crefs)
        host.phase("finish", last, crefs)

    a_spec = (pl.BlockSpec((tk, tm), lambda i, j, k: (k, i)) if trans_a
              else pl.BlockSpec((tm, tk), lambda i, j, k: (i, k)))
    b_spec = (pl.BlockSpec((tn, tk), lambda i, j, k: (j, k)) if trans_b
              else pl.BlockSpec((tk, tn), lambda i, j, k: (k, j)))
    in_specs, out_specs, out_shape, scratch, extra = host.specs(
        [a_spec, b_spec], [pl.BlockSpec((tm, tn), lambda i, j, k: (i, j))],
        [jax.ShapeDtypeStruct((m, n), out_dtype)], [pltpu.VMEM((tm, tn), F32)] if nk > 1 else [])
    res = pl.pallas_call(
        body, name=name, grid=(gm, gn, nk),
        in_specs=in_specs, out_specs=out_specs, out_shape=out_shape, scratch_shapes=scratch,
        compiler_params=_params(("arbitrary", "arbitrary", "arbitrary")),
    )(a, b, *extra)
    return res if comm else res[0]


def mm_blockdiag(a, b, *, name, a_cb0=0, out_dtype=F32, tm=512):
    m = a.shape[0]
    g_n, ka, nb = b.shape
    tm = _pick(m, tm)
    assert a_cb0 % g_n == 0

    def body(a_ref, b_ref, o_ref):
        for g in range(g_n):
            o_ref[:, g * nb:(g + 1) * nb] = jnp.dot(
                a_ref[:, g * ka:(g + 1) * ka].astype(BF16), b_ref[g].astype(BF16),
                preferred_element_type=F32).astype(out_dtype)

    return pl.pallas_call(
        body, name=name, grid=(m // tm,),
        in_specs=[pl.BlockSpec((tm, g_n * ka), lambda i: (i, a_cb0 // g_n)),
                  pl.BlockSpec((g_n, ka, nb), lambda i: (0, 0, 0))],
        out_specs=pl.BlockSpec((tm, g_n * nb), lambda i: (i, 0)),
        out_shape=jax.ShapeDtypeStruct((m, g_n * nb), out_dtype),
        compiler_params=_params(("parallel",)),
    )(a, b)


def mm_blockdiag_tn(a, b, *, name, g_n, ka, kb, a_cb0=0, b_cb0=0, tk=512):
    rows = a.shape[0]
    tk = _pick(rows, tk)
    nk = rows // tk
    assert a_cb0 % g_n == 0 and b_cb0 % g_n == 0

    def body(a_ref, b_ref, o_ref):
        @pl.when(pl.program_id(0) == 0)
        def _():
            o_ref[...] = jnp.zeros_like(o_ref)

        for g in range(g_n):
            o_ref[g] += lax.dot_general(a_ref[:, g * ka:(g + 1) * ka].astype(BF16),
                                        b_ref[:, g * kb:(g + 1) * kb].astype(BF16),
                                        (((0,), (0,)), ((), ())), preferred_element_type=F32)

    return pl.pallas_call(
        body, name=name, grid=(nk,),
        in_specs=[pl.BlockSpec((tk, g_n * ka), lambda k: (k, a_cb0 // g_n)),
                  pl.BlockSpec((tk, g_n * kb), lambda k: (k, b_cb0 // g_n))],
        out_specs=pl.BlockSpec((g_n, ka, kb), lambda k: (0, 0, 0)),
        out_shape=jax.ShapeDtypeStruct((g_n, ka, kb), F32),
        compiler_params=_params(("arbitrary",)),
    )(a, b)


def rowwise(fn, rows, vecs, outs, reds=(), *, name, tm=512):
    metas = []
    for r in rows:
        if isinstance(r, tuple) and len(r) == 3:
            metas.append(("col", r[0], r[1], r[2]))
        elif isinstance(r, tuple):
            metas.append(("lead", r[0], r[0].shape[2], r[1]))
        else:
            metas.append(("full", r, r.shape[1], 0))
    n_rows = metas[0][1].shape[1] if metas[0][0] == "lead" else metas[0][1].shape[0]
    rc = 16 if n_rows % 16 == 0 else (SUBLANES if n_rows % SUBLANES == 0 else n_rows)
    tm = _pick(n_rows, tm, rc)
    n_inner = tm // rc
    windows = [(o[2] if len(o) > 2 else None) for o in outs]
    bases = [(k, o[3]) for k, o in enumerate(outs) if len(o) > 3 and o[3] is not None]
    outs = [(o[0], o[1]) for o in outs]
    nr, nv, no, nbase = len(metas), len(vecs), len(outs), len(bases)

    def body(*refs):
        row_refs, vec_refs = refs[:nr], refs[nr:nr + nv]
        refs = refs[nr + nv + nbase:]
        out_refs, red_refs = refs[:no], refs[no:]
        if reds:
            @pl.when(pl.program_id(0) == 0)
            def _():
                for rr in red_refs:
                    rr[...] = jnp.zeros_like(rr)
        vec_vals = [v[...] for v in vec_refs]

        def step(s, carry):
            r0 = pl.multiple_of(s * rc, rc)
            vals = [ref[pl.ds(r0, rc), :] for ref in row_refs]
            res = fn(*vals, *vec_vals)
            if not isinstance(res, (tuple, list)):
                res = (res,)
            for o_ref, (widths, dt), val in zip(out_refs, outs, res[:no]):
                pieces = val if isinstance(val, (tuple, list)) else (val,)
                off = 0
                for w_, piece in zip(widths, pieces):
                    o_ref[pl.ds(r0, rc), off:off + w_] = piece.astype(dt)
                    off += w_
            for rr, val in zip(red_refs, res[no:]):
                rr[...] += val
            return carry

        lax.fori_loop(0, n_inner, step, 0, unroll=min(n_inner, 4))

    in_specs = []
    for kind, arr, w_, idx in metas:
        if kind == "col":
            in_specs.append(pl.BlockSpec((tm, w_), functools.partial(lambda i, cb: (i, cb), cb=idx)))
        elif kind == "lead":
            in_specs.append(pl.BlockSpec((None, tm, w_), functools.partial(lambda i, p: (p, i, 0), p=idx)))
        else:
            in_specs.append(pl.BlockSpec((tm, w_), lambda i: (i, 0)))
    for v in vecs:
        in_specs.append(pl.BlockSpec(v.shape, lambda i: (0, 0)))
    in_specs += [pl.BlockSpec(memory_space=pl.ANY)] * nbase
    out_specs, out_shape = [], []
    for (ws, dt), win in zip(outs, windows):
        total, cb = win if win is not None else (sum(ws), 0)
        out_specs.append(pl.BlockSpec((tm, sum(ws)), functools.partial(lambda i, cb: (i, cb), cb=cb)))
        out_shape.append(jax.ShapeDtypeStruct((n_rows, total), dt))
    out_specs += [pl.BlockSpec((1, w_), lambda i: (0, 0)) for w_ in reds]
    out_shape += [jax.ShapeDtypeStruct((1, w_), F32) for w_ in reds]
    res = pl.pallas_call(
        body, name=name, grid=(n_rows // tm,),
        in_specs=in_specs, out_specs=out_specs, out_shape=out_shape,
        input_output_aliases={nr + nv + b: k for b, (k, _) in enumerate(bases)},
        compiler_params=_params(("arbitrary",)),
    )(*[m[1] for m in metas], *vecs, *[b for _, b in bases])
    return res


def _rms(x):
    return lax.rsqrt(jnp.mean(x * x, axis=-1, keepdims=True) + RMS_EPS)


def _colsum(x):
    return jnp.sum(x, axis=0, keepdims=True)


def _f_silu(c):
    return c * _sigmoid(c)


def _f_pre(x, a, sh):
    return (x * _rms(x)) * a + sh


def _f_post_add(x, y, bv):
    return x + (y * _rms(y)) * bv


def _f_post_bwd(dxo, y, bv):
    ry = _rms(y)
    yn = y * ry
    dyn = dxo * bv
    dy = ry * (dyn - yn * jnp.mean(dyn * yn, axis=-1, keepdims=True))
    return dy, _colsum(dxo * yn)


def _f_pre_bwd(dxo, dh, x, a):
    r = _rms(x)
    xn = x * r
    dxn = dh * a
    dx = dxo + r * (dxn - xn * jnp.mean(dxn * xn, axis=-1, keepdims=True))
    return dx, _colsum(dh * xn), _colsum(dh)


def _f_swiglu(g, u):
    g = g.astype(F32)
    return (g * _sigmoid(g)) * u.astype(F32)


def _f_swiglu_bwd(g, u, dm):
    g, u, dm = g.astype(F32), u.astype(F32), dm.astype(F32)
    sg = _sigmoid(g)
    dg = dm * u * (sg * (1.0 + g * (1.0 - sg)))
    du = dm * (g * sg)
    return ((dg, du),)


def _gelu_t(x):
    return jnp.tanh(GELU_C * (x + GELU_A * x * x * x))


def _f_gelu_in(y0, u, dvec):
    y = y0 + dvec * u
    return 0.5 * y * (1.0 + _gelu_t(y)), y


def _f_gelu_bwd(dge, y, u):
    t = _gelu_t(y)
    dy = dge * (0.5 * (1.0 + t) + 0.5 * y * (1.0 - t * t) * GELU_C * (1.0 + 3.0 * GELU_A * y * y))
    return dy, _colsum(dy * u)


def _f_du_fin(du0, dys, dvec):
    return du0 + dvec * dys.astype(F32)


def _f_merge(zv, zg, yb, ga, gb):
    zv, zg, yb, ga, gb = [a.astype(F32) for a in (zv, zg, yb, ga, gb)]
    return _sigmoid(ga) * (zv * _sigmoid(zg)) + _sigmoid(gb) * yb


def _f_merge_bwd(dmg, zv, zg, yb, ga, gb):
    zv, zg, yb, ga, gb = [a.astype(F32) for a in (zv, zg, yb, ga, gb)]
    sa, sb, sz = _sigmoid(ga), _sigmoid(gb), _sigmoid(zg)
    ya = zv * sz
    dya = dmg * sa
    dga = dmg * ya * sa * (1.0 - sa)
    dyb = dmg * sb
    dgb = dmg * yb * sb * (1.0 - sb)
    dzv = dya * sz
    dzg = dya * zv * sz * (1.0 - sz)
    return (dzv, dzg), dyb, (dga, dgb)


def _f_sum_parts(*parts):
    acc = parts[0].astype(F32)
    for p in parts[1:]:
        acc = acc + p.astype(F32)
    return acc


def _f_adamw(*args):
    parts, (w, m, v) = args[:-3], args[-3:]
    g = _f_sum_parts(*parts)
    m = ADAM_B1 * m + (1.0 - ADAM_B1) * g
    v = ADAM_B2 * v + (1.0 - ADAM_B2) * (g * g)
    m_hat = m / (1.0 - ADAM_B1 ** ADAM_STEP)
    v_hat = v / (1.0 - ADAM_B2 ** ADAM_STEP)
    delta = -ADAM_LR * (m_hat / (jnp.sqrt(v_hat) + ADAM_EPS) + ADAM_WD * w)
    return g, delta, m, v


def adamw(parts3, w, m, v, *, name):
    c = w.shape[1]
    rows = [(parts3, p) for p in range(parts3.shape[0])] + [w, m, v]
    return rowwise(_f_adamw, rows, [], [((c,), F32)] * 4, name=name)


def _tri_dot(tri, x, dims=(((1,), (0,)), ((), ()))):
    x1 = x.astype(BF16)
    r1 = x - x1.astype(F32)
    x2 = r1.astype(BF16)
    x3 = (r1 - x2.astype(F32)).astype(BF16)
    dot = functools.partial(lax.dot_general, dimension_numbers=dims, preferred_element_type=F32)
    return dot(tri, x1) + dot(tri, x2) + dot(tri, x3)


def cum_fwd(proj, fb, *, col_block, name, t=256):
    n = proj.shape[0]
    t = _pick(n, t, SUBLANES)

    def body(f_ref, fb_ref, cum_ref, car_ref):
        @pl.when(pl.program_id(0) == 0)
        def _():
            car_ref[...] = jnp.zeros_like(car_ref)

        x = f_ref[...] + fb_ref[...]
        lf = jnp.minimum(x, 0.0) - jnp.log(1.0 + jnp.exp(-jnp.abs(x)))
        r = lax.broadcasted_iota(jnp.int32, (t, t), 0)
        c = lax.broadcasted_iota(jnp.int32, (t, t), 1)
        cs = _tri_dot((c <= r).astype(BF16), lf) + car_ref[0:1, :]
        cum_ref[...] = cs
        car_ref[0:1, :] = cs[t - 1:t, :]

    return pl.pallas_call(
        body, name=name, grid=(n // t,),
        in_specs=[pl.BlockSpec((t, LANES), lambda i: (i, col_block)),
                  pl.BlockSpec((1, LANES), lambda i: (0, 0))],
        out_specs=pl.BlockSpec((t, LANES), lambda i: (i, 0)),
        out_shape=jax.ShapeDtypeStruct((n, LANES), F32),
        scratch_shapes=[pltpu.VMEM((SUBLANES, LANES), F32)],
        compiler_params=_params(("arbitrary",)),
    )(proj, fb)


def cum_bwd(dcum, proj, fb, *, col_block, name, t=256):
    n = proj.shape[0]
    slabs = dcum.shape[0]
    t = _pick(n, t, SUBLANES)
    nb = n // t

    def body(dc_ref, f_ref, fb_ref, df_ref, dfb_ref, car_ref):
        @pl.when(pl.program_id(0) == 0)
        def _():
            car_ref[...] = jnp.zeros_like(car_ref)
            dfb_ref[...] = jnp.zeros_like(dfb_ref)

        r = lax.broadcasted_iota(jnp.int32, (t, t), 0)
        c = lax.broadcasted_iota(jnp.int32, (t, t), 1)
        dl = _tri_dot((c >= r).astype(BF16), jnp.sum(dc_ref[...], axis=0)) + car_ref[0:1, :]
        car_ref[0:1, :] = dl[0:1, :]
        x = f_ref[...] + fb_ref[...]
        df = dl * (1.0 / (1.0 + jnp.exp(x)))
        df_ref[...] = df
        dfb_ref[...] += _colsum(df)

    return pl.pallas_call(
        body, name=name, grid=(nb,),
        in_specs=[pl.BlockSpec((slabs, t, LANES), lambda i: (0, nb - 1 - i, 0)),
                  pl.BlockSpec((t, LANES), lambda i: (nb - 1 - i, col_block)),
                  pl.BlockSpec((1, LANES), lambda i: (0, 0))],
        out_specs=[pl.BlockSpec((t, LANES), lambda i: (nb - 1 - i, 0)),
                   pl.BlockSpec((1, LANES), lambda i: (0, 0))],
        out_shape=[jax.ShapeDtypeStruct((n, LANES), F32), jax.ShapeDtypeStruct((1, LANES), F32)],
        scratch_shapes=[pltpu.VMEM((SUBLANES, LANES), F32)],
        compiler_params=_params(("arbitrary",)),
    )(dcum, proj, fb)


_NT =(((1,), (1,)), ((), ()))
_TN = (((0,), (0,)), ((), ()))


def _lane_sums_as_row(x):
    return _tri_dot(jnp.ones((SUBLANES, x.shape[1]), BF16), x, _NT)[0:1, :]


def _causal_keep(t):
    return lax.broadcasted_iota(jnp.int32, (t, t), 1) <= lax.broadcasted_iota(jnp.int32, (t, t), 0)


def attn_fwd(q, k, v, crow, *, name, t=512, hb=8, comm=None):
    h_n, n, dh = q.shape
    hb = min(hb, h_n)
    t = _pick(n, t)
    nb = n // t
    ng = h_n // hb
    host = _Hosted(comm, 4, 2, 3)

    def body(*refs):
        (q_ref, k_ref, v_ref, cr_ref, o_ref, lse_ref, m_sc, l_sc, acc_sc), crefs = host.split(refs)
        g, i, j = pl.program_id(0), pl.program_id(1), pl.program_id(2)
        host.phase("start", (g == 0) & (i == 0) & (j == 0), crefs)
        host.phase("mid", (g == ng - 1) & (i == (3 * nb) // 4) & (j == 0), crefs)

        @pl.when(j == 0)
        def _():
            m_sc[...] = jnp.full_like(m_sc, -jnp.inf)
            l_sc[...] = jnp.zeros_like(l_sc)
            acc_sc[...] = jnp.zeros_like(acc_sc)

        def update(diagonal):
            keep = _causal_keep(t) if diagonal else None
            heads = range(hb)
            ss = [lax.dot_general(q_ref[h], k_ref[h], _NT, preferred_element_type=F32) for h in heads]
            pairs, alphas = [], []
            for h in heads:
                s = ss[h] - cr_ref[h]
                if diagonal:
                    s = jnp.where(keep, s, -jnp.inf)
                m_prev = m_sc[h]
                m_new = jnp.maximum(m_prev, jnp.max(s, axis=-1, keepdims=True))
                p = jnp.exp(s - m_new)
                alpha = jnp.exp(m_prev - m_new)
                l_sc[h] = alpha * l_sc[h] + jnp.sum(p, axis=-1, keepdims=True)
                m_sc[h] = m_new
                p_hi = p.astype(BF16)
                pairs.append((p_hi, (p - p_hi.astype(F32)).astype(BF16)))
                alphas.append(alpha)
            for h in heads:
                vv = v_ref[h]
                acc_sc[h] = (alphas[h] * acc_sc[h] + jnp.dot(pairs[h][0], vv, preferred_element_type=F32)
                             + jnp.dot(pairs[h][1], vv, preferred_element_type=F32))

        @pl.when(j < i)
        def _():
            update(False)

        @pl.when(j == i)
        def _():
            update(True)

        @pl.when(j == nb - 1)
        def _():
            o_ref[...] = acc_sc[...] / l_sc[...]
            lane0 = lax.broadcasted_iota(jnp.int32, (t, LANES), 1) == 0
            for h in range(hb):
                lse_col = m_sc[h] + jnp.log(l_sc[h])
                lse_ref[h] = _lane_sums_as_row(jnp.where(lane0, lse_col, 0.0))

        host.phase("finish", (g == ng - 1) & (i == nb - 1) & (j == nb - 1), crefs)

    qspec = pl.BlockSpec((hb, t, dh), lambda g, i, j: (g, i, 0))
    kspec = pl.BlockSpec((hb, t, dh), lambda g, i, j: (g, jnp.minimum(j, i), 0))
    in_specs, out_specs, out_shape, scratch, extra = host.specs(
        [qspec, kspec, kspec, pl.BlockSpec((hb, 1, t), lambda g, i, j: (g, 0, jnp.minimum(j, i)))],
        [qspec, pl.BlockSpec((hb, 1, t), lambda g, i, j: (g, 0, i))],
        [jax.ShapeDtypeStruct((h_n, n, dh), F32), jax.ShapeDtypeStruct((h_n, 1, n), F32)],
        [pltpu.VMEM((hb, t, 1), F32), pltpu.VMEM((hb, t, 1), F32), pltpu.VMEM((hb, t, dh), F32)])
    return pl.pallas_call(
        body, name=name, grid=(ng, nb, nb),
        in_specs=in_specs, out_specs=out_specs, out_shape=out_shape, scratch_shapes=scratch,
        compiler_params=_params(("arbitrary", "arbitrary", "arbitrary")),
    )(q, k, v, crow, *extra)


def attn_delta(do, o, *, name, t=512, hb=8):
    h_n, n, dh = do.shape
    hb = min(hb, h_n)
    t = _pick(n, t)

    def body(do_ref, o_ref, dl_ref):
        for h in range(hb):
            dl_ref[h] = _lane_sums_as_row(do_ref[h].astype(F32) * o_ref[h])

    spec = pl.BlockSpec((hb, t, dh), lambda g, i: (g, i, 0))
    return pl.pallas_call(
        body, name=name, grid=(h_n // hb, n // t),
        in_specs=[spec, spec], out_specs=pl.BlockSpec((hb, 1, t), lambda g, i: (g, 0, i)),
        out_shape=jax.ShapeDtypeStruct((h_n, 1, n), F32),
        compiler_params=_params(("parallel", "parallel")),
    )(do, o)


def attn_bwd(q, k, v, do, o, lse, cum, *, scale, tag, t=512, hb=2, comm=None):
    h_n, n, dh = q.shape
    hb = min(hb, h_n)
    t = _pick(n, t)
    nb = n // t
    dob = do.astype(BF16)
    delta = attn_delta(dob, o, name=f"attn_delta_{tag}", t=t)

    ng = h_n // hb
    host = _Hosted(comm, 7, 4, 3)

    def body(*refs):
        (q_ref, k_ref, v_ref, do_ref, lse_ref, dl_ref, cum_ref,
         dq_ref, dk_ref, dv_ref, dcum_ref, dk_acc, dv_acc, dcc_acc), crefs = host.split(refs)
        g, j, i = pl.program_id(0), pl.program_id(1), pl.program_id(2)
        host.phase("start", (g == 0) & (j == 0) & (i == 0), crefs)
        lane = lax.broadcasted_iota(jnp.int32, (t, LANES), 1)

        @pl.when((j == 0) & (i == 0))
        def _():
            dq_ref[...] = jnp.zeros_like(dq_ref)

        @pl.when(i == 0)
        def _():
            dk_acc[...] = jnp.zeros_like(dk_acc)
            dv_acc[...] = jnp.zeros_like(dv_acc)
            dcc_acc[...] = jnp.zeros_like(dcc_acc)

        def update(diagonal):
            heads = range(hb)
            r0 = pl.multiple_of(i * t, t)
            if diagonal:
                keep = lax.broadcasted_iota(jnp.int32, (t, t), 0) <= lax.broadcasted_iota(jnp.int32, (t, t), 1)
            qv, kv = [q_ref[h] for h in heads], [k_ref[h] for h in heads]
            vv, dov = [v_ref[h] for h in heads], [do_ref[h] for h in heads]
            st = [lax.dot_general(kv[h], qv[h], _NT, preferred_element_type=F32) for h in heads]
            dpt = [lax.dot_general(vv[h], dov[h], _NT, preferred_element_type=F32) for h in heads]
            pt = []
            cum_tile = cum_ref[...]
            for h in heads:
                cc = jnp.sum(jnp.where(lane == g * hb + h, cum_tile, 0.0), axis=1, keepdims=True)
                s = st[h] - cc
                if diagonal:
                    s = jnp.where(keep, s, -jnp.inf)
                pt.append(jnp.exp(s - lse_ref[h]))
            for h in heads:
                dv_acc[h] += jnp.dot(pt[h].astype(BF16), dov[h], preferred_element_type=F32)
            dsb = []
            for h in heads:
                ds = pt[h] * (dpt[h] - dl_ref[h])
                dcc_acc[h] -= jnp.sum(ds, axis=1, keepdims=True)
                dsb.append(ds.astype(BF16))
            for h in heads:
                dk_acc[h] += jnp.dot(dsb[h], qv[h], preferred_element_type=F32)
            for h in heads:
                dq_ref[h, pl.ds(r0, t), :] += lax.dot_general(dsb[h], kv[h], _TN,
                                                              preferred_element_type=F32) * scale

        @pl.when(i > j)
        def _():
            update(False)

        @pl.when(i == j)
        def _():
            update(True)

        @pl.when(i == nb - 1)
        def _():
            dk_ref[...] = dk_acc[...]
            dv_ref[...] = dv_acc[...]
            tile = jnp.zeros((t, LANES), F32)
            for h in range(hb):
                tile = tile + jnp.where(lane == g * hb + h, dcc_acc[h], 0.0)
            dcum_ref[...] = tile

        host.phase("finish", (g == ng - 1) & (j == nb - 1) & (i == nb - 1), crefs)

    qspec = pl.BlockSpec((hb, t, dh), lambda g, j, i: (g, jnp.maximum(i, j), 0))
    qrow = pl.BlockSpec((hb, 1, t), lambda g, j, i: (g, 0, jnp.maximum(i, j)))
    kspec = pl.BlockSpec((hb, t, dh), lambda g, j, i: (g, j, 0))
    in_specs, out_specs, out_shape, scratch, extra = host.specs(
        [qspec, kspec, kspec, qspec, qrow, qrow, pl.BlockSpec((t, LANES), lambda g, j, i: (j, 0))],
        [pl.BlockSpec((hb, n, dh), lambda g, j, i: (g, 0, 0)), kspec, kspec,
         pl.BlockSpec((None, t, LANES), lambda g, j, i: (g, j, 0))],
        [jax.ShapeDtypeStruct((h_n, n, dh), F32)] * 3 + [jax.ShapeDtypeStruct((ng, n, LANES), F32)],
        [pltpu.VMEM((hb, t, dh), F32), pltpu.VMEM((hb, t, dh), F32), pltpu.VMEM((hb, t, 1), F32)])
    dq, dk, dv, dcum, *arrived = pl.pallas_call(
        body, name=f"attn_bwd_{tag}", grid=(ng, nb, nb),
        in_specs=in_specs, out_specs=out_specs, out_shape=out_shape, scratch_shapes=scratch,
        compiler_params=_params(("arbitrary", "arbitrary", "arbitrary")),
    )(q, k, v, dob, lse, delta, cum, *extra)
    return [dq, dk, dv, dcum] + arrived


SCAN_STEPS = (1, 2, 4)


def ssm_scan(x, tab, *, reverse, name, s_prev=None, tt=512):
    n, width = x.shape
    nc, _, hw = tab.shape
    cw = 2 * hw
    assert width == nc * cw
    tt = _pick(n, tt, SUBLANES)
    nt = n // tt
    ng = tt // SUBLANES
    with_grad = s_prev is not None

    def body(*refs):
        if with_grad:
            x_ref, s_ref, tab_ref, o_ref, g_ref, car_ref = refs
        else:
            x_ref, tab_ref, o_ref, car_ref = refs

        @pl.when(pl.program_id(1) == 0)
        def _():
            car_ref[...] = jnp.zeros_like(car_ref)
            if with_grad:
                g_ref[...] = jnp.zeros_like(g_ref)

        q_re, q_im = tab_ref[0:8, :], tab_ref[8:16, :]
        p_re = [tab_ref[16 + i:17 + i, :] for i in range(3)]
        p_im = [tab_ref[24 + i:25 + i, :] for i in range(3)]
        row = lax.broadcasted_iota(jnp.int32, (SUBLANES, hw), 0)

        def group(gi, carry):
            c_re, c_im = carry
            g = (ng - 1 - gi) if reverse else gi
            r0 = pl.multiple_of(g * SUBLANES, SUBLANES)
            xr = x_ref[pl.ds(r0, SUBLANES), 0:hw]
            xi = x_ref[pl.ds(r0, SUBLANES), hw:cw]
            for i, d in enumerate(SCAN_STEPS):
                if reverse:
                    shift, keep = SUBLANES - d, row < SUBLANES - d
                else:
                    shift, keep = d, row >= d
                sr = jnp.where(keep, pltpu.roll(xr, shift, 0), 0.0)
                si = jnp.where(keep, pltpu.roll(xi, shift, 0), 0.0)
                xr, xi = (xr + p_re[i] * sr - p_im[i] * si,
                          xi + p_re[i] * si + p_im[i] * sr)
            xr, xi = (xr + q_re * c_re - q_im * c_im,
                      xi + q_re * c_im + q_im * c_re)
            o_ref[pl.ds(r0, SUBLANES), 0:hw] = xr
            o_ref[pl.ds(r0, SUBLANES), hw:cw] = xi
            if with_grad:
                nr = jnp.where(row < SUBLANES - 1, pltpu.roll(xr, SUBLANES - 1, 0), c_re)
                ni = jnp.where(row < SUBLANES - 1, pltpu.roll(xi, SUBLANES - 1, 0), c_im)
                sr = s_ref[pl.ds(r0, SUBLANES), 0:hw]
                si = s_ref[pl.ds(r0, SUBLANES), hw:cw]
                g_ref[:, 0:hw] += nr * sr + ni * si
                g_ref[:, hw:cw] += ni * sr - nr * si
            if reverse:
                return xr[0:1, :], xi[0:1, :]
            return xr[SUBLANES - 1:SUBLANES, :], xi[SUBLANES - 1:SUBLANES, :]

        c_re, c_im = lax.fori_loop(0, ng, group, (car_ref[0:1, 0:hw], car_ref[0:1, hw:cw]),
                                   unroll=min(ng, 4))
        car_ref[0:1, 0:hw] = c_re
        car_ref[0:1, hw:cw] = c_im

    if reverse:
        xspec = pl.BlockSpec((tt, cw), lambda c, t: (nt - 1 - t, c))
    else:
        xspec = pl.BlockSpec((tt, cw), lambda c, t: (t, c))
    tspec = pl.BlockSpec((None, 32, hw), lambda c, t: (c, 0, 0))
    in_specs = [xspec, xspec, tspec] if with_grad else [xspec, tspec]
    out_specs = [xspec]
    out_shape = [jax.ShapeDtypeStruct((n, width), F32)]
    if with_grad:
        out_specs.append(pl.BlockSpec((None, SUBLANES, cw), lambda c, t: (c, 0, 0)))
        out_shape.append(jax.ShapeDtypeStruct((nc, SUBLANES, cw), F32))
    operands = (x, s_prev, tab) if with_grad else (x, tab)
    return pl.pallas_call(
        body, name=name, grid=(nc, nt),
        in_specs=in_specs, out_specs=out_specs, out_shape=out_shape,
        scratch_shapes=[pltpu.VMEM((SUBLANES, cw), F32)],
        compiler_params=_params(("parallel", "arbitrary")),
    )(*operands)


def _slot(pos):
    return 4 * pos[0] + 2 * pos[1] + pos[2]


def _comm_scratch(n):
    return [pltpu.SemaphoreType.DMA((7 * n,)), pltpu.SemaphoreType.DMA((7 * n,)), pltpu.SemaphoreType.DMA((n,))]


def _gather_copies(ins, outs, sems):
    send_sems, recv_sems, local_sems = sems
    n = len(ins)
    x, y, c = lax.axis_index("x"), lax.axis_index("y"), lax.axis_index("c")
    me, sibling = (x, y, c), (x, y, 1 - c)
    chips = [(1 - x, y), (x, 1 - y), (1 - x, 1 - y)]

    def copy(t, k, block, to, src=None):
        dst = outs[t].at[_slot(block)]
        return pltpu.make_async_remote_copy(
            src_ref=dst if src is None else src, dst_ref=dst,
            send_sem=send_sems.at[7 * t + k], recv_sem=recv_sems.at[7 * t + k],
            device_id=to, device_id_type=MESH)

    jc = list(enumerate(chips))
    return dict(
        mine=[pltpu.make_async_copy(ins[t], outs[t].at[_slot(me)], local_sems.at[t]) for t in range(n)],
        first=[cp for t in range(n) for cp in
               [copy(t, 0, me, sibling, src=ins[t])] + [copy(t, 1 + j, me, (*chip, c), src=ins[t]) for j, chip in jc]],
        arrive=[copy(t, 1 + j, (*chip, c), me) for t in range(n) for j, chip in jc],
        passed=[copy(t, 4 + j, (*chip, c), sibling) for t in range(n) for j, chip in jc],
        from_sibling=[cp for t in range(n) for cp in
                      [copy(t, 0, sibling, me)] + [copy(t, 4 + j, (*chip, 1 - c), me) for j, chip in jc]])


def _gather_start(ins, outs, sems):
    cps = _gather_copies(ins, outs, sems)
    for cp in cps["mine"] + cps["first"]:
        cp.start()


def _gather_forward(ins, outs, sems):
    cps = _gather_copies(ins, outs, sems)
    for arrived, onward in zip(cps["arrive"], cps["passed"]):
        arrived.wait_recv()
        onward.start()


def _gather_finish(ins, outs, sems):
    cps = _gather_copies(ins, outs, sems)
    for cp in cps["from_sibling"]:
        cp.wait_recv()
    for cp in cps["first"] + cps["passed"]:
        cp.wait_send()
    for cp in cps["mine"]:
        cp.wait()


def gather_comm(arrs):
    return dict(ins=list(arrs), out_shape=[jax.ShapeDtypeStruct((N_DEV,) + a.shape, a.dtype) for a in arrs],
                scratch=_comm_scratch(len(arrs)), start=_gather_start, mid=_gather_forward, finish=_gather_finish)


def _exchange_copies(ins, outs, sems):
    send_sems, recv_sems, local_sems = sems
    n = len(ins)
    me = (lax.axis_index("x"), lax.axis_index("y"), lax.axis_index("c"))
    peers = []
    for k in range(1, N_DEV):
        flip = ((k >> 2) & 1, (k >> 1) & 1, k & 1)
        peers.append(tuple(1 - p if f else p for p, f in zip(me, flip)))

    def copy(t, k, peer, dst_slot):
        return pltpu.make_async_remote_copy(
            src_ref=ins[t].at[_slot(peer)], dst_ref=outs[t].at[dst_slot],
            send_sem=send_sems.at[7 * t + k], recv_sem=recv_sems.at[7 * t + k],
            device_id=peer, device_id_type=MESH)

    return dict(
        mine=[pltpu.make_async_copy(ins[t].at[_slot(me)], outs[t].at[_slot(me)], local_sems.at[t])
              for t in range(n)],
        send=[copy(t, k, peer, _slot(me)) for t in range(n) for k, peer in enumerate(peers)],
        both=[copy(t, k, peer, _slot(peer)) for t in range(n) for k, peer in enumerate(peers)])


def _exchange_start(ins, outs, sems):
    cps = _exchange_copies(ins, outs, sems)
    for cp in cps["mine"] + cps["send"]:
        cp.start()


def _exchange_finish(ins, outs, sems):
    cps = _exchange_copies(ins, outs, sems)
    for cp in cps["both"]:
        cp.wait()
    for cp in cps["mine"]:
        cp.wait()


def exchange_comm(arrs):
    return dict(ins=list(arrs), out_shape=[jax.ShapeDtypeStruct(a.shape, a.dtype) for a in arrs],
                scratch=_comm_scratch(len(arrs)), start=_exchange_start, mid=None, finish=_exchange_finish)


def run_comm(comm, *, name):
    n_in, n_out = len(comm["ins"]), len(comm["out_shape"])

    def body(*refs):
        ins, outs, sems = refs[:n_in], refs[n_in:n_in + n_out], refs[n_in + n_out:]
        comm["start"](ins, outs, sems)
        if comm["mid"] is not None:
            comm["mid"](ins, outs, sems)
        comm["finish"](ins, outs, sems)

    any_spec = pl.BlockSpec(memory_space=pl.ANY)
    return pl.pallas_call(
        body, name=name, in_specs=[any_spec] * n_in, out_specs=[any_spec] * n_out,
        out_shape=comm["out_shape"], scratch_shapes=comm["scratch"],
    )(*comm["ins"])


class _Hosted:
    def __init__(self, comm, n_in, n_out, n_scratch):
        self.comm = comm
        self.n_ci = len(comm["ins"]) if comm else 0
        self.n_co = len(comm["out_shape"]) if comm else 0
        self.n_in, self.n_out, self.n_scratch = n_in, n_out, n_scratch

    def split(self, refs):
        a = self.n_in
        b = a + self.n_ci
        c = b + self.n_out
        e = c + self.n_co
        f = e + self.n_scratch
        return refs[:a] + refs[b:c] + refs[e:f], (refs[a:b], refs[c:e], refs[f:])

    def phase(self, which, when, crefs):
        fn = self.comm[which] if self.comm else None
        if fn is not None:
            pl.when(when)(lambda: fn(*crefs))

    def specs(self, in_specs, out_specs, out_shape, scratch):
        any_spec = pl.BlockSpec(memory_space=pl.ANY)
        if not self.comm:
            return in_specs, out_specs, out_shape, scratch, ()
        return (in_specs + [any_spec] * self.n_ci, out_specs + [any_spec] * self.n_co,
                out_shape + self.comm["out_shape"], scratch + self.comm["scratch"], tuple(self.comm["ins"]))


def all_gather(arrs, *, name):
    return run_comm(gather_comm(arrs), name=name)


def _discretise(a_re, a_im, log_dt, b_re, b_im):
    ar = jnp.minimum(a_re, -1e-4)
    dt = jnp.exp(log_dt)[:, None]
    e, ph = ar * dt, a_im * dt
    mag = jnp.exp(e)
    lr, li = mag * jnp.cos(ph), mag * jnp.sin(ph)
    den = ar * ar + a_im * a_im
    nr, ni = lr - 1.0, li
    cr = (nr * ar + ni * a_im) / den
    ci = (ni * ar - nr * a_im) / den
    bb_re = cr[..., None] * b_re - ci[..., None] * b_im
    bb_im = cr[..., None] * b_im + ci[..., None] * b_re
    return e, ph, bb_re, bb_im


def _lam_pow(e, ph, k, conj):
    mag = jnp.exp(k * e)
    return mag * jnp.cos(k * ph), (-1.0 if conj else 1.0) * mag * jnp.sin(k * ph)


def _scan_table(e, ph, nc, reverse):
    hw = e.size // nc
    e, ph = e.reshape(nc, 1, hw), ph.reshape(nc, 1, hw)
    j = jnp.arange(SUBLANES, dtype=F32).reshape(1, SUBLANES, 1)
    kq = (SUBLANES - j) if reverse else (j + 1.0)
    q_re, q_im = _lam_pow(e, ph, kq, reverse)
    kp = jnp.array(SCAN_STEPS + (0,) * 5, F32).reshape(1, SUBLANES, 1)
    p_re, p_im = _lam_pow(e, ph, kp, reverse)
    return jnp.concatenate([q_re, q_im, p_re, p_im], axis=1)


def _blockdiag(m, nc):
    g, a, b = m.shape
    gc = g // nc
    m = m.reshape(nc, gc, a, b)
    eye = jnp.eye(gc, dtype=m.dtype)
    return jnp.einsum("cgab,gh->cgahb", m, eye).reshape(nc, gc * a, gc * b)


def _blockdiag_take(m, g):
    nc = m.shape[0]
    gc = g // nc
    a, b = m.shape[1] // gc, m.shape[2] // gc
    m = m.reshape(nc, gc, a, gc, b)
    eye = jnp.eye(gc, dtype=m.dtype)
    return jnp.einsum("cgahb,gh->cgab", m, eye).reshape(g, a, b)


def kernel(x, c, mod_w, mod_b, norm_pre, norm_post, ffn_w_in, ffn_w_out, mix_w_in, forget_b, ssm_a_re, ssm_a_im, ssm_log_dt, ssm_b_re, ssm_b_im, ssm_c_re, ssm_c_im, ssm_d, glu_w, attn_w_out, mix_w_out, loss_target, m_mod_w, m_mod_b, m_norm_pre, m_norm_post, m_ffn_w_in, m_ffn_w_out, m_mix_w_in, m_forget_b, m_ssm_a_re, m_ssm_a_im, m_ssm_log_dt, m_ssm_b_re, m_ssm_b_im, m_ssm_c_re, m_ssm_c_im, m_ssm_d, m_glu_w, m_attn_w_out, m_mix_w_out, v_mod_w, v_mod_b, v_norm_pre, v_norm_post, v_ffn_w_in, v_ffn_w_out, v_mix_w_in, v_forget_b, v_ssm_a_re, v_ssm_a_im, v_ssm_log_dt, v_ssm_b_re, v_ssm_b_im, v_ssm_c_re, v_ssm_c_im, v_ssm_d, v_glu_w, v_attn_w_out, v_mix_w_out):
    names = ["mod_w", "mod_b", "norm_pre", "norm_post", "ffn_w_in", "ffn_w_out", "mix_w_in", "forget_b",
             "ssm_a_re", "ssm_a_im", "ssm_log_dt", "ssm_b_re", "ssm_b_im", "ssm_c_re", "ssm_c_im", "ssm_d",
             "glu_w", "attn_w_out", "mix_w_out"]
    w_in = dict(zip(names, [mod_w, mod_b, norm_pre, norm_post, ffn_w_in, ffn_w_out, mix_w_in, forget_b,
                            ssm_a_re, ssm_a_im, ssm_log_dt, ssm_b_re, ssm_b_im, ssm_c_re, ssm_c_im, ssm_d,
                            glu_w, attn_w_out, mix_w_out]))
    m_in = dict(zip(names, [m_mod_w, m_mod_b, m_norm_pre, m_norm_post, m_ffn_w_in, m_ffn_w_out, m_mix_w_in,
                            m_forget_b, m_ssm_a_re, m_ssm_a_im, m_ssm_log_dt, m_ssm_b_re, m_ssm_b_im,
                            m_ssm_c_re, m_ssm_c_im, m_ssm_d, m_glu_w, m_attn_w_out, m_mix_w_out]))
    v_in = dict(zip(names, [v_mod_w, v_mod_b, v_norm_pre, v_norm_post, v_ffn_w_in, v_ffn_w_out, v_mix_w_in,
                            v_forget_b, v_ssm_a_re, v_ssm_a_im, v_ssm_log_dt, v_ssm_b_re, v_ssm_b_im,
                            v_ssm_c_re, v_ssm_c_im, v_ssm_d, v_glu_w, v_attn_w_out, v_mix_w_out]))

    depth = mod_w.shape[0]
    n_tok, d = x.shape[1], x.shape[2]
    ff = ffn_w_out.shape[2] * N_DEV
    heads = forget_b.shape[1]
    sw = ssm_d.shape[1]
    g_n, p_n, n_n = ssm_b_re.shape[1:]
    aw = attn_w_out.shape[1]
    dh = aw // heads
    iw = mix_w_in.shape[2] * N_DEV
    nc = sw // LANES
    hw = g_n * p_n // nc
    mod_cols = mod_w.shape[2]
    scale = dh ** -0.5
    assert iw == sw + 3 * aw + heads + 2 * d and heads <= LANES
    assert math.log2(scale).is_integer(), "q is pre-scaled in bf16: exact only for a power of two"
    off_u, off_q, off_f = 2 * d, 2 * d + sw, 2 * d + sw + 3 * aw
    iwp = off_f + LANES
    assert off_u % sw == 0 and off_q % aw == 0 and off_f % LANES == 0

    me = 4 * lax.axis_index("x") + 2 * lax.axis_index("y") + lax.axis_index("c")
    x2 = x.reshape(n_tok, d)
    tgt = loss_target.reshape(n_tok, d)

    silu_c = rowwise(_f_silu, [c], [], [((d,), F32)], name="silu_c")[0]
    big = ["ffn_w_in", "ffn_w_out", "mix_w_in", "glu_w", "attn_w_out", "mix_w_out"]
    ffn1 = [("ffn_w_in", 0), ("ffn_w_out", 0)]
    mix = [("mix_w_in", None), ("glu_w", None), ("attn_w_out", None), ("mix_w_out", None)]
    ffn2 = [("ffn_w_in", 1), ("ffn_w_out", 1)]

    def riders(l):
        nxt = [(l + 1, p) for p in ffn1 + mix] if l + 1 < depth else []
        return [(l, p) for p in ffn2] + nxt

    def piece_of(dct, piece, l):
        name, j = piece
        return dct[name][l] if j is None else dct[name][l][j]

    def shards(l, pieces):
        return [piece_of(w_in, p, l).astype(BF16) for p in pieces]

    cut = [0, sw, sw + aw, sw + 2 * aw, sw + 3 * aw, sw + 3 * aw + heads, sw + 3 * aw + heads + d, iw]
    lw = [dict(win=[None, None], wout=[None, None]) for _ in range(depth)]

    def install(l, pieces, gathered):
        for (name, j), g in zip(pieces, gathered):
            if name == "ffn_w_in":
                lw[l]["win"][j] = jnp.moveaxis(g, 0, 1).reshape(d, 2 * ff)
            elif name == "ffn_w_out":
                lw[l]["wout"][j] = g.reshape(ff, d)
            elif name == "mix_w_in":
                w_mix_in = jnp.moveaxis(g, 0, 1).reshape(d, iw)
                seg = lambda i: w_mix_in[:, cut[i]:cut[i + 1]]
                lw[l]["wmi"] = jnp.concatenate([seg(5), seg(6), seg(0), seg(1), seg(2), seg(3),
                                                jnp.pad(seg(4), ((0, 0), (0, LANES - heads)))], axis=-1)
            elif name == "mix_w_out":
                lw[l]["mo"] = g.reshape(d, d)
            else:
                lw[l]["glu" if name == "glu_w" else "ao"] = jnp.moveaxis(g, 0, 1).reshape(g.shape[1], -1)

    gathered = all_gather(
        [silu_c, norm_pre.reshape(-1, norm_pre.shape[-1]), norm_post.reshape(-1, norm_post.shape[-1])]
        + shards(0, ffn1), name="gather_first")
    sc_all = gathered[0].reshape(N_DEV, d)
    gpre = jnp.moveaxis(gathered[1].reshape(N_DEV, depth, 3, -1), 0, 2).reshape(depth, 3, d)
    gpost = jnp.moveaxis(gathered[2].reshape(N_DEV, depth, 3, -1), 0, 2).reshape(depth, 3, d)
    install(0, ffn1, gathered[3:])

    sc_pad = jnp.pad(sc_all, ((0, LANES - N_DEV), (0, 0)))
    mod_part = jnp.stack([mm(sc_pad, mod_w[l], name=f"mod_fwd{l}")[:N_DEV] for l in range(depth)], axis=1)
    mod_part = mod_part + lax.dynamic_slice_in_dim(mod_b, me * mod_cols, mod_cols, axis=1)[None]
    mod_all = all_gather([mod_part], name="gather_mod")[0]
    mod_own = lax.dynamic_index_in_dim(mod_all, me, axis=1, keepdims=False)
    mod_own = mod_own.transpose(1, 0, 2).reshape(depth, 3, 3, d)
    res_w = (FFN_RES, 1.0, FFN_RES)

    def vec_a(l, i):
        return (gpre[l, i] * (1.0 + mod_own[l, i, 1])).reshape(1, d)

    def vec_sh(l, i):
        return mod_own[l, i, 0].reshape(1, d)

    def vec_b(l, i):
        return (res_w[i] * mod_own[l, i, 2] * gpost[l, i]).reshape(1, d)

    ssm = []
    for l in range(depth):
        (e, ph, bb_re, bb_im), disc_vjp = jax.vjp(_discretise,ssm_a_re[l], ssm_a_im[l], ssm_log_dt[l],
                                                  ssm_b_re[l], ssm_b_im[l])
        b_mat = jnp.concatenate([_blockdiag(bb_re.transpose(0, 2, 1), nc),
                                 _blockdiag(bb_im.transpose(0, 2, 1), nc)], axis=2)
        c_mat = jnp.concatenate([_blockdiag(ssm_c_re[l].transpose(0, 2, 1), nc),
                                 _blockdiag(-ssm_c_im[l].transpose(0, 2, 1), nc)], axis=1)
        ssm.append(dict(e=e, ph=ph, vjp=disc_vjp, b=b_mat.astype(BF16), c=c_mat.astype(BF16),
                        bt=b_mat.transpose(0, 2, 1).astype(BF16), ct=c_mat.transpose(0, 2, 1).astype(BF16),
                        tab_f=_scan_table(e, ph, nc, False), tab_r=_scan_table(e, ph, nc, True),
                        dvec=ssm_d[l].reshape(1, sw)))

    fb_pad = jnp.pad(forget_b, ((0, 0), (0, LANES - heads)))

    def heads_first(a):
        return a.reshape(n_tok, heads, dh).transpose(1, 0, 2)

    def heads_last(a):
        return a.transpose(1, 0, 2).reshape(n_tok, heads * dh)

    def mm_hosting(a, b, comm, **kw):
        if comm is None:
            return mm(a, b, **kw), []
        out, *arrived = mm(a, b, comm=comm, **kw)
        return out, arrived

    def ffn_fwd(xin, l, i, j, tag, comm=None):
        h = rowwise(_f_pre, [xin], [vec_a(l, i), vec_sh(l, i)], [((d,), BF16)], name=f"pre_{tag}")[0]
        a, arrived = mm_hosting(h, lw[l]["win"][j], comm, name=f"ffn_in_{tag}", out_dtype=BF16)
        m = rowwise(_f_swiglu, [(a, ff, 0), (a, ff, 1)], [], [((ff,), BF16)], name=f"swiglu_{tag}")[0]
        y = mm(m, lw[l]["wout"][j], name=f"ffn_out_{tag}")
        xout = rowwise(_f_post_add, [xin, y], [vec_b(l, i)], [((d,), F32)], name=f"post_{tag}")[0]
        return xout, dict(x=xin, h=h, a=a, m=m, y=y), arrived

    def mixer_fwd(xin, l, tag):
        s5 = ssm[l]
        h = rowwise(_f_pre, [xin], [vec_a(l, 1), vec_sh(l, 1)], [((d,), BF16)], name=f"pre_{tag}")[0]
        proj = mm(h, lw[l]["wmi"], name=f"mix_in_{tag}", out_dtype=BF16)
        projf = mm(h, lw[l]["wmi"][:, off_f:], name=f"mix_in_f_{tag}")
        bu = mm_blockdiag(proj, s5["b"], a_cb0=off_u // LANES, name=f"ssm_bu_{tag}")
        st = ssm_scan(bu, s5["tab_f"], reverse=False, name=f"ssm_scan_{tag}")[0]
        y0 = mm_blockdiag(st, s5["c"], name=f"ssm_y_{tag}")
        ge, ys = rowwise(_f_gelu_in, [y0, (proj, sw, off_u // sw)], [s5["dvec"]],
                         [((sw,), BF16), ((sw,), F32)], name=f"gelu_{tag}")
        z = mm(ge, lw[l]["glu"], name=f"glu_{tag}", out_dtype=BF16)
        cum = cum_fwd(projf, fb_pad[l:l + 1], col_block=0, name=f"cum_{tag}")
        crow = cum[:, :heads].T[:, None, :]
        q, k, v = [heads_first(proj[:, off_q + i * aw:off_q + (i + 1) * aw] * sc_).astype(BF16)
                   for i, sc_ in enumerate((scale, 1.0, 1.0))]
        nxt = gather_comm([piece_of(w_in, p, ll).astype(BF16) for ll, p in riders(l)])
        o, lse, *arrived = attn_fwd(q, k, v, crow, name=f"attn_{tag}", comm=nxt)
        for (ll, p), got in zip(riders(l), arrived):
            install(ll, [p], [got])
        attn = heads_last(o).astype(BF16)
        yb = mm(attn, lw[l]["ao"], name=f"attn_out_{tag}", out_dtype=BF16)
        mg = rowwise(_f_merge, [(z, d, 0), (z, d, 1), yb, (proj, d, 0), (proj, d, 1)], [],
                     [((d,), BF16)], name=f"merge_{tag}")[0]
        y = mm(mg, lw[l]["mo"], name=f"mix_out_{tag}")
        xout = rowwise(_f_post_add, [xin, y], [vec_b(l, 1)], [((d,), F32)], name=f"post_{tag}")[0]
        saved = dict(x=xin, h=h, proj=proj, projf=projf, st=st, ys=ys, ge=ge, z=z, q=q, k=k, v=v, o=o, lse=lse,
                     cum=cum, attn=attn, yb=yb, mg=mg, y=y)
        return xout, saved

    saved = []
    xc = x2
    for l in range(depth):
        xc, s0, arrived = ffn_fwd(xc, l, 0, 0, f"l{l}a", gather_comm(shards(0, mix)) if l == 0 else None)
        if l == 0:
            install(0, mix, arrived)
        xc, s1 = mixer_fwd(xc, l, f"l{l}m")
        xc, s2, _ = ffn_fwd(xc, l, 2, 1, f"l{l}b")
        saved.append((s0, s1, s2))

    def f_loss(xf, t):
        e_ = xf - t
        return e_ * (1.0 / d), _colsum(e_ * e_)

    dx, sq = rowwise(f_loss, [xc, tgt], [], [((d,), F32)], [d], name="loss_head")
    loss_part = 0.5 * jnp.sum(sq) / d

    grads = {k: [None] * depth for k in big}
    small_g = [dict() for _ in range(depth)]
    dmod = [[None] * 3 for _ in range(depth)]
    dgpre = [[None] * 3 for _ in range(depth)]
    dgpost = [[None] * 3 for _ in range(depth)]

    def norm_grads(l, i, d_a, d_sh, d_bv):
        d_a, d_sh, d_bv = d_a.reshape(d), d_sh.reshape(d), d_bv.reshape(d)
        dmod[l][i] = jnp.stack([d_sh, d_a * gpre[l, i], res_w[i] * gpost[l, i] * d_bv])
        dgpre[l][i] = d_a * (1.0 + mod_own[l, i, 1])
        dgpost[l][i] = res_w[i] * mod_own[l, i, 2] * d_bv

    def ffn_bwd(dxo, sv, l, i, j, tag, comm_dw=None, comm_dx_of=None):
        dy, d_bv = rowwise(_f_post_bwd, [dxo, sv["y"]], [vec_b(l, i)], [((d,), BF16)], [d],
                           name=f"post_bwd_{tag}")
        dm = mm(dy, lw[l]["wout"][j], trans_b=True, name=f"ffn_out_dx_{tag}", out_dtype=BF16)
        g_out = mm(sv["m"], dy, trans_a=True, name=f"ffn_out_dw_{tag}", out_dtype=BF16, tm=1408, tn=1024)
        da = rowwise(_f_swiglu_bwd, [(sv["a"], ff, 0), (sv["a"], ff, 1), dm], [], [((ff, ff), BF16)],
                     name=f"swiglu_bwd_{tag}")[0]
        g_in, arrived_dw = mm_hosting(sv["h"], da, comm_dw, trans_a=True, name=f"ffn_in_dw_{tag}",
                                      out_dtype=BF16)
        comm_dx = comm_dx_of(g_in, g_out) if comm_dx_of is not None else None
        dh_, arrived_dx = mm_hosting(da, lw[l]["win"][j], comm_dx, trans_b=True, name=f"ffn_in_dx_{tag}")
        dxn, d_a, d_sh = rowwise(_f_pre_bwd, [dxo, dh_, sv["x"]], [vec_a(l, i)], [((d,), F32)], [d, d],
                                 name=f"pre_bwd_{tag}")
        norm_grads(l, i, d_a, d_sh, d_bv)
        return dxn, g_in, g_out, arrived_dw, arrived_dx

    def mixer_bwd(dxo, sv, l, tag, comm):
        s5 = ssm[l]
        proj = sv["proj"]
        dy, d_bv = rowwise(_f_post_bwd, [dxo, sv["y"]], [vec_b(l, 1)], [((d,), BF16)], [d],
                           name=f"post_bwd_{tag}")
        dmg = mm(dy, lw[l]["mo"], trans_b=True, name=f"mix_out_dx_{tag}", out_dtype=BF16)
        g_mo = mm(sv["mg"], dy, trans_a=True, name=f"mix_out_dw_{tag}", out_dtype=BF16)
        dz, dyb, dproj = rowwise(
            _f_merge_bwd, [dmg, (sv["z"], d, 0), (sv["z"], d, 1), sv["yb"], (proj, d, 0), (proj, d, 1)], [],
            [((d, d), BF16), ((d,), BF16), ((d, d), BF16, (iwp, 0))], name=f"merge_bwd_{tag}")
        dge = mm(dz, lw[l]["glu"], trans_b=True, name=f"glu_dx_{tag}")
        g_glu = mm(sv["ge"], dz, trans_a=True, name=f"glu_dw_{tag}", out_dtype=BF16)
        dys, d_dvec = rowwise(_f_gelu_bwd, [dge, sv["ys"], (proj, sw, off_u // sw)], [], [((sw,), BF16)],
                              [sw], name=f"gelu_bwd_{tag}")
        gadj = mm_blockdiag(dys, s5["ct"], name=f"ssm_dy_{tag}")
        adj, dlam8 = ssm_scan(gadj, s5["tab_r"], reverse=True, s_prev=sv["st"], name=f"ssm_scan_bwd_{tag}")
        du0 = mm_blockdiag(adj, s5["bt"], name=f"ssm_du_{tag}")
        d_bmat = mm_blockdiag_tn(proj, adj, g_n=nc, ka=LANES, kb=2 * hw, a_cb0=off_u // LANES,
                                 name=f"ssm_db_{tag}")
        d_cmat = mm_blockdiag_tn(sv["st"], dys, g_n=nc, ka=2 * hw, kb=LANES, name=f"ssm_dc_{tag}")
        dproj = rowwise(_f_du_fin, [du0, dys], [s5["dvec"]], [((sw,), BF16, (iwp, off_u // sw), dproj)],
                        name=f"ssm_du_fin_{tag}")[0]
        dlam = jnp.sum(dlam8, axis=1)
        dlam_re, dlam_im = dlam[:, :hw].reshape(g_n, p_n), dlam[:, hw:].reshape(g_n, p_n)
        dbb_re = _blockdiag_take(d_bmat[:, :, :hw], g_n).transpose(0, 2, 1)
        dbb_im = _blockdiag_take(d_bmat[:, :, hw:], g_n).transpose(0, 2, 1)
        mag = jnp.exp(s5["e"])
        lr, li = mag * jnp.cos(s5["ph"]), mag * jnp.sin(s5["ph"])
        d_e = dlam_re * lr + dlam_im * li
        d_ph = -dlam_re * li + dlam_im * lr
        da_re, da_im, dlog_dt, db_re, db_im = s5["vjp"]((d_e, d_ph, dbb_re, dbb_im))
        small_g[l].update(
            ssm_a_re=da_re, ssm_a_im=da_im, ssm_log_dt=dlog_dt, ssm_b_re=db_re, ssm_b_im=db_im,
            ssm_c_re=_blockdiag_take(d_cmat[:, :hw, :], g_n).transpose(0, 2, 1),
            ssm_c_im=-_blockdiag_take(d_cmat[:, hw:, :], g_n).transpose(0, 2, 1),
            ssm_d=d_dvec.reshape(sw))
        dattn = mm(dyb, lw[l]["ao"], trans_b=True, name=f"attn_out_dx_{tag}", out_dtype=BF16)
        g_ao = mm(sv["attn"], dyb, trans_a=True, name=f"attn_out_dw_{tag}", out_dtype=BF16)
        dq, dk, dv, dcum, *arrived = attn_bwd(sv["q"], sv["k"], sv["v"], heads_first(dattn), sv["o"], sv["lse"],
                                              sv["cum"], scale=scale, tag=tag, comm=comm)
        df, dfb = cum_bwd(dcum, sv["projf"], fb_pad[l:l + 1], col_block=0, name=f"cum_bwd_{tag}")
        small_g[l]["forget_b"] = dfb[0, :heads]
        for piece, off in ((heads_last(dq), off_q), (heads_last(dk), off_q + aw), (heads_last(dv), off_q + 2 * aw),
                           (df, off_f)):
            dproj = lax.dynamic_update_slice(dproj, piece.astype(BF16), (0, off))
        g_mi = mm(sv["h"], dproj, trans_a=True, name=f"mix_in_dw_{tag}", out_dtype=BF16)
        dh_ = mm(dproj, lw[l]["wmi"], trans_b=True, name=f"mix_in_dx_{tag}")
        dxn, d_a, d_sh = rowwise(_f_pre_bwd, [dxo, dh_, sv["x"]], [vec_a(l, 1)], [((d,), F32)], [d, d],
                                 name=f"pre_bwd_{tag}")
        norm_grads(l, 1, d_a, d_sh, d_bv)
        g_mi = jnp.concatenate([g_mi[:, off_u:off_f + heads], g_mi[:, :off_u]], axis=1)
        return dxn, g_mi, g_glu, g_ao, g_mo, arrived

    def split_last(a):
        return jnp.moveaxis(a.reshape(a.shape[:-1] + (N_DEV, a.shape[-1] // N_DEV)), -2, 0)

    def split_rows(a):
        return jnp.moveaxis(a.reshape(a.shape[:-2] + (N_DEV, a.shape[-2] // N_DEV, a.shape[-1])), -3, 0)

    def owner_blocks(l, pieces):
        out = []
        for name, j in pieces:
            if name == "ffn_w_in":
                out.append(split_last(g_ffn_in[l][j]))
            elif name == "ffn_w_out":
                out.append(split_rows(g_ffn_out[l][j]))
            elif name == "mix_w_out":
                out.append(split_rows(grads[name][l]))
            else:
                out.append(split_last(grads[name][l]))
        return out

    g_ffn_in = [[None, None] for _ in range(depth)]
    g_ffn_out = [[None, None] for _ in range(depth)]
    parts = {}

    def record(l, pieces, arrived):
        for p, a in zip(pieces, arrived):
            parts[(p, l)] = a

    def last_ffn_blocks(g_in, g_out):
        return exchange_comm([split_last(g_in), split_rows(g_out)])

    for l in reversed(range(depth)):
        s0, s1, s2 = saved[l]
        dx, g_ffn_in[l][1], g_ffn_out[l][1], _, _ = ffn_bwd(dx, s2, l, 2, 1, f"l{l}b")
        pending = exchange_comm([owner_blocks(ll, [p])[0] for ll, p in riders(l)])
        (dx, grads["mix_w_in"][l], grads["glu_w"][l], grads["attn_w_out"][l], grads["mix_w_out"][l],
         arrived) = mixer_bwd(dx, s1, l, f"l{l}m", pending)
        for (ll, p), got in zip(riders(l), arrived):
            record(ll, [p], [got])
        if l == 0:
            dx, g_ffn_in[l][0], g_ffn_out[l][0], arrived_mix, arrived_ffn1 = ffn_bwd(
                dx, s0, l, 0, 0, f"l{l}a", exchange_comm(owner_blocks(0, mix)), last_ffn_blocks)
            record(0, mix, arrived_mix)
            record(0, ffn1, arrived_ffn1)
        else:
            dx, g_ffn_in[l][0], g_ffn_out[l][0], _, _ = ffn_bwd(dx, s0, l, 0, 0, f"l{l}a")
    grad_x = dx.reshape(x.shape)

    small_names = ["forget_b", "ssm_a_re", "ssm_a_im", "ssm_log_dt", "ssm_b_re", "ssm_b_im", "ssm_c_re",
                   "ssm_c_im", "ssm_d"]
    pieces = [loss_part.reshape(1), jnp.stack([jnp.stack(dmod[l]) for l in range(depth)]).reshape(-1),
              jnp.stack([jnp.stack(dgpre[l]) for l in range(depth)]).reshape(-1),
              jnp.stack([jnp.stack(dgpost[l]) for l in range(depth)]).reshape(-1)]
    pieces += [jnp.stack([small_g[l][k] for l in range(depth)]).reshape(-1) for k in small_names]
    sizes = [p.size for p in pieces]
    chunk = SUBLANES * 1024
    total = -(-sum(sizes) // chunk) * chunk
    pack = jnp.pad(jnp.concatenate(pieces), (0, total - sum(sizes))).reshape(total // 1024, 1024)
    pack_all = all_gather([pack], name="gather_small_grads")[0]
    pack_sum = rowwise(_f_sum_parts, [(pack_all, p) for p in range(N_DEV)], [], [((1024,), F32)],
                       name="sum_small_grads")[0].reshape(-1)
    offs = [0]
    for s_ in sizes:
        offs.append(offs[-1] + s_)
    take = lambda i: pack_sum[offs[i]:offs[i + 1]]
    loss = take(0).reshape(())
    g_small = {"mod_b": take(1).reshape(mod_b.shape)}
    g_pre_full, g_post_full = take(2).reshape(depth, 3, d), take(3).reshape(depth, 3, d)
    shard = norm_pre.shape[-1]
    g_small["norm_pre"] = lax.dynamic_slice_in_dim(g_pre_full, me * shard, shard, axis=2)
    g_small["norm_post"] = lax.dynamic_slice_in_dim(g_post_full, me * shard, shard, axis=2)
    for i, k in enumerate(small_names):
        g_small[k] = take(4 + i).reshape(w_in[k].shape)

    dmod_all = pack_all.reshape(N_DEV, -1)[:, offs[1]:offs[2]].reshape(N_DEV, depth, 9 * d)
    dmod_mine = lax.dynamic_slice_in_dim(dmod_all, me * mod_cols, mod_cols, axis=2)
    sct_pad = jnp.pad(sc_all.T, ((0, 0), (0, LANES - N_DEV)))
    g_mod_w = jnp.stack([
        mm(sct_pad, jnp.pad(dmod_mine[:, l], ((0, LANES - N_DEV), (0, 0))), name=f"mod_dw{l}")
        for l in range(depth)])

    out_g, out_d, out_m, out_v = {}, {}, {}, {}
    flat = lambda a: a.reshape(-1, a.shape[-1])
    res = adamw(g_mod_w.reshape(1, -1, mod_cols), flat(mod_w), flat(m_mod_w), flat(v_mod_w), name="adamw_mod_w")
    out_g["mod_w"], out_d["mod_w"], out_m["mod_w"], out_v["mod_w"] = [r.reshape(mod_w.shape) for r in res]
    updated = {}
    for l in range(depth):
        for p in ffn1 + mix + ffn2:
            w_p = piece_of(w_in, p, l)
            res = adamw(parts[(p, l)], flat(w_p), flat(piece_of(m_in, p, l)), flat(piece_of(v_in, p, l)),
                        name=f"adamw_{p[0]}_l{l}" + ("" if p[1] is None else f"_{p[1]}"))
            updated[(p, l)] = [r.reshape(w_p.shape) for r in res]
    for k in big:
        for q_, dct in enumerate((out_g, out_d, out_m, out_v)):
            if k.startswith("ffn"):
                dct[k] = jnp.stack([jnp.stack([updated[((k, j), l)][q_] for j in range(2)])
                                    for l in range(depth)])
            else:
                dct[k] = jnp.stack([updated[((k, None), l)][q_] for l in range(depth)])
    small_all = ["mod_b", "norm_pre", "norm_post"] + small_names

    def pack_small(dct):
        flat = jnp.concatenate([dct[k].reshape(-1) for k in small_all])
        tot = -(-flat.size // chunk) * chunk
        return jnp.pad(flat, (0, tot - flat.size)).reshape(tot // 1024, 1024)

    res = adamw(pack_small(g_small)[None], pack_small(w_in), pack_small(m_in), pack_small(v_in),
                name="adamw_small")
    pos = 0
    for k in small_all:
        size = w_in[k].size
        for dct, r in zip((out_g, out_d, out_m, out_v), res):
            dct[k] = r.reshape(-1)[pos:pos + size].reshape(w_in[k].shape)
        pos += size

    return (loss, grad_x, *[out_g[k] for k in names], *[out_d[k] for k in names],
            *[out_m[k] for k in names], *[out_v[k] for k in names])
```

```python
import functools
import math

import jax
import jax.numpy as jnp
from jax import lax
from jax.experimental import pallas as pl
from jax.experimental.pallas import tpu as pltpu

F32 = jnp.float32
BF16 = jnp.bfloat16
MESH = pl.DeviceIdType.MESH
N_DEV = 8
LANES = 128
SUBLANES = 8
VMEM_LIMIT = 48 * 1024 * 1024

RMS_EPS = 1e-6
FFN_RES = 0.5
ADAM_LR = 0.001
ADAM_B1 = 0.9
ADAM_B2 = 0.999
ADAM_EPS = 1e-08
ADAM_WD = 0.01
ADAM_STEP = 10
GELU_C = math.sqrt(2.0 / math.pi)
GELU_A = 0.044715


def _pick(dim, target, mult=LANES):
    t = (min(dim, target) // mult) * mult
    while t >= mult:
        if dim % t == 0:
            return t
        t -= mult
    return dim


def _params(sem):
    return pltpu.CompilerParams(dimension_semantics=sem, vmem_limit_bytes=VMEM_LIMIT)


def _sigmoid(x):
    return 1.0 / (1.0 + jnp.exp(-x))


def mm(a, b, *, name, trans_a=False, trans_b=False, out_dtype=F32, tm=1024, tn=1408, tk=2816, comm=None):
    if trans_a:
        kdim, m = a.shape
    else:
        m, kdim = a.shape
    if trans_b:
        n, kb = b.shape
    else:
        kb, n = b.shape
    assert kdim == kb, (a.shape, b.shape)
    tm, tn, tk = _pick(m, tm), _pick(n, tn), _pick(kdim, tk)
    gm, gn, nk = m // tm, n // tn, kdim // tk
    dims = (((0 if trans_a else 1,), (1 if trans_b else 0,)), ((), ()))
    host = _Hosted(comm, 2, 1, 1 if nk > 1 else 0)

    def body(*refs):
        (a_ref, b_ref, o_ref, *acc), crefs = host.split(refs)
        i, j, k = pl.program_id(0), pl.program_id(1), pl.program_id(2)
        host.phase("start", (i == 0) & (j == 0) & (k == 0), crefs)
        prod = lax.dot_general(a_ref[...].astype(BF16), b_ref[...].astype(BF16), dims,
                               preferred_element_type=F32)
        if nk == 1:
            o_ref[...] = prod.astype(out_dtype)
        else:
            acc_ref, = acc

            @pl.when(k == 0)
            def _():
                acc_ref[...] = prod

            @pl.when((k > 0) & (k < nk - 1))
            def _():
                acc_ref[...] += prod

            @pl.when(k == nk - 1)
            def _():
                o_ref[...] = (acc_ref[...] + prod).astype(out_dtype)

        last = (i == gm - 1) & (j == gn - 1) & (k == nk - 1)
        host.phase("mid", last, crefs)
        host.phase("finish", last, crefs)

    a_spec = (pl.BlockSpec((tk, tm), lambda i, j, k: (k, i)) if trans_a
              else pl.BlockSpec((tm, tk), lambda i, j, k: (i, k)))
    b_spec = (pl.BlockSpec((tn, tk), lambda i, j, k: (j, k)) if trans_b
              else pl.BlockSpec((tk, tn), lambda i, j, k: (k, j)))
    in_specs, out_specs, out_shape, scratch, extra = host.specs(
        [a_spec, b_spec], [pl.BlockSpec((tm, tn), lambda i, j, k: (i, j))],
        [jax.ShapeDtypeStruct((m, n), out_dtype)], [pltpu.VMEM((tm, tn), F32)] if nk > 1 else [])
    res = pl.pallas_call(
        body, name=name, grid=(gm, gn, nk),
        in_specs=in_specs, out_specs=out_specs, out_shape=out_shape, scratch_shapes=scratch,
        compiler_params=_params(("arbitrary", "arbitrary", "arbitrary")),
    )(a, b, *extra)
    return res if comm else res[0]


def mm_blockdiag(a, b, *, name, a_cb0=0, out_dtype=F32, tm=512):
    m = a.shape[0]
    g_n, ka, nb = b.shape
    tm = _pick(m, tm)
    assert a_cb0 % g_n == 0

    def body(a_ref, b_ref, o_ref):
        for g in range(g_n):
            o_ref[:, g * nb:(g + 1) * nb] = jnp.dot(
                a_ref[:, g * ka:(g + 1) * ka].astype(BF16), b_ref[g].astype(BF16),
                preferred_element_type=F32).astype(out_dtype)

    return pl.pallas_call(
        body, name=name, grid=(m // tm,),
        in_specs=[pl.BlockSpec((tm, g_n * ka), lambda i: (i, a_cb0 // g_n)),
                  pl.BlockSpec((g_n, ka, nb), lambda i: (0, 0, 0))],
        out_specs=pl.BlockSpec((tm, g_n * nb), lambda i: (i, 0)),
        out_shape=jax.ShapeDtypeStruct((m, g_n * nb), out_dtype),
        compiler_params=_params(("parallel",)),
    )(a, b)


def mm_blockdiag_tn(a, b, *, name, g_n, ka, kb, a_cb0=0, b_cb0=0, tk=512):
    rows = a.shape[0]
    tk = _pick(rows, tk)
    nk = rows // tk
    assert a_cb0 % g_n == 0 and b_cb0 % g_n == 0

    def body(a_ref, b_ref, o_ref):
        @pl.when(pl.program_id(0) == 0)
        def _():
            o_ref[...] = jnp.zeros_like(o_ref)

        for g in range(g_n):
            o_ref[g] += lax.dot_general(a_ref[:, g * ka:(g + 1) * ka].astype(BF16),
                                        b_ref[:, g * kb:(g + 1) * kb].astype(BF16),
                                        (((0,), (0,)), ((), ())), preferred_element_type=F32)

    return pl.pallas_call(
        body, name=name, grid=(nk,),
        in_specs=[pl.BlockSpec((tk, g_n * ka), lambda k: (k, a_cb0 // g_n)),
                  pl.BlockSpec((tk, g_n * kb), lambda k: (k, b_cb0 // g_n))],
        out_specs=pl.BlockSpec((g_n, ka, kb), lambda k: (0, 0, 0)),
        out_shape=jax.ShapeDtypeStruct((g_n, ka, kb), F32),
        compiler_params=_params(("arbitrary",)),
    )(a, b)


def rowwise(fn, rows, vecs, outs, reds=(), *, name, tm=512):
    metas = []
    for r in rows:
        if isinstance(r, tuple) and len(r) == 3:
            metas.append(("col", r[0], r[1], r[2]))
        elif isinstance(r, tuple):
            metas.append(("lead", r[0], r[0].shape[2], r[1]))
        else:
            metas.append(("full", r, r.shape[1], 0))
    n_rows = metas[0][1].shape[1] if metas[0][0] == "lead" else metas[0][1].shape[0]
    rc = 16 if n_rows % 16 == 0 else (SUBLANES if n_rows % SUBLANES == 0 else n_rows)
    tm = _pick(n_rows, tm, rc)
    n_inner = tm // rc
    windows = [(o[2] if len(o) > 2 else None) for o in outs]
    bases = [(k, o[3]) for k, o in enumerate(outs) if len(o) > 3 and o[3] is not None]
    outs = [(o[0], o[1]) for o in outs]
    nr, nv, no, nbase = len(metas), len(vecs), len(outs), len(bases)

    def body(*refs):
        row_refs, vec_refs = refs[:nr], refs[nr:nr + nv]
        refs = refs[nr + nv + nbase:]
        out_refs, red_refs = refs[:no], refs[no:]
        if reds:
            @pl.when(pl.program_id(0) == 0)
            def _():
                for rr in red_refs:
                    rr[...] = jnp.zeros_like(rr)
        vec_vals = [v[...] for v in vec_refs]

        def step(s, carry):
            r0 = pl.multiple_of(s * rc, rc)
            vals = [ref[pl.ds(r0, rc), :] for ref in row_refs]
            res = fn(*vals, *vec_vals)
            if not isinstance(res, (tuple, list)):
                res = (res,)
            for o_ref, (widths, dt), val in zip(out_refs, outs, res[:no]):
                pieces = val if isinstance(val, (tuple, list)) else (val,)
                off = 0
                for w_, piece in zip(widths, pieces):
                    o_ref[pl.ds(r0, rc), off:off + w_] = piece.astype(dt)
                    off += w_
            for rr, val in zip(red_refs, res[no:]):
                rr[...] += val
            return carry

        lax.fori_loop(0, n_inner, step, 0, unroll=min(n_inner, 4))

    in_specs = []
    for kind, arr, w_, idx in metas:
        if kind == "col":
            in_specs.append(pl.BlockSpec((tm, w_), functools.partial(lambda i, cb: (i, cb), cb=idx)))
        elif kind == "lead":
            in_specs.append(pl.BlockSpec((None, tm, w_), functools.partial(lambda i, p: (p, i, 0), p=idx)))
        else:
            in_specs.append(pl.BlockSpec((tm, w_), lambda i: (i, 0)))
    for v in vecs:
        in_specs.append(pl.BlockSpec(v.shape, lambda i: (0, 0)))
    in_specs += [pl.BlockSpec(memory_space=pl.ANY)] * nbase
    out_specs, out_shape = [], []
    for (ws, dt), win in zip(outs, windows):
        total, cb = win if win is not None else (sum(ws), 0)
        out_specs.append(pl.BlockSpec((tm, sum(ws)), functools.partial(lambda i, cb: (i, cb), cb=cb)))
        out_shape.append(jax.ShapeDtypeStruct((n_rows, total), dt))
    out_specs += [pl.BlockSpec((1, w_), lambda i: (0, 0)) for w_ in reds]
    out_shape += [jax.ShapeDtypeStruct((1, w_), F32) for w_ in reds]
    res = pl.pallas_call(
        body, name=name, grid=(n_rows // tm,),
        in_specs=in_specs, out_specs=out_specs, out_shape=out_shape,
        input_output_aliases={nr + nv + b: k for b, (k, _) in enumerate(bases)},
        compiler_params=_params(("arbitrary",)),
    )(*[m[1] for m in metas], *vecs, *[b for _, b in bases])
    return res


def _rms(x):
    return lax.rsqrt(jnp.mean(x * x, axis=-1, keepdims=True) + RMS_EPS)


def _colsum(x):
    return jnp.sum(x, axis=0, keepdims=True)


def _f_silu(c):
    return c * _sigmoid(c)


def _f_pre(x, a, sh):
    return (x * _rms(x)) * a + sh


def _f_post_add(x, y, bv):
    return x + (y * _rms(y)) * bv


def _f_post_bwd(dxo, y, bv):
    ry = _rms(y)
    yn = y * ry
    dyn = dxo * bv
    dy = ry * (dyn - yn * jnp.mean(dyn * yn, axis=-1, keepdims=True))
    return dy, _colsum(dxo * yn)


def _f_pre_bwd(dxo, dh, x, a):
    r = _rms(x)
    xn = x * r
    dxn = dh * a
    dx = dxo + r * (dxn - xn * jnp.mean(dxn * xn, axis=-1, keepdims=True))
    return dx, _colsum(dh * xn), _colsum(dh)


def _f_swiglu(g, u):
    g = g.astype(F32)
    return (g * _sigmoid(g)) * u.astype(F32)


def _f_swiglu_bwd(g, u, dm):
    g, u, dm = g.astype(F32), u.astype(F32), dm.astype(F32)
    sg = _sigmoid(g)
    dg = dm * u * (sg * (1.0 + g * (1.0 - sg)))
    du = dm * (g * sg)
    return ((dg, du),)


def _gelu_t(x):
    return jnp.tanh(GELU_C * (x + GELU_A * x * x * x))


def _f_gelu_in(y0, u, dvec):
    y = y0 + dvec * u
    return 0.5 * y * (1.0 + _gelu_t(y)), y


def _f_gelu_bwd(dge, y, u):
    t = _gelu_t(y)
    dy = dge * (0.5 * (1.0 + t) + 0.5 * y * (1.0 - t * t) * GELU_C * (1.0 + 3.0 * GELU_A * y * y))
    return dy, _colsum(dy * u)


def _f_du_fin(du0, dys, dvec):
    return du0 + dvec * dys.astype(F32)


def _f_merge(zv, zg, yb, ga, gb):
    zv, zg, yb, ga, gb = [a.astype(F32) for a in (zv, zg, yb, ga, gb)]
    return _sigmoid(ga) * (zv * _sigmoid(zg)) + _sigmoid(gb) * yb


def _f_merge_bwd(dmg, zv, zg, yb, ga, gb):
    zv, zg, yb, ga, gb = [a.astype(F32) for a in (zv, zg, yb, ga, gb)]
    sa, sb, sz = _sigmoid(ga), _sigmoid(gb), _sigmoid(zg)
    ya = zv * sz
    dya = dmg * sa
    dga = dmg * ya * sa * (1.0 - sa)
    dyb = dmg * sb
    dgb = dmg * yb * sb * (1.0 - sb)
    dzv = dya * sz
    dzg = dya * zv * sz * (1.0 - sz)
    return (dzv, dzg), dyb, (dga, dgb)


def _f_sum_parts(*parts):
    acc = parts[0].astype(F32)
    for p in parts[1:]:
        acc = acc + p.astype(F32)
    return acc


def _f_adamw(*args):
    parts, (w, m, v) = args[:-3], args[-3:]
    g = _f_sum_parts(*parts)
    m = ADAM_B1 * m + (1.0 - ADAM_B1) * g
    v = ADAM_B2 * v + (1.0 - ADAM_B2) * (g * g)
    m_hat = m / (1.0 - ADAM_B1 ** ADAM_STEP)
    v_hat = v / (1.0 - ADAM_B2 ** ADAM_STEP)
    delta = -ADAM_LR * (m_hat / (jnp.sqrt(v_hat) + ADAM_EPS) + ADAM_WD * w)
    return g, delta, m, v


def adamw(parts3, w, m, v, *, name):
    c = w.shape[1]
    rows = [(parts3, p) for p in range(parts3.shape[0])] + [w, m, v]
    return rowwise(_f_adamw, rows, [], [((c,), F32)] * 4, name=name)


def _tri_dot(tri, x, dims=(((1,), (0,)), ((), ()))):
    x1 = x.astype(BF16)
    r1 = x - x1.astype(F32)
    x2 = r1.astype(BF16)
    x3 = (r1 - x2.astype(F32)).astype(BF16)
    dot = functools.partial(lax.dot_general, dimension_numbers=dims, preferred_element_type=F32)
    return dot(tri, x1) + dot(tri, x2) + dot(tri, x3)


def cum_fwd(proj, fb, *, col_block, name, t=256):
    n = proj.shape[0]
    t = _pick(n, t, SUBLANES)

    def body(f_ref, fb_ref, cum_ref, car_ref):
        @pl.when(pl.program_id(0) == 0)
        def _():
            car_ref[...] = jnp.zeros_like(car_ref)

        x = f_ref[...] + fb_ref[...]
        lf = jnp.minimum(x, 0.0) - jnp.log(1.0 + jnp.exp(-jnp.abs(x)))
        r = lax.broadcasted_iota(jnp.int32, (t, t), 0)
        c = lax.broadcasted_iota(jnp.int32, (t, t), 1)
        cs = _tri_dot((c <= r).astype(BF16), lf) + car_ref[0:1, :]
        cum_ref[...] = cs
        car_ref[0:1, :] = cs[t - 1:t, :]

    return pl.pallas_call(
        body, name=name, grid=(n // t,),
        in_specs=[pl.BlockSpec((t, LANES), lambda i: (i, col_block)),
                  pl.BlockSpec((1, LANES), lambda i: (0, 0))],
        out_specs=pl.BlockSpec((t, LANES), lambda i: (i, 0)),
        out_shape=jax.ShapeDtypeStruct((n, LANES), F32),
        scratch_shapes=[pltpu.VMEM((SUBLANES, LANES), F32)],
        compiler_params=_params(("arbitrary",)),
    )(proj, fb)


def cum_bwd(dcum, proj, fb, *, col_block, name, t=256):
    n = proj.shape[0]
    slabs = dcum.shape[0]
    t = _pick(n, t, SUBLANES)
    nb = n // t

    def body(dc_ref, f_ref, fb_ref, df_ref, dfb_ref, car_ref):
        @pl.when(pl.program_id(0) == 0)
        def _():
            car_ref[...] = jnp.zeros_like(car_ref)
            dfb_ref[...] = jnp.zeros_like(dfb_ref)

        r = lax.broadcasted_iota(jnp.int32, (t, t), 0)
        c = lax.broadcasted_iota(jnp.int32, (t, t), 1)
        dl = _tri_dot((c >= r).astype(BF16), jnp.sum(dc_ref[...], axis=0)) + car_ref[0:1, :]
        car_ref[0:1, :] = dl[0:1, :]
        x = f_ref[...] + fb_ref[...]
        df = dl * (1.0 / (1.0 + jnp.exp(x)))
        df_ref[...] = df
        dfb_ref[...] += _colsum(df)

    return pl.pallas_call(
        body, name=name, grid=(nb,),
        in_specs=[pl.BlockSpec((slabs, t, LANES), lambda i: (0, nb - 1 - i, 0)),
                  pl.BlockSpec((t, LANES), lambda i: (nb - 1 - i, col_block)),
                  pl.BlockSpec((1, LANES), lambda i: (0, 0))],
        out_specs=[pl.BlockSpec((t, LANES), lambda i: (nb - 1 - i, 0)),
                   pl.BlockSpec((1, LANES), lambda i: (0, 0))],
        out_shape=[jax.ShapeDtypeStruct((n, LANES), F32), jax.ShapeDtypeStruct((1, LANES), F32)],
        scratch_shapes=[pltpu.VMEM((SUBLANES, LANES), F32)],
        compiler_params=_params(("arbitrary",)),
    )(dcum, proj, fb)


_NT =(((1,), (1,)), ((), ()))
_TN = (((0,), (0,)), ((), ()))


def _lane_sums_as_row(x):
    return _tri_dot(jnp.ones((SUBLANES, x.shape[1]), BF16), x, _NT)[0:1, :]


def _causal_keep(t):
    return lax.broadcasted_iota(jnp.int32, (t, t), 1) <= lax.broadcasted_iota(jnp.int32, (t, t), 0)


def attn_fwd(q, k, v, crow, *, name, t=512, hb=8, comm=None):
    h_n, n, dh = q.shape
    hb = min(hb, h_n)
    t = _pick(n, t)
    nb = n // t
    ng = h_n // hb
    host = _Hosted(comm, 4, 2, 3)

    def body(*refs):
        (q_ref, k_ref, v_ref, cr_ref, o_ref, lse_ref, m_sc, l_sc, acc_sc), crefs = host.split(refs)
        g, i, j = pl.program_id(0), pl.program_id(1), pl.program_id(2)
        host.phase("start", (g == 0) & (i == 0) & (j == 0), crefs)
        host.phase("mid", (g == ng - 1) & (i == (3 * nb) // 4) & (j == 0), crefs)

        @pl.when(j == 0)
        def _():
            m_sc[...] = jnp.full_like(m_sc, -jnp.inf)
            l_sc[...] = jnp.zeros_like(l_sc)
            acc_sc[...] = jnp.zeros_like(acc_sc)

        def update(diagonal):
            keep = _causal_keep(t) if diagonal else None
            heads = range(hb)
            ss = [lax.dot_general(q_ref[h], k_ref[h], _NT, preferred_element_type=F32) for h in heads]
            pairs, alphas = [], []
            for h in heads:
                s = ss[h] - cr_ref[h]
                if diagonal:
                    s = jnp.where(keep, s, -jnp.inf)
                m_prev = m_sc[h]
                m_new = jnp.maximum(m_prev, jnp.max(s, axis=-1, keepdims=True))
                p = jnp.exp(s - m_new)
                alpha = jnp.exp(m_prev - m_new)
                l_sc[h] = alpha * l_sc[h] + jnp.sum(p, axis=-1, keepdims=True)
                m_sc[h] = m_new
                p_hi = p.astype(BF16)
                pairs.append((p_hi, (p - p_hi.astype(F32)).astype(BF16)))
                alphas.append(alpha)
            for h in heads:
                vv = v_ref[h]
                acc_sc[h] = (alphas[h] * acc_sc[h] + jnp.dot(pairs[h][0], vv, preferred_element_type=F32)
                             + jnp.dot(pairs[h][1], vv, preferred_element_type=F32))

        @pl.when(j < i)
        def _():
            update(False)

        @pl.when(j == i)
        def _():
            update(True)

        @pl.when(j == nb - 1)
        def _():
            o_ref[...] = acc_sc[...] / l_sc[...]
            lane0 = lax.broadcasted_iota(jnp.int32, (t, LANES), 1) == 0
            for h in range(hb):
                lse_col = m_sc[h] + jnp.log(l_sc[h])
                lse_ref[h] = _lane_sums_as_row(jnp.where(lane0, lse_col, 0.0))

        host.phase("finish", (g == ng - 1) & (i == nb - 1) & (j == nb - 1), crefs)

    qspec = pl.BlockSpec((hb, t, dh), lambda g, i, j: (g, i, 0))
    kspec = pl.BlockSpec((hb, t, dh), lambda g, i, j: (g, jnp.minimum(j, i), 0))
    in_specs, out_specs, out_shape, scratch, extra = host.specs(
        [qspec, kspec, kspec, pl.BlockSpec((hb, 1, t), lambda g, i, j: (g, 0, jnp.minimum(j, i)))],
        [qspec, pl.BlockSpec((hb, 1, t), lambda g, i, j: (g, 0, i))],
        [jax.ShapeDtypeStruct((h_n, n, dh), F32), jax.ShapeDtypeStruct((h_n, 1, n), F32)],
        [pltpu.VMEM((hb, t, 1), F32), pltpu.VMEM((hb, t, 1), F32), pltpu.VMEM((hb, t, dh), F32)])
    return pl.pallas_call(
        body, name=name, grid=(ng, nb, nb),
        in_specs=in_specs, out_specs=out_specs, out_shape=out_shape, scratch_shapes=scratch,
        compiler_params=_params(("arbitrary", "arbitrary", "arbitrary")),
    )(q, k, v, crow, *extra)


def attn_delta(do, o, *, name, t=512, hb=8):
    h_n, n, dh = do.shape
    hb = min(hb, h_n)
    t = _pick(n, t)

    def body(do_ref, o_ref, dl_ref):
        for h in range(hb):
            dl_ref[h] = _lane_sums_as_row(do_ref[h].astype(F32) * o_ref[h])

    spec = pl.BlockSpec((hb, t, dh), lambda g, i: (g, i, 0))
    return pl.pallas_call(
        body, name=name, grid=(h_n // hb, n // t),
        in_specs=[spec, spec], out_specs=pl.BlockSpec((hb, 1, t), lambda g, i: (g, 0, i)),
        out_shape=jax.ShapeDtypeStruct((h_n, 1, n), F32),
        compiler_params=_params(("parallel", "parallel")),
    )(do, o)


def attn_bwd(q, k, v, do, o, lse, cum, *, scale, tag, t=512, hb=2, comm=None):
    h_n, n, dh = q.shape
    hb = min(hb, h_n)
    t = _pick(n, t)
    nb = n // t
    dob = do.astype(BF16)
    delta = attn_delta(dob, o, name=f"attn_delta_{tag}", t=t)

    ng = h_n // hb
    host = _Hosted(comm, 7, 4, 3)

    def body(*refs):
        (q_ref, k_ref, v_ref, do_ref, lse_ref, dl_ref, cum_ref,
         dq_ref, dk_ref, dv_ref, dcum_ref, dk_acc, dv_acc, dcc_acc), crefs = host.split(refs)
        g, j, i = pl.program_id(0), pl.program_id(1), pl.program_id(2)
        host.phase("start", (g == 0) & (j == 0) & (i == 0), crefs)
        lane = lax.broadcasted_iota(jnp.int32, (t, LANES), 1)

        @pl.when((j == 0) & (i == 0))
        def _():
            dq_ref[...] = jnp.zeros_like(dq_ref)

        @pl.when(i == 0)
        def _():
            dk_acc[...] = jnp.zeros_like(dk_acc)
            dv_acc[...] = jnp.zeros_like(dv_acc)
            dcc_acc[...] = jnp.zeros_like(dcc_acc)

        def update(diagonal):
            heads = range(hb)
            r0 = pl.multiple_of(i * t, t)
            if diagonal:
                keep = lax.broadcasted_iota(jnp.int32, (t, t), 0) <= lax.broadcasted_iota(jnp.int32, (t, t), 1)
            qv, kv = [q_ref[h] for h in heads], [k_ref[h] for h in heads]
            vv, dov = [v_ref[h] for h in heads], [do_ref[h] for h in heads]
            st = [lax.dot_general(kv[h], qv[h], _NT, preferred_element_type=F32) for h in heads]
            dpt = [lax.dot_general(vv[h], dov[h], _NT, preferred_element_type=F32) for h in heads]
            pt = []
            cum_tile = cum_ref[...]
            for h in heads:
                cc = jnp.sum(jnp.where(lane == g * hb + h, cum_tile, 0.0), axis=1, keepdims=True)
                s = st[h] - cc
                if diagonal:
                    s = jnp.where(keep, s, -jnp.inf)
                pt.append(jnp.exp(s - lse_ref[h]))
            for h in heads:
                dv_acc[h] += jnp.dot(pt[h].astype(BF16), dov[h], preferred_element_type=F32)
            dsb = []
            for h in heads:
                ds = pt[h] * (dpt[h] - dl_ref[h])
                dcc_acc[h] -= jnp.sum(ds, axis=1, keepdims=True)
                dsb.append(ds.astype(BF16))
            for h in heads:
                dk_acc[h] += jnp.dot(dsb[h], qv[h], preferred_element_type=F32)
            for h in heads:
                dq_ref[h, pl.ds(r0, t), :] += lax.dot_general(dsb[h], kv[h], _TN,
                                                              preferred_element_type=F32) * scale

        @pl.when(i > j)
        def _():
            update(False)

        @pl.when(i == j)
        def _():
            update(True)

        @pl.when(i == nb - 1)
        def _():
            dk_ref[...] = dk_acc[...]
            dv_ref[...] = dv_acc[...]
            tile = jnp.zeros((t, LANES), F32)
            for h in range(hb):
                tile = tile + jnp.where(lane == g * hb + h, dcc_acc[h], 0.0)
            dcum_ref[...] = tile

        host.phase("finish", (g == ng - 1) & (j == nb - 1) & (i == nb - 1), crefs)

    qspec = pl.BlockSpec((hb, t, dh), lambda g, j, i: (g, jnp.maximum(i, j), 0))
    qrow = pl.BlockSpec((hb, 1, t), lambda g, j, i: (g, 0, jnp.maximum(i, j)))
    kspec = pl.BlockSpec((hb, t, dh), lambda g, j, i: (g, j, 0))
    in_specs, out_specs, out_shape, scratch, extra = host.specs(
        [qspec, kspec, kspec, qspec, qrow, qrow, pl.BlockSpec((t, LANES), lambda g, j, i: (j, 0))],
        [pl.BlockSpec((hb, n, dh), lambda g, j, i: (g, 0, 0)), kspec, kspec,
         pl.BlockSpec((None, t, LANES), lambda g, j, i: (g, j, 0))],
        [jax.ShapeDtypeStruct((h_n, n, dh), F32)] * 3 + [jax.ShapeDtypeStruct((ng, n, LANES), F32)],
        [pltpu.VMEM((hb, t, dh), F32), pltpu.VMEM((hb, t, dh), F32), pltpu.VMEM((hb, t, 1), F32)])
    dq, dk, dv, dcum, *arrived = pl.pallas_call(
        body, name=f"attn_bwd_{tag}", grid=(ng, nb, nb),
        in_specs=in_specs, out_specs=out_specs, out_shape=out_shape, scratch_shapes=scratch,
        compiler_params=_params(("arbitrary", "arbitrary", "arbitrary")),
    )(q, k, v, dob, lse, delta, cum, *extra)
    return [dq, dk, dv, dcum] + arrived


SCAN_STEPS = (1, 2, 4)


def ssm_scan(x, tab, *, reverse, name, s_prev=None, tt=512, out_dtype=BF16):
    n, width = x.shape
    nc, _, hw = tab.shape
    cw = 2 * hw
    assert width == nc * cw
    tt = _pick(n, tt, 2 * SUBLANES)
    nt = n // tt
    ng = tt // (2 * SUBLANES)
    with_grad = s_prev is not None

    def body(*refs):
        if with_grad:
            x_ref, s_ref, tab_ref, o_ref, g_ref, car_ref = refs
        else:
            x_ref, tab_ref, o_ref, car_ref = refs

        @pl.when(pl.program_id(1) == 0)
        def _():
            car_ref[...] = jnp.zeros_like(car_ref)
            if with_grad:
                g_ref[...] = jnp.zeros_like(g_ref)

        q_re, q_im = tab_ref[0:8, :], tab_ref[8:16, :]
        p_re = [tab_ref[16 + i:17 + i, :] for i in range(3)]
        p_im = [tab_ref[24 + i:25 + i, :] for i in range(3)]
        row = lax.broadcasted_iota(jnp.int32, (SUBLANES, hw), 0)

        def group(xr, xi, sr_, si_, carry):
            c_re, c_im = carry
            for i, d in enumerate(SCAN_STEPS):
                if reverse:
                    shift, keep = SUBLANES - d, row < SUBLANES - d
                else:
                    shift, keep = d, row >= d
                sr = jnp.where(keep, pltpu.roll(xr, shift, 0), 0.0)
                si = jnp.where(keep, pltpu.roll(xi, shift, 0), 0.0)
                xr, xi = (xr + p_re[i] * sr - p_im[i] * si,
                          xi + p_re[i] * si + p_im[i] * sr)
            xr, xi = (xr + q_re * c_re - q_im * c_im,
                      xi + q_re * c_im + q_im * c_re)
            if with_grad:
                nr = jnp.where(row < SUBLANES - 1, pltpu.roll(xr, SUBLANES - 1, 0), c_re)
                ni = jnp.where(row < SUBLANES - 1, pltpu.roll(xi, SUBLANES - 1, 0), c_im)
                g_ref[:, 0:hw] += nr * sr_ + ni * si_
                g_ref[:, hw:cw] += ni * sr_ - nr * si_
            if reverse:
                return xr, xi, (xr[0:1, :], xi[0:1, :])
            return xr, xi, (xr[SUBLANES - 1:SUBLANES, :], xi[SUBLANES - 1:SUBLANES, :])

        def pair(gi, carry):
            g = (ng - 1 - gi) if reverse else gi
            r0 = pl.multiple_of(g * 2 * SUBLANES, 2 * SUBLANES)
            rows = pl.ds(r0, 2 * SUBLANES)
            xr, xi = x_ref[rows, 0:hw].astype(F32), x_ref[rows, hw:cw].astype(F32)
            if with_grad:
                sr, si = s_ref[rows, 0:hw].astype(F32), s_ref[rows, hw:cw].astype(F32)
            halves = [slice(0, SUBLANES), slice(SUBLANES, 2 * SUBLANES)]
            done = [None, None]
            for k in ((1, 0) if reverse else (0, 1)):
                h = halves[k]
                o_re, o_im, carry = group(xr[h], xi[h], sr[h] if with_grad else None,
                                          si[h] if with_grad else None, carry)
                done[k] = (o_re, o_im)
            o_ref[rows, 0:hw] = jnp.concatenate([done[0][0], done[1][0]], axis=0).astype(o_ref.dtype)
            o_ref[rows, hw:cw] = jnp.concatenate([done[0][1], done[1][1]], axis=0).astype(o_ref.dtype)
            return carry

        c_re, c_im = lax.fori_loop(0, ng, pair, (car_ref[0:1, 0:hw], car_ref[0:1, hw:cw]),
                                   unroll=min(ng, 2))
        car_ref[0:1, 0:hw] = c_re
        car_ref[0:1, hw:cw] = c_im

    if reverse:
        xspec = pl.BlockSpec((tt, cw), lambda c, t: (nt - 1 - t, c))
    else:
        xspec = pl.BlockSpec((tt, cw), lambda c, t: (t, c))
    tspec = pl.BlockSpec((None, 32, hw), lambda c, t: (c, 0, 0))
    in_specs = [xspec, xspec, tspec] if with_grad else [xspec, tspec]
    out_specs = [xspec]
    out_shape = [jax.ShapeDtypeStruct((n, width), out_dtype)]
    if with_grad:
        out_specs.append(pl.BlockSpec((None, SUBLANES, cw), lambda c, t: (c, 0, 0)))
        out_shape.append(jax.ShapeDtypeStruct((nc, SUBLANES, cw), F32))
    operands = (x, s_prev, tab) if with_grad else (x, tab)
    return pl.pallas_call(
        body, name=name, grid=(nc, nt),
        in_specs=in_specs, out_specs=out_specs, out_shape=out_shape,
        scratch_shapes=[pltpu.VMEM((SUBLANES, cw), F32)],
        compiler_params=_params(("parallel", "arbitrary")),
    )(*operands)


def _slot(pos):
    return 4 * pos[0] + 2 * pos[1] + pos[2]


def _comm_scratch(n):
    return [pltpu.SemaphoreType.DMA((7 * n,)), pltpu.SemaphoreType.DMA((7 * n,)), pltpu.SemaphoreType.DMA((n,))]


def _gather_copies(ins, outs, sems):
    send_sems, recv_sems, local_sems = sems
    n = len(ins)
    x, y, c = lax.axis_index("x"), lax.axis_index("y"), lax.axis_index("c")
    me, sibling = (x, y, c), (x, y, 1 - c)
    chips = [(1 - x, y), (x, 1 - y), (1 - x, 1 - y)]

    def copy(t, k, block, to, src=None):
        dst = outs[t].at[_slot(block)]
        return pltpu.make_async_remote_copy(
            src_ref=dst if src is None else src, dst_ref=dst,
            send_sem=send_sems.at[7 * t + k], recv_sem=recv_sems.at[7 * t + k],
            device_id=to, device_id_type=MESH)

    jc = list(enumerate(chips))
    return dict(
        mine=[pltpu.make_async_copy(ins[t], outs[t].at[_slot(me)], local_sems.at[t]) for t in range(n)],
        first=[cp for t in range(n) for cp in
               [copy(t, 0, me, sibling, src=ins[t])] + [copy(t, 1 + j, me, (*chip, c), src=ins[t]) for j, chip in jc]],
        arrive=[copy(t, 1 + j, (*chip, c), me) for t in range(n) for j, chip in jc],
        passed=[copy(t, 4 + j, (*chip, c), sibling) for t in range(n) for j, chip in jc],
        from_sibling=[cp for t in range(n) for cp in
                      [copy(t, 0, sibling, me)] + [copy(t, 4 + j, (*chip, 1 - c), me) for j, chip in jc]])


def _gather_start(ins, outs, sems):
    cps = _gather_copies(ins, outs, sems)
    for cp in cps["mine"] + cps["first"]:
        cp.start()


def _gather_forward(ins, outs, sems):
    cps = _gather_copies(ins, outs, sems)
    for arrived, onward in zip(cps["arrive"], cps["passed"]):
        arrived.wait_recv()
        onward.start()


def _gather_finish(ins, outs, sems):
    cps = _gather_copies(ins, outs, sems)
    for cp in cps["from_sibling"]:
        cp.wait_recv()
    for cp in cps["first"] + cps["passed"]:
        cp.wait_send()
    for cp in cps["mine"]:
        cp.wait()


def gather_comm(arrs):
    return dict(ins=list(arrs), out_shape=[jax.ShapeDtypeStruct((N_DEV,) + a.shape, a.dtype) for a in arrs],
                scratch=_comm_scratch(len(arrs)), start=_gather_start, mid=_gather_forward, finish=_gather_finish)


def _exchange_copies(ins, outs, sems):
    send_sems, recv_sems, local_sems = sems
    n = len(ins)
    me = (lax.axis_index("x"), lax.axis_index("y"), lax.axis_index("c"))
    peers = []
    for k in range(1, N_DEV):
        flip = ((k >> 2) & 1, (k >> 1) & 1, k & 1)
        peers.append(tuple(1 - p if f else p for p, f in zip(me, flip)))

    def copy(t, k, peer, dst_slot):
        return pltpu.make_async_remote_copy(
            src_ref=ins[t].at[_slot(peer)], dst_ref=outs[t].at[dst_slot],
            send_sem=send_sems.at[7 * t + k], recv_sem=recv_sems.at[7 * t + k],
            device_id=peer, device_id_type=MESH)

    return dict(
        mine=[pltpu.make_async_copy(ins[t].at[_slot(me)], outs[t].at[_slot(me)], local_sems.at[t])
              for t in range(n)],
        send=[copy(t, k, peer, _slot(me)) for t in range(n) for k, peer in enumerate(peers)],
        both=[copy(t, k, peer, _slot(peer)) for t in range(n) for k, peer in enumerate(peers)])


def _exchange_start(ins, outs, sems):
    cps = _exchange_copies(ins, outs, sems)
    for cp in cps["mine"] + cps["send"]:
        cp.start()


def _exchange_finish(ins, outs, sems):
    cps = _exchange_copies(ins, outs, sems)
    for cp in cps["both"]:
        cp.wait()
    for cp in cps["mine"]:
        cp.wait()


def exchange_comm(arrs):
    return dict(ins=list(arrs), out_shape=[jax.ShapeDtypeStruct(a.shape, a.dtype) for a in arrs],
                scratch=_comm_scratch(len(arrs)), start=_exchange_start, mid=None, finish=_exchange_finish)


def run_comm(comm, *, name):
    n_in, n_out = len(comm["ins"]), len(comm["out_shape"])

    def body(*refs):
        ins, outs, sems = refs[:n_in], refs[n_in:n_in + n_out], refs[n_in + n_out:]
        comm["start"](ins, outs, sems)
        if comm["mid"] is not None:
            comm["mid"](ins, outs, sems)
        comm["finish"](ins, outs, sems)

    any_spec = pl.BlockSpec(memory_space=pl.ANY)
    return pl.pallas_call(
        body, name=name, in_specs=[any_spec] * n_in, out_specs=[any_spec] * n_out,
        out_shape=comm["out_shape"], scratch_shapes=comm["scratch"],
    )(*comm["ins"])


class _Hosted:
    def __init__(self, comm, n_in, n_out, n_scratch):
        self.comm = comm
        self.n_ci = len(comm["ins"]) if comm else 0
        self.n_co = len(comm["out_shape"]) if comm else 0
        self.n_in, self.n_out, self.n_scratch = n_in, n_out, n_scratch

    def split(self, refs):
        a = self.n_in
        b = a + self.n_ci
        c = b + self.n_out
        e = c + self.n_co
        f = e + self.n_scratch
        return refs[:a] + refs[b:c] + refs[e:f], (refs[a:b], refs[c:e], refs[f:])

    def phase(self, which, when, crefs):
        fn = self.comm[which] if self.comm else None
        if fn is not None:
            pl.when(when)(lambda: fn(*crefs))

    def specs(self, in_specs, out_specs, out_shape, scratch):
        any_spec = pl.BlockSpec(memory_space=pl.ANY)
        if not self.comm:
            return in_specs, out_specs, out_shape, scratch, ()
        return (in_specs + [any_spec] * self.n_ci, out_specs + [any_spec] * self.n_co,
                out_shape + self.comm["out_shape"], scratch + self.comm["scratch"], tuple(self.comm["ins"]))


def all_gather(arrs, *, name):
    return run_comm(gather_comm(arrs), name=name)


def _discretise(a_re, a_im, log_dt, b_re, b_im):
    ar = jnp.minimum(a_re, -1e-4)
    dt = jnp.exp(log_dt)[:, None]
    e, ph = ar * dt, a_im * dt
    mag = jnp.exp(e)
    lr, li = mag * jnp.cos(ph), mag * jnp.sin(ph)
    den = ar * ar + a_im * a_im
    nr, ni = lr - 1.0, li
    cr = (nr * ar + ni * a_im) / den
    ci = (ni * ar - nr * a_im) / den
    bb_re = cr[..., None] * b_re - ci[..., None] * b_im
    bb_im = cr[..., None] * b_im + ci[..., None] * b_re
    return e, ph, bb_re, bb_im


def _lam_pow(e, ph, k, conj):
    mag = jnp.exp(k * e)
    return mag * jnp.cos(k * ph), (-1.0 if conj else 1.0) * mag * jnp.sin(k * ph)


def _scan_table(e, ph, nc, reverse):
    hw = e.size // nc
    e, ph = e.reshape(nc, 1, hw), ph.reshape(nc, 1, hw)
    j = jnp.arange(SUBLANES, dtype=F32).reshape(1, SUBLANES, 1)
    kq = (SUBLANES - j) if reverse else (j + 1.0)
    q_re, q_im = _lam_pow(e, ph, kq, reverse)
    kp = jnp.array(SCAN_STEPS + (0,) * 5, F32).reshape(1, SUBLANES, 1)
    p_re, p_im = _lam_pow(e, ph, kp, reverse)
    return jnp.concatenate([q_re, q_im, p_re, p_im], axis=1)


def _blockdiag(m, nc):
    g, a, b = m.shape
    gc = g // nc
    m = m.reshape(nc, gc, a, b)
    eye = jnp.eye(gc, dtype=m.dtype)
    return jnp.einsum("cgab,gh->cgahb", m, eye).reshape(nc, gc * a, gc * b)


def _blockdiag_take(m, g):
    nc = m.shape[0]
    gc = g // nc
    a, b = m.shape[1] // gc, m.shape[2] // gc
    m = m.reshape(nc, gc, a, gc, b)
    eye = jnp.eye(gc, dtype=m.dtype)
    return jnp.einsum("cgahb,gh->cgab", m, eye).reshape(g, a, b)


def kernel(x, c, mod_w, mod_b, norm_pre, norm_post, ffn_w_in, ffn_w_out, mix_w_in, forget_b, ssm_a_re, ssm_a_im, ssm_log_dt, ssm_b_re, ssm_b_im, ssm_c_re, ssm_c_im, ssm_d, glu_w, attn_w_out, mix_w_out, loss_target, m_mod_w, m_mod_b, m_norm_pre, m_norm_post, m_ffn_w_in, m_ffn_w_out, m_mix_w_in, m_forget_b, m_ssm_a_re, m_ssm_a_im, m_ssm_log_dt, m_ssm_b_re, m_ssm_b_im, m_ssm_c_re, m_ssm_c_im, m_ssm_d, m_glu_w, m_attn_w_out, m_mix_w_out, v_mod_w, v_mod_b, v_norm_pre, v_norm_post, v_ffn_w_in, v_ffn_w_out, v_mix_w_in, v_forget_b, v_ssm_a_re, v_ssm_a_im, v_ssm_log_dt, v_ssm_b_re, v_ssm_b_im, v_ssm_c_re, v_ssm_c_im, v_ssm_d, v_glu_w, v_attn_w_out, v_mix_w_out):
    names = ["mod_w", "mod_b", "norm_pre", "norm_post", "ffn_w_in", "ffn_w_out", "mix_w_in", "forget_b",
             "ssm_a_re", "ssm_a_im", "ssm_log_dt", "ssm_b_re", "ssm_b_im", "ssm_c_re", "ssm_c_im", "ssm_d",
             "glu_w", "attn_w_out", "mix_w_out"]
    w_in = dict(zip(names, [mod_w, mod_b, norm_pre, norm_post, ffn_w_in, ffn_w_out, mix_w_in, forget_b,
                            ssm_a_re, ssm_a_im, ssm_log_dt, ssm_b_re, ssm_b_im, ssm_c_re, ssm_c_im, ssm_d,
                            glu_w, attn_w_out, mix_w_out]))
    m_in = dict(zip(names, [m_mod_w, m_mod_b, m_norm_pre, m_norm_post, m_ffn_w_in, m_ffn_w_out, m_mix_w_in,
                            m_forget_b, m_ssm_a_re, m_ssm_a_im, m_ssm_log_dt, m_ssm_b_re, m_ssm_b_im,
                            m_ssm_c_re, m_ssm_c_im, m_ssm_d, m_glu_w, m_attn_w_out, m_mix_w_out]))
    v_in = dict(zip(names, [v_mod_w, v_mod_b, v_norm_pre, v_norm_post, v_ffn_w_in, v_ffn_w_out, v_mix_w_in,
                            v_forget_b, v_ssm_a_re, v_ssm_a_im, v_ssm_log_dt, v_ssm_b_re, v_ssm_b_im,
                            v_ssm_c_re, v_ssm_c_im, v_ssm_d, v_glu_w, v_attn_w_out, v_mix_w_out]))

    depth = mod_w.shape[0]
    n_tok, d = x.shape[1], x.shape[2]
    ff = ffn_w_out.shape[2] * N_DEV
    heads = forget_b.shape[1]
    sw = ssm_d.shape[1]
    g_n, p_n, n_n = ssm_b_re.shape[1:]
    aw = attn_w_out.shape[1]
    dh = aw // heads
    iw = mix_w_in.shape[2] * N_DEV
    nc = sw // LANES
    hw = g_n * p_n // nc
    mod_cols = mod_w.shape[2]
    scale = dh ** -0.5
    assert iw == sw + 3 * aw + heads + 2 * d and heads <= LANES
    assert math.log2(scale).is_integer(), "q is pre-scaled in bf16: exact only for a power of two"
    off_u, off_q, off_f = 2 * d, 2 * d + sw, 2 * d + sw + 3 * aw
    iwp = off_f + LANES
    assert off_u % sw == 0 and off_q % aw == 0 and off_f % LANES == 0

    me = 4 * lax.axis_index("x") + 2 * lax.axis_index("y") + lax.axis_index("c")
    x2 = x.reshape(n_tok, d)
    tgt = loss_target.reshape(n_tok, d)

    silu_c = rowwise(_f_silu, [c], [], [((d,), F32)], name="silu_c")[0]
    big = ["ffn_w_in", "ffn_w_out", "mix_w_in", "glu_w", "attn_w_out", "mix_w_out"]
    ffn1 = [("ffn_w_in", 0), ("ffn_w_out", 0)]
    mix = [("mix_w_in", None), ("glu_w", None), ("attn_w_out", None), ("mix_w_out", None)]
    ffn2 = [("ffn_w_in", 1), ("ffn_w_out", 1)]

    def riders(l):
        nxt = [(l + 1, p) for p in ffn1 + mix] if l + 1 < depth else []
        return [(l, p) for p in ffn2] + nxt

    def piece_of(dct, piece, l):
        name, j = piece
        return dct[name][l] if j is None else dct[name][l][j]

    def shards(l, pieces):
        return [piece_of(w_in, p, l).astype(BF16) for p in pieces]

    cut = [0, sw, sw + aw, sw + 2 * aw, sw + 3 * aw, sw + 3 * aw + heads, sw + 3 * aw + heads + d, iw]
    lw = [dict(win=[None, None], wout=[None, None]) for _ in range(depth)]

    def install(l, pieces, gathered):
        for (name, j), g in zip(pieces, gathered):
            if name == "ffn_w_in":
                lw[l]["win"][j] = jnp.moveaxis(g, 0, 1).reshape(d, 2 * ff)
            elif name == "ffn_w_out":
                lw[l]["wout"][j] = g.reshape(ff, d)
            elif name == "mix_w_in":
                w_mix_in = jnp.moveaxis(g, 0, 1).reshape(d, iw)
                seg = lambda i: w_mix_in[:, cut[i]:cut[i + 1]]
                lw[l]["wmi"] = jnp.concatenate([seg(5), seg(6), seg(0), seg(1), seg(2), seg(3),
                                                jnp.pad(seg(4), ((0, 0), (0, LANES - heads)))], axis=-1)
            elif name == "mix_w_out":
                lw[l]["mo"] = g.reshape(d, d)
            else:
                lw[l]["glu" if name == "glu_w" else "ao"] = jnp.moveaxis(g, 0, 1).reshape(g.shape[1], -1)

    gathered = all_gather(
        [silu_c, norm_pre.reshape(-1, norm_pre.shape[-1]), norm_post.reshape(-1, norm_post.shape[-1])]
        + shards(0, ffn1), name="gather_first")
    sc_all = gathered[0].reshape(N_DEV, d)
    gpre = jnp.moveaxis(gathered[1].reshape(N_DEV, depth, 3, -1), 0, 2).reshape(depth, 3, d)
    gpost = jnp.moveaxis(gathered[2].reshape(N_DEV, depth, 3, -1), 0, 2).reshape(depth, 3, d)
    install(0, ffn1, gathered[3:])

    sc_pad = jnp.pad(sc_all, ((0, LANES - N_DEV), (0, 0)))
    mod_part = jnp.stack([mm(sc_pad, mod_w[l], name=f"mod_fwd{l}")[:N_DEV] for l in range(depth)], axis=1)
    mod_part = mod_part + lax.dynamic_slice_in_dim(mod_b, me * mod_cols, mod_cols, axis=1)[None]
    mod_all = all_gather([mod_part], name="gather_mod")[0]
    mod_own = lax.dynamic_index_in_dim(mod_all, me, axis=1, keepdims=False)
    mod_own = mod_own.transpose(1, 0, 2).reshape(depth, 3, 3, d)
    res_w = (FFN_RES, 1.0, FFN_RES)

    def vec_a(l, i):
        return (gpre[l, i] * (1.0 + mod_own[l, i, 1])).reshape(1, d)

    def vec_sh(l, i):
        return mod_own[l, i, 0].reshape(1, d)

    def vec_b(l, i):
        return (res_w[i] * mod_own[l, i, 2] * gpost[l, i]).reshape(1, d)

    ssm = []
    for l in range(depth):
        (e, ph, bb_re, bb_im), disc_vjp = jax.vjp(_discretise,ssm_a_re[l], ssm_a_im[l], ssm_log_dt[l],
                                                  ssm_b_re[l], ssm_b_im[l])
        b_mat = jnp.concatenate([_blockdiag(bb_re.transpose(0, 2, 1), nc),
                                 _blockdiag(bb_im.transpose(0, 2, 1), nc)], axis=2)
        c_mat = jnp.concatenate([_blockdiag(ssm_c_re[l].transpose(0, 2, 1), nc),
                                 _blockdiag(-ssm_c_im[l].transpose(0, 2, 1), nc)], axis=1)
        ssm.append(dict(e=e, ph=ph, vjp=disc_vjp, b=b_mat.astype(BF16), c=c_mat.astype(BF16),
                        bt=b_mat.transpose(0, 2, 1).astype(BF16), ct=c_mat.transpose(0, 2, 1).astype(BF16),
                        tab_f=_scan_table(e, ph, nc, False), tab_r=_scan_table(e, ph, nc, True),
                        dvec=ssm_d[l].reshape(1, sw)))

    fb_pad = jnp.pad(forget_b, ((0, 0), (0, LANES - heads)))

    def heads_first(a):
        return a.reshape(n_tok, heads, dh).transpose(1, 0, 2)

    def heads_last(a):
        return a.transpose(1, 0, 2).reshape(n_tok, heads * dh)

    def mm_hosting(a, b, comm, **kw):
        if comm is None:
            return mm(a, b, **kw), []
        out, *arrived = mm(a, b, comm=comm, **kw)
        return out, arrived

    def ffn_fwd(xin, l, i, j, tag, comm=None):
        h = rowwise(_f_pre, [xin], [vec_a(l, i), vec_sh(l, i)], [((d,), BF16)], name=f"pre_{tag}")[0]
        a, arrived = mm_hosting(h, lw[l]["win"][j], comm, name=f"ffn_in_{tag}", out_dtype=BF16)
        m = rowwise(_f_swiglu, [(a, ff, 0), (a, ff, 1)], [], [((ff,), BF16)], name=f"swiglu_{tag}")[0]
        y = mm(m, lw[l]["wout"][j], name=f"ffn_out_{tag}")
        xout = rowwise(_f_post_add, [xin, y], [vec_b(l, i)], [((d,), F32)], name=f"post_{tag}")[0]
        return xout, dict(x=xin, h=h, a=a, m=m, y=y), arrived

    def mixer_fwd(xin, l, tag):
        s5 = ssm[l]
        h = rowwise(_f_pre, [xin], [vec_a(l, 1), vec_sh(l, 1)], [((d,), BF16)], name=f"pre_{tag}")[0]
        proj = mm(h, lw[l]["wmi"], name=f"mix_in_{tag}", out_dtype=BF16)
        projf = mm(h, lw[l]["wmi"][:, off_f:], name=f"mix_in_f_{tag}")
        bu = mm_blockdiag(proj, s5["b"], a_cb0=off_u // LANES, name=f"ssm_bu_{tag}", out_dtype=BF16)
        st = ssm_scan(bu, s5["tab_f"], reverse=False, name=f"ssm_scan_{tag}")[0]
        y0 = mm_blockdiag(st, s5["c"], name=f"ssm_y_{tag}")
        ge, ys = rowwise(_f_gelu_in, [y0, (proj, sw, off_u // sw)], [s5["dvec"]],
                         [((sw,), BF16), ((sw,), F32)], name=f"gelu_{tag}")
        z = mm(ge, lw[l]["glu"], name=f"glu_{tag}", out_dtype=BF16)
        cum = cum_fwd(projf, fb_pad[l:l + 1], col_block=0, name=f"cum_{tag}")
        crow = cum[:, :heads].T[:, None, :]
        q, k, v = [heads_first(proj[:, off_q + i * aw:off_q + (i + 1) * aw] * sc_).astype(BF16)
                   for i, sc_ in enumerate((scale, 1.0, 1.0))]
        nxt = gather_comm([piece_of(w_in, p, ll).astype(BF16) for ll, p in riders(l)])
        o, lse, *arrived = attn_fwd(q, k, v, crow, name=f"attn_{tag}", comm=nxt)
        for (ll, p), got in zip(riders(l), arrived):
            install(ll, [p], [got])
        attn = heads_last(o).astype(BF16)
        yb = mm(attn, lw[l]["ao"], name=f"attn_out_{tag}", out_dtype=BF16)
        mg = rowwise(_f_merge, [(z, d, 0), (z, d, 1), yb, (proj, d, 0), (proj, d, 1)], [],
                     [((d,), BF16)], name=f"merge_{tag}")[0]
        y = mm(mg, lw[l]["mo"], name=f"mix_out_{tag}")
        xout = rowwise(_f_post_add, [xin, y], [vec_b(l, 1)], [((d,), F32)], name=f"post_{tag}")[0]
        saved = dict(x=xin, h=h, proj=proj, projf=projf, st=st, ys=ys, ge=ge, z=z, q=q, k=k, v=v, o=o, lse=lse,
                     cum=cum, attn=attn, yb=yb, mg=mg, y=y)
        return xout, saved

    saved = []
    xc = x2
    for l in range(depth):
        xc, s0, arrived = ffn_fwd(xc, l, 0, 0, f"l{l}a", gather_comm(shards(0, mix)) if l == 0 else None)
        if l == 0:
            install(0, mix, arrived)
        xc, s1 = mixer_fwd(xc, l, f"l{l}m")
        xc, s2, _ = ffn_fwd(xc, l, 2, 1, f"l{l}b")
        saved.append((s0, s1, s2))

    def f_loss(xf, t):
        e_ = xf - t
        return e_ * (1.0 / d), _colsum(e_ * e_)

    dx, sq = rowwise(f_loss, [xc, tgt], [], [((d,), F32)], [d], name="loss_head")
    loss_part = 0.5 * jnp.sum(sq) / d

    grads = {k: [None] * depth for k in big}
    small_g = [dict() for _ in range(depth)]
    dmod = [[None] * 3 for _ in range(depth)]
    dgpre = [[None] * 3 for _ in range(depth)]
    dgpost = [[None] * 3 for _ in range(depth)]

    def norm_grads(l, i, d_a, d_sh, d_bv):
        d_a, d_sh, d_bv = d_a.reshape(d), d_sh.reshape(d), d_bv.reshape(d)
        dmod[l][i] = jnp.stack([d_sh, d_a * gpre[l, i], res_w[i] * gpost[l, i] * d_bv])
        dgpre[l][i] = d_a * (1.0 + mod_own[l, i, 1])
        dgpost[l][i] = res_w[i] * mod_own[l, i, 2] * d_bv

    def ffn_bwd(dxo, sv, l, i, j, tag, comm_dw=None, comm_dx_of=None):
        dy, d_bv = rowwise(_f_post_bwd, [dxo, sv["y"]], [vec_b(l, i)], [((d,), BF16)], [d],
                           name=f"post_bwd_{tag}")
        dm = mm(dy, lw[l]["wout"][j], trans_b=True, name=f"ffn_out_dx_{tag}", out_dtype=BF16)
        g_out = mm(sv["m"], dy, trans_a=True, name=f"ffn_out_dw_{tag}", out_dtype=BF16, tm=1408, tn=1024)
        da = rowwise(_f_swiglu_bwd, [(sv["a"], ff, 0), (sv["a"], ff, 1), dm], [], [((ff, ff), BF16)],
                     name=f"swiglu_bwd_{tag}")[0]
        g_in, arrived_dw = mm_hosting(sv["h"], da, comm_dw, trans_a=True, name=f"ffn_in_dw_{tag}",
                                      out_dtype=BF16)
        comm_dx = comm_dx_of(g_in, g_out) if comm_dx_of is not None else None
        dh_, arrived_dx = mm_hosting(da, lw[l]["win"][j], comm_dx, trans_b=True, name=f"ffn_in_dx_{tag}")
        dxn, d_a, d_sh = rowwise(_f_pre_bwd, [dxo, dh_, sv["x"]], [vec_a(l, i)], [((d,), F32)], [d, d],
                                 name=f"pre_bwd_{tag}")
        norm_grads(l, i, d_a, d_sh, d_bv)
        return dxn, g_in, g_out, arrived_dw, arrived_dx

    def mixer_bwd(dxo, sv, l, tag, comm):
        s5 = ssm[l]
        proj = sv["proj"]
        dy, d_bv = rowwise(_f_post_bwd, [dxo, sv["y"]], [vec_b(l, 1)], [((d,), BF16)], [d],
                           name=f"post_bwd_{tag}")
        dmg = mm(dy, lw[l]["mo"], trans_b=True, name=f"mix_out_dx_{tag}", out_dtype=BF16)
        g_mo = mm(sv["mg"], dy, trans_a=True, name=f"mix_out_dw_{tag}", out_dtype=BF16)
        dz, dyb, dproj = rowwise(
            _f_merge_bwd, [dmg, (sv["z"], d, 0), (sv["z"], d, 1), sv["yb"], (proj, d, 0), (proj, d, 1)], [],
            [((d, d), BF16), ((d,), BF16), ((d, d), BF16, (iwp, 0))], name=f"merge_bwd_{tag}")
        dge = mm(dz, lw[l]["glu"], trans_b=True, name=f"glu_dx_{tag}")
        g_glu = mm(sv["ge"], dz, trans_a=True, name=f"glu_dw_{tag}", out_dtype=BF16)
        dys, d_dvec = rowwise(_f_gelu_bwd, [dge, sv["ys"], (proj, sw, off_u // sw)], [], [((sw,), BF16)],
                              [sw], name=f"gelu_bwd_{tag}")
        gadj = mm_blockdiag(dys, s5["ct"], name=f"ssm_dy_{tag}", out_dtype=BF16)
        adj, dlam8 = ssm_scan(gadj, s5["tab_r"], reverse=True, s_prev=sv["st"], name=f"ssm_scan_bwd_{tag}")
        du0 = mm_blockdiag(adj, s5["bt"], name=f"ssm_du_{tag}")
        d_bmat = mm_blockdiag_tn(proj, adj, g_n=nc, ka=LANES, kb=2 * hw, a_cb0=off_u // LANES,
                                 name=f"ssm_db_{tag}")
        d_cmat = mm_blockdiag_tn(sv["st"], dys, g_n=nc, ka=2 * hw, kb=LANES, name=f"ssm_dc_{tag}")
        dproj = rowwise(_f_du_fin, [du0, dys], [s5["dvec"]], [((sw,), BF16, (iwp, off_u // sw), dproj)],
                        name=f"ssm_du_fin_{tag}")[0]
        dlam = jnp.sum(dlam8, axis=1)
        dlam_re, dlam_im = dlam[:, :hw].reshape(g_n, p_n), dlam[:, hw:].reshape(g_n, p_n)
        dbb_re = _blockdiag_take(d_bmat[:, :, :hw], g_n).transpose(0, 2, 1)
        dbb_im = _blockdiag_take(d_bmat[:, :, hw:], g_n).transpose(0, 2, 1)
        mag = jnp.exp(s5["e"])
        lr, li = mag * jnp.cos(s5["ph"]), mag * jnp.sin(s5["ph"])
        d_e = dlam_re * lr + dlam_im * li
        d_ph = -dlam_re * li + dlam_im * lr
        da_re, da_im, dlog_dt, db_re, db_im = s5["vjp"]((d_e, d_ph, dbb_re, dbb_im))
        small_g[l].update(
            ssm_a_re=da_re, ssm_a_im=da_im, ssm_log_dt=dlog_dt, ssm_b_re=db_re, ssm_b_im=db_im,
            ssm_c_re=_blockdiag_take(d_cmat[:, :hw, :], g_n).transpose(0, 2, 1),
            ssm_c_im=-_blockdiag_take(d_cmat[:, hw:, :], g_n).transpose(0, 2, 1),
            ssm_d=d_dvec.reshape(sw))
        dattn = mm(dyb, lw[l]["ao"], trans_b=True, name=f"attn_out_dx_{tag}", out_dtype=BF16)
        g_ao = mm(sv["attn"], dyb, trans_a=True, name=f"attn_out_dw_{tag}", out_dtype=BF16)
        dq, dk, dv, dcum, *arrived = attn_bwd(sv["q"], sv["k"], sv["v"], heads_first(dattn), sv["o"], sv["lse"],
                                              sv["cum"], scale=scale, tag=tag, comm=comm)
        df, dfb = cum_bwd(dcum, sv["projf"], fb_pad[l:l + 1], col_block=0, name=f"cum_bwd_{tag}")
        small_g[l]["forget_b"] = dfb[0, :heads]
        for piece, off in ((heads_last(dq), off_q), (heads_last(dk), off_q + aw), (heads_last(dv), off_q + 2 * aw),
                           (df, off_f)):
            dproj = lax.dynamic_update_slice(dproj, piece.astype(BF16), (0, off))
        g_mi = mm(sv["h"], dproj, trans_a=True, name=f"mix_in_dw_{tag}", out_dtype=BF16)
        dh_ = mm(dproj, lw[l]["wmi"], trans_b=True, name=f"mix_in_dx_{tag}")
        dxn, d_a, d_sh = rowwise(_f_pre_bwd, [dxo, dh_, sv["x"]], [vec_a(l, 1)], [((d,), F32)], [d, d],
                                 name=f"pre_bwd_{tag}")
        norm_grads(l, 1, d_a, d_sh, d_bv)
        g_mi = jnp.concatenate([g_mi[:, off_u:off_f + heads], g_mi[:, :off_u]], axis=1)
        return dxn, g_mi, g_glu, g_ao, g_mo, arrived

    def split_last(a):
        return jnp.moveaxis(a.reshape(a.shape[:-1] + (N_DEV, a.shape[-1] // N_DEV)), -2, 0)

    def split_rows(a):
        return jnp.moveaxis(a.reshape(a.shape[:-2] + (N_DEV, a.shape[-2] // N_DEV, a.shape[-1])), -3, 0)

    def owner_blocks(l, pieces):
        out = []
        for name, j in pieces:
            if name == "ffn_w_in":
                out.append(split_last(g_ffn_in[l][j]))
            elif name == "ffn_w_out":
                out.append(split_rows(g_ffn_out[l][j]))
            elif name == "mix_w_out":
                out.append(split_rows(grads[name][l]))
            else:
                out.append(split_last(grads[name][l]))
        return out

    g_ffn_in = [[None, None] for _ in range(depth)]
    g_ffn_out = [[None, None] for _ in range(depth)]
    parts = {}

    def record(l, pieces, arrived):
        for p, a in zip(pieces, arrived):
            parts[(p, l)] = a

    def last_ffn_blocks(g_in, g_out):
        return exchange_comm([split_last(g_in), split_rows(g_out)])

    for l in reversed(range(depth)):
        s0, s1, s2 = saved[l]
        dx, g_ffn_in[l][1], g_ffn_out[l][1], _, _ = ffn_bwd(dx, s2, l, 2, 1, f"l{l}b")
        pending = exchange_comm([owner_blocks(ll, [p])[0] for ll, p in riders(l)])
        (dx, grads["mix_w_in"][l], grads["glu_w"][l], grads["attn_w_out"][l], grads["mix_w_out"][l],
         arrived) = mixer_bwd(dx, s1, l, f"l{l}m", pending)
        for (ll, p), got in zip(riders(l), arrived):
            record(ll, [p], [got])
        if l == 0:
            dx, g_ffn_in[l][0], g_ffn_out[l][0], arrived_mix, arrived_ffn1 = ffn_bwd(
                dx, s0, l, 0, 0, f"l{l}a", exchange_comm(owner_blocks(0, mix)), last_ffn_blocks)
            record(0, mix, arrived_mix)
            record(0, ffn1, arrived_ffn1)
        else:
            dx, g_ffn_in[l][0], g_ffn_out[l][0], _, _ = ffn_bwd(dx, s0, l, 0, 0, f"l{l}a")
    grad_x = dx.reshape(x.shape)

    small_names = ["forget_b", "ssm_a_re", "ssm_a_im", "ssm_log_dt", "ssm_b_re", "ssm_b_im", "ssm_c_re",
                   "ssm_c_im", "ssm_d"]
    pieces = [loss_part.reshape(1), jnp.stack([jnp.stack(dmod[l]) for l in range(depth)]).reshape(-1),
              jnp.stack([jnp.stack(dgpre[l]) for l in range(depth)]).reshape(-1),
              jnp.stack([jnp.stack(dgpost[l]) for l in range(depth)]).reshape(-1)]
    pieces += [jnp.stack([small_g[l][k] for l in range(depth)]).reshape(-1) for k in small_names]
    sizes = [p.size for p in pieces]
    chunk = SUBLANES * 1024
    total = -(-sum(sizes) // chunk) * chunk
    pack = jnp.pad(jnp.concatenate(pieces), (0, total - sum(sizes))).reshape(total // 1024, 1024)
    pack_all = all_gather([pack], name="gather_small_grads")[0]
    pack_sum = rowwise(_f_sum_parts, [(pack_all, p) for p in range(N_DEV)], [], [((1024,), F32)],
                       name="sum_small_grads")[0].reshape(-1)
    offs = [0]
    for s_ in sizes:
        offs.append(offs[-1] + s_)
    take = lambda i: pack_sum[offs[i]:offs[i + 1]]
    loss = take(0).reshape(())
    g_small = {"mod_b": take(1).reshape(mod_b.shape)}
    g_pre_full, g_post_full = take(2).reshape(depth, 3, d), take(3).reshape(depth, 3, d)
    shard = norm_pre.shape[-1]
    g_small["norm_pre"] = lax.dynamic_slice_in_dim(g_pre_full, me * shard, shard, axis=2)
    g_small["norm_post"] = lax.dynamic_slice_in_dim(g_post_full, me * shard, shard, axis=2)
    for i, k in enumerate(small_names):
        g_small[k] = take(4 + i).reshape(w_in[k].shape)

    dmod_all = pack_all.reshape(N_DEV, -1)[:, offs[1]:offs[2]].reshape(N_DEV, depth, 9 * d)
    dmod_mine = lax.dynamic_slice_in_dim(dmod_all, me * mod_cols, mod_cols, axis=2)
    sct_pad = jnp.pad(sc_all.T, ((0, 0), (0, LANES - N_DEV)))
    g_mod_w = jnp.stack([
        mm(sct_pad, jnp.pad(dmod_mine[:, l], ((0, LANES - N_DEV), (0, 0))), name=f"mod_dw{l}")
        for l in range(depth)])

    out_g, out_d, out_m, out_v = {}, {}, {}, {}
    flat = lambda a: a.reshape(-1, a.shape[-1])
    res = adamw(g_mod_w.reshape(1, -1, mod_cols), flat(mod_w), flat(m_mod_w), flat(v_mod_w), name="adamw_mod_w")
    out_g["mod_w"], out_d["mod_w"], out_m["mod_w"], out_v["mod_w"] = [r.reshape(mod_w.shape) for r in res]
    updated = {}
    for l in range(depth):
        for p in ffn1 + mix + ffn2:
            w_p = piece_of(w_in, p, l)
            res = adamw(parts[(p, l)], flat(w_p), flat(piece_of(m_in, p, l)), flat(piece_of(v_in, p, l)),
                        name=f"adamw_{p[0]}_l{l}" + ("" if p[1] is None else f"_{p[1]}"))
            updated[(p, l)] = [r.reshape(w_p.shape) for r in res]
    for k in big:
        for q_, dct in enumerate((out_g, out_d, out_m, out_v)):
            if k.startswith("ffn"):
                dct[k] = jnp.stack([jnp.stack([updated[((k, j), l)][q_] for j in range(2)])
                                    for l in range(depth)])
            else:
                dct[k] = jnp.stack([updated[((k, None), l)][q_] for l in range(depth)])
    small_all = ["mod_b", "norm_pre", "norm_post"] + small_names

    def pack_small(dct):
        flat = jnp.concatenate([dct[k].reshape(-1) for k in small_all])
        tot = -(-flat.size // chunk) * chunk
        return jnp.pad(flat, (0, tot - flat.size)).reshape(tot // 1024, 1024)

    res = adamw(pack_small(g_small)[None], pack_small(w_in), pack_small(m_in), pack_small(v_in),
                name="adamw_small")
    pos = 0
    for k in small_all:
        size = w_in[k].size
        for dct, r in zip((out_g, out_d, out_m, out_v), res):
            dct[k] = r.reshape(-1)[pos:pos + size].reshape(w_in[k].shape)
        pos += size

    return (loss, grad_x, *[out_g[k] for k in names], *[out_d[k] for k in names],
            *[out_m[k] for k in names], *[out_v[k] for k in names])
```

```python
import functools
import math

import jax
import jax.numpy as jnp
from jax import lax
from jax.experimental import pallas as pl
from jax.experimental.pallas import tpu as pltpu

F32 = jnp.float32
BF16 = jnp.bfloat16
MESH = pl.DeviceIdType.MESH
N_DEV = 8
LANES = 128
SUBLANES = 8
VMEM_LIMIT = 48 * 1024 * 1024

RMS_EPS = 1e-6
FFN_RES = 0.5
ADAM_LR = 0.001
ADAM_B1 = 0.9
ADAM_B2 = 0.999
ADAM_EPS = 1e-08
ADAM_WD = 0.01
ADAM_STEP = 10
GELU_C = math.sqrt(2.0 / math.pi)
GELU_A = 0.044715


def _pick(dim, target, mult=LANES):
    t = (min(dim, target) // mult) * mult
    while t >= mult:
        if dim % t == 0:
            return t
        t -= mult
    return dim


def _params(sem):
    return pltpu.CompilerParams(dimension_semantics=sem, vmem_limit_bytes=VMEM_LIMIT)


def _sigmoid(x):
    return 1.0 / (1.0 + jnp.exp(-x))


def mm(a, b, *, name, trans_a=False, trans_b=False, out_dtype=F32, tm=1024, tn=1408, tk=2816, comm=None):
    if trans_a:
        kdim, m = a.shape
    else:
        m, kdim = a.shape
    if trans_b:
        n, kb = b.shape
    else:
        kb, n = b.shape
    assert kdim == kb, (a.shape, b.shape)
    tm, tn, tk = _pick(m, tm), _pick(n, tn), _pick(kdim, tk)
    gm, gn, nk = m // tm, n // tn, kdim // tk
    dims = (((0 if trans_a else 1,), (1 if trans_b else 0,)), ((), ()))
    host = _Hosted(comm, 2, 1, 1 if nk > 1 else 0)

    def body(*refs):
        (a_ref, b_ref, o_ref, *acc), crefs = host.split(refs)
        i, j, k = pl.program_id(0), pl.program_id(1), pl.program_id(2)
        host.phase("start", (i == 0) & (j == 0) & (k == 0), crefs)
        prod = lax.dot_general(a_ref[...].astype(BF16), b_ref[...].astype(BF16), dims,
                               preferred_element_type=F32)
        if nk == 1:
            o_ref[...] = prod.astype(out_dtype)
        else:
            acc_ref, = acc

            @pl.when(k == 0)
            def _():
                acc_ref[...] = prod

            @pl.when((k > 0) & (k < nk - 1))
            def _():
                acc_ref[...] += prod

            @pl.when(k == nk - 1)
            def _():
                o_ref[...] = (acc_ref[...] + prod).astype(out_dtype)

        last = (i == gm - 1) & (j == gn - 1) & (k == nk - 1)
        host.phase("mid", last, crefs)
        host.phase("finish", last, crefs)

    a_spec = (pl.BlockSpec((tk, tm), lambda i, j, k: (k, i)) if trans_a
              else pl.BlockSpec((tm, tk), lambda i, j, k: (i, k)))
    b_spec = (pl.BlockSpec((tn, tk), lambda i, j, k: (j, k)) if trans_b
              else pl.BlockSpec((tk, tn), lambda i, j, k: (k, j)))
    in_specs, out_specs, out_shape, scratch, extra = host.specs(
        [a_spec, b_spec], [pl.BlockSpec((tm, tn), lambda i, j, k: (i, j))],
        [jax.ShapeDtypeStruct((m, n), out_dtype)], [pltpu.VMEM((tm, tn), F32)] if nk > 1 else [])
    res = pl.pallas_call(
        body, name=name, grid=(gm, gn, nk),
        in_specs=in_specs, out_specs=out_specs, out_shape=out_shape, scratch_shapes=scratch,
        compiler_params=_params(("arbitrary", "arbitrary", "arbitrary")),
    )(a, b, *extra)
    return res if comm else res[0]


def mm_blockdiag(a, b, *, name, a_cb0=0, out_dtype=F32, tm=512):
    m = a.shape[0]
    g_n, ka, nb = b.shape
    tm = _pick(m, tm)
    assert a_cb0 % g_n == 0

    def body(a_ref, b_ref, o_ref):
        for g in range(g_n):
            o_ref[:, g * nb:(g + 1) * nb] = jnp.dot(
                a_ref[:, g * ka:(g + 1) * ka].astype(BF16), b_ref[g].astype(BF16),
                preferred_element_type=F32).astype(out_dtype)

    return pl.pallas_call(
        body, name=name, grid=(m // tm,),
        in_specs=[pl.BlockSpec((tm, g_n * ka), lambda i: (i, a_cb0 // g_n)),
                  pl.BlockSpec((g_n, ka, nb), lambda i: (0, 0, 0))],
        out_specs=pl.BlockSpec((tm, g_n * nb), lambda i: (i, 0)),
        out_shape=jax.ShapeDtypeStruct((m, g_n * nb), out_dtype),
        compiler_params=_params(("parallel",)),
    )(a, b)


def mm_blockdiag_tn(a, b, *, name, g_n, ka, kb, a_cb0=0, b_cb0=0, tk=512):
    rows = a.shape[0]
    tk = _pick(rows, tk)
    nk = rows // tk
    assert a_cb0 % g_n == 0 and b_cb0 % g_n == 0

    def body(a_ref, b_ref, o_ref):
        @pl.when(pl.program_id(0) == 0)
        def _():
            o_ref[...] = jnp.zeros_like(o_ref)

        for g in range(g_n):
            o_ref[g] += lax.dot_general(a_ref[:, g * ka:(g + 1) * ka].astype(BF16),
                                        b_ref[:, g * kb:(g + 1) * kb].astype(BF16),
                                        (((0,), (0,)), ((), ())), preferred_element_type=F32)

    return pl.pallas_call(
        body, name=name, grid=(nk,),
        in_specs=[pl.BlockSpec((tk, g_n * ka), lambda k: (k, a_cb0 // g_n)),
                  pl.BlockSpec((tk, g_n * kb), lambda k: (k, b_cb0 // g_n))],
        out_specs=pl.BlockSpec((g_n, ka, kb), lambda k: (0, 0, 0)),
        out_shape=jax.ShapeDtypeStruct((g_n, ka, kb), F32),
        compiler_params=_params(("arbitrary",)),
    )(a, b)


def rowwise(fn, rows, vecs, outs, reds=(), *, name, tm=512):
    metas = []
    for r in rows:
        if isinstance(r, tuple) and len(r) == 3:
            metas.append(("col", r[0], r[1], r[2]))
        elif isinstance(r, tuple):
            metas.append(("lead", r[0], r[0].shape[2], r[1]))
        else:
            metas.append(("full", r, r.shape[1], 0))
    n_rows = metas[0][1].shape[1] if metas[0][0] == "lead" else metas[0][1].shape[0]
    rc = 16 if n_rows % 16 == 0 else (SUBLANES if n_rows % SUBLANES == 0 else n_rows)
    tm = _pick(n_rows, tm, rc)
    n_inner = tm // rc
    windows = [(o[2] if len(o) > 2 else None) for o in outs]
    bases = [(k, o[3]) for k, o in enumerate(outs) if len(o) > 3 and o[3] is not None]
    outs = [(o[0], o[1]) for o in outs]
    nr, nv, no, nbase = len(metas), len(vecs), len(outs), len(bases)

    def body(*refs):
        row_refs, vec_refs = refs[:nr], refs[nr:nr + nv]
        refs = refs[nr + nv + nbase:]
        out_refs, red_refs = refs[:no], refs[no:]
        if reds:
            @pl.when(pl.program_id(0) == 0)
            def _():
                for rr in red_refs:
                    rr[...] = jnp.zeros_like(rr)
        vec_vals = [v[...] for v in vec_refs]

        def step(s, carry):
            r0 = pl.multiple_of(s * rc, rc)
            vals = [ref[pl.ds(r0, rc), :] for ref in row_refs]
            res = fn(*vals, *vec_vals)
            if not isinstance(res, (tuple, list)):
                res = (res,)
            for o_ref, (widths, dt), val in zip(out_refs, outs, res[:no]):
                pieces = val if isinstance(val, (tuple, list)) else (val,)
                off = 0
                for w_, piece in zip(widths, pieces):
                    o_ref[pl.ds(r0, rc), off:off + w_] = piece.astype(dt)
                    off += w_
            for rr, val in zip(red_refs, res[no:]):
                rr[...] += val
            return carry

        lax.fori_loop(0, n_inner, step, 0, unroll=min(n_inner, 4))

    in_specs = []
    for kind, arr, w_, idx in metas:
        if kind == "col":
            in_specs.append(pl.BlockSpec((tm, w_), functools.partial(lambda i, cb: (i, cb), cb=idx)))
        elif kind == "lead":
            in_specs.append(pl.BlockSpec((None, tm, w_), functools.partial(lambda i, p: (p, i, 0), p=idx)))
        else:
            in_specs.append(pl.BlockSpec((tm, w_), lambda i: (i, 0)))
    for v in vecs:
        in_specs.append(pl.BlockSpec(v.shape, lambda i: (0, 0)))
    in_specs += [pl.BlockSpec(memory_space=pl.ANY)] * nbase
    out_specs, out_shape = [], []
    for (ws, dt), win in zip(outs, windows):
        total, cb, total_rows, row0 = (tuple(win) + (n_rows, 0))[:4] if win is not None else (sum(ws), 0, n_rows, 0)
        assert row0 % tm == 0
        out_specs.append(pl.BlockSpec((tm, sum(ws)),
                                      functools.partial(lambda i, cb, rb: (i + rb, cb), cb=cb, rb=row0 // tm)))
        out_shape.append(jax.ShapeDtypeStruct((total_rows, total), dt))
    out_specs += [pl.BlockSpec((1, w_), lambda i: (0, 0)) for w_ in reds]
    out_shape += [jax.ShapeDtypeStruct((1, w_), F32) for w_ in reds]
    res = pl.pallas_call(
        body, name=name, grid=(n_rows // tm,),
        in_specs=in_specs, out_specs=out_specs, out_shape=out_shape,
        input_output_aliases={nr + nv + b: k for b, (k, _) in enumerate(bases)},
        compiler_params=_params(("arbitrary",)),
    )(*[m[1] for m in metas], *vecs, *[b for _, b in bases])
    return res


def _rms(x):
    return lax.rsqrt(jnp.mean(x * x, axis=-1, keepdims=True) + RMS_EPS)


def _colsum(x):
    return jnp.sum(x, axis=0, keepdims=True)


def _f_silu(c):
    return c * _sigmoid(c)


def _f_pre(x, a, sh):
    return (x * _rms(x)) * a + sh


def _f_post_add(x, y, bv):
    return x + (y * _rms(y)) * bv


def _f_post_bwd(dxo, y, bv):
    ry = _rms(y)
    yn = y * ry
    dyn = dxo * bv
    dy = ry * (dyn - yn * jnp.mean(dyn * yn, axis=-1, keepdims=True))
    return dy, _colsum(dxo * yn)


def _f_pre_bwd(dxo, dh, x, a):
    r = _rms(x)
    xn = x * r
    dxn = dh * a
    dx = dxo + r * (dxn - xn * jnp.mean(dxn * xn, axis=-1, keepdims=True))
    return dx, _colsum(dh * xn), _colsum(dh)


def _f_swiglu(g, u):
    g = g.astype(F32)
    return (g * _sigmoid(g)) * u.astype(F32)


def _f_swiglu_bwd(g, u, dm):
    g, u, dm = g.astype(F32), u.astype(F32), dm.astype(F32)
    sg = _sigmoid(g)
    dg = dm * u * (sg * (1.0 + g * (1.0 - sg)))
    du = dm * (g * sg)
    return ((dg, du),)


def _gelu_t(x):
    return jnp.tanh(GELU_C * (x + GELU_A * x * x * x))


def _f_gelu_in(y0, u, dvec):
    y = y0 + dvec * u
    return 0.5 * y * (1.0 + _gelu_t(y)), y


def _f_gelu_bwd(dge, y, u):
    t = _gelu_t(y)
    dy = dge * (0.5 * (1.0 + t) + 0.5 * y * (1.0 - t * t) * GELU_C * (1.0 + 3.0 * GELU_A * y * y))
    return dy, _colsum(dy * u)


def _f_du_fin(du0, dys, dvec):
    return du0 + dvec * dys.astype(F32)


def _f_merge(zv, zg, yb, ga, gb):
    zv, zg, yb, ga, gb = [a.astype(F32) for a in (zv, zg, yb, ga, gb)]
    return _sigmoid(ga) * (zv * _sigmoid(zg)) + _sigmoid(gb) * yb


def _f_merge_bwd(dmg, zv, zg, yb, ga, gb):
    zv, zg, yb, ga, gb = [a.astype(F32) for a in (zv, zg, yb, ga, gb)]
    sa, sb, sz = _sigmoid(ga), _sigmoid(gb), _sigmoid(zg)
    ya = zv * sz
    dya = dmg * sa
    dga = dmg * ya * sa * (1.0 - sa)
    dyb = dmg * sb
    dgb = dmg * yb * sb * (1.0 - sb)
    dzv = dya * sz
    dzg = dya * zv * sz * (1.0 - sz)
    return (dzv, dzg), dyb, (dga, dgb)


def _f_sum_parts(*parts):
    acc = parts[0].astype(F32)
    for p in parts[1:]:
        acc = acc + p.astype(F32)
    return acc


def _f_adamw(*args):
    parts, (w, m, v) = args[:-3], args[-3:]
    g = _f_sum_parts(*parts)
    m = ADAM_B1 * m + (1.0 - ADAM_B1) * g
    v = ADAM_B2 * v + (1.0 - ADAM_B2) * (g * g)
    m_hat = m / (1.0 - ADAM_B1 ** ADAM_STEP)
    v_hat = v / (1.0 - ADAM_B2 ** ADAM_STEP)
    delta = -ADAM_LR * (m_hat / (jnp.sqrt(v_hat) + ADAM_EPS) + ADAM_WD * w)
    return g, delta, m, v


def adamw(parts3, w, m, v, *, name, into=None):
    c = w.shape[1]
    rows = [(parts3, p) for p in range(parts3.shape[0])] + [w, m, v]
    if into is None:
        return rowwise(_f_adamw, rows, [], [((c,), F32)] * 4, name=name)
    bases, total_rows, row0 = into
    outs = [((c,), F32, (c, 0, total_rows, row0), bases[k] if bases else None) for k in range(4)]
    return rowwise(_f_adamw, rows, [], outs, name=name)


def _tri_dot(tri, x, dims=(((1,), (0,)), ((), ()))):
    x1 = x.astype(BF16)
    r1 = x - x1.astype(F32)
    x2 = r1.astype(BF16)
    x3 = (r1 - x2.astype(F32)).astype(BF16)
    dot = functools.partial(lax.dot_general, dimension_numbers=dims, preferred_element_type=F32)
    return dot(tri, x1) + dot(tri, x2) + dot(tri, x3)


def cum_fwd(proj, fb, *, col_block, name, t=256):
    n = proj.shape[0]
    t = _pick(n, t, SUBLANES)

    def body(f_ref, fb_ref, cum_ref, car_ref):
        @pl.when(pl.program_id(0) == 0)
        def _():
            car_ref[...] = jnp.zeros_like(car_ref)

        x = f_ref[...] + fb_ref[...]
        lf = jnp.minimum(x, 0.0) - jnp.log(1.0 + jnp.exp(-jnp.abs(x)))
        r = lax.broadcasted_iota(jnp.int32, (t, t), 0)
        c = lax.broadcasted_iota(jnp.int32, (t, t), 1)
        cs = _tri_dot((c <= r).astype(BF16), lf) + car_ref[0:1, :]
        cum_ref[...] = cs
        car_ref[0:1, :] = cs[t - 1:t, :]

    return pl.pallas_call(
        body, name=name, grid=(n // t,),
        in_specs=[pl.BlockSpec((t, LANES), lambda i: (i, col_block)),
                  pl.BlockSpec((1, LANES), lambda i: (0, 0))],
        out_specs=pl.BlockSpec((t, LANES), lambda i: (i, 0)),
        out_shape=jax.ShapeDtypeStruct((n, LANES), F32),
        scratch_shapes=[pltpu.VMEM((SUBLANES, LANES), F32)],
        compiler_params=_params(("arbitrary",)),
    )(proj, fb)


def cum_bwd(dcum, proj, fb, *, col_block, name, t=256):
    n = proj.shape[0]
    slabs = dcum.shape[0]
    t = _pick(n, t, SUBLANES)
    nb = n // t

    def body(dc_ref, f_ref, fb_ref, df_ref, dfb_ref, car_ref):
        @pl.when(pl.program_id(0) == 0)
        def _():
            car_ref[...] = jnp.zeros_like(car_ref)
            dfb_ref[...] = jnp.zeros_like(dfb_ref)

        r = lax.broadcasted_iota(jnp.int32, (t, t), 0)
        c = lax.broadcasted_iota(jnp.int32, (t, t), 1)
        dl = _tri_dot((c >= r).astype(BF16), jnp.sum(dc_ref[...], axis=0)) + car_ref[0:1, :]
        car_ref[0:1, :] = dl[0:1, :]
        x = f_ref[...] + fb_ref[...]
        df = dl * (1.0 / (1.0 + jnp.exp(x)))
        df_ref[...] = df
        dfb_ref[...] += _colsum(df)

    return pl.pallas_call(
        body, name=name, grid=(nb,),
        in_specs=[pl.BlockSpec((slabs, t, LANES), lambda i: (0, nb - 1 - i, 0)),
                  pl.BlockSpec((t, LANES), lambda i: (nb - 1 - i, col_block)),
                  pl.BlockSpec((1, LANES), lambda i: (0, 0))],
        out_specs=[pl.BlockSpec((t, LANES), lambda i: (nb - 1 - i, 0)),
                   pl.BlockSpec((1, LANES), lambda i: (0, 0))],
        out_shape=[jax.ShapeDtypeStruct((n, LANES), F32), jax.ShapeDtypeStruct((1, LANES), F32)],
        scratch_shapes=[pltpu.VMEM((SUBLANES, LANES), F32)],
        compiler_params=_params(("arbitrary",)),
    )(dcum, proj, fb)


_NT =(((1,), (1,)), ((), ()))
_TN = (((0,), (0,)), ((), ()))


def _lane_sums_as_row(x):
    return _tri_dot(jnp.ones((SUBLANES, x.shape[1]), BF16), x, _NT)[0:1, :]


def _causal_keep(t):
    return lax.broadcasted_iota(jnp.int32, (t, t), 1) <= lax.broadcasted_iota(jnp.int32, (t, t), 0)


def attn_fwd(q, k, v, crow, *, name, t=512, hb=8, comm=None):
    h_n, n, dh = q.shape
    hb = min(hb, h_n)
    t = _pick(n, t)
    nb = n // t
    ng = h_n // hb
    host = _Hosted(comm, 4, 2, 3)

    def body(*refs):
        (q_ref, k_ref, v_ref, cr_ref, o_ref, lse_ref, m_sc, l_sc, acc_sc), crefs = host.split(refs)
        g, i, j = pl.program_id(0), pl.program_id(1), pl.program_id(2)
        host.phase("start", (g == 0) & (i == 0) & (j == 0), crefs)
        host.phase("mid", (g == ng - 1) & (i == (3 * nb) // 4) & (j == 0), crefs)

        @pl.when(j == 0)
        def _():
            m_sc[...] = jnp.full_like(m_sc, -jnp.inf)
            l_sc[...] = jnp.zeros_like(l_sc)
            acc_sc[...] = jnp.zeros_like(acc_sc)

        def update(diagonal):
            keep = _causal_keep(t) if diagonal else None
            heads = range(hb)
            ss = [lax.dot_general(q_ref[h], k_ref[h], _NT, preferred_element_type=F32) for h in heads]
            pairs, alphas = [], []
            for h in heads:
                s = ss[h] - cr_ref[h]
                if diagonal:
                    s = jnp.where(keep, s, -jnp.inf)
                m_prev = m_sc[h]
                m_new = jnp.maximum(m_prev, jnp.max(s, axis=-1, keepdims=True))
                p = jnp.exp(s - m_new)
                alpha = jnp.exp(m_prev - m_new)
                l_sc[h] = alpha * l_sc[h] + jnp.sum(p, axis=-1, keepdims=True)
                m_sc[h] = m_new
                p_hi = p.astype(BF16)
                pairs.append((p_hi, (p - p_hi.astype(F32)).astype(BF16)))
                alphas.append(alpha)
            for h in heads:
                vv = v_ref[h]
                acc_sc[h] = (alphas[h] * acc_sc[h] + jnp.dot(pairs[h][0], vv, preferred_element_type=F32)
                             + jnp.dot(pairs[h][1], vv, preferred_element_type=F32))

        @pl.when(j < i)
        def _():
            update(False)

        @pl.when(j == i)
        def _():
            update(True)

        @pl.when(j == nb - 1)
        def _():
            o_ref[...] = acc_sc[...] / l_sc[...]
            lane0 = lax.broadcasted_iota(jnp.int32, (t, LANES), 1) == 0
            for h in range(hb):
                lse_col = m_sc[h] + jnp.log(l_sc[h])
                lse_ref[h] = _lane_sums_as_row(jnp.where(lane0, lse_col, 0.0))

        host.phase("finish", (g == ng - 1) & (i == nb - 1) & (j == nb - 1), crefs)

    qspec = pl.BlockSpec((hb, t, dh), lambda g, i, j: (g, i, 0))
    kspec = pl.BlockSpec((hb, t, dh), lambda g, i, j: (g, jnp.minimum(j, i), 0))
    in_specs, out_specs, out_shape, scratch, extra = host.specs(
        [qspec, kspec, kspec, pl.BlockSpec((hb, 1, t), lambda g, i, j: (g, 0, jnp.minimum(j, i)))],
        [qspec, pl.BlockSpec((hb, 1, t), lambda g, i, j: (g, 0, i))],
        [jax.ShapeDtypeStruct((h_n, n, dh), F32), jax.ShapeDtypeStruct((h_n, 1, n), F32)],
        [pltpu.VMEM((hb, t, 1), F32), pltpu.VMEM((hb, t, 1), F32), pltpu.VMEM((hb, t, dh), F32)])
    return pl.pallas_call(
        body, name=name, grid=(ng, nb, nb),
        in_specs=in_specs, out_specs=out_specs, out_shape=out_shape, scratch_shapes=scratch,
        compiler_params=_params(("arbitrary", "arbitrary", "arbitrary")),
    )(q, k, v, crow, *extra)


def attn_delta(do, o, *, name, t=512, hb=8):
    h_n, n, dh = do.shape
    hb = min(hb, h_n)
    t = _pick(n, t)

    def body(do_ref, o_ref, dl_ref):
        for h in range(hb):
            dl_ref[h] = _lane_sums_as_row(do_ref[h].astype(F32) * o_ref[h])

    spec = pl.BlockSpec((hb, t, dh), lambda g, i: (g, i, 0))
    return pl.pallas_call(
        body, name=name, grid=(h_n // hb, n // t),
        in_specs=[spec, spec], out_specs=pl.BlockSpec((hb, 1, t), lambda g, i: (g, 0, i)),
        out_shape=jax.ShapeDtypeStruct((h_n, 1, n), F32),
        compiler_params=_params(("parallel", "parallel")),
    )(do, o)


def attn_bwd(q, k, v, do, o, lse, cum, *, scale, tag, t=512, hb=2, comm=None):
    h_n, n, dh = q.shape
    hb = min(hb, h_n)
    t = _pick(n, t)
    nb = n // t
    dob = do.astype(BF16)
    delta = attn_delta(dob, o, name=f"attn_delta_{tag}", t=t)

    ng = h_n // hb
    host = _Hosted(comm, 7, 4, 3)

    def body(*refs):
        (q_ref, k_ref, v_ref, do_ref, lse_ref, dl_ref, cum_ref,
         dq_ref, dk_ref, dv_ref, dcum_ref, dk_acc, dv_acc, dcc_acc), crefs = host.split(refs)
        g, j, i = pl.program_id(0), pl.program_id(1), pl.program_id(2)
        host.phase("start", (g == 0) & (j == 0) & (i == 0), crefs)
        lane = lax.broadcasted_iota(jnp.int32, (t, LANES), 1)

        @pl.when((j == 0) & (i == 0))
        def _():
            dq_ref[...] = jnp.zeros_like(dq_ref)

        @pl.when(i == 0)
        def _():
            dk_acc[...] = jnp.zeros_like(dk_acc)
            dv_acc[...] = jnp.zeros_like(dv_acc)
            dcc_acc[...] = jnp.zeros_like(dcc_acc)

        def update(diagonal):
            heads = range(hb)
            r0 = pl.multiple_of(i * t, t)
            if diagonal:
                keep = lax.broadcasted_iota(jnp.int32, (t, t), 0) <= lax.broadcasted_iota(jnp.int32, (t, t), 1)
            qv, kv = [q_ref[h] for h in heads], [k_ref[h] for h in heads]
            vv, dov = [v_ref[h] for h in heads], [do_ref[h] for h in heads]
            st = [lax.dot_general(kv[h], qv[h], _NT, preferred_element_type=F32) for h in heads]
            dpt = [lax.dot_general(vv[h], dov[h], _NT, preferred_element_type=F32) for h in heads]
            pt = []
            cum_tile = cum_ref[...]
            for h in heads:
                cc = jnp.sum(jnp.where(lane == g * hb + h, cum_tile, 0.0), axis=1, keepdims=True)
                s = st[h] - cc
                if diagonal:
                    s = jnp.where(keep, s, -jnp.inf)
                pt.append(jnp.exp(s - lse_ref[h]))
            for h in heads:
                dv_acc[h] += jnp.dot(pt[h].astype(BF16), dov[h], preferred_element_type=F32)
            dsb = []
            for h in heads:
                ds = pt[h] * (dpt[h] - dl_ref[h])
                dcc_acc[h] -= jnp.sum(ds, axis=1, keepdims=True)
                dsb.append(ds.astype(BF16))
            for h in heads:
                dk_acc[h] += jnp.dot(dsb[h], qv[h], preferred_element_type=F32)
            for h in heads:
                dq_ref[h, pl.ds(r0, t), :] += lax.dot_general(dsb[h], kv[h], _TN,
                                                              preferred_element_type=F32) * scale

        @pl.when(i > j)
        def _():
            update(False)

        @pl.when(i == j)
        def _():
            update(True)

        @pl.when(i == nb - 1)
        def _():
            dk_ref[...] = dk_acc[...]
            dv_ref[...] = dv_acc[...]
            tile = jnp.zeros((t, LANES), F32)
            for h in range(hb):
                tile = tile + jnp.where(lane == g * hb + h, dcc_acc[h], 0.0)
            dcum_ref[...] = tile

        host.phase("finish", (g == ng - 1) & (j == nb - 1) & (i == nb - 1), crefs)

    qspec = pl.BlockSpec((hb, t, dh), lambda g, j, i: (g, jnp.maximum(i, j), 0))
    qrow = pl.BlockSpec((hb, 1, t), lambda g, j, i: (g, 0, jnp.maximum(i, j)))
    kspec = pl.BlockSpec((hb, t, dh), lambda g, j, i: (g, j, 0))
    in_specs, out_specs, out_shape, scratch, extra = host.specs(
        [qspec, kspec, kspec, qspec, qrow, qrow, pl.BlockSpec((t, LANES), lambda g, j, i: (j, 0))],
        [pl.BlockSpec((hb, n, dh), lambda g, j, i: (g, 0, 0)), kspec, kspec,
         pl.BlockSpec((None, t, LANES), lambda g, j, i: (g, j, 0))],
        [jax.ShapeDtypeStruct((h_n, n, dh), F32)] * 3 + [jax.ShapeDtypeStruct((ng, n, LANES), F32)],
        [pltpu.VMEM((hb, t, dh), F32), pltpu.VMEM((hb, t, dh), F32), pltpu.VMEM((hb, t, 1), F32)])
    dq, dk, dv, dcum, *arrived = pl.pallas_call(
        body, name=f"attn_bwd_{tag}", grid=(ng, nb, nb),
        in_specs=in_specs, out_specs=out_specs, out_shape=out_shape, scratch_shapes=scratch,
        compiler_params=_params(("arbitrary", "arbitrary", "arbitrary")),
    )(q, k, v, dob, lse, delta, cum, *extra)
    return [dq, dk, dv, dcum] + arrived


SCAN_STEPS = (1, 2, 4)


def ssm_scan(x, tab, *, reverse, name, s_prev=None, tt=512, out_dtype=BF16):
    n, width = x.shape
    nc, _, hw = tab.shape
    cw = 2 * hw
    assert width == nc * cw
    tt = _pick(n, tt, 2 * SUBLANES)
    nt = n // tt
    ng = tt // (2 * SUBLANES)
    with_grad = s_prev is not None

    def body(*refs):
        if with_grad:
            x_ref, s_ref, tab_ref, o_ref, g_ref, car_ref = refs
        else:
            x_ref, tab_ref, o_ref, car_ref = refs

        @pl.when(pl.program_id(1) == 0)
        def _():
            car_ref[...] = jnp.zeros_like(car_ref)
            if with_grad:
                g_ref[...] = jnp.zeros_like(g_ref)

        q_re, q_im = tab_ref[0:8, :], tab_ref[8:16, :]
        p_re = [tab_ref[16 + i:17 + i, :] for i in range(3)]
        p_im = [tab_ref[24 + i:25 + i, :] for i in range(3)]
        row = lax.broadcasted_iota(jnp.int32, (SUBLANES, hw), 0)

        def group(xr, xi, sr_, si_, carry):
            c_re, c_im = carry
            for i, d in enumerate(SCAN_STEPS):
                if reverse:
                    shift, keep = SUBLANES - d, row < SUBLANES - d
                else:
                    shift, keep = d, row >= d
                sr = jnp.where(keep, pltpu.roll(xr, shift, 0), 0.0)
                si = jnp.where(keep, pltpu.roll(xi, shift, 0), 0.0)
                xr, xi = (xr + p_re[i] * sr - p_im[i] * si,
                          xi + p_re[i] * si + p_im[i] * sr)
            xr, xi = (xr + q_re * c_re - q_im * c_im,
                      xi + q_re * c_im + q_im * c_re)
            if with_grad:
                nr = jnp.where(row < SUBLANES - 1, pltpu.roll(xr, SUBLANES - 1, 0), c_re)
                ni = jnp.where(row < SUBLANES - 1, pltpu.roll(xi, SUBLANES - 1, 0), c_im)
                g_ref[:, 0:hw] += nr * sr_ + ni * si_
                g_ref[:, hw:cw] += ni * sr_ - nr * si_
            if reverse:
                return xr, xi, (xr[0:1, :], xi[0:1, :])
            return xr, xi, (xr[SUBLANES - 1:SUBLANES, :], xi[SUBLANES - 1:SUBLANES, :])

        def pair(gi, carry):
            g = (ng - 1 - gi) if reverse else gi
            r0 = pl.multiple_of(g * 2 * SUBLANES, 2 * SUBLANES)
            rows = pl.ds(r0, 2 * SUBLANES)
            xr, xi = x_ref[rows, 0:hw].astype(F32), x_ref[rows, hw:cw].astype(F32)
            if with_grad:
                sr, si = s_ref[rows, 0:hw].astype(F32), s_ref[rows, hw:cw].astype(F32)
            halves = [slice(0, SUBLANES), slice(SUBLANES, 2 * SUBLANES)]
            done = [None, None]
            for k in ((1, 0) if reverse else (0, 1)):
                h = halves[k]
                o_re, o_im, carry = group(xr[h], xi[h], sr[h] if with_grad else None,
                                          si[h] if with_grad else None, carry)
                done[k] = (o_re, o_im)
            o_ref[rows, 0:hw] = jnp.concatenate([done[0][0], done[1][0]], axis=0).astype(o_ref.dtype)
            o_ref[rows, hw:cw] = jnp.concatenate([done[0][1], done[1][1]], axis=0).astype(o_ref.dtype)
            return carry

        c_re, c_im = lax.fori_loop(0, ng, pair, (car_ref[0:1, 0:hw], car_ref[0:1, hw:cw]),
                                   unroll=min(ng, 2))
        car_ref[0:1, 0:hw] = c_re
        car_ref[0:1, hw:cw] = c_im

    if reverse:
        xspec = pl.BlockSpec((tt, cw), lambda c, t: (nt - 1 - t, c))
    else:
        xspec = pl.BlockSpec((tt, cw), lambda c, t: (t, c))
    tspec = pl.BlockSpec((None, 32, hw), lambda c, t: (c, 0, 0))
    in_specs = [xspec, xspec, tspec] if with_grad else [xspec, tspec]
    out_specs = [xspec]
    out_shape = [jax.ShapeDtypeStruct((n, width), out_dtype)]
    if with_grad:
        out_specs.append(pl.BlockSpec((None, SUBLANES, cw), lambda c, t: (c, 0, 0)))
        out_shape.append(jax.ShapeDtypeStruct((nc, SUBLANES, cw), F32))
    operands = (x, s_prev, tab) if with_grad else (x, tab)
    return pl.pallas_call(
        body, name=name, grid=(nc, nt),
        in_specs=in_specs, out_specs=out_specs, out_shape=out_shape,
        scratch_shapes=[pltpu.VMEM((SUBLANES, cw), F32)],
        compiler_params=_params(("parallel", "arbitrary")),
    )(*operands)


def _slot(pos):
    return 4 * pos[0] + 2 * pos[1] + pos[2]


def _comm_scratch(n):
    return [pltpu.SemaphoreType.DMA((7 * n,)), pltpu.SemaphoreType.DMA((7 * n,)), pltpu.SemaphoreType.DMA((n,))]


def _gather_copies(ins, outs, sems):
    send_sems, recv_sems, local_sems = sems
    n = len(ins)
    x, y, c = lax.axis_index("x"), lax.axis_index("y"), lax.axis_index("c")
    me, sibling = (x, y, c), (x, y, 1 - c)
    chips = [(1 - x, y), (x, 1 - y), (1 - x, 1 - y)]

    def copy(t, k, block, to, src=None):
        dst = outs[t].at[_slot(block)]
        return pltpu.make_async_remote_copy(
            src_ref=dst if src is None else src, dst_ref=dst,
            send_sem=send_sems.at[7 * t + k], recv_sem=recv_sems.at[7 * t + k],
            device_id=to, device_id_type=MESH)

    jc = list(enumerate(chips))
    return dict(
        mine=[pltpu.make_async_copy(ins[t], outs[t].at[_slot(me)], local_sems.at[t]) for t in range(n)],
        first=[cp for t in range(n) for cp in
               [copy(t, 0, me, sibling, src=ins[t])] + [copy(t, 1 + j, me, (*chip, c), src=ins[t]) for j, chip in jc]],
        arrive=[copy(t, 1 + j, (*chip, c), me) for t in range(n) for j, chip in jc],
        passed=[copy(t, 4 + j, (*chip, c), sibling) for t in range(n) for j, chip in jc],
        from_sibling=[cp for t in range(n) for cp in
                      [copy(t, 0, sibling, me)] + [copy(t, 4 + j, (*chip, 1 - c), me) for j, chip in jc]])


def _gather_start(ins, outs, sems):
    cps = _gather_copies(ins, outs, sems)
    for cp in cps["mine"] + cps["first"]:
        cp.start()


def _gather_forward(ins, outs, sems):
    cps = _gather_copies(ins, outs, sems)
    for arrived, onward in zip(cps["arrive"], cps["passed"]):
        arrived.wait_recv()
        onward.start()


def _gather_finish(ins, outs, sems):
    cps = _gather_copies(ins, outs, sems)
    for cp in cps["from_sibling"]:
        cp.wait_recv()
    for cp in cps["first"] + cps["passed"]:
        cp.wait_send()
    for cp in cps["mine"]:
        cp.wait()


def gather_comm(arrs):
    return dict(ins=list(arrs), out_shape=[jax.ShapeDtypeStruct((N_DEV,) + a.shape, a.dtype) for a in arrs],
                scratch=_comm_scratch(len(arrs)), start=_gather_start, mid=_gather_forward, finish=_gather_finish)


def _exchange_copies(ins, outs, sems):
    send_sems, recv_sems, local_sems = sems
    n = len(ins)
    me = (lax.axis_index("x"), lax.axis_index("y"), lax.axis_index("c"))
    peers = []
    for k in range(1, N_DEV):
        flip = ((k >> 2) & 1, (k >> 1) & 1, k & 1)
        peers.append(tuple(1 - p if f else p for p, f in zip(me, flip)))

    def copy(t, k, peer, dst_slot):
        return pltpu.make_async_remote_copy(
            src_ref=ins[t].at[_slot(peer)], dst_ref=outs[t].at[dst_slot],
            send_sem=send_sems.at[7 * t + k], recv_sem=recv_sems.at[7 * t + k],
            device_id=peer, device_id_type=MESH)

    return dict(
        mine=[pltpu.make_async_copy(ins[t].at[_slot(me)], outs[t].at[_slot(me)], local_sems.at[t])
              for t in range(n)],
        send=[copy(t, k, peer, _slot(me)) for t in range(n) for k, peer in enumerate(peers)],
        both=[copy(t, k, peer, _slot(peer)) for t in range(n) for k, peer in enumerate(peers)])


def _exchange_start(ins, outs, sems):
    cps = _exchange_copies(ins, outs, sems)
    for cp in cps["mine"] + cps["send"]:
        cp.start()


def _exchange_finish(ins, outs, sems):
    cps = _exchange_copies(ins, outs, sems)
    for cp in cps["both"]:
        cp.wait()
    for cp in cps["mine"]:
        cp.wait()


def exchange_comm(arrs):
    return dict(ins=list(arrs), out_shape=[jax.ShapeDtypeStruct(a.shape, a.dtype) for a in arrs],
                scratch=_comm_scratch(len(arrs)), start=_exchange_start, mid=None, finish=_exchange_finish)


def run_comm(comm, *, name):
    n_in, n_out = len(comm["ins"]), len(comm["out_shape"])

    def body(*refs):
        ins, outs, sems = refs[:n_in], refs[n_in:n_in + n_out], refs[n_in + n_out:]
        comm["start"](ins, outs, sems)
        if comm["mid"] is not None:
            comm["mid"](ins, outs, sems)
        comm["finish"](ins, outs, sems)

    any_spec = pl.BlockSpec(memory_space=pl.ANY)
    return pl.pallas_call(
        body, name=name, in_specs=[any_spec] * n_in, out_specs=[any_spec] * n_out,
        out_shape=comm["out_shape"], scratch_shapes=comm["scratch"],
    )(*comm["ins"])


class _Hosted:
    def __init__(self, comm, n_in, n_out, n_scratch):
        self.comm = comm
        self.n_ci = len(comm["ins"]) if comm else 0
        self.n_co = len(comm["out_shape"]) if comm else 0
        self.n_in, self.n_out, self.n_scratch = n_in, n_out, n_scratch

    def split(self, refs):
        a = self.n_in
        b = a + self.n_ci
        c = b + self.n_out
        e = c + self.n_co
        f = e + self.n_scratch
        return refs[:a] + refs[b:c] + refs[e:f], (refs[a:b], refs[c:e], refs[f:])

    def phase(self, which, when, crefs):
        fn = self.comm[which] if self.comm else None
        if fn is not None:
            pl.when(when)(lambda: fn(*crefs))

    def specs(self, in_specs, out_specs, out_shape, scratch):
        any_spec = pl.BlockSpec(memory_space=pl.ANY)
        if not self.comm:
            return in_specs, out_specs, out_shape, scratch, ()
        return (in_specs + [any_spec] * self.n_ci, out_specs + [any_spec] * self.n_co,
                out_shape + self.comm["out_shape"], scratch + self.comm["scratch"], tuple(self.comm["ins"]))


def all_gather(arrs, *, name):
    return run_comm(gather_comm(arrs), name=name)


def _discretise(a_re, a_im, log_dt, b_re, b_im):
    ar = jnp.minimum(a_re, -1e-4)
    dt = jnp.exp(log_dt)[:, None]
    e, ph = ar * dt, a_im * dt
    mag = jnp.exp(e)
    lr, li = mag * jnp.cos(ph), mag * jnp.sin(ph)
    den = ar * ar + a_im * a_im
    nr, ni = lr - 1.0, li
    cr = (nr * ar + ni * a_im) / den
    ci = (ni * ar - nr * a_im) / den
    bb_re = cr[..., None] * b_re - ci[..., None] * b_im
    bb_im = cr[..., None] * b_im + ci[..., None] * b_re
    return e, ph, bb_re, bb_im


def _lam_pow(e, ph, k, conj):
    mag = jnp.exp(k * e)
    return mag * jnp.cos(k * ph), (-1.0 if conj else 1.0) * mag * jnp.sin(k * ph)


def _scan_table(e, ph, nc, reverse):
    hw = e.size // nc
    e, ph = e.reshape(nc, 1, hw), ph.reshape(nc, 1, hw)
    j = jnp.arange(SUBLANES, dtype=F32).reshape(1, SUBLANES, 1)
    kq = (SUBLANES - j) if reverse else (j + 1.0)
    q_re, q_im = _lam_pow(e, ph, kq, reverse)
    kp = jnp.array(SCAN_STEPS + (0,) * 5, F32).reshape(1, SUBLANES, 1)
    p_re, p_im = _lam_pow(e, ph, kp, reverse)
    return jnp.concatenate([q_re, q_im, p_re, p_im], axis=1)


def _blockdiag(m, nc):
    g, a, b = m.shape
    gc = g // nc
    m = m.reshape(nc, gc, a, b)
    eye = jnp.eye(gc, dtype=m.dtype)
    return jnp.einsum("cgab,gh->cgahb", m, eye).reshape(nc, gc * a, gc * b)


def _blockdiag_take(m, g):
    nc = m.shape[0]
    gc = g // nc
    a, b = m.shape[1] // gc, m.shape[2] // gc
    m = m.reshape(nc, gc, a, gc, b)
    eye = jnp.eye(gc, dtype=m.dtype)
    return jnp.einsum("cgahb,gh->cgab", m, eye).reshape(g, a, b)


def kernel(x, c, mod_w, mod_b, norm_pre, norm_post, ffn_w_in, ffn_w_out, mix_w_in, forget_b, ssm_a_re, ssm_a_im, ssm_log_dt, ssm_b_re, ssm_b_im, ssm_c_re, ssm_c_im, ssm_d, glu_w, attn_w_out, mix_w_out, loss_target, m_mod_w, m_mod_b, m_norm_pre, m_norm_post, m_ffn_w_in, m_ffn_w_out, m_mix_w_in, m_forget_b, m_ssm_a_re, m_ssm_a_im, m_ssm_log_dt, m_ssm_b_re, m_ssm_b_im, m_ssm_c_re, m_ssm_c_im, m_ssm_d, m_glu_w, m_attn_w_out, m_mix_w_out, v_mod_w, v_mod_b, v_norm_pre, v_norm_post, v_ffn_w_in, v_ffn_w_out, v_mix_w_in, v_forget_b, v_ssm_a_re, v_ssm_a_im, v_ssm_log_dt, v_ssm_b_re, v_ssm_b_im, v_ssm_c_re, v_ssm_c_im, v_ssm_d, v_glu_w, v_attn_w_out, v_mix_w_out):
    names = ["mod_w", "mod_b", "norm_pre", "norm_post", "ffn_w_in", "ffn_w_out", "mix_w_in", "forget_b",
             "ssm_a_re", "ssm_a_im", "ssm_log_dt", "ssm_b_re", "ssm_b_im", "ssm_c_re", "ssm_c_im", "ssm_d",
             "glu_w", "attn_w_out", "mix_w_out"]
    w_in = dict(zip(names, [mod_w, mod_b, norm_pre, norm_post, ffn_w_in, ffn_w_out, mix_w_in, forget_b,
                            ssm_a_re, ssm_a_im, ssm_log_dt, ssm_b_re, ssm_b_im, ssm_c_re, ssm_c_im, ssm_d,
                            glu_w, attn_w_out, mix_w_out]))
    m_in = dict(zip(names, [m_mod_w, m_mod_b, m_norm_pre, m_norm_post, m_ffn_w_in, m_ffn_w_out, m_mix_w_in,
                            m_forget_b, m_ssm_a_re, m_ssm_a_im, m_ssm_log_dt, m_ssm_b_re, m_ssm_b_im,
                            m_ssm_c_re, m_ssm_c_im, m_ssm_d, m_glu_w, m_attn_w_out, m_mix_w_out]))
    v_in = dict(zip(names, [v_mod_w, v_mod_b, v_norm_pre, v_norm_post, v_ffn_w_in, v_ffn_w_out, v_mix_w_in,
                            v_forget_b, v_ssm_a_re, v_ssm_a_im, v_ssm_log_dt, v_ssm_b_re, v_ssm_b_im,
                            v_ssm_c_re, v_ssm_c_im, v_ssm_d, v_glu_w, v_attn_w_out, v_mix_w_out]))

    depth = mod_w.shape[0]
    n_tok, d = x.shape[1], x.shape[2]
    ff = ffn_w_out.shape[2] * N_DEV
    heads = forget_b.shape[1]
    sw = ssm_d.shape[1]
    g_n, p_n, n_n = ssm_b_re.shape[1:]
    aw = attn_w_out.shape[1]
    dh = aw // heads
    iw = mix_w_in.shape[2] * N_DEV
    nc = sw // LANES
    hw = g_n * p_n // nc
    mod_cols = mod_w.shape[2]
    scale = dh ** -0.5
    assert iw == sw + 3 * aw + heads + 2 * d and heads <= LANES
    assert math.log2(scale).is_integer(), "q is pre-scaled in bf16: exact only for a power of two"
    off_u, off_q, off_f = 2 * d, 2 * d + sw, 2 * d + sw + 3 * aw
    iwp = off_f + LANES
    assert off_u % sw == 0 and off_q % aw == 0 and off_f % LANES == 0

    me = 4 * lax.axis_index("x") + 2 * lax.axis_index("y") + lax.axis_index("c")
    x2 = x.reshape(n_tok, d)
    tgt = loss_target.reshape(n_tok, d)

    silu_c = rowwise(_f_silu, [c], [], [((d,), F32)], name="silu_c")[0]
    big = ["ffn_w_in", "ffn_w_out", "mix_w_in", "glu_w", "attn_w_out", "mix_w_out"]
    ffn1 = [("ffn_w_in", 0), ("ffn_w_out", 0)]
    mix = [("mix_w_in", None), ("glu_w", None), ("attn_w_out", None), ("mix_w_out", None)]
    ffn2 = [("ffn_w_in", 1), ("ffn_w_out", 1)]

    def riders(l):
        nxt = [(l + 1, p) for p in ffn1 + mix] if l + 1 < depth else []
        return [(l, p) for p in ffn2] + nxt

    def piece_of(dct, piece, l):
        name, j = piece
        return dct[name][l] if j is None else dct[name][l][j]

    row_sharded = ("ffn_w_out", "mix_w_out")

    def send_shard(p, l):
        a = piece_of(w_in, p, l).astype(BF16)
        return a if p[0] in row_sharded else a.T

    def shards(l, pieces):
        return [send_shard(p, l) for p in pieces]

    cut = [0, sw, sw + aw, sw + 2 * aw, sw + 3 * aw, sw + 3 * aw + heads, sw + 3 * aw + heads + d, iw]
    lw = [dict(win_t=[None, None], wout=[None, None]) for _ in range(depth)]

    def install(l, pieces, gathered):
        for (name, j), g in zip(pieces, gathered):
            whole = g.reshape(-1, g.shape[-1])
            if name == "ffn_w_in":
                lw[l]["win_t"][j] = whole
            elif name == "ffn_w_out":
                lw[l]["wout"][j] = whole
            elif name == "mix_w_in":
                seg = lambda i: whole[cut[i]:cut[i + 1]]
                lw[l]["wmi_t"] = jnp.concatenate([seg(5), seg(6), seg(0), seg(1), seg(2), seg(3),
                                                  jnp.pad(seg(4), ((0, LANES - heads), (0, 0)))], axis=0)
            elif name == "mix_w_out":
                lw[l]["mo"] = whole
            else:
                lw[l]["glu_t" if name == "glu_w" else "ao_t"] = whole

    gathered = all_gather(
        [silu_c, norm_pre.reshape(-1, norm_pre.shape[-1]), norm_post.reshape(-1, norm_post.shape[-1])]
        + shards(0, ffn1), name="gather_first")
    sc_all = gathered[0].reshape(N_DEV, d)
    gpre = jnp.moveaxis(gathered[1].reshape(N_DEV, depth, 3, -1), 0, 2).reshape(depth, 3, d)
    gpost = jnp.moveaxis(gathered[2].reshape(N_DEV, depth, 3, -1), 0, 2).reshape(depth, 3, d)
    install(0, ffn1, gathered[3:])

    sc_pad = jnp.pad(sc_all, ((0, LANES - N_DEV), (0, 0)))
    mod_part = jnp.stack([mm(sc_pad, mod_w[l], name=f"mod_fwd{l}")[:N_DEV] for l in range(depth)], axis=1)
    mod_part = mod_part + lax.dynamic_slice_in_dim(mod_b, me * mod_cols, mod_cols, axis=1)[None]
    mod_all = all_gather([mod_part], name="gather_mod")[0]
    mod_own = lax.dynamic_index_in_dim(mod_all, me, axis=1, keepdims=False)
    mod_own = mod_own.transpose(1, 0, 2).reshape(depth, 3, 3, d)
    res_w = (FFN_RES, 1.0, FFN_RES)

    def vec_a(l, i):
        return (gpre[l, i] * (1.0 + mod_own[l, i, 1])).reshape(1, d)

    def vec_sh(l, i):
        return mod_own[l, i, 0].reshape(1, d)

    def vec_b(l, i):
        return (res_w[i] * mod_own[l, i, 2] * gpost[l, i]).reshape(1, d)

    ssm = []
    for l in range(depth):
        (e, ph, bb_re, bb_im), disc_vjp = jax.vjp(_discretise,ssm_a_re[l], ssm_a_im[l], ssm_log_dt[l],
                                                  ssm_b_re[l], ssm_b_im[l])
        b_mat = jnp.concatenate([_blockdiag(bb_re.transpose(0, 2, 1), nc),
                                 _blockdiag(bb_im.transpose(0, 2, 1), nc)], axis=2)
        c_mat = jnp.concatenate([_blockdiag(ssm_c_re[l].transpose(0, 2, 1), nc),
                                 _blockdiag(-ssm_c_im[l].transpose(0, 2, 1), nc)], axis=1)
        ssm.append(dict(e=e, ph=ph, vjp=disc_vjp, b=b_mat.astype(BF16), c=c_mat.astype(BF16),
                        bt=b_mat.transpose(0, 2, 1).astype(BF16), ct=c_mat.transpose(0, 2, 1).astype(BF16),
                        tab_f=_scan_table(e, ph, nc, False), tab_r=_scan_table(e, ph, nc, True),
                        dvec=ssm_d[l].reshape(1, sw)))

    fb_pad = jnp.pad(forget_b, ((0, 0), (0, LANES - heads)))

    def heads_first(a):
        return a.reshape(n_tok, heads, dh).transpose(1, 0, 2)

    def heads_last(a):
        return a.transpose(1, 0, 2).reshape(n_tok, heads * dh)

    def mm_hosting(a, b, comm, **kw):
        if comm is None:
            return mm(a, b, **kw), []
        out, *arrived = mm(a, b, comm=comm, **kw)
        return out, arrived

    def ffn_fwd(xin, l, i, j, tag, comm=None):
        h = rowwise(_f_pre, [xin], [vec_a(l, i), vec_sh(l, i)], [((d,), BF16)], name=f"pre_{tag}")[0]
        a, arrived = mm_hosting(h, lw[l]["win_t"][j], comm, trans_b=True, name=f"ffn_in_{tag}", out_dtype=BF16)
        m = rowwise(_f_swiglu, [(a, ff, 0), (a, ff, 1)], [], [((ff,), BF16)], name=f"swiglu_{tag}")[0]
        y = mm(m, lw[l]["wout"][j], name=f"ffn_out_{tag}")
        xout = rowwise(_f_post_add, [xin, y], [vec_b(l, i)], [((d,), F32)], name=f"post_{tag}")[0]
        return xout, dict(x=xin, h=h, a=a, m=m, y=y), arrived

    def mixer_fwd(xin, l, tag):
        s5 = ssm[l]
        h = rowwise(_f_pre, [xin], [vec_a(l, 1), vec_sh(l, 1)], [((d,), BF16)], name=f"pre_{tag}")[0]
        proj = mm(h, lw[l]["wmi_t"], trans_b=True, name=f"mix_in_{tag}", out_dtype=BF16)
        projf = mm(h, lw[l]["wmi_t"][off_f:], trans_b=True, name=f"mix_in_f_{tag}")
        bu = mm_blockdiag(proj, s5["b"], a_cb0=off_u // LANES, name=f"ssm_bu_{tag}", out_dtype=BF16)
        st = ssm_scan(bu, s5["tab_f"], reverse=False, name=f"ssm_scan_{tag}")[0]
        y0 = mm_blockdiag(st, s5["c"], name=f"ssm_y_{tag}")
        ge, ys = rowwise(_f_gelu_in, [y0, (proj, sw, off_u // sw)], [s5["dvec"]],
                         [((sw,), BF16), ((sw,), F32)], name=f"gelu_{tag}")
        z = mm(ge, lw[l]["glu_t"], trans_b=True, name=f"glu_{tag}", out_dtype=BF16)
        cum = cum_fwd(projf, fb_pad[l:l + 1], col_block=0, name=f"cum_{tag}")
        crow = cum[:, :heads].T[:, None, :]
        q, k, v = [heads_first(proj[:, off_q + i * aw:off_q + (i + 1) * aw] * sc_).astype(BF16)
                   for i, sc_ in enumerate((scale, 1.0, 1.0))]
        nxt = gather_comm([send_shard(p, ll) for ll, p in riders(l)])
        o, lse, *arrived = attn_fwd(q, k, v, crow, name=f"attn_{tag}", comm=nxt)
        for (ll, p), got in zip(riders(l), arrived):
            install(ll, [p], [got])
        attn = heads_last(o).astype(BF16)
        yb = mm(attn, lw[l]["ao_t"], trans_b=True, name=f"attn_out_{tag}", out_dtype=BF16)
        mg = rowwise(_f_merge, [(z, d, 0), (z, d, 1), yb, (proj, d, 0), (proj, d, 1)], [],
                     [((d,), BF16)], name=f"merge_{tag}")[0]
        y = mm(mg, lw[l]["mo"], name=f"mix_out_{tag}")
        xout = rowwise(_f_post_add, [xin, y], [vec_b(l, 1)], [((d,), F32)], name=f"post_{tag}")[0]
        saved = dict(x=xin, h=h, proj=proj, projf=projf, st=st, ys=ys, ge=ge, z=z, q=q, k=k, v=v, o=o, lse=lse,
                     cum=cum, attn=attn, yb=yb, mg=mg, y=y)
        return xout, saved

    saved = []
    xc = x2
    for l in range(depth):
        xc, s0, arrived = ffn_fwd(xc, l, 0, 0, f"l{l}a", gather_comm(shards(0, mix)) if l == 0 else None)
        if l == 0:
            install(0, mix, arrived)
        xc, s1 = mixer_fwd(xc, l, f"l{l}m")
        xc, s2, _ = ffn_fwd(xc, l, 2, 1, f"l{l}b")
        saved.append((s0, s1, s2))

    def f_loss(xf, t):
        e_ = xf - t
        return e_ * (1.0 / d), _colsum(e_ * e_)

    dx, sq = rowwise(f_loss, [xc, tgt], [], [((d,), F32)], [d], name="loss_head")
    loss_part = 0.5 * jnp.sum(sq) / d

    grads = {k: [None] * depth for k in big}
    small_g = [dict() for _ in range(depth)]
    dmod = [[None] * 3 for _ in range(depth)]
    dgpre = [[None] * 3 for _ in range(depth)]
    dgpost = [[None] * 3 for _ in range(depth)]

    def norm_grads(l, i, d_a, d_sh, d_bv):
        d_a, d_sh, d_bv = d_a.reshape(d), d_sh.reshape(d), d_bv.reshape(d)
        dmod[l][i] = jnp.stack([d_sh, d_a * gpre[l, i], res_w[i] * gpost[l, i] * d_bv])
        dgpre[l][i] = d_a * (1.0 + mod_own[l, i, 1])
        dgpost[l][i] = res_w[i] * mod_own[l, i, 2] * d_bv

    def ffn_bwd(dxo, sv, l, i, j, tag, comm_dw=None, comm_dx_of=None):
        dy, d_bv = rowwise(_f_post_bwd, [dxo, sv["y"]], [vec_b(l, i)], [((d,), BF16)], [d],
                           name=f"post_bwd_{tag}")
        dm = mm(dy, lw[l]["wout"][j], trans_b=True, name=f"ffn_out_dx_{tag}", out_dtype=BF16)
        g_out = mm(sv["m"], dy, trans_a=True, name=f"ffn_out_dw_{tag}", out_dtype=BF16, tm=1408, tn=1024)
        da = rowwise(_f_swiglu_bwd, [(sv["a"], ff, 0), (sv["a"], ff, 1), dm], [], [((ff, ff), BF16)],
                     name=f"swiglu_bwd_{tag}")[0]
        g_in, arrived_dw = mm_hosting(sv["h"], da, comm_dw, trans_a=True, name=f"ffn_in_dw_{tag}",
                                      out_dtype=BF16)
        comm_dx = comm_dx_of(g_in, g_out) if comm_dx_of is not None else None
        dh_, arrived_dx = mm_hosting(da, lw[l]["win_t"][j], comm_dx, name=f"ffn_in_dx_{tag}")
        dxn, d_a, d_sh = rowwise(_f_pre_bwd, [dxo, dh_, sv["x"]], [vec_a(l, i)], [((d,), F32)], [d, d],
                                 name=f"pre_bwd_{tag}")
        norm_grads(l, i, d_a, d_sh, d_bv)
        return dxn, g_in, g_out, arrived_dw, arrived_dx

    def mixer_bwd(dxo, sv, l, tag, comm):
        s5 = ssm[l]
        proj = sv["proj"]
        dy, d_bv = rowwise(_f_post_bwd, [dxo, sv["y"]], [vec_b(l, 1)], [((d,), BF16)], [d],
                           name=f"post_bwd_{tag}")
        dmg = mm(dy, lw[l]["mo"], trans_b=True, name=f"mix_out_dx_{tag}", out_dtype=BF16)
        g_mo = mm(sv["mg"], dy, trans_a=True, name=f"mix_out_dw_{tag}", out_dtype=BF16)
        dz, dyb, dproj = rowwise(
            _f_merge_bwd, [dmg, (sv["z"], d, 0), (sv["z"], d, 1), sv["yb"], (proj, d, 0), (proj, d, 1)], [],
            [((d, d), BF16), ((d,), BF16), ((d, d), BF16, (iwp, 0))], name=f"merge_bwd_{tag}")
        dge = mm(dz, lw[l]["glu_t"], name=f"glu_dx_{tag}")
        g_glu = mm(sv["ge"], dz, trans_a=True, name=f"glu_dw_{tag}", out_dtype=BF16)
        dys, d_dvec = rowwise(_f_gelu_bwd, [dge, sv["ys"], (proj, sw, off_u // sw)], [], [((sw,), BF16)],
                              [sw], name=f"gelu_bwd_{tag}")
        gadj = mm_blockdiag(dys, s5["ct"], name=f"ssm_dy_{tag}", out_dtype=BF16)
        adj, dlam8 = ssm_scan(gadj, s5["tab_r"], reverse=True, s_prev=sv["st"], name=f"ssm_scan_bwd_{tag}")
        du0 = mm_blockdiag(adj, s5["bt"], name=f"ssm_du_{tag}")
        d_bmat = mm_blockdiag_tn(proj, adj, g_n=nc, ka=LANES, kb=2 * hw, a_cb0=off_u // LANES,
                                 name=f"ssm_db_{tag}")
        d_cmat = mm_blockdiag_tn(sv["st"], dys, g_n=nc, ka=2 * hw, kb=LANES, name=f"ssm_dc_{tag}")
        dproj = rowwise(_f_du_fin, [du0, dys], [s5["dvec"]], [((sw,), BF16, (iwp, off_u // sw), dproj)],
                        name=f"ssm_du_fin_{tag}")[0]
        dlam = jnp.sum(dlam8, axis=1)
        dlam_re, dlam_im = dlam[:, :hw].reshape(g_n, p_n), dlam[:, hw:].reshape(g_n, p_n)
        dbb_re = _blockdiag_take(d_bmat[:, :, :hw], g_n).transpose(0, 2, 1)
        dbb_im = _blockdiag_take(d_bmat[:, :, hw:], g_n).transpose(0, 2, 1)
        mag = jnp.exp(s5["e"])
        lr, li = mag * jnp.cos(s5["ph"]), mag * jnp.sin(s5["ph"])
        d_e = dlam_re * lr + dlam_im * li
        d_ph = -dlam_re * li + dlam_im * lr
        da_re, da_im, dlog_dt, db_re, db_im = s5["vjp"]((d_e, d_ph, dbb_re, dbb_im))
        small_g[l].update(
            ssm_a_re=da_re, ssm_a_im=da_im, ssm_log_dt=dlog_dt, ssm_b_re=db_re, ssm_b_im=db_im,
            ssm_c_re=_blockdiag_take(d_cmat[:, :hw, :], g_n).transpose(0, 2, 1),
            ssm_c_im=-_blockdiag_take(d_cmat[:, hw:, :], g_n).transpose(0, 2, 1),
            ssm_d=d_dvec.reshape(sw))
        dattn = mm(dyb, lw[l]["ao_t"], name=f"attn_out_dx_{tag}", out_dtype=BF16)
        g_ao = mm(sv["attn"], dyb, trans_a=True, name=f"attn_out_dw_{tag}", out_dtype=BF16)
        dq, dk, dv, dcum, *arrived = attn_bwd(sv["q"], sv["k"], sv["v"], heads_first(dattn), sv["o"], sv["lse"],
                                              sv["cum"], scale=scale, tag=tag, comm=comm)
        df, dfb = cum_bwd(dcum, sv["projf"], fb_pad[l:l + 1], col_block=0, name=f"cum_bwd_{tag}")
        small_g[l]["forget_b"] = dfb[0, :heads]
        for piece, off in ((heads_last(dq), off_q), (heads_last(dk), off_q + aw), (heads_last(dv), off_q + 2 * aw),
                           (df, off_f)):
            dproj = lax.dynamic_update_slice(dproj, piece.astype(BF16), (0, off))
        g_mi = mm(sv["h"], dproj, trans_a=True, name=f"mix_in_dw_{tag}", out_dtype=BF16)
        dh_ = mm(dproj, lw[l]["wmi_t"], name=f"mix_in_dx_{tag}")
        dxn, d_a, d_sh = rowwise(_f_pre_bwd, [dxo, dh_, sv["x"]], [vec_a(l, 1)], [((d,), F32)], [d, d],
                                 name=f"pre_bwd_{tag}")
        norm_grads(l, 1, d_a, d_sh, d_bv)
        g_mi = jnp.concatenate([g_mi[:, off_u:off_f + heads], g_mi[:, :off_u]], axis=1)
        return dxn, g_mi, g_glu, g_ao, g_mo, arrived

    def split_last(a):
        return jnp.moveaxis(a.reshape(a.shape[:-1] + (N_DEV, a.shape[-1] // N_DEV)), -2, 0)

    def split_rows(a):
        return jnp.moveaxis(a.reshape(a.shape[:-2] + (N_DEV, a.shape[-2] // N_DEV, a.shape[-1])), -3, 0)

    def owner_blocks(l, pieces):
        out = []
        for name, j in pieces:
            if name == "ffn_w_in":
                out.append(split_last(g_ffn_in[l][j]))
            elif name == "ffn_w_out":
                out.append(split_rows(g_ffn_out[l][j]))
            elif name == "mix_w_out":
                out.append(split_rows(grads[name][l]))
            else:
                out.append(split_last(grads[name][l]))
        return out

    g_ffn_in = [[None, None] for _ in range(depth)]
    g_ffn_out = [[None, None] for _ in range(depth)]
    parts = {}

    def record(l, pieces, arrived):
        for p, a in zip(pieces, arrived):
            parts[(p, l)] = a

    def last_ffn_blocks(g_in, g_out):
        return exchange_comm([split_last(g_in), split_rows(g_out)])

    for l in reversed(range(depth)):
        s0, s1, s2 = saved[l]
        dx, g_ffn_in[l][1], g_ffn_out[l][1], _, _ = ffn_bwd(dx, s2, l, 2, 1, f"l{l}b")
        pending = exchange_comm([owner_blocks(ll, [p])[0] for ll, p in riders(l)])
        (dx, grads["mix_w_in"][l], grads["glu_w"][l], grads["attn_w_out"][l], grads["mix_w_out"][l],
         arrived) = mixer_bwd(dx, s1, l, f"l{l}m", pending)
        for (ll, p), got in zip(riders(l), arrived):
            record(ll, [p], [got])
        if l == 0:
            dx, g_ffn_in[l][0], g_ffn_out[l][0], arrived_mix, arrived_ffn1 = ffn_bwd(
                dx, s0, l, 0, 0, f"l{l}a", exchange_comm(owner_blocks(0, mix)), last_ffn_blocks)
            record(0, mix, arrived_mix)
            record(0, ffn1, arrived_ffn1)
        else:
            dx, g_ffn_in[l][0], g_ffn_out[l][0], _, _ = ffn_bwd(dx, s0, l, 0, 0, f"l{l}a")
    grad_x = dx.reshape(x.shape)

    small_names = ["forget_b", "ssm_a_re", "ssm_a_im", "ssm_log_dt", "ssm_b_re", "ssm_b_im", "ssm_c_re",
                   "ssm_c_im", "ssm_d"]
    pieces = [loss_part.reshape(1), jnp.stack([jnp.stack(dmod[l]) for l in range(depth)]).reshape(-1),
              jnp.stack([jnp.stack(dgpre[l]) for l in range(depth)]).reshape(-1),
              jnp.stack([jnp.stack(dgpost[l]) for l in range(depth)]).reshape(-1)]
    pieces += [jnp.stack([small_g[l][k] for l in range(depth)]).reshape(-1) for k in small_names]
    sizes = [p.size for p in pieces]
    chunk = SUBLANES * 1024
    total = -(-sum(sizes) // chunk) * chunk
    pack = jnp.pad(jnp.concatenate(pieces), (0, total - sum(sizes))).reshape(total // 1024, 1024)
    pack_all = all_gather([pack], name="gather_small_grads")[0]
    pack_sum = rowwise(_f_sum_parts, [(pack_all, p) for p in range(N_DEV)], [], [((1024,), F32)],
                       name="sum_small_grads")[0].reshape(-1)
    offs = [0]
    for s_ in sizes:
        offs.append(offs[-1] + s_)
    take = lambda i: pack_sum[offs[i]:offs[i + 1]]
    loss = take(0).reshape(())
    g_small = {"mod_b": take(1).reshape(mod_b.shape)}
    g_pre_full, g_post_full = take(2).reshape(depth, 3, d), take(3).reshape(depth, 3, d)
    shard = norm_pre.shape[-1]
    g_small["norm_pre"] = lax.dynamic_slice_in_dim(g_pre_full, me * shard, shard, axis=2)
    g_small["norm_post"] = lax.dynamic_slice_in_dim(g_post_full, me * shard, shard, axis=2)
    for i, k in enumerate(small_names):
        g_small[k] = take(4 + i).reshape(w_in[k].shape)

    dmod_all = pack_all.reshape(N_DEV, -1)[:, offs[1]:offs[2]].reshape(N_DEV, depth, 9 * d)
    dmod_mine = lax.dynamic_slice_in_dim(dmod_all, me * mod_cols, mod_cols, axis=2)
    sct_pad = jnp.pad(sc_all.T, ((0, 0), (0, LANES - N_DEV)))
    g_mod_w = jnp.stack([
        mm(sct_pad, jnp.pad(dmod_mine[:, l], ((0, LANES - N_DEV), (0, 0))), name=f"mod_dw{l}")
        for l in range(depth)])

    out_g, out_d, out_m, out_v = {}, {}, {}, {}
    flat = lambda a: a.reshape(-1, a.shape[-1])
    res = adamw(g_mod_w.reshape(1, -1, mod_cols), flat(mod_w), flat(m_mod_w), flat(v_mod_w), name="adamw_mod_w")
    out_g["mod_w"], out_d["mod_w"], out_m["mod_w"], out_v["mod_w"] = [r.reshape(mod_w.shape) for r in res]
    for k in big:
        js = (0, 1) if k.startswith("ffn") else (None,)
        total_rows = w_in[k].size // w_in[k].shape[-1]
        res = None
        for l in range(depth):
            for j in js:
                p = (k, j)
                w_p = piece_of(w_in, p, l)
                row0 = (l * len(js) + (j or 0)) * (w_p.size // w_p.shape[-1])
                res = adamw(parts[(p, l)], flat(w_p), flat(piece_of(m_in, p, l)), flat(piece_of(v_in, p, l)),
                            name=f"adamw_{k}_l{l}" + ("" if j is None else f"_{j}"), into=(res, total_rows, row0))
        for dct, r in zip((out_g, out_d, out_m, out_v), res):
            dct[k] = r.reshape(w_in[k].shape)
    small_all = ["mod_b", "norm_pre", "norm_post"] + small_names

    def pack_small(dct):
        flat = jnp.concatenate([dct[k].reshape(-1) for k in small_all])
        tot = -(-flat.size // chunk) * chunk
        return jnp.pad(flat, (0, tot - flat.size)).reshape(tot // 1024, 1024)

    res = adamw(pack_small(g_small)[None], pack_small(w_in), pack_small(m_in), pack_small(v_in),
                name="adamw_small")
    pos = 0
    for k in small_all:
        size = w_in[k].size
        for dct, r in zip((out_g, out_d, out_m, out_v), res):
            dct[k] = r.reshape(-1)[pos:pos + size].reshape(w_in[k].shape)
        pos += size

    return (loss, grad_x, *[out_g[k] for k in names], *[out_d[k] for k in names],
            *[out_m[k] for k in names], *[out_v[k] for k in names])
```

```python
import functools
import math

import jax
import jax.numpy as jnp
from jax import lax
from jax.experimental import pallas as pl
from jax.experimental.pallas import tpu as pltpu

F32 = jnp.float32
BF16 = jnp.bfloat16
MESH = pl.DeviceIdType.MESH
N_DEV = 8
LANES = 128
SUBLANES = 8
VMEM_LIMIT = 48 * 1024 * 1024

RMS_EPS = 1e-6
FFN_RES = 0.5
ADAM_LR = 0.001
ADAM_B1 = 0.9
ADAM_B2 = 0.999
ADAM_EPS = 1e-08
ADAM_WD = 0.01
ADAM_STEP = 10
GELU_C = math.sqrt(2.0 / math.pi)
GELU_A = 0.044715


def _pick(dim, target, mult=LANES):
    t = (min(dim, target) // mult) * mult
    while t >= mult:
        if dim % t == 0:
            return t
        t -= mult
    return dim


def _params(sem):
    return pltpu.CompilerParams(dimension_semantics=sem, vmem_limit_bytes=VMEM_LIMIT)


def _sigmoid(x):
    return 1.0 / (1.0 + jnp.exp(-x))


def mm(a, b, *, name, trans_a=False, trans_b=False, out_dtype=F32, tm=1024, tn=1408, tk=2816, comm=None):
    if trans_a:
        kdim, m = a.shape
    else:
        m, kdim = a.shape
    if trans_b:
        n, kb = b.shape
    else:
        kb, n = b.shape
    assert kdim == kb, (a.shape, b.shape)
    tm, tn, tk = _pick(m, tm), _pick(n, tn), _pick(kdim, tk)
    gm, gn, nk = m // tm, n // tn, kdim // tk
    dims = (((0 if trans_a else 1,), (1 if trans_b else 0,)), ((), ()))
    host = _Hosted(comm, 2, 1, 1 if nk > 1 else 0)

    def body(*refs):
        (a_ref, b_ref, o_ref, *acc), crefs = host.split(refs)
        i, j, k = pl.program_id(0), pl.program_id(1), pl.program_id(2)
        host.phase("start", (i == 0) & (j == 0) & (k == 0), crefs)
        prod = lax.dot_general(a_ref[...].astype(BF16), b_ref[...].astype(BF16), dims,
                               preferred_element_type=F32)
        if nk == 1:
            o_ref[...] = prod.astype(out_dtype)
        else:
            acc_ref, = acc

            @pl.when(k == 0)
            def _():
                acc_ref[...] = prod

            @pl.when((k > 0) & (k < nk - 1))
            def _():
                acc_ref[...] += prod

            @pl.when(k == nk - 1)
            def _():
                o_ref[...] = (acc_ref[...] + prod).astype(out_dtype)

        last = (i == gm - 1) & (j == gn - 1) & (k == nk - 1)
        host.phase("mid", last, crefs)
        host.phase("finish", last, crefs)

    a_spec = (pl.BlockSpec((tk, tm), lambda i, j, k: (k, i)) if trans_a
              else pl.BlockSpec((tm, tk), lambda i, j, k: (i, k)))
    b_spec = (pl.BlockSpec((tn, tk), lambda i, j, k: (j, k)) if trans_b
              else pl.BlockSpec((tk, tn), lambda i, j, k: (k, j)))
    in_specs, out_specs, out_shape, scratch, extra = host.specs(
        [a_spec, b_spec], [pl.BlockSpec((tm, tn), lambda i, j, k: (i, j))],
        [jax.ShapeDtypeStruct((m, n), out_dtype)], [pltpu.VMEM((tm, tn), F32)] if nk > 1 else [])
    res = pl.pallas_call(
        body, name=name, grid=(gm, gn, nk),
        in_specs=in_specs, out_specs=out_specs, out_shape=out_shape, scratch_shapes=scratch,
        compiler_params=_params(("arbitrary", "arbitrary", "arbitrary")),
    )(a, b, *extra)
    return res if comm else res[0]


def ffn_in_swiglu(h, w_t, *, name, comm=None, tm=1024, tn=1408):
    m, kdim = h.shape
    f = w_t.shape[0] // 2
    tm, tn = _pick(m, tm), _pick(f, tn)
    gm, gn = m // tm, f // tn
    host = _Hosted(comm, 3, 3, 0)

    def body(*refs):
        (h_ref, wg_ref, wu_ref, g_ref, u_ref, m_ref), crefs = host.split(refs)
        i, j = pl.program_id(0), pl.program_id(1)
        host.phase("start", (i == 0) & (j == 0), crefs)
        hv = h_ref[...]
        g = lax.dot_general(hv, wg_ref[...], (((1,), (1,)), ((), ())), preferred_element_type=F32)
        u = lax.dot_general(hv, wu_ref[...], (((1,), (1,)), ((), ())), preferred_element_type=F32)
        g_ref[...] = g.astype(BF16)
        u_ref[...] = u.astype(BF16)
        m_ref[...] = ((g * _sigmoid(g)) * u).astype(BF16)
        last = (i == gm - 1) & (j == gn - 1)
        host.phase("mid", last, crefs)
        host.phase("finish", last, crefs)

    out_spec = pl.BlockSpec((tm, tn), lambda i, j: (i, j))
    in_specs, out_specs, out_shape, scratch, extra = host.specs(
        [pl.BlockSpec((tm, kdim), lambda i, j: (i, 0)), pl.BlockSpec((tn, kdim), lambda i, j: (j, 0)),
         pl.BlockSpec((tn, kdim), lambda i, j: (j + gn, 0))],
        [out_spec] * 3, [jax.ShapeDtypeStruct((m, f), BF16)] * 3, [])
    return pl.pallas_call(
        body, name=name, grid=(gm, gn),
        in_specs=in_specs, out_specs=out_specs, out_shape=out_shape, scratch_shapes=scratch,
        compiler_params=_params(("arbitrary", "arbitrary")),
    )(h, w_t, w_t, *extra)


def mm_blockdiag(a, b, *, name, a_cb0=0, out_dtype=F32, tm=512):
    m = a.shape[0]
    g_n, ka, nb = b.shape
    tm = _pick(m, tm)
    assert a_cb0 % g_n == 0

    def body(a_ref, b_ref, o_ref):
        for g in range(g_n):
            o_ref[:, g * nb:(g + 1) * nb] = jnp.dot(
                a_ref[:, g * ka:(g + 1) * ka].astype(BF16), b_ref[g].astype(BF16),
                preferred_element_type=F32).astype(out_dtype)

    return pl.pallas_call(
        body, name=name, grid=(m // tm,),
        in_specs=[pl.BlockSpec((tm, g_n * ka), lambda i: (i, a_cb0 // g_n)),
                  pl.BlockSpec((g_n, ka, nb), lambda i: (0, 0, 0))],
        out_specs=pl.BlockSpec((tm, g_n * nb), lambda i: (i, 0)),
        out_shape=jax.ShapeDtypeStruct((m, g_n * nb), out_dtype),
        compiler_params=_params(("parallel",)),
    )(a, b)


def mm_blockdiag_tn(a, b, *, name, g_n, ka, kb, a_cb0=0, b_cb0=0, tk=512):
    rows = a.shape[0]
    tk = _pick(rows, tk)
    nk = rows // tk
    assert a_cb0 % g_n == 0 and b_cb0 % g_n == 0

    def body(a_ref, b_ref, o_ref):
        @pl.when(pl.program_id(0) == 0)
        def _():
            o_ref[...] = jnp.zeros_like(o_ref)

        for g in range(g_n):
            o_ref[g] += lax.dot_general(a_ref[:, g * ka:(g + 1) * ka].astype(BF16),
                                        b_ref[:, g * kb:(g + 1) * kb].astype(BF16),
                                        (((0,), (0,)), ((), ())), preferred_element_type=F32)

    return pl.pallas_call(
        body, name=name, grid=(nk,),
        in_specs=[pl.BlockSpec((tk, g_n * ka), lambda k: (k, a_cb0 // g_n)),
                  pl.BlockSpec((tk, g_n * kb), lambda k: (k, b_cb0 // g_n))],
        out_specs=pl.BlockSpec((g_n, ka, kb), lambda k: (0, 0, 0)),
        out_shape=jax.ShapeDtypeStruct((g_n, ka, kb), F32),
        compiler_params=_params(("arbitrary",)),
    )(a, b)


def rowwise(fn, rows, vecs, outs, reds=(), *, name, tm=512):
    metas = []
    for r in rows:
        if isinstance(r, tuple) and len(r) == 3:
            metas.append(("col", r[0], r[1], r[2]))
        elif isinstance(r, tuple):
            metas.append(("lead", r[0], r[0].shape[2], r[1]))
        else:
            metas.append(("full", r, r.shape[1], 0))
    n_rows = metas[0][1].shape[1] if metas[0][0] == "lead" else metas[0][1].shape[0]
    rc = 16 if n_rows % 16 == 0 else (SUBLANES if n_rows % SUBLANES == 0 else n_rows)
    tm = _pick(n_rows, tm, rc)
    n_inner = tm // rc
    windows = [(o[2] if len(o) > 2 else None) for o in outs]
    bases = [(k, o[3]) for k, o in enumerate(outs) if len(o) > 3 and o[3] is not None]
    outs = [(o[0], o[1]) for o in outs]
    nr, nv, no, nbase = len(metas), len(vecs), len(outs), len(bases)

    def body(*refs):
        row_refs, vec_refs = refs[:nr], refs[nr:nr + nv]
        refs = refs[nr + nv + nbase:]
        out_refs, red_refs = refs[:no], refs[no:]
        if reds:
            @pl.when(pl.program_id(0) == 0)
            def _():
                for rr in red_refs:
                    rr[...] = jnp.zeros_like(rr)
        vec_vals = [v[...] for v in vec_refs]

        def step(s, carry):
            r0 = pl.multiple_of(s * rc, rc)
            vals = [ref[pl.ds(r0, rc), :] for ref in row_refs]
            res = fn(*vals, *vec_vals)
            if not isinstance(res, (tuple, list)):
                res = (res,)
            for o_ref, (widths, dt), val in zip(out_refs, outs, res[:no]):
                pieces = val if isinstance(val, (tuple, list)) else (val,)
                off = 0
                for w_, piece in zip(widths, pieces):
                    o_ref[pl.ds(r0, rc), off:off + w_] = piece.astype(dt)
                    off += w_
            for rr, val in zip(red_refs, res[no:]):
                rr[...] += val
            return carry

        lax.fori_loop(0, n_inner, step, 0, unroll=min(n_inner, 4))

    in_specs = []
    for kind, arr, w_, idx in metas:
        if kind == "col":
            in_specs.append(pl.BlockSpec((tm, w_), functools.partial(lambda i, cb: (i, cb), cb=idx)))
        elif kind == "lead":
            in_specs.append(pl.BlockSpec((None, tm, w_), functools.partial(lambda i, p: (p, i, 0), p=idx)))
        else:
            in_specs.append(pl.BlockSpec((tm, w_), lambda i: (i, 0)))
    for v in vecs:
        in_specs.append(pl.BlockSpec(v.shape, lambda i: (0, 0)))
    in_specs += [pl.BlockSpec(memory_space=pl.ANY)] * nbase
    out_specs, out_shape = [], []
    for (ws, dt), win in zip(outs, windows):
        total, cb, total_rows, row0 = (tuple(win) + (n_rows, 0))[:4] if win is not None else (sum(ws), 0, n_rows, 0)
        assert row0 % tm == 0
        out_specs.append(pl.BlockSpec((tm, sum(ws)),
                                      functools.partial(lambda i, cb, rb: (i + rb, cb), cb=cb, rb=row0 // tm)))
        out_shape.append(jax.ShapeDtypeStruct((total_rows, total), dt))
    out_specs += [pl.BlockSpec((1, w_), lambda i: (0, 0)) for w_ in reds]
    out_shape += [jax.ShapeDtypeStruct((1, w_), F32) for w_ in reds]
    res = pl.pallas_call(
        body, name=name, grid=(n_rows // tm,),
        in_specs=in_specs, out_specs=out_specs, out_shape=out_shape,
        input_output_aliases={nr + nv + b: k for b, (k, _) in enumerate(bases)},
        compiler_params=_params(("arbitrary",)),
    )(*[m[1] for m in metas], *vecs, *[b for _, b in bases])
    return res


def _rms(x):
    return lax.rsqrt(jnp.mean(x * x, axis=-1, keepdims=True) + RMS_EPS)


def _colsum(x):
    return jnp.sum(x, axis=0, keepdims=True)


def _f_silu(c):
    return c * _sigmoid(c)


def _f_pre(x, a, sh):
    return (x * _rms(x)) * a + sh


def _f_post_add(x, y, bv):
    return x + (y * _rms(y)) * bv


def _f_post_bwd(dxo, y, bv):
    ry = _rms(y)
    yn = y * ry
    dyn = dxo * bv
    dy = ry * (dyn - yn * jnp.mean(dyn * yn, axis=-1, keepdims=True))
    return dy, _colsum(dxo * yn)


def _f_pre_bwd(dxo, dh, x, a):
    r = _rms(x)
    xn = x * r
    dxn = dh * a
    dx = dxo + r * (dxn - xn * jnp.mean(dxn * xn, axis=-1, keepdims=True))
    return dx, _colsum(dh * xn), _colsum(dh)


def _f_swiglu_bwd(g, u, dm):
    g, u, dm = g.astype(F32), u.astype(F32), dm.astype(F32)
    sg = _sigmoid(g)
    dg = dm * u * (sg * (1.0 + g * (1.0 - sg)))
    du = dm * (g * sg)
    return ((dg, du),)


def _gelu_t(x):
    return jnp.tanh(GELU_C * (x + GELU_A * x * x * x))


def _f_gelu_in(y0, u, dvec):
    y = y0 + dvec * u
    return 0.5 * y * (1.0 + _gelu_t(y)), y


def _f_gelu_bwd(dge, y, u):
    t = _gelu_t(y)
    dy = dge * (0.5 * (1.0 + t) + 0.5 * y * (1.0 - t * t) * GELU_C * (1.0 + 3.0 * GELU_A * y * y))
    return dy, _colsum(dy * u)


def _f_du_fin(du0, dys, dvec):
    return du0 + dvec * dys.astype(F32)


def _f_merge(zv, zg, yb, ga, gb):
    zv, zg, yb, ga, gb = [a.astype(F32) for a in (zv, zg, yb, ga, gb)]
    return _sigmoid(ga) * (zv * _sigmoid(zg)) + _sigmoid(gb) * yb


def _f_merge_bwd(dmg, zv, zg, yb, ga, gb):
    zv, zg, yb, ga, gb = [a.astype(F32) for a in (zv, zg, yb, ga, gb)]
    sa, sb, sz = _sigmoid(ga), _sigmoid(gb), _sigmoid(zg)
    ya = zv * sz
    dya = dmg * sa
    dga = dmg * ya * sa * (1.0 - sa)
    dyb = dmg * sb
    dgb = dmg * yb * sb * (1.0 - sb)
    dzv = dya * sz
    dzg = dya * zv * sz * (1.0 - sz)
    return (dzv, dzg), dyb, (dga, dgb)


def _f_sum_parts(*parts):
    acc = parts[0].astype(F32)
    for p in parts[1:]:
        acc = acc + p.astype(F32)
    return acc


def _f_adamw(*args):
    parts, (w, m, v) = args[:-3], args[-3:]
    g = _f_sum_parts(*parts)
    m = ADAM_B1 * m + (1.0 - ADAM_B1) * g
    v = ADAM_B2 * v + (1.0 - ADAM_B2) * (g * g)
    m_hat = m / (1.0 - ADAM_B1 ** ADAM_STEP)
    v_hat = v / (1.0 - ADAM_B2 ** ADAM_STEP)
    delta = -ADAM_LR * (m_hat / (jnp.sqrt(v_hat) + ADAM_EPS) + ADAM_WD * w)
    return g, delta, m, v


def adamw(parts3, w, m, v, *, name, into=None):
    c = w.shape[1]
    rows = [(parts3, p) for p in range(parts3.shape[0])] + [w, m, v]
    if into is None:
        return rowwise(_f_adamw, rows, [], [((c,), F32)] * 4, name=name)
    bases, total_rows, row0 = into
    outs = [((c,), F32, (c, 0, total_rows, row0), bases[k] if bases else None) for k in range(4)]
    return rowwise(_f_adamw, rows, [], outs, name=name)


def _tri_dot(tri, x, dims=(((1,), (0,)), ((), ()))):
    x1 = x.astype(BF16)
    r1 = x - x1.astype(F32)
    x2 = r1.astype(BF16)
    x3 = (r1 - x2.astype(F32)).astype(BF16)
    dot = functools.partial(lax.dot_general, dimension_numbers=dims, preferred_element_type=F32)
    return dot(tri, x1) + dot(tri, x2) + dot(tri, x3)


def cum_fwd(proj, fb, *, col_block, name, t=256):
    n = proj.shape[0]
    t = _pick(n, t, SUBLANES)

    def body(f_ref, fb_ref, cum_ref, car_ref):
        @pl.when(pl.program_id(0) == 0)
        def _():
            car_ref[...] = jnp.zeros_like(car_ref)

        x = f_ref[...] + fb_ref[...]
        lf = jnp.minimum(x, 0.0) - jnp.log(1.0 + jnp.exp(-jnp.abs(x)))
        r = lax.broadcasted_iota(jnp.int32, (t, t), 0)
        c = lax.broadcasted_iota(jnp.int32, (t, t), 1)
        cs = _tri_dot((c <= r).astype(BF16), lf) + car_ref[0:1, :]
        cum_ref[...] = cs
        car_ref[0:1, :] = cs[t - 1:t, :]

    return pl.pallas_call(
        body, name=name, grid=(n // t,),
        in_specs=[pl.BlockSpec((t, LANES), lambda i: (i, col_block)),
                  pl.BlockSpec((1, LANES), lambda i: (0, 0))],
        out_specs=pl.BlockSpec((t, LANES), lambda i: (i, 0)),
        out_shape=jax.ShapeDtypeStruct((n, LANES), F32),
        scratch_shapes=[pltpu.VMEM((SUBLANES, LANES), F32)],
        compiler_params=_params(("arbitrary",)),
    )(proj, fb)


def cum_bwd(dcum, proj, fb, *, col_block, name, t=256):
    n = proj.shape[0]
    slabs = dcum.shape[0]
    t = _pick(n, t, SUBLANES)
    nb = n // t

    def body(dc_ref, f_ref, fb_ref, df_ref, dfb_ref, car_ref):
        @pl.when(pl.program_id(0) == 0)
        def _():
            car_ref[...] = jnp.zeros_like(car_ref)
            dfb_ref[...] = jnp.zeros_like(dfb_ref)

        r = lax.broadcasted_iota(jnp.int32, (t, t), 0)
        c = lax.broadcasted_iota(jnp.int32, (t, t), 1)
        dl = _tri_dot((c >= r).astype(BF16), jnp.sum(dc_ref[...], axis=0)) + car_ref[0:1, :]
        car_ref[0:1, :] = dl[0:1, :]
        x = f_ref[...] + fb_ref[...]
        df = dl * (1.0 / (1.0 + jnp.exp(x)))
        df_ref[...] = df
        dfb_ref[...] += _colsum(df)

    return pl.pallas_call(
        body, name=name, grid=(nb,),
        in_specs=[pl.BlockSpec((slabs, t, LANES), lambda i: (0, nb - 1 - i, 0)),
                  pl.BlockSpec((t, LANES), lambda i: (nb - 1 - i, col_block)),
                  pl.BlockSpec((1, LANES), lambda i: (0, 0))],
        out_specs=[pl.BlockSpec((t, LANES), lambda i: (nb - 1 - i, 0)),
                   pl.BlockSpec((1, LANES), lambda i: (0, 0))],
        out_shape=[jax.ShapeDtypeStruct((n, LANES), F32), jax.ShapeDtypeStruct((1, LANES), F32)],
        scratch_shapes=[pltpu.VMEM((SUBLANES, LANES), F32)],
        compiler_params=_params(("arbitrary",)),
    )(dcum, proj, fb)


_NT =(((1,), (1,)), ((), ()))
_TN = (((0,), (0,)), ((), ()))


def _lane_sums_as_row(x):
    return _tri_dot(jnp.ones((SUBLANES, x.shape[1]), BF16), x, _NT)[0:1, :]


def _causal_keep(t):
    return lax.broadcasted_iota(jnp.int32, (t, t), 1) <= lax.broadcasted_iota(jnp.int32, (t, t), 0)


def attn_fwd(q, k, v, crow, *, name, t=512, hb=8, comm=None):
    h_n, n, dh = q.shape
    hb = min(hb, h_n)
    t = _pick(n, t)
    nb = n // t
    ng = h_n // hb
    host = _Hosted(comm, 4, 2, 3)

    def body(*refs):
        (q_ref, k_ref, v_ref, cr_ref, o_ref, lse_ref, m_sc, l_sc, acc_sc), crefs = host.split(refs)
        g, i, j = pl.program_id(0), pl.program_id(1), pl.program_id(2)
        host.phase("start", (g == 0) & (i == 0) & (j == 0), crefs)
        host.phase("mid", (g == ng - 1) & (i == (3 * nb) // 4) & (j == 0), crefs)

        @pl.when(j == 0)
        def _():
            m_sc[...] = jnp.full_like(m_sc, -jnp.inf)
            l_sc[...] = jnp.zeros_like(l_sc)
            acc_sc[...] = jnp.zeros_like(acc_sc)

        def update(diagonal):
            keep = _causal_keep(t) if diagonal else None
            heads = range(hb)
            ss = [lax.dot_general(q_ref[h], k_ref[h], _NT, preferred_element_type=F32) for h in heads]
            pairs, alphas = [], []
            for h in heads:
                s = ss[h] - cr_ref[h]
                if diagonal:
                    s = jnp.where(keep, s, -jnp.inf)
                m_prev = m_sc[h]
                m_new = jnp.maximum(m_prev, jnp.max(s, axis=-1, keepdims=True))
                p = jnp.exp(s - m_new)
                alpha = jnp.exp(m_prev - m_new)
                l_sc[h] = alpha * l_sc[h] + jnp.sum(p, axis=-1, keepdims=True)
                m_sc[h] = m_new
                p_hi = p.astype(BF16)
                pairs.append((p_hi, (p - p_hi.astype(F32)).astype(BF16)))
                alphas.append(alpha)
            for h in heads:
                vv = v_ref[h]
                acc_sc[h] = (alphas[h] * acc_sc[h] + jnp.dot(pairs[h][0], vv, preferred_element_type=F32)
                             + jnp.dot(pairs[h][1], vv, preferred_element_type=F32))

        @pl.when(j < i)
        def _():
            update(False)

        @pl.when(j == i)
        def _():
            update(True)

        @pl.when(j == nb - 1)
        def _():
            o_ref[...] = acc_sc[...] / l_sc[...]
            lane0 = lax.broadcasted_iota(jnp.int32, (t, LANES), 1) == 0
            for h in range(hb):
                lse_col = m_sc[h] + jnp.log(l_sc[h])
                lse_ref[h] = _lane_sums_as_row(jnp.where(lane0, lse_col, 0.0))

        host.phase("finish", (g == ng - 1) & (i == nb - 1) & (j == nb - 1), crefs)

    qspec = pl.BlockSpec((hb, t, dh), lambda g, i, j: (g, i, 0))
    kspec = pl.BlockSpec((hb, t, dh), lambda g, i, j: (g, jnp.minimum(j, i), 0))
    in_specs, out_specs, out_shape, scratch, extra = host.specs(
        [qspec, kspec, kspec, pl.BlockSpec((hb, 1, t), lambda g, i, j: (g, 0, jnp.minimum(j, i)))],
        [qspec, pl.BlockSpec((hb, 1, t), lambda g, i, j: (g, 0, i))],
        [jax.ShapeDtypeStruct((h_n, n, dh), F32), jax.ShapeDtypeStruct((h_n, 1, n), F32)],
        [pltpu.VMEM((hb, t, 1), F32), pltpu.VMEM((hb, t, 1), F32), pltpu.VMEM((hb, t, dh), F32)])
    return pl.pallas_call(
        body, name=name, grid=(ng, nb, nb),
        in_specs=in_specs, out_specs=out_specs, out_shape=out_shape, scratch_shapes=scratch,
        compiler_params=_params(("arbitrary", "arbitrary", "arbitrary")),
    )(q, k, v, crow, *extra)


def attn_delta(do, o, *, name, t=512, hb=8):
    h_n, n, dh = do.shape
    hb = min(hb, h_n)
    t = _pick(n, t)

    def body(do_ref, o_ref, dl_ref):
        for h in range(hb):
            dl_ref[h] = _lane_sums_as_row(do_ref[h].astype(F32) * o_ref[h])

    spec = pl.BlockSpec((hb, t, dh), lambda g, i: (g, i, 0))
    return pl.pallas_call(
        body, name=name, grid=(h_n // hb, n // t),
        in_specs=[spec, spec], out_specs=pl.BlockSpec((hb, 1, t), lambda g, i: (g, 0, i)),
        out_shape=jax.ShapeDtypeStruct((h_n, 1, n), F32),
        compiler_params=_params(("parallel", "parallel")),
    )(do, o)


def attn_bwd(q, k, v, do, o, lse, cum, *, scale, tag, t=512, hb=2, comm=None):
    h_n, n, dh = q.shape
    hb = min(hb, h_n)
    t = _pick(n, t)
    nb = n // t
    dob = do.astype(BF16)
    delta = attn_delta(dob, o, name=f"attn_delta_{tag}", t=t)

    ng = h_n // hb
    host = _Hosted(comm, 7, 4, 3)

    def body(*refs):
        (q_ref, k_ref, v_ref, do_ref, lse_ref, dl_ref, cum_ref,
         dq_ref, dk_ref, dv_ref, dcum_ref, dk_acc, dv_acc, dcc_acc), crefs = host.split(refs)
        g, j, i = pl.program_id(0), pl.program_id(1), pl.program_id(2)
        host.phase("start", (g == 0) & (j == 0) & (i == 0), crefs)
        lane = lax.broadcasted_iota(jnp.int32, (t, LANES), 1)

        @pl.when((j == 0) & (i == 0))
        def _():
            dq_ref[...] = jnp.zeros_like(dq_ref)

        @pl.when(i == 0)
        def _():
            dk_acc[...] = jnp.zeros_like(dk_acc)
            dv_acc[...] = jnp.zeros_like(dv_acc)
            dcc_acc[...] = jnp.zeros_like(dcc_acc)

        def update(diagonal):
            heads = range(hb)
            r0 = pl.multiple_of(i * t, t)
            if diagonal:
                keep = lax.broadcasted_iota(jnp.int32, (t, t), 0) <= lax.broadcasted_iota(jnp.int32, (t, t), 1)
            qv, kv = [q_ref[h] for h in heads], [k_ref[h] for h in heads]
            vv, dov = [v_ref[h] for h in heads], [do_ref[h] for h in heads]
            st = [lax.dot_general(kv[h], qv[h], _NT, preferred_element_type=F32) for h in heads]
            dpt = [lax.dot_general(vv[h], dov[h], _NT, preferred_element_type=F32) for h in heads]
            pt = []
            cum_tile = cum_ref[...]
            for h in heads:
                cc = jnp.sum(jnp.where(lane == g * hb + h, cum_tile, 0.0), axis=1, keepdims=True)
                s = st[h] - cc
                if diagonal:
                    s = jnp.where(keep, s, -jnp.inf)
                pt.append(jnp.exp(s - lse_ref[h]))
            for h in heads:
                dv_acc[h] += jnp.dot(pt[h].astype(BF16), dov[h], preferred_element_type=F32)
            dsb = []
            for h in heads:
                ds = pt[h] * (dpt[h] - dl_ref[h])
                dcc_acc[h] -= jnp.sum(ds, axis=1, keepdims=True)
                dsb.append(ds.astype(BF16))
            for h in heads:
                dk_acc[h] += jnp.dot(dsb[h], qv[h], preferred_element_type=F32)
            for h in heads:
                dq_ref[h, pl.ds(r0, t), :] += lax.dot_general(dsb[h], kv[h], _TN,
                                                              preferred_element_type=F32) * scale

        @pl.when(i > j)
        def _():
            update(False)

        @pl.when(i == j)
        def _():
            update(True)

        @pl.when(i == nb - 1)
        def _():
            dk_ref[...] = dk_acc[...]
            dv_ref[...] = dv_acc[...]
            tile = jnp.zeros((t, LANES), F32)
            for h in range(hb):
                tile = tile + jnp.where(lane == g * hb + h, dcc_acc[h], 0.0)
            dcum_ref[...] = tile

        host.phase("finish", (g == ng - 1) & (j == nb - 1) & (i == nb - 1), crefs)

    qspec = pl.BlockSpec((hb, t, dh), lambda g, j, i: (g, jnp.maximum(i, j), 0))
    qrow = pl.BlockSpec((hb, 1, t), lambda g, j, i: (g, 0, jnp.maximum(i, j)))
    kspec = pl.BlockSpec((hb, t, dh), lambda g, j, i: (g, j, 0))
    in_specs, out_specs, out_shape, scratch, extra = host.specs(
        [qspec, kspec, kspec, qspec, qrow, qrow, pl.BlockSpec((t, LANES), lambda g, j, i: (j, 0))],
        [pl.BlockSpec((hb, n, dh), lambda g, j, i: (g, 0, 0)), kspec, kspec,
         pl.BlockSpec((None, t, LANES), lambda g, j, i: (g, j, 0))],
        [jax.ShapeDtypeStruct((h_n, n, dh), F32)] * 3 + [jax.ShapeDtypeStruct((ng, n, LANES), F32)],
        [pltpu.VMEM((hb, t, dh), F32), pltpu.VMEM((hb, t, dh), F32), pltpu.VMEM((hb, t, 1), F32)])
    dq, dk, dv, dcum, *arrived = pl.pallas_call(
        body, name=f"attn_bwd_{tag}", grid=(ng, nb, nb),
        in_specs=in_specs, out_specs=out_specs, out_shape=out_shape, scratch_shapes=scratch,
        compiler_params=_params(("arbitrary", "arbitrary", "arbitrary")),
    )(q, k, v, dob, lse, delta, cum, *extra)
    return [dq, dk, dv, dcum] + arrived


SCAN_STEPS = (1, 2, 4)


def ssm_scan(x, tab, *, reverse, name, s_prev=None, tt=512, out_dtype=BF16):
    n, width = x.shape
    nc, _, hw = tab.shape
    cw = 2 * hw
    assert width == nc * cw
    tt = _pick(n, tt, 2 * SUBLANES)
    nt = n // tt
    ng = tt // (2 * SUBLANES)
    with_grad = s_prev is not None

    def body(*refs):
        if with_grad:
            x_ref, s_ref, tab_ref, o_ref, g_ref, car_ref = refs
        else:
            x_ref, tab_ref, o_ref, car_ref = refs

        @pl.when(pl.program_id(1) == 0)
        def _():
            car_ref[...] = jnp.zeros_like(car_ref)
            if with_grad:
                g_ref[...] = jnp.zeros_like(g_ref)

        q_re, q_im = tab_ref[0:8, :], tab_ref[8:16, :]
        p_re = [tab_ref[16 + 16 * i:24 + 16 * i, :] for i in range(3)]
        p_im = [tab_ref[24 + 16 * i:32 + 16 * i, :] for i in range(3)]
        row = lax.broadcasted_iota(jnp.int32, (SUBLANES, hw), 0)

        def group(xr, xi, sr_, si_, carry):
            c_re, c_im = carry
            for i, d in enumerate(SCAN_STEPS):
                shift = SUBLANES - d if reverse else d
                sr, si = pltpu.roll(xr, shift, 0), pltpu.roll(xi, shift, 0)
                xr, xi = (xr + p_re[i] * sr - p_im[i] * si,
                          xi + p_re[i] * si + p_im[i] * sr)
            xr, xi = (xr + q_re * c_re - q_im * c_im,
                      xi + q_re * c_im + q_im * c_re)
            if with_grad:
                nr = jnp.where(row < SUBLANES - 1, pltpu.roll(xr, SUBLANES - 1, 0), c_re)
                ni = jnp.where(row < SUBLANES - 1, pltpu.roll(xi, SUBLANES - 1, 0), c_im)
                g_ref[:, 0:hw] += nr * sr_ + ni * si_
                g_ref[:, hw:cw] += ni * sr_ - nr * si_
            if reverse:
                return xr, xi, (xr[0:1, :], xi[0:1, :])
            return xr, xi, (xr[SUBLANES - 1:SUBLANES, :], xi[SUBLANES - 1:SUBLANES, :])

        def pair(gi, carry):
            g = (ng - 1 - gi) if reverse else gi
            r0 = pl.multiple_of(g * 2 * SUBLANES, 2 * SUBLANES)
            rows = pl.ds(r0, 2 * SUBLANES)
            xr, xi = x_ref[rows, 0:hw].astype(F32), x_ref[rows, hw:cw].astype(F32)
            if with_grad:
                sr, si = s_ref[rows, 0:hw].astype(F32), s_ref[rows, hw:cw].astype(F32)
            halves = [slice(0, SUBLANES), slice(SUBLANES, 2 * SUBLANES)]
            done = [None, None]
            for k in ((1, 0) if reverse else (0, 1)):
                h = halves[k]
                o_re, o_im, carry = group(xr[h], xi[h], sr[h] if with_grad else None,
                                          si[h] if with_grad else None, carry)
                done[k] = (o_re, o_im)
            o_ref[rows, 0:hw] = jnp.concatenate([done[0][0], done[1][0]], axis=0).astype(o_ref.dtype)
            o_ref[rows, hw:cw] = jnp.concatenate([done[0][1], done[1][1]], axis=0).astype(o_ref.dtype)
            return carry

        c_re, c_im = lax.fori_loop(0, ng, pair, (car_ref[0:1, 0:hw], car_ref[0:1, hw:cw]),
                                   unroll=min(ng, 2))
        car_ref[0:1, 0:hw] = c_re
        car_ref[0:1, hw:cw] = c_im

    if reverse:
        xspec = pl.BlockSpec((tt, cw), lambda c, t: (nt - 1 - t, c))
    else:
        xspec = pl.BlockSpec((tt, cw), lambda c, t: (t, c))
    tspec = pl.BlockSpec((None, 64, hw), lambda c, t: (c, 0, 0))
    in_specs = [xspec, xspec, tspec] if with_grad else [xspec, tspec]
    out_specs = [xspec]
    out_shape = [jax.ShapeDtypeStruct((n, width), out_dtype)]
    if with_grad:
        out_specs.append(pl.BlockSpec((None, SUBLANES, cw), lambda c, t: (c, 0, 0)))
        out_shape.append(jax.ShapeDtypeStruct((nc, SUBLANES, cw), F32))
    operands = (x, s_prev, tab) if with_grad else (x, tab)
    return pl.pallas_call(
        body, name=name, grid=(nc, nt),
        in_specs=in_specs, out_specs=out_specs, out_shape=out_shape,
        scratch_shapes=[pltpu.VMEM((SUBLANES, cw), F32)],
        compiler_params=_params(("parallel", "arbitrary")),
    )(*operands)


def _slot(pos):
    return 4 * pos[0] + 2 * pos[1] + pos[2]


def _comm_scratch(n):
    return [pltpu.SemaphoreType.DMA((7 * n,)), pltpu.SemaphoreType.DMA((7 * n,)), pltpu.SemaphoreType.DMA((n,))]


def _gather_copies(ins, outs, sems):
    send_sems, recv_sems, local_sems = sems
    n = len(ins)
    x, y, c = lax.axis_index("x"), lax.axis_index("y"), lax.axis_index("c")
    me, sibling = (x, y, c), (x, y, 1 - c)
    chips = [(1 - x, y), (x, 1 - y), (1 - x, 1 - y)]

    def copy(t, k, block, to, src=None):
        dst = outs[t].at[_slot(block)]
        return pltpu.make_async_remote_copy(
            src_ref=dst if src is None else src, dst_ref=dst,
            send_sem=send_sems.at[7 * t + k], recv_sem=recv_sems.at[7 * t + k],
            device_id=to, device_id_type=MESH)

    jc = list(enumerate(chips))
    return dict(
        mine=[pltpu.make_async_copy(ins[t], outs[t].at[_slot(me)], local_sems.at[t]) for t in range(n)],
        first=[cp for t in range(n) for cp in
               [copy(t, 0, me, sibling, src=ins[t])] + [copy(t, 1 + j, me, (*chip, c), src=ins[t]) for j, chip in jc]],
        arrive=[copy(t, 1 + j, (*chip, c), me) for t in range(n) for j, chip in jc],
        passed=[copy(t, 4 + j, (*chip, c), sibling) for t in range(n) for j, chip in jc],
        from_sibling=[cp for t in range(n) for cp in
                      [copy(t, 0, sibling, me)] + [copy(t, 4 + j, (*chip, 1 - c), me) for j, chip in jc]])


def _gather_start(ins, outs, sems):
    cps = _gather_copies(ins, outs, sems)
    for cp in cps["mine"] + cps["first"]:
        cp.start()


def _gather_forward(ins, outs, sems):
    cps = _gather_copies(ins, outs, sems)
    for arrived, onward in zip(cps["arrive"], cps["passed"]):
        arrived.wait_recv()
        onward.start()


def _gather_finish(ins, outs, sems):
    cps = _gather_copies(ins, outs, sems)
    for cp in cps["from_sibling"]:
        cp.wait_recv()
    for cp in cps["first"] + cps["passed"]:
        cp.wait_send()
    for cp in cps["mine"]:
        cp.wait()


def gather_comm(arrs):
    return dict(ins=list(arrs), out_shape=[jax.ShapeDtypeStruct((N_DEV,) + a.shape, a.dtype) for a in arrs],
                scratch=_comm_scratch(len(arrs)), start=_gather_start, mid=_gather_forward, finish=_gather_finish)


def _exchange_copies(ins, outs, sems):
    send_sems, recv_sems, local_sems = sems
    n = len(ins)
    me = (lax.axis_index("x"), lax.axis_index("y"), lax.axis_index("c"))
    peers = []
    for k in range(1, N_DEV):
        flip = ((k >> 2) & 1, (k >> 1) & 1, k & 1)
        peers.append(tuple(1 - p if f else p for p, f in zip(me, flip)))

    def copy(t, k, peer, dst_slot):
        return pltpu.make_async_remote_copy(
            src_ref=ins[t].at[_slot(peer)], dst_ref=outs[t].at[dst_slot],
            send_sem=send_sems.at[7 * t + k], recv_sem=recv_sems.at[7 * t + k],
            device_id=peer, device_id_type=MESH)

    return dict(
        mine=[pltpu.make_async_copy(ins[t].at[_slot(me)], outs[t].at[_slot(me)], local_sems.at[t])
              for t in range(n)],
        send=[copy(t, k, peer, _slot(me)) for t in range(n) for k, peer in enumerate(peers)],
        both=[copy(t, k, peer, _slot(peer)) for t in range(n) for k, peer in enumerate(peers)])


def _exchange_start(ins, outs, sems):
    cps = _exchange_copies(ins, outs, sems)
    for cp in cps["mine"] + cps["send"]:
        cp.start()


def _exchange_finish(ins, outs, sems):
    cps = _exchange_copies(ins, outs, sems)
    for cp in cps["both"]:
        cp.wait()
    for cp in cps["mine"]:
        cp.wait()


def exchange_comm(arrs):
    return dict(ins=list(arrs), out_shape=[jax.ShapeDtypeStruct(a.shape, a.dtype) for a in arrs],
                scratch=_comm_scratch(len(arrs)), start=_exchange_start, mid=None, finish=_exchange_finish)


def run_comm(comm, *, name):
    n_in, n_out = len(comm["ins"]), len(comm["out_shape"])

    def body(*refs):
        ins, outs, sems = refs[:n_in], refs[n_in:n_in + n_out], refs[n_in + n_out:]
        comm["start"](ins, outs, sems)
        if comm["mid"] is not None:
            comm["mid"](ins, outs, sems)
        comm["finish"](ins, outs, sems)

    any_spec = pl.BlockSpec(memory_space=pl.ANY)
    return pl.pallas_call(
        body, name=name, in_specs=[any_spec] * n_in, out_specs=[any_spec] * n_out,
        out_shape=comm["out_shape"], scratch_shapes=comm["scratch"],
    )(*comm["ins"])


class _Hosted:
    def __init__(self, comm, n_in, n_out, n_scratch):
        self.comm = comm
        self.n_ci = len(comm["ins"]) if comm else 0
        self.n_co = len(comm["out_shape"]) if comm else 0
        self.n_in, self.n_out, self.n_scratch = n_in, n_out, n_scratch

    def split(self, refs):
        a = self.n_in
        b = a + self.n_ci
        c = b + self.n_out
        e = c + self.n_co
        f = e + self.n_scratch
        return refs[:a] + refs[b:c] + refs[e:f], (refs[a:b], refs[c:e], refs[f:])

    def phase(self, which, when, crefs):
        fn = self.comm[which] if self.comm else None
        if fn is not None:
            pl.when(when)(lambda: fn(*crefs))

    def specs(self, in_specs, out_specs, out_shape, scratch):
        any_spec = pl.BlockSpec(memory_space=pl.ANY)
        if not self.comm:
            return in_specs, out_specs, out_shape, scratch, ()
        return (in_specs + [any_spec] * self.n_ci, out_specs + [any_spec] * self.n_co,
                out_shape + self.comm["out_shape"], scratch + self.comm["scratch"], tuple(self.comm["ins"]))


def all_gather(arrs, *, name):
    return run_comm(gather_comm(arrs), name=name)


def _discretise(a_re, a_im, log_dt, b_re, b_im):
    ar = jnp.minimum(a_re, -1e-4)
    dt = jnp.exp(log_dt)[:, None]
    e, ph = ar * dt, a_im * dt
    mag = jnp.exp(e)
    lr, li = mag * jnp.cos(ph), mag * jnp.sin(ph)
    den = ar * ar + a_im * a_im
    nr, ni = lr - 1.0, li
    cr = (nr * ar + ni * a_im) / den
    ci = (ni * ar - nr * a_im) / den
    bb_re = cr[..., None] * b_re - ci[..., None] * b_im
    bb_im = cr[..., None] * b_im + ci[..., None] * b_re
    return e, ph, bb_re, bb_im


def _lam_pow(e, ph, k, conj):
    mag = jnp.exp(k * e)
    return mag * jnp.cos(k * ph), (-1.0 if conj else 1.0) * mag * jnp.sin(k * ph)


def _scan_table(e, ph, nc, reverse):
    hw = e.size // nc
    e, ph = e.reshape(nc, 1, hw), ph.reshape(nc, 1, hw)
    j = jnp.arange(SUBLANES, dtype=F32).reshape(1, SUBLANES, 1)
    kq = (SUBLANES - j) if reverse else (j + 1.0)
    parts = list(_lam_pow(e, ph, kq, reverse))
    for step in SCAN_STEPS:
        inside = (j < SUBLANES - step) if reverse else (j >= step)
        p_re, p_im = _lam_pow(e, ph, jnp.full_like(j, float(step)), reverse)
        parts += [jnp.where(inside, p_re, 0.0), jnp.where(inside, p_im, 0.0)]
    return jnp.concatenate(parts, axis=1)


def _blockdiag(m, nc):
    g, a, b = m.shape
    gc = g // nc
    m = m.reshape(nc, gc, a, b)
    eye = jnp.eye(gc, dtype=m.dtype)
    return jnp.einsum("cgab,gh->cgahb", m, eye).reshape(nc, gc * a, gc * b)


def _blockdiag_take(m, g):
    nc = m.shape[0]
    gc = g // nc
    a, b = m.shape[1] // gc, m.shape[2] // gc
    m = m.reshape(nc, gc, a, gc, b)
    eye = jnp.eye(gc, dtype=m.dtype)
    return jnp.einsum("cgahb,gh->cgab", m, eye).reshape(g, a, b)


def kernel(x, c, mod_w, mod_b, norm_pre, norm_post, ffn_w_in, ffn_w_out, mix_w_in, forget_b, ssm_a_re, ssm_a_im, ssm_log_dt, ssm_b_re, ssm_b_im, ssm_c_re, ssm_c_im, ssm_d, glu_w, attn_w_out, mix_w_out, loss_target, m_mod_w, m_mod_b, m_norm_pre, m_norm_post, m_ffn_w_in, m_ffn_w_out, m_mix_w_in, m_forget_b, m_ssm_a_re, m_ssm_a_im, m_ssm_log_dt, m_ssm_b_re, m_ssm_b_im, m_ssm_c_re, m_ssm_c_im, m_ssm_d, m_glu_w, m_attn_w_out, m_mix_w_out, v_mod_w, v_mod_b, v_norm_pre, v_norm_post, v_ffn_w_in, v_ffn_w_out, v_mix_w_in, v_forget_b, v_ssm_a_re, v_ssm_a_im, v_ssm_log_dt, v_ssm_b_re, v_ssm_b_im, v_ssm_c_re, v_ssm_c_im, v_ssm_d, v_glu_w, v_attn_w_out, v_mix_w_out):
    names = ["mod_w", "mod_b", "norm_pre", "norm_post", "ffn_w_in", "ffn_w_out", "mix_w_in", "forget_b",
             "ssm_a_re", "ssm_a_im", "ssm_log_dt", "ssm_b_re", "ssm_b_im", "ssm_c_re", "ssm_c_im", "ssm_d",
             "glu_w", "attn_w_out", "mix_w_out"]
    w_in = dict(zip(names, [mod_w, mod_b, norm_pre, norm_post, ffn_w_in, ffn_w_out, mix_w_in, forget_b,
                            ssm_a_re, ssm_a_im, ssm_log_dt, ssm_b_re, ssm_b_im, ssm_c_re, ssm_c_im, ssm_d,
                            glu_w, attn_w_out, mix_w_out]))
    m_in = dict(zip(names, [m_mod_w, m_mod_b, m_norm_pre, m_norm_post, m_ffn_w_in, m_ffn_w_out, m_mix_w_in,
                            m_forget_b, m_ssm_a_re, m_ssm_a_im, m_ssm_log_dt, m_ssm_b_re, m_ssm_b_im,
                            m_ssm_c_re, m_ssm_c_im, m_ssm_d, m_glu_w, m_attn_w_out, m_mix_w_out]))
    v_in = dict(zip(names, [v_mod_w, v_mod_b, v_norm_pre, v_norm_post, v_ffn_w_in, v_ffn_w_out, v_mix_w_in,
                            v_forget_b, v_ssm_a_re, v_ssm_a_im, v_ssm_log_dt, v_ssm_b_re, v_ssm_b_im,
                            v_ssm_c_re, v_ssm_c_im, v_ssm_d, v_glu_w, v_attn_w_out, v_mix_w_out]))

    depth = mod_w.shape[0]
    n_tok, d = x.shape[1], x.shape[2]
    ff = ffn_w_out.shape[2] * N_DEV
    heads = forget_b.shape[1]
    sw = ssm_d.shape[1]
    g_n, p_n, n_n = ssm_b_re.shape[1:]
    aw = attn_w_out.shape[1]
    dh = aw // heads
    iw = mix_w_in.shape[2] * N_DEV
    nc = sw // LANES
    hw = g_n * p_n // nc
    mod_cols = mod_w.shape[2]
    scale = dh ** -0.5
    assert iw == sw + 3 * aw + heads + 2 * d and heads <= LANES
    assert math.log2(scale).is_integer(), "q is pre-scaled in bf16: exact only for a power of two"
    off_u, off_q, off_f = 2 * d, 2 * d + sw, 2 * d + sw + 3 * aw
    iwp = off_f + LANES
    assert off_u % sw == 0 and off_q % aw == 0 and off_f % LANES == 0

    me = 4 * lax.axis_index("x") + 2 * lax.axis_index("y") + lax.axis_index("c")
    x2 = x.reshape(n_tok, d)
    tgt = loss_target.reshape(n_tok, d)

    silu_c = rowwise(_f_silu, [c], [], [((d,), F32)], name="silu_c")[0]
    big = ["ffn_w_in", "ffn_w_out", "mix_w_in", "glu_w", "attn_w_out", "mix_w_out"]
    ffn1 = [("ffn_w_in", 0), ("ffn_w_out", 0)]
    mix = [("mix_w_in", None), ("glu_w", None), ("attn_w_out", None), ("mix_w_out", None)]
    ffn2 = [("ffn_w_in", 1), ("ffn_w_out", 1)]

    def riders(l):
        nxt = [(l + 1, p) for p in ffn1 + mix] if l + 1 < depth else []
        return [(l, p) for p in ffn2] + nxt

    def piece_of(dct, piece, l):
        name, j = piece
        return dct[name][l] if j is None else dct[name][l][j]

    row_sharded = ("ffn_w_out", "mix_w_out")

    def send_shard(p, l):
        a = piece_of(w_in, p, l).astype(BF16)
        return a if p[0] in row_sharded else a.T

    def shards(l, pieces):
        return [send_shard(p, l) for p in pieces]

    cut = [0, sw, sw + aw, sw + 2 * aw, sw + 3 * aw, sw + 3 * aw + heads, sw + 3 * aw + heads + d, iw]
    lw = [dict(win_t=[None, None], wout=[None, None]) for _ in range(depth)]

    def install(l, pieces, gathered):
        for (name, j), g in zip(pieces, gathered):
            whole = g.reshape(-1, g.shape[-1])
            if name == "ffn_w_in":
                lw[l]["win_t"][j] = whole
            elif name == "ffn_w_out":
                lw[l]["wout"][j] = whole
            elif name == "mix_w_in":
                seg = lambda i: whole[cut[i]:cut[i + 1]]
                lw[l]["wmi_t"] = jnp.concatenate([seg(5), seg(6), seg(0), seg(1), seg(2), seg(3),
                                                  jnp.pad(seg(4), ((0, LANES - heads), (0, 0)))], axis=0)
            elif name == "mix_w_out":
                lw[l]["mo"] = whole
            else:
                lw[l]["glu_t" if name == "glu_w" else "ao_t"] = whole

    gathered = all_gather(
        [silu_c, norm_pre.reshape(-1, norm_pre.shape[-1]), norm_post.reshape(-1, norm_post.shape[-1])]
        + shards(0, ffn1), name="gather_first")
    sc_all = gathered[0].reshape(N_DEV, d)
    gpre = jnp.moveaxis(gathered[1].reshape(N_DEV, depth, 3, -1), 0, 2).reshape(depth, 3, d)
    gpost = jnp.moveaxis(gathered[2].reshape(N_DEV, depth, 3, -1), 0, 2).reshape(depth, 3, d)
    install(0, ffn1, gathered[3:])

    sc_pad = jnp.pad(sc_all, ((0, LANES - N_DEV), (0, 0)))
    mod_part = jnp.stack([mm(sc_pad, mod_w[l], name=f"mod_fwd{l}")[:N_DEV] for l in range(depth)], axis=1)
    mod_part = mod_part + lax.dynamic_slice_in_dim(mod_b, me * mod_cols, mod_cols, axis=1)[None]
    mod_all = all_gather([mod_part], name="gather_mod")[0]
    mod_own = lax.dynamic_index_in_dim(mod_all, me, axis=1, keepdims=False)
    mod_own = mod_own.transpose(1, 0, 2).reshape(depth, 3, 3, d)
    res_w = (FFN_RES, 1.0, FFN_RES)

    def vec_a(l, i):
        return (gpre[l, i] * (1.0 + mod_own[l, i, 1])).reshape(1, d)

    def vec_sh(l, i):
        return mod_own[l, i, 0].reshape(1, d)

    def vec_b(l, i):
        return (res_w[i] * mod_own[l, i, 2] * gpost[l, i]).reshape(1, d)

    ssm = []
    for l in range(depth):
        (e, ph, bb_re, bb_im), disc_vjp = jax.vjp(_discretise,ssm_a_re[l], ssm_a_im[l], ssm_log_dt[l],
                                                  ssm_b_re[l], ssm_b_im[l])
        b_mat = jnp.concatenate([_blockdiag(bb_re.transpose(0, 2, 1), nc),
                                 _blockdiag(bb_im.transpose(0, 2, 1), nc)], axis=2)
        c_mat = jnp.concatenate([_blockdiag(ssm_c_re[l].transpose(0, 2, 1), nc),
                                 _blockdiag(-ssm_c_im[l].transpose(0, 2, 1), nc)], axis=1)
        ssm.append(dict(e=e, ph=ph, vjp=disc_vjp, b=b_mat.astype(BF16), c=c_mat.astype(BF16),
                        bt=b_mat.transpose(0, 2, 1).astype(BF16), ct=c_mat.transpose(0, 2, 1).astype(BF16),
                        tab_f=_scan_table(e, ph, nc, False), tab_r=_scan_table(e, ph, nc, True),
                        dvec=ssm_d[l].reshape(1, sw)))

    fb_pad = jnp.pad(forget_b, ((0, 0), (0, LANES - heads)))

    def heads_first(a):
        return a.reshape(n_tok, heads, dh).transpose(1, 0, 2)

    def heads_last(a):
        return a.transpose(1, 0, 2).reshape(n_tok, heads * dh)

    def mm_hosting(a, b, comm, **kw):
        if comm is None:
            return mm(a, b, **kw), []
        out, *arrived = mm(a, b, comm=comm, **kw)
        return out, arrived

    def ffn_fwd(xin, l, i, j, tag, comm=None):
        h = rowwise(_f_pre, [xin], [vec_a(l, i), vec_sh(l, i)], [((d,), BF16)], name=f"pre_{tag}")[0]
        ag, au, m, *arrived = ffn_in_swiglu(h, lw[l]["win_t"][j], name=f"ffn_in_{tag}", comm=comm)
        y = mm(m, lw[l]["wout"][j], name=f"ffn_out_{tag}")
        xout = rowwise(_f_post_add, [xin, y], [vec_b(l, i)], [((d,), F32)], name=f"post_{tag}")[0]
        return xout, dict(x=xin, h=h, ag=ag, au=au, m=m, y=y), arrived

    def mixer_fwd(xin, l, tag):
        s5 = ssm[l]
        h = rowwise(_f_pre, [xin], [vec_a(l, 1), vec_sh(l, 1)], [((d,), BF16)], name=f"pre_{tag}")[0]
        proj = mm(h, lw[l]["wmi_t"], trans_b=True, name=f"mix_in_{tag}", out_dtype=BF16)
        projf = mm(h, lw[l]["wmi_t"][off_f:], trans_b=True, name=f"mix_in_f_{tag}")
        bu = mm_blockdiag(proj, s5["b"], a_cb0=off_u // LANES, name=f"ssm_bu_{tag}", out_dtype=BF16)
        st = ssm_scan(bu, s5["tab_f"], reverse=False, name=f"ssm_scan_{tag}")[0]
        y0 = mm_blockdiag(st, s5["c"], name=f"ssm_y_{tag}")
        ge, ys = rowwise(_f_gelu_in, [y0, (proj, sw, off_u // sw)], [s5["dvec"]],
                         [((sw,), BF16), ((sw,), F32)], name=f"gelu_{tag}")
        z = mm(ge, lw[l]["glu_t"], trans_b=True, name=f"glu_{tag}", out_dtype=BF16)
        cum = cum_fwd(projf, fb_pad[l:l + 1], col_block=0, name=f"cum_{tag}")
        crow = cum[:, :heads].T[:, None, :]
        q, k, v = [heads_first(proj[:, off_q + i * aw:off_q + (i + 1) * aw] * sc_).astype(BF16)
                   for i, sc_ in enumerate((scale, 1.0, 1.0))]
        nxt = gather_comm([send_shard(p, ll) for ll, p in riders(l)])
        o, lse, *arrived = attn_fwd(q, k, v, crow, name=f"attn_{tag}", comm=nxt)
        for (ll, p), got in zip(riders(l), arrived):
            install(ll, [p], [got])
        attn = heads_last(o).astype(BF16)
        yb = mm(attn, lw[l]["ao_t"], trans_b=True, name=f"attn_out_{tag}", out_dtype=BF16)
        mg = rowwise(_f_merge, [(z, d, 0), (z, d, 1), yb, (proj, d, 0), (proj, d, 1)], [],
                     [((d,), BF16)], name=f"merge_{tag}")[0]
        y = mm(mg, lw[l]["mo"], name=f"mix_out_{tag}")
        xout = rowwise(_f_post_add, [xin, y], [vec_b(l, 1)], [((d,), F32)], name=f"post_{tag}")[0]
        saved = dict(x=xin, h=h, proj=proj, projf=projf, st=st, ys=ys, ge=ge, z=z, q=q, k=k, v=v, o=o, lse=lse,
                     cum=cum, attn=attn, yb=yb, mg=mg, y=y)
        return xout, saved

    saved = []
    xc = x2
    for l in range(depth):
        xc, s0, arrived = ffn_fwd(xc, l, 0, 0, f"l{l}a", gather_comm(shards(0, mix)) if l == 0 else None)
        if l == 0:
            install(0, mix, arrived)
        xc, s1 = mixer_fwd(xc, l, f"l{l}m")
        xc, s2, _ = ffn_fwd(xc, l, 2, 1, f"l{l}b")
        saved.append((s0, s1, s2))

    def f_loss(xf, t):
        e_ = xf - t
        return e_ * (1.0 / d), _colsum(e_ * e_)

    dx, sq = rowwise(f_loss, [xc, tgt], [], [((d,), F32)], [d], name="loss_head")
    loss_part = 0.5 * jnp.sum(sq) / d

    grads = {k: [None] * depth for k in big}
    small_g = [dict() for _ in range(depth)]
    dmod = [[None] * 3 for _ in range(depth)]
    dgpre = [[None] * 3 for _ in range(depth)]
    dgpost = [[None] * 3 for _ in range(depth)]

    def norm_grads(l, i, d_a, d_sh, d_bv):
        d_a, d_sh, d_bv = d_a.reshape(d), d_sh.reshape(d), d_bv.reshape(d)
        dmod[l][i] = jnp.stack([d_sh, d_a * gpre[l, i], res_w[i] * gpost[l, i] * d_bv])
        dgpre[l][i] = d_a * (1.0 + mod_own[l, i, 1])
        dgpost[l][i] = res_w[i] * mod_own[l, i, 2] * d_bv

    def ffn_bwd(dxo, sv, l, i, j, tag, comm_dw=None, comm_dx_of=None):
        dy, d_bv = rowwise(_f_post_bwd, [dxo, sv["y"]], [vec_b(l, i)], [((d,), BF16)], [d],
                           name=f"post_bwd_{tag}")
        dm = mm(dy, lw[l]["wout"][j], trans_b=True, name=f"ffn_out_dx_{tag}", out_dtype=BF16)
        g_out = mm(sv["m"], dy, trans_a=True, name=f"ffn_out_dw_{tag}", out_dtype=BF16, tm=1408, tn=1024)
        da = rowwise(_f_swiglu_bwd, [sv["ag"], sv["au"], dm], [], [((ff, ff), BF16)],
                     name=f"swiglu_bwd_{tag}")[0]
        g_in, arrived_dw = mm_hosting(sv["h"], da, comm_dw, trans_a=True, name=f"ffn_in_dw_{tag}",
                                      out_dtype=BF16)
        comm_dx = comm_dx_of(g_in, g_out) if comm_dx_of is not None else None
        dh_, arrived_dx = mm_hosting(da, lw[l]["win_t"][j], comm_dx, name=f"ffn_in_dx_{tag}")
        dxn, d_a, d_sh = rowwise(_f_pre_bwd, [dxo, dh_, sv["x"]], [vec_a(l, i)], [((d,), F32)], [d, d],
                                 name=f"pre_bwd_{tag}")
        norm_grads(l, i, d_a, d_sh, d_bv)
        return dxn, g_in, g_out, arrived_dw, arrived_dx

    def mixer_bwd(dxo, sv, l, tag, comm):
        s5 = ssm[l]
        proj = sv["proj"]
        dy, d_bv = rowwise(_f_post_bwd, [dxo, sv["y"]], [vec_b(l, 1)], [((d,), BF16)], [d],
                           name=f"post_bwd_{tag}")
        dmg = mm(dy, lw[l]["mo"], trans_b=True, name=f"mix_out_dx_{tag}", out_dtype=BF16)
        g_mo = mm(sv["mg"], dy, trans_a=True, name=f"mix_out_dw_{tag}", out_dtype=BF16)
        dz, dyb, dproj = rowwise(
            _f_merge_bwd, [dmg, (sv["z"], d, 0), (sv["z"], d, 1), sv["yb"], (proj, d, 0), (proj, d, 1)], [],
            [((d, d), BF16), ((d,), BF16), ((d, d), BF16, (iwp, 0))], name=f"merge_bwd_{tag}")
        dge = mm(dz, lw[l]["glu_t"], name=f"glu_dx_{tag}")
        g_glu = mm(sv["ge"], dz, trans_a=True, name=f"glu_dw_{tag}", out_dtype=BF16)
        dys, d_dvec = rowwise(_f_gelu_bwd, [dge, sv["ys"], (proj, sw, off_u // sw)], [], [((sw,), BF16)],
                              [sw], name=f"gelu_bwd_{tag}")
        gadj = mm_blockdiag(dys, s5["ct"], name=f"ssm_dy_{tag}", out_dtype=BF16)
        adj, dlam8 = ssm_scan(gadj, s5["tab_r"], reverse=True, s_prev=sv["st"], name=f"ssm_scan_bwd_{tag}")
        du0 = mm_blockdiag(adj, s5["bt"], name=f"ssm_du_{tag}")
        d_bmat = mm_blockdiag_tn(proj, adj, g_n=nc, ka=LANES, kb=2 * hw, a_cb0=off_u // LANES,
                                 name=f"ssm_db_{tag}")
        d_cmat = mm_blockdiag_tn(sv["st"], dys, g_n=nc, ka=2 * hw, kb=LANES, name=f"ssm_dc_{tag}")
        dproj = rowwise(_f_du_fin, [du0, dys], [s5["dvec"]], [((sw,), BF16, (iwp, off_u // sw), dproj)],
                        name=f"ssm_du_fin_{tag}")[0]
        dlam = jnp.sum(dlam8, axis=1)
        dlam_re, dlam_im = dlam[:, :hw].reshape(g_n, p_n), dlam[:, hw:].reshape(g_n, p_n)
        dbb_re = _blockdiag_take(d_bmat[:, :, :hw], g_n).transpose(0, 2, 1)
        dbb_im = _blockdiag_take(d_bmat[:, :, hw:], g_n).transpose(0, 2, 1)
        mag = jnp.exp(s5["e"])
        lr, li = mag * jnp.cos(s5["ph"]), mag * jnp.sin(s5["ph"])
        d_e = dlam_re * lr + dlam_im * li
        d_ph = -dlam_re * li + dlam_im * lr
        da_re, da_im, dlog_dt, db_re, db_im = s5["vjp"]((d_e, d_ph, dbb_re, dbb_im))
        small_g[l].update(
            ssm_a_re=da_re, ssm_a_im=da_im, ssm_log_dt=dlog_dt, ssm_b_re=db_re, ssm_b_im=db_im,
            ssm_c_re=_blockdiag_take(d_cmat[:, :hw, :], g_n).transpose(0, 2, 1),
            ssm_c_im=-_blockdiag_take(d_cmat[:, hw:, :], g_n).transpose(0, 2, 1),
            ssm_d=d_dvec.reshape(sw))
        dattn = mm(dyb, lw[l]["ao_t"], name=f"attn_out_dx_{tag}", out_dtype=BF16)
        g_ao = mm(sv["attn"], dyb, trans_a=True, name=f"attn_out_dw_{tag}", out_dtype=BF16)
        dq, dk, dv, dcum, *arrived = attn_bwd(sv["q"], sv["k"], sv["v"], heads_first(dattn), sv["o"], sv["lse"],
                                              sv["cum"], scale=scale, tag=tag, comm=comm)
        df, dfb = cum_bwd(dcum, sv["projf"], fb_pad[l:l + 1], col_block=0, name=f"cum_bwd_{tag}")
        small_g[l]["forget_b"] = dfb[0, :heads]
        for piece, off in ((heads_last(dq), off_q), (heads_last(dk), off_q + aw), (heads_last(dv), off_q + 2 * aw),
                           (df, off_f)):
            dproj = lax.dynamic_update_slice(dproj, piece.astype(BF16), (0, off))
        g_mi = mm(sv["h"], dproj, trans_a=True, name=f"mix_in_dw_{tag}", out_dtype=BF16)
        dh_ = mm(dproj, lw[l]["wmi_t"], name=f"mix_in_dx_{tag}")
        dxn, d_a, d_sh = rowwise(_f_pre_bwd, [dxo, dh_, sv["x"]], [vec_a(l, 1)], [((d,), F32)], [d, d],
                                 name=f"pre_bwd_{tag}")
        norm_grads(l, 1, d_a, d_sh, d_bv)
        g_mi = jnp.concatenate([g_mi[:, off_u:off_f + heads], g_mi[:, :off_u]], axis=1)
        return dxn, g_mi, g_glu, g_ao, g_mo, arrived

    def split_last(a):
        return jnp.moveaxis(a.reshape(a.shape[:-1] + (N_DEV, a.shape[-1] // N_DEV)), -2, 0)

    def split_rows(a):
        return jnp.moveaxis(a.reshape(a.shape[:-2] + (N_DEV, a.shape[-2] // N_DEV, a.shape[-1])), -3, 0)

    def owner_blocks(l, pieces):
        out = []
        for name, j in pieces:
            if name == "ffn_w_in":
                out.append(split_last(g_ffn_in[l][j]))
            elif name == "ffn_w_out":
                out.append(split_rows(g_ffn_out[l][j]))
            elif name == "mix_w_out":
                out.append(split_rows(grads[name][l]))
            else:
                out.append(split_last(grads[name][l]))
        return out

    g_ffn_in = [[None, None] for _ in range(depth)]
    g_ffn_out = [[None, None] for _ in range(depth)]
    parts = {}

    def record(l, pieces, arrived):
        for p, a in zip(pieces, arrived):
            parts[(p, l)] = a

    def last_ffn_blocks(g_in, g_out):
        return exchange_comm([split_last(g_in), split_rows(g_out)])

    for l in reversed(range(depth)):
        s0, s1, s2 = saved[l]
        dx, g_ffn_in[l][1], g_ffn_out[l][1], _, _ = ffn_bwd(dx, s2, l, 2, 1, f"l{l}b")
        pending = exchange_comm([owner_blocks(ll, [p])[0] for ll, p in riders(l)])
        (dx, grads["mix_w_in"][l], grads["glu_w"][l], grads["attn_w_out"][l], grads["mix_w_out"][l],
         arrived) = mixer_bwd(dx, s1, l, f"l{l}m", pending)
        for (ll, p), got in zip(riders(l), arrived):
            record(ll, [p], [got])
        if l == 0:
            dx, g_ffn_in[l][0], g_ffn_out[l][0], arrived_mix, arrived_ffn1 = ffn_bwd(
                dx, s0, l, 0, 0, f"l{l}a", exchange_comm(owner_blocks(0, mix)), last_ffn_blocks)
            record(0, mix, arrived_mix)
            record(0, ffn1, arrived_ffn1)
        else:
            dx, g_ffn_in[l][0], g_ffn_out[l][0], _, _ = ffn_bwd(dx, s0, l, 0, 0, f"l{l}a")
    grad_x = dx.reshape(x.shape)

    small_names = ["forget_b", "ssm_a_re", "ssm_a_im", "ssm_log_dt", "ssm_b_re", "ssm_b_im", "ssm_c_re",
                   "ssm_c_im", "ssm_d"]
    pieces = [loss_part.reshape(1), jnp.stack([jnp.stack(dmod[l]) for l in range(depth)]).reshape(-1),
              jnp.stack([jnp.stack(dgpre[l]) for l in range(depth)]).reshape(-1),
              jnp.stack([jnp.stack(dgpost[l]) for l in range(depth)]).reshape(-1)]
    pieces += [jnp.stack([small_g[l][k] for l in range(depth)]).reshape(-1) for k in small_names]
    sizes = [p.size for p in pieces]
    chunk = SUBLANES * 1024
    total = -(-sum(sizes) // chunk) * chunk
    pack = jnp.pad(jnp.concatenate(pieces), (0, total - sum(sizes))).reshape(total // 1024, 1024)
    pack_all = all_gather([pack], name="gather_small_grads")[0]
    pack_sum = rowwise(_f_sum_parts, [(pack_all, p) for p in range(N_DEV)], [], [((1024,), F32)],
                       name="sum_small_grads")[0].reshape(-1)
    offs = [0]
    for s_ in sizes:
        offs.append(offs[-1] + s_)
    take = lambda i: pack_sum[offs[i]:offs[i + 1]]
    loss = take(0).reshape(())
    g_small = {"mod_b": take(1).reshape(mod_b.shape)}
    g_pre_full, g_post_full = take(2).reshape(depth, 3, d), take(3).reshape(depth, 3, d)
    shard = norm_pre.shape[-1]
    g_small["norm_pre"] = lax.dynamic_slice_in_dim(g_pre_full, me * shard, shard, axis=2)
    g_small["norm_post"] = lax.dynamic_slice_in_dim(g_post_full, me * shard, shard, axis=2)
    for i, k in enumerate(small_names):
        g_small[k] = take(4 + i).reshape(w_in[k].shape)

    dmod_all = pack_all.reshape(N_DEV, -1)[:, offs[1]:offs[2]].reshape(N_DEV, depth, 9 * d)
    dmod_mine = lax.dynamic_slice_in_dim(dmod_all, me * mod_cols, mod_cols, axis=2)
    sct_pad = jnp.pad(sc_all.T, ((0, 0), (0, LANES - N_DEV)))
    g_mod_w = jnp.stack([
        mm(sct_pad, jnp.pad(dmod_mine[:, l], ((0, LANES - N_DEV), (0, 0))), name=f"mod_dw{l}")
        for l in range(depth)])

    out_g, out_d, out_m, out_v = {}, {}, {}, {}
    flat = lambda a: a.reshape(-1, a.shape[-1])
    res = adamw(g_mod_w.reshape(1, -1, mod_cols), flat(mod_w), flat(m_mod_w), flat(v_mod_w), name="adamw_mod_w")
    out_g["mod_w"], out_d["mod_w"], out_m["mod_w"], out_v["mod_w"] = [r.reshape(mod_w.shape) for r in res]
    for k in big:
        js = (0, 1) if k.startswith("ffn") else (None,)
        total_rows = w_in[k].size // w_in[k].shape[-1]
        res = None
        for l in range(depth):
            for j in js:
                p = (k, j)
                w_p = piece_of(w_in, p, l)
                row0 = (l * len(js) + (j or 0)) * (w_p.size // w_p.shape[-1])
                res = adamw(parts[(p, l)], flat(w_p), flat(piece_of(m_in, p, l)), flat(piece_of(v_in, p, l)),
                            name=f"adamw_{k}_l{l}" + ("" if j is None else f"_{j}"), into=(res, total_rows, row0))
        for dct, r in zip((out_g, out_d, out_m, out_v), res):
            dct[k] = r.reshape(w_in[k].shape)
    small_all = ["mod_b", "norm_pre", "norm_post"] + small_names

    def pack_small(dct):
        flat = jnp.concatenate([dct[k].reshape(-1) for k in small_all])
        tot = -(-flat.size // chunk) * chunk
        return jnp.pad(flat, (0, tot - flat.size)).reshape(tot // 1024, 1024)

    res = adamw(pack_small(g_small)[None], pack_small(w_in), pack_small(m_in), pack_small(v_in),
                name="adamw_small")
    pos = 0
    for k in small_all:
        size = w_in[k].size
        for dct, r in zip((out_g, out_d, out_m, out_v), res):
            dct[k] = r.reshape(-1)[pos:pos + size].reshape(w_in[k].shape)
        pos += size

    return (loss, grad_x, *[out_g[k] for k in names], *[out_d[k] for k in names],
            *[out_m[k] for k in names], *[out_v[k] for k in names])
```

```python
import functools
import math

import jax
import jax.numpy as jnp
from jax import lax
from jax.experimental import pallas as pl
from jax.experimental.pallas import tpu as pltpu

F32 = jnp.float32
BF16 = jnp.bfloat16
MESH = pl.DeviceIdType.MESH
N_DEV = 8
LANES = 128
SUBLANES = 8
VMEM_LIMIT = 48 * 1024 * 1024
RMS_EPS = 1e-6
FFN_RES = 0.5
ADAM_LR = 0.001
ADAM_B1 = 0.9
ADAM_B2 = 0.999
ADAM_EPS = 1e-08
ADAM_WD = 0.01
ADAM_STEP = 10
GELU_C = math.sqrt(2.0 / math.pi)
GELU_A = 0.044715


def _pick(dim, target, mult=LANES):
    t = (min(dim, target) // mult) * mult
    while t >= mult:
        if dim % t == 0:
            return t
        t -= mult
    return dim


def _params(sem):
    return pltpu.CompilerParams(dimension_semantics=sem, vmem_limit_bytes=VMEM_LIMIT)


def _sigmoid(x):
    return 1.0 / (1.0 + jnp.exp(-x))


def mm(a, b, *, name, trans_a=False, trans_b=False, out_dtype=F32, tm=1024, tn=1408, tk=2816, comm=None):
    if trans_a:
        kdim, m = a.shape
    else:
        m, kdim = a.shape
    if trans_b:
        n, kb = b.shape
    else:
        kb, n = b.shape
    assert kdim == kb, (a.shape, b.shape)
    tm, tn, tk = _pick(m, tm), _pick(n, tn), _pick(kdim, tk)
    gm, gn, nk = m // tm, n // tn, kdim // tk
    dims = (((0 if trans_a else 1,), (1 if trans_b else 0,)), ((), ()))
    host = _Hosted(comm, 2, 1, 1 if nk > 1 else 0)

    def body(*refs):
        (a_ref, b_ref, o_ref, *acc), crefs = host.split(refs)
        i, j, k = pl.program_id(0), pl.program_id(1), pl.program_id(2)
        host.phase("start", (i == 0) & (j == 0) & (k == 0), crefs)
        prod = lax.dot_general(a_ref[...].astype(BF16), b_ref[...].astype(BF16), dims,
                               preferred_element_type=F32)
        if nk == 1:
            o_ref[...] = prod.astype(out_dtype)
        else:
            acc_ref, = acc

            @pl.when(k == 0)
            def _():
                acc_ref[...] = prod

            @pl.when((k > 0) & (k < nk - 1))
            def _():
                acc_ref[...] += prod

            @pl.when(k == nk - 1)
            def _():
                o_ref[...] = (acc_ref[...] + prod).astype(out_dtype)

        last = (i == gm - 1) & (j == gn - 1) & (k == nk - 1)
        host.phase("mid", last, crefs)
        host.phase("finish", last, crefs)

    a_spec = (pl.BlockSpec((tk, tm), lambda i, j, k: (k, i)) if trans_a
              else pl.BlockSpec((tm, tk), lambda i, j, k: (i, k)))
    b_spec = (pl.BlockSpec((tn, tk), lambda i, j, k: (j, k)) if trans_b
              else pl.BlockSpec((tk, tn), lambda i, j, k: (k, j)))
    in_specs, out_specs, out_shape, scratch, extra = host.specs(
        [a_spec, b_spec], [pl.BlockSpec((tm, tn), lambda i, j, k: (i, j))],
        [jax.ShapeDtypeStruct((m, n), out_dtype)], [pltpu.VMEM((tm, tn), F32)] if nk > 1 else [])
    res = pl.pallas_call(
        body, name=name, grid=(gm, gn, nk),
        in_specs=in_specs, out_specs=out_specs, out_shape=out_shape, scratch_shapes=scratch,
        compiler_params=_params(("arbitrary", "arbitrary", "arbitrary")),
    )(a, b, *extra)
    return res if comm else res[0]


def ffn_in_swiglu(h, w_t, *, name, comm=None, tm=1024, tn=1408):
    m, kdim = h.shape
    f = w_t.shape[0] // 2
    tm, tn = _pick(m, tm), _pick(f, tn)
    gm, gn = m // tm, f // tn
    host = _Hosted(comm, 3, 3, 0)

    def body(*refs):
        (h_ref, wg_ref, wu_ref, g_ref, u_ref, m_ref), crefs = host.split(refs)
        i, j = pl.program_id(0), pl.program_id(1)
        host.phase("start", (i == 0) & (j == 0), crefs)
        hv = h_ref[...]
        g = lax.dot_general(hv, wg_ref[...], (((1,), (1,)), ((), ())), preferred_element_type=F32)
        u = lax.dot_general(hv, wu_ref[...], (((1,), (1,)), ((), ())), preferred_element_type=F32)
        g_ref[...] = g.astype(BF16)
        u_ref[...] = u.astype(BF16)
        m_ref[...] = ((g * _sigmoid(g)) * u).astype(BF16)
        last = (i == gm - 1) & (j == gn - 1)
        host.phase("mid", last, crefs)
        host.phase("finish", last, crefs)

    out_spec = pl.BlockSpec((tm, tn), lambda i, j: (i, j))
    in_specs, out_specs, out_shape, scratch, extra = host.specs(
        [pl.BlockSpec((tm, kdim), lambda i, j: (i, 0)), pl.BlockSpec((tn, kdim), lambda i, j: (j, 0)),
         pl.BlockSpec((tn, kdim), lambda i, j: (j + gn, 0))],
        [out_spec] * 3, [jax.ShapeDtypeStruct((m, f), BF16)] * 3, [])
    return pl.pallas_call(
        body, name=name, grid=(gm, gn),
        in_specs=in_specs, out_specs=out_specs, out_shape=out_shape, scratch_shapes=scratch,
        compiler_params=_params(("arbitrary", "arbitrary")),
    )(h, w_t, w_t, *extra)


def mm_blockdiag(a, b, *, name, a_cb0=0, out_dtype=F32, tm=512):
    m = a.shape[0]
    g_n, ka, nb = b.shape
    tm = _pick(m, tm)
    assert a_cb0 % g_n == 0

    def body(a_ref, b_ref, o_ref):
        for g in range(g_n):
            o_ref[:, g * nb:(g + 1) * nb] = jnp.dot(
                a_ref[:, g * ka:(g + 1) * ka].astype(BF16), b_ref[g].astype(BF16),
                preferred_element_type=F32).astype(out_dtype)

    return pl.pallas_call(
        body, name=name, grid=(m // tm,),
        in_specs=[pl.BlockSpec((tm, g_n * ka), lambda i: (i, a_cb0 // g_n)),
                  pl.BlockSpec((g_n, ka, nb), lambda i: (0, 0, 0))],
        out_specs=pl.BlockSpec((tm, g_n * nb), lambda i: (i, 0)),
        out_shape=jax.ShapeDtypeStruct((m, g_n * nb), out_dtype),
        compiler_params=_params(("parallel",)),
    )(a, b)


def mm_blockdiag_tn(a, b, *, name, g_n, ka, kb, a_cb0=0, b_cb0=0, tk=512):
    rows = a.shape[0]
    tk = _pick(rows, tk)
    nk = rows // tk
    assert a_cb0 % g_n == 0 and b_cb0 % g_n == 0

    def body(a_ref, b_ref, o_ref):
        @pl.when(pl.program_id(0) == 0)
        def _():
            o_ref[...] = jnp.zeros_like(o_ref)

        for g in range(g_n):
            o_ref[g] += lax.dot_general(a_ref[:, g * ka:(g + 1) * ka].astype(BF16),
                                        b_ref[:, g * kb:(g + 1) * kb].astype(BF16),
                                        (((0,), (0,)), ((), ())), preferred_element_type=F32)

    return pl.pallas_call(
        body, name=name, grid=(nk,),
        in_specs=[pl.BlockSpec((tk, g_n * ka), lambda k: (k, a_cb0 // g_n)),
                  pl.BlockSpec((tk, g_n * kb), lambda k: (k, b_cb0 // g_n))],
        out_specs=pl.BlockSpec((g_n, ka, kb), lambda k: (0, 0, 0)),
        out_shape=jax.ShapeDtypeStruct((g_n, ka, kb), F32),
        compiler_params=_params(("arbitrary",)),
    )(a, b)


def rowwise(fn, rows, vecs, outs, reds=(), *, name, tm=512):
    metas = []
    for r in rows:
        if isinstance(r, tuple) and len(r) == 3:
            metas.append(("col", r[0], r[1], r[2]))
        elif isinstance(r, tuple):
            metas.append(("lead", r[0], r[0].shape[2], r[1]))
        else:
            metas.append(("full", r, r.shape[1], 0))
    n_rows = metas[0][1].shape[1] if metas[0][0] == "lead" else metas[0][1].shape[0]
    rc = 16 if n_rows % 16 == 0 else (SUBLANES if n_rows % SUBLANES == 0 else n_rows)
    tm = _pick(n_rows, tm, rc)
    n_inner = tm // rc
    windows = [(o[2] if len(o) > 2 else None) for o in outs]
    bases = [(k, o[3]) for k, o in enumerate(outs) if len(o) > 3 and o[3] is not None]
    outs = [(o[0], o[1]) for o in outs]
    nr, nv, no, nbase = len(metas), len(vecs), len(outs), len(bases)

    def body(*refs):
        row_refs, vec_refs = refs[:nr], refs[nr:nr + nv]
        refs = refs[nr + nv + nbase:]
        out_refs, red_refs = refs[:no], refs[no:]
        if reds:
            @pl.when(pl.program_id(0) == 0)
            def _():
                for rr in red_refs:
                    rr[...] = jnp.zeros_like(rr)
        vec_vals = [v[...] for v in vec_refs]

        def step(s, carry):
            r0 = pl.multiple_of(s * rc, rc)
            vals = [ref[pl.ds(r0, rc), :] for ref in row_refs]
            res = fn(*vals, *vec_vals)
            if not isinstance(res, (tuple, list)):
                res = (res,)
            for o_ref, (widths, dt), val in zip(out_refs, outs, res[:no]):
                pieces = val if isinstance(val, (tuple, list)) else (val,)
                off = 0
                for w_, piece in zip(widths, pieces):
                    o_ref[pl.ds(r0, rc), off:off + w_] = piece.astype(dt)
                    off += w_
            for rr, val in zip(red_refs, res[no:]):
                rr[...] += val
            return carry

        lax.fori_loop(0, n_inner, step, 0, unroll=min(n_inner, 4))

    in_specs = []
    for kind, arr, w_, idx in metas:
        if kind == "col":
            in_specs.append(pl.BlockSpec((tm, w_), functools.partial(lambda i, cb: (i, cb), cb=idx)))
        elif kind == "lead":
            in_specs.append(pl.BlockSpec((None, tm, w_), functools.partial(lambda i, p: (p, i, 0), p=idx)))
        else:
            in_specs.append(pl.BlockSpec((tm, w_), lambda i: (i, 0)))
    for v in vecs:
        in_specs.append(pl.BlockSpec(v.shape, lambda i: (0, 0)))
    in_specs += [pl.BlockSpec(memory_space=pl.ANY)] * nbase
    out_specs, out_shape = [], []
    for (ws, dt), win in zip(outs, windows):
        total, cb, total_rows, row0 = (tuple(win) + (n_rows, 0))[:4] if win is not None else (sum(ws), 0, n_rows, 0)
        assert row0 % tm == 0
        out_specs.append(pl.BlockSpec((tm, sum(ws)),
                                      functools.partial(lambda i, cb, rb: (i + rb, cb), cb=cb, rb=row0 // tm)))
        out_shape.append(jax.ShapeDtypeStruct((total_rows, total), dt))
    out_specs += [pl.BlockSpec((1, w_), lambda i: (0, 0)) for w_ in reds]
    out_shape += [jax.ShapeDtypeStruct((1, w_), F32) for w_ in reds]
    res = pl.pallas_call(
        body, name=name, grid=(n_rows // tm,),
        in_specs=in_specs, out_specs=out_specs, out_shape=out_shape,
        input_output_aliases={nr + nv + b: k for b, (k, _) in enumerate(bases)},
        compiler_params=_params(("arbitrary",)),
    )(*[m[1] for m in metas], *vecs, *[b for _, b in bases])
    return res


def _rms(x):
    return lax.rsqrt(jnp.mean(x * x, axis=-1, keepdims=True) + RMS_EPS)


def _colsum(x):
    return jnp.sum(x, axis=0, keepdims=True)


def _f_silu(c):
    return c * _sigmoid(c)


def _f_pre(x, a, sh):
    return (x * _rms(x)) * a + sh


def _f_post_add(x, y, bv):
    return x + (y * _rms(y)) * bv


def _f_post_pre(x, y, bv, a, sh):
    x = _f_post_add(x, y, bv)
    return x, _f_pre(x, a, sh)


def _f_pre_post_bwd(dxo, dh, x, y, a, bv):
    dx, d_a, d_sh = _f_pre_bwd(dxo, dh, x, a)
    dy, d_bv = _f_post_bwd(dx, y, bv)
    return dx, dy, d_a, d_sh, d_bv


def _f_post_bwd(dxo, y, bv):
    ry = _rms(y)
    yn = y * ry
    dyn = dxo * bv
    dy = ry * (dyn - yn * jnp.mean(dyn * yn, axis=-1, keepdims=True))
    return dy, _colsum(dxo * yn)


def _f_pre_bwd(dxo, dh, x, a):
    r = _rms(x)
    xn = x * r
    dxn = dh * a
    dx = dxo + r * (dxn - xn * jnp.mean(dxn * xn, axis=-1, keepdims=True))
    return dx, _colsum(dh * xn), _colsum(dh)


def _f_swiglu_bwd(g, u, dm):
    g, u, dm = g.astype(F32), u.astype(F32), dm.astype(F32)
    sg = _sigmoid(g)
    dg = dm * u * (sg * (1.0 + g * (1.0 - sg)))
    du = dm * (g * sg)
    return ((dg, du),)


def _gelu_t(x):
    return jnp.tanh(GELU_C * (x + GELU_A * x * x * x))


def _f_gelu_in(y0, u, dvec):
    y = y0 + dvec * u
    return 0.5 * y * (1.0 + _gelu_t(y)), y


def _f_gelu_bwd(dge, y, u):
    t = _gelu_t(y)
    dy = dge * (0.5 * (1.0 + t) + 0.5 * y * (1.0 - t * t) * GELU_C * (1.0 + 3.0 * GELU_A * y * y))
    return dy, _colsum(dy * u)


def _f_du_fin(du0, dys, dvec):
    return du0 + dvec * dys.astype(F32)


def _f_merge(zv, zg, yb, ga, gb):
    zv, zg, yb, ga, gb = [a.astype(F32) for a in (zv, zg, yb, ga, gb)]
    return _sigmoid(ga) * (zv * _sigmoid(zg)) + _sigmoid(gb) * yb


def _f_merge_bwd(dmg, zv, zg, yb, ga, gb):
    zv, zg, yb, ga, gb = [a.astype(F32) for a in (zv, zg, yb, ga, gb)]
    sa, sb, sz = _sigmoid(ga), _sigmoid(gb), _sigmoid(zg)
    ya = zv * sz
    dya = dmg * sa
    dga = dmg * ya * sa * (1.0 - sa)
    dyb = dmg * sb
    dgb = dmg * yb * sb * (1.0 - sb)
    dzv = dya * sz
    dzg = dya * zv * sz * (1.0 - sz)
    return (dzv, dzg), dyb, (dga, dgb)


def _f_sum_parts(*parts):
    acc = parts[0].astype(F32)
    for p in parts[1:]:
        acc = acc + p.astype(F32)
    return acc


def _f_adamw(*args):
    parts, (w, m, v) = args[:-3], args[-3:]
    g = _f_sum_parts(*parts)
    m = ADAM_B1 * m + (1.0 - ADAM_B1) * g
    v = ADAM_B2 * v + (1.0 - ADAM_B2) * (g * g)
    m_hat = m / (1.0 - ADAM_B1 ** ADAM_STEP)
    v_hat = v / (1.0 - ADAM_B2 ** ADAM_STEP)
    delta = -ADAM_LR * (m_hat / (jnp.sqrt(v_hat) + ADAM_EPS) + ADAM_WD * w)
    return g, delta, m, v


def adamw(parts3, w, m, v, *, name, into=None):
    c = w.shape[1]
    rows = [(parts3, p) for p in range(parts3.shape[0])] + [w, m, v]
    if into is None:
        return rowwise(_f_adamw, rows, [], [((c,), F32)] * 4, name=name)
    bases, total_rows, row0 = into
    outs = [((c,), F32, (c, 0, total_rows, row0), bases[k] if bases else None) for k in range(4)]
    return rowwise(_f_adamw, rows, [], outs, name=name)


def _tri_dot(tri, x, dims=(((1,), (0,)), ((), ()))):
    x1 = x.astype(BF16)
    r1 = x - x1.astype(F32)
    x2 = r1.astype(BF16)
    x3 = (r1 - x2.astype(F32)).astype(BF16)
    dot = functools.partial(lax.dot_general, dimension_numbers=dims, preferred_element_type=F32)
    return dot(tri, x1) + dot(tri, x2) + dot(tri, x3)


def cum_fwd(proj, fb, *, col_block, name, t=256):
    n = proj.shape[0]
    t = _pick(n, t, SUBLANES)

    def body(f_ref, fb_ref, cum_ref, car_ref):
        @pl.when(pl.program_id(0) == 0)
        def _():
            car_ref[...] = jnp.zeros_like(car_ref)

        x = f_ref[...] + fb_ref[...]
        lf = jnp.minimum(x, 0.0) - jnp.log(1.0 + jnp.exp(-jnp.abs(x)))
        r = lax.broadcasted_iota(jnp.int32, (t, t), 0)
        c = lax.broadcasted_iota(jnp.int32, (t, t), 1)
        cs = _tri_dot((c <= r).astype(BF16), lf) + car_ref[0:1, :]
        cum_ref[...] = cs
        car_ref[0:1, :] = cs[t - 1:t, :]

    return pl.pallas_call(
        body, name=name, grid=(n // t,),
        in_specs=[pl.BlockSpec((t, LANES), lambda i: (i, col_block)),
                  pl.BlockSpec((1, LANES), lambda i: (0, 0))],
        out_specs=pl.BlockSpec((t, LANES), lambda i: (i, 0)),
        out_shape=jax.ShapeDtypeStruct((n, LANES), F32),
        scratch_shapes=[pltpu.VMEM((SUBLANES, LANES), F32)],
        compiler_params=_params(("arbitrary",)),
    )(proj, fb)


def cum_bwd(dcum, proj, fb, *, col_block, name, t=256):
    n = proj.shape[0]
    slabs = dcum.shape[0]
    t = _pick(n, t, SUBLANES)
    nb = n // t

    def body(dc_ref, f_ref, fb_ref, df_ref, dfb_ref, car_ref):
        @pl.when(pl.program_id(0) == 0)
        def _():
            car_ref[...] = jnp.zeros_like(car_ref)
            dfb_ref[...] = jnp.zeros_like(dfb_ref)

        r = lax.broadcasted_iota(jnp.int32, (t, t), 0)
        c = lax.broadcasted_iota(jnp.int32, (t, t), 1)
        dl = _tri_dot((c >= r).astype(BF16), jnp.sum(dc_ref[...], axis=0)) + car_ref[0:1, :]
        car_ref[0:1, :] = dl[0:1, :]
        x = f_ref[...] + fb_ref[...]
        df = dl * (1.0 / (1.0 + jnp.exp(x)))
        df_ref[...] = df
        dfb_ref[...] += _colsum(df)

    return pl.pallas_call(
        body, name=name, grid=(nb,),
        in_specs=[pl.BlockSpec((slabs, t, LANES), lambda i: (0, nb - 1 - i, 0)),
                  pl.BlockSpec((t, LANES), lambda i: (nb - 1 - i, col_block)),
                  pl.BlockSpec((1, LANES), lambda i: (0, 0))],
        out_specs=[pl.BlockSpec((t, LANES), lambda i: (nb - 1 - i, 0)),
                   pl.BlockSpec((1, LANES), lambda i: (0, 0))],
        out_shape=[jax.ShapeDtypeStruct((n, LANES), F32), jax.ShapeDtypeStruct((1, LANES), F32)],
        scratch_shapes=[pltpu.VMEM((SUBLANES, LANES), F32)],
        compiler_params=_params(("arbitrary",)),
    )(dcum, proj, fb)


_NT =(((1,), (1,)), ((), ()))
_TN = (((0,), (0,)), ((), ()))


def _lane_sums_as_row(x):
    return _tri_dot(jnp.ones((SUBLANES, x.shape[1]), BF16), x, _NT)[0:1, :]


def _causal_keep(t):
    return lax.broadcasted_iota(jnp.int32, (t, t), 1) <= lax.broadcasted_iota(jnp.int32, (t, t), 0)


def attn_fwd(q, k, v, crow, *, name, t=512, hb=8, comm=None):
    h_n, n, dh = q.shape
    hb = min(hb, h_n)
    t = _pick(n, t)
    nb = n // t
    ng = h_n // hb
    host = _Hosted(comm, 4, 2, 3)

    def body(*refs):
        (q_ref, k_ref, v_ref, cr_ref, o_ref, lse_ref, m_sc, l_sc, acc_sc), crefs = host.split(refs)
        g, i, j = pl.program_id(0), pl.program_id(1), pl.program_id(2)
        host.phase("start", (g == 0) & (i == 0) & (j == 0), crefs)
        host.phase("mid", (g == ng - 1) & (i == (3 * nb) // 4) & (j == 0), crefs)

        @pl.when(j == 0)
        def _():
            m_sc[...] = jnp.full_like(m_sc, -jnp.inf)
            l_sc[...] = jnp.zeros_like(l_sc)
            acc_sc[...] = jnp.zeros_like(acc_sc)

        def update(diagonal):
            keep = _causal_keep(t) if diagonal else None
            heads = range(hb)
            ss = [lax.dot_general(q_ref[h], k_ref[h], _NT, preferred_element_type=F32) for h in heads]
            pairs, alphas = [], []
            for h in heads:
                s = ss[h] - cr_ref[h]
                if diagonal:
                    s = jnp.where(keep, s, -jnp.inf)
                m_prev = m_sc[h]
                m_new = jnp.maximum(m_prev, jnp.max(s, axis=-1, keepdims=True))
                p = jnp.exp(s - m_new)
                alpha = jnp.exp(m_prev - m_new)
                l_sc[h] = alpha * l_sc[h] + jnp.sum(p, axis=-1, keepdims=True)
                m_sc[h] = m_new
                p_hi = p.astype(BF16)
                pairs.append((p_hi, (p - p_hi.astype(F32)).astype(BF16)))
                alphas.append(alpha)
            for h in heads:
                vv = v_ref[h]
                acc_sc[h] = (alphas[h] * acc_sc[h] + jnp.dot(pairs[h][0], vv, preferred_element_type=F32)
                             + jnp.dot(pairs[h][1], vv, preferred_element_type=F32))

        @pl.when(j < i)
        def _():
            update(False)

        @pl.when(j == i)
        def _():
            update(True)

        @pl.when(j == nb - 1)
        def _():
            o_ref[...] = acc_sc[...] / l_sc[...]
            lane0 = lax.broadcasted_iota(jnp.int32, (t, LANES), 1) == 0
            for h in range(hb):
                lse_col = m_sc[h] + jnp.log(l_sc[h])
                lse_ref[h] = _lane_sums_as_row(jnp.where(lane0, lse_col, 0.0))

        host.phase("finish", (g == ng - 1) & (i == nb - 1) & (j == nb - 1), crefs)

    qspec = pl.BlockSpec((hb, t, dh), lambda g, i, j: (g, i, 0))
    kspec = pl.BlockSpec((hb, t, dh), lambda g, i, j: (g, jnp.minimum(j, i), 0))
    in_specs, out_specs, out_shape, scratch, extra = host.specs(
        [qspec, kspec, kspec, pl.BlockSpec((hb, 1, t), lambda g, i, j: (g, 0, jnp.minimum(j, i)))],
        [qspec, pl.BlockSpec((hb, 1, t), lambda g, i, j: (g, 0, i))],
        [jax.ShapeDtypeStruct((h_n, n, dh), F32), jax.ShapeDtypeStruct((h_n, 1, n), F32)],
        [pltpu.VMEM((hb, t, 1), F32), pltpu.VMEM((hb, t, 1), F32), pltpu.VMEM((hb, t, dh), F32)])
    return pl.pallas_call(
        body, name=name, grid=(ng, nb, nb),
        in_specs=in_specs, out_specs=out_specs, out_shape=out_shape, scratch_shapes=scratch,
        compiler_params=_params(("arbitrary", "arbitrary", "arbitrary")),
    )(q, k, v, crow, *extra)


def attn_delta(do, o, *, name, t=512, hb=8):
    h_n, n, dh = do.shape
    hb = min(hb, h_n)
    t = _pick(n, t)

    def body(do_ref, o_ref, dl_ref):
        for h in range(hb):
            dl_ref[h] = _lane_sums_as_row(do_ref[h].astype(F32) * o_ref[h])

    spec = pl.BlockSpec((hb, t, dh), lambda g, i: (g, i, 0))
    return pl.pallas_call(
        body, name=name, grid=(h_n // hb, n // t),
        in_specs=[spec, spec], out_specs=pl.BlockSpec((hb, 1, t), lambda g, i: (g, 0, i)),
        out_shape=jax.ShapeDtypeStruct((h_n, 1, n), F32),
        compiler_params=_params(("parallel", "parallel")),
    )(do, o)


def attn_bwd(q, k, v, do, o, lse, cum, *, scale, tag, t=512, hb=2, comm=None):
    h_n, n, dh = q.shape
    hb = min(hb, h_n)
    t = _pick(n, t)
    nb = n // t
    dob = do.astype(BF16)
    delta = attn_delta(dob, o, name=f"attn_delta_{tag}", t=t)

    ng = h_n // hb
    host = _Hosted(comm, 7, 4, 3)

    def body(*refs):
        (q_ref, k_ref, v_ref, do_ref, lse_ref, dl_ref, cum_ref,
         dq_ref, dk_ref, dv_ref, dcum_ref, dk_acc, dv_acc, dcc_acc), crefs = host.split(refs)
        g, j, i = pl.program_id(0), pl.program_id(1), pl.program_id(2)
        host.phase("start", (g == 0) & (j == 0) & (i == 0), crefs)
        lane = lax.broadcasted_iota(jnp.int32, (t, LANES), 1)

        @pl.when((j == 0) & (i == 0))
        def _():
            dq_ref[...] = jnp.zeros_like(dq_ref)

        @pl.when(i == 0)
        def _():
            dk_acc[...] = jnp.zeros_like(dk_acc)
            dv_acc[...] = jnp.zeros_like(dv_acc)
            dcc_acc[...] = jnp.zeros_like(dcc_acc)

        def update(diagonal):
            heads = range(hb)
            r0 = pl.multiple_of(i * t, t)
            if diagonal:
                keep = lax.broadcasted_iota(jnp.int32, (t, t), 0) <= lax.broadcasted_iota(jnp.int32, (t, t), 1)
            qv, kv = [q_ref[h] for h in heads], [k_ref[h] for h in heads]
            vv, dov = [v_ref[h] for h in heads], [do_ref[h] for h in heads]
            st = [lax.dot_general(kv[h], qv[h], _NT, preferred_element_type=F32) for h in heads]
            dpt = [lax.dot_general(vv[h], dov[h], _NT, preferred_element_type=F32) for h in heads]
            pt = []
            cum_tile = cum_ref[...]
            for h in heads:
                cc = jnp.sum(jnp.where(lane == g * hb + h, cum_tile, 0.0), axis=1, keepdims=True)
                s = st[h] - cc
                if diagonal:
                    s = jnp.where(keep, s, -jnp.inf)
                pt.append(jnp.exp(s - lse_ref[h]))
            for h in heads:
                dv_acc[h] += jnp.dot(pt[h].astype(BF16), dov[h], preferred_element_type=F32)
            dsb = []
            for h in heads:
                ds = pt[h] * (dpt[h] - dl_ref[h])
                dcc_acc[h] -= jnp.sum(ds, axis=1, keepdims=True)
                dsb.append(ds.astype(BF16))
            for h in heads:
                dk_acc[h] += jnp.dot(dsb[h], qv[h], preferred_element_type=F32)
            for h in heads:
                dq_ref[h, pl.ds(r0, t), :] += lax.dot_general(dsb[h], kv[h], _TN,
                                                              preferred_element_type=F32) * scale

        @pl.when(i > j)
        def _():
            update(False)

        @pl.when(i == j)
        def _():
            update(True)

        @pl.when(i == nb - 1)
        def _():
            dk_ref[...] = dk_acc[...]
            dv_ref[...] = dv_acc[...]
            tile = jnp.zeros((t, LANES), F32)
            for h in range(hb):
                tile = tile + jnp.where(lane == g * hb + h, dcc_acc[h], 0.0)
            dcum_ref[...] = tile

        host.phase("finish", (g == ng - 1) & (j == nb - 1) & (i == nb - 1), crefs)

    qspec = pl.BlockSpec((hb, t, dh), lambda g, j, i: (g, jnp.maximum(i, j), 0))
    qrow = pl.BlockSpec((hb, 1, t), lambda g, j, i: (g, 0, jnp.maximum(i, j)))
    kspec = pl.BlockSpec((hb, t, dh), lambda g, j, i: (g, j, 0))
    in_specs, out_specs, out_shape, scratch, extra = host.specs(
        [qspec, kspec, kspec, qspec, qrow, qrow, pl.BlockSpec((t, LANES), lambda g, j, i: (j, 0))],
        [pl.BlockSpec((hb, n, dh), lambda g, j, i: (g, 0, 0)), kspec, kspec,
         pl.BlockSpec((None, t, LANES), lambda g, j, i: (g, j, 0))],
        [jax.ShapeDtypeStruct((h_n, n, dh), F32)] * 3 + [jax.ShapeDtypeStruct((ng, n, LANES), F32)],
        [pltpu.VMEM((hb, t, dh), F32), pltpu.VMEM((hb, t, dh), F32), pltpu.VMEM((hb, t, 1), F32)])
    dq, dk, dv, dcum, *arrived = pl.pallas_call(
        body, name=f"attn_bwd_{tag}", grid=(ng, nb, nb),
        in_specs=in_specs, out_specs=out_specs, out_shape=out_shape, scratch_shapes=scratch,
        compiler_params=_params(("arbitrary", "arbitrary", "arbitrary")),
    )(q, k, v, dob, lse, delta, cum, *extra)
    return [dq, dk, dv, dcum] + arrived


SCAN_STEPS = (1, 2, 4)


def ssm_scan(x, tab, *, reverse, name, s_prev=None, tt=512, out_dtype=BF16):
    n, width = x.shape
    nc, _, hw = tab.shape
    cw = 2 * hw
    assert width == nc * cw
    tt = _pick(n, tt, 2 * SUBLANES)
    nt = n // tt
    ng = tt // (2 * SUBLANES)
    with_grad = s_prev is not None

    def body(*refs):
        if with_grad:
            x_ref, s_ref, tab_ref, o_ref, g_ref, car_ref = refs
        else:
            x_ref, tab_ref, o_ref, car_ref = refs

        @pl.when(pl.program_id(1) == 0)
        def _():
            car_ref[...] = jnp.zeros_like(car_ref)
            if with_grad:
                g_ref[...] = jnp.zeros_like(g_ref)

        q_re, q_im = tab_ref[0:8, :], tab_ref[8:16, :]
        p_re = [tab_ref[16 + 16 * i:24 + 16 * i, :] for i in range(3)]
        p_im = [tab_ref[24 + 16 * i:32 + 16 * i, :] for i in range(3)]
        row = lax.broadcasted_iota(jnp.int32, (SUBLANES, hw), 0)

        def group(xr, xi, sr_, si_, carry):
            c_re, c_im = carry
            for i, d in enumerate(SCAN_STEPS):
                shift = SUBLANES - d if reverse else d
                sr, si = pltpu.roll(xr, shift, 0), pltpu.roll(xi, shift, 0)
                xr, xi = (xr + p_re[i] * sr - p_im[i] * si,
                          xi + p_re[i] * si + p_im[i] * sr)
            xr, xi = (xr + q_re * c_re - q_im * c_im,
                      xi + q_re * c_im + q_im * c_re)
            if with_grad:
                nr = jnp.where(row < SUBLANES - 1, pltpu.roll(xr, SUBLANES - 1, 0), c_re)
                ni = jnp.where(row < SUBLANES - 1, pltpu.roll(xi, SUBLANES - 1, 0), c_im)
                g_ref[:, 0:hw] += nr * sr_ + ni * si_
                g_ref[:, hw:cw] += ni * sr_ - nr * si_
            if reverse:
                return xr, xi, (xr[0:1, :], xi[0:1, :])
            return xr, xi, (xr[SUBLANES - 1:SUBLANES, :], xi[SUBLANES - 1:SUBLANES, :])

        def pair(gi, carry):
            g = (ng - 1 - gi) if reverse else gi
            r0 = pl.multiple_of(g * 2 * SUBLANES, 2 * SUBLANES)
            rows = pl.ds(r0, 2 * SUBLANES)
            xr, xi = x_ref[rows, 0:hw].astype(F32), x_ref[rows, hw:cw].astype(F32)
            if with_grad:
                sr, si = s_ref[rows, 0:hw].astype(F32), s_ref[rows, hw:cw].astype(F32)
            halves = [slice(0, SUBLANES), slice(SUBLANES, 2 * SUBLANES)]
            done = [None, None]
            for k in ((1, 0) if reverse else (0, 1)):
                h = halves[k]
                o_re, o_im, carry = group(xr[h], xi[h], sr[h] if with_grad else None,
                                          si[h] if with_grad else None, carry)
                done[k] = (o_re, o_im)
            o_ref[rows, 0:hw] = jnp.concatenate([done[0][0], done[1][0]], axis=0).astype(o_ref.dtype)
            o_ref[rows, hw:cw] = jnp.concatenate([done[0][1], done[1][1]], axis=0).astype(o_ref.dtype)
            return carry

        c_re, c_im = lax.fori_loop(0, ng, pair, (car_ref[0:1, 0:hw], car_ref[0:1, hw:cw]),
                                   unroll=min(ng, 2))
        car_ref[0:1, 0:hw] = c_re
        car_ref[0:1, hw:cw] = c_im

    if reverse:
        xspec = pl.BlockSpec((tt, cw), lambda c, t: (nt - 1 - t, c))
    else:
        xspec = pl.BlockSpec((tt, cw), lambda c, t: (t, c))
    tspec = pl.BlockSpec((None, 64, hw), lambda c, t: (c, 0, 0))
    in_specs = [xspec, xspec, tspec] if with_grad else [xspec, tspec]
    out_specs = [xspec]
    out_shape = [jax.ShapeDtypeStruct((n, width), out_dtype)]
    if with_grad:
        out_specs.append(pl.BlockSpec((None, SUBLANES, cw), lambda c, t: (c, 0, 0)))
        out_shape.append(jax.ShapeDtypeStruct((nc, SUBLANES, cw), F32))
    operands = (x, s_prev, tab) if with_grad else (x, tab)
    return pl.pallas_call(
        body, name=name, grid=(nc, nt),
        in_specs=in_specs, out_specs=out_specs, out_shape=out_shape,
        scratch_shapes=[pltpu.VMEM((SUBLANES, cw), F32)],
        compiler_params=_params(("parallel", "arbitrary")),
    )(*operands)


def _slot(pos):
    return 4 * pos[0] + 2 * pos[1] + pos[2]


def _comm_scratch(n):
    return [pltpu.SemaphoreType.DMA((7 * n,)), pltpu.SemaphoreType.DMA((7 * n,)), pltpu.SemaphoreType.DMA((n,))]


def _gather_copies(ins, outs, sems):
    send_sems, recv_sems, local_sems = sems
    n = len(ins)
    x, y, c = lax.axis_index("x"), lax.axis_index("y"), lax.axis_index("c")
    me, sibling = (x, y, c), (x, y, 1 - c)
    chips = [(1 - x, y), (x, 1 - y), (1 - x, 1 - y)]

    def copy(t, k, block, to, src=None):
        dst = outs[t].at[_slot(block)]
        return pltpu.make_async_remote_copy(
            src_ref=dst if src is None else src, dst_ref=dst,
            send_sem=send_sems.at[7 * t + k], recv_sem=recv_sems.at[7 * t + k],
            device_id=to, device_id_type=MESH)

    jc = list(enumerate(chips))
    return dict(
        mine=[pltpu.make_async_copy(ins[t], outs[t].at[_slot(me)], local_sems.at[t]) for t in range(n)],
        first=[cp for t in range(n) for cp in
               [copy(t, 0, me, sibling, src=ins[t])] + [copy(t, 1 + j, me, (*chip, c), src=ins[t]) for j, chip in jc]],
        arrive=[copy(t, 1 + j, (*chip, c), me) for t in range(n) for j, chip in jc],
        passed=[copy(t, 4 + j, (*chip, c), sibling) for t in range(n) for j, chip in jc],
        from_sibling=[cp for t in range(n) for cp in
                      [copy(t, 0, sibling, me)] + [copy(t, 4 + j, (*chip, 1 - c), me) for j, chip in jc]])


def _gather_start(ins, outs, sems):
    cps = _gather_copies(ins, outs, sems)
    for cp in cps["mine"] + cps["first"]:
        cp.start()


def _gather_forward(ins, outs, sems):
    cps = _gather_copies(ins, outs, sems)
    for arrived, onward in zip(cps["arrive"], cps["passed"]):
        arrived.wait_recv()
        onward.start()


def _gather_finish(ins, outs, sems):
    cps = _gather_copies(ins, outs, sems)
    for cp in cps["from_sibling"]:
        cp.wait_recv()
    for cp in cps["first"] + cps["passed"]:
        cp.wait_send()
    for cp in cps["mine"]:
        cp.wait()


def gather_comm(arrs):
    return dict(ins=list(arrs), out_shape=[jax.ShapeDtypeStruct((N_DEV,) + a.shape, a.dtype) for a in arrs],
                scratch=_comm_scratch(len(arrs)), start=_gather_start, mid=_gather_forward, finish=_gather_finish)


def _exchange_copies(ins, outs, sems):
    send_sems, recv_sems, local_sems = sems
    n = len(ins)
    me = (lax.axis_index("x"), lax.axis_index("y"), lax.axis_index("c"))
    peers = []
    for k in range(1, N_DEV):
        flip = ((k >> 2) & 1, (k >> 1) & 1, k & 1)
        peers.append(tuple(1 - p if f else p for p, f in zip(me, flip)))

    def copy(t, k, peer, dst_slot):
        return pltpu.make_async_remote_copy(
            src_ref=ins[t].at[_slot(peer)], dst_ref=outs[t].at[dst_slot],
            send_sem=send_sems.at[7 * t + k], recv_sem=recv_sems.at[7 * t + k],
            device_id=peer, device_id_type=MESH)

    return dict(
        mine=[pltpu.make_async_copy(ins[t].at[_slot(me)], outs[t].at[_slot(me)], local_sems.at[t])
              for t in range(n)],
        send=[copy(t, k, peer, _slot(me)) for t in range(n) for k, peer in enumerate(peers)],
        both=[copy(t, k, peer, _slot(peer)) for t in range(n) for k, peer in enumerate(peers)])


def _exchange_start(ins, outs, sems):
    cps = _exchange_copies(ins, outs, sems)
    for cp in cps["mine"] + cps["send"]:
        cp.start()


def _exchange_finish(ins, outs, sems):
    cps = _exchange_copies(ins, outs, sems)
    for cp in cps["both"]:
        cp.wait()
    for cp in cps["mine"]:
        cp.wait()


def exchange_comm(arrs):
    return dict(ins=list(arrs), out_shape=[jax.ShapeDtypeStruct(a.shape, a.dtype) for a in arrs],
                scratch=_comm_scratch(len(arrs)), start=_exchange_start, mid=None, finish=_exchange_finish)


def run_comm(comm, *, name):
    n_in, n_out = len(comm["ins"]), len(comm["out_shape"])

    def body(*refs):
        ins, outs, sems = refs[:n_in], refs[n_in:n_in + n_out], refs[n_in + n_out:]
        comm["start"](ins, outs, sems)
        if comm["mid"] is not None:
            comm["mid"](ins, outs, sems)
        comm["finish"](ins, outs, sems)

    any_spec = pl.BlockSpec(memory_space=pl.ANY)
    return pl.pallas_call(
        body, name=name, in_specs=[any_spec] * n_in, out_specs=[any_spec] * n_out,
        out_shape=comm["out_shape"], scratch_shapes=comm["scratch"],
    )(*comm["ins"])


class _Hosted:
    def __init__(self, comm, n_in, n_out, n_scratch):
        self.comm = comm
        self.n_ci = len(comm["ins"]) if comm else 0
        self.n_co = len(comm["out_shape"]) if comm else 0
        self.n_in, self.n_out, self.n_scratch = n_in, n_out, n_scratch

    def split(self, refs):
        a = self.n_in
        b = a + self.n_ci
        c = b + self.n_out
        e = c + self.n_co
        f = e + self.n_scratch
        return refs[:a] + refs[b:c] + refs[e:f], (refs[a:b], refs[c:e], refs[f:])

    def phase(self, which, when, crefs):
        fn = self.comm[which] if self.comm else None
        if fn is not None:
            pl.when(when)(lambda: fn(*crefs))

    def specs(self, in_specs, out_specs, out_shape, scratch):
        any_spec = pl.BlockSpec(memory_space=pl.ANY)
        if not self.comm:
            return in_specs, out_specs, out_shape, scratch, ()
        return (in_specs + [any_spec] * self.n_ci, out_specs + [any_spec] * self.n_co,
                out_shape + self.comm["out_shape"], scratch + self.comm["scratch"], tuple(self.comm["ins"]))


def all_gather(arrs, *, name):
    return run_comm(gather_comm(arrs), name=name)


def _discretise(a_re, a_im, log_dt, b_re, b_im):
    ar = jnp.minimum(a_re, -1e-4)
    dt = jnp.exp(log_dt)[:, None]
    e, ph = ar * dt, a_im * dt
    mag = jnp.exp(e)
    lr, li = mag * jnp.cos(ph), mag * jnp.sin(ph)
    den = ar * ar + a_im * a_im
    nr, ni = lr - 1.0, li
    cr = (nr * ar + ni * a_im) / den
    ci = (ni * ar - nr * a_im) / den
    bb_re = cr[..., None] * b_re - ci[..., None] * b_im
    bb_im = cr[..., None] * b_im + ci[..., None] * b_re
    return e, ph, bb_re, bb_im


def _lam_pow(e, ph, k, conj):
    mag = jnp.exp(k * e)
    return mag * jnp.cos(k * ph), (-1.0 if conj else 1.0) * mag * jnp.sin(k * ph)


def _scan_table(e, ph, nc, reverse):
    hw = e.size // nc
    e, ph = e.reshape(nc, 1, hw), ph.reshape(nc, 1, hw)
    j = jnp.arange(SUBLANES, dtype=F32).reshape(1, SUBLANES, 1)
    kq = (SUBLANES - j) if reverse else (j + 1.0)
    parts = list(_lam_pow(e, ph, kq, reverse))
    for step in SCAN_STEPS:
        inside = (j < SUBLANES - step) if reverse else (j >= step)
        p_re, p_im = _lam_pow(e, ph, jnp.full_like(j, float(step)), reverse)
        parts += [jnp.where(inside, p_re, 0.0), jnp.where(inside, p_im, 0.0)]
    return jnp.concatenate(parts, axis=1)


def _blockdiag(m, nc):
    g, a, b = m.shape
    gc = g // nc
    m = m.reshape(nc, gc, a, b)
    eye = jnp.eye(gc, dtype=m.dtype)
    return jnp.einsum("cgab,gh->cgahb", m, eye).reshape(nc, gc * a, gc * b)


def _blockdiag_take(m, g):
    nc = m.shape[0]
    gc = g // nc
    a, b = m.shape[1] // gc, m.shape[2] // gc
    m = m.reshape(nc, gc, a, gc, b)
    eye = jnp.eye(gc, dtype=m.dtype)
    return jnp.einsum("cgahb,gh->cgab", m, eye).reshape(g, a, b)


def kernel(x, c, mod_w, mod_b, norm_pre, norm_post, ffn_w_in, ffn_w_out, mix_w_in, forget_b, ssm_a_re, ssm_a_im, ssm_log_dt, ssm_b_re, ssm_b_im, ssm_c_re, ssm_c_im, ssm_d, glu_w, attn_w_out, mix_w_out, loss_target, m_mod_w, m_mod_b, m_norm_pre, m_norm_post, m_ffn_w_in, m_ffn_w_out, m_mix_w_in, m_forget_b, m_ssm_a_re, m_ssm_a_im, m_ssm_log_dt, m_ssm_b_re, m_ssm_b_im, m_ssm_c_re, m_ssm_c_im, m_ssm_d, m_glu_w, m_attn_w_out, m_mix_w_out, v_mod_w, v_mod_b, v_norm_pre, v_norm_post, v_ffn_w_in, v_ffn_w_out, v_mix_w_in, v_forget_b, v_ssm_a_re, v_ssm_a_im, v_ssm_log_dt, v_ssm_b_re, v_ssm_b_im, v_ssm_c_re, v_ssm_c_im, v_ssm_d, v_glu_w, v_attn_w_out, v_mix_w_out):
    names = ["mod_w", "mod_b", "norm_pre", "norm_post", "ffn_w_in", "ffn_w_out", "mix_w_in", "forget_b",
             "ssm_a_re", "ssm_a_im", "ssm_log_dt", "ssm_b_re", "ssm_b_im", "ssm_c_re", "ssm_c_im", "ssm_d",
             "glu_w", "attn_w_out", "mix_w_out"]
    w_in = dict(zip(names, [mod_w, mod_b, norm_pre, norm_post, ffn_w_in, ffn_w_out, mix_w_in, forget_b,
                            ssm_a_re, ssm_a_im, ssm_log_dt, ssm_b_re, ssm_b_im, ssm_c_re, ssm_c_im, ssm_d,
                            glu_w, attn_w_out, mix_w_out]))
    m_in = dict(zip(names, [m_mod_w, m_mod_b, m_norm_pre, m_norm_post, m_ffn_w_in, m_ffn_w_out, m_mix_w_in,
                            m_forget_b, m_ssm_a_re, m_ssm_a_im, m_ssm_log_dt, m_ssm_b_re, m_ssm_b_im,
                            m_ssm_c_re, m_ssm_c_im, m_ssm_d, m_glu_w, m_attn_w_out, m_mix_w_out]))
    v_in = dict(zip(names, [v_mod_w, v_mod_b, v_norm_pre, v_norm_post, v_ffn_w_in, v_ffn_w_out, v_mix_w_in,
                            v_forget_b, v_ssm_a_re, v_ssm_a_im, v_ssm_log_dt, v_ssm_b_re, v_ssm_b_im,
                            v_ssm_c_re, v_ssm_c_im, v_ssm_d, v_glu_w, v_attn_w_out, v_mix_w_out]))

    depth = mod_w.shape[0]
    n_tok, d = x.shape[1], x.shape[2]
    ff = ffn_w_out.shape[2] * N_DEV
    heads = forget_b.shape[1]
    sw = ssm_d.shape[1]
    g_n, p_n, n_n = ssm_b_re.shape[1:]
    aw = attn_w_out.shape[1]
    dh = aw // heads
    iw = mix_w_in.shape[2] * N_DEV
    nc = sw // LANES
    hw = g_n * p_n // nc
    mod_cols = mod_w.shape[2]
    scale = dh ** -0.5
    assert iw == sw + 3 * aw + heads + 2 * d and heads <= LANES
    assert math.log2(scale).is_integer(), "q is pre-scaled in bf16: exact only for a power of two"
    off_u, off_q, off_f = 2 * d, 2 * d + sw, 2 * d + sw + 3 * aw
    iwp = off_f + LANES
    assert off_u % sw == 0 and off_q % aw == 0 and off_f % LANES == 0

    me = 4 * lax.axis_index("x") + 2 * lax.axis_index("y") + lax.axis_index("c")
    x2 = x.reshape(n_tok, d)
    tgt = loss_target.reshape(n_tok, d)

    silu_c = rowwise(_f_silu, [c], [], [((d,), F32)], name="silu_c")[0]
    big = ["ffn_w_in", "ffn_w_out", "mix_w_in", "glu_w", "attn_w_out", "mix_w_out"]
    ffn1 = [("ffn_w_in", 0), ("ffn_w_out", 0)]
    mix = [("mix_w_in", None), ("glu_w", None), ("attn_w_out", None), ("mix_w_out", None)]
    ffn2 = [("ffn_w_in", 1), ("ffn_w_out", 1)]

    def riders(l):
        nxt = [(l + 1, p) for p in ffn1 + mix] if l + 1 < depth else []
        return [(l, p) for p in ffn2] + nxt

    def piece_of(dct, piece, l):
        name, j = piece
        return dct[name][l] if j is None else dct[name][l][j]

    row_sharded = ("ffn_w_out", "mix_w_out")

    def send_shard(p, l):
        a = piece_of(w_in, p, l).astype(BF16)
        return a if p[0] in row_sharded else a.T

    def shards(l, pieces):
        return [send_shard(p, l) for p in pieces]

    cut = [0, sw, sw + aw, sw + 2 * aw, sw + 3 * aw, sw + 3 * aw + heads, sw + 3 * aw + heads + d, iw]
    lw = [dict(win_t=[None, None], wout=[None, None]) for _ in range(depth)]

    def install(l, pieces, gathered):
        for (name, j), g in zip(pieces, gathered):
            whole = g.reshape(-1, g.shape[-1])
            if name == "ffn_w_in":
                lw[l]["win_t"][j] = whole
            elif name == "ffn_w_out":
                lw[l]["wout"][j] = whole
            elif name == "mix_w_in":
                seg = lambda i: whole[cut[i]:cut[i + 1]]
                lw[l]["wmi_t"] = jnp.concatenate([seg(5), seg(6), seg(0), seg(1), seg(2), seg(3),
                                                  jnp.pad(seg(4), ((0, LANES - heads), (0, 0)))], axis=0)
            elif name == "mix_w_out":
                lw[l]["mo"] = whole
            else:
                lw[l]["glu_t" if name == "glu_w" else "ao_t"] = whole

    gathered = all_gather(
        [silu_c, norm_pre.reshape(-1, norm_pre.shape[-1]), norm_post.reshape(-1, norm_post.shape[-1])]
        + shards(0, ffn1), name="gather_first")
    sc_all = gathered[0].reshape(N_DEV, d)
    gpre = jnp.moveaxis(gathered[1].reshape(N_DEV, depth, 3, -1), 0, 2).reshape(depth, 3, d)
    gpost = jnp.moveaxis(gathered[2].reshape(N_DEV, depth, 3, -1), 0, 2).reshape(depth, 3, d)
    install(0, ffn1, gathered[3:])

    sc_pad = jnp.pad(sc_all, ((0, LANES - N_DEV), (0, 0)))
    mod_part = jnp.stack([mm(sc_pad, mod_w[l], name=f"mod_fwd{l}")[:N_DEV] for l in range(depth)], axis=1)
    mod_part = mod_part + lax.dynamic_slice_in_dim(mod_b, me * mod_cols, mod_cols, axis=1)[None]
    mod_all = all_gather([mod_part], name="gather_mod")[0]
    mod_own = lax.dynamic_index_in_dim(mod_all, me, axis=1, keepdims=False)
    mod_own = mod_own.transpose(1, 0, 2).reshape(depth, 3, 3, d)
    res_w = (FFN_RES, 1.0, FFN_RES)

    def vec_a(l, i):
        return (gpre[l, i] * (1.0 + mod_own[l, i, 1])).reshape(1, d)

    def vec_sh(l, i):
        return mod_own[l, i, 0].reshape(1, d)

    def vec_b(l, i):
        return (res_w[i] * mod_own[l, i, 2] * gpost[l, i]).reshape(1, d)

    ssm = []
    for l in range(depth):
        (e, ph, bb_re, bb_im), disc_vjp = jax.vjp(_discretise,ssm_a_re[l], ssm_a_im[l], ssm_log_dt[l],
                                                  ssm_b_re[l], ssm_b_im[l])
        b_mat = jnp.concatenate([_blockdiag(bb_re.transpose(0, 2, 1), nc),
                                 _blockdiag(bb_im.transpose(0, 2, 1), nc)], axis=2)
        c_mat = jnp.concatenate([_blockdiag(ssm_c_re[l].transpose(0, 2, 1), nc),
                                 _blockdiag(-ssm_c_im[l].transpose(0, 2, 1), nc)], axis=1)
        ssm.append(dict(e=e, ph=ph, vjp=disc_vjp, b=b_mat.astype(BF16), c=c_mat.astype(BF16),
                        bt=b_mat.transpose(0, 2, 1).astype(BF16), ct=c_mat.transpose(0, 2, 1).astype(BF16),
                        tab_f=_scan_table(e, ph, nc, False), tab_r=_scan_table(e, ph, nc, True),
                        dvec=ssm_d[l].reshape(1, sw)))

    fb_pad = jnp.pad(forget_b, ((0, 0), (0, LANES - heads)))

    def heads_first(a):
        return a.reshape(n_tok, heads, dh).transpose(1, 0, 2)

    def heads_last(a):
        return a.transpose(1, 0, 2).reshape(n_tok, heads * dh)

    def mm_hosting(a, b, comm, **kw):
        if comm is None:
            return mm(a, b, **kw), []
        out, *arrived = mm(a, b, comm=comm, **kw)
        return out, arrived

    def ffn_fwd(xin, h, l, j, tag, comm=None):
        ag, au, m, *arrived = ffn_in_swiglu(h, lw[l]["win_t"][j], name=f"ffn_in_{tag}", comm=comm)
        y = mm(m, lw[l]["wout"][j], name=f"ffn_out_{tag}")
        return y, dict(x=xin, h=h, ag=ag, au=au, m=m, y=y), arrived

    def mixer_fwd(xin, h, l, tag):
        s5 = ssm[l]
        proj = mm(h, lw[l]["wmi_t"], trans_b=True, name=f"mix_in_{tag}", out_dtype=BF16)
        projf = mm(h, lw[l]["wmi_t"][off_f:], trans_b=True, name=f"mix_in_f_{tag}")
        bu = mm_blockdiag(proj, s5["b"], a_cb0=off_u // LANES, name=f"ssm_bu_{tag}", out_dtype=BF16)
        st = ssm_scan(bu, s5["tab_f"], reverse=False, name=f"ssm_scan_{tag}")[0]
        y0 = mm_blockdiag(st, s5["c"], name=f"ssm_y_{tag}")
        ge, ys = rowwise(_f_gelu_in, [y0, (proj, sw, off_u // sw)], [s5["dvec"]],
                         [((sw,), BF16), ((sw,), F32)], name=f"gelu_{tag}")
        z = mm(ge, lw[l]["glu_t"], trans_b=True, name=f"glu_{tag}", out_dtype=BF16)
        cum = cum_fwd(projf, fb_pad[l:l + 1], col_block=0, name=f"cum_{tag}")
        crow = cum[:, :heads].T[:, None, :]
        q, k, v = [heads_first(proj[:, off_q + i * aw:off_q + (i + 1) * aw] * sc_).astype(BF16)
                   for i, sc_ in enumerate((scale, 1.0, 1.0))]
        nxt = gather_comm([send_shard(p, ll) for ll, p in riders(l)])
        o, lse, *arrived = attn_fwd(q, k, v, crow, name=f"attn_{tag}", comm=nxt)
        for (ll, p), got in zip(riders(l), arrived):
            install(ll, [p], [got])
        attn = heads_last(o).astype(BF16)
        yb = mm(attn, lw[l]["ao_t"], trans_b=True, name=f"attn_out_{tag}", out_dtype=BF16)
        mg = rowwise(_f_merge, [(z, d, 0), (z, d, 1), yb, (proj, d, 0), (proj, d, 1)], [],
                     [((d,), BF16)], name=f"merge_{tag}")[0]
        y = mm(mg, lw[l]["mo"], name=f"mix_out_{tag}")
        saved = dict(x=xin, h=h, proj=proj, projf=projf, st=st, ys=ys, ge=ge, z=z, q=q, k=k, v=v, o=o, lse=lse,
                     cum=cum, attn=attn, yb=yb, mg=mg, y=y)
        return y, saved

    subs = [(l, i) for l in range(depth) for i in range(3)]
    tag_of = lambda l, i: f"l{l}" + "amb"[i]
    saved = [[None] * 3 for _ in range(depth)]
    xc = x2
    h = rowwise(_f_pre, [xc], [vec_a(0, 0), vec_sh(0, 0)], [((d,), BF16)], name="pre_first")[0]
    for s, (l, i) in enumerate(subs):
        tag = tag_of(l, i)
        if i == 1:
            y, saved[l][i] = mixer_fwd(xc, h, l, tag)
        else:
            first = (l, i) == (0, 0)
            y, saved[l][i], arrived = ffn_fwd(xc, h, l, i // 2, tag, gather_comm(shards(0, mix)) if first else None)
            if first:
                install(0, mix, arrived)
        if s + 1 < len(subs):
            l2, i2 = subs[s + 1]
            xc, h = rowwise(_f_post_pre, [xc, y], [vec_b(l, i), vec_a(l2, i2), vec_sh(l2, i2)],
                            [((d,), F32), ((d,), BF16)], name=f"post_pre_{tag}")
        else:
            xc = rowwise(_f_post_add, [xc, y], [vec_b(l, i)], [((d,), F32)], name=f"post_{tag}")[0]

    def f_loss(xf, t):
        e_ = xf - t
        return e_ * (1.0 / d), _colsum(e_ * e_)

    dx, sq = rowwise(f_loss, [xc, tgt], [], [((d,), F32)], [d], name="loss_head")
    loss_part = 0.5 * jnp.sum(sq) / d

    grads = {k: [None] * depth for k in big}
    small_g = [dict() for _ in range(depth)]
    dmod = [[None] * 3 for _ in range(depth)]
    dgpre = [[None] * 3 for _ in range(depth)]
    dgpost = [[None] * 3 for _ in range(depth)]

    def norm_grads(l, i, d_a, d_sh, d_bv):
        d_a, d_sh, d_bv = d_a.reshape(d), d_sh.reshape(d), d_bv.reshape(d)
        dmod[l][i] = jnp.stack([d_sh, d_a * gpre[l, i], res_w[i] * gpost[l, i] * d_bv])
        dgpre[l][i] = d_a * (1.0 + mod_own[l, i, 1])
        dgpost[l][i] = res_w[i] * mod_own[l, i, 2] * d_bv

    def ffn_bwd(dy, sv, l, j, tag, comm_dw=None, comm_dx_of=None):
        dm = mm(dy, lw[l]["wout"][j], trans_b=True, name=f"ffn_out_dx_{tag}", out_dtype=BF16)
        g_out = mm(sv["m"], dy, trans_a=True, name=f"ffn_out_dw_{tag}", out_dtype=BF16, tm=1408, tn=1024)
        da = rowwise(_f_swiglu_bwd, [sv["ag"], sv["au"], dm], [], [((ff, ff), BF16)],
                     name=f"swiglu_bwd_{tag}")[0]
        g_in, arrived_dw = mm_hosting(sv["h"], da, comm_dw, trans_a=True, name=f"ffn_in_dw_{tag}",
                                      out_dtype=BF16)
        comm_dx = comm_dx_of(g_in, g_out) if comm_dx_of is not None else None
        dh_, arrived_dx = mm_hosting(da, lw[l]["win_t"][j], comm_dx, name=f"ffn_in_dx_{tag}")
        return dh_, g_in, g_out, arrived_dw, arrived_dx

    def mixer_bwd(dy, sv, l, tag, comm):
        s5 = ssm[l]
        proj = sv["proj"]
        dmg = mm(dy, lw[l]["mo"], trans_b=True, name=f"mix_out_dx_{tag}", out_dtype=BF16)
        g_mo = mm(sv["mg"], dy, trans_a=True, name=f"mix_out_dw_{tag}", out_dtype=BF16)
        dz, dyb, dproj = rowwise(
            _f_merge_bwd, [dmg, (sv["z"], d, 0), (sv["z"], d, 1), sv["yb"], (proj, d, 0), (proj, d, 1)], [],
            [((d, d), BF16), ((d,), BF16), ((d, d), BF16, (iwp, 0))], name=f"merge_bwd_{tag}")
        dge = mm(dz, lw[l]["glu_t"], name=f"glu_dx_{tag}")
        g_glu = mm(sv["ge"], dz, trans_a=True, name=f"glu_dw_{tag}", out_dtype=BF16)
        dys, d_dvec = rowwise(_f_gelu_bwd, [dge, sv["ys"], (proj, sw, off_u // sw)], [], [((sw,), BF16)],
                              [sw], name=f"gelu_bwd_{tag}")
        gadj = mm_blockdiag(dys, s5["ct"], name=f"ssm_dy_{tag}", out_dtype=BF16)
        adj, dlam8 = ssm_scan(gadj, s5["tab_r"], reverse=True, s_prev=sv["st"], name=f"ssm_scan_bwd_{tag}")
        du0 = mm_blockdiag(adj, s5["bt"], name=f"ssm_du_{tag}")
        d_bmat = mm_blockdiag_tn(proj, adj, g_n=nc, ka=LANES, kb=2 * hw, a_cb0=off_u // LANES,
                                 name=f"ssm_db_{tag}")
        d_cmat = mm_blockdiag_tn(sv["st"], dys, g_n=nc, ka=2 * hw, kb=LANES, name=f"ssm_dc_{tag}")
        dproj = rowwise(_f_du_fin, [du0, dys], [s5["dvec"]], [((sw,), BF16, (iwp, off_u // sw), dproj)],
                        name=f"ssm_du_fin_{tag}")[0]
        dlam = jnp.sum(dlam8, axis=1)
        dlam_re, dlam_im = dlam[:, :hw].reshape(g_n, p_n), dlam[:, hw:].reshape(g_n, p_n)
        dbb_re = _blockdiag_take(d_bmat[:, :, :hw], g_n).transpose(0, 2, 1)
        dbb_im = _blockdiag_take(d_bmat[:, :, hw:], g_n).transpose(0, 2, 1)
        mag = jnp.exp(s5["e"])
        lr, li = mag * jnp.cos(s5["ph"]), mag * jnp.sin(s5["ph"])
        d_e = dlam_re * lr + dlam_im * li
        d_ph = -dlam_re * li + dlam_im * lr
        da_re, da_im, dlog_dt, db_re, db_im = s5["vjp"]((d_e, d_ph, dbb_re, dbb_im))
        small_g[l].update(
            ssm_a_re=da_re, ssm_a_im=da_im, ssm_log_dt=dlog_dt, ssm_b_re=db_re, ssm_b_im=db_im,
            ssm_c_re=_blockdiag_take(d_cmat[:, :hw, :], g_n).transpose(0, 2, 1),
            ssm_c_im=-_blockdiag_take(d_cmat[:, hw:, :], g_n).transpose(0, 2, 1),
            ssm_d=d_dvec.reshape(sw))
        dattn = mm(dyb, lw[l]["ao_t"], name=f"attn_out_dx_{tag}", out_dtype=BF16)
        g_ao = mm(sv["attn"], dyb, trans_a=True, name=f"attn_out_dw_{tag}", out_dtype=BF16)
        dq, dk, dv, dcum, *arrived = attn_bwd(sv["q"], sv["k"], sv["v"], heads_first(dattn), sv["o"], sv["lse"],
                                              sv["cum"], scale=scale, tag=tag, comm=comm)
        df, dfb = cum_bwd(dcum, sv["projf"], fb_pad[l:l + 1], col_block=0, name=f"cum_bwd_{tag}")
        small_g[l]["forget_b"] = dfb[0, :heads]
        for piece, off in ((heads_last(dq), off_q), (heads_last(dk), off_q + aw), (heads_last(dv), off_q + 2 * aw),
                           (df, off_f)):
            dproj = lax.dynamic_update_slice(dproj, piece.astype(BF16), (0, off))
        g_mi = mm(sv["h"], dproj, trans_a=True, name=f"mix_in_dw_{tag}", out_dtype=BF16)
        dh_ = mm(dproj, lw[l]["wmi_t"], name=f"mix_in_dx_{tag}")
        g_mi = jnp.concatenate([g_mi[:, off_u:off_f + heads], g_mi[:, :off_u]], axis=1)
        return dh_, g_mi, g_glu, g_ao, g_mo, arrived

    def split_last(a):
        return jnp.moveaxis(a.reshape(a.shape[:-1] + (N_DEV, a.shape[-1] // N_DEV)), -2, 0)

    def split_rows(a):
        return jnp.moveaxis(a.reshape(a.shape[:-2] + (N_DEV, a.shape[-2] // N_DEV, a.shape[-1])), -3, 0)

    def owner_blocks(l, pieces):
        out = []
        for name, j in pieces:
            if name == "ffn_w_in":
                out.append(split_last(g_ffn_in[l][j]))
            elif name == "ffn_w_out":
                out.append(split_rows(g_ffn_out[l][j]))
            elif name == "mix_w_out":
                out.append(split_rows(grads[name][l]))
            else:
                out.append(split_last(grads[name][l]))
        return out

    g_ffn_in = [[None, None] for _ in range(depth)]
    g_ffn_out = [[None, None] for _ in range(depth)]
    parts = {}

    def record(l, pieces, arrived):
        for p, a in zip(pieces, arrived):
            parts[(p, l)] = a

    def last_ffn_blocks(g_in, g_out):
        return exchange_comm([split_last(g_in), split_rows(g_out)])

    l_last, i_last = subs[-1]
    dy, d_bv = rowwise(_f_post_bwd, [dx, saved[l_last][i_last]["y"]], [vec_b(l_last, i_last)], [((d,), BF16)], [d],
                       name="post_bwd_last")
    for s in reversed(range(len(subs))):
        l, i = subs[s]
        sv, tag = saved[l][i], tag_of(l, i)
        if i == 2:
            dh_, g_ffn_in[l][1], g_ffn_out[l][1], _, _ = ffn_bwd(dy, sv, l, 1, tag)
        elif i == 1:
            pending = exchange_comm([owner_blocks(ll, [p])[0] for ll, p in riders(l)])
            (dh_, grads["mix_w_in"][l], grads["glu_w"][l], grads["attn_w_out"][l], grads["mix_w_out"][l],
             arrived) = mixer_bwd(dy, sv, l, tag, pending)
            for (ll, p), got in zip(riders(l), arrived):
                record(ll, [p], [got])
        elif l == 0:
            dh_, g_ffn_in[l][0], g_ffn_out[l][0], arrived_mix, arrived_ffn1 = ffn_bwd(
                dy, sv, l, 0, tag, exchange_comm(owner_blocks(0, mix)), last_ffn_blocks)
            record(0, mix, arrived_mix)
            record(0, ffn1, arrived_ffn1)
        else:
            dh_, g_ffn_in[l][0], g_ffn_out[l][0], _, _ = ffn_bwd(dy, sv, l, 0, tag)
        if s > 0:
            lp, ip = subs[s - 1]
            dx, dy, d_a, d_sh, d_bv_before = rowwise(
                _f_pre_post_bwd, [dx, dh_, sv["x"], saved[lp][ip]["y"]], [vec_a(l, i), vec_b(lp, ip)],
                [((d,), F32), ((d,), BF16)], [d, d, d], name=f"pre_post_bwd_{tag}")
        else:
            dx, d_a, d_sh = rowwise(_f_pre_bwd, [dx, dh_, sv["x"]], [vec_a(l, i)], [((d,), F32)], [d, d],
                                    name=f"pre_bwd_{tag}")
            d_bv_before = None
        norm_grads(l, i, d_a, d_sh, d_bv)
        d_bv = d_bv_before
    grad_x = dx.reshape(x.shape)

    small_names = ["forget_b", "ssm_a_re", "ssm_a_im", "ssm_log_dt", "ssm_b_re", "ssm_b_im", "ssm_c_re",
                   "ssm_c_im", "ssm_d"]
    pieces = [loss_part.reshape(1), jnp.stack([jnp.stack(dmod[l]) for l in range(depth)]).reshape(-1),
              jnp.stack([jnp.stack(dgpre[l]) for l in range(depth)]).reshape(-1),
              jnp.stack([jnp.stack(dgpost[l]) for l in range(depth)]).reshape(-1)]
    pieces += [jnp.stack([small_g[l][k] for l in range(depth)]).reshape(-1) for k in small_names]
    sizes = [p.size for p in pieces]
    chunk = SUBLANES * 1024
    total = -(-sum(sizes) // chunk) * chunk
    pack = jnp.pad(jnp.concatenate(pieces), (0, total - sum(sizes))).reshape(total // 1024, 1024)
    pack_all = all_gather([pack], name="gather_small_grads")[0]
    pack_sum = rowwise(_f_sum_parts, [(pack_all, p) for p in range(N_DEV)], [], [((1024,), F32)],
                       name="sum_small_grads")[0].reshape(-1)
    offs = [0]
    for s_ in sizes:
        offs.append(offs[-1] + s_)
    take = lambda i: pack_sum[offs[i]:offs[i + 1]]
    loss = take(0).reshape(())
    g_small = {"mod_b": take(1).reshape(mod_b.shape)}
    g_pre_full, g_post_full = take(2).reshape(depth, 3, d), take(3).reshape(depth, 3, d)
    shard = norm_pre.shape[-1]
    g_small["norm_pre"] = lax.dynamic_slice_in_dim(g_pre_full, me * shard, shard, axis=2)
    g_small["norm_post"] = lax.dynamic_slice_in_dim(g_post_full, me * shard, shard, axis=2)
    for i, k in enumerate(small_names):
        g_small[k] = take(4 + i).reshape(w_in[k].shape)

    dmod_all = pack_all.reshape(N_DEV, -1)[:, offs[1]:offs[2]].reshape(N_DEV, depth, 9 * d)
    dmod_mine = lax.dynamic_slice_in_dim(dmod_all, me * mod_cols, mod_cols, axis=2)
    sct_pad = jnp.pad(sc_all.T, ((0, 0), (0, LANES - N_DEV)))
    g_mod_w = jnp.stack([
        mm(sct_pad, jnp.pad(dmod_mine[:, l], ((0, LANES - N_DEV), (0, 0))), name=f"mod_dw{l}")
        for l in range(depth)])

    out_g, out_d, out_m, out_v = {}, {}, {}, {}
    flat = lambda a: a.reshape(-1, a.shape[-1])
    res = adamw(g_mod_w.reshape(1, -1, mod_cols), flat(mod_w), flat(m_mod_w), flat(v_mod_w), name="adamw_mod_w")
    out_g["mod_w"], out_d["mod_w"], out_m["mod_w"], out_v["mod_w"] = [r.reshape(mod_w.shape) for r in res]
    for k in big:
        js = (0, 1) if k.startswith("ffn") else (None,)
        total_rows = w_in[k].size // w_in[k].shape[-1]
        res = None
        for l in range(depth):
            for j in js:
                p = (k, j)
                w_p = piece_of(w_in, p, l)
                row0 = (l * len(js) + (j or 0)) * (w_p.size // w_p.shape[-1])
                res = adamw(parts[(p, l)], flat(w_p), flat(piece_of(m_in, p, l)), flat(piece_of(v_in, p, l)),
                            name=f"adamw_{k}_l{l}" + ("" if j is None else f"_{j}"), into=(res, total_rows, row0))
        for dct, r in zip((out_g, out_d, out_m, out_v), res):
            dct[k] = r.reshape(w_in[k].shape)
    small_all = ["mod_b", "norm_pre", "norm_post"] + small_names

    def pack_small(dct):
        flat = jnp.concatenate([dct[k].reshape(-1) for k in small_all])
        tot = -(-flat.size // chunk) * chunk
        return jnp.pad(flat, (0, tot - flat.size)).reshape(tot // 1024, 1024)

    res = adamw(pack_small(g_small)[None], pack_small(w_in), pack_small(m_in), pack_small(v_in),
                name="adamw_small")
    pos = 0
    for k in small_all:
        size = w_in[k].size
        for dct, r in zip((out_g, out_d, out_m, out_v), res):
            dct[k] = r.reshape(-1)[pos:pos + size].reshape(w_in[k].shape)
        pos += size

    return (loss, grad_x, *[out_g[k] for k in names], *[out_d[k] for k in names],
            *[out_m[k] for k in names], *[out_v[k] for k in names])
```

```python
import functools
import math

import jax
import jax.numpy as jnp
from jax import lax
from jax.experimental import pallas as pl
from jax.experimental.pallas import tpu as pltpu

F32 = jnp.float32
BF16 = jnp.bfloat16
MESH = pl.DeviceIdType.MESH
N_DEV = 8
LANES = 128
SUBLANES = 8
VMEM_LIMIT = 48 * 1024 * 1024
RMS_EPS = 1e-6
FFN_RES = 0.5
ADAM_LR = 0.001
ADAM_B1 = 0.9
ADAM_B2 = 0.999
ADAM_EPS = 1e-08
ADAM_WD = 0.01
ADAM_STEP = 10
GELU_C = math.sqrt(2.0 / math.pi)
GELU_A = 0.044715


def _pick(dim, target, mult=LANES):
    t = (min(dim, target) // mult) * mult
    while t >= mult:
        if dim % t == 0:
            return t
        t -= mult
    return dim


def _params(sem):
    return pltpu.CompilerParams(dimension_semantics=sem, vmem_limit_bytes=VMEM_LIMIT)


def _sigmoid(x):
    return 1.0 / (1.0 + jnp.exp(-x))


def mm(a, b, *, name, trans_a=False, trans_b=False, out_dtype=F32, tm=1024, tn=1408, tk=2816, comm=None):
    if trans_a:
        kdim, m = a.shape
    else:
        m, kdim = a.shape
    if trans_b:
        n, kb = b.shape
    else:
        kb, n = b.shape
    assert kdim == kb, (a.shape, b.shape)
    tm, tn, tk = _pick(m, tm), _pick(n, tn), _pick(kdim, tk)
    gm, gn, nk = m // tm, n // tn, kdim // tk
    dims = (((0 if trans_a else 1,), (1 if trans_b else 0,)), ((), ()))
    host = _Hosted(comm, 2, 1, 1 if nk > 1 else 0)

    def body(*refs):
        (a_ref, b_ref, o_ref, *acc), crefs = host.split(refs)
        i, j, k = pl.program_id(0), pl.program_id(1), pl.program_id(2)
        host.phase("start", (i == 0) & (j == 0) & (k == 0), crefs)
        prod = lax.dot_general(a_ref[...].astype(BF16), b_ref[...].astype(BF16), dims,
                               preferred_element_type=F32)
        if nk == 1:
            o_ref[...] = prod.astype(out_dtype)
        else:
            acc_ref, = acc

            @pl.when(k == 0)
            def _():
                acc_ref[...] = prod

            @pl.when((k > 0) & (k < nk - 1))
            def _():
                acc_ref[...] += prod

            @pl.when(k == nk - 1)
            def _():
                o_ref[...] = (acc_ref[...] + prod).astype(out_dtype)

        last = (i == gm - 1) & (j == gn - 1) & (k == nk - 1)
        host.phase("mid", last, crefs)
        host.phase("finish", last, crefs)

    a_spec = (pl.BlockSpec((tk, tm), lambda i, j, k: (k, i)) if trans_a
              else pl.BlockSpec((tm, tk), lambda i, j, k: (i, k)))
    b_spec = (pl.BlockSpec((tn, tk), lambda i, j, k: (j, k)) if trans_b
              else pl.BlockSpec((tk, tn), lambda i, j, k: (k, j)))
    in_specs, out_specs, out_shape, scratch, extra = host.specs(
        [a_spec, b_spec], [pl.BlockSpec((tm, tn), lambda i, j, k: (i, j))],
        [jax.ShapeDtypeStruct((m, n), out_dtype)], [pltpu.VMEM((tm, tn), F32)] if nk > 1 else [])
    res = pl.pallas_call(
        body, name=name, grid=(gm, gn, nk),
        in_specs=in_specs, out_specs=out_specs, out_shape=out_shape, scratch_shapes=scratch,
        compiler_params=_params(("arbitrary", "arbitrary", "arbitrary")),
    )(a, b, *extra)
    return res if comm else res[0]


def ffn_in_swiglu(h, w_t, *, name, comm=None, tm=1024, tn=1408):
    m, kdim = h.shape
    f = w_t.shape[0] // 2
    tm, tn = _pick(m, tm), _pick(f, tn)
    gm, gn = m // tm, f // tn
    host = _Hosted(comm, 3, 3, 0)

    def body(*refs):
        (h_ref, wg_ref, wu_ref, g_ref, u_ref, m_ref), crefs = host.split(refs)
        i, j = pl.program_id(0), pl.program_id(1)
        host.phase("start", (i == 0) & (j == 0), crefs)
        hv = h_ref[...]
        g = lax.dot_general(hv, wg_ref[...], (((1,), (1,)), ((), ())), preferred_element_type=F32)
        u = lax.dot_general(hv, wu_ref[...], (((1,), (1,)), ((), ())), preferred_element_type=F32)
        g_ref[...] = g.astype(BF16)
        u_ref[...] = u.astype(BF16)
        m_ref[...] = ((g * _sigmoid(g)) * u).astype(BF16)
        last = (i == gm - 1) & (j == gn - 1)
        host.phase("mid", last, crefs)
        host.phase("finish", last, crefs)

    out_spec = pl.BlockSpec((tm, tn), lambda i, j: (i, j))
    in_specs, out_specs, out_shape, scratch, extra = host.specs(
        [pl.BlockSpec((tm, kdim), lambda i, j: (i, 0)), pl.BlockSpec((tn, kdim), lambda i, j: (j, 0)),
         pl.BlockSpec((tn, kdim), lambda i, j: (j + gn, 0))],
        [out_spec] * 3, [jax.ShapeDtypeStruct((m, f), BF16)] * 3, [])
    return pl.pallas_call(
        body, name=name, grid=(gm, gn),
        in_specs=in_specs, out_specs=out_specs, out_shape=out_shape, scratch_shapes=scratch,
        compiler_params=_params(("arbitrary", "arbitrary")),
    )(h, w_t, w_t, *extra)


def mm_blockdiag(a, b, *, name, a_cb0=0, out_dtype=F32, tm=512):
    m = a.shape[0]
    g_n, ka, nb = b.shape
    tm = _pick(m, tm)
    assert a_cb0 % g_n == 0

    def body(a_ref, b_ref, o_ref):
        for g in range(g_n):
            o_ref[:, g * nb:(g + 1) * nb] = jnp.dot(
                a_ref[:, g * ka:(g + 1) * ka].astype(BF16), b_ref[g].astype(BF16),
                preferred_element_type=F32).astype(out_dtype)

    return pl.pallas_call(
        body, name=name, grid=(m // tm,),
        in_specs=[pl.BlockSpec((tm, g_n * ka), lambda i: (i, a_cb0 // g_n)),
                  pl.BlockSpec((g_n, ka, nb), lambda i: (0, 0, 0))],
        out_specs=pl.BlockSpec((tm, g_n * nb), lambda i: (i, 0)),
        out_shape=jax.ShapeDtypeStruct((m, g_n * nb), out_dtype),
        compiler_params=_params(("parallel",)),
    )(a, b)


def mm_blockdiag_tn(a, b, *, name, g_n, ka, kb, a_cb0=0, b_cb0=0, tk=512):
    rows = a.shape[0]
    tk = _pick(rows, tk)
    nk = rows // tk
    assert a_cb0 % g_n == 0 and b_cb0 % g_n == 0

    def body(a_ref, b_ref, o_ref):
        @pl.when(pl.program_id(0) == 0)
        def _():
            o_ref[...] = jnp.zeros_like(o_ref)

        for g in range(g_n):
            o_ref[g] += lax.dot_general(a_ref[:, g * ka:(g + 1) * ka].astype(BF16),
                                        b_ref[:, g * kb:(g + 1) * kb].astype(BF16),
                                        (((0,), (0,)), ((), ())), preferred_element_type=F32)

    return pl.pallas_call(
        body, name=name, grid=(nk,),
        in_specs=[pl.BlockSpec((tk, g_n * ka), lambda k: (k, a_cb0 // g_n)),
                  pl.BlockSpec((tk, g_n * kb), lambda k: (k, b_cb0 // g_n))],
        out_specs=pl.BlockSpec((g_n, ka, kb), lambda k: (0, 0, 0)),
        out_shape=jax.ShapeDtypeStruct((g_n, ka, kb), F32),
        compiler_params=_params(("arbitrary",)),
    )(a, b)


def rowwise(fn, rows, vecs, outs, reds=(), *, name, tm=512):
    metas = []
    for r in rows:
        if isinstance(r, tuple) and len(r) == 3:
            metas.append(("col", r[0], r[1], r[2]))
        elif isinstance(r, tuple):
            metas.append(("lead", r[0], r[0].shape[2], r[1]))
        else:
            metas.append(("full", r, r.shape[1], 0))
    n_rows = metas[0][1].shape[1] if metas[0][0] == "lead" else metas[0][1].shape[0]
    rc = 16 if n_rows % 16 == 0 else (SUBLANES if n_rows % SUBLANES == 0 else n_rows)
    tm = _pick(n_rows, tm, rc)
    n_inner = tm // rc
    windows = [(o[2] if len(o) > 2 else None) for o in outs]
    bases = [(k, o[3]) for k, o in enumerate(outs) if len(o) > 3 and o[3] is not None]
    outs = [(o[0], o[1]) for o in outs]
    nr, nv, no, nbase = len(metas), len(vecs), len(outs), len(bases)

    def body(*refs):
        row_refs, vec_refs = refs[:nr], refs[nr:nr + nv]
        refs = refs[nr + nv + nbase:]
        out_refs, red_refs = refs[:no], refs[no:]
        if reds:
            @pl.when(pl.program_id(0) == 0)
            def _():
                for rr in red_refs:
                    rr[...] = jnp.zeros_like(rr)
        vec_vals = [v[...] for v in vec_refs]

        def step(s, carry):
            r0 = pl.multiple_of(s * rc, rc)
            vals = [ref[pl.ds(r0, rc), :] for ref in row_refs]
            res = fn(*vals, *vec_vals)
            if not isinstance(res, (tuple, list)):
                res = (res,)
            for o_ref, (widths, dt), val in zip(out_refs, outs, res[:no]):
                pieces = val if isinstance(val, (tuple, list)) else (val,)
                off = 0
                for w_, piece in zip(widths, pieces):
                    o_ref[pl.ds(r0, rc), off:off + w_] = piece.astype(dt)
                    off += w_
            for rr, val in zip(red_refs, res[no:]):
                rr[...] += val
            return carry

        lax.fori_loop(0, n_inner, step, 0, unroll=min(n_inner, 4))

    in_specs = []
    for kind, arr, w_, idx in metas:
        if kind == "col":
            in_specs.append(pl.BlockSpec((tm, w_), functools.partial(lambda i, cb: (i, cb), cb=idx)))
        elif kind == "lead":
            in_specs.append(pl.BlockSpec((None, tm, w_), functools.partial(lambda i, p: (p, i, 0), p=idx)))
        else:
            in_specs.append(pl.BlockSpec((tm, w_), lambda i: (i, 0)))
    for v in vecs:
        in_specs.append(pl.BlockSpec(v.shape, lambda i: (0, 0)))
    in_specs += [pl.BlockSpec(memory_space=pl.ANY)] * nbase
    out_specs, out_shape = [], []
    for (ws, dt), win in zip(outs, windows):
        total, cb, total_rows, row0 = (tuple(win) + (n_rows, 0))[:4] if win is not None else (sum(ws), 0, n_rows, 0)
        assert row0 % tm == 0
        out_specs.append(pl.BlockSpec((tm, sum(ws)),
                                      functools.partial(lambda i, cb, rb: (i + rb, cb), cb=cb, rb=row0 // tm)))
        out_shape.append(jax.ShapeDtypeStruct((total_rows, total), dt))
    out_specs += [pl.BlockSpec((1, w_), lambda i: (0, 0)) for w_ in reds]
    out_shape += [jax.ShapeDtypeStruct((1, w_), F32) for w_ in reds]
    res = pl.pallas_call(
        body, name=name, grid=(n_rows // tm,),
        in_specs=in_specs, out_specs=out_specs, out_shape=out_shape,
        input_output_aliases={nr + nv + b: k for b, (k, _) in enumerate(bases)},
        compiler_params=_params(("arbitrary",)),
    )(*[m[1] for m in metas], *vecs, *[b for _, b in bases])
    return res


def _rms(x):
    return lax.rsqrt(jnp.mean(x * x, axis=-1, keepdims=True) + RMS_EPS)


def _colsum(x):
    return jnp.sum(x, axis=0, keepdims=True)


def _f_silu(c):
    return c * _sigmoid(c)


def _f_pre(x, a, sh):
    return (x * _rms(x)) * a + sh


def _f_post_add(x, y, bv):
    return x + (y * _rms(y)) * bv


def _f_post_pre(x, y, bv, a, sh):
    x = _f_post_add(x, y, bv)
    return x, _f_pre(x, a, sh)


def _f_pre_post_bwd(dxo, dh, x, y, a, bv):
    dx, d_a, d_sh = _f_pre_bwd(dxo, dh, x, a)
    dy, d_bv = _f_post_bwd(dx, y, bv)
    return dx, dy, d_a, d_sh, d_bv


def _f_post_bwd(dxo, y, bv):
    ry = _rms(y)
    yn = y * ry
    dyn = dxo * bv
    dy = ry * (dyn - yn * jnp.mean(dyn * yn, axis=-1, keepdims=True))
    return dy, _colsum(dxo * yn)


def _f_pre_bwd(dxo, dh, x, a):
    r = _rms(x)
    xn = x * r
    dxn = dh * a
    dx = dxo + r * (dxn - xn * jnp.mean(dxn * xn, axis=-1, keepdims=True))
    return dx, _colsum(dh * xn), _colsum(dh)


def _f_swiglu_bwd(g, u, dm):
    g, u, dm = g.astype(F32), u.astype(F32), dm.astype(F32)
    sg = _sigmoid(g)
    dg = dm * u * (sg * (1.0 + g * (1.0 - sg)))
    du = dm * (g * sg)
    return ((dg, du),)


def _gelu_t(x):
    return jnp.tanh(GELU_C * (x + GELU_A * x * x * x))


def _f_gelu_in(y0, u, dvec):
    y = y0 + dvec * u
    return 0.5 * y * (1.0 + _gelu_t(y)), y


def _f_gelu_bwd(dge, y, u):
    t = _gelu_t(y)
    dy = dge * (0.5 * (1.0 + t) + 0.5 * y * (1.0 - t * t) * GELU_C * (1.0 + 3.0 * GELU_A * y * y))
    return dy, _colsum(dy * u)


def _f_du_fin(du0, dys, dvec):
    return du0 + dvec * dys.astype(F32)


def _f_merge(zv, zg, yb, ga, gb):
    zv, zg, yb, ga, gb = [a.astype(F32) for a in (zv, zg, yb, ga, gb)]
    return _sigmoid(ga) * (zv * _sigmoid(zg)) + _sigmoid(gb) * yb


def _f_merge_bwd(dmg, zv, zg, yb, ga, gb):
    zv, zg, yb, ga, gb = [a.astype(F32) for a in (zv, zg, yb, ga, gb)]
    sa, sb, sz = _sigmoid(ga), _sigmoid(gb), _sigmoid(zg)
    ya = zv * sz
    dya = dmg * sa
    dga = dmg * ya * sa * (1.0 - sa)
    dyb = dmg * sb
    dgb = dmg * yb * sb * (1.0 - sb)
    dzv = dya * sz
    dzg = dya * zv * sz * (1.0 - sz)
    return (dzv, dzg), dyb, (dga, dgb)


def _f_sum_parts(*parts):
    acc = parts[0].astype(F32)
    for p in parts[1:]:
        acc = acc + p.astype(F32)
    return acc


def _f_adamw(*args):
    parts, (w, m, v) = args[:-3], args[-3:]
    g = _f_sum_parts(*parts)
    m = ADAM_B1 * m + (1.0 - ADAM_B1) * g
    v = ADAM_B2 * v + (1.0 - ADAM_B2) * (g * g)
    m_hat = m / (1.0 - ADAM_B1 ** ADAM_STEP)
    v_hat = v / (1.0 - ADAM_B2 ** ADAM_STEP)
    delta = -ADAM_LR * (m_hat / (jnp.sqrt(v_hat) + ADAM_EPS) + ADAM_WD * w)
    return g, delta, m, v


def adamw(parts3, w, m, v, *, name, into=None):
    c = w.shape[1]
    rows = [(parts3, p) for p in range(parts3.shape[0])] + [w, m, v]
    if into is None:
        return rowwise(_f_adamw, rows, [], [((c,), F32)] * 4, name=name)
    bases, total_rows, row0 = into
    outs = [((c,), F32, (c, 0, total_rows, row0), bases[k] if bases else None) for k in range(4)]
    return rowwise(_f_adamw, rows, [], outs, name=name)


def _tri_dot(tri, x, dims=(((1,), (0,)), ((), ()))):
    x1 = x.astype(BF16)
    r1 = x - x1.astype(F32)
    x2 = r1.astype(BF16)
    x3 = (r1 - x2.astype(F32)).astype(BF16)
    dot = functools.partial(lax.dot_general, dimension_numbers=dims, preferred_element_type=F32)
    return dot(tri, x1) + dot(tri, x2) + dot(tri, x3)


def cum_fwd(proj, fb, *, col_block, name, t=256):
    n = proj.shape[0]
    t = _pick(n, t, SUBLANES)

    def body(f_ref, fb_ref, cum_ref, car_ref):
        @pl.when(pl.program_id(0) == 0)
        def _():
            car_ref[...] = jnp.zeros_like(car_ref)

        x = f_ref[...] + fb_ref[...]
        lf = jnp.minimum(x, 0.0) - jnp.log(1.0 + jnp.exp(-jnp.abs(x)))
        r = lax.broadcasted_iota(jnp.int32, (t, t), 0)
        c = lax.broadcasted_iota(jnp.int32, (t, t), 1)
        cs = _tri_dot((c <= r).astype(BF16), lf) + car_ref[0:1, :]
        cum_ref[...] = cs
        car_ref[0:1, :] = cs[t - 1:t, :]

    return pl.pallas_call(
        body, name=name, grid=(n // t,),
        in_specs=[pl.BlockSpec((t, LANES), lambda i: (i, col_block)),
                  pl.BlockSpec((1, LANES), lambda i: (0, 0))],
        out_specs=pl.BlockSpec((t, LANES), lambda i: (i, 0)),
        out_shape=jax.ShapeDtypeStruct((n, LANES), F32),
        scratch_shapes=[pltpu.VMEM((SUBLANES, LANES), F32)],
        compiler_params=_params(("arbitrary",)),
    )(proj, fb)


def cum_bwd(dcum, proj, fb, *, col_block, name, t=256):
    n = proj.shape[0]
    slabs = dcum.shape[0]
    t = _pick(n, t, SUBLANES)
    nb = n // t

    def body(dc_ref, f_ref, fb_ref, df_ref, dfb_ref, car_ref):
        @pl.when(pl.program_id(0) == 0)
        def _():
            car_ref[...] = jnp.zeros_like(car_ref)
            dfb_ref[...] = jnp.zeros_like(dfb_ref)

        r = lax.broadcasted_iota(jnp.int32, (t, t), 0)
        c = lax.broadcasted_iota(jnp.int32, (t, t), 1)
        dl = _tri_dot((c >= r).astype(BF16), jnp.sum(dc_ref[...], axis=0)) + car_ref[0:1, :]
        car_ref[0:1, :] = dl[0:1, :]
        x = f_ref[...] + fb_ref[...]
        df = dl * (1.0 / (1.0 + jnp.exp(x)))
        df_ref[...] = df
        dfb_ref[...] += _colsum(df)

    return pl.pallas_call(
        body, name=name, grid=(nb,),
        in_specs=[pl.BlockSpec((slabs, t, LANES), lambda i: (0, nb - 1 - i, 0)),
                  pl.BlockSpec((t, LANES), lambda i: (nb - 1 - i, col_block)),
                  pl.BlockSpec((1, LANES), lambda i: (0, 0))],
        out_specs=[pl.BlockSpec((t, LANES), lambda i: (nb - 1 - i, 0)),
                   pl.BlockSpec((1, LANES), lambda i: (0, 0))],
        out_shape=[jax.ShapeDtypeStruct((n, LANES), F32), jax.ShapeDtypeStruct((1, LANES), F32)],
        scratch_shapes=[pltpu.VMEM((SUBLANES, LANES), F32)],
        compiler_params=_params(("arbitrary",)),
    )(dcum, proj, fb)


_NT =(((1,), (1,)), ((), ()))
_TN = (((0,), (0,)), ((), ()))


def _lane_sums_as_row(x):
    return _tri_dot(jnp.ones((SUBLANES, x.shape[1]), BF16), x, _NT)[0:1, :]


def _causal_keep(t):
    return lax.broadcasted_iota(jnp.int32, (t, t), 1) <= lax.broadcasted_iota(jnp.int32, (t, t), 0)


def attn_fwd(q, k, v, crow, *, name, t=512, hb=8, comm=None):
    h_n, n, dh = q.shape
    hb = min(hb, h_n)
    t = _pick(n, t)
    nb = n // t
    ng = h_n // hb
    host = _Hosted(comm, 4, 2, 3)

    def body(*refs):
        (q_ref, k_ref, v_ref, cr_ref, o_ref, lse_ref, m_sc, l_sc, acc_sc), crefs = host.split(refs)
        g, i, j = pl.program_id(0), pl.program_id(1), pl.program_id(2)
        host.phase("start", (g == 0) & (i == 0) & (j == 0), crefs)
        host.phase("mid", (g == ng - 1) & (i == (3 * nb) // 4) & (j == 0), crefs)

        @pl.when(j == 0)
        def _():
            m_sc[...] = jnp.full_like(m_sc, -jnp.inf)
            l_sc[...] = jnp.zeros_like(l_sc)
            acc_sc[...] = jnp.zeros_like(acc_sc)

        def update(diagonal):
            keep = _causal_keep(t) if diagonal else None
            heads = range(hb)
            ss = [lax.dot_general(q_ref[h], k_ref[h], _NT, preferred_element_type=F32) for h in heads]
            pairs, alphas = [], []
            for h in heads:
                s = ss[h] - cr_ref[h]
                if diagonal:
                    s = jnp.where(keep, s, -jnp.inf)
                m_prev = m_sc[h]
                m_new = jnp.maximum(m_prev, jnp.max(s, axis=-1, keepdims=True))
                p = jnp.exp(s - m_new)
                alpha = jnp.exp(m_prev - m_new)
                l_sc[h] = alpha * l_sc[h] + jnp.sum(p, axis=-1, keepdims=True)
                m_sc[h] = m_new
                p_hi = p.astype(BF16)
                pairs.append((p_hi, (p - p_hi.astype(F32)).astype(BF16)))
                alphas.append(alpha)
            for h in heads:
                vv = v_ref[h]
                acc_sc[h] = (alphas[h] * acc_sc[h] + jnp.dot(pairs[h][0], vv, preferred_element_type=F32)
                             + jnp.dot(pairs[h][1], vv, preferred_element_type=F32))

        @pl.when(j < i)
        def _():
            update(False)

        @pl.when(j == i)
        def _():
            update(True)

        @pl.when(j == nb - 1)
        def _():
            o_ref[...] = acc_sc[...] / l_sc[...]
            lane0 = lax.broadcasted_iota(jnp.int32, (t, LANES), 1) == 0
            for h in range(hb):
                lse_col = m_sc[h] + jnp.log(l_sc[h])
                lse_ref[h] = _lane_sums_as_row(jnp.where(lane0, lse_col, 0.0))

        host.phase("finish", (g == ng - 1) & (i == nb - 1) & (j == nb - 1), crefs)

    qspec = pl.BlockSpec((hb, t, dh), lambda g, i, j: (g, i, 0))
    kspec = pl.BlockSpec((hb, t, dh), lambda g, i, j: (g, jnp.minimum(j, i), 0))
    in_specs, out_specs, out_shape, scratch, extra = host.specs(
        [qspec, kspec, kspec, pl.BlockSpec((hb, 1, t), lambda g, i, j: (g, 0, jnp.minimum(j, i)))],
        [qspec, pl.BlockSpec((hb, 1, t), lambda g, i, j: (g, 0, i))],
        [jax.ShapeDtypeStruct((h_n, n, dh), F32), jax.ShapeDtypeStruct((h_n, 1, n), F32)],
        [pltpu.VMEM((hb, t, 1), F32), pltpu.VMEM((hb, t, 1), F32), pltpu.VMEM((hb, t, dh), F32)])
    return pl.pallas_call(
        body, name=name, grid=(ng, nb, nb),
        in_specs=in_specs, out_specs=out_specs, out_shape=out_shape, scratch_shapes=scratch,
        compiler_params=_params(("arbitrary", "arbitrary", "arbitrary")),
    )(q, k, v, crow, *extra)


def attn_delta(do, o, *, name, t=512, hb=8):
    h_n, n, dh = do.shape
    hb = min(hb, h_n)
    t = _pick(n, t)

    def body(do_ref, o_ref, dl_ref):
        for h in range(hb):
            dl_ref[h] = _lane_sums_as_row(do_ref[h].astype(F32) * o_ref[h])

    spec = pl.BlockSpec((hb, t, dh), lambda g, i: (g, i, 0))
    return pl.pallas_call(
        body, name=name, grid=(h_n // hb, n // t),
        in_specs=[spec, spec], out_specs=pl.BlockSpec((hb, 1, t), lambda g, i: (g, 0, i)),
        out_shape=jax.ShapeDtypeStruct((h_n, 1, n), F32),
        compiler_params=_params(("parallel", "parallel")),
    )(do, o)


def attn_bwd(q, k, v, do, o, lse, cum, *, scale, tag, t=512, hb=2, comm=None):
    h_n, n, dh = q.shape
    hb = min(hb, h_n)
    t = _pick(n, t)
    nb = n // t
    dob = do.astype(BF16)
    delta = attn_delta(dob, o, name=f"attn_delta_{tag}", t=t)

    ng = h_n // hb
    host = _Hosted(comm, 7, 4, 3)

    def body(*refs):
        (q_ref, k_ref, v_ref, do_ref, lse_ref, dl_ref, cum_ref,
         dq_ref, dk_ref, dv_ref, dcum_ref, dk_acc, dv_acc, dcc_acc), crefs = host.split(refs)
        g, j, i = pl.program_id(0), pl.program_id(1), pl.program_id(2)
        host.phase("start", (g == 0) & (j == 0) & (i == 0), crefs)
        lane = lax.broadcasted_iota(jnp.int32, (t, LANES), 1)

        @pl.when((j == 0) & (i == 0))
        def _():
            dq_ref[...] = jnp.zeros_like(dq_ref)

        @pl.when(i == 0)
        def _():
            dk_acc[...] = jnp.zeros_like(dk_acc)
            dv_acc[...] = jnp.zeros_like(dv_acc)
            dcc_acc[...] = jnp.zeros_like(dcc_acc)

        def update(diagonal):
            heads = range(hb)
            r0 = pl.multiple_of(i * t, t)
            if diagonal:
                keep = lax.broadcasted_iota(jnp.int32, (t, t), 0) <= lax.broadcasted_iota(jnp.int32, (t, t), 1)
            qv, kv = [q_ref[h] for h in heads], [k_ref[h] for h in heads]
            vv, dov = [v_ref[h] for h in heads], [do_ref[h] for h in heads]
            st = [lax.dot_general(kv[h], qv[h], _NT, preferred_element_type=F32) for h in heads]
            dpt = [lax.dot_general(vv[h], dov[h], _NT, preferred_element_type=F32) for h in heads]
            pt = []
            cum_tile = cum_ref[...]
            for h in heads:
                cc = jnp.sum(jnp.where(lane == g * hb + h, cum_tile, 0.0), axis=1, keepdims=True)
                s = st[h] - cc
                if diagonal:
                    s = jnp.where(keep, s, -jnp.inf)
                pt.append(jnp.exp(s - lse_ref[h]))
            for h in heads:
                dv_acc[h] += jnp.dot(pt[h].astype(BF16), dov[h], preferred_element_type=F32)
            dsb = []
            for h in heads:
                ds = pt[h] * (dpt[h] - dl_ref[h])
                dcc_acc[h] -= jnp.sum(ds, axis=1, keepdims=True)
                dsb.append(ds.astype(BF16))
            for h in heads:
                dk_acc[h] += jnp.dot(dsb[h], qv[h], preferred_element_type=F32)
            for h in heads:
                dq_ref[h, pl.ds(r0, t), :] += lax.dot_general(dsb[h], kv[h], _TN,
                                                              preferred_element_type=F32) * scale

        @pl.when(i > j)
        def _():
            update(False)

        @pl.when(i == j)
        def _():
            update(True)

        @pl.when(i == nb - 1)
        def _():
            dk_ref[...] = dk_acc[...]
            dv_ref[...] = dv_acc[...]
            tile = jnp.zeros((t, LANES), F32)
            for h in range(hb):
                tile = tile + jnp.where(lane == g * hb + h, dcc_acc[h], 0.0)
            dcum_ref[...] = tile

        host.phase("finish", (g == ng - 1) & (j == nb - 1) & (i == nb - 1), crefs)

    qspec = pl.BlockSpec((hb, t, dh), lambda g, j, i: (g, jnp.maximum(i, j), 0))
    qrow = pl.BlockSpec((hb, 1, t), lambda g, j, i: (g, 0, jnp.maximum(i, j)))
    kspec = pl.BlockSpec((hb, t, dh), lambda g, j, i: (g, j, 0))
    in_specs, out_specs, out_shape, scratch, extra = host.specs(
        [qspec, kspec, kspec, qspec, qrow, qrow, pl.BlockSpec((t, LANES), lambda g, j, i: (j, 0))],
        [pl.BlockSpec((hb, n, dh), lambda g, j, i: (g, 0, 0)), kspec, kspec,
         pl.BlockSpec((None, t, LANES), lambda g, j, i: (g, j, 0))],
        [jax.ShapeDtypeStruct((h_n, n, dh), F32)] * 3 + [jax.ShapeDtypeStruct((ng, n, LANES), F32)],
        [pltpu.VMEM((hb, t, dh), F32), pltpu.VMEM((hb, t, dh), F32), pltpu.VMEM((hb, t, 1), F32)])
    dq, dk, dv, dcum, *arrived = pl.pallas_call(
        body, name=f"attn_bwd_{tag}", grid=(ng, nb, nb),
        in_specs=in_specs, out_specs=out_specs, out_shape=out_shape, scratch_shapes=scratch,
        compiler_params=_params(("arbitrary", "arbitrary", "arbitrary")),
    )(q, k, v, dob, lse, delta, cum, *extra)
    return [dq, dk, dv, dcum] + arrived


SCAN_STEPS = (1, 2, 4)


def ssm_scan(x, tab, *, reverse, name, s_prev=None, tt=512, out_dtype=BF16):
    n, width = x.shape
    nc, _, hw = tab.shape
    cw = 2 * hw
    assert width == nc * cw
    tt = _pick(n, tt, 2 * SUBLANES)
    nt = n // tt
    ng = tt // (2 * SUBLANES)
    with_grad = s_prev is not None

    def body(*refs):
        if with_grad:
            x_ref, s_ref, tab_ref, o_ref, g_ref, car_ref = refs
        else:
            x_ref, tab_ref, o_ref, car_ref = refs

        @pl.when(pl.program_id(1) == 0)
        def _():
            car_ref[...] = jnp.zeros_like(car_ref)
            if with_grad:
                g_ref[...] = jnp.zeros_like(g_ref)

        q_re, q_im = tab_ref[0:8, :], tab_ref[8:16, :]
        p_re = [tab_ref[16 + 16 * i:24 + 16 * i, :] for i in range(3)]
        p_im = [tab_ref[24 + 16 * i:32 + 16 * i, :] for i in range(3)]
        row = lax.broadcasted_iota(jnp.int32, (SUBLANES, hw), 0)

        def group(xr, xi, sr_, si_, carry):
            c_re, c_im = carry
            for i, d in enumerate(SCAN_STEPS):
                shift = SUBLANES - d if reverse else d
                sr, si = pltpu.roll(xr, shift, 0), pltpu.roll(xi, shift, 0)
                xr, xi = (xr + p_re[i] * sr - p_im[i] * si,
                          xi + p_re[i] * si + p_im[i] * sr)
            xr, xi = (xr + q_re * c_re - q_im * c_im,
                      xi + q_re * c_im + q_im * c_re)
            if with_grad:
                nr = jnp.where(row < SUBLANES - 1, pltpu.roll(xr, SUBLANES - 1, 0), c_re)
                ni = jnp.where(row < SUBLANES - 1, pltpu.roll(xi, SUBLANES - 1, 0), c_im)
                g_ref[:, 0:hw] += nr * sr_ + ni * si_
                g_ref[:, hw:cw] += ni * sr_ - nr * si_
            if reverse:
                return xr, xi, (xr[0:1, :], xi[0:1, :])
            return xr, xi, (xr[SUBLANES - 1:SUBLANES, :], xi[SUBLANES - 1:SUBLANES, :])

        def pair(gi, carry):
            g = (ng - 1 - gi) if reverse else gi
            r0 = pl.multiple_of(g * 2 * SUBLANES, 2 * SUBLANES)
            rows = pl.ds(r0, 2 * SUBLANES)
            xr, xi = x_ref[rows, 0:hw].astype(F32), x_ref[rows, hw:cw].astype(F32)
            if with_grad:
                sr, si = s_ref[rows, 0:hw].astype(F32), s_ref[rows, hw:cw].astype(F32)
            halves = [slice(0, SUBLANES), slice(SUBLANES, 2 * SUBLANES)]
            done = [None, None]
            for k in ((1, 0) if reverse else (0, 1)):
                h = halves[k]
                o_re, o_im, carry = group(xr[h], xi[h], sr[h] if with_grad else None,
                                          si[h] if with_grad else None, carry)
                done[k] = (o_re, o_im)
            o_ref[rows, 0:hw] = jnp.concatenate([done[0][0], done[1][0]], axis=0).astype(o_ref.dtype)
            o_ref[rows, hw:cw] = jnp.concatenate([done[0][1], done[1][1]], axis=0).astype(o_ref.dtype)
            return carry

        c_re, c_im = lax.fori_loop(0, ng, pair, (car_ref[0:1, 0:hw], car_ref[0:1, hw:cw]),
                                   unroll=min(ng, 2))
        car_ref[0:1, 0:hw] = c_re
        car_ref[0:1, hw:cw] = c_im

    if reverse:
        xspec = pl.BlockSpec((tt, cw), lambda c, t: (nt - 1 - t, c))
    else:
        xspec = pl.BlockSpec((tt, cw), lambda c, t: (t, c))
    tspec = pl.BlockSpec((None, 64, hw), lambda c, t: (c, 0, 0))
    in_specs = [xspec, xspec, tspec] if with_grad else [xspec, tspec]
    out_specs = [xspec]
    out_shape = [jax.ShapeDtypeStruct((n, width), out_dtype)]
    if with_grad:
        out_specs.append(pl.BlockSpec((None, SUBLANES, cw), lambda c, t: (c, 0, 0)))
        out_shape.append(jax.ShapeDtypeStruct((nc, SUBLANES, cw), F32))
    operands = (x, s_prev, tab) if with_grad else (x, tab)
    return pl.pallas_call(
        body, name=name, grid=(nc, nt),
        in_specs=in_specs, out_specs=out_specs, out_shape=out_shape,
        scratch_shapes=[pltpu.VMEM((SUBLANES, cw), F32)],
        compiler_params=_params(("parallel", "arbitrary")),
    )(*operands)


def _slot(pos):
    return 4 * pos[0] + 2 * pos[1] + pos[2]


def _comm_scratch(n):
    return [pltpu.SemaphoreType.DMA((7 * n,)), pltpu.SemaphoreType.DMA((7 * n,)), pltpu.SemaphoreType.DMA((n,))]


def _gather_copies(ins, outs, sems):
    send_sems, recv_sems, local_sems = sems
    n = len(ins)
    x, y, c = lax.axis_index("x"), lax.axis_index("y"), lax.axis_index("c")
    me, sibling = (x, y, c), (x, y, 1 - c)
    chips = [(1 - x, y), (x, 1 - y), (1 - x, 1 - y)]

    def copy(t, k, block, to, src=None):
        dst = outs[t].at[_slot(block)]
        return pltpu.make_async_remote_copy(
            src_ref=dst if src is None else src, dst_ref=dst,
            send_sem=send_sems.at[7 * t + k], recv_sem=recv_sems.at[7 * t + k],
            device_id=to, device_id_type=MESH)

    jc = list(enumerate(chips))
    return dict(
        mine=[pltpu.make_async_copy(ins[t], outs[t].at[_slot(me)], local_sems.at[t]) for t in range(n)],
        first=[cp for t in range(n) for cp in
               [copy(t, 0, me, sibling, src=ins[t])] + [copy(t, 1 + j, me, (*chip, c), src=ins[t]) for j, chip in jc]],
        arrive=[copy(t, 1 + j, (*chip, c), me) for t in range(n) for j, chip in jc],
        passed=[copy(t, 4 + j, (*chip, c), sibling) for t in range(n) for j, chip in jc],
        from_sibling=[cp for t in range(n) for cp in
                      [copy(t, 0, sibling, me)] + [copy(t, 4 + j, (*chip, 1 - c), me) for j, chip in jc]])


def _gather_start(ins, outs, sems):
    cps = _gather_copies(ins, outs, sems)
    for cp in cps["mine"] + cps["first"]:
        cp.start()


def _gather_forward(ins, outs, sems):
    cps = _gather_copies(ins, outs, sems)
    for arrived, onward in zip(cps["arrive"], cps["passed"]):
        arrived.wait_recv()
        onward.start()


def _gather_finish(ins, outs, sems):
    cps = _gather_copies(ins, outs, sems)
    for cp in cps["from_sibling"]:
        cp.wait_recv()
    for cp in cps["first"] + cps["passed"]:
        cp.wait_send()
    for cp in cps["mine"]:
        cp.wait()


def gather_comm(arrs):
    return dict(ins=list(arrs), out_shape=[jax.ShapeDtypeStruct((N_DEV,) + a.shape, a.dtype) for a in arrs],
                scratch=_comm_scratch(len(arrs)), start=_gather_start, mid=_gather_forward, finish=_gather_finish)


def _exchange_copies(ins, outs, sems):
    send_sems, recv_sems, local_sems = sems
    n = len(ins)
    me = (lax.axis_index("x"), lax.axis_index("y"), lax.axis_index("c"))
    peers = []
    for k in range(1, N_DEV):
        flip = ((k >> 2) & 1, (k >> 1) & 1, k & 1)
        peers.append(tuple(1 - p if f else p for p, f in zip(me, flip)))

    def copy(t, k, peer, dst_slot):
        return pltpu.make_async_remote_copy(
            src_ref=ins[t].at[_slot(peer)], dst_ref=outs[t].at[dst_slot],
            send_sem=send_sems.at[7 * t + k], recv_sem=recv_sems.at[7 * t + k],
            device_id=peer, device_id_type=MESH)

    return dict(
        mine=[pltpu.make_async_copy(ins[t].at[_slot(me)], outs[t].at[_slot(me)], local_sems.at[t])
              for t in range(n)],
        send=[copy(t, k, peer, _slot(me)) for t in range(n) for k, peer in enumerate(peers)],
        both=[copy(t, k, peer, _slot(peer)) for t in range(n) for k, peer in enumerate(peers)])


def _exchange_start(ins, outs, sems):
    cps = _exchange_copies(ins, outs, sems)
    for cp in cps["mine"] + cps["send"]:
        cp.start()


def _exchange_finish(ins, outs, sems):
    cps = _exchange_copies(ins, outs, sems)
    for cp in cps["both"]:
        cp.wait()
    for cp in cps["mine"]:
        cp.wait()


def exchange_comm(arrs):
    return dict(ins=list(arrs), out_shape=[jax.ShapeDtypeStruct(a.shape, a.dtype) for a in arrs],
                scratch=_comm_scratch(len(arrs)), start=_exchange_start, mid=None, finish=_exchange_finish)


def run_comm(comm, *, name):
    n_in, n_out = len(comm["ins"]), len(comm["out_shape"])

    def body(*refs):
        ins, outs, sems = refs[:n_in], refs[n_in:n_in + n_out], refs[n_in + n_out:]
        comm["start"](ins, outs, sems)
        if comm["mid"] is not None:
            comm["mid"](ins, outs, sems)
        comm["finish"](ins, outs, sems)

    any_spec = pl.BlockSpec(memory_space=pl.ANY)
    return pl.pallas_call(
        body, name=name, in_specs=[any_spec] * n_in, out_specs=[any_spec] * n_out,
        out_shape=comm["out_shape"], scratch_shapes=comm["scratch"],
    )(*comm["ins"])


class _Hosted:
    def __init__(self, comm, n_in, n_out, n_scratch):
        self.comm = comm
        self.n_ci = len(comm["ins"]) if comm else 0
        self.n_co = len(comm["out_shape"]) if comm else 0
        self.n_in, self.n_out, self.n_scratch = n_in, n_out, n_scratch

    def split(self, refs):
        a = self.n_in
        b = a + self.n_ci
        c = b + self.n_out
        e = c + self.n_co
        f = e + self.n_scratch
        return refs[:a] + refs[b:c] + refs[e:f], (refs[a:b], refs[c:e], refs[f:])

    def phase(self, which, when, crefs):
        fn = self.comm[which] if self.comm else None
        if fn is not None:
            pl.when(when)(lambda: fn(*crefs))

    def specs(self, in_specs, out_specs, out_shape, scratch):
        any_spec = pl.BlockSpec(memory_space=pl.ANY)
        if not self.comm:
            return in_specs, out_specs, out_shape, scratch, ()
        return (in_specs + [any_spec] * self.n_ci, out_specs + [any_spec] * self.n_co,
                out_shape + self.comm["out_shape"], scratch + self.comm["scratch"], tuple(self.comm["ins"]))


def all_gather(arrs, *, name):
    return run_comm(gather_comm(arrs), name=name)


def _discretise(a_re, a_im, log_dt, b_re, b_im):
    ar = jnp.minimum(a_re, -1e-4)
    dt = jnp.exp(log_dt)[:, None]
    e, ph = ar * dt, a_im * dt
    mag = jnp.exp(e)
    lr, li = mag * jnp.cos(ph), mag * jnp.sin(ph)
    den = ar * ar + a_im * a_im
    nr, ni = lr - 1.0, li
    cr = (nr * ar + ni * a_im) / den
    ci = (ni * ar - nr * a_im) / den
    bb_re = cr[..., None] * b_re - ci[..., None] * b_im
    bb_im = cr[..., None] * b_im + ci[..., None] * b_re
    return e, ph, bb_re, bb_im


def _lam_pow(e, ph, k, conj):
    mag = jnp.exp(k * e)
    return mag * jnp.cos(k * ph), (-1.0 if conj else 1.0) * mag * jnp.sin(k * ph)


def _scan_table(e, ph, nc, reverse):
    hw = e.size // nc
    e, ph = e.reshape(nc, 1, hw), ph.reshape(nc, 1, hw)
    j = jnp.arange(SUBLANES, dtype=F32).reshape(1, SUBLANES, 1)
    kq = (SUBLANES - j) if reverse else (j + 1.0)
    parts = list(_lam_pow(e, ph, kq, reverse))
    for step in SCAN_STEPS:
        inside = (j < SUBLANES - step) if reverse else (j >= step)
        p_re, p_im = _lam_pow(e, ph, jnp.full_like(j, float(step)), reverse)
        parts += [jnp.where(inside, p_re, 0.0), jnp.where(inside, p_im, 0.0)]
    return jnp.concatenate(parts, axis=1)


def _blockdiag(m, nc):
    g, a, b = m.shape
    gc = g // nc
    m = m.reshape(nc, gc, a, b)
    eye = jnp.eye(gc, dtype=m.dtype)
    return jnp.einsum("cgab,gh->cgahb", m, eye).reshape(nc, gc * a, gc * b)


def _blockdiag_take(m, g):
    nc = m.shape[0]
    gc = g // nc
    a, b = m.shape[1] // gc, m.shape[2] // gc
    m = m.reshape(nc, gc, a, gc, b)
    eye = jnp.eye(gc, dtype=m.dtype)
    return jnp.einsum("cgahb,gh->cgab", m, eye).reshape(g, a, b)


def kernel(x, c, mod_w, mod_b, norm_pre, norm_post, ffn_w_in, ffn_w_out, mix_w_in, forget_b, ssm_a_re, ssm_a_im, ssm_log_dt, ssm_b_re, ssm_b_im, ssm_c_re, ssm_c_im, ssm_d, glu_w, attn_w_out, mix_w_out, loss_target, m_mod_w, m_mod_b, m_norm_pre, m_norm_post, m_ffn_w_in, m_ffn_w_out, m_mix_w_in, m_forget_b, m_ssm_a_re, m_ssm_a_im, m_ssm_log_dt, m_ssm_b_re, m_ssm_b_im, m_ssm_c_re, m_ssm_c_im, m_ssm_d, m_glu_w, m_attn_w_out, m_mix_w_out, v_mod_w, v_mod_b, v_norm_pre, v_norm_post, v_ffn_w_in, v_ffn_w_out, v_mix_w_in, v_forget_b, v_ssm_a_re, v_ssm_a_im, v_ssm_log_dt, v_ssm_b_re, v_ssm_b_im, v_ssm_c_re, v_ssm_c_im, v_ssm_d, v_glu_w, v_attn_w_out, v_mix_w_out):
    names = ["mod_w", "mod_b", "norm_pre", "norm_post", "ffn_w_in", "ffn_w_out", "mix_w_in", "forget_b",
             "ssm_a_re", "ssm_a_im", "ssm_log_dt", "ssm_b_re", "ssm_b_im", "ssm_c_re", "ssm_c_im", "ssm_d",
             "glu_w", "attn_w_out", "mix_w_out"]
    w_in = dict(zip(names, [mod_w, mod_b, norm_pre, norm_post, ffn_w_in, ffn_w_out, mix_w_in, forget_b,
                            ssm_a_re, ssm_a_im, ssm_log_dt, ssm_b_re, ssm_b_im, ssm_c_re, ssm_c_im, ssm_d,
                            glu_w, attn_w_out, mix_w_out]))
    m_in = dict(zip(names, [m_mod_w, m_mod_b, m_norm_pre, m_norm_post, m_ffn_w_in, m_ffn_w_out, m_mix_w_in,
                            m_forget_b, m_ssm_a_re, m_ssm_a_im, m_ssm_log_dt, m_ssm_b_re, m_ssm_b_im,
                            m_ssm_c_re, m_ssm_c_im, m_ssm_d, m_glu_w, m_attn_w_out, m_mix_w_out]))
    v_in = dict(zip(names, [v_mod_w, v_mod_b, v_norm_pre, v_norm_post, v_ffn_w_in, v_ffn_w_out, v_mix_w_in,
                            v_forget_b, v_ssm_a_re, v_ssm_a_im, v_ssm_log_dt, v_ssm_b_re, v_ssm_b_im,
                            v_ssm_c_re, v_ssm_c_im, v_ssm_d, v_glu_w, v_attn_w_out, v_mix_w_out]))

    depth = mod_w.shape[0]
    n_tok, d = x.shape[1], x.shape[2]
    ff = ffn_w_out.shape[2] * N_DEV
    heads = forget_b.shape[1]
    sw = ssm_d.shape[1]
    g_n, p_n, n_n = ssm_b_re.shape[1:]
    aw = attn_w_out.shape[1]
    dh = aw // heads
    iw = mix_w_in.shape[2] * N_DEV
    nc = sw // LANES
    hw = g_n * p_n // nc
    mod_cols = mod_w.shape[2]
    scale = dh ** -0.5
    assert iw == sw + 3 * aw + heads + 2 * d and heads <= LANES
    assert math.log2(scale).is_integer(), "q is pre-scaled in bf16: exact only for a power of two"
    off_u, off_q, off_f = 2 * d, 2 * d + sw, 2 * d + sw + 3 * aw
    iwp = off_f + LANES
    assert off_u % sw == 0 and off_q % aw == 0 and off_f % LANES == 0

    me = 4 * lax.axis_index("x") + 2 * lax.axis_index("y") + lax.axis_index("c")
    x2 = x.reshape(n_tok, d)
    tgt = loss_target.reshape(n_tok, d)

    silu_c = rowwise(_f_silu, [c], [], [((d,), F32)], name="silu_c")[0]
    big = ["ffn_w_in", "ffn_w_out", "mix_w_in", "glu_w", "attn_w_out", "mix_w_out"]
    ffn1 = [("ffn_w_in", 0), ("ffn_w_out", 0)]
    mix = [("mix_w_in", None), ("glu_w", None), ("attn_w_out", None), ("mix_w_out", None)]
    ffn2 = [("ffn_w_in", 1), ("ffn_w_out", 1)]

    def riders(l):
        nxt = [(l + 1, p) for p in ffn1 + mix] if l + 1 < depth else []
        return [(l, p) for p in ffn2] + nxt

    def piece_of(dct, piece, l):
        name, j = piece
        return dct[name][l] if j is None else dct[name][l][j]

    row_sharded = ("ffn_w_out", "mix_w_out")

    def send_shard(p, l):
        a = piece_of(w_in, p, l).astype(BF16)
        return a if p[0] in row_sharded else a.T

    def shards(l, pieces):
        return [send_shard(p, l) for p in pieces]

    cut = [0, sw, sw + aw, sw + 2 * aw, sw + 3 * aw, sw + 3 * aw + heads, sw + 3 * aw + heads + d, iw]
    lw = [dict(win_t=[None, None], wout=[None, None]) for _ in range(depth)]

    def install(l, pieces, gathered):
        for (name, j), g in zip(pieces, gathered):
            whole = g.reshape(-1, g.shape[-1])
            if name == "ffn_w_in":
                lw[l]["win_t"][j] = whole
            elif name == "ffn_w_out":
                lw[l]["wout"][j] = whole
            elif name == "mix_w_in":
                seg = lambda i: whole[cut[i]:cut[i + 1]]
                lw[l]["wmi_t"] = jnp.concatenate([seg(5), seg(6), seg(0), seg(1), seg(2), seg(3),
                                                  jnp.pad(seg(4), ((0, LANES - heads), (0, 0)))], axis=0)
            elif name == "mix_w_out":
                lw[l]["mo"] = whole
            else:
                lw[l]["glu_t" if name == "glu_w" else "ao_t"] = whole

    gathered = all_gather(
        [silu_c, norm_pre.reshape(-1, norm_pre.shape[-1]), norm_post.reshape(-1, norm_post.shape[-1])]
        + shards(0, ffn1[:1]), name="gather_first")
    sc_all = gathered[0].reshape(N_DEV, d)
    gpre = jnp.moveaxis(gathered[1].reshape(N_DEV, depth, 3, -1), 0, 2).reshape(depth, 3, d)
    gpost = jnp.moveaxis(gathered[2].reshape(N_DEV, depth, 3, -1), 0, 2).reshape(depth, 3, d)
    install(0, ffn1[:1], gathered[3:])

    sc_pad = jnp.pad(sc_all, ((0, LANES - N_DEV), (0, 0)))
    mod_part = jnp.stack([mm(sc_pad, mod_w[l], name=f"mod_fwd{l}")[:N_DEV] for l in range(depth)], axis=1)
    mod_part = mod_part + lax.dynamic_slice_in_dim(mod_b, me * mod_cols, mod_cols, axis=1)[None]
    mod_all = all_gather([mod_part], name="gather_mod")[0]
    mod_own = lax.dynamic_index_in_dim(mod_all, me, axis=1, keepdims=False)
    mod_own = mod_own.transpose(1, 0, 2).reshape(depth, 3, 3, d)
    res_w = (FFN_RES, 1.0, FFN_RES)

    def vec_a(l, i):
        return (gpre[l, i] * (1.0 + mod_own[l, i, 1])).reshape(1, d)

    def vec_sh(l, i):
        return mod_own[l, i, 0].reshape(1, d)

    def vec_b(l, i):
        return (res_w[i] * mod_own[l, i, 2] * gpost[l, i]).reshape(1, d)

    ssm = []
    for l in range(depth):
        (e, ph, bb_re, bb_im), disc_vjp = jax.vjp(_discretise,ssm_a_re[l], ssm_a_im[l], ssm_log_dt[l],
                                                  ssm_b_re[l], ssm_b_im[l])
        b_mat = jnp.concatenate([_blockdiag(bb_re.transpose(0, 2, 1), nc),
                                 _blockdiag(bb_im.transpose(0, 2, 1), nc)], axis=2)
        c_mat = jnp.concatenate([_blockdiag(ssm_c_re[l].transpose(0, 2, 1), nc),
                                 _blockdiag(-ssm_c_im[l].transpose(0, 2, 1), nc)], axis=1)
        ssm.append(dict(e=e, ph=ph, vjp=disc_vjp, b=b_mat.astype(BF16), c=c_mat.astype(BF16),
                        bt=b_mat.transpose(0, 2, 1).astype(BF16), ct=c_mat.transpose(0, 2, 1).astype(BF16),
                        tab_f=_scan_table(e, ph, nc, False), tab_r=_scan_table(e, ph, nc, True),
                        dvec=ssm_d[l].reshape(1, sw)))

    fb_pad = jnp.pad(forget_b, ((0, 0), (0, LANES - heads)))

    def heads_first(a):
        return a.reshape(n_tok, heads, dh).transpose(1, 0, 2)

    def heads_last(a):
        return a.transpose(1, 0, 2).reshape(n_tok, heads * dh)

    def mm_hosting(a, b, comm, **kw):
        if comm is None:
            return mm(a, b, **kw), []
        out, *arrived = mm(a, b, comm=comm, **kw)
        return out, arrived

    def ffn_fwd(xin, h, l, j, tag, comm=None, on_arrival=None):
        ag, au, m, *arrived = ffn_in_swiglu(h, lw[l]["win_t"][j], name=f"ffn_in_{tag}", comm=comm)
        if on_arrival is not None:
            on_arrival(arrived)
        y = mm(m, lw[l]["wout"][j], name=f"ffn_out_{tag}")
        return y, dict(x=xin, h=h, ag=ag, au=au, m=m, y=y)

    def mixer_fwd(xin, h, l, tag):
        s5 = ssm[l]
        proj = mm(h, lw[l]["wmi_t"], trans_b=True, name=f"mix_in_{tag}", out_dtype=BF16)
        projf = mm(h, lw[l]["wmi_t"][off_f:], trans_b=True, name=f"mix_in_f_{tag}")
        bu = mm_blockdiag(proj, s5["b"], a_cb0=off_u // LANES, name=f"ssm_bu_{tag}", out_dtype=BF16)
        st = ssm_scan(bu, s5["tab_f"], reverse=False, name=f"ssm_scan_{tag}")[0]
        y0 = mm_blockdiag(st, s5["c"], name=f"ssm_y_{tag}")
        ge, ys = rowwise(_f_gelu_in, [y0, (proj, sw, off_u // sw)], [s5["dvec"]],
                         [((sw,), BF16), ((sw,), F32)], name=f"gelu_{tag}")
        z = mm(ge, lw[l]["glu_t"], trans_b=True, name=f"glu_{tag}", out_dtype=BF16)
        cum = cum_fwd(projf, fb_pad[l:l + 1], col_block=0, name=f"cum_{tag}")
        crow = cum[:, :heads].T[:, None, :]
        q, k, v = [heads_first(proj[:, off_q + i * aw:off_q + (i + 1) * aw] * sc_).astype(BF16)
                   for i, sc_ in enumerate((scale, 1.0, 1.0))]
        nxt = gather_comm([send_shard(p, ll) for ll, p in riders(l)])
        o, lse, *arrived = attn_fwd(q, k, v, crow, name=f"attn_{tag}", comm=nxt)
        for (ll, p), got in zip(riders(l), arrived):
            install(ll, [p], [got])
        attn = heads_last(o).astype(BF16)
        yb = mm(attn, lw[l]["ao_t"], trans_b=True, name=f"attn_out_{tag}", out_dtype=BF16)
        mg = rowwise(_f_merge, [(z, d, 0), (z, d, 1), yb, (proj, d, 0), (proj, d, 1)], [],
                     [((d,), BF16)], name=f"merge_{tag}")[0]
        y = mm(mg, lw[l]["mo"], name=f"mix_out_{tag}")
        saved = dict(x=xin, h=h, proj=proj, projf=projf, st=st, ys=ys, ge=ge, z=z, q=q, k=k, v=v, o=o, lse=lse,
                     cum=cum, attn=attn, yb=yb, mg=mg, y=y)
        return y, saved

    subs = [(l, i) for l in range(depth) for i in range(3)]
    tag_of = lambda l, i: f"l{l}" + "amb"[i]
    saved = [[None] * 3 for _ in range(depth)]
    xc = x2
    h = rowwise(_f_pre, [xc], [vec_a(0, 0), vec_sh(0, 0)], [((d,), BF16)], name="pre_first")[0]
    for s, (l, i) in enumerate(subs):
        tag = tag_of(l, i)
        if i == 1:
            y, saved[l][i] = mixer_fwd(xc, h, l, tag)
        else:
            if (l, i) == (0, 0):
                late = ffn1[1:] + mix
                y, saved[l][i] = ffn_fwd(xc, h, l, 0, tag, gather_comm(shards(0, late)),
                                         lambda got: install(0, late, got))
            else:
                y, saved[l][i] = ffn_fwd(xc, h, l, i // 2, tag)
        if s + 1 < len(subs):
            l2, i2 = subs[s + 1]
            xc, h = rowwise(_f_post_pre, [xc, y], [vec_b(l, i), vec_a(l2, i2), vec_sh(l2, i2)],
                            [((d,), F32), ((d,), BF16)], name=f"post_pre_{tag}")

    def f_tail(xf, yl, t, bv):
        e_ = _f_post_add(xf, yl, bv) - t
        dxo = e_ * (1.0 / d)
        dyl, d_bv_ = _f_post_bwd(dxo, yl, bv)
        return dxo, dyl, _colsum(e_ * e_), d_bv_

    l_last, i_last = subs[-1]
    dx, dy, sq, d_bv = rowwise(f_tail, [xc, y, tgt], [vec_b(l_last, i_last)], [((d,), F32), ((d,), BF16)], [d, d],
                               name="loss_tail")
    loss_part = 0.5 * jnp.sum(sq) / d

    grads = {k: [None] * depth for k in big}
    small_g = [dict() for _ in range(depth)]
    dmod = [[None] * 3 for _ in range(depth)]
    dgpre = [[None] * 3 for _ in range(depth)]
    dgpost = [[None] * 3 for _ in range(depth)]

    def norm_grads(l, i, d_a, d_sh, d_bv):
        d_a, d_sh, d_bv = d_a.reshape(d), d_sh.reshape(d), d_bv.reshape(d)
        dmod[l][i] = jnp.stack([d_sh, d_a * gpre[l, i], res_w[i] * gpost[l, i] * d_bv])
        dgpre[l][i] = d_a * (1.0 + mod_own[l, i, 1])
        dgpost[l][i] = res_w[i] * mod_own[l, i, 2] * d_bv

    def ffn_bwd(dy, sv, l, j, tag, comm_dw=None, comm_dx_of=None):
        dm = mm(dy, lw[l]["wout"][j], trans_b=True, name=f"ffn_out_dx_{tag}", out_dtype=BF16)
        g_out = mm(sv["m"], dy, trans_a=True, name=f"ffn_out_dw_{tag}", out_dtype=BF16, tm=1408, tn=1024)
        da = rowwise(_f_swiglu_bwd, [sv["ag"], sv["au"], dm], [], [((ff, ff), BF16)],
                     name=f"swiglu_bwd_{tag}")[0]
        g_in, arrived_dw = mm_hosting(sv["h"], da, comm_dw, trans_a=True, name=f"ffn_in_dw_{tag}",
                                      out_dtype=BF16)
        comm_dx = comm_dx_of(g_in, g_out) if comm_dx_of is not None else None
        dh_, arrived_dx = mm_hosting(da, lw[l]["win_t"][j], comm_dx, name=f"ffn_in_dx_{tag}")
        return dh_, g_in, g_out, arrived_dw, arrived_dx

    def mixer_bwd(dy, sv, l, tag, comm):
        s5 = ssm[l]
        proj = sv["proj"]
        dmg = mm(dy, lw[l]["mo"], trans_b=True, name=f"mix_out_dx_{tag}", out_dtype=BF16)
        g_mo = mm(sv["mg"], dy, trans_a=True, name=f"mix_out_dw_{tag}", out_dtype=BF16)
        dz, dyb, dproj = rowwise(
            _f_merge_bwd, [dmg, (sv["z"], d, 0), (sv["z"], d, 1), sv["yb"], (proj, d, 0), (proj, d, 1)], [],
            [((d, d), BF16), ((d,), BF16), ((d, d), BF16, (iwp, 0))], name=f"merge_bwd_{tag}")
        dge = mm(dz, lw[l]["glu_t"], name=f"glu_dx_{tag}")
        g_glu = mm(sv["ge"], dz, trans_a=True, name=f"glu_dw_{tag}", out_dtype=BF16)
        dys, d_dvec = rowwise(_f_gelu_bwd, [dge, sv["ys"], (proj, sw, off_u // sw)], [], [((sw,), BF16)],
                              [sw], name=f"gelu_bwd_{tag}")
        gadj = mm_blockdiag(dys, s5["ct"], name=f"ssm_dy_{tag}", out_dtype=BF16)
        adj, dlam8 = ssm_scan(gadj, s5["tab_r"], reverse=True, s_prev=sv["st"], name=f"ssm_scan_bwd_{tag}")
        du0 = mm_blockdiag(adj, s5["bt"], name=f"ssm_du_{tag}")
        d_bmat = mm_blockdiag_tn(proj, adj, g_n=nc, ka=LANES, kb=2 * hw, a_cb0=off_u // LANES,
                                 name=f"ssm_db_{tag}")
        d_cmat = mm_blockdiag_tn(sv["st"], dys, g_n=nc, ka=2 * hw, kb=LANES, name=f"ssm_dc_{tag}")
        dproj = rowwise(_f_du_fin, [du0, dys], [s5["dvec"]], [((sw,), BF16, (iwp, off_u // sw), dproj)],
                        name=f"ssm_du_fin_{tag}")[0]
        dlam = jnp.sum(dlam8, axis=1)
        dlam_re, dlam_im = dlam[:, :hw].reshape(g_n, p_n), dlam[:, hw:].reshape(g_n, p_n)
        dbb_re = _blockdiag_take(d_bmat[:, :, :hw], g_n).transpose(0, 2, 1)
        dbb_im = _blockdiag_take(d_bmat[:, :, hw:], g_n).transpose(0, 2, 1)
        mag = jnp.exp(s5["e"])
        lr, li = mag * jnp.cos(s5["ph"]), mag * jnp.sin(s5["ph"])
        d_e = dlam_re * lr + dlam_im * li
        d_ph = -dlam_re * li + dlam_im * lr
        da_re, da_im, dlog_dt, db_re, db_im = s5["vjp"]((d_e, d_ph, dbb_re, dbb_im))
        small_g[l].update(
            ssm_a_re=da_re, ssm_a_im=da_im, ssm_log_dt=dlog_dt, ssm_b_re=db_re, ssm_b_im=db_im,
            ssm_c_re=_blockdiag_take(d_cmat[:, :hw, :], g_n).transpose(0, 2, 1),
            ssm_c_im=-_blockdiag_take(d_cmat[:, hw:, :], g_n).transpose(0, 2, 1),
            ssm_d=d_dvec.reshape(sw))
        dattn = mm(dyb, lw[l]["ao_t"], name=f"attn_out_dx_{tag}", out_dtype=BF16)
        g_ao = mm(sv["attn"], dyb, trans_a=True, name=f"attn_out_dw_{tag}", out_dtype=BF16)
        dq, dk, dv, dcum, *arrived = attn_bwd(sv["q"], sv["k"], sv["v"], heads_first(dattn), sv["o"], sv["lse"],
                                              sv["cum"], scale=scale, tag=tag, comm=comm)
        df, dfb = cum_bwd(dcum, sv["projf"], fb_pad[l:l + 1], col_block=0, name=f"cum_bwd_{tag}")
        small_g[l]["forget_b"] = dfb[0, :heads]
        for piece, off in ((heads_last(dq), off_q), (heads_last(dk), off_q + aw), (heads_last(dv), off_q + 2 * aw),
                           (df, off_f)):
            dproj = lax.dynamic_update_slice(dproj, piece.astype(BF16), (0, off))
        g_mi = mm(sv["h"], dproj, trans_a=True, name=f"mix_in_dw_{tag}", out_dtype=BF16)
        dh_ = mm(dproj, lw[l]["wmi_t"], name=f"mix_in_dx_{tag}")
        g_mi = jnp.concatenate([g_mi[:, off_u:off_f + heads], g_mi[:, :off_u]], axis=1)
        return dh_, g_mi, g_glu, g_ao, g_mo, arrived

    def split_last(a):
        return jnp.moveaxis(a.reshape(a.shape[:-1] + (N_DEV, a.shape[-1] // N_DEV)), -2, 0)

    def split_rows(a):
        return jnp.moveaxis(a.reshape(a.shape[:-2] + (N_DEV, a.shape[-2] // N_DEV, a.shape[-1])), -3, 0)

    def owner_blocks(l, pieces):
        out = []
        for name, j in pieces:
            if name == "ffn_w_in":
                out.append(split_last(g_ffn_in[l][j]))
            elif name == "ffn_w_out":
                out.append(split_rows(g_ffn_out[l][j]))
            elif name == "mix_w_out":
                out.append(split_rows(grads[name][l]))
            else:
                out.append(split_last(grads[name][l]))
        return out

    g_ffn_in = [[None, None] for _ in range(depth)]
    g_ffn_out = [[None, None] for _ in range(depth)]
    parts = {}

    def record(l, pieces, arrived):
        for p, a in zip(pieces, arrived):
            parts[(p, l)] = a

    def last_ffn_blocks(g_in, g_out):
        return exchange_comm([split_last(g_in), split_rows(g_out)])

    for s in reversed(range(len(subs))):
        l, i = subs[s]
        sv, tag = saved[l][i], tag_of(l, i)
        if i == 2:
            dh_, g_ffn_in[l][1], g_ffn_out[l][1], _, _ = ffn_bwd(dy, sv, l, 1, tag)
        elif i == 1:
            pending = exchange_comm([owner_blocks(ll, [p])[0] for ll, p in riders(l)])
            (dh_, grads["mix_w_in"][l], grads["glu_w"][l], grads["attn_w_out"][l], grads["mix_w_out"][l],
             arrived) = mixer_bwd(dy, sv, l, tag, pending)
            for (ll, p), got in zip(riders(l), arrived):
                record(ll, [p], [got])
        elif l == 0:
            dh_, g_ffn_in[l][0], g_ffn_out[l][0], arrived_mix, arrived_ffn1 = ffn_bwd(
                dy, sv, l, 0, tag, exchange_comm(owner_blocks(0, mix)), last_ffn_blocks)
            record(0, mix, arrived_mix)
            record(0, ffn1, arrived_ffn1)
        else:
            dh_, g_ffn_in[l][0], g_ffn_out[l][0], _, _ = ffn_bwd(dy, sv, l, 0, tag)
        if s > 0:
            lp, ip = subs[s - 1]
            dx, dy, d_a, d_sh, d_bv_before = rowwise(
                _f_pre_post_bwd, [dx, dh_, sv["x"], saved[lp][ip]["y"]], [vec_a(l, i), vec_b(lp, ip)],
                [((d,), F32), ((d,), BF16)], [d, d, d], name=f"pre_post_bwd_{tag}")
        else:
            dx, d_a, d_sh = rowwise(_f_pre_bwd, [dx, dh_, sv["x"]], [vec_a(l, i)], [((d,), F32)], [d, d],
                                    name=f"pre_bwd_{tag}")
            d_bv_before = None
        norm_grads(l, i, d_a, d_sh, d_bv)
        d_bv = d_bv_before
    grad_x = dx.reshape(x.shape)

    small_names = ["forget_b", "ssm_a_re", "ssm_a_im", "ssm_log_dt", "ssm_b_re", "ssm_b_im", "ssm_c_re",
                   "ssm_c_im", "ssm_d"]
    pieces = [loss_part.reshape(1), jnp.stack([jnp.stack(dmod[l]) for l in range(depth)]).reshape(-1),
              jnp.stack([jnp.stack(dgpre[l]) for l in range(depth)]).reshape(-1),
              jnp.stack([jnp.stack(dgpost[l]) for l in range(depth)]).reshape(-1)]
    pieces += [jnp.stack([small_g[l][k] for l in range(depth)]).reshape(-1) for k in small_names]
    sizes = [p.size for p in pieces]
    chunk = SUBLANES * 1024
    total = -(-sum(sizes) // chunk) * chunk
    pack = jnp.pad(jnp.concatenate(pieces), (0, total - sum(sizes))).reshape(total // 1024, 1024)
    pack_all = all_gather([pack], name="gather_small_grads")[0]
    pack_sum = rowwise(_f_sum_parts, [(pack_all, p) for p in range(N_DEV)], [], [((1024,), F32)],
                       name="sum_small_grads")[0].reshape(-1)
    offs = [0]
    for s_ in sizes:
        offs.append(offs[-1] + s_)
    take = lambda i: pack_sum[offs[i]:offs[i + 1]]
    loss = take(0).reshape(())
    g_small = {"mod_b": take(1).reshape(mod_b.shape)}
    g_pre_full, g_post_full = take(2).reshape(depth, 3, d), take(3).reshape(depth, 3, d)
    shard = norm_pre.shape[-1]
    g_small["norm_pre"] = lax.dynamic_slice_in_dim(g_pre_full, me * shard, shard, axis=2)
    g_small["norm_post"] = lax.dynamic_slice_in_dim(g_post_full, me * shard, shard, axis=2)
    for i, k in enumerate(small_names):
        g_small[k] = take(4 + i).reshape(w_in[k].shape)

    dmod_all = pack_all.reshape(N_DEV, -1)[:, offs[1]:offs[2]].reshape(N_DEV, depth, 9 * d)
    dmod_mine = lax.dynamic_slice_in_dim(dmod_all, me * mod_cols, mod_cols, axis=2)
    sct_pad = jnp.pad(sc_all.T, ((0, 0), (0, LANES - N_DEV)))
    g_mod_w = jnp.stack([
        mm(sct_pad, jnp.pad(dmod_mine[:, l], ((0, LANES - N_DEV), (0, 0))), name=f"mod_dw{l}")
        for l in range(depth)])

    out_g, out_d, out_m, out_v = {}, {}, {}, {}
    flat = lambda a: a.reshape(-1, a.shape[-1])
    res = adamw(g_mod_w.reshape(1, -1, mod_cols), flat(mod_w), flat(m_mod_w), flat(v_mod_w), name="adamw_mod_w")
    out_g["mod_w"], out_d["mod_w"], out_m["mod_w"], out_v["mod_w"] = [r.reshape(mod_w.shape) for r in res]
    for k in big:
        js = (0, 1) if k.startswith("ffn") else (None,)
        total_rows = w_in[k].size // w_in[k].shape[-1]
        res = None
        for l in range(depth):
            for j in js:
                p = (k, j)
                w_p = piece_of(w_in, p, l)
                row0 = (l * len(js) + (j or 0)) * (w_p.size // w_p.shape[-1])
                res = adamw(parts[(p, l)], flat(w_p), flat(piece_of(m_in, p, l)), flat(piece_of(v_in, p, l)),
                            name=f"adamw_{k}_l{l}" + ("" if j is None else f"_{j}"), into=(res, total_rows, row0))
        for dct, r in zip((out_g, out_d, out_m, out_v), res):
            dct[k] = r.reshape(w_in[k].shape)
    small_all = ["mod_b", "norm_pre", "norm_post"] + small_names

    def pack_small(dct):
        flat = jnp.concatenate([dct[k].reshape(-1) for k in small_all])
        tot = -(-flat.size // chunk) * chunk
        return jnp.pad(flat, (0, tot - flat.size)).reshape(tot // 1024, 1024)

    res = adamw(pack_small(g_small)[None], pack_small(w_in), pack_small(m_in), pack_small(v_in),
                name="adamw_small")
    pos = 0
    for k in small_all:
        size = w_in[k].size
        for dct, r in zip((out_g, out_d, out_m, out_v), res):
            dct[k] = r.reshape(-1)[pos:pos + size].reshape(w_in[k].shape)
        pos += size

    return (loss, grad_x, *[out_g[k] for k in names], *[out_d[k] for k in names],
            *[out_m[k] for k in names], *[out_v[k] for k in names])
```

```python
import functools
import math

import jax
import jax.numpy as jnp
from jax import lax
from jax.experimental import pallas as pl
from jax.experimental.pallas import tpu as pltpu

F32 = jnp.float32
BF16 = jnp.bfloat16
MESH = pl.DeviceIdType.MESH
N_DEV = 8
LANES = 128
SUBLANES = 8
VMEM_LIMIT = 48 * 1024 * 1024
RMS_EPS = 1e-6
FFN_RES = 0.5
ADAM_LR = 0.001
ADAM_B1 = 0.9
ADAM_B2 = 0.999
ADAM_EPS = 1e-08
ADAM_WD = 0.01
ADAM_STEP = 10
GELU_C = math.sqrt(2.0 / math.pi)
GELU_A = 0.044715


def _pick(dim, target, mult=LANES):
    t = (min(dim, target) // mult) * mult
    while t >= mult:
        if dim % t == 0:
            return t
        t -= mult
    return dim


def _params(sem):
    return pltpu.CompilerParams(dimension_semantics=sem, vmem_limit_bytes=VMEM_LIMIT)


def _sigmoid(x):
    return 1.0 / (1.0 + jnp.exp(-x))


def mm(a, b, *, name, trans_a=False, trans_b=False, out_dtype=F32, tm=1024, tn=1408, tk=2816, comm=None):
    if trans_a:
        kdim, m = a.shape
    else:
        m, kdim = a.shape
    if trans_b:
        n, kb = b.shape
    else:
        kb, n = b.shape
    assert kdim == kb, (a.shape, b.shape)
    tm, tn, tk = _pick(m, tm), _pick(n, tn), _pick(kdim, tk)
    gm, gn, nk = m // tm, n // tn, kdim // tk
    dims = (((0 if trans_a else 1,), (1 if trans_b else 0,)), ((), ()))
    host = _Hosted(comm, 2, 1, 1 if nk > 1 else 0)

    def body(*refs):
        (a_ref, b_ref, o_ref, *acc), crefs = host.split(refs)
        i, j, k = pl.program_id(0), pl.program_id(1), pl.program_id(2)
        host.phase("start", (i == 0) & (j == 0) & (k == 0), crefs)
        prod = lax.dot_general(a_ref[...].astype(BF16), b_ref[...].astype(BF16), dims,
                               preferred_element_type=F32)
        if nk == 1:
            o_ref[...] = prod.astype(out_dtype)
        else:
            acc_ref, = acc

            @pl.when(k == 0)
            def _():
                acc_ref[...] = prod

            @pl.when((k > 0) & (k < nk - 1))
            def _():
                acc_ref[...] += prod

            @pl.when(k == nk - 1)
            def _():
                o_ref[...] = (acc_ref[...] + prod).astype(out_dtype)

        last = (i == gm - 1) & (j == gn - 1) & (k == nk - 1)
        host.phase("mid", last, crefs)
        host.phase("finish", last, crefs)

    a_spec = (pl.BlockSpec((tk, tm), lambda i, j, k: (k, i)) if trans_a
              else pl.BlockSpec((tm, tk), lambda i, j, k: (i, k)))
    b_spec = (pl.BlockSpec((tn, tk), lambda i, j, k: (j, k)) if trans_b
              else pl.BlockSpec((tk, tn), lambda i, j, k: (k, j)))
    in_specs, out_specs, out_shape, scratch, extra = host.specs(
        [a_spec, b_spec], [pl.BlockSpec((tm, tn), lambda i, j, k: (i, j))],
        [jax.ShapeDtypeStruct((m, n), out_dtype)], [pltpu.VMEM((tm, tn), F32)] if nk > 1 else [])
    res = pl.pallas_call(
        body, name=name, grid=(gm, gn, nk),
        in_specs=in_specs, out_specs=out_specs, out_shape=out_shape, scratch_shapes=scratch,
        compiler_params=_params(("arbitrary", "arbitrary", "arbitrary")),
    )(a, b, *extra)
    return res if comm else res[0]


def ffn_in_swiglu(h, w_t, *, name, comm=None, tm=1024, tn=1408):
    m, kdim = h.shape
    f = w_t.shape[0] // 2
    tm, tn = _pick(m, tm), _pick(f, tn)
    gm, gn = m // tm, f // tn
    host = _Hosted(comm, 3, 3, 0)

    def body(*refs):
        (h_ref, wg_ref, wu_ref, g_ref, u_ref, m_ref), crefs = host.split(refs)
        i, j = pl.program_id(0), pl.program_id(1)
        host.phase("start", (i == 0) & (j == 0), crefs)
        hv = h_ref[...]
        g = lax.dot_general(hv, wg_ref[...], (((1,), (1,)), ((), ())), preferred_element_type=F32)
        u = lax.dot_general(hv, wu_ref[...], (((1,), (1,)), ((), ())), preferred_element_type=F32)
        g_ref[...] = g.astype(BF16)
        u_ref[...] = u.astype(BF16)
        m_ref[...] = ((g * _sigmoid(g)) * u).astype(BF16)
        last = (i == gm - 1) & (j == gn - 1)
        host.phase("mid", last, crefs)
        host.phase("finish", last, crefs)

    out_spec = pl.BlockSpec((tm, tn), lambda i, j: (i, j))
    in_specs, out_specs, out_shape, scratch, extra = host.specs(
        [pl.BlockSpec((tm, kdim), lambda i, j: (i, 0)), pl.BlockSpec((tn, kdim), lambda i, j: (j, 0)),
         pl.BlockSpec((tn, kdim), lambda i, j: (j + gn, 0))],
        [out_spec] * 3, [jax.ShapeDtypeStruct((m, f), BF16)] * 3, [])
    return pl.pallas_call(
        body, name=name, grid=(gm, gn),
        in_specs=in_specs, out_specs=out_specs, out_shape=out_shape, scratch_shapes=scratch,
        compiler_params=_params(("arbitrary", "arbitrary")),
    )(h, w_t, w_t, *extra)


def mm_blockdiag(a, b, *, name, a_cb0=0, out_dtype=F32, tm=512):
    m = a.shape[0]
    g_n, ka, nb = b.shape
    tm = _pick(m, tm)
    assert a_cb0 % g_n == 0

    def body(a_ref, b_ref, o_ref):
        for g in range(g_n):
            o_ref[:, g * nb:(g + 1) * nb] = jnp.dot(
                a_ref[:, g * ka:(g + 1) * ka].astype(BF16), b_ref[g].astype(BF16),
                preferred_element_type=F32).astype(out_dtype)

    return pl.pallas_call(
        body, name=name, grid=(m // tm,),
        in_specs=[pl.BlockSpec((tm, g_n * ka), lambda i: (i, a_cb0 // g_n)),
                  pl.BlockSpec((g_n, ka, nb), lambda i: (0, 0, 0))],
        out_specs=pl.BlockSpec((tm, g_n * nb), lambda i: (i, 0)),
        out_shape=jax.ShapeDtypeStruct((m, g_n * nb), out_dtype),
        compiler_params=_params(("parallel",)),
    )(a, b)


def mm_blockdiag_tn(a, b, *, name, g_n, ka, kb, a_cb0=0, b_cb0=0, tk=512):
    rows = a.shape[0]
    tk = _pick(rows, tk)
    nk = rows // tk
    assert a_cb0 % g_n == 0 and b_cb0 % g_n == 0

    def body(a_ref, b_ref, o_ref):
        @pl.when(pl.program_id(0) == 0)
        def _():
            o_ref[...] = jnp.zeros_like(o_ref)

        for g in range(g_n):
            o_ref[g] += lax.dot_general(a_ref[:, g * ka:(g + 1) * ka].astype(BF16),
                                        b_ref[:, g * kb:(g + 1) * kb].astype(BF16),
                                        (((0,), (0,)), ((), ())), preferred_element_type=F32)

    return pl.pallas_call(
        body, name=name, grid=(nk,),
        in_specs=[pl.BlockSpec((tk, g_n * ka), lambda k: (k, a_cb0 // g_n)),
                  pl.BlockSpec((tk, g_n * kb), lambda k: (k, b_cb0 // g_n))],
        out_specs=pl.BlockSpec((g_n, ka, kb), lambda k: (0, 0, 0)),
        out_shape=jax.ShapeDtypeStruct((g_n, ka, kb), F32),
        compiler_params=_params(("arbitrary",)),
    )(a, b)


def rowwise(fn, rows, vecs, outs, reds=(), *, name, tm=512):
    metas = []
    for r in rows:
        if isinstance(r, tuple) and len(r) == 3:
            metas.append(("col", r[0], r[1], r[2]))
        elif isinstance(r, tuple):
            metas.append(("lead", r[0], r[0].shape[2], r[1]))
        else:
            metas.append(("full", r, r.shape[1], 0))
    n_rows = metas[0][1].shape[1] if metas[0][0] == "lead" else metas[0][1].shape[0]
    rc = 16 if n_rows % 16 == 0 else (SUBLANES if n_rows % SUBLANES == 0 else n_rows)
    tm = _pick(n_rows, tm, rc)
    n_inner = tm // rc
    windows = [(o[2] if len(o) > 2 else None) for o in outs]
    bases = [(k, o[3]) for k, o in enumerate(outs) if len(o) > 3 and o[3] is not None]
    outs = [(o[0], o[1]) for o in outs]
    nr, nv, no, nbase = len(metas), len(vecs), len(outs), len(bases)

    def body(*refs):
        row_refs, vec_refs = refs[:nr], refs[nr:nr + nv]
        refs = refs[nr + nv + nbase:]
        out_refs, red_refs = refs[:no], refs[no:]
        if reds:
            @pl.when(pl.program_id(0) == 0)
            def _():
                for rr in red_refs:
                    rr[...] = jnp.zeros_like(rr)
        vec_vals = [v[...] for v in vec_refs]

        def step(s, carry):
            r0 = pl.multiple_of(s * rc, rc)
            vals = [ref[pl.ds(r0, rc), :] for ref in row_refs]
            res = fn(*vals, *vec_vals)
            if not isinstance(res, (tuple, list)):
                res = (res,)
            for o_ref, (widths, dt), val in zip(out_refs, outs, res[:no]):
                pieces = val if isinstance(val, (tuple, list)) else (val,)
                off = 0
                for w_, piece in zip(widths, pieces):
                    o_ref[pl.ds(r0, rc), off:off + w_] = piece.astype(dt)
                    off += w_
            for rr, val in zip(red_refs, res[no:]):
                rr[...] += val
            return carry

        lax.fori_loop(0, n_inner, step, 0, unroll=min(n_inner, 4))

    in_specs = []
    for kind, arr, w_, idx in metas:
        if kind == "col":
            in_specs.append(pl.BlockSpec((tm, w_), functools.partial(lambda i, cb: (i, cb), cb=idx)))
        elif kind == "lead":
            in_specs.append(pl.BlockSpec((None, tm, w_), functools.partial(lambda i, p: (p, i, 0), p=idx)))
        else:
            in_specs.append(pl.BlockSpec((tm, w_), lambda i: (i, 0)))
    for v in vecs:
        in_specs.append(pl.BlockSpec(v.shape, lambda i: (0, 0)))
    in_specs += [pl.BlockSpec(memory_space=pl.ANY)] * nbase
    out_specs, out_shape = [], []
    for (ws, dt), win in zip(outs, windows):
        total, cb, total_rows, row0 = (tuple(win) + (n_rows, 0))[:4] if win is not None else (sum(ws), 0, n_rows, 0)
        assert row0 % tm == 0
        out_specs.append(pl.BlockSpec((tm, sum(ws)),
                                      functools.partial(lambda i, cb, rb: (i + rb, cb), cb=cb, rb=row0 // tm)))
        out_shape.append(jax.ShapeDtypeStruct((total_rows, total), dt))
    out_specs += [pl.BlockSpec((1, w_), lambda i: (0, 0)) for w_ in reds]
    out_shape += [jax.ShapeDtypeStruct((1, w_), F32) for w_ in reds]
    res = pl.pallas_call(
        body, name=name, grid=(n_rows // tm,),
        in_specs=in_specs, out_specs=out_specs, out_shape=out_shape,
        input_output_aliases={nr + nv + b: k for b, (k, _) in enumerate(bases)},
        compiler_params=_params(("arbitrary",)),
    )(*[m[1] for m in metas], *vecs, *[b for _, b in bases])
    return res


def _rms(x):
    return lax.rsqrt(jnp.mean(x * x, axis=-1, keepdims=True) + RMS_EPS)


def _colsum(x):
    return jnp.sum(x, axis=0, keepdims=True)


def _f_silu(c):
    return c * _sigmoid(c)


def _f_pre(x, a, sh):
    return (x * _rms(x)) * a + sh


def _f_post_add(x, y, bv):
    return x + (y * _rms(y)) * bv


def _f_post_pre(x, y, bv, a, sh):
    x = _f_post_add(x, y, bv)
    return x, _f_pre(x, a, sh)


def _f_pre_post_bwd(dxo, dh, x, y, a, bv):
    dx, d_a, d_sh = _f_pre_bwd(dxo, dh, x, a)
    dy, d_bv = _f_post_bwd(dx, y, bv)
    return dx, dy, d_a, d_sh, d_bv


def _f_post_bwd(dxo, y, bv):
    ry = _rms(y)
    yn = y * ry
    dyn = dxo * bv
    dy = ry * (dyn - yn * jnp.mean(dyn * yn, axis=-1, keepdims=True))
    return dy, _colsum(dxo * yn)


def _f_pre_bwd(dxo, dh, x, a):
    r = _rms(x)
    xn = x * r
    dxn = dh * a
    dx = dxo + r * (dxn - xn * jnp.mean(dxn * xn, axis=-1, keepdims=True))
    return dx, _colsum(dh * xn), _colsum(dh)


def _f_swiglu_bwd(g, u, dm):
    g, u, dm = g.astype(F32), u.astype(F32), dm.astype(F32)
    sg = _sigmoid(g)
    dg = dm * u * (sg * (1.0 + g * (1.0 - sg)))
    du = dm * (g * sg)
    return ((dg, du),)


def _gelu_t(x):
    return jnp.tanh(GELU_C * (x + GELU_A * x * x * x))


def _f_gelu_in(y0, u, dvec):
    y = y0 + dvec * u
    return 0.5 * y * (1.0 + _gelu_t(y)), y


def _f_gelu_bwd(dge, y, u):
    t = _gelu_t(y)
    dy = dge * (0.5 * (1.0 + t) + 0.5 * y * (1.0 - t * t) * GELU_C * (1.0 + 3.0 * GELU_A * y * y))
    return dy, _colsum(dy * u)


def _f_du_fin(du0, dys, dvec):
    return du0 + dvec * dys.astype(F32)


def _f_merge(zv, zg, yb, ga, gb):
    zv, zg, yb, ga, gb = [a.astype(F32) for a in (zv, zg, yb, ga, gb)]
    return _sigmoid(ga) * (zv * _sigmoid(zg)) + _sigmoid(gb) * yb


def _f_merge_bwd(dmg, zv, zg, yb, ga, gb):
    zv, zg, yb, ga, gb = [a.astype(F32) for a in (zv, zg, yb, ga, gb)]
    sa, sb, sz = _sigmoid(ga), _sigmoid(gb), _sigmoid(zg)
    ya = zv * sz
    dya = dmg * sa
    dga = dmg * ya * sa * (1.0 - sa)
    dyb = dmg * sb
    dgb = dmg * yb * sb * (1.0 - sb)
    dzv = dya * sz
    dzg = dya * zv * sz * (1.0 - sz)
    return (dzv, dzg), dyb, (dga, dgb)


def _f_sum_parts(*parts):
    acc = parts[0].astype(F32)
    for p in parts[1:]:
        acc = acc + p.astype(F32)
    return acc


def _f_adamw(*args):
    parts, (w, m, v) = args[:-3], args[-3:]
    g = _f_sum_parts(*parts)
    m = ADAM_B1 * m + (1.0 - ADAM_B1) * g
    v = ADAM_B2 * v + (1.0 - ADAM_B2) * (g * g)
    m_hat = m / (1.0 - ADAM_B1 ** ADAM_STEP)
    v_hat = v / (1.0 - ADAM_B2 ** ADAM_STEP)
    delta = -ADAM_LR * (m_hat / (jnp.sqrt(v_hat) + ADAM_EPS) + ADAM_WD * w)
    return g, delta, m, v


def adamw(parts3, w, m, v, *, name, into=None):
    c = w.shape[1]
    rows = [(parts3, p) for p in range(parts3.shape[0])] + [w, m, v]
    if into is None:
        return rowwise(_f_adamw, rows, [], [((c,), F32)] * 4, name=name)
    bases, total_rows, row0 = into
    outs = [((c,), F32, (c, 0, total_rows, row0), bases[k] if bases else None) for k in range(4)]
    return rowwise(_f_adamw, rows, [], outs, name=name)


def _tri_dot(tri, x, dims=(((1,), (0,)), ((), ()))):
    x1 = x.astype(BF16)
    r1 = x - x1.astype(F32)
    x2 = r1.astype(BF16)
    x3 = (r1 - x2.astype(F32)).astype(BF16)
    dot = functools.partial(lax.dot_general, dimension_numbers=dims, preferred_element_type=F32)
    return dot(tri, x1) + dot(tri, x2) + dot(tri, x3)


def cum_fwd(proj, fb, *, col_block, name, t=256):
    n = proj.shape[0]
    t = _pick(n, t, SUBLANES)

    def body(f_ref, fb_ref, cum_ref, car_ref):
        @pl.when(pl.program_id(0) == 0)
        def _():
            car_ref[...] = jnp.zeros_like(car_ref)

        x = f_ref[...] + fb_ref[...]
        lf = jnp.minimum(x, 0.0) - jnp.log(1.0 + jnp.exp(-jnp.abs(x)))
        r = lax.broadcasted_iota(jnp.int32, (t, t), 0)
        c = lax.broadcasted_iota(jnp.int32, (t, t), 1)
        cs = _tri_dot((c <= r).astype(BF16), lf) + car_ref[0:1, :]
        cum_ref[...] = cs
        car_ref[0:1, :] = cs[t - 1:t, :]

    return pl.pallas_call(
        body, name=name, grid=(n // t,),
        in_specs=[pl.BlockSpec((t, LANES), lambda i: (i, col_block)),
                  pl.BlockSpec((1, LANES), lambda i: (0, 0))],
        out_specs=pl.BlockSpec((t, LANES), lambda i: (i, 0)),
        out_shape=jax.ShapeDtypeStruct((n, LANES), F32),
        scratch_shapes=[pltpu.VMEM((SUBLANES, LANES), F32)],
        compiler_params=_params(("arbitrary",)),
    )(proj, fb)


def cum_bwd(dcum, proj, fb, *, col_block, name, t=256):
    n = proj.shape[0]
    slabs = dcum.shape[0]
    t = _pick(n, t, SUBLANES)
    nb = n // t

    def body(dc_ref, f_ref, fb_ref, df_ref, dfb_ref, car_ref):
        @pl.when(pl.program_id(0) == 0)
        def _():
            car_ref[...] = jnp.zeros_like(car_ref)
            dfb_ref[...] = jnp.zeros_like(dfb_ref)

        r = lax.broadcasted_iota(jnp.int32, (t, t), 0)
        c = lax.broadcasted_iota(jnp.int32, (t, t), 1)
        dl = _tri_dot((c >= r).astype(BF16), jnp.sum(dc_ref[...], axis=0)) + car_ref[0:1, :]
        car_ref[0:1, :] = dl[0:1, :]
        x = f_ref[...] + fb_ref[...]
        df = dl * (1.0 / (1.0 + jnp.exp(x)))
        df_ref[...] = df
        dfb_ref[...] += _colsum(df)

    return pl.pallas_call(
        body, name=name, grid=(nb,),
        in_specs=[pl.BlockSpec((slabs, t, LANES), lambda i: (0, nb - 1 - i, 0)),
                  pl.BlockSpec((t, LANES), lambda i: (nb - 1 - i, col_block)),
                  pl.BlockSpec((1, LANES), lambda i: (0, 0))],
        out_specs=[pl.BlockSpec((t, LANES), lambda i: (nb - 1 - i, 0)),
                   pl.BlockSpec((1, LANES), lambda i: (0, 0))],
        out_shape=[jax.ShapeDtypeStruct((n, LANES), F32), jax.ShapeDtypeStruct((1, LANES), F32)],
        scratch_shapes=[pltpu.VMEM((SUBLANES, LANES), F32)],
        compiler_params=_params(("arbitrary",)),
    )(dcum, proj, fb)


_NT =(((1,), (1,)), ((), ()))
_TN = (((0,), (0,)), ((), ()))


def _lane_sums_as_row(x):
    return _tri_dot(jnp.ones((SUBLANES, x.shape[1]), BF16), x, _NT)[0:1, :]


def _causal_keep(t):
    return lax.broadcasted_iota(jnp.int32, (t, t), 1) <= lax.broadcasted_iota(jnp.int32, (t, t), 0)


def attn_fwd(q, k, v, crow, *, name, t=512, hb=8, comm=None):
    h_n, n, dh = q.shape
    hb = min(hb, h_n)
    t = _pick(n, t)
    nb = n // t
    ng = h_n // hb
    host = _Hosted(comm, 4, 2, 3)

    def body(*refs):
        (q_ref, k_ref, v_ref, cr_ref, o_ref, lse_ref, m_sc, l_sc, acc_sc), crefs = host.split(refs)
        g, i, j = pl.program_id(0), pl.program_id(1), pl.program_id(2)
        host.phase("start", (g == 0) & (i == 0) & (j == 0), crefs)
        host.phase("mid", (g == ng - 1) & (i == (3 * nb) // 4) & (j == 0), crefs)

        @pl.when(j == 0)
        def _():
            m_sc[...] = jnp.full_like(m_sc, -jnp.inf)
            l_sc[...] = jnp.zeros_like(l_sc)
            acc_sc[...] = jnp.zeros_like(acc_sc)

        def update(diagonal):
            keep = _causal_keep(t) if diagonal else None
            heads = range(hb)
            ss = [lax.dot_general(q_ref[h], k_ref[h], _NT, preferred_element_type=F32) for h in heads]
            pairs, alphas = [], []
            for h in heads:
                s = ss[h] - cr_ref[h]
                if diagonal:
                    s = jnp.where(keep, s, -jnp.inf)
                m_prev = m_sc[h]
                m_new = jnp.maximum(m_prev, jnp.max(s, axis=-1, keepdims=True))
                p = jnp.exp(s - m_new)
                alpha = jnp.exp(m_prev - m_new)
                l_sc[h] = alpha * l_sc[h] + jnp.sum(p, axis=-1, keepdims=True)
                m_sc[h] = m_new
                p_hi = p.astype(BF16)
                pairs.append((p_hi, (p - p_hi.astype(F32)).astype(BF16)))
                alphas.append(alpha)
            for h in heads:
                vv = v_ref[h]
                acc_sc[h] = (alphas[h] * acc_sc[h] + jnp.dot(pairs[h][0], vv, preferred_element_type=F32)
                             + jnp.dot(pairs[h][1], vv, preferred_element_type=F32))

        @pl.when(j < i)
        def _():
            update(False)

        @pl.when(j == i)
        def _():
            update(True)

        @pl.when(j == nb - 1)
        def _():
            o_ref[...] = acc_sc[...] / l_sc[...]
            lane0 = lax.broadcasted_iota(jnp.int32, (t, LANES), 1) == 0
            for h in range(hb):
                lse_col = m_sc[h] + jnp.log(l_sc[h])
                lse_ref[h] = _lane_sums_as_row(jnp.where(lane0, lse_col, 0.0))

        host.phase("finish", (g == ng - 1) & (i == nb - 1) & (j == nb - 1), crefs)

    qspec = pl.BlockSpec((hb, t, dh), lambda g, i, j: (g, i, 0))
    kspec = pl.BlockSpec((hb, t, dh), lambda g, i, j: (g, jnp.minimum(j, i), 0))
    in_specs, out_specs, out_shape, scratch, extra = host.specs(
        [qspec, kspec, kspec, pl.BlockSpec((hb, 1, t), lambda g, i, j: (g, 0, jnp.minimum(j, i)))],
        [qspec, pl.BlockSpec((hb, 1, t), lambda g, i, j: (g, 0, i))],
        [jax.ShapeDtypeStruct((h_n, n, dh), F32), jax.ShapeDtypeStruct((h_n, 1, n), F32)],
        [pltpu.VMEM((hb, t, 1), F32), pltpu.VMEM((hb, t, 1), F32), pltpu.VMEM((hb, t, dh), F32)])
    return pl.pallas_call(
        body, name=name, grid=(ng, nb, nb),
        in_specs=in_specs, out_specs=out_specs, out_shape=out_shape, scratch_shapes=scratch,
        compiler_params=_params(("arbitrary", "arbitrary", "arbitrary")),
    )(q, k, v, crow, *extra)


def attn_delta(do, o, *, name, t=512, hb=8):
    h_n, n, dh = do.shape
    hb = min(hb, h_n)
    t = _pick(n, t)

    def body(do_ref, o_ref, dl_ref):
        for h in range(hb):
            dl_ref[h] = _lane_sums_as_row(do_ref[h].astype(F32) * o_ref[h])

    spec = pl.BlockSpec((hb, t, dh), lambda g, i: (g, i, 0))
    return pl.pallas_call(
        body, name=name, grid=(h_n // hb, n // t),
        in_specs=[spec, spec], out_specs=pl.BlockSpec((hb, 1, t), lambda g, i: (g, 0, i)),
        out_shape=jax.ShapeDtypeStruct((h_n, 1, n), F32),
        compiler_params=_params(("parallel", "parallel")),
    )(do, o)


def attn_bwd(q, k, v, do, o, lse, cum, *, scale, tag, t=512, hb=2, comm=None):
    h_n, n, dh = q.shape
    hb = min(hb, h_n)
    t = _pick(n, t)
    nb = n // t
    dob = do.astype(BF16)
    delta = attn_delta(dob, o, name=f"attn_delta_{tag}", t=t)

    ng = h_n // hb
    host = _Hosted(comm, 7, 4, 3)

    def body(*refs):
        (q_ref, k_ref, v_ref, do_ref, lse_ref, dl_ref, cum_ref,
         dq_ref, dk_ref, dv_ref, dcum_ref, dk_acc, dv_acc, dcc_acc), crefs = host.split(refs)
        g, j, i = pl.program_id(0), pl.program_id(1), pl.program_id(2)
        host.phase("start", (g == 0) & (j == 0) & (i == 0), crefs)
        lane = lax.broadcasted_iota(jnp.int32, (t, LANES), 1)

        @pl.when((j == 0) & (i == 0))
        def _():
            dq_ref[...] = jnp.zeros_like(dq_ref)

        @pl.when(i == 0)
        def _():
            dk_acc[...] = jnp.zeros_like(dk_acc)
            dv_acc[...] = jnp.zeros_like(dv_acc)
            dcc_acc[...] = jnp.zeros_like(dcc_acc)

        def update(diagonal):
            heads = range(hb)
            r0 = pl.multiple_of(i * t, t)
            if diagonal:
                keep = lax.broadcasted_iota(jnp.int32, (t, t), 0) <= lax.broadcasted_iota(jnp.int32, (t, t), 1)
            qv, kv = [q_ref[h] for h in heads], [k_ref[h] for h in heads]
            vv, dov = [v_ref[h] for h in heads], [do_ref[h] for h in heads]
            st = [lax.dot_general(kv[h], qv[h], _NT, preferred_element_type=F32) for h in heads]
            dpt = [lax.dot_general(vv[h], dov[h], _NT, preferred_element_type=F32) for h in heads]
            pt = []
            cum_tile = cum_ref[...]
            for h in heads:
                cc = jnp.sum(jnp.where(lane == g * hb + h, cum_tile, 0.0), axis=1, keepdims=True)
                s = st[h] - cc
                if diagonal:
                    s = jnp.where(keep, s, -jnp.inf)
                pt.append(jnp.exp(s - lse_ref[h]))
            for h in heads:
                dv_acc[h] += jnp.dot(pt[h].astype(BF16), dov[h], preferred_element_type=F32)
            dsb = []
            for h in heads:
                ds = pt[h] * (dpt[h] - dl_ref[h])
                dcc_acc[h] -= jnp.sum(ds, axis=1, keepdims=True)
                dsb.append(ds.astype(BF16))
            for h in heads:
                dk_acc[h] += jnp.dot(dsb[h], qv[h], preferred_element_type=F32)
            for h in heads:
                dq_ref[h, pl.ds(r0, t), :] += lax.dot_general(dsb[h], kv[h], _TN,
                                                              preferred_element_type=F32) * scale

        @pl.when(i > j)
        def _():
            update(False)

        @pl.when(i == j)
        def _():
            update(True)

        @pl.when(i == nb - 1)
        def _():
            dk_ref[...] = dk_acc[...]
            dv_ref[...] = dv_acc[...]
            tile = jnp.zeros((t, LANES), F32)
            for h in range(hb):
                tile = tile + jnp.where(lane == g * hb + h, dcc_acc[h], 0.0)
            dcum_ref[...] = tile

        host.phase("finish", (g == ng - 1) & (j == nb - 1) & (i == nb - 1), crefs)

    qspec = pl.BlockSpec((hb, t, dh), lambda g, j, i: (g, jnp.maximum(i, j), 0))
    qrow = pl.BlockSpec((hb, 1, t), lambda g, j, i: (g, 0, jnp.maximum(i, j)))
    kspec = pl.BlockSpec((hb, t, dh), lambda g, j, i: (g, j, 0))
    in_specs, out_specs, out_shape, scratch, extra = host.specs(
        [qspec, kspec, kspec, qspec, qrow, qrow, pl.BlockSpec((t, LANES), lambda g, j, i: (j, 0))],
        [pl.BlockSpec((hb, n, dh), lambda g, j, i: (g, 0, 0)), kspec, kspec,
         pl.BlockSpec((None, t, LANES), lambda g, j, i: (g, j, 0))],
        [jax.ShapeDtypeStruct((h_n, n, dh), F32)] * 3 + [jax.ShapeDtypeStruct((ng, n, LANES), F32)],
        [pltpu.VMEM((hb, t, dh), F32), pltpu.VMEM((hb, t, dh), F32), pltpu.VMEM((hb, t, 1), F32)])
    dq, dk, dv, dcum, *arrived = pl.pallas_call(
        body, name=f"attn_bwd_{tag}", grid=(ng, nb, nb),
        in_specs=in_specs, out_specs=out_specs, out_shape=out_shape, scratch_shapes=scratch,
        compiler_params=_params(("arbitrary", "arbitrary", "arbitrary")),
    )(q, k, v, dob, lse, delta, cum, *extra)
    return [dq, dk, dv, dcum] + arrived


SCAN_STEPS = (1, 2, 4)


def ssm_scan(x, tab, *, reverse, name, s_prev=None, tt=512, out_dtype=BF16):
    n, width = x.shape
    nc, _, hw = tab.shape
    cw = 2 * hw
    assert width == nc * cw
    tt = _pick(n, tt, 2 * SUBLANES)
    nt = n // tt
    ng = tt // (2 * SUBLANES)
    with_grad = s_prev is not None

    def body(*refs):
        if with_grad:
            x_ref, s_ref, tab_ref, o_ref, g_ref, car_ref = refs
        else:
            x_ref, tab_ref, o_ref, car_ref = refs

        @pl.when(pl.program_id(1) == 0)
        def _():
            car_ref[...] = jnp.zeros_like(car_ref)
            if with_grad:
                g_ref[...] = jnp.zeros_like(g_ref)

        q_re, q_im = tab_ref[0:8, :], tab_ref[8:16, :]
        p_re = [tab_ref[16 + 16 * i:24 + 16 * i, :] for i in range(3)]
        p_im = [tab_ref[24 + 16 * i:32 + 16 * i, :] for i in range(3)]
        row = lax.broadcasted_iota(jnp.int32, (SUBLANES, hw), 0)

        def group(xr, xi, sr_, si_, carry):
            c_re, c_im = carry
            for i, d in enumerate(SCAN_STEPS):
                shift = SUBLANES - d if reverse else d
                sr, si = pltpu.roll(xr, shift, 0), pltpu.roll(xi, shift, 0)
                xr, xi = (xr + p_re[i] * sr - p_im[i] * si,
                          xi + p_re[i] * si + p_im[i] * sr)
            xr, xi = (xr + q_re * c_re - q_im * c_im,
                      xi + q_re * c_im + q_im * c_re)
            if with_grad:
                nr = jnp.where(row < SUBLANES - 1, pltpu.roll(xr, SUBLANES - 1, 0), c_re)
                ni = jnp.where(row < SUBLANES - 1, pltpu.roll(xi, SUBLANES - 1, 0), c_im)
                g_ref[:, 0:hw] += nr * sr_ + ni * si_
                g_ref[:, hw:cw] += ni * sr_ - nr * si_
            if reverse:
                return xr, xi, (xr[0:1, :], xi[0:1, :])
            return xr, xi, (xr[SUBLANES - 1:SUBLANES, :], xi[SUBLANES - 1:SUBLANES, :])

        def pair(gi, carry):
            g = (ng - 1 - gi) if reverse else gi
            r0 = pl.multiple_of(g * 2 * SUBLANES, 2 * SUBLANES)
            rows = pl.ds(r0, 2 * SUBLANES)
            xr, xi = x_ref[rows, 0:hw].astype(F32), x_ref[rows, hw:cw].astype(F32)
            if with_grad:
                sr, si = s_ref[rows, 0:hw].astype(F32), s_ref[rows, hw:cw].astype(F32)
            halves = [slice(0, SUBLANES), slice(SUBLANES, 2 * SUBLANES)]
            done = [None, None]
            for k in ((1, 0) if reverse else (0, 1)):
                h = halves[k]
                o_re, o_im, carry = group(xr[h], xi[h], sr[h] if with_grad else None,
                                          si[h] if with_grad else None, carry)
                done[k] = (o_re, o_im)
            o_ref[rows, 0:hw] = jnp.concatenate([done[0][0], done[1][0]], axis=0).astype(o_ref.dtype)
            o_ref[rows, hw:cw] = jnp.concatenate([done[0][1], done[1][1]], axis=0).astype(o_ref.dtype)
            return carry

        c_re, c_im = lax.fori_loop(0, ng, pair, (car_ref[0:1, 0:hw], car_ref[0:1, hw:cw]),
                                   unroll=min(ng, 2))
        car_ref[0:1, 0:hw] = c_re
        car_ref[0:1, hw:cw] = c_im

    if reverse:
        xspec = pl.BlockSpec((tt, cw), lambda c, t: (nt - 1 - t, c))
    else:
        xspec = pl.BlockSpec((tt, cw), lambda c, t: (t, c))
    tspec = pl.BlockSpec((None, 64, hw), lambda c, t: (c, 0, 0))
    in_specs = [xspec, xspec, tspec] if with_grad else [xspec, tspec]
    out_specs = [xspec]
    out_shape = [jax.ShapeDtypeStruct((n, width), out_dtype)]
    if with_grad:
        out_specs.append(pl.BlockSpec((None, SUBLANES, cw), lambda c, t: (c, 0, 0)))
        out_shape.append(jax.ShapeDtypeStruct((nc, SUBLANES, cw), F32))
    operands = (x, s_prev, tab) if with_grad else (x, tab)
    return pl.pallas_call(
        body, name=name, grid=(nc, nt),
        in_specs=in_specs, out_specs=out_specs, out_shape=out_shape,
        scratch_shapes=[pltpu.VMEM((SUBLANES, cw), F32)],
        compiler_params=_params(("parallel", "arbitrary")),
    )(*operands)


def _slot(pos):
    return 4 * pos[0] + 2 * pos[1] + pos[2]


def _comm_scratch(n):
    return [pltpu.SemaphoreType.DMA((7 * n,)), pltpu.SemaphoreType.DMA((7 * n,)), pltpu.SemaphoreType.DMA((n,))]


def _gather_copies(ins, outs, sems):
    send_sems, recv_sems, local_sems = sems
    n = len(ins)
    x, y, c = lax.axis_index("x"), lax.axis_index("y"), lax.axis_index("c")
    me, sibling = (x, y, c), (x, y, 1 - c)
    chips = [(1 - x, y), (x, 1 - y), (1 - x, 1 - y)]

    def copy(t, k, block, to, src=None):
        dst = outs[t].at[_slot(block)]
        return pltpu.make_async_remote_copy(
            src_ref=dst if src is None else src, dst_ref=dst,
            send_sem=send_sems.at[7 * t + k], recv_sem=recv_sems.at[7 * t + k],
            device_id=to, device_id_type=MESH)

    jc = list(enumerate(chips))
    return dict(
        mine=[pltpu.make_async_copy(ins[t], outs[t].at[_slot(me)], local_sems.at[t]) for t in range(n)],
        first=[cp for t in range(n) for cp in
               [copy(t, 0, me, sibling, src=ins[t])] + [copy(t, 1 + j, me, (*chip, c), src=ins[t]) for j, chip in jc]],
        arrive=[copy(t, 1 + j, (*chip, c), me) for t in range(n) for j, chip in jc],
        passed=[copy(t, 4 + j, (*chip, c), sibling) for t in range(n) for j, chip in jc],
        from_sibling=[cp for t in range(n) for cp in
                      [copy(t, 0, sibling, me)] + [copy(t, 4 + j, (*chip, 1 - c), me) for j, chip in jc]])


def _gather_start(ins, outs, sems):
    cps = _gather_copies(ins, outs, sems)
    for cp in cps["mine"] + cps["first"]:
        cp.start()


def _gather_forward(ins, outs, sems):
    cps = _gather_copies(ins, outs, sems)
    for arrived, onward in zip(cps["arrive"], cps["passed"]):
        arrived.wait_recv()
        onward.start()


def _gather_finish(ins, outs, sems):
    cps = _gather_copies(ins, outs, sems)
    for cp in cps["from_sibling"]:
        cp.wait_recv()
    for cp in cps["first"] + cps["passed"]:
        cp.wait_send()
    for cp in cps["mine"]:
        cp.wait()


def gather_comm(arrs):
    return dict(ins=list(arrs), out_shape=[jax.ShapeDtypeStruct((N_DEV,) + a.shape, a.dtype) for a in arrs],
                scratch=_comm_scratch(len(arrs)), start=_gather_start, mid=_gather_forward, finish=_gather_finish)


def _exchange_copies(ins, outs, sems):
    send_sems, recv_sems, local_sems = sems
    n = len(ins)
    me = (lax.axis_index("x"), lax.axis_index("y"), lax.axis_index("c"))
    peers = []
    for k in range(1, N_DEV):
        flip = ((k >> 2) & 1, (k >> 1) & 1, k & 1)
        peers.append(tuple(1 - p if f else p for p, f in zip(me, flip)))

    def copy(t, k, peer, dst_slot):
        return pltpu.make_async_remote_copy(
            src_ref=ins[t].at[_slot(peer)], dst_ref=outs[t].at[dst_slot],
            send_sem=send_sems.at[7 * t + k], recv_sem=recv_sems.at[7 * t + k],
            device_id=peer, device_id_type=MESH)

    return dict(
        mine=[pltpu.make_async_copy(ins[t].at[_slot(me)], outs[t].at[_slot(me)], local_sems.at[t])
              for t in range(n)],
        send=[copy(t, k, peer, _slot(me)) for t in range(n) for k, peer in enumerate(peers)],
        both=[copy(t, k, peer, _slot(peer)) for t in range(n) for k, peer in enumerate(peers)])


def _exchange_start(ins, outs, sems):
    cps = _exchange_copies(ins, outs, sems)
    for cp in cps["mine"] + cps["send"]:
        cp.start()


def _exchange_finish(ins, outs, sems):
    cps = _exchange_copies(ins, outs, sems)
    for cp in cps["both"]:
        cp.wait()
    for cp in cps["mine"]:
        cp.wait()


def exchange_comm(arrs):
    return dict(ins=list(arrs), out_shape=[jax.ShapeDtypeStruct(a.shape, a.dtype) for a in arrs],
                scratch=_comm_scratch(len(arrs)), start=_exchange_start, mid=None, finish=_exchange_finish)


def run_comm(comm, *, name):
    n_in, n_out = len(comm["ins"]), len(comm["out_shape"])

    def body(*refs):
        ins, outs, sems = refs[:n_in], refs[n_in:n_in + n_out], refs[n_in + n_out:]
        comm["start"](ins, outs, sems)
        if comm["mid"] is not None:
            comm["mid"](ins, outs, sems)
        comm["finish"](ins, outs, sems)

    any_spec = pl.BlockSpec(memory_space=pl.ANY)
    return pl.pallas_call(
        body, name=name, in_specs=[any_spec] * n_in, out_specs=[any_spec] * n_out,
        out_shape=comm["out_shape"], scratch_shapes=comm["scratch"],
    )(*comm["ins"])


class _Hosted:
    def __init__(self, comm, n_in, n_out, n_scratch):
        self.comm = comm
        self.n_ci = len(comm["ins"]) if comm else 0
        self.n_co = len(comm["out_shape"]) if comm else 0
        self.n_in, self.n_out, self.n_scratch = n_in, n_out, n_scratch

    def split(self, refs):
        a = self.n_in
        b = a + self.n_ci
        c = b + self.n_out
        e = c + self.n_co
        f = e + self.n_scratch
        return refs[:a] + refs[b:c] + refs[e:f], (refs[a:b], refs[c:e], refs[f:])

    def phase(self, which, when, crefs):
        fn = self.comm[which] if self.comm else None
        if fn is not None:
            pl.when(when)(lambda: fn(*crefs))

    def specs(self, in_specs, out_specs, out_shape, scratch):
        any_spec = pl.BlockSpec(memory_space=pl.ANY)
        if not self.comm:
            return in_specs, out_specs, out_shape, scratch, ()
        return (in_specs + [any_spec] * self.n_ci, out_specs + [any_spec] * self.n_co,
                out_shape + self.comm["out_shape"], scratch + self.comm["scratch"], tuple(self.comm["ins"]))


def all_gather(arrs, *, name):
    return run_comm(gather_comm(arrs), name=name)


def _discretise(a_re, a_im, log_dt, b_re, b_im):
    ar = jnp.minimum(a_re, -1e-4)
    dt = jnp.exp(log_dt)[:, None]
    e, ph = ar * dt, a_im * dt
    mag = jnp.exp(e)
    lr, li = mag * jnp.cos(ph), mag * jnp.sin(ph)
    den = ar * ar + a_im * a_im
    nr, ni = lr - 1.0, li
    cr = (nr * ar + ni * a_im) / den
    ci = (ni * ar - nr * a_im) / den
    bb_re = cr[..., None] * b_re - ci[..., None] * b_im
    bb_im = cr[..., None] * b_im + ci[..., None] * b_re
    return e, ph, bb_re, bb_im


def _lam_pow(e, ph, k, conj):
    mag = jnp.exp(k * e)
    return mag * jnp.cos(k * ph), (-1.0 if conj else 1.0) * mag * jnp.sin(k * ph)


def _scan_table(e, ph, nc, reverse):
    hw = e.size // nc
    e, ph = e.reshape(nc, 1, hw), ph.reshape(nc, 1, hw)
    j = jnp.arange(SUBLANES, dtype=F32).reshape(1, SUBLANES, 1)
    kq = (SUBLANES - j) if reverse else (j + 1.0)
    parts = list(_lam_pow(e, ph, kq, reverse))
    for step in SCAN_STEPS:
        inside = (j < SUBLANES - step) if reverse else (j >= step)
        p_re, p_im = _lam_pow(e, ph, jnp.full_like(j, float(step)), reverse)
        parts += [jnp.where(inside, p_re, 0.0), jnp.where(inside, p_im, 0.0)]
    return jnp.concatenate(parts, axis=1)


def _blockdiag(m, nc):
    g, a, b = m.shape
    gc = g // nc
    m = m.reshape(nc, gc, a, b)
    eye = jnp.eye(gc, dtype=m.dtype)
    return jnp.einsum("cgab,gh->cgahb", m, eye).reshape(nc, gc * a, gc * b)


def _blockdiag_take(m, g):
    nc = m.shape[0]
    gc = g // nc
    a, b = m.shape[1] // gc, m.shape[2] // gc
    blocks = [m[:, i * a:(i + 1) * a, i * b:(i + 1) * b] for i in range(gc)]
    return jnp.stack(blocks, axis=1).reshape(g, a, b)


def kernel(x, c, mod_w, mod_b, norm_pre, norm_post, ffn_w_in, ffn_w_out, mix_w_in, forget_b, ssm_a_re, ssm_a_im, ssm_log_dt, ssm_b_re, ssm_b_im, ssm_c_re, ssm_c_im, ssm_d, glu_w, attn_w_out, mix_w_out, loss_target, m_mod_w, m_mod_b, m_norm_pre, m_norm_post, m_ffn_w_in, m_ffn_w_out, m_mix_w_in, m_forget_b, m_ssm_a_re, m_ssm_a_im, m_ssm_log_dt, m_ssm_b_re, m_ssm_b_im, m_ssm_c_re, m_ssm_c_im, m_ssm_d, m_glu_w, m_attn_w_out, m_mix_w_out, v_mod_w, v_mod_b, v_norm_pre, v_norm_post, v_ffn_w_in, v_ffn_w_out, v_mix_w_in, v_forget_b, v_ssm_a_re, v_ssm_a_im, v_ssm_log_dt, v_ssm_b_re, v_ssm_b_im, v_ssm_c_re, v_ssm_c_im, v_ssm_d, v_glu_w, v_attn_w_out, v_mix_w_out):
    names = ["mod_w", "mod_b", "norm_pre", "norm_post", "ffn_w_in", "ffn_w_out", "mix_w_in", "forget_b",
             "ssm_a_re", "ssm_a_im", "ssm_log_dt", "ssm_b_re", "ssm_b_im", "ssm_c_re", "ssm_c_im", "ssm_d",
             "glu_w", "attn_w_out", "mix_w_out"]
    w_in = dict(zip(names, [mod_w, mod_b, norm_pre, norm_post, ffn_w_in, ffn_w_out, mix_w_in, forget_b,
                            ssm_a_re, ssm_a_im, ssm_log_dt, ssm_b_re, ssm_b_im, ssm_c_re, ssm_c_im, ssm_d,
                            glu_w, attn_w_out, mix_w_out]))
    m_in = dict(zip(names, [m_mod_w, m_mod_b, m_norm_pre, m_norm_post, m_ffn_w_in, m_ffn_w_out, m_mix_w_in,
                            m_forget_b, m_ssm_a_re, m_ssm_a_im, m_ssm_log_dt, m_ssm_b_re, m_ssm_b_im,
                            m_ssm_c_re, m_ssm_c_im, m_ssm_d, m_glu_w, m_attn_w_out, m_mix_w_out]))
    v_in = dict(zip(names, [v_mod_w, v_mod_b, v_norm_pre, v_norm_post, v_ffn_w_in, v_ffn_w_out, v_mix_w_in,
                            v_forget_b, v_ssm_a_re, v_ssm_a_im, v_ssm_log_dt, v_ssm_b_re, v_ssm_b_im,
                            v_ssm_c_re, v_ssm_c_im, v_ssm_d, v_glu_w, v_attn_w_out, v_mix_w_out]))

    depth = mod_w.shape[0]
    n_tok, d = x.shape[1], x.shape[2]
    ff = ffn_w_out.shape[2] * N_DEV
    heads = forget_b.shape[1]
    sw = ssm_d.shape[1]
    g_n, p_n, n_n = ssm_b_re.shape[1:]
    aw = attn_w_out.shape[1]
    dh = aw // heads
    iw = mix_w_in.shape[2] * N_DEV
    nc = sw // LANES
    hw = g_n * p_n // nc
    mod_cols = mod_w.shape[2]
    scale = dh ** -0.5
    assert iw == sw + 3 * aw + heads + 2 * d and heads <= LANES
    assert math.log2(scale).is_integer(), "q is pre-scaled in bf16: exact only for a power of two"
    off_u, off_q, off_f = 2 * d, 2 * d + sw, 2 * d + sw + 3 * aw
    iwp = off_f + LANES
    assert off_u % sw == 0 and off_q % aw == 0 and off_f % LANES == 0

    me = 4 * lax.axis_index("x") + 2 * lax.axis_index("y") + lax.axis_index("c")
    x2 = x.reshape(n_tok, d)
    tgt = loss_target.reshape(n_tok, d)

    silu_c = rowwise(_f_silu, [c], [], [((d,), F32)], name="silu_c")[0]
    big = ["ffn_w_in", "ffn_w_out", "mix_w_in", "glu_w", "attn_w_out", "mix_w_out"]
    ffn1 = [("ffn_w_in", 0), ("ffn_w_out", 0)]
    mix = [("mix_w_in", None), ("glu_w", None), ("attn_w_out", None), ("mix_w_out", None)]
    ffn2 = [("ffn_w_in", 1), ("ffn_w_out", 1)]

    def riders(l):
        nxt = [(l + 1, p) for p in ffn1 + mix] if l + 1 < depth else []
        return [(l, p) for p in ffn2] + nxt

    def piece_of(dct, piece, l):
        name, j = piece
        return dct[name][l] if j is None else dct[name][l][j]

    row_sharded = ("ffn_w_out", "mix_w_out")

    def send_shard(p, l):
        a = piece_of(w_in, p, l).astype(BF16)
        return a if p[0] in row_sharded else a.T

    def shards(l, pieces):
        return [send_shard(p, l) for p in pieces]

    cut = [0, sw, sw + aw, sw + 2 * aw, sw + 3 * aw, sw + 3 * aw + heads, sw + 3 * aw + heads + d, iw]
    lw = [dict(win_t=[None, None], wout=[None, None]) for _ in range(depth)]

    def install(l, pieces, gathered):
        for (name, j), g in zip(pieces, gathered):
            whole = g.reshape(-1, g.shape[-1])
            if name == "ffn_w_in":
                lw[l]["win_t"][j] = whole
            elif name == "ffn_w_out":
                lw[l]["wout"][j] = whole
            elif name == "mix_w_in":
                seg = lambda i: whole[cut[i]:cut[i + 1]]
                lw[l]["wmi_t"] = jnp.concatenate([seg(5), seg(6), seg(0), seg(1), seg(2), seg(3),
                                                  jnp.pad(seg(4), ((0, LANES - heads), (0, 0)))], axis=0)
            elif name == "mix_w_out":
                lw[l]["mo"] = whole
            else:
                lw[l]["glu_t" if name == "glu_w" else "ao_t"] = whole

    gathered = all_gather(
        [silu_c, norm_pre.reshape(-1, norm_pre.shape[-1]), norm_post.reshape(-1, norm_post.shape[-1])]
        + shards(0, ffn1[:1]), name="gather_first")
    sc_all = gathered[0].reshape(N_DEV, d)
    gpre = jnp.moveaxis(gathered[1].reshape(N_DEV, depth, 3, -1), 0, 2).reshape(depth, 3, d)
    gpost = jnp.moveaxis(gathered[2].reshape(N_DEV, depth, 3, -1), 0, 2).reshape(depth, 3, d)
    install(0, ffn1[:1], gathered[3:])

    sc_pad = jnp.pad(sc_all, ((0, LANES - N_DEV), (0, 0)))
    mod_part = jnp.stack([mm(sc_pad, mod_w[l], name=f"mod_fwd{l}")[:N_DEV] for l in range(depth)], axis=1)
    mod_part = mod_part + lax.dynamic_slice_in_dim(mod_b, me * mod_cols, mod_cols, axis=1)[None]
    mod_all = all_gather([mod_part], name="gather_mod")[0]
    mod_own = lax.dynamic_index_in_dim(mod_all, me, axis=1, keepdims=False)
    mod_own = mod_own.transpose(1, 0, 2).reshape(depth, 3, 3, d)
    res_w = (FFN_RES, 1.0, FFN_RES)

    def vec_a(l, i):
        return (gpre[l, i] * (1.0 + mod_own[l, i, 1])).reshape(1, d)

    def vec_sh(l, i):
        return mod_own[l, i, 0].reshape(1, d)

    def vec_b(l, i):
        return (res_w[i] * mod_own[l, i, 2] * gpost[l, i]).reshape(1, d)

    ssm = []
    for l in range(depth):
        (e, ph, bb_re, bb_im), disc_vjp = jax.vjp(_discretise,ssm_a_re[l], ssm_a_im[l], ssm_log_dt[l],
                                                  ssm_b_re[l], ssm_b_im[l])
        b_mat = jnp.concatenate([_blockdiag(bb_re.transpose(0, 2, 1), nc),
                                 _blockdiag(bb_im.transpose(0, 2, 1), nc)], axis=2)
        c_mat = jnp.concatenate([_blockdiag(ssm_c_re[l].transpose(0, 2, 1), nc),
                                 _blockdiag(-ssm_c_im[l].transpose(0, 2, 1), nc)], axis=1)
        ssm.append(dict(e=e, ph=ph, vjp=disc_vjp, b=b_mat.astype(BF16), c=c_mat.astype(BF16),
                        bt=b_mat.transpose(0, 2, 1).astype(BF16), ct=c_mat.transpose(0, 2, 1).astype(BF16),
                        tab_f=_scan_table(e, ph, nc, False), tab_r=_scan_table(e, ph, nc, True),
                        dvec=ssm_d[l].reshape(1, sw)))

    fb_pad = jnp.pad(forget_b, ((0, 0), (0, LANES - heads)))

    def heads_first(a):
        return a.reshape(n_tok, heads, dh).transpose(1, 0, 2)

    def heads_last(a):
        return a.transpose(1, 0, 2).reshape(n_tok, heads * dh)

    def mm_hosting(a, b, comm, **kw):
        if comm is None:
            return mm(a, b, **kw), []
        out, *arrived = mm(a, b, comm=comm, **kw)
        return out, arrived

    def ffn_fwd(xin, h, l, j, tag, comm=None, on_arrival=None):
        ag, au, m, *arrived = ffn_in_swiglu(h, lw[l]["win_t"][j], name=f"ffn_in_{tag}", comm=comm)
        if on_arrival is not None:
            on_arrival(arrived)
        y = mm(m, lw[l]["wout"][j], name=f"ffn_out_{tag}")
        return y, dict(x=xin, h=h, ag=ag, au=au, m=m, y=y)

    def mixer_fwd(xin, h, l, tag):
        s5 = ssm[l]
        proj = mm(h, lw[l]["wmi_t"], trans_b=True, name=f"mix_in_{tag}", out_dtype=BF16)
        projf = mm(h, lw[l]["wmi_t"][off_f:], trans_b=True, name=f"mix_in_f_{tag}")
        bu = mm_blockdiag(proj, s5["b"], a_cb0=off_u // LANES, name=f"ssm_bu_{tag}", out_dtype=BF16)
        st = ssm_scan(bu, s5["tab_f"], reverse=False, name=f"ssm_scan_{tag}")[0]
        y0 = mm_blockdiag(st, s5["c"], name=f"ssm_y_{tag}")
        ge, ys = rowwise(_f_gelu_in, [y0, (proj, sw, off_u // sw)], [s5["dvec"]],
                         [((sw,), BF16), ((sw,), F32)], name=f"gelu_{tag}")
        z = mm(ge, lw[l]["glu_t"], trans_b=True, name=f"glu_{tag}", out_dtype=BF16)
        cum = cum_fwd(projf, fb_pad[l:l + 1], col_block=0, name=f"cum_{tag}")
        crow = cum[:, :heads].T[:, None, :]
        q, k, v = [heads_first(proj[:, off_q + i * aw:off_q + (i + 1) * aw] * sc_).astype(BF16)
                   for i, sc_ in enumerate((scale, 1.0, 1.0))]
        nxt = gather_comm([send_shard(p, ll) for ll, p in riders(l)])
        o, lse, *arrived = attn_fwd(q, k, v, crow, name=f"attn_{tag}", comm=nxt)
        for (ll, p), got in zip(riders(l), arrived):
            install(ll, [p], [got])
        attn = heads_last(o).astype(BF16)
        yb = mm(attn, lw[l]["ao_t"], trans_b=True, name=f"attn_out_{tag}", out_dtype=BF16)
        mg = rowwise(_f_merge, [(z, d, 0), (z, d, 1), yb, (proj, d, 0), (proj, d, 1)], [],
                     [((d,), BF16)], name=f"merge_{tag}")[0]
        y = mm(mg, lw[l]["mo"], name=f"mix_out_{tag}")
        saved = dict(x=xin, h=h, proj=proj, projf=projf, st=st, ys=ys, ge=ge, z=z, q=q, k=k, v=v, o=o, lse=lse,
                     cum=cum, attn=attn, yb=yb, mg=mg, y=y)
        return y, saved

    subs = [(l, i) for l in range(depth) for i in range(3)]
    tag_of = lambda l, i: f"l{l}" + "amb"[i]
    saved = [[None] * 3 for _ in range(depth)]
    xc = x2
    h = rowwise(_f_pre, [xc], [vec_a(0, 0), vec_sh(0, 0)], [((d,), BF16)], name="pre_first")[0]
    for s, (l, i) in enumerate(subs):
        tag = tag_of(l, i)
        if i == 1:
            y, saved[l][i] = mixer_fwd(xc, h, l, tag)
        else:
            if (l, i) == (0, 0):
                late = ffn1[1:] + mix
                y, saved[l][i] = ffn_fwd(xc, h, l, 0, tag, gather_comm(shards(0, late)),
                                         lambda got: install(0, late, got))
            else:
                y, saved[l][i] = ffn_fwd(xc, h, l, i // 2, tag)
        if s + 1 < len(subs):
            l2, i2 = subs[s + 1]
            xc, h = rowwise(_f_post_pre, [xc, y], [vec_b(l, i), vec_a(l2, i2), vec_sh(l2, i2)],
                            [((d,), F32), ((d,), BF16)], name=f"post_pre_{tag}")

    def f_tail(xf, yl, t, bv):
        e_ = _f_post_add(xf, yl, bv) - t
        dxo = e_ * (1.0 / d)
        dyl, d_bv_ = _f_post_bwd(dxo, yl, bv)
        return dxo, dyl, _colsum(e_ * e_), d_bv_

    l_last, i_last = subs[-1]
    dx, dy, sq, d_bv = rowwise(f_tail, [xc, y, tgt], [vec_b(l_last, i_last)], [((d,), F32), ((d,), BF16)], [d, d],
                               name="loss_tail")
    loss_part = 0.5 * jnp.sum(sq) / d

    grads = {k: [None] * depth for k in big}
    small_g = [dict() for _ in range(depth)]
    dmod = [[None] * 3 for _ in range(depth)]
    dgpre = [[None] * 3 for _ in range(depth)]
    dgpost = [[None] * 3 for _ in range(depth)]

    def norm_grads(l, i, d_a, d_sh, d_bv):
        d_a, d_sh, d_bv = d_a.reshape(d), d_sh.reshape(d), d_bv.reshape(d)
        dmod[l][i] = jnp.stack([d_sh, d_a * gpre[l, i], res_w[i] * gpost[l, i] * d_bv])
        dgpre[l][i] = d_a * (1.0 + mod_own[l, i, 1])
        dgpost[l][i] = res_w[i] * mod_own[l, i, 2] * d_bv

    def ffn_bwd(dy, sv, l, j, tag, comm_dw=None, comm_dx_of=None):
        dm = mm(dy, lw[l]["wout"][j], trans_b=True, name=f"ffn_out_dx_{tag}", out_dtype=BF16)
        g_out = mm(sv["m"], dy, trans_a=True, name=f"ffn_out_dw_{tag}", out_dtype=BF16, tm=1408, tn=1024)
        da = rowwise(_f_swiglu_bwd, [sv["ag"], sv["au"], dm], [], [((ff, ff), BF16)],
                     name=f"swiglu_bwd_{tag}")[0]
        g_in, arrived_dw = mm_hosting(sv["h"], da, comm_dw, trans_a=True, name=f"ffn_in_dw_{tag}",
                                      out_dtype=BF16)
        comm_dx = comm_dx_of(g_in, g_out) if comm_dx_of is not None else None
        dh_, arrived_dx = mm_hosting(da, lw[l]["win_t"][j], comm_dx, name=f"ffn_in_dx_{tag}")
        return dh_, g_in, g_out, arrived_dw, arrived_dx

    def mixer_bwd(dy, sv, l, tag, comm):
        s5 = ssm[l]
        proj = sv["proj"]
        dmg = mm(dy, lw[l]["mo"], trans_b=True, name=f"mix_out_dx_{tag}", out_dtype=BF16)
        g_mo = mm(sv["mg"], dy, trans_a=True, name=f"mix_out_dw_{tag}", out_dtype=BF16)
        dz, dyb, dproj = rowwise(
            _f_merge_bwd, [dmg, (sv["z"], d, 0), (sv["z"], d, 1), sv["yb"], (proj, d, 0), (proj, d, 1)], [],
            [((d, d), BF16), ((d,), BF16), ((d, d), BF16, (iwp, 0))], name=f"merge_bwd_{tag}")
        dge = mm(dz, lw[l]["glu_t"], name=f"glu_dx_{tag}")
        g_glu = mm(sv["ge"], dz, trans_a=True, name=f"glu_dw_{tag}", out_dtype=BF16)
        dys, d_dvec = rowwise(_f_gelu_bwd, [dge, sv["ys"], (proj, sw, off_u // sw)], [], [((sw,), BF16)],
                              [sw], name=f"gelu_bwd_{tag}")
        gadj = mm_blockdiag(dys, s5["ct"], name=f"ssm_dy_{tag}", out_dtype=BF16)
        adj, dlam8 = ssm_scan(gadj, s5["tab_r"], reverse=True, s_prev=sv["st"], name=f"ssm_scan_bwd_{tag}")
        du0 = mm_blockdiag(adj, s5["bt"], name=f"ssm_du_{tag}")
        d_bmat = mm_blockdiag_tn(proj, adj, g_n=nc, ka=LANES, kb=2 * hw, a_cb0=off_u // LANES,
                                 name=f"ssm_db_{tag}")
        d_cmat = mm_blockdiag_tn(sv["st"], dys, g_n=nc, ka=2 * hw, kb=LANES, name=f"ssm_dc_{tag}")
        dproj = rowwise(_f_du_fin, [du0, dys], [s5["dvec"]], [((sw,), BF16, (iwp, off_u // sw), dproj)],
                        name=f"ssm_du_fin_{tag}")[0]
        dlam = jnp.sum(dlam8, axis=1)
        dlam_re, dlam_im = dlam[:, :hw].reshape(g_n, p_n), dlam[:, hw:].reshape(g_n, p_n)
        dbb_re = _blockdiag_take(d_bmat[:, :, :hw], g_n).transpose(0, 2, 1)
        dbb_im = _blockdiag_take(d_bmat[:, :, hw:], g_n).transpose(0, 2, 1)
        mag = jnp.exp(s5["e"])
        lr, li = mag * jnp.cos(s5["ph"]), mag * jnp.sin(s5["ph"])
        d_e = dlam_re * lr + dlam_im * li
        d_ph = -dlam_re * li + dlam_im * lr
        da_re, da_im, dlog_dt, db_re, db_im = s5["vjp"]((d_e, d_ph, dbb_re, dbb_im))
        small_g[l].update(
            ssm_a_re=da_re, ssm_a_im=da_im, ssm_log_dt=dlog_dt, ssm_b_re=db_re, ssm_b_im=db_im,
            ssm_c_re=_blockdiag_take(d_cmat[:, :hw, :], g_n).transpose(0, 2, 1),
            ssm_c_im=-_blockdiag_take(d_cmat[:, hw:, :], g_n).transpose(0, 2, 1),
            ssm_d=d_dvec.reshape(sw))
        dattn = mm(dyb, lw[l]["ao_t"], name=f"attn_out_dx_{tag}", out_dtype=BF16)
        g_ao = mm(sv["attn"], dyb, trans_a=True, name=f"attn_out_dw_{tag}", out_dtype=BF16)
        dq, dk, dv, dcum, *arrived = attn_bwd(sv["q"], sv["k"], sv["v"], heads_first(dattn), sv["o"], sv["lse"],
                                              sv["cum"], scale=scale, tag=tag, comm=comm)
        df, dfb = cum_bwd(dcum, sv["projf"], fb_pad[l:l + 1], col_block=0, name=f"cum_bwd_{tag}")
        small_g[l]["forget_b"] = dfb[0, :heads]
        for piece, off in ((heads_last(dq), off_q), (heads_last(dk), off_q + aw), (heads_last(dv), off_q + 2 * aw),
                           (df, off_f)):
            dproj = lax.dynamic_update_slice(dproj, piece.astype(BF16), (0, off))
        g_mi = mm(sv["h"], dproj, trans_a=True, name=f"mix_in_dw_{tag}", out_dtype=BF16)
        dh_ = mm(dproj, lw[l]["wmi_t"], name=f"mix_in_dx_{tag}")
        g_mi = jnp.concatenate([g_mi[:, off_u:off_f + heads], g_mi[:, :off_u]], axis=1)
        return dh_, g_mi, g_glu, g_ao, g_mo, arrived

    def split_last(a):
        return jnp.moveaxis(a.reshape(a.shape[:-1] + (N_DEV, a.shape[-1] // N_DEV)), -2, 0)

    def split_rows(a):
        return jnp.moveaxis(a.reshape(a.shape[:-2] + (N_DEV, a.shape[-2] // N_DEV, a.shape[-1])), -3, 0)

    def owner_blocks(l, pieces):
        out = []
        for name, j in pieces:
            if name == "ffn_w_in":
                out.append(split_last(g_ffn_in[l][j]))
            elif name == "ffn_w_out":
                out.append(split_rows(g_ffn_out[l][j]))
            elif name == "mix_w_out":
                out.append(split_rows(grads[name][l]))
            else:
                out.append(split_last(grads[name][l]))
        return out

    g_ffn_in = [[None, None] for _ in range(depth)]
    g_ffn_out = [[None, None] for _ in range(depth)]
    parts = {}

    def record(l, pieces, arrived):
        for p, a in zip(pieces, arrived):
            parts[(p, l)] = a

    def last_ffn_blocks(g_in, g_out):
        return exchange_comm([split_last(g_in), split_rows(g_out)])

    for s in reversed(range(len(subs))):
        l, i = subs[s]
        sv, tag = saved[l][i], tag_of(l, i)
        if i == 2:
            dh_, g_ffn_in[l][1], g_ffn_out[l][1], _, _ = ffn_bwd(dy, sv, l, 1, tag)
        elif i == 1:
            pending = exchange_comm([owner_blocks(ll, [p])[0] for ll, p in riders(l)])
            (dh_, grads["mix_w_in"][l], grads["glu_w"][l], grads["attn_w_out"][l], grads["mix_w_out"][l],
             arrived) = mixer_bwd(dy, sv, l, tag, pending)
            for (ll, p), got in zip(riders(l), arrived):
                record(ll, [p], [got])
        elif l == 0:
            dh_, g_ffn_in[l][0], g_ffn_out[l][0], arrived_mix, arrived_ffn1 = ffn_bwd(
                dy, sv, l, 0, tag, exchange_comm(owner_blocks(0, mix)), last_ffn_blocks)
            record(0, mix, arrived_mix)
            record(0, ffn1, arrived_ffn1)
        else:
            dh_, g_ffn_in[l][0], g_ffn_out[l][0], _, _ = ffn_bwd(dy, sv, l, 0, tag)
        if s > 0:
            lp, ip = subs[s - 1]
            dx, dy, d_a, d_sh, d_bv_before = rowwise(
                _f_pre_post_bwd, [dx, dh_, sv["x"], saved[lp][ip]["y"]], [vec_a(l, i), vec_b(lp, ip)],
                [((d,), F32), ((d,), BF16)], [d, d, d], name=f"pre_post_bwd_{tag}")
        else:
            dx, d_a, d_sh = rowwise(_f_pre_bwd, [dx, dh_, sv["x"]], [vec_a(l, i)], [((d,), F32)], [d, d],
                                    name=f"pre_bwd_{tag}")
            d_bv_before = None
        norm_grads(l, i, d_a, d_sh, d_bv)
        d_bv = d_bv_before
    grad_x = dx.reshape(x.shape)

    small_names = ["forget_b", "ssm_a_re", "ssm_a_im", "ssm_log_dt", "ssm_b_re", "ssm_b_im", "ssm_c_re",
                   "ssm_c_im", "ssm_d"]
    pieces = [loss_part.reshape(1), jnp.stack([jnp.stack(dmod[l]) for l in range(depth)]).reshape(-1),
              jnp.stack([jnp.stack(dgpre[l]) for l in range(depth)]).reshape(-1),
              jnp.stack([jnp.stack(dgpost[l]) for l in range(depth)]).reshape(-1)]
    pieces += [jnp.stack([small_g[l][k] for l in range(depth)]).reshape(-1) for k in small_names]
    sizes = [p.size for p in pieces]
    chunk = SUBLANES * 1024
    total = -(-sum(sizes) // chunk) * chunk
    pack = jnp.pad(jnp.concatenate(pieces), (0, total - sum(sizes))).reshape(total // 1024, 1024)
    pack_all = all_gather([pack], name="gather_small_grads")[0]
    pack_sum = rowwise(_f_sum_parts, [(pack_all, p) for p in range(N_DEV)], [], [((1024,), F32)],
                       name="sum_small_grads")[0].reshape(-1)
    offs = [0]
    for s_ in sizes:
        offs.append(offs[-1] + s_)
    take = lambda i: pack_sum[offs[i]:offs[i + 1]]
    loss = take(0).reshape(())
    g_small = {"mod_b": take(1).reshape(mod_b.shape)}
    g_pre_full, g_post_full = take(2).reshape(depth, 3, d), take(3).reshape(depth, 3, d)
    shard = norm_pre.shape[-1]
    g_small["norm_pre"] = lax.dynamic_slice_in_dim(g_pre_full, me * shard, shard, axis=2)
    g_small["norm_post"] = lax.dynamic_slice_in_dim(g_post_full, me * shard, shard, axis=2)
    for i, k in enumerate(small_names):
        g_small[k] = take(4 + i).reshape(w_in[k].shape)

    dmod_all = pack_all.reshape(N_DEV, -1)[:, offs[1]:offs[2]].reshape(N_DEV, depth, 9 * d)
    dmod_mine = lax.dynamic_slice_in_dim(dmod_all, me * mod_cols, mod_cols, axis=2)
    sct_pad = jnp.pad(sc_all.T, ((0, 0), (0, LANES - N_DEV)))
    g_mod_w = jnp.stack([
        mm(sct_pad, jnp.pad(dmod_mine[:, l], ((0, LANES - N_DEV), (0, 0))), name=f"mod_dw{l}")
        for l in range(depth)])

    out_g, out_d, out_m, out_v = {}, {}, {}, {}
    flat = lambda a: a.reshape(-1, a.shape[-1])
    res = adamw(g_mod_w.reshape(1, -1, mod_cols), flat(mod_w), flat(m_mod_w), flat(v_mod_w), name="adamw_mod_w")
    out_g["mod_w"], out_d["mod_w"], out_m["mod_w"], out_v["mod_w"] = [r.reshape(mod_w.shape) for r in res]
    for k in big:
        js = (0, 1) if k.startswith("ffn") else (None,)
        total_rows = w_in[k].size // w_in[k].shape[-1]
        res = None
        for l in range(depth):
            for j in js:
                p = (k, j)
                w_p = piece_of(w_in, p, l)
                row0 = (l * len(js) + (j or 0)) * (w_p.size // w_p.shape[-1])
                res = adamw(parts[(p, l)], flat(w_p), flat(piece_of(m_in, p, l)), flat(piece_of(v_in, p, l)),
                            name=f"adamw_{k}_l{l}" + ("" if j is None else f"_{j}"), into=(res, total_rows, row0))
        for dct, r in zip((out_g, out_d, out_m, out_v), res):
            dct[k] = r.reshape(w_in[k].shape)
    small_all = ["mod_b", "norm_pre", "norm_post"] + small_names

    def pack_small(dct):
        flat = jnp.concatenate([dct[k].reshape(-1) for k in small_all])
        tot = -(-flat.size // chunk) * chunk
        return jnp.pad(flat, (0, tot - flat.size)).reshape(tot // 1024, 1024)

    res = adamw(pack_small(g_small)[None], pack_small(w_in), pack_small(m_in), pack_small(v_in),
                name="adamw_small")
    pos = 0
    for k in small_all:
        size = w_in[k].size
        for dct, r in zip((out_g, out_d, out_m, out_v), res):
            dct[k] = r.reshape(-1)[pos:pos + size].reshape(w_in[k].shape)
        pos += size

    return (loss, grad_x, *[out_g[k] for k in names], *[out_d[k] for k in names],
            *[out_m[k] for k in names], *[out_v[k] for k in names])
```

```python
import functools
import math

import jax
import jax.numpy as jnp
from jax import lax
from jax.experimental import pallas as pl
from jax.experimental.pallas import tpu as pltpu

F32 = jnp.float32
BF16 = jnp.bfloat16
MESH = pl.DeviceIdType.MESH
N_DEV = 8
LANES = 128
SUBLANES = 8
VMEM_LIMIT = 48 * 1024 * 1024
RMS_EPS = 1e-6
FFN_RES = 0.5
ADAM_LR = 0.001
ADAM_B1 = 0.9
ADAM_B2 = 0.999
ADAM_EPS = 1e-08
ADAM_WD = 0.01
ADAM_STEP = 10
GELU_C = math.sqrt(2.0 / math.pi)
GELU_A = 0.044715


def _pick(dim, target, mult=LANES):
    t = (min(dim, target) // mult) * mult
    while t >= mult:
        if dim % t == 0:
            return t
        t -= mult
    return dim


def _params(sem):
    return pltpu.CompilerParams(dimension_semantics=sem, vmem_limit_bytes=VMEM_LIMIT)


def _sigmoid(x):
    return 1.0 / (1.0 + jnp.exp(-x))


def mm(a, b, *, name, trans_a=False, trans_b=False, out_dtype=F32, tm=1024, tn=1408, tk=2816, comm=None):
    if trans_a:
        kdim, m = a.shape
    else:
        m, kdim = a.shape
    if trans_b:
        n, kb = b.shape
    else:
        kb, n = b.shape
    assert kdim == kb, (a.shape, b.shape)
    tm, tn, tk = _pick(m, tm), _pick(n, tn), _pick(kdim, tk)
    gm, gn, nk = m // tm, n // tn, kdim // tk
    dims = (((0 if trans_a else 1,), (1 if trans_b else 0,)), ((), ()))
    host = _Hosted(comm, 2, 1, 1 if nk > 1 else 0)

    def body(*refs):
        (a_ref, b_ref, o_ref, *acc), crefs = host.split(refs)
        i, j, k = pl.program_id(0), pl.program_id(1), pl.program_id(2)
        host.phase("start", (i == 0) & (j == 0) & (k == 0), crefs)
        prod = lax.dot_general(a_ref[...].astype(BF16), b_ref[...].astype(BF16), dims,
                               preferred_element_type=F32)
        if nk == 1:
            o_ref[...] = prod.astype(out_dtype)
        else:
            acc_ref, = acc

            @pl.when(k == 0)
            def _():
                acc_ref[...] = prod

            @pl.when((k > 0) & (k < nk - 1))
            def _():
                acc_ref[...] += prod

            @pl.when(k == nk - 1)
            def _():
                o_ref[...] = (acc_ref[...] + prod).astype(out_dtype)

        last = (i == gm - 1) & (j == gn - 1) & (k == nk - 1)
        host.phase("mid", last, crefs)
        host.phase("finish", last, crefs)

    a_spec = (pl.BlockSpec((tk, tm), lambda i, j, k: (k, i)) if trans_a
              else pl.BlockSpec((tm, tk), lambda i, j, k: (i, k)))
    b_spec = (pl.BlockSpec((tn, tk), lambda i, j, k: (j, k)) if trans_b
              else pl.BlockSpec((tk, tn), lambda i, j, k: (k, j)))
    in_specs, out_specs, out_shape, scratch, extra = host.specs(
        [a_spec, b_spec], [pl.BlockSpec((tm, tn), lambda i, j, k: (i, j))],
        [jax.ShapeDtypeStruct((m, n), out_dtype)], [pltpu.VMEM((tm, tn), F32)] if nk > 1 else [])
    res = pl.pallas_call(
        body, name=name, grid=(gm, gn, nk),
        in_specs=in_specs, out_specs=out_specs, out_shape=out_shape, scratch_shapes=scratch,
        compiler_params=_params(("arbitrary", "arbitrary", "arbitrary")),
    )(a, b, *extra)
    return res if comm else res[0]


def ffn_in_swiglu(h, w_t, *, name, comm=None, tm=1024, tn=1408):
    m, kdim = h.shape
    f = w_t.shape[0] // 2
    tm, tn = _pick(m, tm), _pick(f, tn)
    gm, gn = m // tm, f // tn
    host = _Hosted(comm, 3, 3, 0)

    def body(*refs):
        (h_ref, wg_ref, wu_ref, g_ref, u_ref, m_ref), crefs = host.split(refs)
        i, j = pl.program_id(0), pl.program_id(1)
        host.phase("start", (i == 0) & (j == 0), crefs)
        hv = h_ref[...]
        g = lax.dot_general(hv, wg_ref[...], (((1,), (1,)), ((), ())), preferred_element_type=F32)
        u = lax.dot_general(hv, wu_ref[...], (((1,), (1,)), ((), ())), preferred_element_type=F32)
        g_ref[...] = g.astype(BF16)
        u_ref[...] = u.astype(BF16)
        m_ref[...] = ((g * _sigmoid(g)) * u).astype(BF16)
        last = (i == gm - 1) & (j == gn - 1)
        host.phase("mid", last, crefs)
        host.phase("finish", last, crefs)

    out_spec = pl.BlockSpec((tm, tn), lambda i, j: (i, j))
    in_specs, out_specs, out_shape, scratch, extra = host.specs(
        [pl.BlockSpec((tm, kdim), lambda i, j: (i, 0)), pl.BlockSpec((tn, kdim), lambda i, j: (j, 0)),
         pl.BlockSpec((tn, kdim), lambda i, j: (j + gn, 0))],
        [out_spec] * 3, [jax.ShapeDtypeStruct((m, f), BF16)] * 3, [])
    return pl.pallas_call(
        body, name=name, grid=(gm, gn),
        in_specs=in_specs, out_specs=out_specs, out_shape=out_shape, scratch_shapes=scratch,
        compiler_params=_params(("arbitrary", "arbitrary")),
    )(h, w_t, w_t, *extra)


def mm_blockdiag(a, b, *, name, a_cb0=0, out_dtype=F32, tm=512):
    m = a.shape[0]
    g_n, ka, nb = b.shape
    tm = _pick(m, tm)
    assert a_cb0 % g_n == 0

    def body(a_ref, b_ref, o_ref):
        for g in range(g_n):
            o_ref[:, g * nb:(g + 1) * nb] = jnp.dot(
                a_ref[:, g * ka:(g + 1) * ka].astype(BF16), b_ref[g].astype(BF16),
                preferred_element_type=F32).astype(out_dtype)

    return pl.pallas_call(
        body, name=name, grid=(m // tm,),
        in_specs=[pl.BlockSpec((tm, g_n * ka), lambda i: (i, a_cb0 // g_n)),
                  pl.BlockSpec((g_n, ka, nb), lambda i: (0, 0, 0))],
        out_specs=pl.BlockSpec((tm, g_n * nb), lambda i: (i, 0)),
        out_shape=jax.ShapeDtypeStruct((m, g_n * nb), out_dtype),
        compiler_params=_params(("parallel",)),
    )(a, b)


def mm_blockdiag_tn(a, b, *, name, g_n, ka, kb, a_cb0=0, b_cb0=0, tk=512):
    rows = a.shape[0]
    tk = _pick(rows, tk)
    nk = rows // tk
    assert a_cb0 % g_n == 0 and b_cb0 % g_n == 0

    def body(a_ref, b_ref, o_ref):
        @pl.when(pl.program_id(0) == 0)
        def _():
            o_ref[...] = jnp.zeros_like(o_ref)

        for g in range(g_n):
            o_ref[g] += lax.dot_general(a_ref[:, g * ka:(g + 1) * ka].astype(BF16),
                                        b_ref[:, g * kb:(g + 1) * kb].astype(BF16),
                                        (((0,), (0,)), ((), ())), preferred_element_type=F32)

    return pl.pallas_call(
        body, name=name, grid=(nk,),
        in_specs=[pl.BlockSpec((tk, g_n * ka), lambda k: (k, a_cb0 // g_n)),
                  pl.BlockSpec((tk, g_n * kb), lambda k: (k, b_cb0 // g_n))],
        out_specs=pl.BlockSpec((g_n, ka, kb), lambda k: (0, 0, 0)),
        out_shape=jax.ShapeDtypeStruct((g_n, ka, kb), F32),
        compiler_params=_params(("arbitrary",)),
    )(a, b)


def rowwise(fn, rows, vecs, outs, reds=(), *, name, tm=512):
    metas = []
    for r in rows:
        if isinstance(r, tuple) and len(r) == 3:
            metas.append(("col", r[0], r[1], r[2]))
        elif isinstance(r, tuple):
            metas.append(("lead", r[0], r[0].shape[2], r[1]))
        else:
            metas.append(("full", r, r.shape[1], 0))
    n_rows = metas[0][1].shape[1] if metas[0][0] == "lead" else metas[0][1].shape[0]
    rc = 16 if n_rows % 16 == 0 else (SUBLANES if n_rows % SUBLANES == 0 else n_rows)
    tm = _pick(n_rows, tm, rc)
    n_inner = tm // rc
    windows = [(o[2] if len(o) > 2 else None) for o in outs]
    bases = [(k, o[3]) for k, o in enumerate(outs) if len(o) > 3 and o[3] is not None]
    outs = [(o[0], o[1]) for o in outs]
    nr, nv, no, nbase = len(metas), len(vecs), len(outs), len(bases)

    def body(*refs):
        row_refs, vec_refs = refs[:nr], refs[nr:nr + nv]
        refs = refs[nr + nv + nbase:]
        out_refs, red_refs = refs[:no], refs[no:]
        if reds:
            @pl.when(pl.program_id(0) == 0)
            def _():
                for rr in red_refs:
                    rr[...] = jnp.zeros_like(rr)
        vec_vals = [v[...] for v in vec_refs]

        def step(s, carry):
            r0 = pl.multiple_of(s * rc, rc)
            vals = [ref[pl.ds(r0, rc), :] for ref in row_refs]
            res = fn(*vals, *vec_vals)
            if not isinstance(res, (tuple, list)):
                res = (res,)
            for o_ref, (widths, dt), val in zip(out_refs, outs, res[:no]):
                pieces = val if isinstance(val, (tuple, list)) else (val,)
                off = 0
                for w_, piece in zip(widths, pieces):
                    o_ref[pl.ds(r0, rc), off:off + w_] = piece.astype(dt)
                    off += w_
            for rr, val in zip(red_refs, res[no:]):
                rr[...] += val
            return carry

        lax.fori_loop(0, n_inner, step, 0, unroll=min(n_inner, 4))

    in_specs = []
    for kind, arr, w_, idx in metas:
        if kind == "col":
            in_specs.append(pl.BlockSpec((tm, w_), functools.partial(lambda i, cb: (i, cb), cb=idx)))
        elif kind == "lead":
            in_specs.append(pl.BlockSpec((None, tm, w_), functools.partial(lambda i, p: (p, i, 0), p=idx)))
        else:
            in_specs.append(pl.BlockSpec((tm, w_), lambda i: (i, 0)))
    for v in vecs:
        in_specs.append(pl.BlockSpec(v.shape, lambda i: (0, 0)))
    in_specs += [pl.BlockSpec(memory_space=pl.ANY)] * nbase
    out_specs, out_shape = [], []
    for (ws, dt), win in zip(outs, windows):
        total, cb, total_rows, row0 = (tuple(win) + (n_rows, 0))[:4] if win is not None else (sum(ws), 0, n_rows, 0)
        assert row0 % tm == 0
        out_specs.append(pl.BlockSpec((tm, sum(ws)),
                                      functools.partial(lambda i, cb, rb: (i + rb, cb), cb=cb, rb=row0 // tm)))
        out_shape.append(jax.ShapeDtypeStruct((total_rows, total), dt))
    out_specs += [pl.BlockSpec((1, w_), lambda i: (0, 0)) for w_ in reds]
    out_shape += [jax.ShapeDtypeStruct((1, w_), F32) for w_ in reds]
    res = pl.pallas_call(
        body, name=name, grid=(n_rows // tm,),
        in_specs=in_specs, out_specs=out_specs, out_shape=out_shape,
        input_output_aliases={nr + nv + b: k for b, (k, _) in enumerate(bases)},
        compiler_params=_params(("arbitrary",)),
    )(*[m[1] for m in metas], *vecs, *[b for _, b in bases])
    return res


def _rms(x):
    return lax.rsqrt(jnp.mean(x * x, axis=-1, keepdims=True) + RMS_EPS)


def _colsum(x):
    return jnp.sum(x, axis=0, keepdims=True)


def _f_silu(c):
    return c * _sigmoid(c)


def _f_pre(x, a, sh):
    return (x * _rms(x)) * a + sh


def _f_post_add(x, y, bv):
    return x + (y * _rms(y)) * bv


def _f_post_pre(x, y, bv, a, sh):
    x = _f_post_add(x, y, bv)
    return x, _f_pre(x, a, sh)


def _f_pre_post_bwd(dxo, dh, x, y, a, bv):
    dx, d_a, d_sh = _f_pre_bwd(dxo, dh, x, a)
    dy, d_bv = _f_post_bwd(dx, y, bv)
    return dx, dy, d_a, d_sh, d_bv


def _f_post_bwd(dxo, y, bv):
    ry = _rms(y)
    yn = y * ry
    dyn = dxo * bv
    dy = ry * (dyn - yn * jnp.mean(dyn * yn, axis=-1, keepdims=True))
    return dy, _colsum(dxo * yn)


def _f_pre_bwd(dxo, dh, x, a):
    r = _rms(x)
    xn = x * r
    dxn = dh * a
    dx = dxo + r * (dxn - xn * jnp.mean(dxn * xn, axis=-1, keepdims=True))
    return dx, _colsum(dh * xn), _colsum(dh)


def _f_swiglu_bwd(g, u, dm):
    g, u, dm = g.astype(F32), u.astype(F32), dm.astype(F32)
    sg = _sigmoid(g)
    dg = dm * u * (sg * (1.0 + g * (1.0 - sg)))
    du = dm * (g * sg)
    return ((dg, du),)


def _gelu_t(x):
    return jnp.tanh(GELU_C * (x + GELU_A * x * x * x))


def _f_gelu_in(y0, u, dvec):
    y = y0 + dvec * u
    return 0.5 * y * (1.0 + _gelu_t(y)), y


def _f_gelu_bwd(dge, y, u):
    t = _gelu_t(y)
    dy = dge * (0.5 * (1.0 + t) + 0.5 * y * (1.0 - t * t) * GELU_C * (1.0 + 3.0 * GELU_A * y * y))
    return dy, _colsum(dy * u)


def _f_du_fin(du0, dys, dvec):
    return du0 + dvec * dys.astype(F32)


def _f_merge(zv, zg, yb, ga, gb):
    zv, zg, yb, ga, gb = [a.astype(F32) for a in (zv, zg, yb, ga, gb)]
    return _sigmoid(ga) * (zv * _sigmoid(zg)) + _sigmoid(gb) * yb


def _f_merge_bwd(dmg, zv, zg, yb, ga, gb):
    zv, zg, yb, ga, gb = [a.astype(F32) for a in (zv, zg, yb, ga, gb)]
    sa, sb, sz = _sigmoid(ga), _sigmoid(gb), _sigmoid(zg)
    ya = zv * sz
    dya = dmg * sa
    dga = dmg * ya * sa * (1.0 - sa)
    dyb = dmg * sb
    dgb = dmg * yb * sb * (1.0 - sb)
    dzv = dya * sz
    dzg = dya * zv * sz * (1.0 - sz)
    return (dzv, dzg), dyb, (dga, dgb)


def _f_sum_parts(*parts):
    acc = parts[0].astype(F32)
    for p in parts[1:]:
        acc = acc + p.astype(F32)
    return acc


def _f_adamw(*args):
    parts, (w, m, v) = args[:-3], args[-3:]
    g = _f_sum_parts(*parts)
    m = ADAM_B1 * m + (1.0 - ADAM_B1) * g
    v = ADAM_B2 * v + (1.0 - ADAM_B2) * (g * g)
    m_hat = m / (1.0 - ADAM_B1 ** ADAM_STEP)
    v_hat = v / (1.0 - ADAM_B2 ** ADAM_STEP)
    delta = -ADAM_LR * (m_hat / (jnp.sqrt(v_hat) + ADAM_EPS) + ADAM_WD * w)
    return g, delta, m, v


def adamw(parts3, w, m, v, *, name, into=None):
    c = w.shape[1]
    rows = [(parts3, p) for p in range(parts3.shape[0])] + [w, m, v]
    if into is None:
        return rowwise(_f_adamw, rows, [], [((c,), F32)] * 4, name=name)
    bases, total_rows, row0 = into
    outs = [((c,), F32, (c, 0, total_rows, row0), bases[k] if bases else None) for k in range(4)]
    return rowwise(_f_adamw, rows, [], outs, name=name)


def _tri_dot(tri, x, dims=(((1,), (0,)), ((), ()))):
    x1 = x.astype(BF16)
    r1 = x - x1.astype(F32)
    x2 = r1.astype(BF16)
    x3 = (r1 - x2.astype(F32)).astype(BF16)
    dot = functools.partial(lax.dot_general, dimension_numbers=dims, preferred_element_type=F32)
    return dot(tri, x1) + dot(tri, x2) + dot(tri, x3)


def cum_fwd(proj, fb, *, col_block, name, t=256):
    n = proj.shape[0]
    t = _pick(n, t, SUBLANES)

    def body(f_ref, fb_ref, cum_ref, car_ref):
        @pl.when(pl.program_id(0) == 0)
        def _():
            car_ref[...] = jnp.zeros_like(car_ref)

        x = f_ref[...] + fb_ref[...]
        lf = jnp.minimum(x, 0.0) - jnp.log(1.0 + jnp.exp(-jnp.abs(x)))
        r = lax.broadcasted_iota(jnp.int32, (t, t), 0)
        c = lax.broadcasted_iota(jnp.int32, (t, t), 1)
        cs = _tri_dot((c <= r).astype(BF16), lf) + car_ref[0:1, :]
        cum_ref[...] = cs
        car_ref[0:1, :] = cs[t - 1:t, :]

    return pl.pallas_call(
        body, name=name, grid=(n // t,),
        in_specs=[pl.BlockSpec((t, LANES), lambda i: (i, col_block)),
                  pl.BlockSpec((1, LANES), lambda i: (0, 0))],
        out_specs=pl.BlockSpec((t, LANES), lambda i: (i, 0)),
        out_shape=jax.ShapeDtypeStruct((n, LANES), F32),
        scratch_shapes=[pltpu.VMEM((SUBLANES, LANES), F32)],
        compiler_params=_params(("arbitrary",)),
    )(proj, fb)


def cum_bwd(dcum, proj, fb, *, col_block, name, t=256):
    n = proj.shape[0]
    slabs = dcum.shape[0]
    t = _pick(n, t, SUBLANES)
    nb = n // t

    def body(dc_ref, f_ref, fb_ref, df_ref, dfb_ref, car_ref):
        @pl.when(pl.program_id(0) == 0)
        def _():
            car_ref[...] = jnp.zeros_like(car_ref)
            dfb_ref[...] = jnp.zeros_like(dfb_ref)

        r = lax.broadcasted_iota(jnp.int32, (t, t), 0)
        c = lax.broadcasted_iota(jnp.int32, (t, t), 1)
        dl = _tri_dot((c >= r).astype(BF16), jnp.sum(dc_ref[...], axis=0)) + car_ref[0:1, :]
        car_ref[0:1, :] = dl[0:1, :]
        x = f_ref[...] + fb_ref[...]
        df = dl * (1.0 / (1.0 + jnp.exp(x)))
        df_ref[...] = df
        dfb_ref[...] += _colsum(df)

    return pl.pallas_call(
        body, name=name, grid=(nb,),
        in_specs=[pl.BlockSpec((slabs, t, LANES), lambda i: (0, nb - 1 - i, 0)),
                  pl.BlockSpec((t, LANES), lambda i: (nb - 1 - i, col_block)),
                  pl.BlockSpec((1, LANES), lambda i: (0, 0))],
        out_specs=[pl.BlockSpec((t, LANES), lambda i: (nb - 1 - i, 0)),
                   pl.BlockSpec((1, LANES), lambda i: (0, 0))],
        out_shape=[jax.ShapeDtypeStruct((n, LANES), F32), jax.ShapeDtypeStruct((1, LANES), F32)],
        scratch_shapes=[pltpu.VMEM((SUBLANES, LANES), F32)],
        compiler_params=_params(("arbitrary",)),
    )(dcum, proj, fb)


_NT =(((1,), (1,)), ((), ()))
_TN = (((0,), (0,)), ((), ()))


def _lane_sums_as_row(x):
    return _tri_dot(jnp.ones((SUBLANES, x.shape[1]), BF16), x, _NT)[0:1, :]


def _causal_keep(t):
    return lax.broadcasted_iota(jnp.int32, (t, t), 1) <= lax.broadcasted_iota(jnp.int32, (t, t), 0)


def attn_fwd(q, k, v, crow, *, name, t=512, hb=8, comm=None):
    h_n, n, dh = q.shape
    hb = min(hb, h_n)
    t = _pick(n, t)
    nb = n // t
    ng = h_n // hb
    host = _Hosted(comm, 4, 2, 3)

    def body(*refs):
        (q_ref, k_ref, v_ref, cr_ref, o_ref, lse_ref, m_sc, l_sc, acc_sc), crefs = host.split(refs)
        g, i, j = pl.program_id(0), pl.program_id(1), pl.program_id(2)
        host.phase("start", (g == 0) & (i == 0) & (j == 0), crefs)
        host.phase("mid", (g == ng - 1) & (i == (3 * nb) // 4) & (j == 0), crefs)

        @pl.when(j == 0)
        def _():
            m_sc[...] = jnp.full_like(m_sc, -jnp.inf)
            l_sc[...] = jnp.zeros_like(l_sc)
            acc_sc[...] = jnp.zeros_like(acc_sc)

        def update(diagonal):
            keep = _causal_keep(t) if diagonal else None
            half = t // 2
            units = [(h, pl.ds(r * half, half), slice(r * half, (r + 1) * half)) for h in range(hb) for r in (0, 1)]
            ss = [lax.dot_general(q_ref[h], k_ref[h], _NT, preferred_element_type=F32) for h in range(hb)]
            pairs, alphas = [], []
            for u, (h, rows, rs) in enumerate(units):
                s = ss[h][rs] - cr_ref[h]
                if diagonal:
                    s = jnp.where(keep[rs], s, -jnp.inf)
                m_prev = m_sc[h, rows, :]
                m_new = jnp.maximum(m_prev, jnp.max(s, axis=-1, keepdims=True))
                p = jnp.exp(s - m_new)
                alpha = jnp.exp(m_prev - m_new)
                l_sc[h, rows, :] = alpha * l_sc[h, rows, :] + jnp.sum(p, axis=-1, keepdims=True)
                m_sc[h, rows, :] = m_new
                p_hi = p.astype(BF16)
                pairs.append((p_hi, (p - p_hi.astype(F32)).astype(BF16)))
                alphas.append(alpha)
            for u, (h, rows, _) in enumerate(units):
                vv = v_ref[h]
                acc_sc[h, rows, :] = (alphas[u] * acc_sc[h, rows, :]
                                      + jnp.dot(pairs[u][0], vv, preferred_element_type=F32)
                                      + jnp.dot(pairs[u][1], vv, preferred_element_type=F32))

        @pl.when(j < i)
        def _():
            update(False)

        @pl.when(j == i)
        def _():
            update(True)

        @pl.when(j == nb - 1)
        def _():
            o_ref[...] = acc_sc[...] / l_sc[...]
            lane0 = lax.broadcasted_iota(jnp.int32, (t, LANES), 1) == 0
            for h in range(hb):
                lse_col = m_sc[h] + jnp.log(l_sc[h])
                lse_ref[h] = _lane_sums_as_row(jnp.where(lane0, lse_col, 0.0))

        host.phase("finish", (g == ng - 1) & (i == nb - 1) & (j == nb - 1), crefs)

    qspec = pl.BlockSpec((hb, t, dh), lambda g, i, j: (g, i, 0))
    kspec = pl.BlockSpec((hb, t, dh), lambda g, i, j: (g, jnp.minimum(j, i), 0))
    in_specs, out_specs, out_shape, scratch, extra = host.specs(
        [qspec, kspec, kspec, pl.BlockSpec((hb, 1, t), lambda g, i, j: (g, 0, jnp.minimum(j, i)))],
        [qspec, pl.BlockSpec((hb, 1, t), lambda g, i, j: (g, 0, i))],
        [jax.ShapeDtypeStruct((h_n, n, dh), F32), jax.ShapeDtypeStruct((h_n, 1, n), F32)],
        [pltpu.VMEM((hb, t, 1), F32), pltpu.VMEM((hb, t, 1), F32), pltpu.VMEM((hb, t, dh), F32)])
    return pl.pallas_call(
        body, name=name, grid=(ng, nb, nb),
        in_specs=in_specs, out_specs=out_specs, out_shape=out_shape, scratch_shapes=scratch,
        compiler_params=_params(("arbitrary", "arbitrary", "arbitrary")),
    )(q, k, v, crow, *extra)


def attn_delta(do, o, *, name, t=512, hb=8):
    h_n, n, dh = do.shape
    hb = min(hb, h_n)
    t = _pick(n, t)

    def body(do_ref, o_ref, dl_ref):
        for h in range(hb):
            dl_ref[h] = _lane_sums_as_row(do_ref[h].astype(F32) * o_ref[h])

    spec = pl.BlockSpec((hb, t, dh), lambda g, i: (g, i, 0))
    return pl.pallas_call(
        body, name=name, grid=(h_n // hb, n // t),
        in_specs=[spec, spec], out_specs=pl.BlockSpec((hb, 1, t), lambda g, i: (g, 0, i)),
        out_shape=jax.ShapeDtypeStruct((h_n, 1, n), F32),
        compiler_params=_params(("parallel", "parallel")),
    )(do, o)


def attn_bwd(q, k, v, do, o, lse, cum, *, scale, tag, t=512, hb=2, comm=None):
    h_n, n, dh = q.shape
    hb = min(hb, h_n)
    t = _pick(n, t)
    nb = n // t
    dob = do.astype(BF16)
    delta = attn_delta(dob, o, name=f"attn_delta_{tag}", t=t)

    ng = h_n // hb
    host = _Hosted(comm, 7, 4, 3)

    def body(*refs):
        (q_ref, k_ref, v_ref, do_ref, lse_ref, dl_ref, cum_ref,
         dq_ref, dk_ref, dv_ref, dcum_ref, dk_acc, dv_acc, dcc_acc), crefs = host.split(refs)
        g, j, i = pl.program_id(0), pl.program_id(1), pl.program_id(2)
        host.phase("start", (g == 0) & (j == 0) & (i == 0), crefs)
        lane = lax.broadcasted_iota(jnp.int32, (t, LANES), 1)

        @pl.when((j == 0) & (i == 0))
        def _():
            dq_ref[...] = jnp.zeros_like(dq_ref)

        @pl.when(i == 0)
        def _():
            dk_acc[...] = jnp.zeros_like(dk_acc)
            dv_acc[...] = jnp.zeros_like(dv_acc)
            dcc_acc[...] = jnp.zeros_like(dcc_acc)

        def update(diagonal):
            heads = range(hb)
            r0 = pl.multiple_of(i * t, t)
            if diagonal:
                keep = lax.broadcasted_iota(jnp.int32, (t, t), 0) <= lax.broadcasted_iota(jnp.int32, (t, t), 1)
            qv, kv = [q_ref[h] for h in heads], [k_ref[h] for h in heads]
            vv, dov = [v_ref[h] for h in heads], [do_ref[h] for h in heads]
            st = [lax.dot_general(kv[h], qv[h], _NT, preferred_element_type=F32) for h in heads]
            dpt = [lax.dot_general(vv[h], dov[h], _NT, preferred_element_type=F32) for h in heads]
            pt = []
            cum_tile = cum_ref[...]
            for h in heads:
                cc = jnp.sum(jnp.where(lane == g * hb + h, cum_tile, 0.0), axis=1, keepdims=True)
                s = st[h] - cc
                if diagonal:
                    s = jnp.where(keep, s, -jnp.inf)
                pt.append(jnp.exp(s - lse_ref[h]))
            for h in heads:
                dv_acc[h] += jnp.dot(pt[h].astype(BF16), dov[h], preferred_element_type=F32)
            dsb = []
            for h in heads:
                ds = pt[h] * (dpt[h] - dl_ref[h])
                dcc_acc[h] -= jnp.sum(ds, axis=1, keepdims=True)
                dsb.append(ds.astype(BF16))
            for h in heads:
                dk_acc[h] += jnp.dot(dsb[h], qv[h], preferred_element_type=F32)
            for h in heads:
                dq_ref[h, pl.ds(r0, t), :] += lax.dot_general(dsb[h], kv[h], _TN,
                                                              preferred_element_type=F32) * scale

        @pl.when(i > j)
        def _():
            update(False)

        @pl.when(i == j)
        def _():
            update(True)

        @pl.when(i == nb - 1)
        def _():
            dk_ref[...] = dk_acc[...]
            dv_ref[...] = dv_acc[...]
            tile = jnp.zeros((t, LANES), F32)
            for h in range(hb):
                tile = tile + jnp.where(lane == g * hb + h, dcc_acc[h], 0.0)
            dcum_ref[...] = tile

        host.phase("finish", (g == ng - 1) & (j == nb - 1) & (i == nb - 1), crefs)

    qspec = pl.BlockSpec((hb, t, dh), lambda g, j, i: (g, jnp.maximum(i, j), 0))
    qrow = pl.BlockSpec((hb, 1, t), lambda g, j, i: (g, 0, jnp.maximum(i, j)))
    kspec = pl.BlockSpec((hb, t, dh), lambda g, j, i: (g, j, 0))
    in_specs, out_specs, out_shape, scratch, extra = host.specs(
        [qspec, kspec, kspec, qspec, qrow, qrow, pl.BlockSpec((t, LANES), lambda g, j, i: (j, 0))],
        [pl.BlockSpec((hb, n, dh), lambda g, j, i: (g, 0, 0)), kspec, kspec,
         pl.BlockSpec((None, t, LANES), lambda g, j, i: (g, j, 0))],
        [jax.ShapeDtypeStruct((h_n, n, dh), F32)] * 3 + [jax.ShapeDtypeStruct((ng, n, LANES), F32)],
        [pltpu.VMEM((hb, t, dh), F32), pltpu.VMEM((hb, t, dh), F32), pltpu.VMEM((hb, t, 1), F32)])
    dq, dk, dv, dcum, *arrived = pl.pallas_call(
        body, name=f"attn_bwd_{tag}", grid=(ng, nb, nb),
        in_specs=in_specs, out_specs=out_specs, out_shape=out_shape, scratch_shapes=scratch,
        compiler_params=_params(("arbitrary", "arbitrary", "arbitrary")),
    )(q, k, v, dob, lse, delta, cum, *extra)
    return [dq, dk, dv, dcum] + arrived


SCAN_STEPS = (1, 2, 4)


def ssm_scan(x, tab, *, reverse, name, s_prev=None, tt=512, out_dtype=BF16):
    n, width = x.shape
    nc, _, hw = tab.shape
    cw = 2 * hw
    assert width == nc * cw
    tt = _pick(n, tt, 2 * SUBLANES)
    nt = n // tt
    ng = tt // (2 * SUBLANES)
    with_grad = s_prev is not None

    def body(*refs):
        if with_grad:
            x_ref, s_ref, tab_ref, o_ref, g_ref, car_ref = refs
        else:
            x_ref, tab_ref, o_ref, car_ref = refs

        @pl.when(pl.program_id(1) == 0)
        def _():
            car_ref[...] = jnp.zeros_like(car_ref)
            if with_grad:
                g_ref[...] = jnp.zeros_like(g_ref)

        q_re, q_im = tab_ref[0:8, :], tab_ref[8:16, :]
        p_re = [tab_ref[16 + 16 * i:24 + 16 * i, :] for i in range(3)]
        p_im = [tab_ref[24 + 16 * i:32 + 16 * i, :] for i in range(3)]
        row = lax.broadcasted_iota(jnp.int32, (SUBLANES, hw), 0)

        def group(xr, xi, sr_, si_, carry):
            c_re, c_im = carry
            for i, d in enumerate(SCAN_STEPS):
                shift = SUBLANES - d if reverse else d
                sr, si = pltpu.roll(xr, shift, 0), pltpu.roll(xi, shift, 0)
                xr, xi = (xr + p_re[i] * sr - p_im[i] * si,
                          xi + p_re[i] * si + p_im[i] * sr)
            xr, xi = (xr + q_re * c_re - q_im * c_im,
                      xi + q_re * c_im + q_im * c_re)
            if with_grad:
                nr = jnp.where(row < SUBLANES - 1, pltpu.roll(xr, SUBLANES - 1, 0), c_re)
                ni = jnp.where(row < SUBLANES - 1, pltpu.roll(xi, SUBLANES - 1, 0), c_im)
                g_ref[:, 0:hw] += nr * sr_ + ni * si_
                g_ref[:, hw:cw] += ni * sr_ - nr * si_
            if reverse:
                return xr, xi, (xr[0:1, :], xi[0:1, :])
            return xr, xi, (xr[SUBLANES - 1:SUBLANES, :], xi[SUBLANES - 1:SUBLANES, :])

        def pair(gi, carry):
            g = (ng - 1 - gi) if reverse else gi
            r0 = pl.multiple_of(g * 2 * SUBLANES, 2 * SUBLANES)
            rows = pl.ds(r0, 2 * SUBLANES)
            xr, xi = x_ref[rows, 0:hw].astype(F32), x_ref[rows, hw:cw].astype(F32)
            if with_grad:
                sr, si = s_ref[rows, 0:hw].astype(F32), s_ref[rows, hw:cw].astype(F32)
            halves = [slice(0, SUBLANES), slice(SUBLANES, 2 * SUBLANES)]
            done = [None, None]
            for k in ((1, 0) if reverse else (0, 1)):
                h = halves[k]
                o_re, o_im, carry = group(xr[h], xi[h], sr[h] if with_grad else None,
                                          si[h] if with_grad else None, carry)
                done[k] = (o_re, o_im)
            o_ref[rows, 0:hw] = jnp.concatenate([done[0][0], done[1][0]], axis=0).astype(o_ref.dtype)
            o_ref[rows, hw:cw] = jnp.concatenate([done[0][1], done[1][1]], axis=0).astype(o_ref.dtype)
            return carry

        c_re, c_im = lax.fori_loop(0, ng, pair, (car_ref[0:1, 0:hw], car_ref[0:1, hw:cw]),
                                   unroll=min(ng, 2))
        car_ref[0:1, 0:hw] = c_re
        car_ref[0:1, hw:cw] = c_im

    if reverse:
        xspec = pl.BlockSpec((tt, cw), lambda c, t: (nt - 1 - t, c))
    else:
        xspec = pl.BlockSpec((tt, cw), lambda c, t: (t, c))
    tspec = pl.BlockSpec((None, 64, hw), lambda c, t: (c, 0, 0))
    in_specs = [xspec, xspec, tspec] if with_grad else [xspec, tspec]
    out_specs = [xspec]
    out_shape = [jax.ShapeDtypeStruct((n, width), out_dtype)]
    if with_grad:
        out_specs.append(pl.BlockSpec((None, SUBLANES, cw), lambda c, t: (c, 0, 0)))
        out_shape.append(jax.ShapeDtypeStruct((nc, SUBLANES, cw), F32))
    operands = (x, s_prev, tab) if with_grad else (x, tab)
    return pl.pallas_call(
        body, name=name, grid=(nc, nt),
        in_specs=in_specs, out_specs=out_specs, out_shape=out_shape,
        scratch_shapes=[pltpu.VMEM((SUBLANES, cw), F32)],
        compiler_params=_params(("parallel", "arbitrary")),
    )(*operands)


def _slot(pos):
    return 4 * pos[0] + 2 * pos[1] + pos[2]


def _comm_scratch(n):
    return [pltpu.SemaphoreType.DMA((7 * n,)), pltpu.SemaphoreType.DMA((7 * n,)), pltpu.SemaphoreType.DMA((n,))]


def _gather_copies(ins, outs, sems):
    send_sems, recv_sems, local_sems = sems
    n = len(ins)
    x, y, c = lax.axis_index("x"), lax.axis_index("y"), lax.axis_index("c")
    me, sibling = (x, y, c), (x, y, 1 - c)
    chips = [(1 - x, y), (x, 1 - y), (1 - x, 1 - y)]

    def copy(t, k, block, to, src=None):
        dst = outs[t].at[_slot(block)]
        return pltpu.make_async_remote_copy(
            src_ref=dst if src is None else src, dst_ref=dst,
            send_sem=send_sems.at[7 * t + k], recv_sem=recv_sems.at[7 * t + k],
            device_id=to, device_id_type=MESH)

    jc = list(enumerate(chips))
    return dict(
        mine=[pltpu.make_async_copy(ins[t], outs[t].at[_slot(me)], local_sems.at[t]) for t in range(n)],
        first=[cp for t in range(n) for cp in
               [copy(t, 0, me, sibling, src=ins[t])] + [copy(t, 1 + j, me, (*chip, c), src=ins[t]) for j, chip in jc]],
        arrive=[copy(t, 1 + j, (*chip, c), me) for t in range(n) for j, chip in jc],
        passed=[copy(t, 4 + j, (*chip, c), sibling) for t in range(n) for j, chip in jc],
        from_sibling=[cp for t in range(n) for cp in
                      [copy(t, 0, sibling, me)] + [copy(t, 4 + j, (*chip, 1 - c), me) for j, chip in jc]])


def _gather_start(ins, outs, sems):
    cps = _gather_copies(ins, outs, sems)
    for cp in cps["mine"] + cps["first"]:
        cp.start()


def _gather_forward(ins, outs, sems):
    cps = _gather_copies(ins, outs, sems)
    for arrived, onward in zip(cps["arrive"], cps["passed"]):
        arrived.wait_recv()
        onward.start()


def _gather_finish(ins, outs, sems):
    cps = _gather_copies(ins, outs, sems)
    for cp in cps["from_sibling"]:
        cp.wait_recv()
    for cp in cps["first"] + cps["passed"]:
        cp.wait_send()
    for cp in cps["mine"]:
        cp.wait()


def gather_comm(arrs):
    return dict(ins=list(arrs), out_shape=[jax.ShapeDtypeStruct((N_DEV,) + a.shape, a.dtype) for a in arrs],
                scratch=_comm_scratch(len(arrs)), start=_gather_start, mid=_gather_forward, finish=_gather_finish)


def _exchange_copies(ins, outs, sems):
    send_sems, recv_sems, local_sems = sems
    n = len(ins)
    me = (lax.axis_index("x"), lax.axis_index("y"), lax.axis_index("c"))
    peers = []
    for k in range(1, N_DEV):
        flip = ((k >> 2) & 1, (k >> 1) & 1, k & 1)
        peers.append(tuple(1 - p if f else p for p, f in zip(me, flip)))

    def copy(t, k, peer, dst_slot):
        return pltpu.make_async_remote_copy(
            src_ref=ins[t].at[_slot(peer)], dst_ref=outs[t].at[dst_slot],
            send_sem=send_sems.at[7 * t + k], recv_sem=recv_sems.at[7 * t + k],
            device_id=peer, device_id_type=MESH)

    return dict(
        mine=[pltpu.make_async_copy(ins[t].at[_slot(me)], outs[t].at[_slot(me)], local_sems.at[t])
              for t in range(n)],
        send=[copy(t, k, peer, _slot(me)) for t in range(n) for k, peer in enumerate(peers)],
        both=[copy(t, k, peer, _slot(peer)) for t in range(n) for k, peer in enumerate(peers)])


def _exchange_start(ins, outs, sems):
    cps = _exchange_copies(ins, outs, sems)
    for cp in cps["mine"] + cps["send"]:
        cp.start()


def _exchange_finish(ins, outs, sems):
    cps = _exchange_copies(ins, outs, sems)
    for cp in cps["both"]:
        cp.wait()
    for cp in cps["mine"]:
        cp.wait()


def exchange_comm(arrs):
    return dict(ins=list(arrs), out_shape=[jax.ShapeDtypeStruct(a.shape, a.dtype) for a in arrs],
                scratch=_comm_scratch(len(arrs)), start=_exchange_start, mid=None, finish=_exchange_finish)


def run_comm(comm, *, name):
    n_in, n_out = len(comm["ins"]), len(comm["out_shape"])

    def body(*refs):
        ins, outs, sems = refs[:n_in], refs[n_in:n_in + n_out], refs[n_in + n_out:]
        comm["start"](ins, outs, sems)
        if comm["mid"] is not None:
            comm["mid"](ins, outs, sems)
        comm["finish"](ins, outs, sems)

    any_spec = pl.BlockSpec(memory_space=pl.ANY)
    return pl.pallas_call(
        body, name=name, in_specs=[any_spec] * n_in, out_specs=[any_spec] * n_out,
        out_shape=comm["out_shape"], scratch_shapes=comm["scratch"],
    )(*comm["ins"])


class _Hosted:
    def __init__(self, comm, n_in, n_out, n_scratch):
        self.comm = comm
        self.n_ci = len(comm["ins"]) if comm else 0
        self.n_co = len(comm["out_shape"]) if comm else 0
        self.n_in, self.n_out, self.n_scratch = n_in, n_out, n_scratch

    def split(self, refs):
        a = self.n_in
        b = a + self.n_ci
        c = b + self.n_out
        e = c + self.n_co
        f = e + self.n_scratch
        return refs[:a] + refs[b:c] + refs[e:f], (refs[a:b], refs[c:e], refs[f:])

    def phase(self, which, when, crefs):
        fn = self.comm[which] if self.comm else None
        if fn is not None:
            pl.when(when)(lambda: fn(*crefs))

    def specs(self, in_specs, out_specs, out_shape, scratch):
        any_spec = pl.BlockSpec(memory_space=pl.ANY)
        if not self.comm:
            return in_specs, out_specs, out_shape, scratch, ()
        return (in_specs + [any_spec] * self.n_ci, out_specs + [any_spec] * self.n_co,
                out_shape + self.comm["out_shape"], scratch + self.comm["scratch"], tuple(self.comm["ins"]))


def all_gather(arrs, *, name):
    return run_comm(gather_comm(arrs), name=name)


def _discretise(a_re, a_im, log_dt, b_re, b_im):
    ar = jnp.minimum(a_re, -1e-4)
    dt = jnp.exp(log_dt)[:, None]
    e, ph = ar * dt, a_im * dt
    mag = jnp.exp(e)
    lr, li = mag * jnp.cos(ph), mag * jnp.sin(ph)
    den = ar * ar + a_im * a_im
    nr, ni = lr - 1.0, li
    cr = (nr * ar + ni * a_im) / den
    ci = (ni * ar - nr * a_im) / den
    bb_re = cr[..., None] * b_re - ci[..., None] * b_im
    bb_im = cr[..., None] * b_im + ci[..., None] * b_re
    return e, ph, bb_re, bb_im


def _lam_pow(e, ph, k, conj):
    mag = jnp.exp(k * e)
    return mag * jnp.cos(k * ph), (-1.0 if conj else 1.0) * mag * jnp.sin(k * ph)


def _scan_table(e, ph, nc, reverse):
    hw = e.size // nc
    e, ph = e.reshape(nc, 1, hw), ph.reshape(nc, 1, hw)
    j = jnp.arange(SUBLANES, dtype=F32).reshape(1, SUBLANES, 1)
    kq = (SUBLANES - j) if reverse else (j + 1.0)
    parts = list(_lam_pow(e, ph, kq, reverse))
    for step in SCAN_STEPS:
        inside = (j < SUBLANES - step) if reverse else (j >= step)
        p_re, p_im = _lam_pow(e, ph, jnp.full_like(j, float(step)), reverse)
        parts += [jnp.where(inside, p_re, 0.0), jnp.where(inside, p_im, 0.0)]
    return jnp.concatenate(parts, axis=1)


def _blockdiag(m, nc):
    g, a, b = m.shape
    gc = g // nc
    m = m.reshape(nc, gc, a, b)
    eye = jnp.eye(gc, dtype=m.dtype)
    return jnp.einsum("cgab,gh->cgahb", m, eye).reshape(nc, gc * a, gc * b)


def _blockdiag_take(m, g):
    nc = m.shape[0]
    gc = g // nc
    a, b = m.shape[1] // gc, m.shape[2] // gc
    blocks = [m[:, i * a:(i + 1) * a, i * b:(i + 1) * b] for i in range(gc)]
    return jnp.stack(blocks, axis=1).reshape(g, a, b)


def kernel(x, c, mod_w, mod_b, norm_pre, norm_post, ffn_w_in, ffn_w_out, mix_w_in, forget_b, ssm_a_re, ssm_a_im, ssm_log_dt, ssm_b_re, ssm_b_im, ssm_c_re, ssm_c_im, ssm_d, glu_w, attn_w_out, mix_w_out, loss_target, m_mod_w, m_mod_b, m_norm_pre, m_norm_post, m_ffn_w_in, m_ffn_w_out, m_mix_w_in, m_forget_b, m_ssm_a_re, m_ssm_a_im, m_ssm_log_dt, m_ssm_b_re, m_ssm_b_im, m_ssm_c_re, m_ssm_c_im, m_ssm_d, m_glu_w, m_attn_w_out, m_mix_w_out, v_mod_w, v_mod_b, v_norm_pre, v_norm_post, v_ffn_w_in, v_ffn_w_out, v_mix_w_in, v_forget_b, v_ssm_a_re, v_ssm_a_im, v_ssm_log_dt, v_ssm_b_re, v_ssm_b_im, v_ssm_c_re, v_ssm_c_im, v_ssm_d, v_glu_w, v_attn_w_out, v_mix_w_out):
    names = ["mod_w", "mod_b", "norm_pre", "norm_post", "ffn_w_in", "ffn_w_out", "mix_w_in", "forget_b",
             "ssm_a_re", "ssm_a_im", "ssm_log_dt", "ssm_b_re", "ssm_b_im", "ssm_c_re", "ssm_c_im", "ssm_d",
             "glu_w", "attn_w_out", "mix_w_out"]
    w_in = dict(zip(names, [mod_w, mod_b, norm_pre, norm_post, ffn_w_in, ffn_w_out, mix_w_in, forget_b,
                            ssm_a_re, ssm_a_im, ssm_log_dt, ssm_b_re, ssm_b_im, ssm_c_re, ssm_c_im, ssm_d,
                            glu_w, attn_w_out, mix_w_out]))
    m_in = dict(zip(names, [m_mod_w, m_mod_b, m_norm_pre, m_norm_post, m_ffn_w_in, m_ffn_w_out, m_mix_w_in,
                            m_forget_b, m_ssm_a_re, m_ssm_a_im, m_ssm_log_dt, m_ssm_b_re, m_ssm_b_im,
                            m_ssm_c_re, m_ssm_c_im, m_ssm_d, m_glu_w, m_attn_w_out, m_mix_w_out]))
    v_in = dict(zip(names, [v_mod_w, v_mod_b, v_norm_pre, v_norm_post, v_ffn_w_in, v_ffn_w_out, v_mix_w_in,
                            v_forget_b, v_ssm_a_re, v_ssm_a_im, v_ssm_log_dt, v_ssm_b_re, v_ssm_b_im,
                            v_ssm_c_re, v_ssm_c_im, v_ssm_d, v_glu_w, v_attn_w_out, v_mix_w_out]))

    depth = mod_w.shape[0]
    n_tok, d = x.shape[1], x.shape[2]
    ff = ffn_w_out.shape[2] * N_DEV
    heads = forget_b.shape[1]
    sw = ssm_d.shape[1]
    g_n, p_n, n_n = ssm_b_re.shape[1:]
    aw = attn_w_out.shape[1]
    dh = aw // heads
    iw = mix_w_in.shape[2] * N_DEV
    nc = sw // LANES
    hw = g_n * p_n // nc
    mod_cols = mod_w.shape[2]
    scale = dh ** -0.5
    assert iw == sw + 3 * aw + heads + 2 * d and heads <= LANES
    assert math.log2(scale).is_integer(), "q is pre-scaled in bf16: exact only for a power of two"
    off_u, off_q, off_f = 2 * d, 2 * d + sw, 2 * d + sw + 3 * aw
    iwp = off_f + LANES
    assert off_u % sw == 0 and off_q % aw == 0 and off_f % LANES == 0

    me = 4 * lax.axis_index("x") + 2 * lax.axis_index("y") + lax.axis_index("c")
    x2 = x.reshape(n_tok, d)
    tgt = loss_target.reshape(n_tok, d)

    silu_c = rowwise(_f_silu, [c], [], [((d,), F32)], name="silu_c")[0]
    big = ["ffn_w_in", "ffn_w_out", "mix_w_in", "glu_w", "attn_w_out", "mix_w_out"]
    ffn1 = [("ffn_w_in", 0), ("ffn_w_out", 0)]
    mix = [("mix_w_in", None), ("glu_w", None), ("attn_w_out", None), ("mix_w_out", None)]
    ffn2 = [("ffn_w_in", 1), ("ffn_w_out", 1)]

    def riders(l):
        nxt = [(l + 1, p) for p in ffn1 + mix] if l + 1 < depth else []
        return [(l, p) for p in ffn2] + nxt

    def piece_of(dct, piece, l):
        name, j = piece
        return dct[name][l] if j is None else dct[name][l][j]

    row_sharded = ("ffn_w_out", "mix_w_out")

    def send_shard(p, l):
        a = piece_of(w_in, p, l).astype(BF16)
        return a if p[0] in row_sharded else a.T

    def shards(l, pieces):
        return [send_shard(p, l) for p in pieces]

    cut = [0, sw, sw + aw, sw + 2 * aw, sw + 3 * aw, sw + 3 * aw + heads, sw + 3 * aw + heads + d, iw]
    lw = [dict(win_t=[None, None], wout=[None, None]) for _ in range(depth)]

    def install(l, pieces, gathered):
        for (name, j), g in zip(pieces, gathered):
            whole = g.reshape(-1, g.shape[-1])
            if name == "ffn_w_in":
                lw[l]["win_t"][j] = whole
            elif name == "ffn_w_out":
                lw[l]["wout"][j] = whole
            elif name == "mix_w_in":
                seg = lambda i: whole[cut[i]:cut[i + 1]]
                lw[l]["wmi_t"] = jnp.concatenate([seg(5), seg(6), seg(0), seg(1), seg(2), seg(3),
                                                  jnp.pad(seg(4), ((0, LANES - heads), (0, 0)))], axis=0)
            elif name == "mix_w_out":
                lw[l]["mo"] = whole
            else:
                lw[l]["glu_t" if name == "glu_w" else "ao_t"] = whole

    gathered = all_gather(
        [silu_c, norm_pre.reshape(-1, norm_pre.shape[-1]), norm_post.reshape(-1, norm_post.shape[-1])]
        + shards(0, ffn1[:1]), name="gather_first")
    sc_all = gathered[0].reshape(N_DEV, d)
    gpre = jnp.moveaxis(gathered[1].reshape(N_DEV, depth, 3, -1), 0, 2).reshape(depth, 3, d)
    gpost = jnp.moveaxis(gathered[2].reshape(N_DEV, depth, 3, -1), 0, 2).reshape(depth, 3, d)
    install(0, ffn1[:1], gathered[3:])

    sc_pad = jnp.pad(sc_all, ((0, LANES - N_DEV), (0, 0)))
    mod_part = jnp.stack([mm(sc_pad, mod_w[l], name=f"mod_fwd{l}")[:N_DEV] for l in range(depth)], axis=1)
    mod_part = mod_part + lax.dynamic_slice_in_dim(mod_b, me * mod_cols, mod_cols, axis=1)[None]
    mod_all = all_gather([mod_part], name="gather_mod")[0]
    mod_own = lax.dynamic_index_in_dim(mod_all, me, axis=1, keepdims=False)
    mod_own = mod_own.transpose(1, 0, 2).reshape(depth, 3, 3, d)
    res_w = (FFN_RES, 1.0, FFN_RES)

    def vec_a(l, i):
        return (gpre[l, i] * (1.0 + mod_own[l, i, 1])).reshape(1, d)

    def vec_sh(l, i):
        return mod_own[l, i, 0].reshape(1, d)

    def vec_b(l, i):
        return (res_w[i] * mod_own[l, i, 2] * gpost[l, i]).reshape(1, d)

    ssm = []
    for l in range(depth):
        (e, ph, bb_re, bb_im), disc_vjp = jax.vjp(_discretise,ssm_a_re[l], ssm_a_im[l], ssm_log_dt[l],
                                                  ssm_b_re[l], ssm_b_im[l])
        b_mat = jnp.concatenate([_blockdiag(bb_re.transpose(0, 2, 1), nc),
                                 _blockdiag(bb_im.transpose(0, 2, 1), nc)], axis=2)
        c_mat = jnp.concatenate([_blockdiag(ssm_c_re[l].transpose(0, 2, 1), nc),
                                 _blockdiag(-ssm_c_im[l].transpose(0, 2, 1), nc)], axis=1)
        ssm.append(dict(e=e, ph=ph, vjp=disc_vjp, b=b_mat.astype(BF16), c=c_mat.astype(BF16),
                        bt=b_mat.transpose(0, 2, 1).astype(BF16), ct=c_mat.transpose(0, 2, 1).astype(BF16),
                        tab_f=_scan_table(e, ph, nc, False), tab_r=_scan_table(e, ph, nc, True),
                        dvec=ssm_d[l].reshape(1, sw)))

    fb_pad = jnp.pad(forget_b, ((0, 0), (0, LANES - heads)))

    def heads_first(a):
        return a.reshape(n_tok, heads, dh).transpose(1, 0, 2)

    def heads_last(a):
        return a.transpose(1, 0, 2).reshape(n_tok, heads * dh)

    def mm_hosting(a, b, comm, **kw):
        if comm is None:
            return mm(a, b, **kw), []
        out, *arrived = mm(a, b, comm=comm, **kw)
        return out, arrived

    def ffn_fwd(xin, h, l, j, tag, comm=None, on_arrival=None):
        ag, au, m, *arrived = ffn_in_swiglu(h, lw[l]["win_t"][j], name=f"ffn_in_{tag}", comm=comm)
        if on_arrival is not None:
            on_arrival(arrived)
        y = mm(m, lw[l]["wout"][j], name=f"ffn_out_{tag}")
        return y, dict(x=xin, h=h, ag=ag, au=au, m=m, y=y)

    def mixer_fwd(xin, h, l, tag):
        s5 = ssm[l]
        proj = mm(h, lw[l]["wmi_t"], trans_b=True, name=f"mix_in_{tag}", out_dtype=BF16)
        projf = mm(h, lw[l]["wmi_t"][off_f:], trans_b=True, name=f"mix_in_f_{tag}")
        bu = mm_blockdiag(proj, s5["b"], a_cb0=off_u // LANES, name=f"ssm_bu_{tag}", out_dtype=BF16)
        st = ssm_scan(bu, s5["tab_f"], reverse=False, name=f"ssm_scan_{tag}")[0]
        y0 = mm_blockdiag(st, s5["c"], name=f"ssm_y_{tag}")
        ge, ys = rowwise(_f_gelu_in, [y0, (proj, sw, off_u // sw)], [s5["dvec"]],
                         [((sw,), BF16), ((sw,), F32)], name=f"gelu_{tag}")
        z = mm(ge, lw[l]["glu_t"], trans_b=True, name=f"glu_{tag}", out_dtype=BF16)
        cum = cum_fwd(projf, fb_pad[l:l + 1], col_block=0, name=f"cum_{tag}")
        crow = cum[:, :heads].T[:, None, :]
        q, k, v = [heads_first(proj[:, off_q + i * aw:off_q + (i + 1) * aw] * sc_).astype(BF16)
                   for i, sc_ in enumerate((scale, 1.0, 1.0))]
        nxt = gather_comm([send_shard(p, ll) for ll, p in riders(l)])
        o, lse, *arrived = attn_fwd(q, k, v, crow, name=f"attn_{tag}", comm=nxt)
        for (ll, p), got in zip(riders(l), arrived):
            install(ll, [p], [got])
        attn = heads_last(o).astype(BF16)
        yb = mm(attn, lw[l]["ao_t"], trans_b=True, name=f"attn_out_{tag}", out_dtype=BF16)
        mg = rowwise(_f_merge, [(z, d, 0), (z, d, 1), yb, (proj, d, 0), (proj, d, 1)], [],
                     [((d,), BF16)], name=f"merge_{tag}")[0]
        y = mm(mg, lw[l]["mo"], name=f"mix_out_{tag}")
        saved = dict(x=xin, h=h, proj=proj, projf=projf, st=st, ys=ys, ge=ge, z=z, q=q, k=k, v=v, o=o, lse=lse,
                     cum=cum, attn=attn, yb=yb, mg=mg, y=y)
        return y, saved

    subs = [(l, i) for l in range(depth) for i in range(3)]
    tag_of = lambda l, i: f"l{l}" + "amb"[i]
    saved = [[None] * 3 for _ in range(depth)]
    xc = x2
    h = rowwise(_f_pre, [xc], [vec_a(0, 0), vec_sh(0, 0)], [((d,), BF16)], name="pre_first")[0]
    for s, (l, i) in enumerate(subs):
        tag = tag_of(l, i)
        if i == 1:
            y, saved[l][i] = mixer_fwd(xc, h, l, tag)
        else:
            if (l, i) == (0, 0):
                late = ffn1[1:] + mix
                y, saved[l][i] = ffn_fwd(xc, h, l, 0, tag, gather_comm(shards(0, late)),
                                         lambda got: install(0, late, got))
            else:
                y, saved[l][i] = ffn_fwd(xc, h, l, i // 2, tag)
        if s + 1 < len(subs):
            l2, i2 = subs[s + 1]
            xc, h = rowwise(_f_post_pre, [xc, y], [vec_b(l, i), vec_a(l2, i2), vec_sh(l2, i2)],
                            [((d,), F32), ((d,), BF16)], name=f"post_pre_{tag}")

    def f_tail(xf, yl, t, bv):
        e_ = _f_post_add(xf, yl, bv) - t
        dxo = e_ * (1.0 / d)
        dyl, d_bv_ = _f_post_bwd(dxo, yl, bv)
        return dxo, dyl, _colsum(e_ * e_), d_bv_

    l_last, i_last = subs[-1]
    dx, dy, sq, d_bv = rowwise(f_tail, [xc, y, tgt], [vec_b(l_last, i_last)], [((d,), F32), ((d,), BF16)], [d, d],
                               name="loss_tail")
    loss_part = 0.5 * jnp.sum(sq) / d

    grads = {k: [None] * depth for k in big}
    small_g = [dict() for _ in range(depth)]
    dmod = [[None] * 3 for _ in range(depth)]
    dgpre = [[None] * 3 for _ in range(depth)]
    dgpost = [[None] * 3 for _ in range(depth)]

    def norm_grads(l, i, d_a, d_sh, d_bv):
        d_a, d_sh, d_bv = d_a.reshape(d), d_sh.reshape(d), d_bv.reshape(d)
        dmod[l][i] = jnp.stack([d_sh, d_a * gpre[l, i], res_w[i] * gpost[l, i] * d_bv])
        dgpre[l][i] = d_a * (1.0 + mod_own[l, i, 1])
        dgpost[l][i] = res_w[i] * mod_own[l, i, 2] * d_bv

    def ffn_bwd(dy, sv, l, j, tag, comm_dw=None, comm_dx_of=None):
        dm = mm(dy, lw[l]["wout"][j], trans_b=True, name=f"ffn_out_dx_{tag}", out_dtype=BF16)
        g_out = mm(sv["m"], dy, trans_a=True, name=f"ffn_out_dw_{tag}", out_dtype=BF16, tm=1408, tn=1024)
        da = rowwise(_f_swiglu_bwd, [sv["ag"], sv["au"], dm], [], [((ff, ff), BF16)],
                     name=f"swiglu_bwd_{tag}")[0]
        g_in, arrived_dw = mm_hosting(sv["h"], da, comm_dw, trans_a=True, name=f"ffn_in_dw_{tag}",
                                      out_dtype=BF16)
        comm_dx = comm_dx_of(g_in, g_out) if comm_dx_of is not None else None
        dh_, arrived_dx = mm_hosting(da, lw[l]["win_t"][j], comm_dx, name=f"ffn_in_dx_{tag}")
        return dh_, g_in, g_out, arrived_dw, arrived_dx

    def mixer_bwd(dy, sv, l, tag, comm):
        s5 = ssm[l]
        proj = sv["proj"]
        dmg = mm(dy, lw[l]["mo"], trans_b=True, name=f"mix_out_dx_{tag}", out_dtype=BF16)
        g_mo = mm(sv["mg"], dy, trans_a=True, name=f"mix_out_dw_{tag}", out_dtype=BF16)
        dz, dyb, dproj = rowwise(
            _f_merge_bwd, [dmg, (sv["z"], d, 0), (sv["z"], d, 1), sv["yb"], (proj, d, 0), (proj, d, 1)], [],
            [((d, d), BF16), ((d,), BF16), ((d, d), BF16, (iwp, 0))], name=f"merge_bwd_{tag}")
        dge = mm(dz, lw[l]["glu_t"], name=f"glu_dx_{tag}")
        g_glu = mm(sv["ge"], dz, trans_a=True, name=f"glu_dw_{tag}", out_dtype=BF16)
        dys, d_dvec = rowwise(_f_gelu_bwd, [dge, sv["ys"], (proj, sw, off_u // sw)], [], [((sw,), BF16)],
                              [sw], name=f"gelu_bwd_{tag}")
        gadj = mm_blockdiag(dys, s5["ct"], name=f"ssm_dy_{tag}", out_dtype=BF16)
        adj, dlam8 = ssm_scan(gadj, s5["tab_r"], reverse=True, s_prev=sv["st"], name=f"ssm_scan_bwd_{tag}")
        du0 = mm_blockdiag(adj, s5["bt"], name=f"ssm_du_{tag}")
        d_bmat = mm_blockdiag_tn(proj, adj, g_n=nc, ka=LANES, kb=2 * hw, a_cb0=off_u // LANES,
                                 name=f"ssm_db_{tag}")
        d_cmat = mm_blockdiag_tn(sv["st"], dys, g_n=nc, ka=2 * hw, kb=LANES, name=f"ssm_dc_{tag}")
        dproj = rowwise(_f_du_fin, [du0, dys], [s5["dvec"]], [((sw,), BF16, (iwp, off_u // sw), dproj)],
                        name=f"ssm_du_fin_{tag}")[0]
        dlam = jnp.sum(dlam8, axis=1)
        dlam_re, dlam_im = dlam[:, :hw].reshape(g_n, p_n), dlam[:, hw:].reshape(g_n, p_n)
        dbb_re = _blockdiag_take(d_bmat[:, :, :hw], g_n).transpose(0, 2, 1)
        dbb_im = _blockdiag_take(d_bmat[:, :, hw:], g_n).transpose(0, 2, 1)
        mag = jnp.exp(s5["e"])
        lr, li = mag * jnp.cos(s5["ph"]), mag * jnp.sin(s5["ph"])
        d_e = dlam_re * lr + dlam_im * li
        d_ph = -dlam_re * li + dlam_im * lr
        da_re, da_im, dlog_dt, db_re, db_im = s5["vjp"]((d_e, d_ph, dbb_re, dbb_im))
        small_g[l].update(
            ssm_a_re=da_re, ssm_a_im=da_im, ssm_log_dt=dlog_dt, ssm_b_re=db_re, ssm_b_im=db_im,
            ssm_c_re=_blockdiag_take(d_cmat[:, :hw, :], g_n).transpose(0, 2, 1),
            ssm_c_im=-_blockdiag_take(d_cmat[:, hw:, :], g_n).transpose(0, 2, 1),
            ssm_d=d_dvec.reshape(sw))
        dattn = mm(dyb, lw[l]["ao_t"], name=f"attn_out_dx_{tag}", out_dtype=BF16)
        g_ao = mm(sv["attn"], dyb, trans_a=True, name=f"attn_out_dw_{tag}", out_dtype=BF16)
        dq, dk, dv, dcum, *arrived = attn_bwd(sv["q"], sv["k"], sv["v"], heads_first(dattn), sv["o"], sv["lse"],
                                              sv["cum"], scale=scale, tag=tag, comm=comm)
        df, dfb = cum_bwd(dcum, sv["projf"], fb_pad[l:l + 1], col_block=0, name=f"cum_bwd_{tag}")
        small_g[l]["forget_b"] = dfb[0, :heads]
        for piece, off in ((heads_last(dq), off_q), (heads_last(dk), off_q + aw), (heads_last(dv), off_q + 2 * aw),
                           (df, off_f)):
            dproj = lax.dynamic_update_slice(dproj, piece.astype(BF16), (0, off))
        g_mi = mm(sv["h"], dproj, trans_a=True, name=f"mix_in_dw_{tag}", out_dtype=BF16)
        dh_ = mm(dproj, lw[l]["wmi_t"], name=f"mix_in_dx_{tag}")
        g_mi = jnp.concatenate([g_mi[:, off_u:off_f + heads], g_mi[:, :off_u]], axis=1)
        return dh_, g_mi, g_glu, g_ao, g_mo, arrived

    def split_last(a):
        return jnp.moveaxis(a.reshape(a.shape[:-1] + (N_DEV, a.shape[-1] // N_DEV)), -2, 0)

    def split_rows(a):
        return jnp.moveaxis(a.reshape(a.shape[:-2] + (N_DEV, a.shape[-2] // N_DEV, a.shape[-1])), -3, 0)

    def owner_blocks(l, pieces):
        out = []
        for name, j in pieces:
            if name == "ffn_w_in":
                out.append(split_last(g_ffn_in[l][j]))
            elif name == "ffn_w_out":
                out.append(split_rows(g_ffn_out[l][j]))
            elif name == "mix_w_out":
                out.append(split_rows(grads[name][l]))
            else:
                out.append(split_last(grads[name][l]))
        return out

    g_ffn_in = [[None, None] for _ in range(depth)]
    g_ffn_out = [[None, None] for _ in range(depth)]
    parts = {}

    def record(l, pieces, arrived):
        for p, a in zip(pieces, arrived):
            parts[(p, l)] = a

    def last_ffn_blocks(g_in, g_out):
        return exchange_comm([split_last(g_in), split_rows(g_out)])

    for s in reversed(range(len(subs))):
        l, i = subs[s]
        sv, tag = saved[l][i], tag_of(l, i)
        if i == 2:
            dh_, g_ffn_in[l][1], g_ffn_out[l][1], _, _ = ffn_bwd(dy, sv, l, 1, tag)
        elif i == 1:
            pending = exchange_comm([owner_blocks(ll, [p])[0] for ll, p in riders(l)])
            (dh_, grads["mix_w_in"][l], grads["glu_w"][l], grads["attn_w_out"][l], grads["mix_w_out"][l],
             arrived) = mixer_bwd(dy, sv, l, tag, pending)
            for (ll, p), got in zip(riders(l), arrived):
                record(ll, [p], [got])
        elif l == 0:
            dh_, g_ffn_in[l][0], g_ffn_out[l][0], arrived_mix, arrived_ffn1 = ffn_bwd(
                dy, sv, l, 0, tag, exchange_comm(owner_blocks(0, mix)), last_ffn_blocks)
            record(0, mix, arrived_mix)
            record(0, ffn1, arrived_ffn1)
        else:
            dh_, g_ffn_in[l][0], g_ffn_out[l][0], _, _ = ffn_bwd(dy, sv, l, 0, tag)
        if s > 0:
            lp, ip = subs[s - 1]
            dx, dy, d_a, d_sh, d_bv_before = rowwise(
                _f_pre_post_bwd, [dx, dh_, sv["x"], saved[lp][ip]["y"]], [vec_a(l, i), vec_b(lp, ip)],
                [((d,), F32), ((d,), BF16)], [d, d, d], name=f"pre_post_bwd_{tag}")
        else:
            dx, d_a, d_sh = rowwise(_f_pre_bwd, [dx, dh_, sv["x"]], [vec_a(l, i)], [((d,), F32)], [d, d],
                                    name=f"pre_bwd_{tag}")
            d_bv_before = None
        norm_grads(l, i, d_a, d_sh, d_bv)
        d_bv = d_bv_before
    grad_x = dx.reshape(x.shape)

    small_names = ["forget_b", "ssm_a_re", "ssm_a_im", "ssm_log_dt", "ssm_b_re", "ssm_b_im", "ssm_c_re",
                   "ssm_c_im", "ssm_d"]
    pieces = [loss_part.reshape(1), jnp.stack([jnp.stack(dmod[l]) for l in range(depth)]).reshape(-1),
              jnp.stack([jnp.stack(dgpre[l]) for l in range(depth)]).reshape(-1),
              jnp.stack([jnp.stack(dgpost[l]) for l in range(depth)]).reshape(-1)]
    pieces += [jnp.stack([small_g[l][k] for l in range(depth)]).reshape(-1) for k in small_names]
    sizes = [p.size for p in pieces]
    chunk = SUBLANES * 1024
    total = -(-sum(sizes) // chunk) * chunk
    pack = jnp.pad(jnp.concatenate(pieces), (0, total - sum(sizes))).reshape(total // 1024, 1024)
    pack_all = all_gather([pack], name="gather_small_grads")[0]
    pack_sum = rowwise(_f_sum_parts, [(pack_all, p) for p in range(N_DEV)], [], [((1024,), F32)],
                       name="sum_small_grads")[0].reshape(-1)
    offs = [0]
    for s_ in sizes:
        offs.append(offs[-1] + s_)
    take = lambda i: pack_sum[offs[i]:offs[i + 1]]
    loss = take(0).reshape(())
    g_small = {"mod_b": take(1).reshape(mod_b.shape)}
    g_pre_full, g_post_full = take(2).reshape(depth, 3, d), take(3).reshape(depth, 3, d)
    shard = norm_pre.shape[-1]
    g_small["norm_pre"] = lax.dynamic_slice_in_dim(g_pre_full, me * shard, shard, axis=2)
    g_small["norm_post"] = lax.dynamic_slice_in_dim(g_post_full, me * shard, shard, axis=2)
    for i, k in enumerate(small_names):
        g_small[k] = take(4 + i).reshape(w_in[k].shape)

    dmod_all = pack_all.reshape(N_DEV, -1)[:, offs[1]:offs[2]].reshape(N_DEV, depth, 9 * d)
    dmod_mine = lax.dynamic_slice_in_dim(dmod_all, me * mod_cols, mod_cols, axis=2)
    sct_pad = jnp.pad(sc_all.T, ((0, 0), (0, LANES - N_DEV)))
    g_mod_w = jnp.stack([
        mm(sct_pad, jnp.pad(dmod_mine[:, l], ((0, LANES - N_DEV), (0, 0))), name=f"mod_dw{l}")
        for l in range(depth)])

    out_g, out_d, out_m, out_v = {}, {}, {}, {}
    flat = lambda a: a.reshape(-1, a.shape[-1])
    res = adamw(g_mod_w.reshape(1, -1, mod_cols), flat(mod_w), flat(m_mod_w), flat(v_mod_w), name="adamw_mod_w")
    out_g["mod_w"], out_d["mod_w"], out_m["mod_w"], out_v["mod_w"] = [r.reshape(mod_w.shape) for r in res]
    for k in big:
        js = (0, 1) if k.startswith("ffn") else (None,)
        total_rows = w_in[k].size // w_in[k].shape[-1]
        res = None
        for l in range(depth):
            for j in js:
                p = (k, j)
                w_p = piece_of(w_in, p, l)
                row0 = (l * len(js) + (j or 0)) * (w_p.size // w_p.shape[-1])
                res = adamw(parts[(p, l)], flat(w_p), flat(piece_of(m_in, p, l)), flat(piece_of(v_in, p, l)),
                            name=f"adamw_{k}_l{l}" + ("" if j is None else f"_{j}"), into=(res, total_rows, row0))
        for dct, r in zip((out_g, out_d, out_m, out_v), res):
            dct[k] = r.reshape(w_in[k].shape)
    small_all = ["mod_b", "norm_pre", "norm_post"] + small_names

    def pack_small(dct):
        flat = jnp.concatenate([dct[k].reshape(-1) for k in small_all])
        tot = -(-flat.size // chunk) * chunk
        return jnp.pad(flat, (0, tot - flat.size)).reshape(tot // 1024, 1024)

    res = adamw(pack_small(g_small)[None], pack_small(w_in), pack_small(m_in), pack_small(v_in),
                name="adamw_small")
    pos = 0
    for k in small_all:
        size = w_in[k].size
        for dct, r in zip((out_g, out_d, out_m, out_v), res):
            dct[k] = r.reshape(-1)[pos:pos + size].reshape(w_in[k].shape)
        pos += size

    return (loss, grad_x, *[out_g[k] for k in names], *[out_d[k] for k in names],
            *[out_m[k] for k in names], *[out_v[k] for k in names])
```
